```python
import jax, jax.numpy as jnp
from jax import lax
import numpy as np


D_MODEL = 1024
BATCH = 2
SEQ = 8192
DEPTH = 1

HEAD_DIM = 64
NSA_HEADS = 8
NSA_KV = 2
CMP_BLOCK = 32
CMP_STRIDE = 16
CMP_HIDDEN = 128
SEL_BLOCK = 64
SEL_TOPN = 16
NSA_WINDOW = 512
SWA_HEADS = 8
SWA_KV = 2
SWA_WINDOW = 128
Q_BLOCK = 128
ROPE_THETA = 10000.0
NEG = -1e30
N_EXPERTS = 32
TOP_K = 4
D_FF = 1024
SWIGLU_LIMIT = 7.0
SWIGLU_ALPHA = 1.702
MOE_BLOCK = 128
DN_ALPHA = (2.0 * DEPTH) ** 0.25
DN_BETA = (8.0 * DEPTH) ** -0.25
LN_EPS = 1e-5

NSA_Q = NSA_HEADS * HEAD_DIM
NSA_KVW = NSA_KV * HEAD_DIM
SWA_Q = SWA_HEADS * HEAD_DIM
SWA_KVW = SWA_KV * HEAD_DIM
IN_WIDTHS = (
    NSA_Q,
    NSA_KVW, NSA_KVW,
    NSA_KVW, NSA_KVW,
    NSA_KVW, NSA_KVW,
    NSA_HEADS * 3,
    SWA_Q, SWA_KVW, SWA_KVW,
    2 * D_MODEL,
)
IN_COLS = sum(IN_WIDTHS)
IN_SPLITS = tuple(sum(IN_WIDTHS[:i + 1]) for i in range(len(IN_WIDTHS) - 1))

kernel_name = 'hybrid_nsa_swasink_moe_deepnorm'


def layer_norm(x, g, b):
    xf = x.astype(jnp.float32)
    mu = jnp.mean(xf, -1, keepdims=True)
    var = jnp.mean(jnp.square(xf - mu), -1, keepdims=True)
    return ((xf - mu) * lax.rsqrt(var + LN_EPS) * g + b).astype(x.dtype)


def rope_tables(s):
    half = HEAD_DIM // 2
    inv = ROPE_THETA ** (-jnp.arange(half, dtype=jnp.float32) / half)
    ang = jnp.arange(s, dtype=jnp.float32)[:, None] * inv[None, :]
    return jnp.cos(ang), jnp.sin(ang)


def rope(t, cos, sin):
    half = HEAD_DIM // 2
    tf = t.astype(jnp.float32)
    t1, t2 = tf[..., :half], tf[..., half:]
    return jnp.concatenate([t1 * cos - t2 * sin, t2 * cos + t1 * sin], -1).astype(t.dtype)


def to_heads(t, n):
    b, s, _ = t.shape
    return t.reshape(b, s, n, HEAD_DIM).transpose(0, 2, 1, 3)


def from_heads(t):
    b, h, s, d = t.shape
    return t.transpose(0, 2, 1, 3).reshape(b, s, h * d)


def banded_attention(q, k, v, window, sinks=None):
    b, h, s, dh = q.shape
    g = k.shape[1]
    r = h // g
    nb = s // Q_BLOCK
    back = -(-window // Q_BLOCK)
    wlen = (back + 1) * Q_BLOCK
    qb = q.reshape(b, g, r, nb, Q_BLOCK, dh)

    def windows(t):
        tp = jnp.pad(t, ((0, 0), (0, 0), (back * Q_BLOCK, 0), (0, 0))).reshape(b, g, nb + back, Q_BLOCK, dh)
        return jnp.concatenate([tp[:, :, j:j + nb] for j in range(back + 1)], axis=3)

    kw, vw = windows(k), windows(v)
    qpos = jnp.arange(s).reshape(nb, Q_BLOCK)
    kpos = (jnp.arange(nb)[:, None] - back) * Q_BLOCK + jnp.arange(wlen)[None, :]
    rel = qpos[:, :, None] - kpos[:, None, :]
    mask = (rel >= 0) & (rel < window) & (kpos[:, None, :] >= 0)
    sc = jnp.einsum('bgrnqd,bgnkd->bgrnqk', qb, kw, preferred_element_type=jnp.float32) * (dh ** -0.5)
    sc = jnp.where(mask, sc, NEG)
    if sinks is None:
        p = jax.nn.softmax(sc, axis=-1)
    else:
        sk = sinks.astype(jnp.float32).reshape(1, g, r, 1, 1, 1)
        m = jnp.maximum(jnp.max(sc, -1, keepdims=True), sk)
        e = jnp.exp(sc - m)
        p = e / (jnp.sum(e, -1, keepdims=True) + jnp.exp(sk - m))
    o = jnp.einsum('bgrnqk,bgnkd->bgrnqd', p.astype(v.dtype), vw)
    return o.reshape(b, h, s, dh)


def nsa_compress(t, pe, w1, w2):
    s = t.shape[2]
    nc = (s - CMP_BLOCK) // CMP_STRIDE + 1
    idx = jnp.arange(nc)[:, None] * CMP_STRIDE + jnp.arange(CMP_BLOCK)[None, :]
    blk = t[:, :, idx] + pe
    flat = blk.reshape(blk.shape[0], blk.shape[1], nc, CMP_BLOCK * HEAD_DIM)
    return jax.nn.gelu(flat @ w1) @ w2


def nsa_attention(q, kc_raw, vc_raw, k_sel, v_sel, k_win, v_win, gate, cos, sin,
                  k_pe, k_w1, k_w2, v_pe, v_w1, v_w2):
    b, h, s, dh = q.shape
    g = NSA_KV
    r = h // g
    scale = dh ** -0.5
    pos = jnp.arange(s)
    kc = nsa_compress(kc_raw, k_pe, k_w1, k_w2)
    vc = nsa_compress(vc_raw, v_pe, v_w1, v_w2)
    nc = kc.shape[2]
    qg = q.reshape(b, g, r, s, dh)
    sc = jnp.einsum('bgrsd,bgcd->bgrsc', qg, kc, preferred_element_type=jnp.float32) * scale
    cend = jnp.arange(nc) * CMP_STRIDE + CMP_BLOCK - 1
    cmask = cend[None, :] <= pos[:, None]
    p_cmp = jax.nn.softmax(jnp.where(cmask, sc, NEG), axis=-1) * (pos >= CMP_BLOCK - 1)[:, None]
    o_cmp = jnp.einsum('bgrsc,bgcd->bgrsd', p_cmp.astype(vc.dtype), vc).reshape(b, h, s, dh)
    nsel = s // SEL_BLOCK
    cstart = jnp.arange(nc) * CMP_STRIDE
    sstart = jnp.arange(nsel) * SEL_BLOCK
    overlap = ((cstart[:, None] < sstart[None, :] + SEL_BLOCK) &
               (cstart[:, None] + CMP_BLOCK > sstart[None, :])).astype(jnp.float32)
    imp = jnp.einsum('bgrsc,cj->bgsj', p_cmp, overlap)
    cur = pos // SEL_BLOCK
    jb = jnp.arange(nsel)
    forced = (jb[None, :] == 0) | (jb[None, :] == cur[:, None]) | (jb[None, :] == cur[:, None] - 1)
    imp = jnp.where(jb[None, :] > cur[:, None], -1.0, jnp.where(forced, 1e6, imp))
    n_top = min(SEL_TOPN, nsel)
    _, sel = lax.top_k(imp, n_top)
    q_rot = rope(q, cos, sin)
    kb = rope(k_sel, cos, sin).reshape(b, g, nsel, SEL_BLOCK, dh)
    vb = v_sel.reshape(b, g, nsel, SEL_BLOCK, dh)
    nq = s // Q_BLOCK
    qr = q_rot.reshape(b, g, r, nq, Q_BLOCK, dh).transpose(3, 0, 1, 2, 4, 5)
    selr = sel.reshape(b, g, nq, Q_BLOCK, n_top).transpose(2, 0, 1, 3, 4)
    posr = pos.reshape(nq, Q_BLOCK)
    bi = jnp.arange(b)[:, None, None, None]
    gi = jnp.arange(g)[None, :, None, None]
    offs = jnp.arange(SEL_BLOCK)

    def sel_chunk(args):
        qc, sc_idx, pc = args
        ks = kb[bi, gi, sc_idx]
        vs = vb[bi, gi, sc_idx]
        kp = sc_idx[..., None] * SEL_BLOCK + offs
        msk = kp <= pc[None, None, :, None, None]
        ss = jnp.einsum('bgrqd,bgqnkd->bgrqnk', qc, ks, preferred_element_type=jnp.float32) * scale
        ss = jnp.where(msk[:, :, None], ss, NEG)
        p = jax.nn.softmax(ss, axis=(-2, -1))
        return jnp.einsum('bgrqnk,bgqnkd->bgrqd', p.astype(vs.dtype), vs)

    o_sel = lax.map(sel_chunk, (qr, selr, posr))
    o_sel = o_sel.transpose(1, 2, 3, 0, 4, 5).reshape(b, h, s, dh)
    o_win = banded_attention(q_rot, rope(k_win, cos, sin), v_win, NSA_WINDOW)
    gt = jax.nn.sigmoid(gate.astype(jnp.float32)).reshape(b, s, h, 3).transpose(0, 2, 1, 3)[..., None]
    o = gt[:, :, :, 0] * o_cmp + gt[:, :, :, 1] * o_sel + gt[:, :, :, 2] * o_win
    return o.astype(q.dtype)


def moe_ffn(x, w_router, b_router, w_e_in, b_e_in, w_e_out, b_e_out):
    b, s, d = x.shape
    t = b * s
    a = t * TOP_K
    xf = x.reshape(t, d)
    logits = (xf @ w_router).astype(jnp.float32) + b_router
    top_v, top_e = lax.top_k(logits, TOP_K)
    gates = jax.nn.softmax(top_v, axis=-1)
    flat_e = top_e.reshape(a)
    flat_t = jnp.arange(a, dtype=jnp.int32) // TOP_K
    flat_g = gates.reshape(a)
    order = jnp.argsort(flat_e)
    se = flat_e[order]
    counts = jnp.bincount(flat_e, length=N_EXPERTS)
    padded = (counts + MOE_BLOCK - 1) // MOE_BLOCK * MOE_BLOCK
    start = jnp.cumsum(counts) - counts
    pend = jnp.cumsum(padded)
    pstart = pend - padded
    dest = pstart[se] + jnp.arange(a) - start[se]
    n_blk = -(-a // MOE_BLOCK) + N_EXPERTS
    n_rows = n_blk * MOE_BLOCK
    row_t = jnp.zeros((n_rows,), jnp.int32).at[dest].set(flat_t[order])
    row_g = jnp.zeros((n_rows,), jnp.float32).at[dest].set(flat_g[order])
    blk_e = jnp.minimum(jnp.searchsorted(pend, jnp.arange(n_blk) * MOE_BLOCK, side='right'), N_EXPERTS - 1)
    xs = xf[row_t].reshape(n_blk, MOE_BLOCK, d)

    def expert_block(args):
        xb, e = args
        hdn = xb @ w_e_in[e] + b_e_in[e]
        hg = jnp.minimum(hdn[:, :D_FF], SWIGLU_LIMIT)
        hu = jnp.clip(hdn[:, D_FF:], -SWIGLU_LIMIT, SWIGLU_LIMIT)
        act = hg * jax.nn.sigmoid(SWIGLU_ALPHA * hg) * (hu + 1.0)
        return act @ w_e_out[e] + b_e_out[e]

    ys = lax.map(expert_block, (xs, blk_e)).reshape(n_rows, d)
    out = jax.ops.segment_sum(ys.astype(jnp.float32) * row_g[:, None], row_t, num_segments=t)
    return out.reshape(b, s, d).astype(x.dtype)


def hybrid_layer(x, cos, sin, w_in, nsa_k_pe, nsa_k_w1, nsa_k_w2, nsa_v_pe, nsa_v_w1, nsa_v_w2,
                 swa_sinks, w_br_nsa, w_br_swa, w_out, ln1_g, ln1_b, w_router, b_router,
                 w_expert_in, b_expert_in, w_expert_out, b_expert_out, ln2_g, ln2_b):
    d = x.shape[-1]
    proj = x @ w_in
    (q_n, kc_n, vc_n, ks_n, vs_n, kw_n, vw_n, g_n, q_s, k_s, v_s, g_m) = jnp.split(proj, IN_SPLITS, axis=-1)
    o_nsa = nsa_attention(to_heads(q_n, NSA_HEADS), to_heads(kc_n, NSA_KV), to_heads(vc_n, NSA_KV),
                          to_heads(ks_n, NSA_KV), to_heads(vs_n, NSA_KV), to_heads(kw_n, NSA_KV),
                          to_heads(vw_n, NSA_KV), g_n, cos, sin,
                          nsa_k_pe, nsa_k_w1, nsa_k_w2, nsa_v_pe, nsa_v_w1, nsa_v_w2)
    o_swa = banded_attention(rope(to_heads(q_s, SWA_HEADS), cos, sin), rope(to_heads(k_s, SWA_KV), cos, sin),
                             to_heads(v_s, SWA_KV), SWA_WINDOW, swa_sinks)
    y_nsa = from_heads(o_nsa) @ w_br_nsa
    y_swa = from_heads(o_swa) @ w_br_swa
    gm = jax.nn.sigmoid(g_m.astype(jnp.float32))
    mixed = (gm[..., :d] * y_nsa + gm[..., d:] * y_swa).astype(x.dtype)
    h = layer_norm(DN_ALPHA * x + mixed @ w_out, ln1_g, ln1_b)
    f = moe_ffn(h, w_router, b_router, w_expert_in, b_expert_in, w_expert_out, b_expert_out)
    return layer_norm(DN_ALPHA * h + f, ln2_g, ln2_b)


def setup_inputs(seed: int = 0) -> dict:
    key = jax.random.key(seed)
    ks = jax.random.split(key, 24)
    L, D, E, F = DEPTH, D_MODEL, N_EXPERTS, D_FF
    cw = CMP_BLOCK * HEAD_DIM

    def nrm(k, shape, scale):
        return jax.random.normal(k, shape, jnp.float32) * scale

    return {
        'x': nrm(ks[0], (BATCH, SEQ, D), 1.0),
        'w_in': nrm(ks[1], (L, D, IN_COLS), D ** -0.5),
        'nsa_k_pe': nrm(ks[2], (L, CMP_BLOCK, HEAD_DIM), 0.1),
        'nsa_k_w1': nrm(ks[3], (L, cw, CMP_HIDDEN), cw ** -0.5),
        'nsa_k_w2': nrm(ks[4], (L, CMP_HIDDEN, HEAD_DIM), CMP_HIDDEN ** -0.5),
        'nsa_v_pe': nrm(ks[5], (L, CMP_BLOCK, HEAD_DIM), 0.1),
        'nsa_v_w1': nrm(ks[6], (L, cw, CMP_HIDDEN), cw ** -0.5),
        'nsa_v_w2': nrm(ks[7], (L, CMP_HIDDEN, HEAD_DIM), CMP_HIDDEN ** -0.5),
        'swa_sinks': nrm(ks[8], (L, SWA_HEADS), 0.5),
        'w_br_nsa': nrm(ks[9], (L, NSA_Q, D), NSA_Q ** -0.5),
        'w_br_swa': nrm(ks[10], (L, SWA_Q, D), SWA_Q ** -0.5),
        'w_out': nrm(ks[11], (L, D, D), DN_BETA * D ** -0.5),
        'ln1_g': 1.0 + nrm(ks[12], (L, D), 0.01),
        'ln1_b': nrm(ks[13], (L, D), 0.01),
        'w_router': nrm(ks[14], (L, D, E), D ** -0.5),
        'b_router': nrm(ks[15], (L, E), 0.01),
        'w_expert_in': nrm(ks[16], (L, E, D, 2 * F), D ** -0.5),
        'b_expert_in': nrm(ks[17], (L, E, 2 * F), 0.01),
        'w_expert_out': nrm(ks[18], (L, E, F, D), DN_BETA * F ** -0.5),
        'b_expert_out': nrm(ks[19], (L, E, D), 0.01),
        'ln2_g': 1.0 + nrm(ks[20], (L, D), 0.01),
        'ln2_b': nrm(ks[21], (L, D), 0.01),
    }


def reference(x, w_in, nsa_k_pe, nsa_k_w1, nsa_k_w2, nsa_v_pe, nsa_v_w1, nsa_v_w2, swa_sinks,
              w_br_nsa, w_br_swa, w_out, ln1_g, ln1_b, w_router, b_router, w_expert_in, b_expert_in,
              w_expert_out, b_expert_out, ln2_g, ln2_b):
    cos, sin = rope_tables(x.shape[1])
    for l in range(DEPTH):
        x = hybrid_layer(x, cos, sin, w_in[l], nsa_k_pe[l], nsa_k_w1[l], nsa_k_w2[l], nsa_v_pe[l],
                         nsa_v_w1[l], nsa_v_w2[l], swa_sinks[l], w_br_nsa[l], w_br_swa[l], w_out[l],
                         ln1_g[l], ln1_b[l], w_router[l], b_router[l], w_expert_in[l], b_expert_in[l],
                         w_expert_out[l], b_expert_out[l], ln2_g[l], ln2_b[l])
    return x
```

```python
import functools

import jax
import jax.numpy as jnp
from jax import lax
from jax.experimental import pallas as pl
from jax.experimental.pallas import tpu as pltpu

BF16 = jnp.bfloat16
F32 = jnp.float32
I32 = jnp.int32

HEAD_DIM = 64
NSA_HEADS = 8
NSA_KV = 2
CMP_BLOCK = 32
CMP_STRIDE = 16
SEL_BLOCK = 64
SEL_TOPN = 16
NSA_WINDOW = 512
SWA_HEADS = 8
SWA_KV = 2
SWA_WINDOW = 128
Q_BLOCK = 128
ROPE_THETA = 10000.0
N_EXPERTS = 32
TOP_K = 4
SWIGLU_LIMIT = 7.0
SWIGLU_ALPHA = 1.702
LN_EPS = 1e-5

LANES = 128
MASKED = -1e30
M_INIT = -1e29
SEL_PENALTY = -(2.0 ** 100)
VMEM_LIMIT = 52 * 1024 * 1024
MOE_ROWS = 256

R_NSA = NSA_HEADS // NSA_KV
R_SWA = SWA_HEADS // SWA_KV
NT_DIMS = (((1,), (1,)), ((), ()))


def _params(*sem):
    return pltpu.CompilerParams(dimension_semantics=sem, vmem_limit_bytes=VMEM_LIMIT)


def _full(shape):
    n = len(shape)
    return pl.BlockSpec(shape, lambda *_: (0,) * n)


def _proj_kernel(x_ref, wr_ref, wp_ref, wg_ref, cos_ref, sin_ref,
                 qn_rot_ref, qs_rot_ref, kk_rot_ref, qn_raw_ref, plain_ref, gates_ref):
    xb = x_ref[...].astype(BF16)
    acc = jnp.dot(xb, wr_ref[...], preferred_element_type=F32)
    cos = cos_ref[...]
    sin = sin_ref[...]
    lane = lax.broadcasted_iota(I32, cos.shape, 1)
    first_half = (lane & (HEAD_DIM - 1)) < HEAD_DIM // 2

    def rope(t):
        partner = jnp.where(first_half, pltpu.roll(t, LANES - HEAD_DIM // 2, 1),
                            pltpu.roll(t, HEAD_DIM // 2, 1))
        return (t * cos + partner * sin).astype(BF16)

    nq = qn_rot_ref.shape[1] // LANES
    ns = qs_rot_ref.shape[1] // LANES
    nk = kk_rot_ref.shape[1] // LANES
    for c in range(nq):
        qn_rot_ref[:, c * LANES:(c + 1) * LANES] = rope(acc[:, c * LANES:(c + 1) * LANES])
    for c in range(ns):
        o = (nq + c) * LANES
        qs_rot_ref[:, c * LANES:(c + 1) * LANES] = rope(acc[:, o:o + LANES])
    for c in range(nk):
        o = (nq + ns + c) * LANES
        kk_rot_ref[:, c * LANES:(c + 1) * LANES] = rope(acc[:, o:o + LANES])
    qn_raw_ref[...] = acc[:, :nq * LANES].astype(BF16)
    plain_ref[...] = jnp.dot(xb, wp_ref[...], preferred_element_type=F32).astype(BF16)
    gates_ref[...] = jnp.dot(xb, wg_ref[...], preferred_element_type=F32)


def _project(x2, w_rope, w_plain, w_gate, cos_t, sin_t, seq, tm):
    t, d = x2.shape
    nr, npl, ng = w_rope.shape[1], w_plain.shape[1], w_gate.shape[1]
    nqn, nqs = NSA_HEADS * HEAD_DIM, SWA_HEADS * HEAD_DIM
    nkk = nr - nqn - nqs
    spb = seq // tm
    row = lambda i: (i, 0)
    return pl.pallas_call(
        _proj_kernel,
        grid=(t // tm,),
        in_specs=[pl.BlockSpec((tm, d), row), _full(w_rope.shape), _full(w_plain.shape),
                  _full(w_gate.shape),
                  pl.BlockSpec((tm, LANES), lambda i: (i % spb, 0)),
                  pl.BlockSpec((tm, LANES), lambda i: (i % spb, 0))],
        out_specs=[pl.BlockSpec((tm, nqn), row), pl.BlockSpec((tm, nqs), row),
                   pl.BlockSpec((tm, nkk), row), pl.BlockSpec((tm, nqn), row),
                   pl.BlockSpec((tm, npl), row), pl.BlockSpec((tm, ng), row)],
        out_shape=[jax.ShapeDtypeStruct((t, nqn), BF16), jax.ShapeDtypeStruct((t, nqs), BF16),
                   jax.ShapeDtypeStruct((t, nkk), BF16), jax.ShapeDtypeStruct((t, nqn), BF16),
                   jax.ShapeDtypeStruct((t, npl), BF16), jax.ShapeDtypeStruct((t, ng), F32)],
        compiler_params=_params("parallel"),
        name="proj",
    )(x2, w_rope, w_plain, w_gate, cos_t, sin_t)


def _compress_kernel(a_ref, b_ref, pe_ref, w1_ref, w2_ref, out_ref):
    half = a_ref.shape[1]
    a = (a_ref[...].astype(F32) + pe_ref[0:1, :]).astype(BF16)
    b = (b_ref[...].astype(F32) + pe_ref[1:2, :]).astype(BF16)
    hid = jnp.dot(a, w1_ref[0:half, :], preferred_element_type=F32)
    hid = hid + jnp.dot(b, w1_ref[half:2 * half, :], preferred_element_type=F32)
    act = jax.nn.gelu(hid).astype(BF16)
    out_ref[...] = jnp.dot(act, w2_ref[...], preferred_element_type=F32).astype(BF16)


def _compress(t_lo, t_hi, pe2, w1, w2):
    two, bg, ncp, half = t_lo.shape
    hid = w1.shape[2]
    blk = lambda shape: pl.BlockSpec((None, None) + shape, lambda j, i: (j, i, 0, 0))
    wsp = lambda shape: pl.BlockSpec((None,) + shape, lambda j, i: (j, 0, 0))
    return pl.pallas_call(
        _compress_kernel,
        grid=(two, bg),
        in_specs=[blk((ncp, half)), blk((ncp, half)), wsp((2, half)), wsp((2 * half, hid)),
                  wsp((hid, HEAD_DIM))],
        out_specs=blk((ncp, HEAD_DIM)),
        out_shape=jax.ShapeDtypeStruct((two, bg, ncp, HEAD_DIM), BF16),
        compiler_params=_params("parallel", "parallel"),
        name="compress",
    )(t_lo, t_hi, pe2, w1, w2)


def _stack_heads(q, g, r):
    return jnp.concatenate(
        [q[:, (g * r + j) * HEAD_DIM:(g * r + j + 1) * HEAD_DIM] for j in range(r)], axis=0)


def _unstack_heads(parts, r, qb):
    return jnp.concatenate([o[j * qb:(j + 1) * qb] for o in parts for j in range(r)], axis=1)


def _topk_mask(vals, k):
    n = vals.shape[1]
    col = lax.broadcasted_iota(I32, vals.shape, 1).astype(F32)
    taken = jnp.zeros(vals.shape, F32)
    work = vals
    for _ in range(k):
        mx = jnp.max(work, axis=1, keepdims=True)
        first = jnp.min(jnp.where(work == mx, col, float(n)), axis=1, keepdims=True)
        pick = col == first
        taken = jnp.where(pick, 1.0, taken)
        work = jnp.where(pick, -jnp.inf, work)
    return taken > 0.5


def _cmp_kernel(q_ref, kc_ref, vc_ref, ov_ref, o_ref, notsel_ref):
    qb = q_ref.shape[0]
    ncp = kc_ref.shape[1]
    nselp = ov_ref.shape[1]
    i = pl.program_id(1)
    rows = R_NSA * qb
    pos = i * qb + (lax.broadcasted_iota(I32, (rows, 1), 0) & (qb - 1))
    cend = lax.broadcasted_iota(I32, (1, ncp), 1) * CMP_STRIDE + (CMP_BLOCK - 1)
    visible = cend <= pos
    live = (pos >= CMP_BLOCK - 1).astype(F32)
    q = q_ref[...]
    outs = []
    for g in range(NSA_KV):
        qg = _stack_heads(q, g, R_NSA) * (HEAD_DIM ** -0.5)
        s = lax.dot_general(qg, kc_ref[g], NT_DIMS, preferred_element_type=F32)
        s = jnp.where(visible, s, MASKED)
        e = jnp.exp(s - jnp.max(s, axis=1, keepdims=True))
        p = e / jnp.sum(e, axis=1, keepdims=True) * live
        outs.append(jnp.dot(p.astype(BF16), vc_ref[g], preferred_element_type=F32))
        psum = p[0:qb]
        for j in range(1, R_NSA):
            psum = psum + p[j * qb:(j + 1) * qb]
        p_hi = psum.astype(BF16)
        p_lo = (psum - p_hi.astype(F32)).astype(BF16)
        imp = (jnp.dot(p_hi, ov_ref[...], preferred_element_type=F32)
               + jnp.dot(p_lo, ov_ref[...], preferred_element_type=F32))
        qpos = i * qb + lax.broadcasted_iota(I32, (qb, 1), 0)
        cur = qpos >> 6
        jb = lax.broadcasted_iota(I32, (1, nselp), 1)
        forced = (jb == 0) | (jb == cur) | (jb == cur - 1)
        imp = jnp.where(jb > cur, -1.0, jnp.where(forced, 1e6, imp))
        sel = _topk_mask(imp, SEL_TOPN)
        notsel_ref[g] = jnp.where(sel, 0.0, 1.0).astype(BF16)
    o_ref[...] = _unstack_heads(outs, R_NSA, qb)


def _cmp_attention(q_raw, kc, vc, overlap, qb):
    b, s, hq = q_raw.shape
    _, g, ncp, dh = kc.shape
    nselp = overlap.shape[1]
    return pl.pallas_call(
        _cmp_kernel,
        grid=(b, s // qb),
        in_specs=[pl.BlockSpec((None, qb, hq), lambda bi, i: (bi, i, 0)),
                  pl.BlockSpec((None, g, ncp, dh), lambda bi, i: (bi, 0, 0, 0)),
                  pl.BlockSpec((None, g, ncp, dh), lambda bi, i: (bi, 0, 0, 0)),
                  _full(overlap.shape)],
        out_specs=[pl.BlockSpec((None, qb, hq), lambda bi, i: (bi, i, 0)),
                   pl.BlockSpec((None, g, qb, nselp), lambda bi, i: (bi, 0, i, 0))],
        out_shape=[jax.ShapeDtypeStruct((b, s, hq), F32),
                   jax.ShapeDtypeStruct((b, g, s, nselp), BF16)],
        compiler_params=_params("parallel", "parallel"),
        name="cmp_attn",
    )(q_raw, kc, vc, overlap)


def _sel_kernel(q_ref, notsel_ref, k_ref, v_ref, o_ref, *, tk):
    qb = q_ref.shape[0]
    i = pl.program_id(1)
    rows = R_NSA * qb
    qpos = i * qb + (lax.broadcasted_iota(I32, (rows, 1), 0) & (qb - 1))
    n_clear = (i * qb) // tk
    q = q_ref[...]
    outs = []
    for g in range(NSA_KV):
        qg = _stack_heads(q, g, R_NSA) * (HEAD_DIM ** -0.5)
        ns = notsel_ref[g]
        q_aug = jnp.concatenate(
            [qg, jnp.zeros((rows, LANES - HEAD_DIM), BF16),
             jnp.concatenate([ns] * R_NSA, axis=0)], axis=1)

        def step(kt, carry, causal, g=g, q_aug=q_aug):
            m, acc = carry
            start = pl.multiple_of(kt * tk, tk)
            k_t = k_ref[g, pl.ds(start, tk), :]
            v_t = v_ref[g, pl.ds(start, tk), :]
            s = lax.dot_general(q_aug, k_t, NT_DIMS, preferred_element_type=F32)
            if causal:
                kpos = start + lax.broadcasted_iota(I32, (1, tk), 1)
                s = jnp.where(kpos <= qpos, s, MASKED)
            m_new = jnp.maximum(m, jnp.max(s, axis=1, keepdims=True))
            p = jnp.exp(s - m_new)
            acc = jnp.exp(m - m_new) * acc + jnp.dot(p.astype(BF16), v_t,
                                                    preferred_element_type=F32)
            return m_new, acc

        init = (jnp.full((rows, 1), M_INIT, F32), jnp.zeros((rows, LANES), F32))
        carry = lax.fori_loop(0, n_clear, functools.partial(step, causal=False), init)
        _, acc = step(n_clear, carry, True)
        outs.append(acc[:, :HEAD_DIM] / acc[:, HEAD_DIM:HEAD_DIM + 1])
    o_ref[...] = _unstack_heads(outs, R_NSA, qb)


def _sel_attention(q_rot, notsel, k_aug, v_aug, qb, tk):
    b, s, hq = q_rot.shape
    _, g, _, kw = k_aug.shape
    nselp = notsel.shape[3]
    return pl.pallas_call(
        functools.partial(_sel_kernel, tk=tk),
        grid=(b, s // qb),
        in_specs=[pl.BlockSpec((None, qb, hq), lambda bi, i: (bi, i, 0)),
                  pl.BlockSpec((None, g, qb, nselp), lambda bi, i: (bi, 0, i, 0)),
                  pl.BlockSpec((None, g, s, kw), lambda bi, i: (bi, 0, 0, 0)),
                  pl.BlockSpec((None, g, s, LANES), lambda bi, i: (bi, 0, 0, 0))],
        out_specs=pl.BlockSpec((None, qb, hq), lambda bi, i: (bi, i, 0)),
        out_shape=jax.ShapeDtypeStruct((b, s, hq), F32),
        compiler_params=_params("parallel", "parallel"),
        name="sel_attn",
    )(q_rot, notsel, k_aug, v_aug)


def _band_kernel(*refs, window, wlen, r, kv, has_sinks):
    if has_sinks:
        sink_ref, q_ref, k_ref, v_ref, o_ref = refs
    else:
        q_ref, k_ref, v_ref, o_ref = refs
    qb = q_ref.shape[0]
    i = pl.program_id(1)
    rows = r * qb
    start = pl.multiple_of(jnp.maximum((i + 1) * qb - wlen, 0), qb)
    qpos = i * qb + (lax.broadcasted_iota(I32, (rows, 1), 0) & (qb - 1))
    kpos = start + lax.broadcasted_iota(I32, (1, wlen), 1)
    rel = qpos - kpos
    inside = (rel >= 0) & (rel < window)
    q = q_ref[...]
    kw = k_ref[pl.ds(start, wlen), :]
    vw = v_ref[pl.ds(start, wlen), :]
    outs = []
    for g in range(kv):
        qg = _stack_heads(q, g, r) * (HEAD_DIM ** -0.5)
        s = lax.dot_general(qg, kw[:, g * HEAD_DIM:(g + 1) * HEAD_DIM], NT_DIMS,
                            preferred_element_type=F32)
        s = jnp.where(inside, s, MASKED)
        m = jnp.max(s, axis=1, keepdims=True)
        if has_sinks:
            sk = jnp.concatenate(
                [jnp.full((qb, 1), sink_ref[g * r + j], F32) for j in range(r)], axis=0)
            m = jnp.maximum(m, sk)
        e = jnp.exp(s - m)
        den = jnp.sum(e, axis=1, keepdims=True)
        if has_sinks:
            den = den + jnp.exp(sk - m)
        p = (e / den).astype(BF16)
        outs.append(jnp.dot(p, vw[:, g * HEAD_DIM:(g + 1) * HEAD_DIM],
                            preferred_element_type=F32))
    o_ref[...] = _unstack_heads(outs, r, qb)


def _band_attention(q_rot, k, v, window, sinks, qb):
    b, s, hq = q_rot.shape
    gk = k.shape[2]
    kv = gk // HEAD_DIM
    r = hq // gk
    back = -(-window // qb)
    wlen = (back + 1) * qb
    assert wlen <= s
    has_sinks = sinks is not None
    in_specs = [pl.BlockSpec((None, qb, hq), lambda bi, i: (bi, i, 0)),
                pl.BlockSpec((None, s, gk), lambda bi, i: (bi, 0, 0)),
                pl.BlockSpec((None, s, gk), lambda bi, i: (bi, 0, 0))]
    args = [q_rot, k, v]
    if has_sinks:
        in_specs = [pl.BlockSpec(memory_space=pltpu.SMEM)] + in_specs
        args = [sinks.astype(F32)] + args
    return pl.pallas_call(
        functools.partial(_band_kernel, window=window, wlen=wlen, r=r, kv=kv,
                          has_sinks=has_sinks),
        grid=(b, s // qb),
        in_specs=in_specs,
        out_specs=pl.BlockSpec((None, qb, hq), lambda bi, i: (bi, i, 0)),
        out_shape=jax.ShapeDtypeStruct((b, s, hq), F32),
        compiler_params=_params("parallel", "parallel"),
        name="band_attn_sink" if has_sinks else "band_attn",
    )(*args)


def _layer_norm(v, g, b):
    mu = jnp.mean(v, axis=1, keepdims=True)
    c = v - mu
    var = jnp.mean(c * c, axis=1, keepdims=True)
    return c * lax.rsqrt(var + LN_EPS) * g + b


def _split_dot(a, w):
    hi = a.astype(BF16)
    lo = (a - hi.astype(F32)).astype(BF16)
    return (jnp.dot(hi, w, preferred_element_type=F32)
            + jnp.dot(lo, w, preferred_element_type=F32))


def _merge_kernel(ocmp_ref, osel_ref, owin_ref, oswa_ref, gates_ref, x_ref, exp_ref,
                  wbn_ref, wbs_ref, wo_ref, lng_ref, lnb_ref, wrh_ref, wrl_ref, br_ref,
                  h_ref, hb_ref, te_ref, tg_ref, *, alpha):
    d = x_ref.shape[1]
    hq = ocmp_ref.shape[1]
    gates = gates_ref[...]
    gn = jax.nn.sigmoid(gates[:, 2 * d:])
    gexp = _split_dot(gn, exp_ref[...])
    o_nsa = (gexp[:, 0:hq] * ocmp_ref[...] + gexp[:, hq:2 * hq] * osel_ref[...]
             + gexp[:, 2 * hq:3 * hq] * owin_ref[...])
    y_nsa = jnp.dot(o_nsa.astype(BF16), wbn_ref[...], preferred_element_type=F32)
    y_swa = jnp.dot(oswa_ref[...].astype(BF16), wbs_ref[...], preferred_element_type=F32)
    gm = jax.nn.sigmoid(gates[:, :2 * d])
    mixed = gm[:, :d] * y_nsa + gm[:, d:] * y_swa
    z = jnp.dot(mixed.astype(BF16), wo_ref[...], preferred_element_type=F32)
    h = _layer_norm(alpha * x_ref[...] + z, lng_ref[...], lnb_ref[...])
    h_ref[...] = h
    h_hi = h.astype(BF16)
    hb_ref[...] = h_hi
    h_lo = (h - h_hi.astype(F32)).astype(BF16)
    logits = (jnp.dot(h_hi, wrh_ref[...], preferred_element_type=F32)
              + jnp.dot(h_lo, wrh_ref[...], preferred_element_type=F32)
              + jnp.dot(h_hi, wrl_ref[...], preferred_element_type=F32)) + br_ref[...]
    col = lax.broadcasted_iota(I32, logits.shape, 1).astype(F32)
    work = logits
    vals, ids = [], []
    for _ in range(TOP_K):
        mx = jnp.max(work, axis=1, keepdims=True)
        first = jnp.min(jnp.where(work == mx, col, float(LANES)), axis=1, keepdims=True)
        vals.append(mx)
        ids.append(first)
        work = jnp.where(col == first, -jnp.inf, work)
    es = [jnp.exp(v - vals[0]) for v in vals]
    den = es[0]
    for e in es[1:]:
        den = den + e
    te = jnp.zeros(logits.shape, F32)
    tg = jnp.zeros(logits.shape, F32)
    for k in range(TOP_K):
        te = jnp.where(col == float(k), ids[k], te)
        tg = jnp.where(col == float(k), es[k] / den, tg)
    te_ref[...] = te.astype(I32)
    tg_ref[...] = tg


def _merge(o_cmp, o_sel, o_win, o_swa, gates, x2, expand, w_bn, w_bs, w_o, ln_g, ln_b,
           wr_hi, wr_lo, b_r, alpha, tm):
    t, d = x2.shape
    hq = o_cmp.shape[1]
    row = lambda i: (i, 0)
    tok = lambda n: pl.BlockSpec((tm, n), row)
    return pl.pallas_call(
        functools.partial(_merge_kernel, alpha=alpha),
        grid=(t // tm,),
        in_specs=[tok(hq), tok(hq), tok(hq), tok(hq), tok(gates.shape[1]), tok(d),
                  _full(expand.shape), _full(w_bn.shape), _full(w_bs.shape), _full(w_o.shape),
                  _full(ln_g.shape), _full(ln_b.shape), _full(wr_hi.shape), _full(wr_lo.shape),
                  _full(b_r.shape)],
        out_specs=[tok(d), tok(d), tok(LANES), tok(LANES)],
        out_shape=[jax.ShapeDtypeStruct((t, d), F32), jax.ShapeDtypeStruct((t, d), BF16),
                   jax.ShapeDtypeStruct((t, LANES), I32), jax.ShapeDtypeStruct((t, LANES), F32)],
        compiler_params=_params("parallel"),
        name="merge_ln_router",
    )(o_cmp, o_sel, o_win, o_swa, gates, x2, expand, w_bn, w_bs, w_o, ln_g, ln_b,
      wr_hi, wr_lo, b_r)


def _moe_kernel(blk_e_ref, n_used_ref, x_ref, g_ref, wi_ref, bi_ref, wo_ref, bo_ref, y_ref):
    del blk_e_ref
    f = wo_ref.shape[0]
    b = pl.program_id(0)

    @pl.when(b < n_used_ref[0])
    def _():
        hdn = jnp.dot(x_ref[...], wi_ref[...], preferred_element_type=F32) + bi_ref[...]
        hg = jnp.minimum(hdn[:, :f], SWIGLU_LIMIT)
        hu = jnp.clip(hdn[:, f:], -SWIGLU_LIMIT, SWIGLU_LIMIT)
        act = hg * jax.nn.sigmoid(SWIGLU_ALPHA * hg) * (hu + 1.0)
        y = jnp.dot(act.astype(BF16), wo_ref[...], preferred_element_type=F32) + bo_ref[...]
        y_ref[...] = y * g_ref[...]

    @pl.when(b >= n_used_ref[0])
    def _():
        y_ref[...] = jnp.zeros(y_ref.shape, F32)


def _moe_blocks(blk_e, n_used, xs, row_g, w_in, b_in, w_out, b_out):
    n_rows, d = xs.shape
    e, _, f2 = w_in.shape
    f = w_out.shape[1]
    n_blk = n_rows // MOE_ROWS
    grid_spec = pltpu.PrefetchScalarGridSpec(
        num_scalar_prefetch=2,
        grid=(n_blk,),
        in_specs=[pl.BlockSpec((MOE_ROWS, d), lambda b, be, nu: (b, 0)),
                  pl.BlockSpec((MOE_ROWS, 1), lambda b, be, nu: (b, 0)),
                  pl.BlockSpec((None, d, f2), lambda b, be, nu: (be[b], 0, 0)),
                  pl.BlockSpec((None, 1, f2), lambda b, be, nu: (be[b], 0, 0)),
                  pl.BlockSpec((None, f, d), lambda b, be, nu: (be[b], 0, 0)),
                  pl.BlockSpec((None, 1, d), lambda b, be, nu: (be[b], 0, 0))],
        out_specs=pl.BlockSpec((MOE_ROWS, d), lambda b, be, nu: (b, 0)),
    )
    return pl.pallas_call(
        _moe_kernel,
        grid_spec=grid_spec,
        out_shape=jax.ShapeDtypeStruct((n_rows, d), F32),
        compiler_params=_params("arbitrary"),
        name="moe_experts",
    )(blk_e, n_used, xs, row_g, w_in, b_in, w_out, b_out)


def _final_kernel(h_ref, y_ref, g_ref, b_ref, o_ref, *, alpha):
    f = y_ref[0]
    for k in range(1, y_ref.shape[0]):
        f = f + y_ref[k]
    o_ref[...] = _layer_norm(alpha * h_ref[...] + f, g_ref[...], b_ref[...])


def _final(h, ysel, ln_g, ln_b, alpha, tm):
    t, d = h.shape
    k = ysel.shape[0]
    return pl.pallas_call(
        functools.partial(_final_kernel, alpha=alpha),
        grid=(t // tm,),
        in_specs=[pl.BlockSpec((tm, d), lambda i: (i, 0)),
                  pl.BlockSpec((k, tm, d), lambda i: (0, i, 0)),
                  _full(ln_g.shape), _full(ln_b.shape)],
        out_specs=pl.BlockSpec((tm, d), lambda i: (i, 0)),
        out_shape=jax.ShapeDtypeStruct((t, d), F32),
        compiler_params=_params("parallel"),
        name="combine_ln",
    )(h, ysel, ln_g, ln_b)


def _rope_tables(s):
    half = HEAD_DIM // 2
    inv = ROPE_THETA ** (-jnp.arange(half, dtype=F32) / half)
    ang = jnp.arange(s, dtype=F32)[:, None] * inv[None, :]
    cos, sin = jnp.cos(ang), jnp.sin(ang)
    reps = LANES // HEAD_DIM
    cos_t = jnp.tile(jnp.concatenate([cos, cos], axis=1), (1, reps))
    sin_t = jnp.tile(jnp.concatenate([-sin, sin], axis=1), (1, reps))
    return cos_t, sin_t


def _moe_plan(top_e, top_g, t):
    a = t * TOP_K
    flat_e = top_e.reshape(a)
    flat_g = top_g.reshape(a)
    order = jnp.argsort(flat_e)
    se = flat_e[order]
    counts = jnp.bincount(flat_e, length=N_EXPERTS)
    padded = (counts + MOE_ROWS - 1) // MOE_ROWS * MOE_ROWS
    start = jnp.cumsum(counts) - counts
    pend = jnp.cumsum(padded)
    pstart = pend - padded
    dest = (pstart[se] + jnp.arange(a) - start[se]).astype(I32)
    n_blk = -(-a // MOE_ROWS) + N_EXPERTS
    n_rows = n_blk * MOE_ROWS
    row_t = jnp.zeros((n_rows,), I32).at[dest].set((order // TOP_K).astype(I32))
    row_g = jnp.zeros((n_rows,), F32).at[dest].set(flat_g[order])
    blk_e = jnp.minimum(jnp.searchsorted(pend, jnp.arange(n_blk) * MOE_ROWS, side='right'),
                        N_EXPERTS - 1).astype(I32)
    n_used = (pend[-1] // MOE_ROWS).astype(I32).reshape(1)
    slot = jnp.zeros((a,), I32).at[order].set(dest)
    return row_t, row_g, blk_e, n_used, slot.reshape(t, TOP_K)


def _layer(x, w_in, k_pe, k_w1, k_w2, v_pe, v_w1, v_w2, sinks, w_br_nsa, w_br_swa, w_out,
           ln1_g, ln1_b, w_router, b_router, w_e_in, b_e_in, w_e_out, b_e_out, ln2_g, ln2_b,
           alpha):
    b, s, d = x.shape
    t = b * s
    qb = Q_BLOCK
    nq_n, nkv = NSA_HEADS * HEAD_DIM, NSA_KV * HEAD_DIM
    nq_s, nkv_s = SWA_HEADS * HEAD_DIM, SWA_KV * HEAD_DIM
    widths = (nq_n, nkv, nkv, nkv, nkv, nkv, nkv, NSA_HEADS * 3, nq_s, nkv_s, nkv_s, 2 * d)
    offs = [0]
    for w in widths:
        offs.append(offs[-1] + w)
    col = lambda j: w_in[:, offs[j]:offs[j + 1]]
    (c_qn, c_kc, c_vc, c_ks, c_vs, c_kw, c_vw, c_gn, c_qs, c_k_s, c_v_s, c_gm) = map(col, range(12))
    w_rope = jnp.concatenate([c_qn, c_qs, c_ks, c_kw, c_k_s], axis=1).astype(BF16)
    w_plain = jnp.concatenate([c_kc, c_vc, c_vs, c_vw, c_v_s], axis=1).astype(BF16)
    gn_pad = LANES - NSA_HEADS * 3
    w_gate = jnp.concatenate([c_gm, c_gn, jnp.zeros((d, gn_pad), F32)], axis=1).astype(BF16)
    cos_t, sin_t = _rope_tables(s)

    x2 = x.reshape(t, d)
    qn_rot, qs_rot, kk_rot, qn_raw, plain, gates = _project(
        x2, w_rope, w_plain, w_gate, cos_t, sin_t, s, min(256, s))

    nc = (s - CMP_BLOCK) // CMP_STRIDE + 1
    ncp = s // CMP_STRIDE
    half = CMP_STRIDE * HEAD_DIM

    def halves(cols):
        v = cols.reshape(b, s, NSA_KV, HEAD_DIM).transpose(0, 2, 1, 3)
        return v.reshape(b * NSA_KV, ncp, half)

    t2 = jnp.stack([halves(plain[:, 0:nkv]), halves(plain[:, nkv:2 * nkv])])
    t_lo = t2
    t_hi = jnp.concatenate([t2[:, :, 1:], jnp.zeros_like(t2[:, :, :1])], axis=2)
    pe2 = jnp.stack([k_pe.reshape(2, half), v_pe.reshape(2, half)])
    w1 = jnp.stack([k_w1, v_w1]).astype(BF16)
    w2 = jnp.stack([k_w2, v_w2]).astype(BF16)
    kvc = _compress(t_lo, t_hi, pe2, w1, w2).reshape(2, b, NSA_KV, ncp, HEAD_DIM)

    nsel = s // SEL_BLOCK
    nselp = -(-nsel // LANES) * LANES
    cstart = jnp.arange(ncp) * CMP_STRIDE
    sstart = jnp.arange(nselp) * SEL_BLOCK
    overlap = ((cstart[:, None] < sstart[None, :] + SEL_BLOCK)
               & (cstart[:, None] + CMP_BLOCK > sstart[None, :])
               & (jnp.arange(ncp)[:, None] < nc) & (jnp.arange(nselp)[None, :] < nsel))
    o_cmp, notsel = _cmp_attention(qn_raw.reshape(b, s, nq_n), kvc[0], kvc[1],
                                   overlap.astype(BF16), qb)

    def group_major(cols):
        return cols.reshape(b, s, NSA_KV, HEAD_DIM).transpose(0, 2, 1, 3)

    k_sel = group_major(kk_rot[:, 0:nkv])
    v_sel = group_major(plain[:, 2 * nkv:3 * nkv])
    onehot = (jnp.arange(s)[:, None] // SEL_BLOCK == jnp.arange(nselp)[None, :])
    k_tail = jnp.concatenate([jnp.zeros((s, LANES - HEAD_DIM), F32),
                              jnp.where(onehot, SEL_PENALTY, 0.0)], axis=1).astype(BF16)
    k_aug = jnp.concatenate(
        [k_sel, jnp.broadcast_to(k_tail, (b, NSA_KV) + k_tail.shape)], axis=3)
    v_tail = jnp.zeros((LANES - HEAD_DIM,), BF16).at[0].set(1.0)
    v_aug = jnp.concatenate(
        [v_sel, jnp.broadcast_to(v_tail, (b, NSA_KV, s, LANES - HEAD_DIM))], axis=3)
    o_sel = _sel_attention(qn_rot.reshape(b, s, nq_n), notsel, k_aug, v_aug, qb, min(512, s))

    o_win = _band_attention(qn_rot.reshape(b, s, nq_n), kk_rot[:, nkv:2 * nkv].reshape(b, s, nkv),
                            plain[:, 3 * nkv:4 * nkv].reshape(b, s, nkv), NSA_WINDOW, None, qb)
    o_swa = _band_attention(qs_rot.reshape(b, s, nq_s),
                            kk_rot[:, 2 * nkv:2 * nkv + nkv_s].reshape(b, s, nkv_s),
                            plain[:, 4 * nkv:4 * nkv + nkv_s].reshape(b, s, nkv_s),
                            SWA_WINDOW, sinks, qb)

    gi = jnp.arange(LANES)
    ci = jnp.arange(3 * nq_n)
    expand = ((gi[:, None] // 3 == (ci[None, :] % nq_n) // HEAD_DIM)
              & (gi[:, None] % 3 == ci[None, :] // nq_n)
              & (gi[:, None] < NSA_HEADS * 3)).astype(BF16)
    wr_pad = jnp.pad(w_router, ((0, 0), (0, LANES - N_EXPERTS)))
    wr_hi = wr_pad.astype(BF16)
    wr_lo = (wr_pad - wr_hi.astype(F32)).astype(BF16)
    b_r = jnp.concatenate([b_router, jnp.full((LANES - N_EXPERTS,), -jnp.inf, F32)]).reshape(1, LANES)
    h, h_bf, top_e, top_g = _merge(
        o_cmp.reshape(t, nq_n), o_sel.reshape(t, nq_n), o_win.reshape(t, nq_n),
        o_swa.reshape(t, nq_s), gates, x2, expand, w_br_nsa.astype(BF16), w_br_swa.astype(BF16),
        w_out.astype(BF16), ln1_g.reshape(1, d), ln1_b.reshape(1, d), wr_hi, wr_lo, b_r,
        alpha, min(256, t))

    row_t, row_g, blk_e, n_used, slot = _moe_plan(top_e[:, :TOP_K], top_g[:, :TOP_K], t)
    xs = h_bf[row_t]
    ys = _moe_blocks(blk_e, n_used, xs, row_g.reshape(-1, 1), w_e_in.astype(BF16),
                     b_e_in.reshape(N_EXPERTS, 1, -1), w_e_out.astype(BF16),
                     b_e_out.reshape(N_EXPERTS, 1, -1))
    ysel = ys[slot.T]
    out = _final(h, ysel, ln2_g.reshape(1, d), ln2_b.reshape(1, d), alpha, min(256, t))
    return out.reshape(b, s, d)


def kernel(x, w_in, nsa_k_pe, nsa_k_w1, nsa_k_w2, nsa_v_pe, nsa_v_w1, nsa_v_w2, swa_sinks, w_br_nsa, w_br_swa, w_out, ln1_g, ln1_b, w_router, b_router, w_expert_in, b_expert_in, w_expert_out, b_expert_out, ln2_g, ln2_b):
    depth = w_in.shape[0]
    alpha = (2.0 * depth) ** 0.25
    for l in range(depth):
        x = _layer(x, w_in[l], nsa_k_pe[l], nsa_k_w1[l], nsa_k_w2[l], nsa_v_pe[l], nsa_v_w1[l],
                   nsa_v_w2[l], swa_sinks[l], w_br_nsa[l], w_br_swa[l], w_out[l], ln1_g[l],
                   ln1_b[l], w_router[l], b_router[l], w_expert_in[l], b_expert_in[l],
                   w_expert_out[l], b_expert_out[l], ln2_g[l], ln2_b[l], alpha)
    return x
```

```python
import functools

import jax
import jax.numpy as jnp
from jax import lax
from jax.experimental import pallas as pl
from jax.experimental.pallas import tpu as pltpu

BF16 = jnp.bfloat16
F32 = jnp.float32
I32 = jnp.int32

HEAD_DIM = 64
NSA_HEADS = 8
NSA_KV = 2
CMP_BLOCK = 32
CMP_STRIDE = 16
SEL_BLOCK = 64
SEL_TOPN = 16
NSA_WINDOW = 512
SWA_HEADS = 8
SWA_KV = 2
SWA_WINDOW = 128
Q_BLOCK = 128
ROPE_THETA = 10000.0
N_EXPERTS = 32
TOP_K = 4
SWIGLU_LIMIT = 7.0
SWIGLU_ALPHA = 1.702
LN_EPS = 1e-5

LANES = 128
MASKED = -1e30
M_INIT = -1e29
SEL_PENALTY = -(2.0 ** 100)
VMEM_LIMIT = 52 * 1024 * 1024
MOE_ROWS = 256

R_NSA = NSA_HEADS // NSA_KV
R_SWA = SWA_HEADS // SWA_KV
NT_DIMS = (((1,), (1,)), ((), ()))


def _params(*sem):
    return pltpu.CompilerParams(dimension_semantics=sem, vmem_limit_bytes=VMEM_LIMIT)


def _full(shape):
    n = len(shape)
    return pl.BlockSpec(shape, lambda *_: (0,) * n)


def _proj_kernel(x_ref, wr_ref, wp_ref, wg_ref, cos_ref, sin_ref,
                 qn_rot_ref, qs_rot_ref, kk_rot_ref, qn_raw_ref, plain_ref, gates_ref):
    xb = x_ref[...].astype(BF16)
    acc = jnp.dot(xb, wr_ref[...], preferred_element_type=F32)
    cos = cos_ref[...]
    sin = sin_ref[...]
    lane = lax.broadcasted_iota(I32, cos.shape, 1)
    first_half = (lane & (HEAD_DIM - 1)) < HEAD_DIM // 2

    def rope(t):
        partner = jnp.where(first_half, pltpu.roll(t, LANES - HEAD_DIM // 2, 1),
                            pltpu.roll(t, HEAD_DIM // 2, 1))
        return (t * cos + partner * sin).astype(BF16)

    nq = qn_rot_ref.shape[1] // LANES
    ns = qs_rot_ref.shape[1] // LANES
    nk = kk_rot_ref.shape[1] // LANES
    for c in range(nq):
        qn_rot_ref[:, c * LANES:(c + 1) * LANES] = rope(acc[:, c * LANES:(c + 1) * LANES])
    for c in range(ns):
        o = (nq + c) * LANES
        qs_rot_ref[:, c * LANES:(c + 1) * LANES] = rope(acc[:, o:o + LANES])
    for c in range(nk):
        o = (nq + ns + c) * LANES
        kk_rot_ref[:, c * LANES:(c + 1) * LANES] = rope(acc[:, o:o + LANES])
    qn_raw_ref[...] = acc[:, :nq * LANES].astype(BF16)
    plain_ref[...] = jnp.dot(xb, wp_ref[...], preferred_element_type=F32).astype(BF16)
    gates_ref[...] = jnp.dot(xb, wg_ref[...], preferred_element_type=F32)


def _project(x2, w_rope, w_plain, w_gate, cos_t, sin_t, seq, tm):
    t, d = x2.shape
    nr, npl, ng = w_rope.shape[1], w_plain.shape[1], w_gate.shape[1]
    nqn, nqs = NSA_HEADS * HEAD_DIM, SWA_HEADS * HEAD_DIM
    nkk = nr - nqn - nqs
    spb = seq // tm
    row = lambda i: (i, 0)
    return pl.pallas_call(
        _proj_kernel,
        grid=(t // tm,),
        in_specs=[pl.BlockSpec((tm, d), row), _full(w_rope.shape), _full(w_plain.shape),
                  _full(w_gate.shape),
                  pl.BlockSpec((tm, LANES), lambda i: (i % spb, 0)),
                  pl.BlockSpec((tm, LANES), lambda i: (i % spb, 0))],
        out_specs=[pl.BlockSpec((tm, nqn), row), pl.BlockSpec((tm, nqs), row),
                   pl.BlockSpec((tm, nkk), row), pl.BlockSpec((tm, nqn), row),
                   pl.BlockSpec((tm, npl), row), pl.BlockSpec((tm, ng), row)],
        out_shape=[jax.ShapeDtypeStruct((t, nqn), BF16), jax.ShapeDtypeStruct((t, nqs), BF16),
                   jax.ShapeDtypeStruct((t, nkk), BF16), jax.ShapeDtypeStruct((t, nqn), BF16),
                   jax.ShapeDtypeStruct((t, npl), BF16), jax.ShapeDtypeStruct((t, ng), F32)],
        compiler_params=_params("parallel"),
        name="proj",
    )(x2, w_rope, w_plain, w_gate, cos_t, sin_t)


def _compress_kernel(a_ref, b_ref, pe_ref, w1_ref, w2_ref, out_ref):
    half = a_ref.shape[1]
    a = (a_ref[...].astype(F32) + pe_ref[0:1, :]).astype(BF16)
    b = (b_ref[...].astype(F32) + pe_ref[1:2, :]).astype(BF16)
    hid = jnp.dot(a, w1_ref[0:half, :], preferred_element_type=F32)
    hid = hid + jnp.dot(b, w1_ref[half:2 * half, :], preferred_element_type=F32)
    act = jax.nn.gelu(hid).astype(BF16)
    out_ref[...] = jnp.dot(act, w2_ref[...], preferred_element_type=F32).astype(BF16)


def _compress(t_lo, t_hi, pe2, w1, w2):
    two, bg, ncp, half = t_lo.shape
    hid = w1.shape[2]
    blk = lambda shape: pl.BlockSpec((None, None) + shape, lambda j, i: (j, i, 0, 0))
    wsp = lambda shape: pl.BlockSpec((None,) + shape, lambda j, i: (j, 0, 0))
    return pl.pallas_call(
        _compress_kernel,
        grid=(two, bg),
        in_specs=[blk((ncp, half)), blk((ncp, half)), wsp((2, half)), wsp((2 * half, hid)),
                  wsp((hid, HEAD_DIM))],
        out_specs=blk((ncp, HEAD_DIM)),
        out_shape=jax.ShapeDtypeStruct((two, bg, ncp, HEAD_DIM), BF16),
        compiler_params=_params("parallel", "parallel"),
        name="compress",
    )(t_lo, t_hi, pe2, w1, w2)


def _stack_heads(q, g, r):
    return jnp.concatenate(
        [q[:, (g * r + j) * HEAD_DIM:(g * r + j + 1) * HEAD_DIM] for j in range(r)], axis=0)


def _unstack_heads(parts, r, qb):
    return jnp.concatenate([o[j * qb:(j + 1) * qb] for o in parts for j in range(r)], axis=1)


def _topk_mask(vals, k):
    n = vals.shape[1]
    col = lax.broadcasted_iota(I32, vals.shape, 1).astype(F32)
    taken = jnp.zeros(vals.shape, F32)
    work = vals
    for _ in range(k):
        mx = jnp.max(work, axis=1, keepdims=True)
        first = jnp.min(jnp.where(work == mx, col, float(n)), axis=1, keepdims=True)
        pick = col == first
        taken = jnp.where(pick, 1.0, taken)
        work = jnp.where(pick, -jnp.inf, work)
    return taken > 0.5


def _cmp_kernel(q_ref, kc_ref, vc_ref, ov_ref, o_ref, notsel_ref):
    qb = q_ref.shape[0]
    ncp = kc_ref.shape[1]
    nselp = ov_ref.shape[1]
    i = pl.program_id(1)
    rows = R_NSA * qb
    pos = i * qb + (lax.broadcasted_iota(I32, (rows, 1), 0) & (qb - 1))
    cend = lax.broadcasted_iota(I32, (1, ncp), 1) * CMP_STRIDE + (CMP_BLOCK - 1)
    visible = cend <= pos
    live = (pos >= CMP_BLOCK - 1).astype(F32)
    q = q_ref[...]
    outs = []
    for g in range(NSA_KV):
        qg = _stack_heads(q, g, R_NSA) * (HEAD_DIM ** -0.5)
        s = lax.dot_general(qg, kc_ref[g], NT_DIMS, preferred_element_type=F32)
        s = jnp.where(visible, s, MASKED)
        e = jnp.exp(s - jnp.max(s, axis=1, keepdims=True))
        p = e / jnp.sum(e, axis=1, keepdims=True) * live
        outs.append(jnp.dot(p.astype(BF16), vc_ref[g], preferred_element_type=F32))
        psum = p[0:qb]
        for j in range(1, R_NSA):
            psum = psum + p[j * qb:(j + 1) * qb]
        p_hi = psum.astype(BF16)
        p_lo = (psum - p_hi.astype(F32)).astype(BF16)
        imp = (jnp.dot(p_hi, ov_ref[...], preferred_element_type=F32)
               + jnp.dot(p_lo, ov_ref[...], preferred_element_type=F32))
        qpos = i * qb + lax.broadcasted_iota(I32, (qb, 1), 0)
        cur = qpos >> 6
        jb = lax.broadcasted_iota(I32, (1, nselp), 1)
        forced = (jb == 0) | (jb == cur) | (jb == cur - 1)
        imp = jnp.where(jb > cur, -1.0, jnp.where(forced, 1e6, imp))
        sel = _topk_mask(imp, SEL_TOPN)
        notsel_ref[g] = jnp.where(sel, 0.0, 1.0).astype(BF16)
    o_ref[...] = _unstack_heads(outs, R_NSA, qb)


def _cmp_attention(q_raw, kc, vc, overlap, qb):
    b, s, hq = q_raw.shape
    _, g, ncp, dh = kc.shape
    nselp = overlap.shape[1]
    return pl.pallas_call(
        _cmp_kernel,
        grid=(b, s // qb),
        in_specs=[pl.BlockSpec((None, qb, hq), lambda bi, i: (bi, i, 0)),
                  pl.BlockSpec((None, g, ncp, dh), lambda bi, i: (bi, 0, 0, 0)),
                  pl.BlockSpec((None, g, ncp, dh), lambda bi, i: (bi, 0, 0, 0)),
                  _full(overlap.shape)],
        out_specs=[pl.BlockSpec((None, qb, hq), lambda bi, i: (bi, i, 0)),
                   pl.BlockSpec((None, g, qb, nselp), lambda bi, i: (bi, 0, i, 0))],
        out_shape=[jax.ShapeDtypeStruct((b, s, hq), F32),
                   jax.ShapeDtypeStruct((b, g, s, nselp), BF16)],
        compiler_params=_params("parallel", "parallel"),
        name="cmp_attn",
    )(q_raw, kc, vc, overlap)


def _sel_kernel(q_ref, notsel_ref, k_ref, v_ref, o_ref, *, tk):
    qb = q_ref.shape[0]
    i = pl.program_id(1)
    rows = R_NSA * qb
    qpos = i * qb + (lax.broadcasted_iota(I32, (rows, 1), 0) & (qb - 1))
    n_clear = (i * qb) // tk
    q = q_ref[...]
    outs = []
    for g in range(NSA_KV):
        qg = _stack_heads(q, g, R_NSA) * (HEAD_DIM ** -0.5)
        ns = notsel_ref[g]
        q_aug = jnp.concatenate(
            [qg, jnp.zeros((rows, LANES - HEAD_DIM), BF16),
             jnp.concatenate([ns] * R_NSA, axis=0)], axis=1)

        def step(kt, carry, causal, g=g, q_aug=q_aug):
            m, acc = carry
            start = pl.multiple_of(kt * tk, tk)
            k_t = k_ref[g, pl.ds(start, tk), :]
            v_t = v_ref[g, pl.ds(start, tk), :]
            s = lax.dot_general(q_aug, k_t, NT_DIMS, preferred_element_type=F32)
            if causal:
                kpos = start + lax.broadcasted_iota(I32, (1, tk), 1)
                s = jnp.where(kpos <= qpos, s, MASKED)
            m_new = jnp.maximum(m, jnp.max(s, axis=1, keepdims=True))
            p = jnp.exp(s - m_new)
            acc = jnp.exp(m - m_new) * acc + jnp.dot(p.astype(BF16), v_t,
                                                    preferred_element_type=F32)
            return m_new, acc

        init = (jnp.full((rows, 1), M_INIT, F32), jnp.zeros((rows, LANES), F32))
        carry = lax.fori_loop(0, n_clear, functools.partial(step, causal=False), init)
        _, acc = step(n_clear, carry, True)
        outs.append(acc[:, :HEAD_DIM] / acc[:, HEAD_DIM:HEAD_DIM + 1])
    o_ref[...] = _unstack_heads(outs, R_NSA, qb)


def _sel_attention(q_rot, notsel, k_aug, v_aug, qb, tk):
    b, s, hq = q_rot.shape
    _, g, _, kw = k_aug.shape
    nselp = notsel.shape[3]
    return pl.pallas_call(
        functools.partial(_sel_kernel, tk=tk),
        grid=(b, s // qb),
        in_specs=[pl.BlockSpec((None, qb, hq), lambda bi, i: (bi, i, 0)),
                  pl.BlockSpec((None, g, qb, nselp), lambda bi, i: (bi, 0, i, 0)),
                  pl.BlockSpec((None, g, s, kw), lambda bi, i: (bi, 0, 0, 0)),
                  pl.BlockSpec((None, g, s, LANES), lambda bi, i: (bi, 0, 0, 0))],
        out_specs=pl.BlockSpec((None, qb, hq), lambda bi, i: (bi, i, 0)),
        out_shape=jax.ShapeDtypeStruct((b, s, hq), F32),
        compiler_params=_params("parallel", "parallel"),
        name="sel_attn",
    )(q_rot, notsel, k_aug, v_aug)


def _band_kernel(*refs, window, wlen, r, kv, has_sinks):
    if has_sinks:
        sink_ref, q_ref, k_ref, v_ref, o_ref = refs
    else:
        q_ref, k_ref, v_ref, o_ref = refs
    qb = q_ref.shape[0]
    i = pl.program_id(1)
    rows = r * qb
    start = pl.multiple_of(jnp.maximum((i + 1) * qb - wlen, 0), qb)
    qpos = i * qb + (lax.broadcasted_iota(I32, (rows, 1), 0) & (qb - 1))
    kpos = start + lax.broadcasted_iota(I32, (1, wlen), 1)
    rel = qpos - kpos
    inside = (rel >= 0) & (rel < window)
    q = q_ref[...]
    kw = k_ref[pl.ds(start, wlen), :]
    vw = v_ref[pl.ds(start, wlen), :]
    outs = []
    for g in range(kv):
        qg = _stack_heads(q, g, r) * (HEAD_DIM ** -0.5)
        s = lax.dot_general(qg, kw[:, g * HEAD_DIM:(g + 1) * HEAD_DIM], NT_DIMS,
                            preferred_element_type=F32)
        s = jnp.where(inside, s, MASKED)
        m = jnp.max(s, axis=1, keepdims=True)
        if has_sinks:
            sk = jnp.concatenate(
                [jnp.full((qb, 1), sink_ref[g * r + j], F32) for j in range(r)], axis=0)
            m = jnp.maximum(m, sk)
        e = jnp.exp(s - m)
        den = jnp.sum(e, axis=1, keepdims=True)
        if has_sinks:
            den = den + jnp.exp(sk - m)
        p = (e / den).astype(BF16)
        outs.append(jnp.dot(p, vw[:, g * HEAD_DIM:(g + 1) * HEAD_DIM],
                            preferred_element_type=F32))
    o_ref[...] = _unstack_heads(outs, r, qb)


def _band_attention(q_rot, k, v, window, sinks, qb):
    b, s, hq = q_rot.shape
    gk = k.shape[2]
    kv = gk // HEAD_DIM
    r = hq // gk
    back = -(-window // qb)
    wlen = (back + 1) * qb
    assert wlen <= s
    has_sinks = sinks is not None
    in_specs = [pl.BlockSpec((None, qb, hq), lambda bi, i: (bi, i, 0)),
                pl.BlockSpec((None, s, gk), lambda bi, i: (bi, 0, 0)),
                pl.BlockSpec((None, s, gk), lambda bi, i: (bi, 0, 0))]
    args = [q_rot, k, v]
    if has_sinks:
        in_specs = [pl.BlockSpec(memory_space=pltpu.SMEM)] + in_specs
        args = [sinks.astype(F32)] + args
    return pl.pallas_call(
        functools.partial(_band_kernel, window=window, wlen=wlen, r=r, kv=kv,
                          has_sinks=has_sinks),
        grid=(b, s // qb),
        in_specs=in_specs,
        out_specs=pl.BlockSpec((None, qb, hq), lambda bi, i: (bi, i, 0)),
        out_shape=jax.ShapeDtypeStruct((b, s, hq), F32),
        compiler_params=_params("parallel", "parallel"),
        name="band_attn_sink" if has_sinks else "band_attn",
    )(*args)


def _layer_norm(v, g, b):
    mu = jnp.mean(v, axis=1, keepdims=True)
    c = v - mu
    var = jnp.mean(c * c, axis=1, keepdims=True)
    return c * lax.rsqrt(var + LN_EPS) * g + b


def _split_dot(a, w):
    hi = a.astype(BF16)
    lo = (a - hi.astype(F32)).astype(BF16)
    return (jnp.dot(hi, w, preferred_element_type=F32)
            + jnp.dot(lo, w, preferred_element_type=F32))


def _merge_kernel(ocmp_ref, osel_ref, owin_ref, oswa_ref, gates_ref, x_ref, exp_ref,
                  wbn_ref, wbs_ref, wo_ref, lng_ref, lnb_ref, wrh_ref, wrl_ref, br_ref,
                  h_ref, te_ref, tg_ref, tr_ref, counts_ref, cnt_ref, *, alpha):
    d = x_ref.shape[1]
    hq = ocmp_ref.shape[1]
    gates = gates_ref[...]
    gn = jax.nn.sigmoid(gates[:, 2 * d:])
    gexp = _split_dot(gn, exp_ref[...])
    o_nsa = (gexp[:, 0:hq] * ocmp_ref[...] + gexp[:, hq:2 * hq] * osel_ref[...]
             + gexp[:, 2 * hq:3 * hq] * owin_ref[...])
    y_nsa = jnp.dot(o_nsa.astype(BF16), wbn_ref[...], preferred_element_type=F32)
    y_swa = jnp.dot(oswa_ref[...].astype(BF16), wbs_ref[...], preferred_element_type=F32)
    gm = jax.nn.sigmoid(gates[:, :2 * d])
    mixed = gm[:, :d] * y_nsa + gm[:, d:] * y_swa
    z = jnp.dot(mixed.astype(BF16), wo_ref[...], preferred_element_type=F32)
    h = _layer_norm(alpha * x_ref[...] + z, lng_ref[...], lnb_ref[...])
    h_ref[...] = h
    h_hi = h.astype(BF16)
    h_lo = (h - h_hi.astype(F32)).astype(BF16)
    logits = (jnp.dot(h_hi, wrh_ref[...], preferred_element_type=F32)
              + jnp.dot(h_lo, wrh_ref[...], preferred_element_type=F32)
              + jnp.dot(h_hi, wrl_ref[...], preferred_element_type=F32)) + br_ref[...]
    col = lax.broadcasted_iota(I32, logits.shape, 1).astype(F32)
    work = logits
    vals, ids = [], []
    for _ in range(TOP_K):
        mx = jnp.max(work, axis=1, keepdims=True)
        first = jnp.min(jnp.where(work == mx, col, float(LANES)), axis=1, keepdims=True)
        vals.append(mx)
        ids.append(first)
        work = jnp.where(col == first, -jnp.inf, work)
    es = [jnp.exp(v - vals[0]) for v in vals]
    den = es[0]
    for e in es[1:]:
        den = den + e
    @pl.when(pl.program_id(0) == 0)
    def _():
        cnt_ref[...] = jnp.zeros(cnt_ref.shape, F32)

    tm = logits.shape[0]
    hits = jnp.zeros(logits.shape, F32)
    for k in range(TOP_K):
        hits = jnp.where(col == ids[k], 1.0, hits)
    earlier = (lax.broadcasted_iota(I32, (tm, tm), 1)
               < lax.broadcasted_iota(I32, (tm, tm), 0))
    before = jnp.dot(jnp.where(earlier, 1.0, 0.0).astype(BF16), hits.astype(BF16),
                     preferred_element_type=F32) + cnt_ref[...]
    cnt_ref[...] = cnt_ref[...] + jnp.sum(hits, axis=0, keepdims=True)
    counts_ref[...] = cnt_ref[...]
    te = jnp.zeros(logits.shape, F32)
    tg = jnp.zeros(logits.shape, F32)
    tr = jnp.zeros(logits.shape, F32)
    for k in range(TOP_K):
        rank = jnp.sum(jnp.where(col == ids[k], before, 0.0), axis=1, keepdims=True)
        te = jnp.where(col == float(k), ids[k], te)
        tg = jnp.where(col == float(k), es[k] / den, tg)
        tr = jnp.where(col == float(k), rank, tr)
    te_ref[...] = te.astype(I32)
    tg_ref[...] = tg
    tr_ref[...] = tr.astype(I32)


def _merge(o_cmp, o_sel, o_win, o_swa, gates, x2, expand, w_bn, w_bs, w_o, ln_g, ln_b,
           wr_hi, wr_lo, b_r, alpha, tm):
    t, d = x2.shape
    hq = o_cmp.shape[1]
    row = lambda i: (i, 0)
    tok = lambda n: pl.BlockSpec((tm, n), row)
    return pl.pallas_call(
        functools.partial(_merge_kernel, alpha=alpha),
        grid=(t // tm,),
        in_specs=[tok(hq), tok(hq), tok(hq), tok(hq), tok(gates.shape[1]), tok(d),
                  _full(expand.shape), _full(w_bn.shape), _full(w_bs.shape), _full(w_o.shape),
                  _full(ln_g.shape), _full(ln_b.shape), _full(wr_hi.shape), _full(wr_lo.shape),
                  _full(b_r.shape)],
        out_specs=[tok(d), tok(LANES), tok(LANES), tok(LANES), _full((1, LANES))],
        out_shape=[jax.ShapeDtypeStruct((t, d), F32), jax.ShapeDtypeStruct((t, LANES), I32),
                   jax.ShapeDtypeStruct((t, LANES), F32), jax.ShapeDtypeStruct((t, LANES), I32),
                   jax.ShapeDtypeStruct((1, LANES), F32)],
        scratch_shapes=[pltpu.VMEM((1, LANES), F32)],
        compiler_params=_params("arbitrary"),
        name="merge_ln_router",
    )(o_cmp, o_sel, o_win, o_swa, gates, x2, expand, w_bn, w_bs, w_o, ln_g, ln_b,
      wr_hi, wr_lo, b_r)


def _dispatch_kernel(dest_ref, h_ref, xs_init_ref, xs_ref, sem):
    del xs_init_ref
    tm = h_ref.shape[0]

    def issue(r, carry):
        for k in range(TOP_K):
            dst = dest_ref[r * TOP_K + k]
            pltpu.make_async_copy(h_ref.at[pl.ds(r, 1), :], xs_ref.at[pl.ds(dst, 1), :],
                                  sem).start()
        return carry

    lax.fori_loop(0, tm, issue, 0, unroll=8)
    for _ in range(TOP_K):
        pltpu.make_async_copy(h_ref, xs_ref.at[pl.ds(0, tm), :], sem).wait()


def _dispatch(dest, h, n_rows, tm):
    t, d = h.shape
    return pl.pallas_call(
        _dispatch_kernel,
        grid=(t // tm,),
        in_specs=[pl.BlockSpec((tm * TOP_K,), lambda i: (i,), memory_space=pltpu.SMEM),
                  pl.BlockSpec((tm, d), lambda i: (i, 0)),
                  pl.BlockSpec(memory_space=pl.ANY)],
        out_specs=pl.BlockSpec(memory_space=pl.ANY),
        out_shape=jax.ShapeDtypeStruct((n_rows, d), F32),
        scratch_shapes=[pltpu.SemaphoreType.DMA(())],
        input_output_aliases={2: 0},
        compiler_params=_params("arbitrary"),
        name="moe_dispatch",
    )(dest, h, jnp.zeros((n_rows, d), F32))


def _moe_kernel(blk_e_ref, n_used_ref, x_ref, wi_ref, bi_ref, wo_ref, bo_ref, y_ref):
    del blk_e_ref
    f = wo_ref.shape[0]
    b = pl.program_id(0)

    @pl.when(b < n_used_ref[0])
    def _():
        hdn = jnp.dot(x_ref[...].astype(BF16), wi_ref[...],
                      preferred_element_type=F32) + bi_ref[...]
        hg = jnp.minimum(hdn[:, :f], SWIGLU_LIMIT)
        hu = jnp.clip(hdn[:, f:], -SWIGLU_LIMIT, SWIGLU_LIMIT)
        act = hg * jax.nn.sigmoid(SWIGLU_ALPHA * hg) * (hu + 1.0)
        y_ref[...] = jnp.dot(act.astype(BF16), wo_ref[...],
                             preferred_element_type=F32) + bo_ref[...]

    @pl.when(b >= n_used_ref[0])
    def _():
        y_ref[...] = jnp.zeros(y_ref.shape, F32)


def _moe_blocks(blk_e, n_used, xs, w_in, b_in, w_out, b_out):
    n_rows, d = xs.shape
    e, _, f2 = w_in.shape
    f = w_out.shape[1]
    n_blk = n_rows // MOE_ROWS
    grid_spec = pltpu.PrefetchScalarGridSpec(
        num_scalar_prefetch=2,
        grid=(n_blk,),
        in_specs=[pl.BlockSpec((MOE_ROWS, d), lambda b, be, nu: (b, 0)),
                  pl.BlockSpec((None, d, f2), lambda b, be, nu: (be[b], 0, 0)),
                  pl.BlockSpec((None, 1, f2), lambda b, be, nu: (be[b], 0, 0)),
                  pl.BlockSpec((None, f, d), lambda b, be, nu: (be[b], 0, 0)),
                  pl.BlockSpec((None, 1, d), lambda b, be, nu: (be[b], 0, 0))],
        out_specs=pl.BlockSpec((MOE_ROWS, d), lambda b, be, nu: (b, 0)),
    )
    return pl.pallas_call(
        _moe_kernel,
        grid_spec=grid_spec,
        out_shape=jax.ShapeDtypeStruct((n_rows, d), F32),
        compiler_params=_params("arbitrary"),
        name="moe_experts",
    )(blk_e, n_used, xs, w_in, b_in, w_out, b_out)


def _final_kernel(dest_ref, h_ref, tg_ref, ys_ref, g_ref, b_ref, o_ref, ybuf, sem, *, alpha):
    tm = h_ref.shape[0]

    def issue(r, carry):
        for k in range(TOP_K):
            src = dest_ref[r * TOP_K + k]
            pltpu.make_async_copy(ys_ref.at[pl.ds(src, 1), :], ybuf.at[k, pl.ds(r, 1), :],
                                  sem).start()
        return carry

    lax.fori_loop(0, tm, issue, 0, unroll=8)
    for k in range(TOP_K):
        pltpu.make_async_copy(ys_ref.at[pl.ds(0, tm), :], ybuf.at[k], sem).wait()
    tg = tg_ref[...]
    f = tg[:, 0:1] * ybuf[0]
    for k in range(1, TOP_K):
        f = f + tg[:, k:k + 1] * ybuf[k]
    o_ref[...] = _layer_norm(alpha * h_ref[...] + f, g_ref[...], b_ref[...])


def _final(dest, h, top_g, ys, ln_g, ln_b, alpha, tm):
    t, d = h.shape
    return pl.pallas_call(
        functools.partial(_final_kernel, alpha=alpha),
        grid=(t // tm,),
        in_specs=[pl.BlockSpec((tm * TOP_K,), lambda i: (i,), memory_space=pltpu.SMEM),
                  pl.BlockSpec((tm, d), lambda i: (i, 0)),
                  pl.BlockSpec((tm, LANES), lambda i: (i, 0)),
                  pl.BlockSpec(memory_space=pl.ANY),
                  _full(ln_g.shape), _full(ln_b.shape)],
        out_specs=pl.BlockSpec((tm, d), lambda i: (i, 0)),
        out_shape=jax.ShapeDtypeStruct((t, d), F32),
        scratch_shapes=[pltpu.VMEM((TOP_K, tm, d), F32), pltpu.SemaphoreType.DMA(())],
        compiler_params=_params("arbitrary"),
        name="combine_ln",
    )(dest, h, top_g, ys, ln_g, ln_b)


def _rope_tables(s):
    half = HEAD_DIM // 2
    inv = ROPE_THETA ** (-jnp.arange(half, dtype=F32) / half)
    ang = jnp.arange(s, dtype=F32)[:, None] * inv[None, :]
    cos, sin = jnp.cos(ang), jnp.sin(ang)
    reps = LANES // HEAD_DIM
    cos_t = jnp.tile(jnp.concatenate([cos, cos], axis=1), (1, reps))
    sin_t = jnp.tile(jnp.concatenate([-sin, sin], axis=1), (1, reps))
    return cos_t, sin_t


def _moe_plan(top_e, rank, counts, t):
    a = t * TOP_K
    padded = (counts + MOE_ROWS - 1) // MOE_ROWS * MOE_ROWS
    pend = jnp.cumsum(padded)
    pstart = pend - padded
    experts = jnp.arange(N_EXPERTS, dtype=I32)
    base = jnp.sum(jnp.where(top_e[:, :, None] == experts, pstart, 0), axis=2)
    dest = (base + rank).astype(I32).reshape(a)
    n_blk = -(-a // MOE_ROWS) + N_EXPERTS
    blk_first = jnp.arange(n_blk, dtype=I32) * MOE_ROWS
    blk_e = jnp.minimum(jnp.sum(pend[None, :] <= blk_first[:, None], axis=1),
                        N_EXPERTS - 1).astype(I32)
    n_used = (pend[-1] // MOE_ROWS).astype(I32).reshape(1)
    return dest, blk_e, n_used, n_blk * MOE_ROWS


def _layer(x, w_in, k_pe, k_w1, k_w2, v_pe, v_w1, v_w2, sinks, w_br_nsa, w_br_swa, w_out,
           ln1_g, ln1_b, w_router, b_router, w_e_in, b_e_in, w_e_out, b_e_out, ln2_g, ln2_b,
           alpha):
    b, s, d = x.shape
    t = b * s
    qb = Q_BLOCK
    nq_n, nkv = NSA_HEADS * HEAD_DIM, NSA_KV * HEAD_DIM
    nq_s, nkv_s = SWA_HEADS * HEAD_DIM, SWA_KV * HEAD_DIM
    widths = (nq_n, nkv, nkv, nkv, nkv, nkv, nkv, NSA_HEADS * 3, nq_s, nkv_s, nkv_s, 2 * d)
    offs = [0]
    for w in widths:
        offs.append(offs[-1] + w)
    col = lambda j: w_in[:, offs[j]:offs[j + 1]]
    (c_qn, c_kc, c_vc, c_ks, c_vs, c_kw, c_vw, c_gn, c_qs, c_k_s, c_v_s, c_gm) = map(col, range(12))
    w_rope = jnp.concatenate([c_qn, c_qs, c_ks, c_kw, c_k_s], axis=1).astype(BF16)
    w_plain = jnp.concatenate([c_kc, c_vc, c_vs, c_vw, c_v_s], axis=1).astype(BF16)
    gn_pad = LANES - NSA_HEADS * 3
    w_gate = jnp.concatenate([c_gm, c_gn, jnp.zeros((d, gn_pad), F32)], axis=1).astype(BF16)
    cos_t, sin_t = _rope_tables(s)

    x2 = x.reshape(t, d)
    qn_rot, qs_rot, kk_rot, qn_raw, plain, gates = _project(
        x2, w_rope, w_plain, w_gate, cos_t, sin_t, s, min(256, s))

    nc = (s - CMP_BLOCK) // CMP_STRIDE + 1
    ncp = s // CMP_STRIDE
    half = CMP_STRIDE * HEAD_DIM

    def halves(cols):
        v = cols.reshape(b, s, NSA_KV, HEAD_DIM).transpose(0, 2, 1, 3)
        return v.reshape(b * NSA_KV, ncp, half)

    t2 = jnp.stack([halves(plain[:, 0:nkv]), halves(plain[:, nkv:2 * nkv])])
    t_lo = t2
    t_hi = jnp.concatenate([t2[:, :, 1:], jnp.zeros_like(t2[:, :, :1])], axis=2)
    pe2 = jnp.stack([k_pe.reshape(2, half), v_pe.reshape(2, half)])
    w1 = jnp.stack([k_w1, v_w1]).astype(BF16)
    w2 = jnp.stack([k_w2, v_w2]).astype(BF16)
    kvc = _compress(t_lo, t_hi, pe2, w1, w2).reshape(2, b, NSA_KV, ncp, HEAD_DIM)

    nsel = s // SEL_BLOCK
    nselp = -(-nsel // LANES) * LANES
    cstart = jnp.arange(ncp) * CMP_STRIDE
    sstart = jnp.arange(nselp) * SEL_BLOCK
    overlap = ((cstart[:, None] < sstart[None, :] + SEL_BLOCK)
               & (cstart[:, None] + CMP_BLOCK > sstart[None, :])
               & (jnp.arange(ncp)[:, None] < nc) & (jnp.arange(nselp)[None, :] < nsel))
    o_cmp, notsel = _cmp_attention(qn_raw.reshape(b, s, nq_n), kvc[0], kvc[1],
                                   overlap.astype(BF16), qb)

    def group_major(cols):
        return cols.reshape(b, s, NSA_KV, HEAD_DIM).transpose(0, 2, 1, 3)

    k_sel = group_major(kk_rot[:, 0:nkv])
    v_sel = group_major(plain[:, 2 * nkv:3 * nkv])
    onehot = (jnp.arange(s)[:, None] // SEL_BLOCK == jnp.arange(nselp)[None, :])
    k_tail = jnp.concatenate([jnp.zeros((s, LANES - HEAD_DIM), F32),
                              jnp.where(onehot, SEL_PENALTY, 0.0)], axis=1).astype(BF16)
    k_aug = jnp.concatenate(
        [k_sel, jnp.broadcast_to(k_tail, (b, NSA_KV) + k_tail.shape)], axis=3)
    v_tail = jnp.zeros((LANES - HEAD_DIM,), BF16).at[0].set(1.0)
    v_aug = jnp.concatenate(
        [v_sel, jnp.broadcast_to(v_tail, (b, NSA_KV, s, LANES - HEAD_DIM))], axis=3)
    o_sel = _sel_attention(qn_rot.reshape(b, s, nq_n), notsel, k_aug, v_aug, qb, min(512, s))

    o_win = _band_attention(qn_rot.reshape(b, s, nq_n), kk_rot[:, nkv:2 * nkv].reshape(b, s, nkv),
                            plain[:, 3 * nkv:4 * nkv].reshape(b, s, nkv), NSA_WINDOW, None, qb)
    o_swa = _band_attention(qs_rot.reshape(b, s, nq_s),
                            kk_rot[:, 2 * nkv:2 * nkv + nkv_s].reshape(b, s, nkv_s),
                            plain[:, 4 * nkv:4 * nkv + nkv_s].reshape(b, s, nkv_s),
                            SWA_WINDOW, sinks, qb)

    gi = jnp.arange(LANES)
    ci = jnp.arange(3 * nq_n)
    expand = ((gi[:, None] // 3 == (ci[None, :] % nq_n) // HEAD_DIM)
              & (gi[:, None] % 3 == ci[None, :] // nq_n)
              & (gi[:, None] < NSA_HEADS * 3)).astype(BF16)
    wr_pad = jnp.pad(w_router, ((0, 0), (0, LANES - N_EXPERTS)))
    wr_hi = wr_pad.astype(BF16)
    wr_lo = (wr_pad - wr_hi.astype(F32)).astype(BF16)
    b_r = jnp.concatenate([b_router, jnp.full((LANES - N_EXPERTS,), -jnp.inf, F32)]).reshape(1, LANES)
    tm = min(256, t)
    h, top_e, top_g, rank, counts = _merge(
        o_cmp.reshape(t, nq_n), o_sel.reshape(t, nq_n), o_win.reshape(t, nq_n),
        o_swa.reshape(t, nq_s), gates, x2, expand, w_br_nsa.astype(BF16), w_br_swa.astype(BF16),
        w_out.astype(BF16), ln1_g.reshape(1, d), ln1_b.reshape(1, d), wr_hi, wr_lo, b_r,
        alpha, tm)

    dest, blk_e, n_used, n_rows = _moe_plan(
        top_e[:, :TOP_K], rank[:, :TOP_K], counts[0, :N_EXPERTS].astype(I32), t)
    xs = _dispatch(dest, h, n_rows, tm)
    ys = _moe_blocks(blk_e, n_used, xs, w_e_in.astype(BF16), b_e_in.reshape(N_EXPERTS, 1, -1),
                     w_e_out.astype(BF16), b_e_out.reshape(N_EXPERTS, 1, -1))
    out = _final(dest, h, top_g, ys, ln2_g.reshape(1, d), ln2_b.reshape(1, d), alpha, tm)
    return out.reshape(b, s, d)


def kernel(x, w_in, nsa_k_pe, nsa_k_w1, nsa_k_w2, nsa_v_pe, nsa_v_w1, nsa_v_w2, swa_sinks, w_br_nsa, w_br_swa, w_out, ln1_g, ln1_b, w_router, b_router, w_expert_in, b_expert_in, w_expert_out, b_expert_out, ln2_g, ln2_b):
    depth = w_in.shape[0]
    alpha = (2.0 * depth) ** 0.25
    for l in range(depth):
        x = _layer(x, w_in[l], nsa_k_pe[l], nsa_k_w1[l], nsa_k_w2[l], nsa_v_pe[l], nsa_v_w1[l],
                   nsa_v_w2[l], swa_sinks[l], w_br_nsa[l], w_br_swa[l], w_out[l], ln1_g[l],
                   ln1_b[l], w_router[l], b_router[l], w_expert_in[l], b_expert_in[l],
                   w_expert_out[l], b_expert_out[l], ln2_g[l], ln2_b[l], alpha)
    return x
```

```python
import functools

import jax
import jax.numpy as jnp
from jax import lax
from jax.experimental import pallas as pl
from jax.experimental.pallas import tpu as pltpu

BF16 = jnp.bfloat16
F32 = jnp.float32
I32 = jnp.int32

HEAD_DIM = 64
NSA_HEADS = 8
NSA_KV = 2
CMP_BLOCK = 32
CMP_STRIDE = 16
SEL_BLOCK = 64
SEL_TOPN = 16
NSA_WINDOW = 512
SWA_HEADS = 8
SWA_KV = 2
SWA_WINDOW = 128
Q_BLOCK = 128
ROPE_THETA = 10000.0
N_EXPERTS = 32
TOP_K = 4
SWIGLU_LIMIT = 7.0
SWIGLU_ALPHA = 1.702
LN_EPS = 1e-5

LANES = 128
MASKED = -1e30
M_INIT = -1e29
SEL_PENALTY = -(2.0 ** 100)
VMEM_LIMIT = 52 * 1024 * 1024
MOE_ROWS = 256
BAND_STEP_QUERIES = 512

R_NSA = NSA_HEADS // NSA_KV
R_SWA = SWA_HEADS // SWA_KV
NT_DIMS = (((1,), (1,)), ((), ()))


def _params(*sem):
    return pltpu.CompilerParams(dimension_semantics=sem, vmem_limit_bytes=VMEM_LIMIT)


def _full(shape):
    n = len(shape)
    return pl.BlockSpec(shape, lambda *_: (0,) * n)


def _proj_kernel(x_ref, wr_ref, wp_ref, wg_ref, cos_ref, sin_ref,
                 qn_rot_ref, qs_rot_ref, kk_rot_ref, qn_raw_ref, plain_ref, gates_ref):
    xb = x_ref[...].astype(BF16)
    acc = jnp.dot(xb, wr_ref[...], preferred_element_type=F32)
    cos = cos_ref[...]
    sin = sin_ref[...]
    lane = lax.broadcasted_iota(I32, cos.shape, 1)
    first_half = (lane & (HEAD_DIM - 1)) < HEAD_DIM // 2

    def rope(t):
        partner = jnp.where(first_half, pltpu.roll(t, LANES - HEAD_DIM // 2, 1),
                            pltpu.roll(t, HEAD_DIM // 2, 1))
        return (t * cos + partner * sin).astype(BF16)

    nq = qn_rot_ref.shape[1] // LANES
    ns = qs_rot_ref.shape[1] // LANES
    nk = kk_rot_ref.shape[1] // LANES
    for c in range(nq):
        qn_rot_ref[:, c * LANES:(c + 1) * LANES] = rope(acc[:, c * LANES:(c + 1) * LANES])
    for c in range(ns):
        o = (nq + c) * LANES
        qs_rot_ref[:, c * LANES:(c + 1) * LANES] = rope(acc[:, o:o + LANES])
    for c in range(nk):
        o = (nq + ns + c) * LANES
        kk_rot_ref[:, c * LANES:(c + 1) * LANES] = rope(acc[:, o:o + LANES])
    qn_raw_ref[...] = acc[:, :nq * LANES].astype(BF16)
    plain_ref[...] = jnp.dot(xb, wp_ref[...], preferred_element_type=F32).astype(BF16)
    gates_ref[...] = jnp.dot(xb, wg_ref[...], preferred_element_type=F32)


def _project(x2, w_rope, w_plain, w_gate, cos_t, sin_t, seq, tm):
    t, d = x2.shape
    nr, npl, ng = w_rope.shape[1], w_plain.shape[1], w_gate.shape[1]
    nqn, nqs = NSA_HEADS * HEAD_DIM, SWA_HEADS * HEAD_DIM
    nkk = nr - nqn - nqs
    spb = seq // tm
    row = lambda i: (i, 0)
    return pl.pallas_call(
        _proj_kernel,
        grid=(t // tm,),
        in_specs=[pl.BlockSpec((tm, d), row), _full(w_rope.shape), _full(w_plain.shape),
                  _full(w_gate.shape),
                  pl.BlockSpec((tm, LANES), lambda i: (i % spb, 0)),
                  pl.BlockSpec((tm, LANES), lambda i: (i % spb, 0))],
        out_specs=[pl.BlockSpec((tm, nqn), row), pl.BlockSpec((tm, nqs), row),
                   pl.BlockSpec((tm, nkk), row), pl.BlockSpec((tm, nqn), row),
                   pl.BlockSpec((tm, npl), row), pl.BlockSpec((tm, ng), row)],
        out_shape=[jax.ShapeDtypeStruct((t, nqn), BF16), jax.ShapeDtypeStruct((t, nqs), BF16),
                   jax.ShapeDtypeStruct((t, nkk), BF16), jax.ShapeDtypeStruct((t, nqn), BF16),
                   jax.ShapeDtypeStruct((t, npl), BF16), jax.ShapeDtypeStruct((t, ng), F32)],
        compiler_params=_params("parallel"),
        name="proj",
    )(x2, w_rope, w_plain, w_gate, cos_t, sin_t)


def _compress_kernel(a_ref, b_ref, pe_ref, w1_ref, w2_ref, out_ref):
    half = a_ref.shape[1]
    a = (a_ref[...].astype(F32) + pe_ref[0:1, :]).astype(BF16)
    b = (b_ref[...].astype(F32) + pe_ref[1:2, :]).astype(BF16)
    hid = jnp.dot(a, w1_ref[0:half, :], preferred_element_type=F32)
    hid = hid + jnp.dot(b, w1_ref[half:2 * half, :], preferred_element_type=F32)
    act = jax.nn.gelu(hid).astype(BF16)
    out_ref[...] = jnp.dot(act, w2_ref[...], preferred_element_type=F32).astype(BF16)


def _compress(t_lo, t_hi, pe2, w1, w2):
    two, bg, ncp, half = t_lo.shape
    hid = w1.shape[2]
    blk = lambda shape: pl.BlockSpec((None, None) + shape, lambda j, i: (j, i, 0, 0))
    wsp = lambda shape: pl.BlockSpec((None,) + shape, lambda j, i: (j, 0, 0))
    return pl.pallas_call(
        _compress_kernel,
        grid=(two, bg),
        in_specs=[blk((ncp, half)), blk((ncp, half)), wsp((2, half)), wsp((2 * half, hid)),
                  wsp((hid, HEAD_DIM))],
        out_specs=blk((ncp, HEAD_DIM)),
        out_shape=jax.ShapeDtypeStruct((two, bg, ncp, HEAD_DIM), BF16),
        compiler_params=_params("parallel", "parallel"),
        name="compress",
    )(t_lo, t_hi, pe2, w1, w2)


def _stack_heads(q, g, r):
    return jnp.concatenate(
        [q[:, (g * r + j) * HEAD_DIM:(g * r + j + 1) * HEAD_DIM] for j in range(r)], axis=0)


def _unstack_heads(parts, r, qb):
    return jnp.concatenate([o[j * qb:(j + 1) * qb] for o in parts for j in range(r)], axis=1)


def _topk_mask_cols(vals, k):
    n = vals.shape[0]
    row = lax.broadcasted_iota(I32, vals.shape, 0).astype(F32)
    taken = jnp.zeros(vals.shape, F32)
    work = vals
    for _ in range(k):
        mx = jnp.max(work, axis=0, keepdims=True)
        first = jnp.min(jnp.where(work == mx, row, float(n)), axis=0, keepdims=True)
        pick = row == first
        taken = jnp.where(pick, 1.0, taken)
        work = jnp.where(pick, -jnp.inf, work)
    return taken > 0.5


def _cmp_kernel(q_ref, kc_ref, vc_ref, ovt_ref, o_ref, notsel_ref):
    qb = q_ref.shape[0]
    ncp = kc_ref.shape[1]
    nselp = ovt_ref.shape[0]
    i = pl.program_id(1)
    rows = R_NSA * qb
    pos = i * qb + (lax.broadcasted_iota(I32, (rows, 1), 0) & (qb - 1))
    cend = lax.broadcasted_iota(I32, (1, ncp), 1) * CMP_STRIDE + (CMP_BLOCK - 1)
    visible = cend <= pos
    live = (pos >= CMP_BLOCK - 1).astype(F32)
    q = q_ref[...]
    outs, imps = [], []
    for g in range(NSA_KV):
        qg = _stack_heads(q, g, R_NSA) * (HEAD_DIM ** -0.5)
        s = lax.dot_general(qg, kc_ref[g], NT_DIMS, preferred_element_type=F32)
        s = jnp.where(visible, s, MASKED)
        e = jnp.exp(s - jnp.max(s, axis=1, keepdims=True))
        p = e / jnp.sum(e, axis=1, keepdims=True) * live
        outs.append(jnp.dot(p.astype(BF16), vc_ref[g], preferred_element_type=F32))
        psum = p[0:qb]
        for j in range(1, R_NSA):
            psum = psum + p[j * qb:(j + 1) * qb]
        p_hi = psum.astype(BF16)
        p_lo = (psum - p_hi.astype(F32)).astype(BF16)
        imps.append(lax.dot_general(ovt_ref[...], p_hi, NT_DIMS, preferred_element_type=F32)
                    + lax.dot_general(ovt_ref[...], p_lo, NT_DIMS, preferred_element_type=F32))
    imp = jnp.concatenate(imps, axis=1)
    lane = lax.broadcasted_iota(I32, (1, NSA_KV * qb), 1)
    cur = (i * qb + (lane & (qb - 1))) >> 6
    jb = lax.broadcasted_iota(I32, (nselp, 1), 0)
    forced = (jb == 0) | (jb == cur) | (jb == cur - 1)
    imp = jnp.where(jb > cur, -1.0, jnp.where(forced, 1e6, imp))
    notsel = jnp.where(_topk_mask_cols(imp, SEL_TOPN), 0.0, 1.0)
    for g in range(NSA_KV):
        notsel_ref[g] = notsel[:, g * qb:(g + 1) * qb].T.astype(BF16)
    o_ref[...] = _unstack_heads(outs, R_NSA, qb)


def _cmp_attention(q_raw, kc, vc, overlap_t, qb):
    b, s, hq = q_raw.shape
    _, g, ncp, dh = kc.shape
    nselp = overlap_t.shape[0]
    return pl.pallas_call(
        _cmp_kernel,
        grid=(b, s // qb),
        in_specs=[pl.BlockSpec((None, qb, hq), lambda bi, i: (bi, i, 0)),
                  pl.BlockSpec((None, g, ncp, dh), lambda bi, i: (bi, 0, 0, 0)),
                  pl.BlockSpec((None, g, ncp, dh), lambda bi, i: (bi, 0, 0, 0)),
                  _full(overlap_t.shape)],
        out_specs=[pl.BlockSpec((None, qb, hq), lambda bi, i: (bi, i, 0)),
                   pl.BlockSpec((None, g, qb, nselp), lambda bi, i: (bi, 0, i, 0))],
        out_shape=[jax.ShapeDtypeStruct((b, s, hq), F32),
                   jax.ShapeDtypeStruct((b, g, s, nselp), BF16)],
        compiler_params=_params("parallel", "parallel"),
        name="cmp_attn",
    )(q_raw, kc, vc, overlap_t)


def _sel_kernel(q_ref, notsel_ref, k_ref, v_ref, o_ref, *, tk):
    qb = q_ref.shape[0]
    i = pl.program_id(1)
    rows = R_NSA * qb
    qpos = i * qb + (lax.broadcasted_iota(I32, (rows, 1), 0) & (qb - 1))
    n_clear = (i * qb) // tk
    q = q_ref[...]
    outs = []
    for g in range(NSA_KV):
        qg = _stack_heads(q, g, R_NSA) * (HEAD_DIM ** -0.5)
        ns = notsel_ref[g]
        q_aug = jnp.concatenate(
            [qg, jnp.zeros((rows, LANES - HEAD_DIM), BF16),
             jnp.concatenate([ns] * R_NSA, axis=0)], axis=1)

        def step(kt, carry, causal, g=g, q_aug=q_aug):
            m, acc = carry
            start = pl.multiple_of(kt * tk, tk)
            k_t = k_ref[g, pl.ds(start, tk), :]
            v_t = v_ref[g, pl.ds(start, tk), :]
            s = lax.dot_general(q_aug, k_t, NT_DIMS, preferred_element_type=F32)
            if causal:
                kpos = start + lax.broadcasted_iota(I32, (1, tk), 1)
                s = jnp.where(kpos <= qpos, s, MASKED)
            m_new = jnp.maximum(m, jnp.max(s, axis=1, keepdims=True))
            p = jnp.exp(s - m_new)
            acc = jnp.exp(m - m_new) * acc + jnp.dot(p.astype(BF16), v_t,
                                                    preferred_element_type=F32)
            return m_new, acc

        init = (jnp.full((rows, 1), M_INIT, F32), jnp.zeros((rows, LANES), F32))
        carry = lax.fori_loop(0, n_clear, functools.partial(step, causal=False), init)
        _, acc = step(n_clear, carry, True)
        outs.append(acc[:, :HEAD_DIM] / acc[:, HEAD_DIM:HEAD_DIM + 1])
    o_ref[...] = _unstack_heads(outs, R_NSA, qb)


def _sel_attention(q_rot, notsel, k_aug, v_aug, qb, tk):
    b, s, hq = q_rot.shape
    _, g, _, kw = k_aug.shape
    nselp = notsel.shape[3]
    return pl.pallas_call(
        functools.partial(_sel_kernel, tk=tk),
        grid=(b, s // qb),
        in_specs=[pl.BlockSpec((None, qb, hq), lambda bi, i: (bi, i, 0)),
                  pl.BlockSpec((None, g, qb, nselp), lambda bi, i: (bi, 0, i, 0)),
                  pl.BlockSpec((None, g, s, kw), lambda bi, i: (bi, 0, 0, 0)),
                  pl.BlockSpec((None, g, s, LANES), lambda bi, i: (bi, 0, 0, 0))],
        out_specs=pl.BlockSpec((None, qb, hq), lambda bi, i: (bi, i, 0)),
        out_shape=jax.ShapeDtypeStruct((b, s, hq), F32),
        compiler_params=_params("parallel", "parallel"),
        name="sel_attn",
    )(q_rot, notsel, k_aug, v_aug)


def _band_kernel(*refs, window, wlen, r, kv, qb, has_sinks):
    if has_sinks:
        sink_ref, q_ref, k_ref, v_ref, o_ref = refs
    else:
        q_ref, k_ref, v_ref, o_ref = refs
    nsub = q_ref.shape[0] // qb
    rows = r * qb
    for sb in range(nsub):
        i = pl.program_id(1) * nsub + sb
        start = pl.multiple_of(jnp.maximum((i + 1) * qb - wlen, 0), qb)
        qpos = i * qb + (lax.broadcasted_iota(I32, (rows, 1), 0) & (qb - 1))
        kpos = start + lax.broadcasted_iota(I32, (1, wlen), 1)
        rel = qpos - kpos
        inside = (rel >= 0) & (rel < window)
        q = q_ref[sb * qb:(sb + 1) * qb, :]
        kw = k_ref[pl.ds(start, wlen), :]
        vw = v_ref[pl.ds(start, wlen), :]
        outs = []
        for g in range(kv):
            qg = _stack_heads(q, g, r) * (HEAD_DIM ** -0.5)
            s = lax.dot_general(qg, kw[:, g * HEAD_DIM:(g + 1) * HEAD_DIM], NT_DIMS,
                                preferred_element_type=F32)
            s = jnp.where(inside, s, MASKED)
            m = jnp.max(s, axis=1, keepdims=True)
            if has_sinks:
                sk = jnp.concatenate(
                    [jnp.full((qb, 1), sink_ref[g * r + j], F32) for j in range(r)], axis=0)
                m = jnp.maximum(m, sk)
            e = jnp.exp(s - m)
            den = jnp.sum(e, axis=1, keepdims=True)
            if has_sinks:
                den = den + jnp.exp(sk - m)
            p = (e / den).astype(BF16)
            outs.append(jnp.dot(p, vw[:, g * HEAD_DIM:(g + 1) * HEAD_DIM],
                                preferred_element_type=F32))
        o_ref[sb * qb:(sb + 1) * qb, :] = _unstack_heads(outs, r, qb)


def _band_attention(q_rot, k, v, window, sinks, qb):
    b, s, hq = q_rot.shape
    gk = k.shape[2]
    kv = gk // HEAD_DIM
    r = hq // gk
    back = -(-window // qb)
    wlen = (back + 1) * qb
    assert wlen <= s
    has_sinks = sinks is not None
    tq = min(BAND_STEP_QUERIES, s)
    in_specs = [pl.BlockSpec((None, tq, hq), lambda bi, i: (bi, i, 0)),
                pl.BlockSpec((None, s, gk), lambda bi, i: (bi, 0, 0)),
                pl.BlockSpec((None, s, gk), lambda bi, i: (bi, 0, 0))]
    args = [q_rot, k, v]
    if has_sinks:
        in_specs = [pl.BlockSpec(memory_space=pltpu.SMEM)] + in_specs
        args = [sinks.astype(F32)] + args
    return pl.pallas_call(
        functools.partial(_band_kernel, window=window, wlen=wlen, r=r, kv=kv, qb=qb,
                          has_sinks=has_sinks),
        grid=(b, s // tq),
        in_specs=in_specs,
        out_specs=pl.BlockSpec((None, tq, hq), lambda bi, i: (bi, i, 0)),
        out_shape=jax.ShapeDtypeStruct((b, s, hq), F32),
        compiler_params=_params("parallel", "parallel"),
        name="band_attn_sink" if has_sinks else "band_attn",
    )(*args)


def _layer_norm(v, g, b):
    mu = jnp.mean(v, axis=1, keepdims=True)
    c = v - mu
    var = jnp.mean(c * c, axis=1, keepdims=True)
    return c * lax.rsqrt(var + LN_EPS) * g + b


def _split_dot(a, w):
    hi = a.astype(BF16)
    lo = (a - hi.astype(F32)).astype(BF16)
    return (jnp.dot(hi, w, preferred_element_type=F32)
            + jnp.dot(lo, w, preferred_element_type=F32))


def _merge_kernel(ocmp_ref, osel_ref, owin_ref, oswa_ref, gates_ref, x_ref, exp_ref,
                  wbn_ref, wbs_ref, wo_ref, lng_ref, lnb_ref, wrh_ref, wrl_ref, br_ref,
                  h_ref, te_ref, tg_ref, tr_ref, counts_ref, cnt_ref, *, alpha):
    d = x_ref.shape[1]
    hq = ocmp_ref.shape[1]
    gates = gates_ref[...]
    gn = jax.nn.sigmoid(gates[:, 2 * d:])
    gexp = _split_dot(gn, exp_ref[...])
    o_nsa = (gexp[:, 0:hq] * ocmp_ref[...] + gexp[:, hq:2 * hq] * osel_ref[...]
             + gexp[:, 2 * hq:3 * hq] * owin_ref[...])
    y_nsa = jnp.dot(o_nsa.astype(BF16), wbn_ref[...], preferred_element_type=F32)
    y_swa = jnp.dot(oswa_ref[...].astype(BF16), wbs_ref[...], preferred_element_type=F32)
    gm = jax.nn.sigmoid(gates[:, :2 * d])
    mixed = gm[:, :d] * y_nsa + gm[:, d:] * y_swa
    z = jnp.dot(mixed.astype(BF16), wo_ref[...], preferred_element_type=F32)
    h = _layer_norm(alpha * x_ref[...] + z, lng_ref[...], lnb_ref[...])
    h_ref[...] = h
    h_hi = h.astype(BF16)
    h_lo = (h - h_hi.astype(F32)).astype(BF16)
    logits = (jnp.dot(h_hi, wrh_ref[...], preferred_element_type=F32)
              + jnp.dot(h_lo, wrh_ref[...], preferred_element_type=F32)
              + jnp.dot(h_hi, wrl_ref[...], preferred_element_type=F32)) + br_ref[...]
    col = lax.broadcasted_iota(I32, logits.shape, 1).astype(F32)
    work = logits
    vals, ids = [], []
    for _ in range(TOP_K):
        mx = jnp.max(work, axis=1, keepdims=True)
        first = jnp.min(jnp.where(work == mx, col, float(LANES)), axis=1, keepdims=True)
        vals.append(mx)
        ids.append(first)
        work = jnp.where(col == first, -jnp.inf, work)
    es = [jnp.exp(v - vals[0]) for v in vals]
    den = es[0]
    for e in es[1:]:
        den = den + e
    @pl.when(pl.program_id(0) == 0)
    def _():
        cnt_ref[...] = jnp.zeros(cnt_ref.shape, F32)

    tm = logits.shape[0]
    hits = jnp.zeros(logits.shape, F32)
    for k in range(TOP_K):
        hits = jnp.where(col == ids[k], 1.0, hits)
    earlier = (lax.broadcasted_iota(I32, (tm, tm), 1)
               < lax.broadcasted_iota(I32, (tm, tm), 0))
    before = jnp.dot(jnp.where(earlier, 1.0, 0.0).astype(BF16), hits.astype(BF16),
                     preferred_element_type=F32) + cnt_ref[...]
    cnt_ref[...] = cnt_ref[...] + jnp.sum(hits, axis=0, keepdims=True)
    counts_ref[...] = cnt_ref[...]
    te = jnp.zeros(logits.shape, F32)
    tg = jnp.zeros(logits.shape, F32)
    tr = jnp.zeros(logits.shape, F32)
    for k in range(TOP_K):
        rank = jnp.sum(jnp.where(col == ids[k], before, 0.0), axis=1, keepdims=True)
        te = jnp.where(col == float(k), ids[k], te)
        tg = jnp.where(col == float(k), es[k] / den, tg)
        tr = jnp.where(col == float(k), rank, tr)
    te_ref[...] = te.astype(I32)
    tg_ref[...] = tg
    tr_ref[...] = tr.astype(I32)


def _merge(o_cmp, o_sel, o_win, o_swa, gates, x2, expand, w_bn, w_bs, w_o, ln_g, ln_b,
           wr_hi, wr_lo, b_r, alpha, tm):
    t, d = x2.shape
    hq = o_cmp.shape[1]
    row = lambda i: (i, 0)
    tok = lambda n: pl.BlockSpec((tm, n), row)
    return pl.pallas_call(
        functools.partial(_merge_kernel, alpha=alpha),
        grid=(t // tm,),
        in_specs=[tok(hq), tok(hq), tok(hq), tok(hq), tok(gates.shape[1]), tok(d),
                  _full(expand.shape), _full(w_bn.shape), _full(w_bs.shape), _full(w_o.shape),
                  _full(ln_g.shape), _full(ln_b.shape), _full(wr_hi.shape), _full(wr_lo.shape),
                  _full(b_r.shape)],
        out_specs=[tok(d), tok(LANES), tok(LANES), tok(LANES), _full((1, LANES))],
        out_shape=[jax.ShapeDtypeStruct((t, d), F32), jax.ShapeDtypeStruct((t, LANES), I32),
                   jax.ShapeDtypeStruct((t, LANES), F32), jax.ShapeDtypeStruct((t, LANES), I32),
                   jax.ShapeDtypeStruct((1, LANES), F32)],
        scratch_shapes=[pltpu.VMEM((1, LANES), F32)],
        compiler_params=_params("arbitrary"),
        name="merge_ln_router",
    )(o_cmp, o_sel, o_win, o_swa, gates, x2, expand, w_bn, w_bs, w_o, ln_g, ln_b,
      wr_hi, wr_lo, b_r)


def _dispatch_kernel(dest_ref, h_ref, xs_init_ref, xs_ref, sem):
    del xs_init_ref
    tm = h_ref.shape[0]

    def issue(r, carry):
        for k in range(TOP_K):
            dst = dest_ref[r * TOP_K + k]
            pltpu.make_async_copy(h_ref.at[pl.ds(r, 1), :], xs_ref.at[pl.ds(dst, 1), :],
                                  sem).start()
        return carry

    lax.fori_loop(0, tm, issue, 0, unroll=8)
    for _ in range(TOP_K):
        pltpu.make_async_copy(h_ref, xs_ref.at[pl.ds(0, tm), :], sem).wait()


def _dispatch(dest, h, n_rows, tm):
    t, d = h.shape
    return pl.pallas_call(
        _dispatch_kernel,
        grid=(t // tm,),
        in_specs=[pl.BlockSpec((tm * TOP_K,), lambda i: (i,), memory_space=pltpu.SMEM),
                  pl.BlockSpec((tm, d), lambda i: (i, 0)),
                  pl.BlockSpec(memory_space=pl.ANY)],
        out_specs=pl.BlockSpec(memory_space=pl.ANY),
        out_shape=jax.ShapeDtypeStruct((n_rows, d), F32),
        scratch_shapes=[pltpu.SemaphoreType.DMA(())],
        input_output_aliases={2: 0},
        compiler_params=_params("arbitrary"),
        name="moe_dispatch",
    )(dest, h, jnp.zeros((n_rows, d), F32))


def _moe_kernel(blk_e_ref, n_used_ref, x_ref, wi_ref, bi_ref, wo_ref, bo_ref, y_ref,
                wi_bf, wo_bf):
    f = wo_ref.shape[0]
    b = pl.program_id(0)

    @pl.when((b == 0) | (blk_e_ref[b] != blk_e_ref[jnp.maximum(b - 1, 0)]))
    def _():
        wi_bf[...] = wi_ref[...].astype(BF16)
        wo_bf[...] = wo_ref[...].astype(BF16)

    @pl.when(b < n_used_ref[0])
    def _():
        hdn = jnp.dot(x_ref[...].astype(BF16), wi_bf[...],
                      preferred_element_type=F32) + bi_ref[...]
        hg = jnp.minimum(hdn[:, :f], SWIGLU_LIMIT)
        hu = jnp.clip(hdn[:, f:], -SWIGLU_LIMIT, SWIGLU_LIMIT)
        act = hg * jax.nn.sigmoid(SWIGLU_ALPHA * hg) * (hu + 1.0)
        y_ref[...] = jnp.dot(act.astype(BF16), wo_bf[...],
                             preferred_element_type=F32) + bo_ref[...]

    @pl.when(b >= n_used_ref[0])
    def _():
        y_ref[...] = jnp.zeros(y_ref.shape, F32)


def _moe_blocks(blk_e, n_used, xs, w_in, b_in, w_out, b_out):
    n_rows, d = xs.shape
    e, _, f2 = w_in.shape
    f = w_out.shape[1]
    n_blk = n_rows // MOE_ROWS
    grid_spec = pltpu.PrefetchScalarGridSpec(
        num_scalar_prefetch=2,
        grid=(n_blk,),
        in_specs=[pl.BlockSpec((MOE_ROWS, d), lambda b, be, nu: (jnp.minimum(b, nu[0] - 1), 0)),
                  pl.BlockSpec((None, d, f2), lambda b, be, nu: (be[b], 0, 0)),
                  pl.BlockSpec((None, 1, f2), lambda b, be, nu: (be[b], 0, 0)),
                  pl.BlockSpec((None, f, d), lambda b, be, nu: (be[b], 0, 0)),
                  pl.BlockSpec((None, 1, d), lambda b, be, nu: (be[b], 0, 0))],
        out_specs=pl.BlockSpec((MOE_ROWS, d), lambda b, be, nu: (b, 0)),
        scratch_shapes=[pltpu.VMEM((d, f2), BF16), pltpu.VMEM((f, d), BF16)],
    )
    return pl.pallas_call(
        _moe_kernel,
        grid_spec=grid_spec,
        out_shape=jax.ShapeDtypeStruct((n_rows, d), F32),
        compiler_params=_params("arbitrary"),
        name="moe_experts",
    )(blk_e, n_used, xs, w_in, b_in, w_out, b_out)


def _final_kernel(dest_ref, h_ref, tg_ref, ys_ref, g_ref, b_ref, o_ref, ybuf, sem, *, alpha):
    tm = h_ref.shape[0]

    def issue(r, carry):
        for k in range(TOP_K):
            src = dest_ref[r * TOP_K + k]
            pltpu.make_async_copy(ys_ref.at[pl.ds(src, 1), :], ybuf.at[k, pl.ds(r, 1), :],
                                  sem).start()
        return carry

    lax.fori_loop(0, tm, issue, 0, unroll=8)
    for k in range(TOP_K):
        pltpu.make_async_copy(ys_ref.at[pl.ds(0, tm), :], ybuf.at[k], sem).wait()
    tg = tg_ref[...]
    f = tg[:, 0:1] * ybuf[0]
    for k in range(1, TOP_K):
        f = f + tg[:, k:k + 1] * ybuf[k]
    o_ref[...] = _layer_norm(alpha * h_ref[...] + f, g_ref[...], b_ref[...])


def _final(dest, h, top_g, ys, ln_g, ln_b, alpha, tm):
    t, d = h.shape
    return pl.pallas_call(
        functools.partial(_final_kernel, alpha=alpha),
        grid=(t // tm,),
        in_specs=[pl.BlockSpec((tm * TOP_K,), lambda i: (i,), memory_space=pltpu.SMEM),
                  pl.BlockSpec((tm, d), lambda i: (i, 0)),
                  pl.BlockSpec((tm, LANES), lambda i: (i, 0)),
                  pl.BlockSpec(memory_space=pl.ANY),
                  _full(ln_g.shape), _full(ln_b.shape)],
        out_specs=pl.BlockSpec((tm, d), lambda i: (i, 0)),
        out_shape=jax.ShapeDtypeStruct((t, d), F32),
        scratch_shapes=[pltpu.VMEM((TOP_K, tm, d), F32), pltpu.SemaphoreType.DMA(())],
        compiler_params=_params("arbitrary"),
        name="combine_ln",
    )(dest, h, top_g, ys, ln_g, ln_b)


def _rope_tables(s):
    half = HEAD_DIM // 2
    inv = ROPE_THETA ** (-jnp.arange(half, dtype=F32) / half)
    ang = jnp.arange(s, dtype=F32)[:, None] * inv[None, :]
    cos, sin = jnp.cos(ang), jnp.sin(ang)
    reps = LANES // HEAD_DIM
    cos_t = jnp.tile(jnp.concatenate([cos, cos], axis=1), (1, reps))
    sin_t = jnp.tile(jnp.concatenate([-sin, sin], axis=1), (1, reps))
    return cos_t, sin_t


def _moe_plan(top_e, rank, counts, t):
    a = t * TOP_K
    padded = (counts + MOE_ROWS - 1) // MOE_ROWS * MOE_ROWS
    pend = jnp.cumsum(padded)
    pstart = pend - padded
    experts = jnp.arange(N_EXPERTS, dtype=I32)
    base = jnp.sum(jnp.where(top_e[:, :, None] == experts, pstart, 0), axis=2)
    dest = (base + rank).astype(I32).reshape(a)
    n_blk = -(-a // MOE_ROWS) + N_EXPERTS
    blk_first = jnp.arange(n_blk, dtype=I32) * MOE_ROWS
    blk_e = jnp.minimum(jnp.sum(pend[None, :] <= blk_first[:, None], axis=1),
                        N_EXPERTS - 1).astype(I32)
    n_used = (pend[-1] // MOE_ROWS).astype(I32).reshape(1)
    return dest, blk_e, n_used, n_blk * MOE_ROWS


def _layer(x, w_in, k_pe, k_w1, k_w2, v_pe, v_w1, v_w2, sinks, w_br_nsa, w_br_swa, w_out,
           ln1_g, ln1_b, w_router, b_router, w_e_in, b_e_in, w_e_out, b_e_out, ln2_g, ln2_b,
           alpha):
    b, s, d = x.shape
    t = b * s
    qb = Q_BLOCK
    nq_n, nkv = NSA_HEADS * HEAD_DIM, NSA_KV * HEAD_DIM
    nq_s, nkv_s = SWA_HEADS * HEAD_DIM, SWA_KV * HEAD_DIM
    widths = (nq_n, nkv, nkv, nkv, nkv, nkv, nkv, NSA_HEADS * 3, nq_s, nkv_s, nkv_s, 2 * d)
    offs = [0]
    for w in widths:
        offs.append(offs[-1] + w)
    col = lambda j: w_in[:, offs[j]:offs[j + 1]]
    (c_qn, c_kc, c_vc, c_ks, c_vs, c_kw, c_vw, c_gn, c_qs, c_k_s, c_v_s, c_gm) = map(col, range(12))
    w_rope = jnp.concatenate([c_qn, c_qs, c_ks, c_kw, c_k_s], axis=1).astype(BF16)
    w_plain = jnp.concatenate([c_kc, c_vc, c_vs, c_vw, c_v_s], axis=1).astype(BF16)
    gn_pad = LANES - NSA_HEADS * 3
    w_gate = jnp.concatenate([c_gm, c_gn, jnp.zeros((d, gn_pad), F32)], axis=1).astype(BF16)
    cos_t, sin_t = _rope_tables(s)

    x2 = x.reshape(t, d)
    qn_rot, qs_rot, kk_rot, qn_raw, plain, gates = _project(
        x2, w_rope, w_plain, w_gate, cos_t, sin_t, s, min(256, s))

    nc = (s - CMP_BLOCK) // CMP_STRIDE + 1
    ncp = s // CMP_STRIDE
    half = CMP_STRIDE * HEAD_DIM

    def halves(cols):
        v = cols.reshape(b, s, NSA_KV, HEAD_DIM).transpose(0, 2, 1, 3)
        return v.reshape(b * NSA_KV, ncp, half)

    t2 = jnp.stack([halves(plain[:, 0:nkv]), halves(plain[:, nkv:2 * nkv])])
    t_lo = t2
    t_hi = jnp.concatenate([t2[:, :, 1:], jnp.zeros_like(t2[:, :, :1])], axis=2)
    pe2 = jnp.stack([k_pe.reshape(2, half), v_pe.reshape(2, half)])
    w1 = jnp.stack([k_w1, v_w1]).astype(BF16)
    w2 = jnp.stack([k_w2, v_w2]).astype(BF16)
    kvc = _compress(t_lo, t_hi, pe2, w1, w2).reshape(2, b, NSA_KV, ncp, HEAD_DIM)

    nsel = s // SEL_BLOCK
    nselp = -(-nsel // LANES) * LANES
    cstart = jnp.arange(ncp) * CMP_STRIDE
    sstart = jnp.arange(nselp) * SEL_BLOCK
    overlap = ((cstart[:, None] < sstart[None, :] + SEL_BLOCK)
               & (cstart[:, None] + CMP_BLOCK > sstart[None, :])
               & (jnp.arange(ncp)[:, None] < nc) & (jnp.arange(nselp)[None, :] < nsel))
    o_cmp, notsel = _cmp_attention(qn_raw.reshape(b, s, nq_n), kvc[0], kvc[1],
                                   overlap.T.astype(BF16), qb)

    def group_major(cols):
        return cols.reshape(b, s, NSA_KV, HEAD_DIM).transpose(0, 2, 1, 3)

    k_sel = group_major(kk_rot[:, 0:nkv])
    v_sel = group_major(plain[:, 2 * nkv:3 * nkv])
    onehot = (jnp.arange(s)[:, None] // SEL_BLOCK == jnp.arange(nselp)[None, :])
    k_tail = jnp.concatenate([jnp.zeros((s, LANES - HEAD_DIM), F32),
                              jnp.where(onehot, SEL_PENALTY, 0.0)], axis=1).astype(BF16)
    k_aug = jnp.concatenate(
        [k_sel, jnp.broadcast_to(k_tail, (b, NSA_KV) + k_tail.shape)], axis=3)
    v_tail = jnp.zeros((LANES - HEAD_DIM,), BF16).at[0].set(1.0)
    v_aug = jnp.concatenate(
        [v_sel, jnp.broadcast_to(v_tail, (b, NSA_KV, s, LANES - HEAD_DIM))], axis=3)
    o_sel = _sel_attention(qn_rot.reshape(b, s, nq_n), notsel, k_aug, v_aug, qb, min(512, s))

    o_win = _band_attention(qn_rot.reshape(b, s, nq_n), kk_rot[:, nkv:2 * nkv].reshape(b, s, nkv),
                            plain[:, 3 * nkv:4 * nkv].reshape(b, s, nkv), NSA_WINDOW, None, qb)
    o_swa = _band_attention(qs_rot.reshape(b, s, nq_s),
                            kk_rot[:, 2 * nkv:2 * nkv + nkv_s].reshape(b, s, nkv_s),
                            plain[:, 4 * nkv:4 * nkv + nkv_s].reshape(b, s, nkv_s),
                            SWA_WINDOW, sinks, qb)

    gi = jnp.arange(LANES)
    ci = jnp.arange(3 * nq_n)
    expand = ((gi[:, None] // 3 == (ci[None, :] % nq_n) // HEAD_DIM)
              & (gi[:, None] % 3 == ci[None, :] // nq_n)
              & (gi[:, None] < NSA_HEADS * 3)).astype(BF16)
    wr_pad = jnp.pad(w_router, ((0, 0), (0, LANES - N_EXPERTS)))
    wr_hi = wr_pad.astype(BF16)
    wr_lo = (wr_pad - wr_hi.astype(F32)).astype(BF16)
    b_r = jnp.concatenate([b_router, jnp.full((LANES - N_EXPERTS,), -jnp.inf, F32)]).reshape(1, LANES)
    tm = min(256, t)
    h, top_e, top_g, rank, counts = _merge(
        o_cmp.reshape(t, nq_n), o_sel.reshape(t, nq_n), o_win.reshape(t, nq_n),
        o_swa.reshape(t, nq_s), gates, x2, expand, w_br_nsa.astype(BF16), w_br_swa.astype(BF16),
        w_out.astype(BF16), ln1_g.reshape(1, d), ln1_b.reshape(1, d), wr_hi, wr_lo, b_r,
        alpha, tm)

    dest, blk_e, n_used, n_rows = _moe_plan(
        top_e[:, :TOP_K], rank[:, :TOP_K], counts[0, :N_EXPERTS].astype(I32), t)
    xs = _dispatch(dest, h, n_rows, tm)
    ys = _moe_blocks(blk_e, n_used, xs, w_e_in, b_e_in.reshape(N_EXPERTS, 1, -1),
                     w_e_out, b_e_out.reshape(N_EXPERTS, 1, -1))
    out = _final(dest, h, top_g, ys, ln2_g.reshape(1, d), ln2_b.reshape(1, d), alpha, tm)
    return out.reshape(b, s, d)


def kernel(x, w_in, nsa_k_pe, nsa_k_w1, nsa_k_w2, nsa_v_pe, nsa_v_w1, nsa_v_w2, swa_sinks, w_br_nsa, w_br_swa, w_out, ln1_g, ln1_b, w_router, b_router, w_expert_in, b_expert_in, w_expert_out, b_expert_out, ln2_g, ln2_b):
    depth = w_in.shape[0]
    alpha = (2.0 * depth) ** 0.25
    for l in range(depth):
        x = _layer(x, w_in[l], nsa_k_pe[l], nsa_k_w1[l], nsa_k_w2[l], nsa_v_pe[l], nsa_v_w1[l],
                   nsa_v_w2[l], swa_sinks[l], w_br_nsa[l], w_br_swa[l], w_out[l], ln1_g[l],
                   ln1_b[l], w_router[l], b_router[l], w_expert_in[l], b_expert_in[l],
                   w_expert_out[l], b_expert_out[l], ln2_g[l], ln2_b[l], alpha)
    return x
```

```python
import functools

import jax
import jax.numpy as jnp
from jax import lax
from jax.experimental import pallas as pl
from jax.experimental.pallas import tpu as pltpu

BF16 = jnp.bfloat16
F32 = jnp.float32
I32 = jnp.int32

HEAD_DIM = 64
NSA_HEADS = 8
NSA_KV = 2
CMP_BLOCK = 32
CMP_STRIDE = 16
SEL_BLOCK = 64
SEL_TOPN = 16
NSA_WINDOW = 512
SWA_HEADS = 8
SWA_KV = 2
SWA_WINDOW = 128
Q_BLOCK = 128
ROPE_THETA = 10000.0
N_EXPERTS = 32
TOP_K = 4
SWIGLU_LIMIT = 7.0
SWIGLU_ALPHA = 1.702
LN_EPS = 1e-5

LANES = 128
MASKED = -1e30
M_INIT = -1e29
SEL_PENALTY = -(2.0 ** 100)
VMEM_LIMIT = 52 * 1024 * 1024
MOE_ROWS = 256
BAND_STEP_QUERIES = 512

R_NSA = NSA_HEADS // NSA_KV
R_SWA = SWA_HEADS // SWA_KV
NT_DIMS = (((1,), (1,)), ((), ()))


def _params(*sem):
    return pltpu.CompilerParams(dimension_semantics=sem, vmem_limit_bytes=VMEM_LIMIT)


def _full(shape):
    n = len(shape)
    return pl.BlockSpec(shape, lambda *_: (0,) * n)


def _proj_kernel(x_ref, wr_ref, wp_ref, wg_ref, cos_ref, sin_ref,
                 qn_rot_ref, qs_rot_ref, kk_rot_ref, qn_raw_ref, plain_ref, gates_ref):
    xb = x_ref[...].astype(BF16)
    acc = jnp.dot(xb, wr_ref[...], preferred_element_type=F32)
    cos = cos_ref[...]
    sin = sin_ref[...]
    lane = lax.broadcasted_iota(I32, cos.shape, 1)
    first_half = (lane & (HEAD_DIM - 1)) < HEAD_DIM // 2

    def rope(t):
        partner = jnp.where(first_half, pltpu.roll(t, LANES - HEAD_DIM // 2, 1),
                            pltpu.roll(t, HEAD_DIM // 2, 1))
        return (t * cos + partner * sin).astype(BF16)

    nq = qn_rot_ref.shape[1] // LANES
    ns = qs_rot_ref.shape[1] // LANES
    nk = kk_rot_ref.shape[1] // LANES
    for c in range(nq):
        qn_rot_ref[:, c * LANES:(c + 1) * LANES] = rope(acc[:, c * LANES:(c + 1) * LANES])
    for c in range(ns):
        o = (nq + c) * LANES
        qs_rot_ref[:, c * LANES:(c + 1) * LANES] = rope(acc[:, o:o + LANES])
    for c in range(nk):
        o = (nq + ns + c) * LANES
        kk_rot_ref[:, c * LANES:(c + 1) * LANES] = rope(acc[:, o:o + LANES])
    qn_raw_ref[...] = acc[:, :nq * LANES].astype(BF16)
    plain_ref[...] = jnp.dot(xb, wp_ref[...], preferred_element_type=F32).astype(BF16)
    gates_ref[...] = jnp.dot(xb, wg_ref[...], preferred_element_type=F32)


def _project(x2, w_rope, w_plain, w_gate, cos_t, sin_t, seq, tm):
    t, d = x2.shape
    nr, npl, ng = w_rope.shape[1], w_plain.shape[1], w_gate.shape[1]
    nqn, nqs = NSA_HEADS * HEAD_DIM, SWA_HEADS * HEAD_DIM
    nkk = nr - nqn - nqs
    spb = seq // tm
    row = lambda i: (i, 0)
    return pl.pallas_call(
        _proj_kernel,
        grid=(t // tm,),
        in_specs=[pl.BlockSpec((tm, d), row), _full(w_rope.shape), _full(w_plain.shape),
                  _full(w_gate.shape),
                  pl.BlockSpec((tm, LANES), lambda i: (i % spb, 0)),
                  pl.BlockSpec((tm, LANES), lambda i: (i % spb, 0))],
        out_specs=[pl.BlockSpec((tm, nqn), row), pl.BlockSpec((tm, nqs), row),
                   pl.BlockSpec((tm, nkk), row), pl.BlockSpec((tm, nqn), row),
                   pl.BlockSpec((tm, npl), row), pl.BlockSpec((tm, ng), row)],
        out_shape=[jax.ShapeDtypeStruct((t, nqn), BF16), jax.ShapeDtypeStruct((t, nqs), BF16),
                   jax.ShapeDtypeStruct((t, nkk), BF16), jax.ShapeDtypeStruct((t, nqn), BF16),
                   jax.ShapeDtypeStruct((t, npl), BF16), jax.ShapeDtypeStruct((t, ng), F32)],
        compiler_params=_params("parallel"),
        name="proj",
    )(x2, w_rope, w_plain, w_gate, cos_t, sin_t)


def _compress_kernel(a_ref, b_ref, pe_ref, w1_ref, w2_ref, out_ref):
    half = a_ref.shape[1]
    a = (a_ref[...].astype(F32) + pe_ref[0:1, :]).astype(BF16)
    b = (b_ref[...].astype(F32) + pe_ref[1:2, :]).astype(BF16)
    hid = jnp.dot(a, w1_ref[0:half, :], preferred_element_type=F32)
    hid = hid + jnp.dot(b, w1_ref[half:2 * half, :], preferred_element_type=F32)
    act = jax.nn.gelu(hid).astype(BF16)
    out_ref[...] = jnp.dot(act, w2_ref[...], preferred_element_type=F32).astype(BF16)


def _compress(t_lo, t_hi, pe2, w1, w2):
    two, bg, ncp, half = t_lo.shape
    hid = w1.shape[2]
    blk = lambda shape: pl.BlockSpec((None, None) + shape, lambda j, i: (j, i, 0, 0))
    wsp = lambda shape: pl.BlockSpec((None,) + shape, lambda j, i: (j, 0, 0))
    return pl.pallas_call(
        _compress_kernel,
        grid=(two, bg),
        in_specs=[blk((ncp, half)), blk((ncp, half)), wsp((2, half)), wsp((2 * half, hid)),
                  wsp((hid, HEAD_DIM))],
        out_specs=blk((ncp, HEAD_DIM)),
        out_shape=jax.ShapeDtypeStruct((two, bg, ncp, HEAD_DIM), BF16),
        compiler_params=_params("parallel", "parallel"),
        name="compress",
    )(t_lo, t_hi, pe2, w1, w2)


def _stack_heads(q, g, r):
    return jnp.concatenate(
        [q[:, (g * r + j) * HEAD_DIM:(g * r + j + 1) * HEAD_DIM] for j in range(r)], axis=0)


def _unstack_heads(parts, r, qb):
    return jnp.concatenate([o[j * qb:(j + 1) * qb] for o in parts for j in range(r)], axis=1)


def _unstack_heads_t(parts, r, qb):
    blocks = []
    for o in parts:
        for j in range(0, r, 2):
            pair = jnp.concatenate([o[:, j * qb:(j + 1) * qb], o[:, (j + 1) * qb:(j + 2) * qb]],
                                   axis=0)
            blocks.append(pair.T)
    return jnp.concatenate(blocks, axis=1)


def _topk_mask_cols(vals, k):
    n = vals.shape[0]
    row = lax.broadcasted_iota(I32, vals.shape, 0).astype(F32)
    taken = jnp.zeros(vals.shape, F32)
    work = vals
    for _ in range(k):
        mx = jnp.max(work, axis=0, keepdims=True)
        first = jnp.min(jnp.where(work == mx, row, float(n)), axis=0, keepdims=True)
        pick = row == first
        taken = jnp.where(pick, 1.0, taken)
        work = jnp.where(pick, -jnp.inf, work)
    return taken > 0.5


def _cmp_kernel(q_ref, kc_ref, vct_ref, ovt_ref, o_ref, notsel_ref):
    qb = q_ref.shape[0]
    ncp = kc_ref.shape[1]
    nselp = ovt_ref.shape[0]
    i = pl.program_id(1)
    rows = R_NSA * qb
    pos = i * qb + (lax.broadcasted_iota(I32, (1, rows), 1) & (qb - 1))
    cend = lax.broadcasted_iota(I32, (ncp, 1), 0) * CMP_STRIDE + (CMP_BLOCK - 1)
    visible = cend <= pos
    live = (pos >= CMP_BLOCK - 1).astype(F32)
    q = q_ref[...]
    outs, imps = [], []
    for g in range(NSA_KV):
        qg = _stack_heads(q, g, R_NSA) * (HEAD_DIM ** -0.5)
        st = lax.dot_general(kc_ref[g], qg, NT_DIMS, preferred_element_type=F32)
        st = jnp.where(visible, st, MASKED)
        e = jnp.exp(st - jnp.max(st, axis=0, keepdims=True))
        pt = e * (live / jnp.sum(e, axis=0, keepdims=True))
        outs.append(jnp.dot(vct_ref[g], pt.astype(BF16), preferred_element_type=F32))
        psum = pt[:, 0:qb]
        for j in range(1, R_NSA):
            psum = psum + pt[:, j * qb:(j + 1) * qb]
        p_hi = psum.astype(BF16)
        p_lo = (psum - p_hi.astype(F32)).astype(BF16)
        imps.append(jnp.dot(ovt_ref[...], p_hi, preferred_element_type=F32)
                    + jnp.dot(ovt_ref[...], p_lo, preferred_element_type=F32))
    imp = jnp.concatenate(imps, axis=1)
    lane = lax.broadcasted_iota(I32, (1, NSA_KV * qb), 1)
    cur = (i * qb + (lane & (qb - 1))) >> 6
    jb = lax.broadcasted_iota(I32, (nselp, 1), 0)
    forced = (jb == 0) | (jb == cur) | (jb == cur - 1)
    imp = jnp.where(jb > cur, -1.0, jnp.where(forced, 1e6, imp))
    notsel = jnp.where(_topk_mask_cols(imp, SEL_TOPN), 0.0, 1.0)
    for g in range(NSA_KV):
        notsel_ref[g] = notsel[:, g * qb:(g + 1) * qb].T.astype(BF16)
    o_ref[...] = _unstack_heads_t(outs, R_NSA, qb)


def _cmp_attention(q_raw, kc, vc, overlap_t, qb):
    vc = vc.transpose(0, 1, 3, 2)
    b, s, hq = q_raw.shape
    _, g, ncp, dh = kc.shape
    nselp = overlap_t.shape[0]
    return pl.pallas_call(
        _cmp_kernel,
        grid=(b, s // qb),
        in_specs=[pl.BlockSpec((None, qb, hq), lambda bi, i: (bi, i, 0)),
                  pl.BlockSpec((None, g, ncp, dh), lambda bi, i: (bi, 0, 0, 0)),
                  pl.BlockSpec((None, g, dh, ncp), lambda bi, i: (bi, 0, 0, 0)),
                  _full(overlap_t.shape)],
        out_specs=[pl.BlockSpec((None, qb, hq), lambda bi, i: (bi, i, 0)),
                   pl.BlockSpec((None, g, qb, nselp), lambda bi, i: (bi, 0, i, 0))],
        out_shape=[jax.ShapeDtypeStruct((b, s, hq), F32),
                   jax.ShapeDtypeStruct((b, g, s, nselp), BF16)],
        compiler_params=_params("parallel", "parallel"),
        name="cmp_attn",
    )(q_raw, kc, vc, overlap_t)


def _sel_kernel(q_ref, notsel_ref, k_ref, vt_ref, o_ref, *, tk):
    qb = q_ref.shape[0]
    i = pl.program_id(1)
    rows = R_NSA * qb
    qpos = i * qb + (lax.broadcasted_iota(I32, (1, rows), 1) & (qb - 1))
    n_clear = (i * qb) // tk
    q = q_ref[...]
    q_augs = []
    for g in range(NSA_KV):
        qg = _stack_heads(q, g, R_NSA) * (HEAD_DIM ** -0.5)
        q_augs.append(jnp.concatenate(
            [qg, jnp.zeros((rows, LANES - HEAD_DIM), BF16),
             jnp.concatenate([notsel_ref[g]] * R_NSA, axis=0)], axis=1))

    def step(kt, carry, causal):
        start = pl.multiple_of(kt * tk, tk)
        new = []
        for g in range(NSA_KV):
            m, acc = carry[g]
            k_t = k_ref[g, pl.ds(start, tk), :]
            vt_t = vt_ref[g, :, pl.ds(start, tk)]
            st = lax.dot_general(k_t, q_augs[g], NT_DIMS, preferred_element_type=F32)
            if causal:
                kpos = start + lax.broadcasted_iota(I32, (tk, 1), 0)
                st = jnp.where(kpos <= qpos, st, MASKED)
            m_new = jnp.maximum(m, jnp.max(st, axis=0, keepdims=True))
            pt = jnp.exp(st - m_new).astype(BF16)
            acc = jnp.exp(m - m_new) * acc + jnp.dot(vt_t, pt, preferred_element_type=F32)
            new.append((m_new, acc))
        return tuple(new)

    init = tuple((jnp.full((1, rows), M_INIT, F32), jnp.zeros((LANES, rows), F32))
                 for _ in range(NSA_KV))
    carry = lax.fori_loop(0, n_clear, functools.partial(step, causal=False), init)
    carry = step(n_clear, carry, True)
    outs = [acc[:HEAD_DIM] / acc[HEAD_DIM:HEAD_DIM + 1] for _, acc in carry]
    o_ref[...] = _unstack_heads_t(outs, R_NSA, qb)


def _sel_attention(q_rot, notsel, k_aug, vt_aug, qb, tk):
    b, s, hq = q_rot.shape
    _, g, _, kw = k_aug.shape
    nselp = notsel.shape[3]
    return pl.pallas_call(
        functools.partial(_sel_kernel, tk=tk),
        grid=(b, s // qb),
        in_specs=[pl.BlockSpec((None, qb, hq), lambda bi, i: (bi, i, 0)),
                  pl.BlockSpec((None, g, qb, nselp), lambda bi, i: (bi, 0, i, 0)),
                  pl.BlockSpec((None, g, s, kw), lambda bi, i: (bi, 0, 0, 0)),
                  pl.BlockSpec((None, g, LANES, s), lambda bi, i: (bi, 0, 0, 0))],
        out_specs=pl.BlockSpec((None, qb, hq), lambda bi, i: (bi, i, 0)),
        out_shape=jax.ShapeDtypeStruct((b, s, hq), F32),
        compiler_params=_params("parallel", "parallel"),
        name="sel_attn",
    )(q_rot, notsel, k_aug, vt_aug)


def _band_kernel(*refs, window, wlen, r, kv, qb, has_sinks):
    if has_sinks:
        sink_ref, q_ref, k_ref, vt_ref, o_ref = refs
    else:
        q_ref, k_ref, vt_ref, o_ref = refs
    nsub = q_ref.shape[0] // qb
    rows = r * qb
    lane = lax.broadcasted_iota(I32, (1, rows), 1)
    for sb in range(nsub):
        i = pl.program_id(1) * nsub + sb
        start = pl.multiple_of(jnp.maximum((i + 1) * qb - wlen, 0), qb)
        qpos = i * qb + (lane & (qb - 1))
        kpos = start + lax.broadcasted_iota(I32, (wlen, 1), 0)
        rel = qpos - kpos
        inside = (rel >= 0) & (rel < window)
        q = q_ref[sb * qb:(sb + 1) * qb, :]
        kw = k_ref[pl.ds(start, wlen), :]
        vtw = vt_ref[:, pl.ds(start, wlen)]
        outs = []
        for g in range(kv):
            qg = _stack_heads(q, g, r) * (HEAD_DIM ** -0.5)
            st = lax.dot_general(kw[:, g * HEAD_DIM:(g + 1) * HEAD_DIM], qg, NT_DIMS,
                                 preferred_element_type=F32)
            st = jnp.where(inside, st, MASKED)
            m = jnp.max(st, axis=0, keepdims=True)
            if has_sinks:
                sk = jnp.full((1, rows), sink_ref[g * r], F32)
                for j in range(1, r):
                    sk = jnp.where(lane >= j * qb, sink_ref[g * r + j], sk)
                m = jnp.maximum(m, sk)
            e = jnp.exp(st - m)
            den = jnp.sum(e, axis=0, keepdims=True)
            if has_sinks:
                den = den + jnp.exp(sk - m)
            ot = jnp.dot(vtw[g * HEAD_DIM:(g + 1) * HEAD_DIM, :], e.astype(BF16),
                         preferred_element_type=F32)
            outs.append(ot / den)
        o_ref[sb * qb:(sb + 1) * qb, :] = _unstack_heads_t(outs, r, qb)


def _band_attention(q_rot, k, v, window, sinks, qb):
    v = v.transpose(0, 2, 1)
    b, s, hq = q_rot.shape
    gk = k.shape[2]
    kv = gk // HEAD_DIM
    r = hq // gk
    back = -(-window // qb)
    wlen = (back + 1) * qb
    assert wlen <= s
    has_sinks = sinks is not None
    tq = min(BAND_STEP_QUERIES, s)
    in_specs = [pl.BlockSpec((None, tq, hq), lambda bi, i: (bi, i, 0)),
                pl.BlockSpec((None, s, gk), lambda bi, i: (bi, 0, 0)),
                pl.BlockSpec((None, gk, s), lambda bi, i: (bi, 0, 0))]
    args = [q_rot, k, v]
    if has_sinks:
        in_specs = [pl.BlockSpec(memory_space=pltpu.SMEM)] + in_specs
        args = [sinks.astype(F32)] + args
    return pl.pallas_call(
        functools.partial(_band_kernel, window=window, wlen=wlen, r=r, kv=kv, qb=qb,
                          has_sinks=has_sinks),
        grid=(b, s // tq),
        in_specs=in_specs,
        out_specs=pl.BlockSpec((None, tq, hq), lambda bi, i: (bi, i, 0)),
        out_shape=jax.ShapeDtypeStruct((b, s, hq), F32),
        compiler_params=_params("parallel", "parallel"),
        name="band_attn_sink" if has_sinks else "band_attn",
    )(*args)


def _layer_norm(v, g, b):
    mu = jnp.mean(v, axis=1, keepdims=True)
    c = v - mu
    var = jnp.mean(c * c, axis=1, keepdims=True)
    return c * lax.rsqrt(var + LN_EPS) * g + b


def _split_dot(a, w):
    hi = a.astype(BF16)
    lo = (a - hi.astype(F32)).astype(BF16)
    return (jnp.dot(hi, w, preferred_element_type=F32)
            + jnp.dot(lo, w, preferred_element_type=F32))


def _merge_kernel(ocmp_ref, osel_ref, owin_ref, oswa_ref, gates_ref, x_ref, exp_ref,
                  wbn_ref, wbs_ref, wo_ref, lng_ref, lnb_ref, wrh_ref, wrl_ref, br_ref,
                  h_ref, te_ref, tg_ref, tr_ref, counts_ref, cnt_ref, *, alpha):
    d = x_ref.shape[1]
    hq = ocmp_ref.shape[1]
    gates = gates_ref[...]
    gn = jax.nn.sigmoid(gates[:, 2 * d:])
    gexp = _split_dot(gn, exp_ref[...])
    o_nsa = (gexp[:, 0:hq] * ocmp_ref[...] + gexp[:, hq:2 * hq] * osel_ref[...]
             + gexp[:, 2 * hq:3 * hq] * owin_ref[...])
    y_nsa = jnp.dot(o_nsa.astype(BF16), wbn_ref[...], preferred_element_type=F32)
    y_swa = jnp.dot(oswa_ref[...].astype(BF16), wbs_ref[...], preferred_element_type=F32)
    gm = jax.nn.sigmoid(gates[:, :2 * d])
    mixed = gm[:, :d] * y_nsa + gm[:, d:] * y_swa
    z = jnp.dot(mixed.astype(BF16), wo_ref[...], preferred_element_type=F32)
    h = _layer_norm(alpha * x_ref[...] + z, lng_ref[...], lnb_ref[...])
    h_ref[...] = h
    h_hi = h.astype(BF16)
    h_lo = (h - h_hi.astype(F32)).astype(BF16)
    logits = (jnp.dot(h_hi, wrh_ref[...], preferred_element_type=F32)
              + jnp.dot(h_lo, wrh_ref[...], preferred_element_type=F32)
              + jnp.dot(h_hi, wrl_ref[...], preferred_element_type=F32)) + br_ref[...]
    col = lax.broadcasted_iota(I32, logits.shape, 1).astype(F32)
    work = logits
    vals, ids = [], []
    for _ in range(TOP_K):
        mx = jnp.max(work, axis=1, keepdims=True)
        first = jnp.min(jnp.where(work == mx, col, float(LANES)), axis=1, keepdims=True)
        vals.append(mx)
        ids.append(first)
        work = jnp.where(col == first, -jnp.inf, work)
    es = [jnp.exp(v - vals[0]) for v in vals]
    den = es[0]
    for e in es[1:]:
        den = den + e
    @pl.when(pl.program_id(0) == 0)
    def _():
        cnt_ref[...] = jnp.zeros(cnt_ref.shape, F32)

    tm = logits.shape[0]
    hits = jnp.zeros(logits.shape, F32)
    for k in range(TOP_K):
        hits = jnp.where(col == ids[k], 1.0, hits)
    earlier = (lax.broadcasted_iota(I32, (tm, tm), 1)
               < lax.broadcasted_iota(I32, (tm, tm), 0))
    before = jnp.dot(jnp.where(earlier, 1.0, 0.0).astype(BF16), hits.astype(BF16),
                     preferred_element_type=F32) + cnt_ref[...]
    cnt_ref[...] = cnt_ref[...] + jnp.sum(hits, axis=0, keepdims=True)
    counts_ref[...] = cnt_ref[...]
    te = jnp.zeros(logits.shape, F32)
    tg = jnp.zeros(logits.shape, F32)
    tr = jnp.zeros(logits.shape, F32)
    for k in range(TOP_K):
        rank = jnp.sum(jnp.where(col == ids[k], before, 0.0), axis=1, keepdims=True)
        te = jnp.where(col == float(k), ids[k], te)
        tg = jnp.where(col == float(k), es[k] / den, tg)
        tr = jnp.where(col == float(k), rank, tr)
    te_ref[...] = te.astype(I32)
    tg_ref[...] = tg
    tr_ref[...] = tr.astype(I32)


def _merge(o_cmp, o_sel, o_win, o_swa, gates, x2, expand, w_bn, w_bs, w_o, ln_g, ln_b,
           wr_hi, wr_lo, b_r, alpha, tm):
    t, d = x2.shape
    hq = o_cmp.shape[1]
    row = lambda i: (i, 0)
    tok = lambda n: pl.BlockSpec((tm, n), row)
    return pl.pallas_call(
        functools.partial(_merge_kernel, alpha=alpha),
        grid=(t // tm,),
        in_specs=[tok(hq), tok(hq), tok(hq), tok(hq), tok(gates.shape[1]), tok(d),
                  _full(expand.shape), _full(w_bn.shape), _full(w_bs.shape), _full(w_o.shape),
                  _full(ln_g.shape), _full(ln_b.shape), _full(wr_hi.shape), _full(wr_lo.shape),
                  _full(b_r.shape)],
        out_specs=[tok(d), tok(LANES), tok(LANES), tok(LANES), _full((1, LANES))],
        out_shape=[jax.ShapeDtypeStruct((t, d), F32), jax.ShapeDtypeStruct((t, LANES), I32),
                   jax.ShapeDtypeStruct((t, LANES), F32), jax.ShapeDtypeStruct((t, LANES), I32),
                   jax.ShapeDtypeStruct((1, LANES), F32)],
        scratch_shapes=[pltpu.VMEM((1, LANES), F32)],
        compiler_params=_params("arbitrary"),
        name="merge_ln_router",
    )(o_cmp, o_sel, o_win, o_swa, gates, x2, expand, w_bn, w_bs, w_o, ln_g, ln_b,
      wr_hi, wr_lo, b_r)


def _dispatch_kernel(dest_ref, h_ref, xs_init_ref, xs_ref, sem):
    del xs_init_ref
    tm = h_ref.shape[0]

    def issue(r, carry):
        for k in range(TOP_K):
            dst = dest_ref[r * TOP_K + k]
            pltpu.make_async_copy(h_ref.at[pl.ds(r, 1), :], xs_ref.at[pl.ds(dst, 1), :],
                                  sem).start()
        return carry

    lax.fori_loop(0, tm, issue, 0, unroll=8)
    for _ in range(TOP_K):
        pltpu.make_async_copy(h_ref, xs_ref.at[pl.ds(0, tm), :], sem).wait()


def _dispatch(dest, h, n_rows, tm):
    t, d = h.shape
    return pl.pallas_call(
        _dispatch_kernel,
        grid=(t // tm,),
        in_specs=[pl.BlockSpec((tm * TOP_K,), lambda i: (i,), memory_space=pltpu.SMEM),
                  pl.BlockSpec((tm, d), lambda i: (i, 0)),
                  pl.BlockSpec(memory_space=pl.ANY)],
        out_specs=pl.BlockSpec(memory_space=pl.ANY),
        out_shape=jax.ShapeDtypeStruct((n_rows, d), F32),
        scratch_shapes=[pltpu.SemaphoreType.DMA(())],
        input_output_aliases={2: 0},
        compiler_params=_params("arbitrary"),
        name="moe_dispatch",
    )(dest, h, jnp.zeros((n_rows, d), F32))


def _moe_kernel(blk_e_ref, n_used_ref, x_ref, wi_ref, bi_ref, wo_ref, bo_ref, y_ref,
                wi_bf, wo_bf):
    f = wo_ref.shape[0]
    b = pl.program_id(0)

    @pl.when((b == 0) | (blk_e_ref[b] != blk_e_ref[jnp.maximum(b - 1, 0)]))
    def _():
        wi_bf[...] = wi_ref[...].astype(BF16)
        wo_bf[...] = wo_ref[...].astype(BF16)

    @pl.when(b < n_used_ref[0])
    def _():
        hdn = jnp.dot(x_ref[...].astype(BF16), wi_bf[...],
                      preferred_element_type=F32) + bi_ref[...]
        hg = jnp.minimum(hdn[:, :f], SWIGLU_LIMIT)
        hu = jnp.clip(hdn[:, f:], -SWIGLU_LIMIT, SWIGLU_LIMIT)
        act = hg * jax.nn.sigmoid(SWIGLU_ALPHA * hg) * (hu + 1.0)
        y_ref[...] = jnp.dot(act.astype(BF16), wo_bf[...],
                             preferred_element_type=F32) + bo_ref[...]

    @pl.when(b >= n_used_ref[0])
    def _():
        y_ref[...] = jnp.zeros(y_ref.shape, F32)


def _moe_blocks(blk_e, n_used, xs, w_in, b_in, w_out, b_out):
    n_rows, d = xs.shape
    e, _, f2 = w_in.shape
    f = w_out.shape[1]
    n_blk = n_rows // MOE_ROWS
    grid_spec = pltpu.PrefetchScalarGridSpec(
        num_scalar_prefetch=2,
        grid=(n_blk,),
        in_specs=[pl.BlockSpec((MOE_ROWS, d), lambda b, be, nu: (jnp.minimum(b, nu[0] - 1), 0)),
                  pl.BlockSpec((None, d, f2), lambda b, be, nu: (be[b], 0, 0)),
                  pl.BlockSpec((None, 1, f2), lambda b, be, nu: (be[b], 0, 0)),
                  pl.BlockSpec((None, f, d), lambda b, be, nu: (be[b], 0, 0)),
                  pl.BlockSpec((None, 1, d), lambda b, be, nu: (be[b], 0, 0))],
        out_specs=pl.BlockSpec((MOE_ROWS, d), lambda b, be, nu: (b, 0)),
        scratch_shapes=[pltpu.VMEM((d, f2), BF16), pltpu.VMEM((f, d), BF16)],
    )
    return pl.pallas_call(
        _moe_kernel,
        grid_spec=grid_spec,
        out_shape=jax.ShapeDtypeStruct((n_rows, d), F32),
        compiler_params=_params("arbitrary"),
        name="moe_experts",
    )(blk_e, n_used, xs, w_in, b_in, w_out, b_out)


def _final_kernel(dest_ref, h_ref, tg_ref, ys_ref, g_ref, b_ref, o_ref, ybuf, sem, *, alpha):
    tm = h_ref.shape[0]

    def issue(r, carry):
        for k in range(TOP_K):
            src = dest_ref[r * TOP_K + k]
            pltpu.make_async_copy(ys_ref.at[pl.ds(src, 1), :], ybuf.at[k, pl.ds(r, 1), :],
                                  sem).start()
        return carry

    lax.fori_loop(0, tm, issue, 0, unroll=8)
    for k in range(TOP_K):
        pltpu.make_async_copy(ys_ref.at[pl.ds(0, tm), :], ybuf.at[k], sem).wait()
    tg = tg_ref[...]
    f = tg[:, 0:1] * ybuf[0]
    for k in range(1, TOP_K):
        f = f + tg[:, k:k + 1] * ybuf[k]
    o_ref[...] = _layer_norm(alpha * h_ref[...] + f, g_ref[...], b_ref[...])


def _final(dest, h, top_g, ys, ln_g, ln_b, alpha, tm):
    t, d = h.shape
    return pl.pallas_call(
        functools.partial(_final_kernel, alpha=alpha),
        grid=(t // tm,),
        in_specs=[pl.BlockSpec((tm * TOP_K,), lambda i: (i,), memory_space=pltpu.SMEM),
                  pl.BlockSpec((tm, d), lambda i: (i, 0)),
                  pl.BlockSpec((tm, LANES), lambda i: (i, 0)),
                  pl.BlockSpec(memory_space=pl.ANY),
                  _full(ln_g.shape), _full(ln_b.shape)],
        out_specs=pl.BlockSpec((tm, d), lambda i: (i, 0)),
        out_shape=jax.ShapeDtypeStruct((t, d), F32),
        scratch_shapes=[pltpu.VMEM((TOP_K, tm, d), F32), pltpu.SemaphoreType.DMA(())],
        compiler_params=_params("arbitrary"),
        name="combine_ln",
    )(dest, h, top_g, ys, ln_g, ln_b)


def _rope_tables(s):
    half = HEAD_DIM // 2
    inv = ROPE_THETA ** (-jnp.arange(half, dtype=F32) / half)
    ang = jnp.arange(s, dtype=F32)[:, None] * inv[None, :]
    cos, sin = jnp.cos(ang), jnp.sin(ang)
    reps = LANES // HEAD_DIM
    cos_t = jnp.tile(jnp.concatenate([cos, cos], axis=1), (1, reps))
    sin_t = jnp.tile(jnp.concatenate([-sin, sin], axis=1), (1, reps))
    return cos_t, sin_t


def _moe_plan(top_e, rank, counts, t):
    a = t * TOP_K
    padded = (counts + MOE_ROWS - 1) // MOE_ROWS * MOE_ROWS
    pend = jnp.cumsum(padded)
    pstart = pend - padded
    experts = jnp.arange(N_EXPERTS, dtype=I32)
    base = jnp.sum(jnp.where(top_e[:, :, None] == experts, pstart, 0), axis=2)
    dest = (base + rank).astype(I32).reshape(a)
    n_blk = -(-a // MOE_ROWS) + N_EXPERTS
    blk_first = jnp.arange(n_blk, dtype=I32) * MOE_ROWS
    blk_e = jnp.minimum(jnp.sum(pend[None, :] <= blk_first[:, None], axis=1),
                        N_EXPERTS - 1).astype(I32)
    n_used = (pend[-1] // MOE_ROWS).astype(I32).reshape(1)
    return dest, blk_e, n_used, n_blk * MOE_ROWS


def _layer(x, w_in, k_pe, k_w1, k_w2, v_pe, v_w1, v_w2, sinks, w_br_nsa, w_br_swa, w_out,
           ln1_g, ln1_b, w_router, b_router, w_e_in, b_e_in, w_e_out, b_e_out, ln2_g, ln2_b,
           alpha):
    b, s, d = x.shape
    t = b * s
    qb = Q_BLOCK
    nq_n, nkv = NSA_HEADS * HEAD_DIM, NSA_KV * HEAD_DIM
    nq_s, nkv_s = SWA_HEADS * HEAD_DIM, SWA_KV * HEAD_DIM
    widths = (nq_n, nkv, nkv, nkv, nkv, nkv, nkv, NSA_HEADS * 3, nq_s, nkv_s, nkv_s, 2 * d)
    offs = [0]
    for w in widths:
        offs.append(offs[-1] + w)
    col = lambda j: w_in[:, offs[j]:offs[j + 1]]
    (c_qn, c_kc, c_vc, c_ks, c_vs, c_kw, c_vw, c_gn, c_qs, c_k_s, c_v_s, c_gm) = map(col, range(12))
    w_rope = jnp.concatenate([c_qn, c_qs, c_ks, c_kw, c_k_s], axis=1).astype(BF16)
    w_plain = jnp.concatenate([c_kc, c_vc, c_vs, c_vw, c_v_s], axis=1).astype(BF16)
    gn_pad = LANES - NSA_HEADS * 3
    w_gate = jnp.concatenate([c_gm, c_gn, jnp.zeros((d, gn_pad), F32)], axis=1).astype(BF16)
    cos_t, sin_t = _rope_tables(s)

    x2 = x.reshape(t, d)
    qn_rot, qs_rot, kk_rot, qn_raw, plain, gates = _project(
        x2, w_rope, w_plain, w_gate, cos_t, sin_t, s, min(256, s))

    nc = (s - CMP_BLOCK) // CMP_STRIDE + 1
    ncp = s // CMP_STRIDE
    half = CMP_STRIDE * HEAD_DIM

    def halves(cols):
        v = cols.reshape(b, s, NSA_KV, HEAD_DIM).transpose(0, 2, 1, 3)
        return v.reshape(b * NSA_KV, ncp, half)

    t2 = jnp.stack([halves(plain[:, 0:nkv]), halves(plain[:, nkv:2 * nkv])])
    t_lo = t2
    t_hi = jnp.concatenate([t2[:, :, 1:], jnp.zeros_like(t2[:, :, :1])], axis=2)
    pe2 = jnp.stack([k_pe.reshape(2, half), v_pe.reshape(2, half)])
    w1 = jnp.stack([k_w1, v_w1]).astype(BF16)
    w2 = jnp.stack([k_w2, v_w2]).astype(BF16)
    kvc = _compress(t_lo, t_hi, pe2, w1, w2).reshape(2, b, NSA_KV, ncp, HEAD_DIM)

    nsel = s // SEL_BLOCK
    nselp = -(-nsel // LANES) * LANES
    cstart = jnp.arange(ncp) * CMP_STRIDE
    sstart = jnp.arange(nselp) * SEL_BLOCK
    overlap = ((cstart[:, None] < sstart[None, :] + SEL_BLOCK)
               & (cstart[:, None] + CMP_BLOCK > sstart[None, :])
               & (jnp.arange(ncp)[:, None] < nc) & (jnp.arange(nselp)[None, :] < nsel))
    o_cmp, notsel = _cmp_attention(qn_raw.reshape(b, s, nq_n), kvc[0], kvc[1],
                                   overlap.T.astype(BF16), qb)

    def group_major(cols):
        return cols.reshape(b, s, NSA_KV, HEAD_DIM).transpose(0, 2, 1, 3)

    k_sel = group_major(kk_rot[:, 0:nkv])
    v_sel = group_major(plain[:, 2 * nkv:3 * nkv])
    onehot = (jnp.arange(s)[:, None] // SEL_BLOCK == jnp.arange(nselp)[None, :])
    k_tail = jnp.concatenate([jnp.zeros((s, LANES - HEAD_DIM), F32),
                              jnp.where(onehot, SEL_PENALTY, 0.0)], axis=1).astype(BF16)
    k_aug = jnp.concatenate(
        [k_sel, jnp.broadcast_to(k_tail, (b, NSA_KV) + k_tail.shape)], axis=3)
    v_tail = jnp.zeros((LANES - HEAD_DIM, 1), BF16).at[0].set(1.0)
    vt_aug = jnp.concatenate(
        [v_sel.transpose(0, 1, 3, 2),
         jnp.broadcast_to(v_tail, (b, NSA_KV, LANES - HEAD_DIM, s))], axis=2)
    o_sel = _sel_attention(qn_rot.reshape(b, s, nq_n), notsel, k_aug, vt_aug, qb, min(512, s))

    o_win = _band_attention(qn_rot.reshape(b, s, nq_n), kk_rot[:, nkv:2 * nkv].reshape(b, s, nkv),
                            plain[:, 3 * nkv:4 * nkv].reshape(b, s, nkv), NSA_WINDOW, None, qb)
    o_swa = _band_attention(qs_rot.reshape(b, s, nq_s),
                            kk_rot[:, 2 * nkv:2 * nkv + nkv_s].reshape(b, s, nkv_s),
                            plain[:, 4 * nkv:4 * nkv + nkv_s].reshape(b, s, nkv_s),
                            SWA_WINDOW, sinks, qb)

    gi = jnp.arange(LANES)
    ci = jnp.arange(3 * nq_n)
    expand = ((gi[:, None] // 3 == (ci[None, :] % nq_n) // HEAD_DIM)
              & (gi[:, None] % 3 == ci[None, :] // nq_n)
              & (gi[:, None] < NSA_HEADS * 3)).astype(BF16)
    wr_pad = jnp.pad(w_router, ((0, 0), (0, LANES - N_EXPERTS)))
    wr_hi = wr_pad.astype(BF16)
    wr_lo = (wr_pad - wr_hi.astype(F32)).astype(BF16)
    b_r = jnp.concatenate([b_router, jnp.full((LANES - N_EXPERTS,), -jnp.inf, F32)]).reshape(1, LANES)
    tm = min(256, t)
    h, top_e, top_g, rank, counts = _merge(
        o_cmp.reshape(t, nq_n), o_sel.reshape(t, nq_n), o_win.reshape(t, nq_n),
        o_swa.reshape(t, nq_s), gates, x2, expand, w_br_nsa.astype(BF16), w_br_swa.astype(BF16),
        w_out.astype(BF16), ln1_g.reshape(1, d), ln1_b.reshape(1, d), wr_hi, wr_lo, b_r,
        alpha, tm)

    dest, blk_e, n_used, n_rows = _moe_plan(
        top_e[:, :TOP_K], rank[:, :TOP_K], counts[0, :N_EXPERTS].astype(I32), t)
    xs = _dispatch(dest, h, n_rows, tm)
    ys = _moe_blocks(blk_e, n_used, xs, w_e_in, b_e_in.reshape(N_EXPERTS, 1, -1),
                     w_e_out, b_e_out.reshape(N_EXPERTS, 1, -1))
    out = _final(dest, h, top_g, ys, ln2_g.reshape(1, d), ln2_b.reshape(1, d), alpha, tm)
    return out.reshape(b, s, d)


def kernel(x, w_in, nsa_k_pe, nsa_k_w1, nsa_k_w2, nsa_v_pe, nsa_v_w1, nsa_v_w2, swa_sinks, w_br_nsa, w_br_swa, w_out, ln1_g, ln1_b, w_router, b_router, w_expert_in, b_expert_in, w_expert_out, b_expert_out, ln2_g, ln2_b):
    depth = w_in.shape[0]
    alpha = (2.0 * depth) ** 0.25
    for l in range(depth):
        x = _layer(x, w_in[l], nsa_k_pe[l], nsa_k_w1[l], nsa_k_w2[l], nsa_v_pe[l], nsa_v_w1[l],
                   nsa_v_w2[l], swa_sinks[l], w_br_nsa[l], w_br_swa[l], w_out[l], ln1_g[l],
                   ln1_b[l], w_router[l], b_router[l], w_expert_in[l], b_expert_in[l],
                   w_expert_out[l], b_expert_out[l], ln2_g[l], ln2_b[l], alpha)
    return x
```

```python
import functools

import jax
import jax.numpy as jnp
from jax import lax
from jax.experimental import pallas as pl
from jax.experimental.pallas import tpu as pltpu

BF16 = jnp.bfloat16
F32 = jnp.float32
I32 = jnp.int32

HEAD_DIM = 64
NSA_HEADS = 8
NSA_KV = 2
CMP_BLOCK = 32
CMP_STRIDE = 16
SEL_BLOCK = 64
SEL_TOPN = 16
NSA_WINDOW = 512
SWA_HEADS = 8
SWA_KV = 2
SWA_WINDOW = 128
Q_BLOCK = 128
ROPE_THETA = 10000.0
N_EXPERTS = 32
TOP_K = 4
SWIGLU_LIMIT = 7.0
SWIGLU_ALPHA = 1.702
LN_EPS = 1e-5

LANES = 128
MASKED = -1e30
M_INIT = -1e29
SEL_PENALTY = -(2.0 ** 100)
VMEM_LIMIT = 52 * 1024 * 1024
MOE_ROWS = 256
BAND_STEP_QUERIES = 512
ROWMAP_UNROLL = 16

R_NSA = NSA_HEADS // NSA_KV
R_SWA = SWA_HEADS // SWA_KV
NT_DIMS = (((1,), (1,)), ((), ()))


def _params(*sem):
    return pltpu.CompilerParams(dimension_semantics=sem, vmem_limit_bytes=VMEM_LIMIT)


def _full(shape):
    n = len(shape)
    return pl.BlockSpec(shape, lambda *_: (0,) * n)


def _proj_kernel(x_ref, wr_ref, wp_ref, wg_ref, cos_ref, sin_ref,
                 qn_rot_ref, qs_rot_ref, kk_rot_ref, qn_raw_ref, plain_ref, gates_ref):
    xb = x_ref[...].astype(BF16)
    acc = jnp.dot(xb, wr_ref[...], preferred_element_type=F32)
    cos = cos_ref[...]
    sin = sin_ref[...]
    lane = lax.broadcasted_iota(I32, cos.shape, 1)
    first_half = (lane & (HEAD_DIM - 1)) < HEAD_DIM // 2

    def rope(t):
        partner = jnp.where(first_half, pltpu.roll(t, LANES - HEAD_DIM // 2, 1),
                            pltpu.roll(t, HEAD_DIM // 2, 1))
        return (t * cos + partner * sin).astype(BF16)

    nq = qn_rot_ref.shape[1] // LANES
    ns = qs_rot_ref.shape[1] // LANES
    nk = kk_rot_ref.shape[1] // LANES
    for c in range(nq):
        qn_rot_ref[:, c * LANES:(c + 1) * LANES] = rope(acc[:, c * LANES:(c + 1) * LANES])
    for c in range(ns):
        o = (nq + c) * LANES
        qs_rot_ref[:, c * LANES:(c + 1) * LANES] = rope(acc[:, o:o + LANES])
    for c in range(nk):
        o = (nq + ns + c) * LANES
        kk_rot_ref[:, c * LANES:(c + 1) * LANES] = rope(acc[:, o:o + LANES])
    qn_raw_ref[...] = acc[:, :nq * LANES].astype(BF16)
    plain_ref[...] = jnp.dot(xb, wp_ref[...], preferred_element_type=F32).astype(BF16)
    gates_ref[...] = jnp.dot(xb, wg_ref[...], preferred_element_type=F32)


def _project(x2, w_rope, w_plain, w_gate, cos_t, sin_t, seq, tm):
    t, d = x2.shape
    nr, npl, ng = w_rope.shape[1], w_plain.shape[1], w_gate.shape[1]
    nqn, nqs = NSA_HEADS * HEAD_DIM, SWA_HEADS * HEAD_DIM
    nkk = nr - nqn - nqs
    spb = seq // tm
    row = lambda i: (i, 0)
    return pl.pallas_call(
        _proj_kernel,
        grid=(t // tm,),
        in_specs=[pl.BlockSpec((tm, d), row), _full(w_rope.shape), _full(w_plain.shape),
                  _full(w_gate.shape),
                  pl.BlockSpec((tm, LANES), lambda i: (i % spb, 0)),
                  pl.BlockSpec((tm, LANES), lambda i: (i % spb, 0))],
        out_specs=[pl.BlockSpec((tm, nqn), row), pl.BlockSpec((tm, nqs), row),
                   pl.BlockSpec((tm, nkk), row), pl.BlockSpec((tm, nqn), row),
                   pl.BlockSpec((tm, npl), row), pl.BlockSpec((tm, ng), row)],
        out_shape=[jax.ShapeDtypeStruct((t, nqn), BF16), jax.ShapeDtypeStruct((t, nqs), BF16),
                   jax.ShapeDtypeStruct((t, nkk), BF16), jax.ShapeDtypeStruct((t, nqn), BF16),
                   jax.ShapeDtypeStruct((t, npl), BF16), jax.ShapeDtypeStruct((t, ng), F32)],
        compiler_params=_params("parallel"),
        name="proj",
    )(x2, w_rope, w_plain, w_gate, cos_t, sin_t)


def _compress_kernel(a_ref, b_ref, pe_ref, w1_ref, w2_ref, out_ref):
    half = a_ref.shape[1]
    a = (a_ref[...].astype(F32) + pe_ref[0:1, :]).astype(BF16)
    b = (b_ref[...].astype(F32) + pe_ref[1:2, :]).astype(BF16)
    hid = jnp.dot(a, w1_ref[0:half, :], preferred_element_type=F32)
    hid = hid + jnp.dot(b, w1_ref[half:2 * half, :], preferred_element_type=F32)
    act = jax.nn.gelu(hid).astype(BF16)
    out_ref[...] = jnp.dot(act, w2_ref[...], preferred_element_type=F32).astype(BF16)


def _compress(t_lo, t_hi, pe2, w1, w2):
    two, bg, ncp, half = t_lo.shape
    hid = w1.shape[2]
    blk = lambda shape: pl.BlockSpec((None, None) + shape, lambda j, i: (j, i, 0, 0))
    wsp = lambda shape: pl.BlockSpec((None,) + shape, lambda j, i: (j, 0, 0))
    return pl.pallas_call(
        _compress_kernel,
        grid=(two, bg),
        in_specs=[blk((ncp, half)), blk((ncp, half)), wsp((2, half)), wsp((2 * half, hid)),
                  wsp((hid, HEAD_DIM))],
        out_specs=blk((ncp, HEAD_DIM)),
        out_shape=jax.ShapeDtypeStruct((two, bg, ncp, HEAD_DIM), BF16),
        compiler_params=_params("parallel", "parallel"),
        name="compress",
    )(t_lo, t_hi, pe2, w1, w2)


def _stack_heads(q, g, r):
    return jnp.concatenate(
        [q[:, (g * r + j) * HEAD_DIM:(g * r + j + 1) * HEAD_DIM] for j in range(r)], axis=0)


def _unstack_heads(parts, r, qb):
    return jnp.concatenate([o[j * qb:(j + 1) * qb] for o in parts for j in range(r)], axis=1)


def _unstack_heads_t(parts, r, qb):
    blocks = []
    for o in parts:
        for j in range(0, r, 2):
            pair = jnp.concatenate([o[:, j * qb:(j + 1) * qb], o[:, (j + 1) * qb:(j + 2) * qb]],
                                   axis=0)
            blocks.append(pair.T)
    return jnp.concatenate(blocks, axis=1)


def _topk_mask_cols(vals, k):
    n = vals.shape[0]
    row = lax.broadcasted_iota(I32, vals.shape, 0).astype(F32)
    taken = jnp.zeros(vals.shape, F32)
    work = vals
    for _ in range(k):
        mx = jnp.max(work, axis=0, keepdims=True)
        first = jnp.min(jnp.where(work == mx, row, float(n)), axis=0, keepdims=True)
        pick = row == first
        taken = jnp.where(pick, 1.0, taken)
        work = jnp.where(pick, -jnp.inf, work)
    return taken > 0.5


def _cmp_kernel(q_ref, kc_ref, vct_ref, ovt_ref, o_ref, notsel_ref):
    qb = q_ref.shape[0]
    ncp = kc_ref.shape[1]
    nselp = ovt_ref.shape[0]
    i = pl.program_id(1)
    rows = R_NSA * qb
    pos = i * qb + (lax.broadcasted_iota(I32, (1, rows), 1) & (qb - 1))
    cend = lax.broadcasted_iota(I32, (ncp, 1), 0) * CMP_STRIDE + (CMP_BLOCK - 1)
    visible = cend <= pos
    live = (pos >= CMP_BLOCK - 1).astype(F32)
    q = q_ref[...]
    outs, imps = [], []
    for g in range(NSA_KV):
        qg = _stack_heads(q, g, R_NSA) * (HEAD_DIM ** -0.5)
        st = lax.dot_general(kc_ref[g], qg, NT_DIMS, preferred_element_type=F32)
        st = jnp.where(visible, st, MASKED)
        e = jnp.exp(st - jnp.max(st, axis=0, keepdims=True))
        pt = e * (live / jnp.sum(e, axis=0, keepdims=True))
        outs.append(jnp.dot(vct_ref[g], pt.astype(BF16), preferred_element_type=F32))
        psum = pt[:, 0:qb]
        for j in range(1, R_NSA):
            psum = psum + pt[:, j * qb:(j + 1) * qb]
        p_hi = psum.astype(BF16)
        p_lo = (psum - p_hi.astype(F32)).astype(BF16)
        imps.append(jnp.dot(ovt_ref[...], p_hi, preferred_element_type=F32)
                    + jnp.dot(ovt_ref[...], p_lo, preferred_element_type=F32))
    imp = jnp.concatenate(imps, axis=1)
    lane = lax.broadcasted_iota(I32, (1, NSA_KV * qb), 1)
    cur = (i * qb + (lane & (qb - 1))) >> 6
    jb = lax.broadcasted_iota(I32, (nselp, 1), 0)
    forced = (jb == 0) | (jb == cur) | (jb == cur - 1)
    imp = jnp.where(jb > cur, -1.0, jnp.where(forced, 1e6, imp))
    notsel = jnp.where(_topk_mask_cols(imp, SEL_TOPN), 0.0, 1.0)
    for g in range(NSA_KV):
        notsel_ref[g] = notsel[:, g * qb:(g + 1) * qb].T.astype(BF16)
    o_ref[...] = _unstack_heads_t(outs, R_NSA, qb)


def _cmp_attention(q_raw, kc, vc, overlap_t, qb):
    vc = vc.transpose(0, 1, 3, 2)
    b, s, hq = q_raw.shape
    _, g, ncp, dh = kc.shape
    nselp = overlap_t.shape[0]
    return pl.pallas_call(
        _cmp_kernel,
        grid=(b, s // qb),
        in_specs=[pl.BlockSpec((None, qb, hq), lambda bi, i: (bi, i, 0)),
                  pl.BlockSpec((None, g, ncp, dh), lambda bi, i: (bi, 0, 0, 0)),
                  pl.BlockSpec((None, g, dh, ncp), lambda bi, i: (bi, 0, 0, 0)),
                  _full(overlap_t.shape)],
        out_specs=[pl.BlockSpec((None, qb, hq), lambda bi, i: (bi, i, 0)),
                   pl.BlockSpec((None, g, qb, nselp), lambda bi, i: (bi, 0, i, 0))],
        out_shape=[jax.ShapeDtypeStruct((b, s, hq), F32),
                   jax.ShapeDtypeStruct((b, g, s, nselp), BF16)],
        compiler_params=_params("parallel", "parallel"),
        name="cmp_attn",
    )(q_raw, kc, vc, overlap_t)


def _sel_kernel(q_ref, notsel_ref, k_ref, vt_ref, o_ref, *, tk):
    qb = q_ref.shape[0]
    i = pl.program_id(1)
    rows = R_NSA * qb
    qpos = i * qb + (lax.broadcasted_iota(I32, (1, rows), 1) & (qb - 1))
    n_clear = (i * qb) // tk
    q = q_ref[...]
    q_augs = []
    for g in range(NSA_KV):
        qg = _stack_heads(q, g, R_NSA) * (HEAD_DIM ** -0.5)
        q_augs.append(jnp.concatenate(
            [qg, jnp.zeros((rows, LANES - HEAD_DIM), BF16),
             jnp.concatenate([notsel_ref[g]] * R_NSA, axis=0)], axis=1))

    def step(kt, carry, causal):
        start = pl.multiple_of(kt * tk, tk)
        new = []
        for g in range(NSA_KV):
            m, acc = carry[g]
            k_t = k_ref[g, pl.ds(start, tk), :]
            vt_t = vt_ref[g, :, pl.ds(start, tk)]
            st = lax.dot_general(k_t, q_augs[g], NT_DIMS, preferred_element_type=F32)
            if causal:
                kpos = start + lax.broadcasted_iota(I32, (tk, 1), 0)
                st = jnp.where(kpos <= qpos, st, MASKED)
            m_new = jnp.maximum(m, jnp.max(st, axis=0, keepdims=True))
            pt = jnp.exp(st - m_new).astype(BF16)
            acc = jnp.exp(m - m_new) * acc + jnp.dot(vt_t, pt, preferred_element_type=F32)
            new.append((m_new, acc))
        return tuple(new)

    init = tuple((jnp.full((1, rows), M_INIT, F32), jnp.zeros((LANES, rows), F32))
                 for _ in range(NSA_KV))
    carry = lax.fori_loop(0, n_clear, functools.partial(step, causal=False), init)
    carry = step(n_clear, carry, True)
    outs = [acc[:HEAD_DIM] / acc[HEAD_DIM:HEAD_DIM + 1] for _, acc in carry]
    o_ref[...] = _unstack_heads_t(outs, R_NSA, qb)


def _sel_attention(q_rot, notsel, k_aug, vt_aug, qb, tk):
    b, s, hq = q_rot.shape
    _, g, _, kw = k_aug.shape
    nselp = notsel.shape[3]
    return pl.pallas_call(
        functools.partial(_sel_kernel, tk=tk),
        grid=(b, s // qb),
        in_specs=[pl.BlockSpec((None, qb, hq), lambda bi, i: (bi, i, 0)),
                  pl.BlockSpec((None, g, qb, nselp), lambda bi, i: (bi, 0, i, 0)),
                  pl.BlockSpec((None, g, s, kw), lambda bi, i: (bi, 0, 0, 0)),
                  pl.BlockSpec((None, g, LANES, s), lambda bi, i: (bi, 0, 0, 0))],
        out_specs=pl.BlockSpec((None, qb, hq), lambda bi, i: (bi, i, 0)),
        out_shape=jax.ShapeDtypeStruct((b, s, hq), F32),
        compiler_params=_params("parallel", "parallel"),
        name="sel_attn",
    )(q_rot, notsel, k_aug, vt_aug)


def _band_kernel(*refs, window, wlen, r, kv, qb, has_sinks):
    if has_sinks:
        sink_ref, q_ref, k_ref, vt_ref, o_ref = refs
    else:
        q_ref, k_ref, vt_ref, o_ref = refs
    nsub = q_ref.shape[0] // qb
    rows = r * qb
    lane = lax.broadcasted_iota(I32, (1, rows), 1)
    for sb in range(nsub):
        i = pl.program_id(1) * nsub + sb
        start = pl.multiple_of(jnp.maximum((i + 1) * qb - wlen, 0), qb)
        qpos = i * qb + (lane & (qb - 1))
        kpos = start + lax.broadcasted_iota(I32, (wlen, 1), 0)
        rel = qpos - kpos
        inside = (rel >= 0) & (rel < window)
        q = q_ref[sb * qb:(sb + 1) * qb, :]
        kw = k_ref[pl.ds(start, wlen), :]
        vtw = vt_ref[:, pl.ds(start, wlen)]
        outs = []
        for g in range(kv):
            qg = _stack_heads(q, g, r) * (HEAD_DIM ** -0.5)
            st = lax.dot_general(kw[:, g * HEAD_DIM:(g + 1) * HEAD_DIM], qg, NT_DIMS,
                                 preferred_element_type=F32)
            st = jnp.where(inside, st, MASKED)
            m = jnp.max(st, axis=0, keepdims=True)
            if has_sinks:
                sk = jnp.full((1, rows), sink_ref[g * r], F32)
                for j in range(1, r):
                    sk = jnp.where(lane >= j * qb, sink_ref[g * r + j], sk)
                m = jnp.maximum(m, sk)
            e = jnp.exp(st - m)
            den = jnp.sum(e, axis=0, keepdims=True)
            if has_sinks:
                den = den + jnp.exp(sk - m)
            ot = jnp.dot(vtw[g * HEAD_DIM:(g + 1) * HEAD_DIM, :], e.astype(BF16),
                         preferred_element_type=F32)
            outs.append(ot / den)
        o_ref[sb * qb:(sb + 1) * qb, :] = _unstack_heads_t(outs, r, qb)


def _band_attention(q_rot, k, v, window, sinks, qb):
    v = v.transpose(0, 2, 1)
    b, s, hq = q_rot.shape
    gk = k.shape[2]
    kv = gk // HEAD_DIM
    r = hq // gk
    back = -(-window // qb)
    wlen = (back + 1) * qb
    assert wlen <= s
    has_sinks = sinks is not None
    tq = min(BAND_STEP_QUERIES, s)
    in_specs = [pl.BlockSpec((None, tq, hq), lambda bi, i: (bi, i, 0)),
                pl.BlockSpec((None, s, gk), lambda bi, i: (bi, 0, 0)),
                pl.BlockSpec((None, gk, s), lambda bi, i: (bi, 0, 0))]
    args = [q_rot, k, v]
    if has_sinks:
        in_specs = [pl.BlockSpec(memory_space=pltpu.SMEM)] + in_specs
        args = [sinks.astype(F32)] + args
    return pl.pallas_call(
        functools.partial(_band_kernel, window=window, wlen=wlen, r=r, kv=kv, qb=qb,
                          has_sinks=has_sinks),
        grid=(b, s // tq),
        in_specs=in_specs,
        out_specs=pl.BlockSpec((None, tq, hq), lambda bi, i: (bi, i, 0)),
        out_shape=jax.ShapeDtypeStruct((b, s, hq), F32),
        compiler_params=_params("parallel", "parallel"),
        name="band_attn_sink" if has_sinks else "band_attn",
    )(*args)


def _layer_norm(v, g, b):
    mu = jnp.mean(v, axis=1, keepdims=True)
    c = v - mu
    var = jnp.mean(c * c, axis=1, keepdims=True)
    return c * lax.rsqrt(var + LN_EPS) * g + b


def _split_dot(a, w):
    hi = a.astype(BF16)
    lo = (a - hi.astype(F32)).astype(BF16)
    return (jnp.dot(hi, w, preferred_element_type=F32)
            + jnp.dot(lo, w, preferred_element_type=F32))


def _merge_kernel(ocmp_ref, osel_ref, owin_ref, oswa_ref, gates_ref, x_ref, exp_ref,
                  wbn_ref, wbs_ref, wo_ref, lng_ref, lnb_ref, wrh_ref, wrl_ref, br_ref,
                  h_ref, te_ref, tg_ref, tr_ref, counts_ref, cnt_ref, *, alpha):
    d = x_ref.shape[1]
    hq = ocmp_ref.shape[1]
    gates = gates_ref[...]
    gn = jax.nn.sigmoid(gates[:, 2 * d:])
    gexp = _split_dot(gn, exp_ref[...])
    o_nsa = (gexp[:, 0:hq] * ocmp_ref[...] + gexp[:, hq:2 * hq] * osel_ref[...]
             + gexp[:, 2 * hq:3 * hq] * owin_ref[...])
    y_nsa = jnp.dot(o_nsa.astype(BF16), wbn_ref[...], preferred_element_type=F32)
    y_swa = jnp.dot(oswa_ref[...].astype(BF16), wbs_ref[...], preferred_element_type=F32)
    gm = jax.nn.sigmoid(gates[:, :2 * d])
    mixed = gm[:, :d] * y_nsa + gm[:, d:] * y_swa
    z = jnp.dot(mixed.astype(BF16), wo_ref[...], preferred_element_type=F32)
    h = _layer_norm(alpha * x_ref[...] + z, lng_ref[...], lnb_ref[...])
    h_ref[...] = h
    h_hi = h.astype(BF16)
    h_lo = (h - h_hi.astype(F32)).astype(BF16)
    logits = (jnp.dot(h_hi, wrh_ref[...], preferred_element_type=F32)
              + jnp.dot(h_lo, wrh_ref[...], preferred_element_type=F32)
              + jnp.dot(h_hi, wrl_ref[...], preferred_element_type=F32)) + br_ref[...]
    col = lax.broadcasted_iota(I32, logits.shape, 1).astype(F32)
    work = logits
    vals, ids = [], []
    for _ in range(TOP_K):
        mx = jnp.max(work, axis=1, keepdims=True)
        first = jnp.min(jnp.where(work == mx, col, float(LANES)), axis=1, keepdims=True)
        vals.append(mx)
        ids.append(first)
        work = jnp.where(col == first, -jnp.inf, work)
    es = [jnp.exp(v - vals[0]) for v in vals]
    den = es[0]
    for e in es[1:]:
        den = den + e
    @pl.when(pl.program_id(0) == 0)
    def _():
        cnt_ref[...] = jnp.zeros(cnt_ref.shape, F32)

    tm = logits.shape[0]
    hits = jnp.zeros(logits.shape, F32)
    for k in range(TOP_K):
        hits = jnp.where(col == ids[k], 1.0, hits)
    earlier = (lax.broadcasted_iota(I32, (tm, tm), 1)
               < lax.broadcasted_iota(I32, (tm, tm), 0))
    before = jnp.dot(jnp.where(earlier, 1.0, 0.0).astype(BF16), hits.astype(BF16),
                     preferred_element_type=F32) + cnt_ref[...]
    cnt_ref[...] = cnt_ref[...] + jnp.sum(hits, axis=0, keepdims=True)
    counts_ref[...] = cnt_ref[...]
    te = jnp.zeros(logits.shape, F32)
    tg = jnp.zeros(logits.shape, F32)
    tr = jnp.zeros(logits.shape, F32)
    for k in range(TOP_K):
        rank = jnp.sum(jnp.where(col == ids[k], before, 0.0), axis=1, keepdims=True)
        te = jnp.where(col == float(k), ids[k], te)
        tg = jnp.where(col == float(k), es[k] / den, tg)
        tr = jnp.where(col == float(k), rank, tr)
    te_ref[...] = te.astype(I32)
    tg_ref[...] = tg
    tr_ref[...] = tr.astype(I32)


def _merge(o_cmp, o_sel, o_win, o_swa, gates, x2, expand, w_bn, w_bs, w_o, ln_g, ln_b,
           wr_hi, wr_lo, b_r, alpha, tm):
    t, d = x2.shape
    hq = o_cmp.shape[1]
    row = lambda i: (i, 0)
    tok = lambda n: pl.BlockSpec((tm, n), row)
    return pl.pallas_call(
        functools.partial(_merge_kernel, alpha=alpha),
        grid=(t // tm,),
        in_specs=[tok(hq), tok(hq), tok(hq), tok(hq), tok(gates.shape[1]), tok(d),
                  _full(expand.shape), _full(w_bn.shape), _full(w_bs.shape), _full(w_o.shape),
                  _full(ln_g.shape), _full(ln_b.shape), _full(wr_hi.shape), _full(wr_lo.shape),
                  _full(b_r.shape)],
        out_specs=[tok(d), tok(LANES), tok(LANES), tok(LANES), _full((1, LANES))],
        out_shape=[jax.ShapeDtypeStruct((t, d), F32), jax.ShapeDtypeStruct((t, LANES), I32),
                   jax.ShapeDtypeStruct((t, LANES), F32), jax.ShapeDtypeStruct((t, LANES), I32),
                   jax.ShapeDtypeStruct((1, LANES), F32)],
        scratch_shapes=[pltpu.VMEM((1, LANES), F32)],
        compiler_params=_params("arbitrary"),
        name="merge_ln_router",
    )(o_cmp, o_sel, o_win, o_swa, gates, x2, expand, w_bn, w_bs, w_o, ln_g, ln_b,
      wr_hi, wr_lo, b_r)


def _rowmap_kernel(dest_ref, src_ref, *, pad):
    def fill(j, carry):
        for u in range(ROWMAP_UNROLL):
            src_ref[j * ROWMAP_UNROLL + u] = pad
        return carry

    def put(j, carry):
        for u in range(ROWMAP_UNROLL):
            a = j * ROWMAP_UNROLL + u
            src_ref[dest_ref[a]] = a
        return carry

    lax.fori_loop(0, src_ref.shape[0] // ROWMAP_UNROLL, fill, 0)
    lax.fori_loop(0, dest_ref.shape[0] // ROWMAP_UNROLL, put, 0)


def _rowmap(dest, n_rows):
    return pl.pallas_call(
        functools.partial(_rowmap_kernel, pad=dest.shape[0]),
        in_specs=[pl.BlockSpec(memory_space=pltpu.SMEM)],
        out_specs=pl.BlockSpec(memory_space=pltpu.SMEM),
        out_shape=jax.ShapeDtypeStruct((n_rows,), I32),
        name="moe_rowmap",
    )(dest)


def _moe_kernel(blk_e_ref, n_used_ref, src_ref, h_ref, wi_ref, bi_ref, wo_ref, bo_ref, y_ref,
                wi_bf, wo_bf, xbuf0, xbuf1, gsem):
    f = wo_ref.shape[0]
    t = h_ref.shape[0]
    b = pl.program_id(0)
    n_used = n_used_ref[0]
    xbufs = (xbuf0, xbuf1)

    def gather(blk, into):
        for r in range(MOE_ROWS):
            tok = jnp.minimum(src_ref[blk * MOE_ROWS + r] >> 2, t - 1)
            pltpu.make_async_copy(h_ref.at[pl.ds(tok, 1), :], xbufs[into].at[pl.ds(r, 1), :],
                                  gsem.at[into]).start()

    def wait_gather(into):
        pltpu.make_async_copy(h_ref.at[pl.ds(0, MOE_ROWS), :], xbufs[into],
                              gsem.at[into]).wait()

    @pl.when(b == 0)
    def _():
        gather(0, 0)

    @pl.when((b == 0) | (blk_e_ref[b] != blk_e_ref[jnp.maximum(b - 1, 0)]))
    def _():
        wi_bf[...] = wi_ref[...].astype(BF16)
        wo_bf[...] = wo_ref[...].astype(BF16)

    for slot in range(2):
        parity = (b & 1) == slot

        @pl.when((b < n_used) & parity)
        def _(slot=slot):
            wait_gather(slot)
            gather(jnp.minimum(b + 1, n_used - 1), 1 - slot)
            hdn = jnp.dot(xbufs[slot][...].astype(BF16), wi_bf[...],
                          preferred_element_type=F32) + bi_ref[...]
            hg = jnp.minimum(hdn[:, :f], SWIGLU_LIMIT)
            hu = jnp.clip(hdn[:, f:], -SWIGLU_LIMIT, SWIGLU_LIMIT)
            act = hg * jax.nn.sigmoid(SWIGLU_ALPHA * hg) * (hu + 1.0)
            y_ref[...] = jnp.dot(act.astype(BF16), wo_bf[...],
                                 preferred_element_type=F32) + bo_ref[...]

        @pl.when((b == n_used) & parity)
        def _(slot=slot):
            wait_gather(slot)

    @pl.when(b >= n_used)
    def _():
        y_ref[...] = jnp.zeros(y_ref.shape, F32)


def _moe_blocks(blk_e, n_used, src, h, w_in, b_in, w_out, b_out):
    t, d = h.shape
    e, _, f2 = w_in.shape
    f = w_out.shape[1]
    n_blk = src.shape[0] // MOE_ROWS
    grid_spec = pltpu.PrefetchScalarGridSpec(
        num_scalar_prefetch=3,
        grid=(n_blk,),
        in_specs=[pl.BlockSpec(memory_space=pl.ANY),
                  pl.BlockSpec((None, d, f2), lambda b, be, nu, sr: (be[b], 0, 0)),
                  pl.BlockSpec((None, 1, f2), lambda b, be, nu, sr: (be[b], 0, 0)),
                  pl.BlockSpec((None, f, d), lambda b, be, nu, sr: (be[b], 0, 0)),
                  pl.BlockSpec((None, 1, d), lambda b, be, nu, sr: (be[b], 0, 0))],
        out_specs=pl.BlockSpec((MOE_ROWS, d), lambda b, be, nu, sr: (b, 0)),
        scratch_shapes=[pltpu.VMEM((d, f2), BF16), pltpu.VMEM((f, d), BF16),
                        pltpu.VMEM((MOE_ROWS, d), F32), pltpu.VMEM((MOE_ROWS, d), F32),
                        pltpu.SemaphoreType.DMA((2,))],
    )
    return pl.pallas_call(
        _moe_kernel,
        grid_spec=grid_spec,
        out_shape=jax.ShapeDtypeStruct((n_blk * MOE_ROWS, d), F32),
        compiler_params=_params("arbitrary"),
        name="moe_experts",
    )(blk_e, n_used, src, h, w_in, b_in, w_out, b_out)


def _final_kernel(dest_ref, h_ref, tg_ref, ys_ref, g_ref, b_ref, o_ref, ybuf, sem, *, alpha):
    tm = h_ref.shape[0]

    def issue(r, carry):
        for k in range(TOP_K):
            src = dest_ref[r * TOP_K + k]
            pltpu.make_async_copy(ys_ref.at[pl.ds(src, 1), :], ybuf.at[k, pl.ds(r, 1), :],
                                  sem).start(priority=k % 2)
        return carry

    lax.fori_loop(0, tm, issue, 0, unroll=8)
    for k in range(TOP_K):
        pltpu.make_async_copy(ys_ref.at[pl.ds(0, tm), :], ybuf.at[k], sem).wait()
    tg = tg_ref[...]
    f = tg[:, 0:1] * ybuf[0]
    for k in range(1, TOP_K):
        f = f + tg[:, k:k + 1] * ybuf[k]
    o_ref[...] = _layer_norm(alpha * h_ref[...] + f, g_ref[...], b_ref[...])


def _final(dest, h, top_g, ys, ln_g, ln_b, alpha, tm):
    t, d = h.shape
    return pl.pallas_call(
        functools.partial(_final_kernel, alpha=alpha),
        grid=(t // tm,),
        in_specs=[pl.BlockSpec((tm * TOP_K,), lambda i: (i,), memory_space=pltpu.SMEM),
                  pl.BlockSpec((tm, d), lambda i: (i, 0)),
                  pl.BlockSpec((tm, LANES), lambda i: (i, 0)),
                  pl.BlockSpec(memory_space=pl.ANY),
                  _full(ln_g.shape), _full(ln_b.shape)],
        out_specs=pl.BlockSpec((tm, d), lambda i: (i, 0)),
        out_shape=jax.ShapeDtypeStruct((t, d), F32),
        scratch_shapes=[pltpu.VMEM((TOP_K, tm, d), F32), pltpu.SemaphoreType.DMA(())],
        compiler_params=_params("arbitrary"),
        name="combine_ln",
    )(dest, h, top_g, ys, ln_g, ln_b)


def _rope_tables(s):
    half = HEAD_DIM // 2
    inv = ROPE_THETA ** (-jnp.arange(half, dtype=F32) / half)
    ang = jnp.arange(s, dtype=F32)[:, None] * inv[None, :]
    cos, sin = jnp.cos(ang), jnp.sin(ang)
    reps = LANES // HEAD_DIM
    cos_t = jnp.tile(jnp.concatenate([cos, cos], axis=1), (1, reps))
    sin_t = jnp.tile(jnp.concatenate([-sin, sin], axis=1), (1, reps))
    return cos_t, sin_t


def _moe_plan(top_e, rank, counts, t):
    a = t * TOP_K
    padded = (counts + MOE_ROWS - 1) // MOE_ROWS * MOE_ROWS
    pend = jnp.cumsum(padded)
    pstart = pend - padded
    experts = jnp.arange(N_EXPERTS, dtype=I32)
    base = jnp.sum(jnp.where(top_e[:, :, None] == experts, pstart, 0), axis=2)
    dest = (base + rank).astype(I32).reshape(a)
    n_blk = -(-a // MOE_ROWS) + N_EXPERTS + 1
    blk_first = jnp.arange(n_blk, dtype=I32) * MOE_ROWS
    blk_e = jnp.minimum(jnp.sum(pend[None, :] <= blk_first[:, None], axis=1),
                        N_EXPERTS - 1).astype(I32)
    n_used = (pend[-1] // MOE_ROWS).astype(I32).reshape(1)
    return dest, blk_e, n_used, n_blk * MOE_ROWS


def _layer(x, w_in, k_pe, k_w1, k_w2, v_pe, v_w1, v_w2, sinks, w_br_nsa, w_br_swa, w_out,
           ln1_g, ln1_b, w_router, b_router, w_e_in, b_e_in, w_e_out, b_e_out, ln2_g, ln2_b,
           alpha):
    b, s, d = x.shape
    t = b * s
    qb = Q_BLOCK
    nq_n, nkv = NSA_HEADS * HEAD_DIM, NSA_KV * HEAD_DIM
    nq_s, nkv_s = SWA_HEADS * HEAD_DIM, SWA_KV * HEAD_DIM
    widths = (nq_n, nkv, nkv, nkv, nkv, nkv, nkv, NSA_HEADS * 3, nq_s, nkv_s, nkv_s, 2 * d)
    offs = [0]
    for w in widths:
        offs.append(offs[-1] + w)
    col = lambda j: w_in[:, offs[j]:offs[j + 1]]
    (c_qn, c_kc, c_vc, c_ks, c_vs, c_kw, c_vw, c_gn, c_qs, c_k_s, c_v_s, c_gm) = map(col, range(12))
    w_rope = jnp.concatenate([c_qn, c_qs, c_ks, c_kw, c_k_s], axis=1).astype(BF16)
    w_plain = jnp.concatenate([c_kc, c_vc, c_vs, c_vw, c_v_s], axis=1).astype(BF16)
    gn_pad = LANES - NSA_HEADS * 3
    w_gate = jnp.concatenate([c_gm, c_gn, jnp.zeros((d, gn_pad), F32)], axis=1).astype(BF16)
    cos_t, sin_t = _rope_tables(s)

    x2 = x.reshape(t, d)
    qn_rot, qs_rot, kk_rot, qn_raw, plain, gates = _project(
        x2, w_rope, w_plain, w_gate, cos_t, sin_t, s, min(256, s))

    nc = (s - CMP_BLOCK) // CMP_STRIDE + 1
    ncp = s // CMP_STRIDE
    half = CMP_STRIDE * HEAD_DIM

    def halves(cols):
        v = cols.reshape(b, s, NSA_KV, HEAD_DIM).transpose(0, 2, 1, 3)
        return v.reshape(b * NSA_KV, ncp, half)

    t2 = jnp.stack([halves(plain[:, 0:nkv]), halves(plain[:, nkv:2 * nkv])])
    t_lo = t2
    t_hi = jnp.concatenate([t2[:, :, 1:], jnp.zeros_like(t2[:, :, :1])], axis=2)
    pe2 = jnp.stack([k_pe.reshape(2, half), v_pe.reshape(2, half)])
    w1 = jnp.stack([k_w1, v_w1]).astype(BF16)
    w2 = jnp.stack([k_w2, v_w2]).astype(BF16)
    kvc = _compress(t_lo, t_hi, pe2, w1, w2).reshape(2, b, NSA_KV, ncp, HEAD_DIM)

    nsel = s // SEL_BLOCK
    nselp = -(-nsel // LANES) * LANES
    cstart = jnp.arange(ncp) * CMP_STRIDE
    sstart = jnp.arange(nselp) * SEL_BLOCK
    overlap = ((cstart[:, None] < sstart[None, :] + SEL_BLOCK)
               & (cstart[:, None] + CMP_BLOCK > sstart[None, :])
               & (jnp.arange(ncp)[:, None] < nc) & (jnp.arange(nselp)[None, :] < nsel))
    o_cmp, notsel = _cmp_attention(qn_raw.reshape(b, s, nq_n), kvc[0], kvc[1],
                                   overlap.T.astype(BF16), qb)

    def group_major(cols):
        return cols.reshape(b, s, NSA_KV, HEAD_DIM).transpose(0, 2, 1, 3)

    k_sel = group_major(kk_rot[:, 0:nkv])
    v_sel = group_major(plain[:, 2 * nkv:3 * nkv])
    onehot = (jnp.arange(s)[:, None] // SEL_BLOCK == jnp.arange(nselp)[None, :])
    k_tail = jnp.concatenate([jnp.zeros((s, LANES - HEAD_DIM), F32),
                              jnp.where(onehot, SEL_PENALTY, 0.0)], axis=1).astype(BF16)
    k_aug = jnp.concatenate(
        [k_sel, jnp.broadcast_to(k_tail, (b, NSA_KV) + k_tail.shape)], axis=3)
    v_tail = jnp.zeros((LANES - HEAD_DIM, 1), BF16).at[0].set(1.0)
    vt_aug = jnp.concatenate(
        [v_sel.transpose(0, 1, 3, 2),
         jnp.broadcast_to(v_tail, (b, NSA_KV, LANES - HEAD_DIM, s))], axis=2)
    o_sel = _sel_attention(qn_rot.reshape(b, s, nq_n), notsel, k_aug, vt_aug, qb, min(512, s))

    o_win = _band_attention(qn_rot.reshape(b, s, nq_n), kk_rot[:, nkv:2 * nkv].reshape(b, s, nkv),
                            plain[:, 3 * nkv:4 * nkv].reshape(b, s, nkv), NSA_WINDOW, None, qb)
    o_swa = _band_attention(qs_rot.reshape(b, s, nq_s),
                            kk_rot[:, 2 * nkv:2 * nkv + nkv_s].reshape(b, s, nkv_s),
                            plain[:, 4 * nkv:4 * nkv + nkv_s].reshape(b, s, nkv_s),
                            SWA_WINDOW, sinks, qb)

    gi = jnp.arange(LANES)
    ci = jnp.arange(3 * nq_n)
    expand = ((gi[:, None] // 3 == (ci[None, :] % nq_n) // HEAD_DIM)
              & (gi[:, None] % 3 == ci[None, :] // nq_n)
              & (gi[:, None] < NSA_HEADS * 3)).astype(BF16)
    wr_pad = jnp.pad(w_router, ((0, 0), (0, LANES - N_EXPERTS)))
    wr_hi = wr_pad.astype(BF16)
    wr_lo = (wr_pad - wr_hi.astype(F32)).astype(BF16)
    b_r = jnp.concatenate([b_router, jnp.full((LANES - N_EXPERTS,), -jnp.inf, F32)]).reshape(1, LANES)
    tm = min(256, t)
    h, top_e, top_g, rank, counts = _merge(
        o_cmp.reshape(t, nq_n), o_sel.reshape(t, nq_n), o_win.reshape(t, nq_n),
        o_swa.reshape(t, nq_s), gates, x2, expand, w_br_nsa.astype(BF16), w_br_swa.astype(BF16),
        w_out.astype(BF16), ln1_g.reshape(1, d), ln1_b.reshape(1, d), wr_hi, wr_lo, b_r,
        alpha, tm)

    dest, blk_e, n_used, n_rows = _moe_plan(
        top_e[:, :TOP_K], rank[:, :TOP_K], counts[0, :N_EXPERTS].astype(I32), t)
    src = _rowmap(dest, n_rows)
    ys = _moe_blocks(blk_e, n_used, src, h, w_e_in, b_e_in.reshape(N_EXPERTS, 1, -1),
                     w_e_out, b_e_out.reshape(N_EXPERTS, 1, -1))
    out = _final(dest, h, top_g, ys, ln2_g.reshape(1, d), ln2_b.reshape(1, d), alpha, tm)
    return out.reshape(b, s, d)


def kernel(x, w_in, nsa_k_pe, nsa_k_w1, nsa_k_w2, nsa_v_pe, nsa_v_w1, nsa_v_w2, swa_sinks, w_br_nsa, w_br_swa, w_out, ln1_g, ln1_b, w_router, b_router, w_expert_in, b_expert_in, w_expert_out, b_expert_out, ln2_g, ln2_b):
    depth = w_in.shape[0]
    alpha = (2.0 * depth) ** 0.25
    for l in range(depth):
        x = _layer(x, w_in[l], nsa_k_pe[l], nsa_k_w1[l], nsa_k_w2[l], nsa_v_pe[l], nsa_v_w1[l],
                   nsa_v_w2[l], swa_sinks[l], w_br_nsa[l], w_br_swa[l], w_out[l], ln1_g[l],
                   ln1_b[l], w_router[l], b_router[l], w_expert_in[l], b_expert_in[l],
                   w_expert_out[l], b_expert_out[l], ln2_g[l], ln2_b[l], alpha)
    return x
```

```python
import functools

import jax
import jax.numpy as jnp
from jax import lax
from jax.experimental import pallas as pl
from jax.experimental.pallas import tpu as pltpu

BF16 = jnp.bfloat16
F32 = jnp.float32
I32 = jnp.int32

HEAD_DIM = 64
NSA_HEADS = 8
NSA_KV = 2
CMP_BLOCK = 32
CMP_STRIDE = 16
SEL_BLOCK = 64
SEL_TOPN = 16
NSA_WINDOW = 512
SWA_HEADS = 8
SWA_KV = 2
SWA_WINDOW = 128
Q_BLOCK = 128
ROPE_THETA = 10000.0
N_EXPERTS = 32
TOP_K = 4
SWIGLU_LIMIT = 7.0
SWIGLU_ALPHA = 1.702
LN_EPS = 1e-5

LANES = 128
MASKED = -1e30
M_INIT = -1e29
SEL_PENALTY = -(2.0 ** 100)
VMEM_LIMIT = 52 * 1024 * 1024
MOE_ROWS = 256
BAND_STEP_QUERIES = 512

R_NSA = NSA_HEADS // NSA_KV
R_SWA = SWA_HEADS // SWA_KV
NT_DIMS = (((1,), (1,)), ((), ()))


def _params(*sem):
    return pltpu.CompilerParams(dimension_semantics=sem, vmem_limit_bytes=VMEM_LIMIT)


def _full(shape):
    n = len(shape)
    return pl.BlockSpec(shape, lambda *_: (0,) * n)


def _proj_kernel(x_ref, wr_ref, wp_ref, wg_ref, cos_ref, sin_ref,
                 qn_rot_ref, qs_rot_ref, kk_rot_ref, qn_raw_ref, plain_ref, gates_ref):
    xb = x_ref[...].astype(BF16)
    acc = jnp.dot(xb, wr_ref[...], preferred_element_type=F32)
    cos = cos_ref[...]
    sin = sin_ref[...]
    lane = lax.broadcasted_iota(I32, cos.shape, 1)
    first_half = (lane & (HEAD_DIM - 1)) < HEAD_DIM // 2

    def rope(t):
        partner = jnp.where(first_half, pltpu.roll(t, LANES - HEAD_DIM // 2, 1),
                            pltpu.roll(t, HEAD_DIM // 2, 1))
        return (t * cos + partner * sin).astype(BF16)

    nq = qn_rot_ref.shape[1] // LANES
    ns = qs_rot_ref.shape[1] // LANES
    nk = kk_rot_ref.shape[1] // LANES
    for c in range(nq):
        qn_rot_ref[:, c * LANES:(c + 1) * LANES] = rope(acc[:, c * LANES:(c + 1) * LANES])
    for c in range(ns):
        o = (nq + c) * LANES
        qs_rot_ref[:, c * LANES:(c + 1) * LANES] = rope(acc[:, o:o + LANES])
    for c in range(nk):
        o = (nq + ns + c) * LANES
        kk_rot_ref[:, c * LANES:(c + 1) * LANES] = rope(acc[:, o:o + LANES])
    qn_raw_ref[...] = acc[:, :nq * LANES].astype(BF16)
    plain_ref[...] = jnp.dot(xb, wp_ref[...], preferred_element_type=F32).astype(BF16)
    gates_ref[...] = jnp.dot(xb, wg_ref[...], preferred_element_type=F32)


def _project(x2, w_rope, w_plain, w_gate, cos_t, sin_t, seq, tm):
    t, d = x2.shape
    nr, npl, ng = w_rope.shape[1], w_plain.shape[1], w_gate.shape[1]
    nqn, nqs = NSA_HEADS * HEAD_DIM, SWA_HEADS * HEAD_DIM
    nkk = nr - nqn - nqs
    spb = seq // tm
    row = lambda i: (i, 0)
    return pl.pallas_call(
        _proj_kernel,
        grid=(t // tm,),
        in_specs=[pl.BlockSpec((tm, d), row), _full(w_rope.shape), _full(w_plain.shape),
                  _full(w_gate.shape),
                  pl.BlockSpec((tm, LANES), lambda i: (i % spb, 0)),
                  pl.BlockSpec((tm, LANES), lambda i: (i % spb, 0))],
        out_specs=[pl.BlockSpec((tm, nqn), row), pl.BlockSpec((tm, nqs), row),
                   pl.BlockSpec((tm, nkk), row), pl.BlockSpec((tm, nqn), row),
                   pl.BlockSpec((tm, npl), row), pl.BlockSpec((tm, ng), row)],
        out_shape=[jax.ShapeDtypeStruct((t, nqn), BF16), jax.ShapeDtypeStruct((t, nqs), BF16),
                   jax.ShapeDtypeStruct((t, nkk), BF16), jax.ShapeDtypeStruct((t, nqn), BF16),
                   jax.ShapeDtypeStruct((t, npl), BF16), jax.ShapeDtypeStruct((t, ng), F32)],
        compiler_params=_params("parallel"),
        name="proj",
    )(x2, w_rope, w_plain, w_gate, cos_t, sin_t)


def _compress_kernel(a_ref, b_ref, pe_ref, w1_ref, w2_ref, out_ref):
    half = a_ref.shape[1]
    a = (a_ref[...].astype(F32) + pe_ref[0:1, :]).astype(BF16)
    b = (b_ref[...].astype(F32) + pe_ref[1:2, :]).astype(BF16)
    hid = jnp.dot(a, w1_ref[0:half, :], preferred_element_type=F32)
    hid = hid + jnp.dot(b, w1_ref[half:2 * half, :], preferred_element_type=F32)
    act = jax.nn.gelu(hid).astype(BF16)
    out_ref[...] = jnp.dot(act, w2_ref[...], preferred_element_type=F32).astype(BF16)


def _compress(t_lo, t_hi, pe2, w1, w2):
    two, bg, ncp, half = t_lo.shape
    hid = w1.shape[2]
    blk = lambda shape: pl.BlockSpec((None, None) + shape, lambda j, i: (j, i, 0, 0))
    wsp = lambda shape: pl.BlockSpec((None,) + shape, lambda j, i: (j, 0, 0))
    return pl.pallas_call(
        _compress_kernel,
        grid=(two, bg),
        in_specs=[blk((ncp, half)), blk((ncp, half)), wsp((2, half)), wsp((2 * half, hid)),
                  wsp((hid, HEAD_DIM))],
        out_specs=blk((ncp, HEAD_DIM)),
        out_shape=jax.ShapeDtypeStruct((two, bg, ncp, HEAD_DIM), BF16),
        compiler_params=_params("parallel", "parallel"),
        name="compress",
    )(t_lo, t_hi, pe2, w1, w2)


def _stack_heads(q, g, r):
    return jnp.concatenate(
        [q[:, (g * r + j) * HEAD_DIM:(g * r + j + 1) * HEAD_DIM] for j in range(r)], axis=0)


def _unstack_heads(parts, r, qb):
    return jnp.concatenate([o[j * qb:(j + 1) * qb] for o in parts for j in range(r)], axis=1)


def _unstack_heads_t(parts, r, qb):
    blocks = []
    for o in parts:
        for j in range(0, r, 2):
            pair = jnp.concatenate([o[:, j * qb:(j + 1) * qb], o[:, (j + 1) * qb:(j + 2) * qb]],
                                   axis=0)
            blocks.append(pair.T)
    return jnp.concatenate(blocks, axis=1)


def _topk_mask_cols(vals, k):
    n = vals.shape[0]
    row = lax.broadcasted_iota(I32, vals.shape, 0).astype(F32)
    taken = jnp.zeros(vals.shape, F32)
    work = vals
    for _ in range(k):
        mx = jnp.max(work, axis=0, keepdims=True)
        first = jnp.min(jnp.where(work == mx, row, float(n)), axis=0, keepdims=True)
        pick = row == first
        taken = jnp.where(pick, 1.0, taken)
        work = jnp.where(pick, -jnp.inf, work)
    return taken > 0.5


def _cmp_kernel(q_ref, kc_ref, vct_ref, ovt_ref, o_ref, notsel_ref):
    qb = q_ref.shape[0]
    ncp = kc_ref.shape[1]
    nselp = ovt_ref.shape[0]
    i = pl.program_id(1)
    rows = R_NSA * qb
    pos = i * qb + (lax.broadcasted_iota(I32, (1, rows), 1) & (qb - 1))
    cend = lax.broadcasted_iota(I32, (ncp, 1), 0) * CMP_STRIDE + (CMP_BLOCK - 1)
    visible = cend <= pos
    live = (pos >= CMP_BLOCK - 1).astype(F32)
    q = q_ref[...]
    outs, imps = [], []
    for g in range(NSA_KV):
        qg = _stack_heads(q, g, R_NSA) * (HEAD_DIM ** -0.5)
        st = lax.dot_general(kc_ref[g], qg, NT_DIMS, preferred_element_type=F32)
        st = jnp.where(visible, st, MASKED)
        e = jnp.exp(st - jnp.max(st, axis=0, keepdims=True))
        pt = e * (live / jnp.sum(e, axis=0, keepdims=True))
        outs.append(jnp.dot(vct_ref[g], pt.astype(BF16), preferred_element_type=F32))
        psum = pt[:, 0:qb]
        for j in range(1, R_NSA):
            psum = psum + pt[:, j * qb:(j + 1) * qb]
        p_hi = psum.astype(BF16)
        p_lo = (psum - p_hi.astype(F32)).astype(BF16)
        imps.append(jnp.dot(ovt_ref[...], p_hi, preferred_element_type=F32)
                    + jnp.dot(ovt_ref[...], p_lo, preferred_element_type=F32))
    imp = jnp.concatenate(imps, axis=1)
    lane = lax.broadcasted_iota(I32, (1, NSA_KV * qb), 1)
    cur = (i * qb + (lane & (qb - 1))) >> 6
    jb = lax.broadcasted_iota(I32, (nselp, 1), 0)
    forced = (jb == 0) | (jb == cur) | (jb == cur - 1)
    imp = jnp.where(jb > cur, -1.0, jnp.where(forced, 1e6, imp))
    notsel = jnp.where(_topk_mask_cols(imp, SEL_TOPN), 0.0, 1.0)
    for g in range(NSA_KV):
        notsel_ref[g] = notsel[:, g * qb:(g + 1) * qb].T.astype(BF16)
    o_ref[...] = _unstack_heads_t(outs, R_NSA, qb)


def _cmp_attention(q_raw, kc, vc, overlap_t, qb):
    vc = vc.transpose(0, 1, 3, 2)
    b, s, hq = q_raw.shape
    _, g, ncp, dh = kc.shape
    nselp = overlap_t.shape[0]
    return pl.pallas_call(
        _cmp_kernel,
        grid=(b, s // qb),
        in_specs=[pl.BlockSpec((None, qb, hq), lambda bi, i: (bi, i, 0)),
                  pl.BlockSpec((None, g, ncp, dh), lambda bi, i: (bi, 0, 0, 0)),
                  pl.BlockSpec((None, g, dh, ncp), lambda bi, i: (bi, 0, 0, 0)),
                  _full(overlap_t.shape)],
        out_specs=[pl.BlockSpec((None, qb, hq), lambda bi, i: (bi, i, 0)),
                   pl.BlockSpec((None, g, qb, nselp), lambda bi, i: (bi, 0, i, 0))],
        out_shape=[jax.ShapeDtypeStruct((b, s, hq), F32),
                   jax.ShapeDtypeStruct((b, g, s, nselp), BF16)],
        compiler_params=_params("parallel", "parallel"),
        name="cmp_attn",
    )(q_raw, kc, vc, overlap_t)


def _sel_kernel(q_ref, notsel_ref, k_ref, vt_ref, o_ref, *score_bufs, tk):
    s_even, s_odd = score_bufs[:NSA_KV], score_bufs[NSA_KV:]
    qb = q_ref.shape[0]
    i = pl.program_id(1)
    rows = R_NSA * qb
    qpos = i * qb + (lax.broadcasted_iota(I32, (1, rows), 1) & (qb - 1))
    n_clear = (i * qb) // tk
    q = q_ref[...]
    q_augs = []
    for g in range(NSA_KV):
        qg = _stack_heads(q, g, R_NSA) * (HEAD_DIM ** -0.5)
        q_augs.append(jnp.concatenate(
            [qg, jnp.zeros((rows, LANES - HEAD_DIM), BF16),
             jnp.concatenate([notsel_ref[g]] * R_NSA, axis=0)], axis=1))

    def scores(kt, g):
        start = pl.multiple_of(kt * tk, tk)
        return lax.dot_general(k_ref[g, pl.ds(start, tk), :], q_augs[g], NT_DIMS,
                               preferred_element_type=F32)

    def consume(kt, g, st, m, acc, causal):
        start = pl.multiple_of(kt * tk, tk)
        if causal:
            kpos = start + lax.broadcasted_iota(I32, (tk, 1), 0)
            st = jnp.where(kpos <= qpos, st, MASKED)
        m_new = jnp.maximum(m, jnp.max(st, axis=0, keepdims=True))
        pt = jnp.exp(st - m_new).astype(BF16)
        acc = jnp.exp(m - m_new) * acc + jnp.dot(vt_ref[g, :, pl.ds(start, tk)], pt,
                                                preferred_element_type=F32)
        return m_new, acc

    def advance(kt, carry, cur, nxt):
        new = []
        for g in range(NSA_KV):
            nxt[g][...] = scores(kt + 1, g)
            new.append(consume(kt, g, cur[g][...], *carry[g], False))
        return tuple(new)

    def pair(j, carry):
        carry = advance(2 * j, carry, s_even, s_odd)
        return advance(2 * j + 1, carry, s_odd, s_even)

    def finish(carry, cur):
        outs = []
        for g in range(NSA_KV):
            _, acc = consume(n_clear, g, cur[g][...], *carry[g], True)
            outs.append(acc[:HEAD_DIM] / acc[HEAD_DIM:HEAD_DIM + 1])
        o_ref[...] = _unstack_heads_t(outs, R_NSA, qb)

    for g in range(NSA_KV):
        s_even[g][...] = scores(0, g)
    init = tuple((jnp.full((1, rows), M_INIT, F32), jnp.zeros((LANES, rows), F32))
                 for _ in range(NSA_KV))
    carry = lax.fori_loop(0, n_clear // 2, pair, init)

    @pl.when((n_clear & 1) == 0)
    def _():
        finish(carry, s_even)

    @pl.when((n_clear & 1) == 1)
    def _():
        finish(advance(n_clear - 1, carry, s_even, s_odd), s_odd)


def _sel_attention(q_rot, notsel, k_aug, vt_aug, qb, tk):
    b, s, hq = q_rot.shape
    _, g, _, kw = k_aug.shape
    nselp = notsel.shape[3]
    return pl.pallas_call(
        functools.partial(_sel_kernel, tk=tk),
        grid=(b, s // qb),
        in_specs=[pl.BlockSpec((None, qb, hq), lambda bi, i: (bi, i, 0)),
                  pl.BlockSpec((None, g, qb, nselp), lambda bi, i: (bi, 0, i, 0)),
                  pl.BlockSpec((None, g, s, kw), lambda bi, i: (bi, 0, 0, 0)),
                  pl.BlockSpec((None, g, LANES, s), lambda bi, i: (bi, 0, 0, 0))],
        out_specs=pl.BlockSpec((None, qb, hq), lambda bi, i: (bi, i, 0)),
        out_shape=jax.ShapeDtypeStruct((b, s, hq), F32),
        scratch_shapes=[pltpu.VMEM((tk, R_NSA * qb), F32)] * (2 * g),
        compiler_params=_params("parallel", "parallel"),
        name="sel_attn",
    )(q_rot, notsel, k_aug, vt_aug)


def _band_kernel(*refs, window, wlen, r, kv, qb, has_sinks):
    if has_sinks:
        sink_ref, q_ref, k_ref, vt_ref, o_ref = refs
    else:
        q_ref, k_ref, vt_ref, o_ref = refs
    nsub = q_ref.shape[0] // qb
    rows = r * qb
    lane = lax.broadcasted_iota(I32, (1, rows), 1)
    for sb in range(nsub):
        i = pl.program_id(1) * nsub + sb
        start = pl.multiple_of(jnp.maximum((i + 1) * qb - wlen, 0), qb)
        qpos = i * qb + (lane & (qb - 1))
        kpos = start + lax.broadcasted_iota(I32, (wlen, 1), 0)
        rel = qpos - kpos
        inside = (rel >= 0) & (rel < window)
        q = q_ref[sb * qb:(sb + 1) * qb, :]
        kw = k_ref[pl.ds(start, wlen), :]
        vtw = vt_ref[:, pl.ds(start, wlen)]
        outs = []
        for g in range(kv):
            qg = _stack_heads(q, g, r) * (HEAD_DIM ** -0.5)
            st = lax.dot_general(kw[:, g * HEAD_DIM:(g + 1) * HEAD_DIM], qg, NT_DIMS,
                                 preferred_element_type=F32)
            st = jnp.where(inside, st, MASKED)
            m = jnp.max(st, axis=0, keepdims=True)
            if has_sinks:
                sk = jnp.full((1, rows), sink_ref[g * r], F32)
                for j in range(1, r):
                    sk = jnp.where(lane >= j * qb, sink_ref[g * r + j], sk)
                m = jnp.maximum(m, sk)
            e = jnp.exp(st - m)
            den = jnp.sum(e, axis=0, keepdims=True)
            if has_sinks:
                den = den + jnp.exp(sk - m)
            ot = jnp.dot(vtw[g * HEAD_DIM:(g + 1) * HEAD_DIM, :], e.astype(BF16),
                         preferred_element_type=F32)
            outs.append(ot / den)
        o_ref[sb * qb:(sb + 1) * qb, :] = _unstack_heads_t(outs, r, qb)


def _band_attention(q_rot, k, v, window, sinks, qb):
    v = v.transpose(0, 2, 1)
    b, s, hq = q_rot.shape
    gk = k.shape[2]
    kv = gk // HEAD_DIM
    r = hq // gk
    back = -(-window // qb)
    wlen = (back + 1) * qb
    assert wlen <= s
    has_sinks = sinks is not None
    tq = min(BAND_STEP_QUERIES, s)
    in_specs = [pl.BlockSpec((None, tq, hq), lambda bi, i: (bi, i, 0)),
                pl.BlockSpec((None, s, gk), lambda bi, i: (bi, 0, 0)),
                pl.BlockSpec((None, gk, s), lambda bi, i: (bi, 0, 0))]
    args = [q_rot, k, v]
    if has_sinks:
        in_specs = [pl.BlockSpec(memory_space=pltpu.SMEM)] + in_specs
        args = [sinks.astype(F32)] + args
    return pl.pallas_call(
        functools.partial(_band_kernel, window=window, wlen=wlen, r=r, kv=kv, qb=qb,
                          has_sinks=has_sinks),
        grid=(b, s // tq),
        in_specs=in_specs,
        out_specs=pl.BlockSpec((None, tq, hq), lambda bi, i: (bi, i, 0)),
        out_shape=jax.ShapeDtypeStruct((b, s, hq), F32),
        compiler_params=_params("parallel", "parallel"),
        name="band_attn_sink" if has_sinks else "band_attn",
    )(*args)


def _layer_norm(v, g, b):
    mu = jnp.mean(v, axis=1, keepdims=True)
    c = v - mu
    var = jnp.mean(c * c, axis=1, keepdims=True)
    return c * lax.rsqrt(var + LN_EPS) * g + b


def _split_dot(a, w):
    hi = a.astype(BF16)
    lo = (a - hi.astype(F32)).astype(BF16)
    return (jnp.dot(hi, w, preferred_element_type=F32)
            + jnp.dot(lo, w, preferred_element_type=F32))


def _merge_kernel(ocmp_ref, osel_ref, owin_ref, oswa_ref, gates_ref, x_ref, exp_ref,
                  wbn_ref, wbs_ref, wo_ref, lng_ref, lnb_ref, wrh_ref, wrl_ref, br_ref,
                  h_ref, te_ref, tg_ref, tr_ref, counts_ref, cnt_ref, *, alpha):
    d = x_ref.shape[1]
    hq = ocmp_ref.shape[1]
    gates = gates_ref[...]
    gn = jax.nn.sigmoid(gates[:, 2 * d:])
    gexp = _split_dot(gn, exp_ref[...])
    o_nsa = (gexp[:, 0:hq] * ocmp_ref[...] + gexp[:, hq:2 * hq] * osel_ref[...]
             + gexp[:, 2 * hq:3 * hq] * owin_ref[...])
    y_nsa = jnp.dot(o_nsa.astype(BF16), wbn_ref[...], preferred_element_type=F32)
    y_swa = jnp.dot(oswa_ref[...].astype(BF16), wbs_ref[...], preferred_element_type=F32)
    gm = jax.nn.sigmoid(gates[:, :2 * d])
    mixed = gm[:, :d] * y_nsa + gm[:, d:] * y_swa
    z = jnp.dot(mixed.astype(BF16), wo_ref[...], preferred_element_type=F32)
    h = _layer_norm(alpha * x_ref[...] + z, lng_ref[...], lnb_ref[...])
    h_ref[...] = h
    h_hi = h.astype(BF16)
    h_lo = (h - h_hi.astype(F32)).astype(BF16)
    logits = (jnp.dot(h_hi, wrh_ref[...], preferred_element_type=F32)
              + jnp.dot(h_lo, wrh_ref[...], preferred_element_type=F32)
              + jnp.dot(h_hi, wrl_ref[...], preferred_element_type=F32)) + br_ref[...]
    col = lax.broadcasted_iota(I32, logits.shape, 1).astype(F32)
    work = logits
    vals, ids = [], []
    for _ in range(TOP_K):
        mx = jnp.max(work, axis=1, keepdims=True)
        first = jnp.min(jnp.where(work == mx, col, float(LANES)), axis=1, keepdims=True)
        vals.append(mx)
        ids.append(first)
        work = jnp.where(col == first, -jnp.inf, work)
    es = [jnp.exp(v - vals[0]) for v in vals]
    den = es[0]
    for e in es[1:]:
        den = den + e
    @pl.when(pl.program_id(0) == 0)
    def _():
        cnt_ref[...] = jnp.zeros(cnt_ref.shape, F32)

    tm = logits.shape[0]
    hits = jnp.zeros(logits.shape, F32)
    for k in range(TOP_K):
        hits = jnp.where(col == ids[k], 1.0, hits)
    earlier = (lax.broadcasted_iota(I32, (tm, tm), 1)
               < lax.broadcasted_iota(I32, (tm, tm), 0))
    before = jnp.dot(jnp.where(earlier, 1.0, 0.0).astype(BF16), hits.astype(BF16),
                     preferred_element_type=F32) + cnt_ref[...]
    cnt_ref[...] = cnt_ref[...] + jnp.sum(hits, axis=0, keepdims=True)
    counts_ref[...] = cnt_ref[...]
    te = jnp.zeros(logits.shape, F32)
    tg = jnp.zeros(logits.shape, F32)
    tr = jnp.zeros(logits.shape, F32)
    for k in range(TOP_K):
        rank = jnp.sum(jnp.where(col == ids[k], before, 0.0), axis=1, keepdims=True)
        te = jnp.where(col == float(k), ids[k], te)
        tg = jnp.where(col == float(k), es[k] / den, tg)
        tr = jnp.where(col == float(k), rank, tr)
    te_ref[...] = te.astype(I32)
    tg_ref[...] = tg
    tr_ref[...] = tr.astype(I32)


def _merge(o_cmp, o_sel, o_win, o_swa, gates, x2, expand, w_bn, w_bs, w_o, ln_g, ln_b,
           wr_hi, wr_lo, b_r, alpha, tm):
    t, d = x2.shape
    hq = o_cmp.shape[1]
    row = lambda i: (i, 0)
    tok = lambda n: pl.BlockSpec((tm, n), row)
    return pl.pallas_call(
        functools.partial(_merge_kernel, alpha=alpha),
        grid=(t // tm,),
        in_specs=[tok(hq), tok(hq), tok(hq), tok(hq), tok(gates.shape[1]), tok(d),
                  _full(expand.shape), _full(w_bn.shape), _full(w_bs.shape), _full(w_o.shape),
                  _full(ln_g.shape), _full(ln_b.shape), _full(wr_hi.shape), _full(wr_lo.shape),
                  _full(b_r.shape)],
        out_specs=[tok(d), tok(LANES), tok(LANES), tok(LANES), _full((1, LANES))],
        out_shape=[jax.ShapeDtypeStruct((t, d), F32), jax.ShapeDtypeStruct((t, LANES), I32),
                   jax.ShapeDtypeStruct((t, LANES), F32), jax.ShapeDtypeStruct((t, LANES), I32),
                   jax.ShapeDtypeStruct((1, LANES), F32)],
        scratch_shapes=[pltpu.VMEM((1, LANES), F32)],
        compiler_params=_params("arbitrary"),
        name="merge_ln_router",
    )(o_cmp, o_sel, o_win, o_swa, gates, x2, expand, w_bn, w_bs, w_o, ln_g, ln_b,
      wr_hi, wr_lo, b_r)


def _dispatch_kernel(pend_ref, padded_ref, dest_ref, h_ref, xs_ref, zbuf, sem, zsem):
    tm = h_ref.shape[0]
    n_rows = xs_ref.shape[0]

    @pl.when(pl.program_id(0) == 0)
    def _():
        zbuf[...] = jnp.zeros(zbuf.shape, F32)
        used_rows = pend_ref[N_EXPERTS - 1]

        def zero_copy(start):
            start = pl.multiple_of(start, MOE_ROWS)
            return pltpu.make_async_copy(zbuf, xs_ref.at[pl.ds(start, MOE_ROWS), :], zsem)

        blocks = [(padded_ref[e] > 0, pend_ref[e] - MOE_ROWS) for e in range(N_EXPERTS)]
        blocks += [(used_rows + j * MOE_ROWS < n_rows, used_rows + j * MOE_ROWS)
                   for j in range(N_EXPERTS)]
        for cond, start in blocks:
            @pl.when(cond)
            def _(start=start):
                zero_copy(start).start()
        for cond, start in blocks:
            @pl.when(cond)
            def _(start=start):
                zero_copy(start).wait()

    def issue(r, carry):
        for k in range(TOP_K):
            dst = dest_ref[r * TOP_K + k]
            pltpu.make_async_copy(h_ref.at[pl.ds(r, 1), :], xs_ref.at[pl.ds(dst, 1), :],
                                  sem).start(priority=k % 2)
        return carry

    lax.fori_loop(0, tm, issue, 0, unroll=8)
    for _ in range(TOP_K):
        pltpu.make_async_copy(h_ref, xs_ref.at[pl.ds(0, tm), :], sem).wait()


def _dispatch(pend, padded, dest, h, n_rows, tm):
    t, d = h.shape
    return pl.pallas_call(
        _dispatch_kernel,
        grid=(t // tm,),
        in_specs=[pl.BlockSpec(memory_space=pltpu.SMEM), pl.BlockSpec(memory_space=pltpu.SMEM),
                  pl.BlockSpec((tm * TOP_K,), lambda i: (i,), memory_space=pltpu.SMEM),
                  pl.BlockSpec((tm, d), lambda i: (i, 0))],
        out_specs=pl.BlockSpec(memory_space=pl.ANY),
        out_shape=jax.ShapeDtypeStruct((n_rows, d), F32),
        scratch_shapes=[pltpu.VMEM((MOE_ROWS, d), F32), pltpu.SemaphoreType.DMA(()),
                        pltpu.SemaphoreType.DMA(())],
        compiler_params=_params("arbitrary"),
        name="moe_dispatch",
    )(pend, padded, dest, h)


def _moe_kernel(blk_e_ref, n_used_ref, x_ref, wi_ref, bi_ref, wo_ref, bo_ref, y_ref,
                wi_bf, wo_bf):
    f = wo_ref.shape[0]
    b = pl.program_id(0)

    @pl.when((b == 0) | (blk_e_ref[b] != blk_e_ref[jnp.maximum(b - 1, 0)]))
    def _():
        wi_bf[...] = wi_ref[...].astype(BF16)
        wo_bf[...] = wo_ref[...].astype(BF16)

    @pl.when(b < n_used_ref[0])
    def _():
        hdn = jnp.dot(x_ref[...].astype(BF16), wi_bf[...],
                      preferred_element_type=F32) + bi_ref[...]
        hg = jnp.minimum(hdn[:, :f], SWIGLU_LIMIT)
        hu = jnp.clip(hdn[:, f:], -SWIGLU_LIMIT, SWIGLU_LIMIT)
        act = hg * jax.nn.sigmoid(SWIGLU_ALPHA * hg) * (hu + 1.0)
        y_ref[...] = jnp.dot(act.astype(BF16), wo_bf[...],
                             preferred_element_type=F32) + bo_ref[...]

    @pl.when(b >= n_used_ref[0])
    def _():
        y_ref[...] = jnp.zeros(y_ref.shape, F32)


def _moe_blocks(blk_e, n_used, xs, w_in, b_in, w_out, b_out):
    n_rows, d = xs.shape
    e, _, f2 = w_in.shape
    f = w_out.shape[1]
    n_blk = n_rows // MOE_ROWS
    grid_spec = pltpu.PrefetchScalarGridSpec(
        num_scalar_prefetch=2,
        grid=(n_blk,),
        in_specs=[pl.BlockSpec((MOE_ROWS, d), lambda b, be, nu: (jnp.minimum(b, nu[0] - 1), 0)),
                  pl.BlockSpec((None, d, f2), lambda b, be, nu: (be[b], 0, 0)),
                  pl.BlockSpec((None, 1, f2), lambda b, be, nu: (be[b], 0, 0)),
                  pl.BlockSpec((None, f, d), lambda b, be, nu: (be[b], 0, 0)),
                  pl.BlockSpec((None, 1, d), lambda b, be, nu: (be[b], 0, 0))],
        out_specs=pl.BlockSpec((MOE_ROWS, d), lambda b, be, nu: (b, 0)),
        scratch_shapes=[pltpu.VMEM((d, f2), BF16), pltpu.VMEM((f, d), BF16)],
    )
    return pl.pallas_call(
        _moe_kernel,
        grid_spec=grid_spec,
        out_shape=jax.ShapeDtypeStruct((n_rows, d), F32),
        compiler_params=_params("arbitrary"),
        name="moe_experts",
    )(blk_e, n_used, xs, w_in, b_in, w_out, b_out)


def _final_kernel(dest_ref, h_ref, tg_ref, ys_ref, g_ref, b_ref, o_ref, ybuf, sem, *, alpha):
    tm = h_ref.shape[0]

    def issue(r, carry):
        for k in range(TOP_K):
            src = dest_ref[r * TOP_K + k]
            pltpu.make_async_copy(ys_ref.at[pl.ds(src, 1), :], ybuf.at[k, pl.ds(r, 1), :],
                                  sem).start(priority=k % 2)
        return carry

    lax.fori_loop(0, tm, issue, 0, unroll=8)
    for k in range(TOP_K):
        pltpu.make_async_copy(ys_ref.at[pl.ds(0, tm), :], ybuf.at[k], sem).wait()
    tg = tg_ref[...]
    f = tg[:, 0:1] * ybuf[0]
    for k in range(1, TOP_K):
        f = f + tg[:, k:k + 1] * ybuf[k]
    o_ref[...] = _layer_norm(alpha * h_ref[...] + f, g_ref[...], b_ref[...])


def _final(dest, h, top_g, ys, ln_g, ln_b, alpha, tm):
    t, d = h.shape
    return pl.pallas_call(
        functools.partial(_final_kernel, alpha=alpha),
        grid=(t // tm,),
        in_specs=[pl.BlockSpec((tm * TOP_K,), lambda i: (i,), memory_space=pltpu.SMEM),
                  pl.BlockSpec((tm, d), lambda i: (i, 0)),
                  pl.BlockSpec((tm, LANES), lambda i: (i, 0)),
                  pl.BlockSpec(memory_space=pl.ANY),
                  _full(ln_g.shape), _full(ln_b.shape)],
        out_specs=pl.BlockSpec((tm, d), lambda i: (i, 0)),
        out_shape=jax.ShapeDtypeStruct((t, d), F32),
        scratch_shapes=[pltpu.VMEM((TOP_K, tm, d), F32), pltpu.SemaphoreType.DMA(())],
        compiler_params=_params("arbitrary"),
        name="combine_ln",
    )(dest, h, top_g, ys, ln_g, ln_b)


def _rope_tables(s):
    half = HEAD_DIM // 2
    inv = ROPE_THETA ** (-jnp.arange(half, dtype=F32) / half)
    ang = jnp.arange(s, dtype=F32)[:, None] * inv[None, :]
    cos, sin = jnp.cos(ang), jnp.sin(ang)
    reps = LANES // HEAD_DIM
    cos_t = jnp.tile(jnp.concatenate([cos, cos], axis=1), (1, reps))
    sin_t = jnp.tile(jnp.concatenate([-sin, sin], axis=1), (1, reps))
    return cos_t, sin_t


def _moe_plan(top_e, rank, counts, t):
    a = t * TOP_K
    padded = (counts + MOE_ROWS - 1) // MOE_ROWS * MOE_ROWS
    pend = jnp.cumsum(padded)
    pstart = pend - padded
    experts = jnp.arange(N_EXPERTS, dtype=I32)
    base = jnp.sum(jnp.where(top_e[:, :, None] == experts, pstart, 0), axis=2)
    dest = (base + rank).astype(I32).reshape(a)
    n_blk = -(-a // MOE_ROWS) + N_EXPERTS
    blk_first = jnp.arange(n_blk, dtype=I32) * MOE_ROWS
    blk_e = jnp.minimum(jnp.sum(pend[None, :] <= blk_first[:, None], axis=1),
                        N_EXPERTS - 1).astype(I32)
    n_used = (pend[-1] // MOE_ROWS).astype(I32).reshape(1)
    return dest, blk_e, n_used, n_blk * MOE_ROWS, pend.astype(I32), padded.astype(I32)


def _layer(x, w_in, k_pe, k_w1, k_w2, v_pe, v_w1, v_w2, sinks, w_br_nsa, w_br_swa, w_out,
           ln1_g, ln1_b, w_router, b_router, w_e_in, b_e_in, w_e_out, b_e_out, ln2_g, ln2_b,
           alpha):
    b, s, d = x.shape
    t = b * s
    qb = Q_BLOCK
    nq_n, nkv = NSA_HEADS * HEAD_DIM, NSA_KV * HEAD_DIM
    nq_s, nkv_s = SWA_HEADS * HEAD_DIM, SWA_KV * HEAD_DIM
    widths = (nq_n, nkv, nkv, nkv, nkv, nkv, nkv, NSA_HEADS * 3, nq_s, nkv_s, nkv_s, 2 * d)
    offs = [0]
    for w in widths:
        offs.append(offs[-1] + w)
    col = lambda j: w_in[:, offs[j]:offs[j + 1]]
    (c_qn, c_kc, c_vc, c_ks, c_vs, c_kw, c_vw, c_gn, c_qs, c_k_s, c_v_s, c_gm) = map(col, range(12))
    w_rope = jnp.concatenate([c_qn, c_qs, c_ks, c_kw, c_k_s], axis=1).astype(BF16)
    w_plain = jnp.concatenate([c_kc, c_vc, c_vs, c_vw, c_v_s], axis=1).astype(BF16)
    gn_pad = LANES - NSA_HEADS * 3
    w_gate = jnp.concatenate([c_gm, c_gn, jnp.zeros((d, gn_pad), F32)], axis=1).astype(BF16)
    cos_t, sin_t = _rope_tables(s)

    x2 = x.reshape(t, d)
    qn_rot, qs_rot, kk_rot, qn_raw, plain, gates = _project(
        x2, w_rope, w_plain, w_gate, cos_t, sin_t, s, min(256, s))

    nc = (s - CMP_BLOCK) // CMP_STRIDE + 1
    ncp = s // CMP_STRIDE
    half = CMP_STRIDE * HEAD_DIM

    def halves(cols):
        v = cols.reshape(b, s, NSA_KV, HEAD_DIM).transpose(0, 2, 1, 3)
        return v.reshape(b * NSA_KV, ncp, half)

    t2 = jnp.stack([halves(plain[:, 0:nkv]), halves(plain[:, nkv:2 * nkv])])
    t_lo = t2
    t_hi = jnp.concatenate([t2[:, :, 1:], jnp.zeros_like(t2[:, :, :1])], axis=2)
    pe2 = jnp.stack([k_pe.reshape(2, half), v_pe.reshape(2, half)])
    w1 = jnp.stack([k_w1, v_w1]).astype(BF16)
    w2 = jnp.stack([k_w2, v_w2]).astype(BF16)
    kvc = _compress(t_lo, t_hi, pe2, w1, w2).reshape(2, b, NSA_KV, ncp, HEAD_DIM)

    nsel = s // SEL_BLOCK
    nselp = -(-nsel // LANES) * LANES
    cstart = jnp.arange(ncp) * CMP_STRIDE
    sstart = jnp.arange(nselp) * SEL_BLOCK
    overlap = ((cstart[:, None] < sstart[None, :] + SEL_BLOCK)
               & (cstart[:, None] + CMP_BLOCK > sstart[None, :])
               & (jnp.arange(ncp)[:, None] < nc) & (jnp.arange(nselp)[None, :] < nsel))
    o_cmp, notsel = _cmp_attention(qn_raw.reshape(b, s, nq_n), kvc[0], kvc[1],
                                   overlap.T.astype(BF16), qb)

    def group_major(cols):
        return cols.reshape(b, s, NSA_KV, HEAD_DIM).transpose(0, 2, 1, 3)

    k_sel = group_major(kk_rot[:, 0:nkv])
    v_sel = group_major(plain[:, 2 * nkv:3 * nkv])
    onehot = (jnp.arange(s)[:, None] // SEL_BLOCK == jnp.arange(nselp)[None, :])
    k_tail = jnp.concatenate([jnp.zeros((s, LANES - HEAD_DIM), F32),
                              jnp.where(onehot, SEL_PENALTY, 0.0)], axis=1).astype(BF16)
    k_aug = jnp.concatenate(
        [k_sel, jnp.broadcast_to(k_tail, (b, NSA_KV) + k_tail.shape)], axis=3)
    v_tail = jnp.zeros((LANES - HEAD_DIM, 1), BF16).at[0].set(1.0)
    vt_aug = jnp.concatenate(
        [v_sel.transpose(0, 1, 3, 2),
         jnp.broadcast_to(v_tail, (b, NSA_KV, LANES - HEAD_DIM, s))], axis=2)
    o_sel = _sel_attention(qn_rot.reshape(b, s, nq_n), notsel, k_aug, vt_aug, qb, min(512, s))

    o_win = _band_attention(qn_rot.reshape(b, s, nq_n), kk_rot[:, nkv:2 * nkv].reshape(b, s, nkv),
                            plain[:, 3 * nkv:4 * nkv].reshape(b, s, nkv), NSA_WINDOW, None, qb)
    o_swa = _band_attention(qs_rot.reshape(b, s, nq_s),
                            kk_rot[:, 2 * nkv:2 * nkv + nkv_s].reshape(b, s, nkv_s),
                            plain[:, 4 * nkv:4 * nkv + nkv_s].reshape(b, s, nkv_s),
                            SWA_WINDOW, sinks, qb)

    gi = jnp.arange(LANES)
    ci = jnp.arange(3 * nq_n)
    expand = ((gi[:, None] // 3 == (ci[None, :] % nq_n) // HEAD_DIM)
              & (gi[:, None] % 3 == ci[None, :] // nq_n)
              & (gi[:, None] < NSA_HEADS * 3)).astype(BF16)
    wr_pad = jnp.pad(w_router, ((0, 0), (0, LANES - N_EXPERTS)))
    wr_hi = wr_pad.astype(BF16)
    wr_lo = (wr_pad - wr_hi.astype(F32)).astype(BF16)
    b_r = jnp.concatenate([b_router, jnp.full((LANES - N_EXPERTS,), -jnp.inf, F32)]).reshape(1, LANES)
    tm = min(256, t)
    h, top_e, top_g, rank, counts = _merge(
        o_cmp.reshape(t, nq_n), o_sel.reshape(t, nq_n), o_win.reshape(t, nq_n),
        o_swa.reshape(t, nq_s), gates, x2, expand, w_br_nsa.astype(BF16), w_br_swa.astype(BF16),
        w_out.astype(BF16), ln1_g.reshape(1, d), ln1_b.reshape(1, d), wr_hi, wr_lo, b_r,
        alpha, tm)

    dest, blk_e, n_used, n_rows, pend, padded = _moe_plan(
        top_e[:, :TOP_K], rank[:, :TOP_K], counts[0, :N_EXPERTS].astype(I32), t)
    xs = _dispatch(pend, padded, dest, h, n_rows, tm)
    ys = _moe_blocks(blk_e, n_used, xs, w_e_in, b_e_in.reshape(N_EXPERTS, 1, -1),
                     w_e_out, b_e_out.reshape(N_EXPERTS, 1, -1))
    out = _final(dest, h, top_g, ys, ln2_g.reshape(1, d), ln2_b.reshape(1, d), alpha, tm)
    return out.reshape(b, s, d)


def kernel(x, w_in, nsa_k_pe, nsa_k_w1, nsa_k_w2, nsa_v_pe, nsa_v_w1, nsa_v_w2, swa_sinks, w_br_nsa, w_br_swa, w_out, ln1_g, ln1_b, w_router, b_router, w_expert_in, b_expert_in, w_expert_out, b_expert_out, ln2_g, ln2_b):
    depth = w_in.shape[0]
    alpha = (2.0 * depth) ** 0.25
    for l in range(depth):
        x = _layer(x, w_in[l], nsa_k_pe[l], nsa_k_w1[l], nsa_k_w2[l], nsa_v_pe[l], nsa_v_w1[l],
                   nsa_v_w2[l], swa_sinks[l], w_br_nsa[l], w_br_swa[l], w_out[l], ln1_g[l],
                   ln1_b[l], w_router[l], b_router[l], w_expert_in[l], b_expert_in[l],
                   w_expert_out[l], b_expert_out[l], ln2_g[l], ln2_b[l], alpha)
    return x
```

```python
import functools

import jax
import jax.numpy as jnp
from jax import lax
from jax.experimental import pallas as pl
from jax.experimental.pallas import tpu as pltpu

BF16 = jnp.bfloat16
F32 = jnp.float32
I32 = jnp.int32

HEAD_DIM = 64
NSA_HEADS = 8
NSA_KV = 2
CMP_BLOCK = 32
CMP_STRIDE = 16
SEL_BLOCK = 64
SEL_TOPN = 16
NSA_WINDOW = 512
SWA_HEADS = 8
SWA_KV = 2
SWA_WINDOW = 128
Q_BLOCK = 128
ROPE_THETA = 10000.0
N_EXPERTS = 32
TOP_K = 4
SWIGLU_LIMIT = 7.0
SWIGLU_ALPHA = 1.702
LN_EPS = 1e-5

LANES = 128
BF16_SUBLANES = 16
MASKED = -1e30
M_INIT = -1e29
SEL_PENALTY = -(2.0 ** 100)
VMEM_LIMIT = 52 * 1024 * 1024
MOE_ROWS = 256
BAND_STEP_QUERIES = 512
CMP_STEP_QUERIES = 512

R_NSA = NSA_HEADS // NSA_KV
R_SWA = SWA_HEADS // SWA_KV
NT_DIMS = (((1,), (1,)), ((), ()))


def _params(*sem):
    return pltpu.CompilerParams(dimension_semantics=sem, vmem_limit_bytes=VMEM_LIMIT)


def _full(shape):
    n = len(shape)
    return pl.BlockSpec(shape, lambda *_: (0,) * n)


def _proj_kernel(x_ref, wr_ref, wp_ref, wg_ref, cos_ref, sin_ref,
                 qn_rot_ref, qs_rot_ref, kk_rot_ref, qn_raw_ref, plain_ref, gates_ref):
    xb = x_ref[...].astype(BF16)
    acc = jnp.dot(xb, wr_ref[...], preferred_element_type=F32)
    cos = cos_ref[...]
    sin = sin_ref[...]
    lane = lax.broadcasted_iota(I32, cos.shape, 1)
    first_half = (lane & (HEAD_DIM - 1)) < HEAD_DIM // 2

    def rope(t):
        partner = jnp.where(first_half, pltpu.roll(t, LANES - HEAD_DIM // 2, 1),
                            pltpu.roll(t, HEAD_DIM // 2, 1))
        return (t * cos + partner * sin).astype(BF16)

    nq = qn_rot_ref.shape[1] // LANES
    ns = qs_rot_ref.shape[1] // LANES
    nk = kk_rot_ref.shape[1] // LANES
    for c in range(nq):
        qn_rot_ref[:, c * LANES:(c + 1) * LANES] = rope(acc[:, c * LANES:(c + 1) * LANES])
    for c in range(ns):
        o = (nq + c) * LANES
        qs_rot_ref[:, c * LANES:(c + 1) * LANES] = rope(acc[:, o:o + LANES])
    for c in range(nk):
        o = (nq + ns + c) * LANES
        kk_rot_ref[:, c * LANES:(c + 1) * LANES] = rope(acc[:, o:o + LANES])
    qn_raw_ref[...] = acc[:, :nq * LANES].astype(BF16)
    plain_ref[...] = jnp.dot(xb, wp_ref[...], preferred_element_type=F32).astype(BF16)
    gates_ref[...] = jnp.dot(xb, wg_ref[...], preferred_element_type=F32)


def _project(x2, w_rope, w_plain, w_gate, cos_t, sin_t, seq, tm):
    t, d = x2.shape
    nr, npl, ng = w_rope.shape[1], w_plain.shape[1], w_gate.shape[1]
    nqn, nqs = NSA_HEADS * HEAD_DIM, SWA_HEADS * HEAD_DIM
    nkk = nr - nqn - nqs
    spb = seq // tm
    row = lambda i: (i, 0)
    return pl.pallas_call(
        _proj_kernel,
        grid=(t // tm,),
        in_specs=[pl.BlockSpec((tm, d), row), _full(w_rope.shape), _full(w_plain.shape),
                  _full(w_gate.shape),
                  pl.BlockSpec((tm, LANES), lambda i: (i % spb, 0)),
                  pl.BlockSpec((tm, LANES), lambda i: (i % spb, 0))],
        out_specs=[pl.BlockSpec((tm, nqn), row), pl.BlockSpec((tm, nqs), row),
                   pl.BlockSpec((tm, nkk), row), pl.BlockSpec((tm, nqn), row),
                   pl.BlockSpec((tm, npl), row), pl.BlockSpec((tm, ng), row)],
        out_shape=[jax.ShapeDtypeStruct((t, nqn), BF16), jax.ShapeDtypeStruct((t, nqs), BF16),
                   jax.ShapeDtypeStruct((t, nkk), BF16), jax.ShapeDtypeStruct((t, nqn), BF16),
                   jax.ShapeDtypeStruct((t, npl), BF16), jax.ShapeDtypeStruct((t, ng), F32)],
        compiler_params=_params("parallel"),
        name="proj",
    )(x2, w_rope, w_plain, w_gate, cos_t, sin_t)


def _compress_kernel(a_ref, b_ref, pe_ref, w1_ref, w2_ref, out_ref):
    half = a_ref.shape[1]
    a = (a_ref[...].astype(F32) + pe_ref[0:1, :]).astype(BF16)
    b = (b_ref[...].astype(F32) + pe_ref[1:2, :]).astype(BF16)
    hid = jnp.dot(a, w1_ref[0:half, :], preferred_element_type=F32)
    hid = hid + jnp.dot(b, w1_ref[half:2 * half, :], preferred_element_type=F32)
    act = jax.nn.gelu(hid).astype(BF16)
    out_ref[...] = jnp.dot(act, w2_ref[...], preferred_element_type=F32).astype(BF16)


def _compress(t_lo, t_hi, pe2, w1, w2):
    two, bg, ncp, half = t_lo.shape
    hid = w1.shape[2]
    blk = lambda shape: pl.BlockSpec((None, None) + shape, lambda j, i: (j, i, 0, 0))
    wsp = lambda shape: pl.BlockSpec((None,) + shape, lambda j, i: (j, 0, 0))
    return pl.pallas_call(
        _compress_kernel,
        grid=(two, bg),
        in_specs=[blk((ncp, half)), blk((ncp, half)), wsp((2, half)), wsp((2 * half, hid)),
                  wsp((hid, HEAD_DIM))],
        out_specs=blk((ncp, HEAD_DIM)),
        out_shape=jax.ShapeDtypeStruct((two, bg, ncp, HEAD_DIM), BF16),
        compiler_params=_params("parallel", "parallel"),
        name="compress",
    )(t_lo, t_hi, pe2, w1, w2)


def _stack_heads(q, g, r):
    return jnp.concatenate(
        [q[:, (g * r + j) * HEAD_DIM:(g * r + j + 1) * HEAD_DIM] for j in range(r)], axis=0)


def _unstack_heads(parts, r, qb):
    return jnp.concatenate([o[j * qb:(j + 1) * qb] for o in parts for j in range(r)], axis=1)


def _unstack_heads_t(parts, r, qb):
    blocks = []
    for o in parts:
        for j in range(0, r, 2):
            pair = jnp.concatenate([o[:, j * qb:(j + 1) * qb], o[:, (j + 1) * qb:(j + 2) * qb]],
                                   axis=0)
            blocks.append(pair.T)
    return jnp.concatenate(blocks, axis=1)


def _topk_mask_cols(vals, k):
    n = vals.shape[0]
    row = lax.broadcasted_iota(I32, vals.shape, 0).astype(F32)
    taken = jnp.zeros(vals.shape, F32)
    work = vals
    for _ in range(k):
        mx = jnp.max(work, axis=0, keepdims=True)
        first = jnp.min(jnp.where(work == mx, row, float(n)), axis=0, keepdims=True)
        pick = row == first
        taken = jnp.where(pick, 1.0, taken)
        work = jnp.where(pick, -jnp.inf, work)
    return taken > 0.5


def _cmp_kernel(q_ref, kc_ref, vct_ref, ovt_ref, o_ref, notsel_ref):
    qb = q_ref.shape[0]
    ncp = kc_ref.shape[1]
    nselp = ovt_ref.shape[0]
    i = pl.program_id(1)
    rows = R_NSA * qb
    pos = i * qb + (lax.broadcasted_iota(I32, (1, rows), 1) & (qb - 1))
    cend = lax.broadcasted_iota(I32, (ncp, 1), 0) * CMP_STRIDE + (CMP_BLOCK - 1)
    bias = jnp.where(cend <= pos, 0.0, MASKED)
    live = (pos >= CMP_BLOCK - 1).astype(F32)
    q = q_ref[...]
    outs, imps = [], []
    for g in range(NSA_KV):
        qg = _stack_heads(q, g, R_NSA) * (HEAD_DIM ** -0.5)
        st = lax.dot_general(kc_ref[g], qg, NT_DIMS, preferred_element_type=F32)
        st = st + bias
        e = jnp.exp(st - jnp.max(st, axis=0, keepdims=True))
        pt = e * (live / jnp.sum(e, axis=0, keepdims=True))
        outs.append(jnp.dot(vct_ref[g], pt.astype(BF16), preferred_element_type=F32))
        psum = pt[:, 0:qb]
        for j in range(1, R_NSA):
            psum = psum + pt[:, j * qb:(j + 1) * qb]
        p_hi = psum.astype(BF16)
        p_lo = (psum - p_hi.astype(F32)).astype(BF16)
        imps.append(jnp.dot(ovt_ref[...], p_hi, preferred_element_type=F32)
                    + jnp.dot(ovt_ref[...], p_lo, preferred_element_type=F32))
    imp = jnp.concatenate(imps, axis=1)
    lane = lax.broadcasted_iota(I32, (1, NSA_KV * qb), 1)
    cur = (i * qb + (lane & (qb - 1))) >> 6
    jb = lax.broadcasted_iota(I32, (nselp, 1), 0)
    forced = (jb == 0) | (jb == cur) | (jb == cur - 1)
    imp = jnp.where(jb > cur, -1.0, jnp.where(forced, 1e6, imp))
    notsel = jnp.where(_topk_mask_cols(imp, SEL_TOPN), 0.0, 1.0)
    for g in range(NSA_KV):
        notsel_ref[g] = notsel[:, g * qb:(g + 1) * qb].T.astype(BF16)
    o_ref[...] = _unstack_heads_t(outs, R_NSA, qb)


def _cmp_attention(q_raw, kc, vc, overlap_t, qb):
    vc = vc.transpose(0, 1, 3, 2)
    b, s, hq = q_raw.shape
    _, g, ncp, dh = kc.shape
    nselp = overlap_t.shape[0]
    return pl.pallas_call(
        _cmp_kernel,
        grid=(b, s // qb),
        in_specs=[pl.BlockSpec((None, qb, hq), lambda bi, i: (bi, i, 0)),
                  pl.BlockSpec((None, g, ncp, dh), lambda bi, i: (bi, 0, 0, 0)),
                  pl.BlockSpec((None, g, dh, ncp), lambda bi, i: (bi, 0, 0, 0)),
                  _full(overlap_t.shape)],
        out_specs=[pl.BlockSpec((None, qb, hq), lambda bi, i: (bi, i, 0)),
                   pl.BlockSpec((None, g, qb, nselp), lambda bi, i: (bi, 0, i, 0))],
        out_shape=[jax.ShapeDtypeStruct((b, s, hq), F32),
                   jax.ShapeDtypeStruct((b, g, s, nselp), BF16)],
        compiler_params=_params("parallel", "parallel"),
        name="cmp_attn",
    )(q_raw, kc, vc, overlap_t)


def _sel_kernel(q_ref, notsel_ref, k_ref, vt_ref, o_ref, *score_bufs, tk):
    s_even, s_odd = score_bufs[:NSA_KV], score_bufs[NSA_KV:]
    qb = q_ref.shape[0]
    i = pl.program_id(1)
    rows = R_NSA * qb
    qpos = i * qb + (lax.broadcasted_iota(I32, (1, rows), 1) & (qb - 1))
    n_clear = (i * qb) // tk
    q = q_ref[...]
    q_augs = []
    for g in range(NSA_KV):
        qg = _stack_heads(q, g, R_NSA) * (HEAD_DIM ** -0.5)
        q_augs.append(jnp.concatenate(
            [qg, jnp.zeros((rows, LANES - HEAD_DIM), BF16),
             jnp.concatenate([notsel_ref[g]] * R_NSA, axis=0)], axis=1))

    def scores(kt, g):
        start = pl.multiple_of(kt * tk, tk)
        return lax.dot_general(k_ref[g, pl.ds(start, tk), :], q_augs[g], NT_DIMS,
                               preferred_element_type=F32)

    def consume(kt, g, st, m, acc, causal):
        start = pl.multiple_of(kt * tk, tk)
        if causal:
            kpos = start + lax.broadcasted_iota(I32, (tk, 1), 0)
            st = jnp.where(kpos <= qpos, st, MASKED)
        m_new = jnp.maximum(m, jnp.max(st, axis=0, keepdims=True))
        pt = jnp.exp(st - m_new).astype(BF16)
        acc = jnp.exp(m - m_new) * acc + jnp.dot(vt_ref[g, :, pl.ds(start, tk)], pt,
                                                preferred_element_type=F32)
        return m_new, acc

    def advance(kt, carry, cur, nxt):
        new = []
        for g in range(NSA_KV):
            nxt[g][...] = scores(kt + 1, g)
            new.append(consume(kt, g, cur[g][...], *carry[g], False))
        return tuple(new)

    def pair(j, carry):
        carry = advance(2 * j, carry, s_even, s_odd)
        return advance(2 * j + 1, carry, s_odd, s_even)

    def finish(carry, cur):
        outs = []
        for g in range(NSA_KV):
            _, acc = consume(n_clear, g, cur[g][...], *carry[g], True)
            outs.append(acc[:HEAD_DIM] / acc[HEAD_DIM:HEAD_DIM + 1])
        o_ref[...] = _unstack_heads_t(outs, R_NSA, qb)

    for g in range(NSA_KV):
        s_even[g][...] = scores(0, g)
    init = tuple((jnp.full((1, rows), M_INIT, F32), jnp.zeros((LANES, rows), F32))
                 for _ in range(NSA_KV))
    carry = lax.fori_loop(0, n_clear // 2, pair, init)

    @pl.when((n_clear & 1) == 0)
    def _():
        finish(carry, s_even)

    @pl.when((n_clear & 1) == 1)
    def _():
        finish(advance(n_clear - 1, carry, s_even, s_odd), s_odd)


def _sel_attention(q_rot, notsel, k_aug, vt_aug, qb, tk):
    b, s, hq = q_rot.shape
    _, g, _, kw = k_aug.shape
    nselp = notsel.shape[3]
    return pl.pallas_call(
        functools.partial(_sel_kernel, tk=tk),
        grid=(b, s // qb),
        in_specs=[pl.BlockSpec((None, qb, hq), lambda bi, i: (bi, i, 0)),
                  pl.BlockSpec((None, g, qb, nselp), lambda bi, i: (bi, 0, i, 0)),
                  pl.BlockSpec((None, g, s, kw), lambda bi, i: (bi, 0, 0, 0)),
                  pl.BlockSpec((None, g, LANES, s), lambda bi, i: (bi, 0, 0, 0))],
        out_specs=pl.BlockSpec((None, qb, hq), lambda bi, i: (bi, i, 0)),
        out_shape=jax.ShapeDtypeStruct((b, s, hq), F32),
        scratch_shapes=[pltpu.VMEM((tk, R_NSA * qb), F32)] * (2 * g),
        compiler_params=_params("parallel", "parallel"),
        name="sel_attn",
    )(q_rot, notsel, k_aug, vt_aug)


def _band_kernel(*refs, window, wlen, r, kv, qb, has_sinks):
    if has_sinks:
        sink_ref, q_ref, k_ref, vt_ref, o_ref = refs
    else:
        q_ref, k_ref, vt_ref, o_ref = refs
    nsub = q_ref.shape[0] // qb
    rows = r * qb
    nv = kv * HEAD_DIM
    lane = lax.broadcasted_iota(I32, (1, rows), 1)

    def mask_bias(i):
        start = jnp.maximum((i + 1) * qb - wlen, 0)
        rel = i * qb + (lane & (qb - 1)) - start - lax.broadcasted_iota(I32, (wlen, 1), 0)
        return jnp.where((rel >= 0) & (rel < window), 0.0, MASKED)

    def body(shared_bias):
        first = pl.program_id(1) * nsub
        if shared_bias:
            bias = mask_bias(first)
        for sb in range(nsub):
            i = first + sb
            if not shared_bias:
                bias = mask_bias(i)
            start = pl.multiple_of(jnp.maximum((i + 1) * qb - wlen, 0), qb)
            q = q_ref[sb * qb:(sb + 1) * qb, :]
            kw = k_ref[pl.ds(start, wlen), :]
            vtw = vt_ref[:, pl.ds(start, wlen)]
            outs = []
            for g in range(kv):
                qg = _stack_heads(q, g, r) * (HEAD_DIM ** -0.5)
                st = lax.dot_general(kw[:, g * HEAD_DIM:(g + 1) * HEAD_DIM], qg, NT_DIMS,
                                     preferred_element_type=F32) + bias
                m = jnp.max(st, axis=0, keepdims=True)
                if has_sinks:
                    sk = jnp.full((1, rows), sink_ref[g * r], F32)
                    for j in range(1, r):
                        sk = jnp.where(lane >= j * qb, sink_ref[g * r + j], sk)
                    m = jnp.maximum(m, sk)
                e = jnp.exp(st - m).astype(BF16)
                v_ones = jnp.concatenate([vtw[g * HEAD_DIM:(g + 1) * HEAD_DIM, :], vtw[nv:, :]],
                                         axis=0)
                ot = jnp.dot(v_ones, e, preferred_element_type=F32)
                den = ot[HEAD_DIM:HEAD_DIM + 1]
                if has_sinks:
                    den = den + jnp.exp(sk - m)
                outs.append(ot[:HEAD_DIM] / den)
            o_ref[sb * qb:(sb + 1) * qb, :] = _unstack_heads_t(outs, r, qb)

    if nsub * qb >= wlen - qb:
        @pl.when(pl.program_id(1) == 0)
        def _():
            body(False)

        @pl.when(pl.program_id(1) > 0)
        def _():
            body(True)
    else:
        body(False)


def _band_attention(q_rot, k, v, window, sinks, qb):
    b, s, hq = q_rot.shape
    gk = k.shape[2]
    v = jnp.concatenate([v.transpose(0, 2, 1), jnp.ones((b, BF16_SUBLANES, s), v.dtype)], axis=1)
    kv = gk // HEAD_DIM
    r = hq // gk
    back = -(-window // qb)
    wlen = (back + 1) * qb
    assert wlen <= s
    has_sinks = sinks is not None
    tq = min(BAND_STEP_QUERIES, s)
    in_specs = [pl.BlockSpec((None, tq, hq), lambda bi, i: (bi, i, 0)),
                pl.BlockSpec((None, s, gk), lambda bi, i: (bi, 0, 0)),
                pl.BlockSpec((None, gk + BF16_SUBLANES, s), lambda bi, i: (bi, 0, 0))]
    args = [q_rot, k, v]
    if has_sinks:
        in_specs = [pl.BlockSpec(memory_space=pltpu.SMEM)] + in_specs
        args = [sinks.astype(F32)] + args
    return pl.pallas_call(
        functools.partial(_band_kernel, window=window, wlen=wlen, r=r, kv=kv, qb=qb,
                          has_sinks=has_sinks),
        grid=(b, s // tq),
        in_specs=in_specs,
        out_specs=pl.BlockSpec((None, tq, hq), lambda bi, i: (bi, i, 0)),
        out_shape=jax.ShapeDtypeStruct((b, s, hq), F32),
        compiler_params=_params("parallel", "parallel"),
        name="band_attn_sink" if has_sinks else "band_attn",
    )(*args)


def _layer_norm(v, g, b):
    mu = jnp.mean(v, axis=1, keepdims=True)
    c = v - mu
    var = jnp.mean(c * c, axis=1, keepdims=True)
    return c * lax.rsqrt(var + LN_EPS) * g + b


def _split_dot(a, w):
    hi = a.astype(BF16)
    lo = (a - hi.astype(F32)).astype(BF16)
    return (jnp.dot(hi, w, preferred_element_type=F32)
            + jnp.dot(lo, w, preferred_element_type=F32))


def _merge_kernel(ocmp_ref, osel_ref, owin_ref, oswa_ref, gates_ref, x_ref, exp_ref,
                  wbn_ref, wbs_ref, wo_ref, lng_ref, lnb_ref, wrh_ref, wrl_ref, br_ref,
                  h_ref, te_ref, tg_ref, tr_ref, counts_ref, cnt_ref, *, alpha):
    d = x_ref.shape[1]
    hq = ocmp_ref.shape[1]
    gates = gates_ref[...]
    gn = jax.nn.sigmoid(gates[:, 2 * d:])
    gexp = _split_dot(gn, exp_ref[...])
    o_nsa = (gexp[:, 0:hq] * ocmp_ref[...] + gexp[:, hq:2 * hq] * osel_ref[...]
             + gexp[:, 2 * hq:3 * hq] * owin_ref[...])
    y_nsa = jnp.dot(o_nsa.astype(BF16), wbn_ref[...], preferred_element_type=F32)
    y_swa = jnp.dot(oswa_ref[...].astype(BF16), wbs_ref[...], preferred_element_type=F32)
    gm = jax.nn.sigmoid(gates[:, :2 * d])
    mixed = gm[:, :d] * y_nsa + gm[:, d:] * y_swa
    z = jnp.dot(mixed.astype(BF16), wo_ref[...], preferred_element_type=F32)
    h = _layer_norm(alpha * x_ref[...] + z, lng_ref[...], lnb_ref[...])
    h_ref[...] = h
    h_hi = h.astype(BF16)
    h_lo = (h - h_hi.astype(F32)).astype(BF16)
    logits = (jnp.dot(h_hi, wrh_ref[...], preferred_element_type=F32)
              + jnp.dot(h_lo, wrh_ref[...], preferred_element_type=F32)
              + jnp.dot(h_hi, wrl_ref[...], preferred_element_type=F32)) + br_ref[...]
    col = lax.broadcasted_iota(I32, logits.shape, 1).astype(F32)
    work = logits
    vals, ids = [], []
    for _ in range(TOP_K):
        mx = jnp.max(work, axis=1, keepdims=True)
        first = jnp.min(jnp.where(work == mx, col, float(LANES)), axis=1, keepdims=True)
        vals.append(mx)
        ids.append(first)
        work = jnp.where(col == first, -jnp.inf, work)
    es = [jnp.exp(v - vals[0]) for v in vals]
    den = es[0]
    for e in es[1:]:
        den = den + e
    @pl.when(pl.program_id(0) == 0)
    def _():
        cnt_ref[...] = jnp.zeros(cnt_ref.shape, F32)

    tm = logits.shape[0]
    hits = jnp.zeros(logits.shape, F32)
    for k in range(TOP_K):
        hits = jnp.where(col == ids[k], 1.0, hits)
    earlier = (lax.broadcasted_iota(I32, (tm, tm), 1)
               < lax.broadcasted_iota(I32, (tm, tm), 0))
    before = jnp.dot(jnp.where(earlier, 1.0, 0.0).astype(BF16), hits.astype(BF16),
                     preferred_element_type=F32) + cnt_ref[...]
    cnt_ref[...] = cnt_ref[...] + jnp.sum(hits, axis=0, keepdims=True)
    counts_ref[...] = cnt_ref[...]
    te = jnp.zeros(logits.shape, F32)
    tg = jnp.zeros(logits.shape, F32)
    tr = jnp.zeros(logits.shape, F32)
    for k in range(TOP_K):
        rank = jnp.sum(jnp.where(col == ids[k], before, 0.0), axis=1, keepdims=True)
        te = jnp.where(col == float(k), ids[k], te)
        tg = jnp.where(col == float(k), es[k] / den, tg)
        tr = jnp.where(col == float(k), rank, tr)
    te_ref[...] = te.astype(I32)
    tg_ref[...] = tg
    tr_ref[...] = tr.astype(I32)


def _merge(o_cmp, o_sel, o_win, o_swa, gates, x2, expand, w_bn, w_bs, w_o, ln_g, ln_b,
           wr_hi, wr_lo, b_r, alpha, tm):
    t, d = x2.shape
    hq = o_cmp.shape[1]
    row = lambda i: (i, 0)
    tok = lambda n: pl.BlockSpec((tm, n), row)
    return pl.pallas_call(
        functools.partial(_merge_kernel, alpha=alpha),
        grid=(t // tm,),
        in_specs=[tok(hq), tok(hq), tok(hq), tok(hq), tok(gates.shape[1]), tok(d),
                  _full(expand.shape), _full(w_bn.shape), _full(w_bs.shape), _full(w_o.shape),
                  _full(ln_g.shape), _full(ln_b.shape), _full(wr_hi.shape), _full(wr_lo.shape),
                  _full(b_r.shape)],
        out_specs=[tok(d), tok(LANES), tok(LANES), tok(LANES), _full((1, LANES))],
        out_shape=[jax.ShapeDtypeStruct((t, d), F32), jax.ShapeDtypeStruct((t, LANES), I32),
                   jax.ShapeDtypeStruct((t, LANES), F32), jax.ShapeDtypeStruct((t, LANES), I32),
                   jax.ShapeDtypeStruct((1, LANES), F32)],
        scratch_shapes=[pltpu.VMEM((1, LANES), F32)],
        compiler_params=_params("arbitrary"),
        name="merge_ln_router",
    )(o_cmp, o_sel, o_win, o_swa, gates, x2, expand, w_bn, w_bs, w_o, ln_g, ln_b,
      wr_hi, wr_lo, b_r)


def _dispatch_kernel(pend_ref, padded_ref, dest_ref, h_ref, xs_ref, zbuf, sem, zsem):
    tm = h_ref.shape[0]
    n_rows = xs_ref.shape[0]

    @pl.when(pl.program_id(0) == 0)
    def _():
        zbuf[...] = jnp.zeros(zbuf.shape, F32)
        used_rows = pend_ref[N_EXPERTS - 1]

        def zero_copy(start):
            start = pl.multiple_of(start, MOE_ROWS)
            return pltpu.make_async_copy(zbuf, xs_ref.at[pl.ds(start, MOE_ROWS), :], zsem)

        blocks = [(padded_ref[e] > 0, pend_ref[e] - MOE_ROWS) for e in range(N_EXPERTS)]
        blocks += [(used_rows + j * MOE_ROWS < n_rows, used_rows + j * MOE_ROWS)
                   for j in range(N_EXPERTS)]
        for cond, start in blocks:
            @pl.when(cond)
            def _(start=start):
                zero_copy(start).start()
        for cond, start in blocks:
            @pl.when(cond)
            def _(start=start):
                zero_copy(start).wait()

    def issue(r, carry):
        for k in range(TOP_K):
            dst = dest_ref[r * TOP_K + k]
            pltpu.make_async_copy(h_ref.at[pl.ds(r, 1), :], xs_ref.at[pl.ds(dst, 1), :],
                                  sem).start(priority=k % 2)
        return carry

    lax.fori_loop(0, tm, issue, 0, unroll=8)
    for _ in range(TOP_K):
        pltpu.make_async_copy(h_ref, xs_ref.at[pl.ds(0, tm), :], sem).wait()


def _dispatch(pend, padded, dest, h, n_rows, tm):
    t, d = h.shape
    return pl.pallas_call(
        _dispatch_kernel,
        grid=(t // tm,),
        in_specs=[pl.BlockSpec(memory_space=pltpu.SMEM), pl.BlockSpec(memory_space=pltpu.SMEM),
                  pl.BlockSpec((tm * TOP_K,), lambda i: (i,), memory_space=pltpu.SMEM),
                  pl.BlockSpec((tm, d), lambda i: (i, 0))],
        out_specs=pl.BlockSpec(memory_space=pl.ANY),
        out_shape=jax.ShapeDtypeStruct((n_rows, d), F32),
        scratch_shapes=[pltpu.VMEM((MOE_ROWS, d), F32), pltpu.SemaphoreType.DMA(()),
                        pltpu.SemaphoreType.DMA(())],
        compiler_params=_params("arbitrary"),
        name="moe_dispatch",
    )(pend, padded, dest, h)


def _moe_kernel(blk_e_ref, n_used_ref, x_ref, wi_ref, bi_ref, wo_ref, bo_ref, y_ref,
                wi_bf, wo_bf):
    f = wo_ref.shape[0]
    b = pl.program_id(0)

    @pl.when((b == 0) | (blk_e_ref[b] != blk_e_ref[jnp.maximum(b - 1, 0)]))
    def _():
        wi_bf[...] = wi_ref[...].astype(BF16)
        wo_bf[...] = wo_ref[...].astype(BF16)

    @pl.when(b < n_used_ref[0])
    def _():
        hdn = jnp.dot(x_ref[...].astype(BF16), wi_bf[...],
                      preferred_element_type=F32) + bi_ref[...]
        hg = jnp.minimum(hdn[:, :f], SWIGLU_LIMIT)
        hu = jnp.clip(hdn[:, f:], -SWIGLU_LIMIT, SWIGLU_LIMIT)
        act = hg * jax.nn.sigmoid(SWIGLU_ALPHA * hg) * (hu + 1.0)
        y_ref[...] = jnp.dot(act.astype(BF16), wo_bf[...],
                             preferred_element_type=F32) + bo_ref[...]

    @pl.when(b >= n_used_ref[0])
    def _():
        y_ref[...] = jnp.zeros(y_ref.shape, F32)


def _moe_blocks(blk_e, n_used, xs, w_in, b_in, w_out, b_out):
    n_rows, d = xs.shape
    e, _, f2 = w_in.shape
    f = w_out.shape[1]
    n_blk = n_rows // MOE_ROWS
    grid_spec = pltpu.PrefetchScalarGridSpec(
        num_scalar_prefetch=2,
        grid=(n_blk,),
        in_specs=[pl.BlockSpec((MOE_ROWS, d), lambda b, be, nu: (jnp.minimum(b, nu[0] - 1), 0)),
                  pl.BlockSpec((None, d, f2), lambda b, be, nu: (be[b], 0, 0)),
                  pl.BlockSpec((None, 1, f2), lambda b, be, nu: (be[b], 0, 0)),
                  pl.BlockSpec((None, f, d), lambda b, be, nu: (be[b], 0, 0)),
                  pl.BlockSpec((None, 1, d), lambda b, be, nu: (be[b], 0, 0))],
        out_specs=pl.BlockSpec((MOE_ROWS, d), lambda b, be, nu: (b, 0)),
        scratch_shapes=[pltpu.VMEM((d, f2), BF16), pltpu.VMEM((f, d), BF16)],
    )
    return pl.pallas_call(
        _moe_kernel,
        grid_spec=grid_spec,
        out_shape=jax.ShapeDtypeStruct((n_rows, d), F32),
        compiler_params=_params("arbitrary"),
        name="moe_experts",
    )(blk_e, n_used, xs, w_in, b_in, w_out, b_out)


def _final_kernel(dest_ref, h_ref, tg_ref, ys_ref, g_ref, b_ref, o_ref, ybuf, sem, *, alpha):
    tm = h_ref.shape[0]

    def issue(r, carry):
        for k in range(TOP_K):
            src = dest_ref[r * TOP_K + k]
            pltpu.make_async_copy(ys_ref.at[pl.ds(src, 1), :], ybuf.at[k, pl.ds(r, 1), :],
                                  sem).start(priority=k % 2)
        return carry

    lax.fori_loop(0, tm, issue, 0, unroll=8)
    for k in range(TOP_K):
        pltpu.make_async_copy(ys_ref.at[pl.ds(0, tm), :], ybuf.at[k], sem).wait()
    tg = tg_ref[...]
    f = tg[:, 0:1] * ybuf[0]
    for k in range(1, TOP_K):
        f = f + tg[:, k:k + 1] * ybuf[k]
    o_ref[...] = _layer_norm(alpha * h_ref[...] + f, g_ref[...], b_ref[...])


def _final(dest, h, top_g, ys, ln_g, ln_b, alpha, tm):
    t, d = h.shape
    return pl.pallas_call(
        functools.partial(_final_kernel, alpha=alpha),
        grid=(t // tm,),
        in_specs=[pl.BlockSpec((tm * TOP_K,), lambda i: (i,), memory_space=pltpu.SMEM),
                  pl.BlockSpec((tm, d), lambda i: (i, 0)),
                  pl.BlockSpec((tm, LANES), lambda i: (i, 0)),
                  pl.BlockSpec(memory_space=pl.ANY),
                  _full(ln_g.shape), _full(ln_b.shape)],
        out_specs=pl.BlockSpec((tm, d), lambda i: (i, 0)),
        out_shape=jax.ShapeDtypeStruct((t, d), F32),
        scratch_shapes=[pltpu.VMEM((TOP_K, tm, d), F32), pltpu.SemaphoreType.DMA(())],
        compiler_params=_params("arbitrary"),
        name="combine_ln",
    )(dest, h, top_g, ys, ln_g, ln_b)


def _rope_tables(s):
    half = HEAD_DIM // 2
    inv = ROPE_THETA ** (-jnp.arange(half, dtype=F32) / half)
    ang = jnp.arange(s, dtype=F32)[:, None] * inv[None, :]
    cos, sin = jnp.cos(ang), jnp.sin(ang)
    reps = LANES // HEAD_DIM
    cos_t = jnp.tile(jnp.concatenate([cos, cos], axis=1), (1, reps))
    sin_t = jnp.tile(jnp.concatenate([-sin, sin], axis=1), (1, reps))
    return cos_t, sin_t


def _moe_plan(top_e, rank, counts, t):
    a = t * TOP_K
    padded = (counts + MOE_ROWS - 1) // MOE_ROWS * MOE_ROWS
    pend = jnp.cumsum(padded)
    pstart = pend - padded
    experts = jnp.arange(N_EXPERTS, dtype=I32)
    base = jnp.sum(jnp.where(top_e[:, :, None] == experts, pstart, 0), axis=2)
    dest = (base + rank).astype(I32).reshape(a)
    n_blk = -(-a // MOE_ROWS) + N_EXPERTS
    blk_first = jnp.arange(n_blk, dtype=I32) * MOE_ROWS
    blk_e = jnp.minimum(jnp.sum(pend[None, :] <= blk_first[:, None], axis=1),
                        N_EXPERTS - 1).astype(I32)
    n_used = (pend[-1] // MOE_ROWS).astype(I32).reshape(1)
    return dest, blk_e, n_used, n_blk * MOE_ROWS, pend.astype(I32), padded.astype(I32)


def _layer(x, w_in, k_pe, k_w1, k_w2, v_pe, v_w1, v_w2, sinks, w_br_nsa, w_br_swa, w_out,
           ln1_g, ln1_b, w_router, b_router, w_e_in, b_e_in, w_e_out, b_e_out, ln2_g, ln2_b,
           alpha):
    b, s, d = x.shape
    t = b * s
    qb = Q_BLOCK
    nq_n, nkv = NSA_HEADS * HEAD_DIM, NSA_KV * HEAD_DIM
    nq_s, nkv_s = SWA_HEADS * HEAD_DIM, SWA_KV * HEAD_DIM
    widths = (nq_n, nkv, nkv, nkv, nkv, nkv, nkv, NSA_HEADS * 3, nq_s, nkv_s, nkv_s, 2 * d)
    offs = [0]
    for w in widths:
        offs.append(offs[-1] + w)
    col = lambda j: w_in[:, offs[j]:offs[j + 1]]
    (c_qn, c_kc, c_vc, c_ks, c_vs, c_kw, c_vw, c_gn, c_qs, c_k_s, c_v_s, c_gm) = map(col, range(12))
    w_rope = jnp.concatenate([c_qn, c_qs, c_ks, c_kw, c_k_s], axis=1).astype(BF16)
    w_plain = jnp.concatenate([c_kc, c_vc, c_vs, c_vw, c_v_s], axis=1).astype(BF16)
    gn_pad = LANES - NSA_HEADS * 3
    w_gate = jnp.concatenate([c_gm, c_gn, jnp.zeros((d, gn_pad), F32)], axis=1).astype(BF16)
    cos_t, sin_t = _rope_tables(s)

    x2 = x.reshape(t, d)
    qn_rot, qs_rot, kk_rot, qn_raw, plain, gates = _project(
        x2, w_rope, w_plain, w_gate, cos_t, sin_t, s, min(256, s))

    nc = (s - CMP_BLOCK) // CMP_STRIDE + 1
    ncp = s // CMP_STRIDE
    half = CMP_STRIDE * HEAD_DIM

    def halves(cols):
        v = cols.reshape(b, s, NSA_KV, HEAD_DIM).transpose(0, 2, 1, 3)
        return v.reshape(b * NSA_KV, ncp, half)

    t2 = jnp.stack([halves(plain[:, 0:nkv]), halves(plain[:, nkv:2 * nkv])])
    t_lo = t2
    t_hi = jnp.concatenate([t2[:, :, 1:], jnp.zeros_like(t2[:, :, :1])], axis=2)
    pe2 = jnp.stack([k_pe.reshape(2, half), v_pe.reshape(2, half)])
    w1 = jnp.stack([k_w1, v_w1]).astype(BF16)
    w2 = jnp.stack([k_w2, v_w2]).astype(BF16)
    kvc = _compress(t_lo, t_hi, pe2, w1, w2).reshape(2, b, NSA_KV, ncp, HEAD_DIM)

    nsel = s // SEL_BLOCK
    nselp = -(-nsel // LANES) * LANES
    cstart = jnp.arange(ncp) * CMP_STRIDE
    sstart = jnp.arange(nselp) * SEL_BLOCK
    overlap = ((cstart[:, None] < sstart[None, :] + SEL_BLOCK)
               & (cstart[:, None] + CMP_BLOCK > sstart[None, :])
               & (jnp.arange(ncp)[:, None] < nc) & (jnp.arange(nselp)[None, :] < nsel))
    o_cmp, notsel = _cmp_attention(qn_raw.reshape(b, s, nq_n), kvc[0], kvc[1],
                                   overlap.T.astype(BF16), min(CMP_STEP_QUERIES, s))

    def group_major(cols):
        return cols.reshape(b, s, NSA_KV, HEAD_DIM).transpose(0, 2, 1, 3)

    k_sel = group_major(kk_rot[:, 0:nkv])
    v_sel = group_major(plain[:, 2 * nkv:3 * nkv])
    onehot = (jnp.arange(s)[:, None] // SEL_BLOCK == jnp.arange(nselp)[None, :])
    k_tail = jnp.concatenate([jnp.zeros((s, LANES - HEAD_DIM), F32),
                              jnp.where(onehot, SEL_PENALTY, 0.0)], axis=1).astype(BF16)
    k_aug = jnp.concatenate(
        [k_sel, jnp.broadcast_to(k_tail, (b, NSA_KV) + k_tail.shape)], axis=3)
    v_tail = jnp.zeros((LANES - HEAD_DIM, 1), BF16).at[0].set(1.0)
    vt_aug = jnp.concatenate(
        [v_sel.transpose(0, 1, 3, 2),
         jnp.broadcast_to(v_tail, (b, NSA_KV, LANES - HEAD_DIM, s))], axis=2)
    o_sel = _sel_attention(qn_rot.reshape(b, s, nq_n), notsel, k_aug, vt_aug, qb, min(512, s))

    o_win = _band_attention(qn_rot.reshape(b, s, nq_n), kk_rot[:, nkv:2 * nkv].reshape(b, s, nkv),
                            plain[:, 3 * nkv:4 * nkv].reshape(b, s, nkv), NSA_WINDOW, None, qb)
    o_swa = _band_attention(qs_rot.reshape(b, s, nq_s),
                            kk_rot[:, 2 * nkv:2 * nkv + nkv_s].reshape(b, s, nkv_s),
                            plain[:, 4 * nkv:4 * nkv + nkv_s].reshape(b, s, nkv_s),
                            SWA_WINDOW, sinks, qb)

    gi = jnp.arange(LANES)
    ci = jnp.arange(3 * nq_n)
    expand = ((gi[:, None] // 3 == (ci[None, :] % nq_n) // HEAD_DIM)
              & (gi[:, None] % 3 == ci[None, :] // nq_n)
              & (gi[:, None] < NSA_HEADS * 3)).astype(BF16)
    wr_pad = jnp.pad(w_router, ((0, 0), (0, LANES - N_EXPERTS)))
    wr_hi = wr_pad.astype(BF16)
    wr_lo = (wr_pad - wr_hi.astype(F32)).astype(BF16)
    b_r = jnp.concatenate([b_router, jnp.full((LANES - N_EXPERTS,), -jnp.inf, F32)]).reshape(1, LANES)
    tm = min(256, t)
    h, top_e, top_g, rank, counts = _merge(
        o_cmp.reshape(t, nq_n), o_sel.reshape(t, nq_n), o_win.reshape(t, nq_n),
        o_swa.reshape(t, nq_s), gates, x2, expand, w_br_nsa.astype(BF16), w_br_swa.astype(BF16),
        w_out.astype(BF16), ln1_g.reshape(1, d), ln1_b.reshape(1, d), wr_hi, wr_lo, b_r,
        alpha, tm)

    dest, blk_e, n_used, n_rows, pend, padded = _moe_plan(
        top_e[:, :TOP_K], rank[:, :TOP_K], counts[0, :N_EXPERTS].astype(I32), t)
    xs = _dispatch(pend, padded, dest, h, n_rows, tm)
    ys = _moe_blocks(blk_e, n_used, xs, w_e_in, b_e_in.reshape(N_EXPERTS, 1, -1),
                     w_e_out, b_e_out.reshape(N_EXPERTS, 1, -1))
    out = _final(dest, h, top_g, ys, ln2_g.reshape(1, d), ln2_b.reshape(1, d), alpha, tm)
    return out.reshape(b, s, d)


def kernel(x, w_in, nsa_k_pe, nsa_k_w1, nsa_k_w2, nsa_v_pe, nsa_v_w1, nsa_v_w2, swa_sinks, w_br_nsa, w_br_swa, w_out, ln1_g, ln1_b, w_router, b_router, w_expert_in, b_expert_in, w_expert_out, b_expert_out, ln2_g, ln2_b):
    depth = w_in.shape[0]
    alpha = (2.0 * depth) ** 0.25
    for l in range(depth):
        x = _layer(x, w_in[l], nsa_k_pe[l], nsa_k_w1[l], nsa_k_w2[l], nsa_v_pe[l], nsa_v_w1[l],
                   nsa_v_w2[l], swa_sinks[l], w_br_nsa[l], w_br_swa[l], w_out[l], ln1_g[l],
                   ln1_b[l], w_router[l], b_router[l], w_expert_in[l], b_expert_in[l],
                   w_expert_out[l], b_expert_out[l], ln2_g[l], ln2_b[l], alpha)
    return x
```

```python
import functools

import jax
import jax.numpy as jnp
from jax import lax
from jax.experimental import pallas as pl
from jax.experimental.pallas import tpu as pltpu

BF16 = jnp.bfloat16
F32 = jnp.float32
I32 = jnp.int32

HEAD_DIM = 64
NSA_HEADS = 8
NSA_KV = 2
CMP_BLOCK = 32
CMP_STRIDE = 16
SEL_BLOCK = 64
SEL_TOPN = 16
NSA_WINDOW = 512
SWA_HEADS = 8
SWA_KV = 2
SWA_WINDOW = 128
Q_BLOCK = 128
ROPE_THETA = 10000.0
N_EXPERTS = 32
TOP_K = 4
SWIGLU_LIMIT = 7.0
SWIGLU_ALPHA = 1.702
LN_EPS = 1e-5

LANES = 128
BF16_SUBLANES = 16
F32_SUBLANES = 8
MXU_DEPTH = 256
MASKED = -1e30
M_INIT = -1e29
SEL_PENALTY = -(2.0 ** 100)
VMEM_LIMIT = 52 * 1024 * 1024
MOE_ROWS = 256
BAND_STEP_QUERIES = 512
CMP_STEP_QUERIES = 512
SEL_STEP_QUERIES = 128
SEL_CHUNK = 512
SEL_KEY_TILE = 512

R_NSA = NSA_HEADS // NSA_KV
R_SWA = SWA_HEADS // SWA_KV
NT_DIMS = (((1,), (1,)), ((), ()))


def _params(*sem):
    return pltpu.CompilerParams(dimension_semantics=sem, vmem_limit_bytes=VMEM_LIMIT)


def _full(shape):
    n = len(shape)
    return pl.BlockSpec(shape, lambda *_: (0,) * n)


def _proj_kernel(x_ref, wr_ref, wp_ref, wg_ref, cos_ref, sin_ref,
                 qn_rot_ref, qs_rot_ref, kk_rot_ref, qn_raw_ref, plain_ref, gates_ref):
    xb = x_ref[...].astype(BF16)
    acc = jnp.dot(xb, wr_ref[...], preferred_element_type=F32)
    cos = cos_ref[...]
    sin = sin_ref[...]
    lane = lax.broadcasted_iota(I32, cos.shape, 1)
    first_half = (lane & (HEAD_DIM - 1)) < HEAD_DIM // 2

    def rope(t):
        partner = jnp.where(first_half, pltpu.roll(t, LANES - HEAD_DIM // 2, 1),
                            pltpu.roll(t, HEAD_DIM // 2, 1))
        return (t * cos + partner * sin).astype(BF16)

    nq = qn_rot_ref.shape[1] // LANES
    ns = qs_rot_ref.shape[1] // LANES
    nk = kk_rot_ref.shape[1] // LANES
    for c in range(nq):
        qn_rot_ref[:, c * LANES:(c + 1) * LANES] = rope(acc[:, c * LANES:(c + 1) * LANES])
    for c in range(ns):
        o = (nq + c) * LANES
        qs_rot_ref[:, c * LANES:(c + 1) * LANES] = rope(acc[:, o:o + LANES])
    for c in range(nk):
        o = (nq + ns + c) * LANES
        kk_rot_ref[:, c * LANES:(c + 1) * LANES] = rope(acc[:, o:o + LANES])
    qn_raw_ref[...] = acc[:, :nq * LANES].astype(BF16)
    plain_ref[...] = jnp.dot(xb, wp_ref[...], preferred_element_type=F32).astype(BF16)
    gates_ref[...] = jnp.dot(xb, wg_ref[...], preferred_element_type=F32)


def _project(x2, w_rope, w_plain, w_gate, cos_t, sin_t, seq, tm):
    t, d = x2.shape
    nr, npl, ng = w_rope.shape[1], w_plain.shape[1], w_gate.shape[1]
    nqn, nqs = NSA_HEADS * HEAD_DIM, SWA_HEADS * HEAD_DIM
    nkk = nr - nqn - nqs
    spb = seq // tm
    row = lambda i: (i, 0)
    return pl.pallas_call(
        _proj_kernel,
        grid=(t // tm,),
        in_specs=[pl.BlockSpec((tm, d), row), _full(w_rope.shape), _full(w_plain.shape),
                  _full(w_gate.shape),
                  pl.BlockSpec((tm, LANES), lambda i: (i % spb, 0)),
                  pl.BlockSpec((tm, LANES), lambda i: (i % spb, 0))],
        out_specs=[pl.BlockSpec((tm, nqn), row), pl.BlockSpec((tm, nqs), row),
                   pl.BlockSpec((tm, nkk), row), pl.BlockSpec((tm, nqn), row),
                   pl.BlockSpec((tm, npl), row), pl.BlockSpec((tm, ng), row)],
        out_shape=[jax.ShapeDtypeStruct((t, nqn), BF16), jax.ShapeDtypeStruct((t, nqs), BF16),
                   jax.ShapeDtypeStruct((t, nkk), BF16), jax.ShapeDtypeStruct((t, nqn), BF16),
                   jax.ShapeDtypeStruct((t, npl), BF16), jax.ShapeDtypeStruct((t, ng), F32)],
        compiler_params=_params("parallel"),
        name="proj",
    )(x2, w_rope, w_plain, w_gate, cos_t, sin_t)


def _compress_kernel(a_ref, b_ref, pe_ref, w1_ref, w2_ref, out_ref):
    half = a_ref.shape[1]
    a = (a_ref[...].astype(F32) + pe_ref[0:1, :]).astype(BF16)
    b = (b_ref[...].astype(F32) + pe_ref[1:2, :]).astype(BF16)
    hid = jnp.dot(a, w1_ref[0:half, :], preferred_element_type=F32)
    hid = hid + jnp.dot(b, w1_ref[half:2 * half, :], preferred_element_type=F32)
    act = jax.nn.gelu(hid).astype(BF16)
    out_ref[...] = jnp.dot(act, w2_ref[...], preferred_element_type=F32).astype(BF16)


def _compress(t_lo, t_hi, pe2, w1, w2):
    two, bg, ncp, half = t_lo.shape
    hid = w1.shape[2]
    blk = lambda shape: pl.BlockSpec((None, None) + shape, lambda j, i: (j, i, 0, 0))
    wsp = lambda shape: pl.BlockSpec((None,) + shape, lambda j, i: (j, 0, 0))
    return pl.pallas_call(
        _compress_kernel,
        grid=(two, bg),
        in_specs=[blk((ncp, half)), blk((ncp, half)), wsp((2, half)), wsp((2 * half, hid)),
                  wsp((hid, HEAD_DIM))],
        out_specs=blk((ncp, HEAD_DIM)),
        out_shape=jax.ShapeDtypeStruct((two, bg, ncp, HEAD_DIM), BF16),
        compiler_params=_params("parallel", "parallel"),
        name="compress",
    )(t_lo, t_hi, pe2, w1, w2)


def _stack_heads(q, g, r):
    return jnp.concatenate(
        [q[:, (g * r + j) * HEAD_DIM:(g * r + j + 1) * HEAD_DIM] for j in range(r)], axis=0)


def _unstack_heads(parts, r, qb):
    return jnp.concatenate([o[j * qb:(j + 1) * qb] for o in parts for j in range(r)], axis=1)


def _unstack_heads_t(parts, r, qb):
    blocks = []
    for o in parts:
        for j in range(0, r, 2):
            pair = jnp.concatenate([o[:, j * qb:(j + 1) * qb], o[:, (j + 1) * qb:(j + 2) * qb]],
                                   axis=0)
            blocks.append(pair.T)
    return jnp.concatenate(blocks, axis=1)


def _topk_mask_cols(vals, k):
    n = vals.shape[0]
    row = lax.broadcasted_iota(I32, vals.shape, 0).astype(F32)
    taken = jnp.zeros(vals.shape, F32)
    work = vals
    for _ in range(k):
        mx = jnp.max(work, axis=0, keepdims=True)
        first = jnp.min(jnp.where(work == mx, row, float(n)), axis=0, keepdims=True)
        pick = row == first
        taken = jnp.where(pick, 1.0, taken)
        work = jnp.where(pick, -jnp.inf, work)
    return taken > 0.5


def _cmp_kernel(q_ref, kc_ref, vct_ref, ovt_ref, o_ref, notsel_ref):
    qb = q_ref.shape[0]
    ncp = kc_ref.shape[1]
    nselp = ovt_ref.shape[0]
    i = pl.program_id(1)
    rows = R_NSA * qb
    pos = i * qb + (lax.broadcasted_iota(I32, (1, rows), 1) & (qb - 1))
    cend = lax.broadcasted_iota(I32, (ncp, 1), 0) * CMP_STRIDE + (CMP_BLOCK - 1)
    bias = jnp.where(cend <= pos, 0.0, MASKED)
    live = (pos >= CMP_BLOCK - 1).astype(F32)
    q = q_ref[...]
    outs, imps = [], []
    for g in range(NSA_KV):
        qg = _stack_heads(q, g, R_NSA) * (HEAD_DIM ** -0.5)
        st = lax.dot_general(kc_ref[g], qg, NT_DIMS, preferred_element_type=F32)
        st = st + bias
        e = jnp.exp(st - jnp.max(st, axis=0, keepdims=True))
        pt = e * (live / jnp.sum(e, axis=0, keepdims=True))
        outs.append(jnp.dot(vct_ref[g], pt.astype(BF16), preferred_element_type=F32))
        psum = pt[:, 0:qb]
        for j in range(1, R_NSA):
            psum = psum + pt[:, j * qb:(j + 1) * qb]
        p_hi = psum.astype(BF16)
        p_lo = (psum - p_hi.astype(F32)).astype(BF16)
        imps.append(jnp.dot(ovt_ref[...], p_hi, preferred_element_type=F32)
                    + jnp.dot(ovt_ref[...], p_lo, preferred_element_type=F32))
    imp = jnp.concatenate(imps, axis=1)
    lane = lax.broadcasted_iota(I32, (1, NSA_KV * qb), 1)
    cur = (i * qb + (lane & (qb - 1))) >> 6
    jb = lax.broadcasted_iota(I32, (nselp, 1), 0)
    forced = (jb == 0) | (jb == cur) | (jb == cur - 1)
    imp = jnp.where(jb > cur, -1.0, jnp.where(forced, 1e6, imp))
    notsel = jnp.where(_topk_mask_cols(imp, SEL_TOPN), 0.0, 1.0)
    for g in range(NSA_KV):
        notsel_ref[g] = notsel[:, g * qb:(g + 1) * qb].T.astype(BF16)
    o_ref[...] = _unstack_heads_t(outs, R_NSA, qb)


def _cmp_attention(q_raw, kc, vc, overlap_t, qb):
    vc = vc.transpose(0, 1, 3, 2)
    b, s, hq = q_raw.shape
    _, g, ncp, dh = kc.shape
    nselp = overlap_t.shape[0]
    return pl.pallas_call(
        _cmp_kernel,
        grid=(b, s // qb),
        in_specs=[pl.BlockSpec((None, qb, hq), lambda bi, i: (bi, i, 0)),
                  pl.BlockSpec((None, g, ncp, dh), lambda bi, i: (bi, 0, 0, 0)),
                  pl.BlockSpec((None, g, dh, ncp), lambda bi, i: (bi, 0, 0, 0)),
                  _full(overlap_t.shape)],
        out_specs=[pl.BlockSpec((None, qb, hq), lambda bi, i: (bi, i, 0)),
                   pl.BlockSpec((None, g, qb, nselp), lambda bi, i: (bi, 0, i, 0))],
        out_shape=[jax.ShapeDtypeStruct((b, s, hq), F32),
                   jax.ShapeDtypeStruct((b, g, s, nselp), BF16)],
        compiler_params=_params("parallel", "parallel"),
        name="cmp_attn",
    )(q_raw, kc, vc, overlap_t)


def _sel_kernel(q_ref, notsel_ref, k_ref, vt_ref, o_ref, *score_bufs, tk):
    s_even, s_odd = score_bufs[:NSA_KV], score_bufs[NSA_KV:]
    qb = q_ref.shape[0]
    i = pl.program_id(1)
    rows = R_NSA * qb
    qpos = i * qb + (lax.broadcasted_iota(I32, (1, rows), 1) & (qb - 1))
    n_clear = (i * qb) // tk
    q = q_ref[...]
    q_augs = []
    for g in range(NSA_KV):
        qg = _stack_heads(q, g, R_NSA) * (HEAD_DIM ** -0.5)
        q_augs.append(jnp.concatenate(
            [qg, jnp.zeros((rows, LANES - HEAD_DIM), BF16),
             jnp.concatenate([notsel_ref[g]] * R_NSA, axis=0)], axis=1))

    def scores(kt, g):
        start = pl.multiple_of(kt * tk, tk)
        return lax.dot_general(k_ref[g, pl.ds(start, tk), :], q_augs[g], NT_DIMS,
                               preferred_element_type=F32)

    def consume(kt, g, s_ref, m, acc, causal):
        start = pl.multiple_of(kt * tk, tk)
        vt_t = vt_ref[g, :, pl.ds(start, tk)]
        ms, accs = [], []
        for c in range(rows // SEL_CHUNK):
            cols = slice(c * SEL_CHUNK, (c + 1) * SEL_CHUNK)
            st = s_ref[:, cols]
            if causal:
                kpos = start + lax.broadcasted_iota(I32, (tk, 1), 0)
                st = jnp.where(kpos <= qpos[:, cols], st, MASKED)
            m_new = jnp.maximum(m[:, cols], jnp.max(st, axis=0, keepdims=True))
            pt = jnp.exp(st - m_new).astype(BF16)
            accs.append(jnp.exp(m[:, cols] - m_new) * acc[:, cols]
                        + jnp.dot(vt_t, pt, preferred_element_type=F32))
            ms.append(m_new)
        return jnp.concatenate(ms, axis=1), jnp.concatenate(accs, axis=1)

    def advance(kt, carry, cur, nxt):
        new = []
        for g in range(NSA_KV):
            nxt[g][...] = scores(kt + 1, g)
            new.append(consume(kt, g, cur[g], *carry[g], False))
        return tuple(new)

    def pair(j, carry):
        carry = advance(2 * j, carry, s_even, s_odd)
        return advance(2 * j + 1, carry, s_odd, s_even)

    def finish(carry, cur):
        outs = []
        for g in range(NSA_KV):
            _, acc = consume(n_clear, g, cur[g], *carry[g], True)
            outs.append(acc[:HEAD_DIM] / acc[HEAD_DIM:HEAD_DIM + 1])
        o_ref[...] = _unstack_heads_t(outs, R_NSA, qb)

    for g in range(NSA_KV):
        s_even[g][...] = scores(0, g)
    init = tuple((jnp.full((1, rows), M_INIT, F32), jnp.zeros((LANES, rows), F32))
                 for _ in range(NSA_KV))
    carry = lax.fori_loop(0, n_clear // 2, pair, init)

    @pl.when((n_clear & 1) == 0)
    def _():
        finish(carry, s_even)

    @pl.when((n_clear & 1) == 1)
    def _():
        finish(advance(n_clear - 1, carry, s_even, s_odd), s_odd)


def _sel_attention(q_rot, notsel, k_aug, vt_aug, qb, tk):
    b, s, hq = q_rot.shape
    _, g, _, kw = k_aug.shape
    nselp = notsel.shape[3]
    return pl.pallas_call(
        functools.partial(_sel_kernel, tk=tk),
        grid=(b, s // qb),
        in_specs=[pl.BlockSpec((None, qb, hq), lambda bi, i: (bi, i, 0)),
                  pl.BlockSpec((None, g, qb, nselp), lambda bi, i: (bi, 0, i, 0)),
                  pl.BlockSpec((None, g, s, kw), lambda bi, i: (bi, 0, 0, 0)),
                  pl.BlockSpec((None, g, LANES, s), lambda bi, i: (bi, 0, 0, 0))],
        out_specs=pl.BlockSpec((None, qb, hq), lambda bi, i: (bi, i, 0)),
        out_shape=jax.ShapeDtypeStruct((b, s, hq), F32),
        scratch_shapes=[pltpu.VMEM((tk, R_NSA * qb), F32)] * (2 * g),
        compiler_params=_params("parallel", "parallel"),
        name="sel_attn",
    )(q_rot, notsel, k_aug, vt_aug)


def _band_kernel(*refs, window, wlen, r, kv, qb, has_sinks):
    if has_sinks:
        sink_ref, q_ref, k_ref, vt_ref, o_ref = refs
    else:
        q_ref, k_ref, vt_ref, o_ref = refs
    nsub = q_ref.shape[0] // qb
    rows = r * qb
    nv = kv * HEAD_DIM
    lane = lax.broadcasted_iota(I32, (1, rows), 1)

    def mask_bias(i):
        start = jnp.maximum((i + 1) * qb - wlen, 0)
        rel = i * qb + (lane & (qb - 1)) - start - lax.broadcasted_iota(I32, (wlen, 1), 0)
        return jnp.where((rel >= 0) & (rel < window), 0.0, MASKED)

    def body(shared_bias):
        first = pl.program_id(1) * nsub
        if shared_bias:
            bias = mask_bias(first)
        for sb in range(nsub):
            i = first + sb
            if not shared_bias:
                bias = mask_bias(i)
            start = pl.multiple_of(jnp.maximum((i + 1) * qb - wlen, 0), qb)
            q = q_ref[sb * qb:(sb + 1) * qb, :]
            kw = k_ref[pl.ds(start, wlen), :]
            vtw = vt_ref[:, pl.ds(start, wlen)]
            outs = []
            for g in range(kv):
                qg = _stack_heads(q, g, r) * (HEAD_DIM ** -0.5)
                st = lax.dot_general(kw[:, g * HEAD_DIM:(g + 1) * HEAD_DIM], qg, NT_DIMS,
                                     preferred_element_type=F32) + bias
                m = jnp.max(st, axis=0, keepdims=True)
                if has_sinks:
                    sk = jnp.full((1, rows), sink_ref[g * r], F32)
                    for j in range(1, r):
                        sk = jnp.where(lane >= j * qb, sink_ref[g * r + j], sk)
                    m = jnp.maximum(m, sk)
                e = jnp.exp(st - m).astype(BF16)
                v_ones = jnp.concatenate([vtw[g * HEAD_DIM:(g + 1) * HEAD_DIM, :], vtw[nv:, :]],
                                         axis=0)
                ot = jnp.dot(v_ones, e, preferred_element_type=F32)
                den = ot[HEAD_DIM:HEAD_DIM + 1]
                if has_sinks:
                    den = den + jnp.exp(sk - m)
                outs.append(ot[:HEAD_DIM] / den)
            o_ref[sb * qb:(sb + 1) * qb, :] = _unstack_heads_t(outs, r, qb)

    if nsub * qb >= wlen - qb:
        @pl.when(pl.program_id(1) == 0)
        def _():
            body(False)

        @pl.when(pl.program_id(1) > 0)
        def _():
            body(True)
    else:
        body(False)


def _band_attention(q_rot, k, v, window, sinks, qb):
    b, s, hq = q_rot.shape
    gk = k.shape[2]
    v = jnp.concatenate([v.transpose(0, 2, 1), jnp.ones((b, BF16_SUBLANES, s), v.dtype)], axis=1)
    kv = gk // HEAD_DIM
    r = hq // gk
    back = -(-window // qb)
    wlen = (back + 1) * qb
    assert wlen <= s
    has_sinks = sinks is not None
    tq = min(BAND_STEP_QUERIES, s)
    in_specs = [pl.BlockSpec((None, tq, hq), lambda bi, i: (bi, i, 0)),
                pl.BlockSpec((None, s, gk), lambda bi, i: (bi, 0, 0)),
                pl.BlockSpec((None, gk + BF16_SUBLANES, s), lambda bi, i: (bi, 0, 0))]
    args = [q_rot, k, v]
    if has_sinks:
        in_specs = [pl.BlockSpec(memory_space=pltpu.SMEM)] + in_specs
        args = [sinks.astype(F32)] + args
    return pl.pallas_call(
        functools.partial(_band_kernel, window=window, wlen=wlen, r=r, kv=kv, qb=qb,
                          has_sinks=has_sinks),
        grid=(b, s // tq),
        in_specs=in_specs,
        out_specs=pl.BlockSpec((None, tq, hq), lambda bi, i: (bi, i, 0)),
        out_shape=jax.ShapeDtypeStruct((b, s, hq), F32),
        compiler_params=_params("parallel", "parallel"),
        name="band_attn_sink" if has_sinks else "band_attn",
    )(*args)


def _layer_norm(v, g, b):
    mu = jnp.mean(v, axis=1, keepdims=True)
    c = v - mu
    var = jnp.mean(c * c, axis=1, keepdims=True)
    return c * lax.rsqrt(var + LN_EPS) * g + b


def _split_dot(a, w):
    hi = a.astype(BF16)
    lo = (a - hi.astype(F32)).astype(BF16)
    return (jnp.dot(hi, w, preferred_element_type=F32)
            + jnp.dot(lo, w, preferred_element_type=F32))


def _merge_kernel(ocmp_ref, osel_ref, owin_ref, oswa_ref, gates_ref, x_ref, exp_ref,
                  wbn_ref, wbs_ref, wo_ref, lng_ref, lnb_ref, wrh_ref, wrl_ref, br_ref,
                  h_ref, te_ref, tg_ref, tr_ref, counts_ref, base_ref, cnt_ref, *, alpha):
    d = x_ref.shape[1]
    hq = ocmp_ref.shape[1]
    gates = gates_ref[...]
    gn = jax.nn.sigmoid(gates[:, 2 * d:])
    gexp = _split_dot(gn, exp_ref[...])
    o_nsa = (gexp[:, 0:hq] * ocmp_ref[...] + gexp[:, hq:2 * hq] * osel_ref[...]
             + gexp[:, 2 * hq:3 * hq] * owin_ref[...])
    y_nsa = jnp.dot(o_nsa.astype(BF16), wbn_ref[...], preferred_element_type=F32)
    y_swa = jnp.dot(oswa_ref[...].astype(BF16), wbs_ref[...], preferred_element_type=F32)
    gm = jax.nn.sigmoid(gates[:, :2 * d])
    mixed = gm[:, :d] * y_nsa + gm[:, d:] * y_swa
    z = jnp.dot(mixed.astype(BF16), wo_ref[...], preferred_element_type=F32)
    h = _layer_norm(alpha * x_ref[...] + z, lng_ref[...], lnb_ref[...])
    h_ref[...] = h
    h_hi = h.astype(BF16)
    h_lo = (h - h_hi.astype(F32)).astype(BF16)
    logits = (jnp.dot(h_hi, wrh_ref[...], preferred_element_type=F32)
              + jnp.dot(h_lo, wrh_ref[...], preferred_element_type=F32)
              + jnp.dot(h_hi, wrl_ref[...], preferred_element_type=F32)) + br_ref[...]
    col = lax.broadcasted_iota(I32, logits.shape, 1).astype(F32)
    work = logits
    vals, ids = [], []
    for _ in range(TOP_K):
        mx = jnp.max(work, axis=1, keepdims=True)
        first = jnp.min(jnp.where(work == mx, col, float(LANES)), axis=1, keepdims=True)
        vals.append(mx)
        ids.append(first)
        work = jnp.where(col == first, -jnp.inf, work)
    es = [jnp.exp(v - vals[0]) for v in vals]
    den = es[0]
    for e in es[1:]:
        den = den + e
    @pl.when(pl.program_id(0) == 0)
    def _():
        cnt_ref[...] = jnp.zeros(cnt_ref.shape, F32)

    tm = logits.shape[0]
    hits = jnp.zeros(logits.shape, F32)
    for k in range(TOP_K):
        hits = jnp.where(col == ids[k], 1.0, hits)
    earlier = (lax.broadcasted_iota(I32, (tm, tm), 1)
               < lax.broadcasted_iota(I32, (tm, tm), 0))
    before = jnp.dot(jnp.where(earlier, 1.0, 0.0).astype(BF16), hits.astype(BF16),
                     preferred_element_type=F32) + cnt_ref[...]
    base_ref[...] = cnt_ref[...]
    cnt_ref[...] = cnt_ref[...] + jnp.sum(hits, axis=0, keepdims=True)
    counts_ref[...] = cnt_ref[...]
    te = jnp.zeros(logits.shape, F32)
    tg = jnp.zeros(logits.shape, F32)
    tr = jnp.zeros(logits.shape, F32)
    for k in range(TOP_K):
        rank = jnp.sum(jnp.where(col == ids[k], before, 0.0), axis=1, keepdims=True)
        te = jnp.where(col == float(k), ids[k], te)
        tg = jnp.where(col == float(k), es[k] / den, tg)
        tr = jnp.where(col == float(k), rank, tr)
    te_ref[...] = te.astype(I32)
    tg_ref[...] = tg
    tr_ref[...] = tr.astype(I32)


def _merge(o_cmp, o_sel, o_win, o_swa, gates, x2, expand, w_bn, w_bs, w_o, ln_g, ln_b,
           wr_hi, wr_lo, b_r, alpha, tm):
    t, d = x2.shape
    hq = o_cmp.shape[1]
    row = lambda i: (i, 0)
    tok = lambda n: pl.BlockSpec((tm, n), row)
    return pl.pallas_call(
        functools.partial(_merge_kernel, alpha=alpha),
        grid=(t // tm,),
        in_specs=[tok(hq), tok(hq), tok(hq), tok(hq), tok(gates.shape[1]), tok(d),
                  _full(expand.shape), _full(w_bn.shape), _full(w_bs.shape), _full(w_o.shape),
                  _full(ln_g.shape), _full(ln_b.shape), _full(wr_hi.shape), _full(wr_lo.shape),
                  _full(b_r.shape)],
        out_specs=[tok(d), tok(LANES), tok(LANES), tok(LANES), _full((1, LANES)),
                   pl.BlockSpec((None, 1, LANES), lambda i: (i, 0, 0))],
        out_shape=[jax.ShapeDtypeStruct((t, d), F32), jax.ShapeDtypeStruct((t, LANES), I32),
                   jax.ShapeDtypeStruct((t, LANES), F32), jax.ShapeDtypeStruct((t, LANES), I32),
                   jax.ShapeDtypeStruct((1, LANES), F32),
                   jax.ShapeDtypeStruct((t // tm, 1, LANES), F32)],
        scratch_shapes=[pltpu.VMEM((1, LANES), F32)],
        compiler_params=_params("arbitrary"),
        name="merge_ln_router",
    )(o_cmp, o_sel, o_win, o_swa, gates, x2, expand, w_bn, w_bs, w_o, ln_g, ln_b,
      wr_hi, wr_lo, b_r)


def _dispatch_kernel(pend_ref, padded_ref, dest_ref, h_ref, xs_ref, zbuf, sem, zsem):
    tm = h_ref.shape[0]
    n_rows = xs_ref.shape[0]

    @pl.when(pl.program_id(0) == 0)
    def _():
        zbuf[...] = jnp.zeros(zbuf.shape, F32)
        used_rows = pend_ref[N_EXPERTS - 1]

        def zero_copy(start):
            start = pl.multiple_of(start, MOE_ROWS)
            return pltpu.make_async_copy(zbuf, xs_ref.at[pl.ds(start, MOE_ROWS), :], zsem)

        blocks = [(padded_ref[e] > 0, pend_ref[e] - MOE_ROWS) for e in range(N_EXPERTS)]
        blocks += [(used_rows + j * MOE_ROWS < n_rows, used_rows + j * MOE_ROWS)
                   for j in range(N_EXPERTS)]
        for cond, start in blocks:
            @pl.when(cond)
            def _(start=start):
                zero_copy(start).start()
        for cond, start in blocks:
            @pl.when(cond)
            def _(start=start):
                zero_copy(start).wait()

    def issue(r, carry):
        for k in range(TOP_K):
            dst = dest_ref[r * TOP_K + k]
            pltpu.make_async_copy(h_ref.at[pl.ds(r, 1), :], xs_ref.at[pl.ds(dst, 1), :],
                                  sem).start(priority=k % 2)
        return carry

    lax.fori_loop(0, tm, issue, 0, unroll=8)
    for _ in range(TOP_K):
        pltpu.make_async_copy(h_ref, xs_ref.at[pl.ds(0, tm), :], sem).wait()


def _dispatch(pend, padded, dest, h, n_rows, tm):
    t, d = h.shape
    return pl.pallas_call(
        _dispatch_kernel,
        grid=(t // tm,),
        in_specs=[pl.BlockSpec(memory_space=pltpu.SMEM), pl.BlockSpec(memory_space=pltpu.SMEM),
                  pl.BlockSpec((tm * TOP_K,), lambda i: (i,), memory_space=pltpu.SMEM),
                  pl.BlockSpec((tm, d), lambda i: (i, 0))],
        out_specs=pl.BlockSpec(memory_space=pl.ANY),
        out_shape=jax.ShapeDtypeStruct((n_rows, d), F32),
        scratch_shapes=[pltpu.VMEM((MOE_ROWS, d), F32), pltpu.SemaphoreType.DMA(()),
                        pltpu.SemaphoreType.DMA(())],
        compiler_params=_params("arbitrary"),
        name="moe_dispatch",
    )(pend, padded, dest, h)


def _moe_kernel(blk_e_ref, n_used_ref, x_ref, wi_ref, bi_ref, wo_ref, bo_ref, y_ref,
                wi_bf, wo_bf):
    f = wo_ref.shape[0]
    b = pl.program_id(0)

    @pl.when((b == 0) | (blk_e_ref[b] != blk_e_ref[jnp.maximum(b - 1, 0)]))
    def _():
        wi_bf[...] = wi_ref[...].astype(BF16)
        wo_bf[...] = wo_ref[...].astype(BF16)

    @pl.when(b < n_used_ref[0])
    def _():
        hdn = jnp.dot(x_ref[...].astype(BF16), wi_bf[...],
                      preferred_element_type=F32) + bi_ref[...]
        hg = jnp.minimum(hdn[:, :f], SWIGLU_LIMIT)
        hu = jnp.clip(hdn[:, f:], -SWIGLU_LIMIT, SWIGLU_LIMIT)
        act = hg * jax.nn.sigmoid(SWIGLU_ALPHA * hg) * (hu + 1.0)
        y_ref[...] = jnp.dot(act.astype(BF16), wo_bf[...],
                             preferred_element_type=F32) + bo_ref[...]

    @pl.when(b >= n_used_ref[0])
    def _():
        y_ref[...] = jnp.zeros(y_ref.shape, F32)


def _moe_blocks(blk_e, n_used, xs, w_in, b_in, w_out, b_out):
    n_rows, d = xs.shape
    e, _, f2 = w_in.shape
    f = w_out.shape[1]
    n_blk = n_rows // MOE_ROWS
    grid_spec = pltpu.PrefetchScalarGridSpec(
        num_scalar_prefetch=2,
        grid=(n_blk,),
        in_specs=[pl.BlockSpec((MOE_ROWS, d), lambda b, be, nu: (jnp.minimum(b, nu[0] - 1), 0)),
                  pl.BlockSpec((None, d, f2), lambda b, be, nu: (be[b], 0, 0)),
                  pl.BlockSpec((None, 1, f2), lambda b, be, nu: (be[b], 0, 0)),
                  pl.BlockSpec((None, f, d), lambda b, be, nu: (be[b], 0, 0)),
                  pl.BlockSpec((None, 1, d), lambda b, be, nu: (be[b], 0, 0))],
        out_specs=pl.BlockSpec((MOE_ROWS, d), lambda b, be, nu: (b, 0)),
        scratch_shapes=[pltpu.VMEM((d, f2), BF16), pltpu.VMEM((f, d), BF16)],
    )
    return pl.pallas_call(
        _moe_kernel,
        grid_spec=grid_spec,
        out_shape=jax.ShapeDtypeStruct((n_rows, d), F32),
        compiler_params=_params("arbitrary"),
        name="moe_experts",
    )(blk_e, n_used, xs, w_in, b_in, w_out, b_out)


def _run_sizes(tm):
    sizes, p = [], F32_SUBLANES
    while p <= tm:
        sizes.append(p)
        p *= 2
    return tuple(reversed(sizes))


def _stage_rows(tm):
    rows = tm * TOP_K + N_EXPERTS * 2 * (F32_SUBLANES - 1)
    return -(-rows // MXU_DEPTH) * MXU_DEPTH


def _final_kernel(start_ref, len_ref, seg_ref, h_ref, tg_ref, sp_ref, ys_ref, g_ref, b_ref,
                  o_ref, stage, sem, *, alpha):
    i = pl.program_id(0)
    tm = h_ref.shape[0]
    n_stage = stage.shape[0]

    @pl.when(i == 0)
    def _():
        stage[...] = jnp.zeros(stage.shape, F32)

    copies = []
    for e in range(N_EXPERTS):
        ln = len_ref[i * N_EXPERTS + e]
        first = start_ref[i * N_EXPERTS + e]
        seg = seg_ref[i * N_EXPERTS + e]
        for p in _run_sizes(tm):
            done = ln & (-2 * p)
            src = ys_ref.at[pl.ds(pl.multiple_of(first + done, F32_SUBLANES), p), :]
            dst = stage.at[pl.ds(pl.multiple_of(seg + done, F32_SUBLANES), p), :]
            copies.append(((ln & p) != 0, pltpu.make_async_copy(src, dst, sem)))
    for cond, cp in copies:
        pl.when(cond)(cp.start)
    for cond, cp in copies:
        pl.when(cond)(cp.wait)

    col = lax.broadcasted_iota(I32, (1, n_stage), 1)
    tg = tg_ref[...]
    sp = sp_ref[...]
    q = jnp.zeros((tm, n_stage), F32)
    for k in range(TOP_K):
        q = jnp.where(col == sp[:, k:k + 1], tg[:, k:k + 1], q)
    f = _split_dot(q, stage[...].astype(BF16))
    o_ref[...] = _layer_norm(alpha * h_ref[...] + f, g_ref[...], b_ref[...])


def _final(run_start, run_len, run_seg, h, top_g, spos, ys, ln_g, ln_b, alpha, tm):
    t, d = h.shape
    grid_spec = pltpu.PrefetchScalarGridSpec(
        num_scalar_prefetch=3,
        grid=(t // tm,),
        in_specs=[pl.BlockSpec((tm, d), lambda i, *_: (i, 0)),
                  pl.BlockSpec((tm, LANES), lambda i, *_: (i, 0)),
                  pl.BlockSpec((tm, TOP_K), lambda i, *_: (i, 0)),
                  pl.BlockSpec(memory_space=pl.ANY),
                  pl.BlockSpec(ln_g.shape, lambda i, *_: (0, 0)),
                  pl.BlockSpec(ln_b.shape, lambda i, *_: (0, 0))],
        out_specs=pl.BlockSpec((tm, d), lambda i, *_: (i, 0)),
        scratch_shapes=[pltpu.VMEM((_stage_rows(tm), d), F32), pltpu.SemaphoreType.DMA(())],
    )
    return pl.pallas_call(
        functools.partial(_final_kernel, alpha=alpha),
        grid_spec=grid_spec,
        out_shape=jax.ShapeDtypeStruct((t, d), F32),
        compiler_params=_params("arbitrary"),
        name="combine_ln",
    )(run_start, run_len, run_seg, h, top_g, spos, ys, ln_g, ln_b)


def _rope_tables(s):
    half = HEAD_DIM // 2
    inv = ROPE_THETA ** (-jnp.arange(half, dtype=F32) / half)
    ang = jnp.arange(s, dtype=F32)[:, None] * inv[None, :]
    cos, sin = jnp.cos(ang), jnp.sin(ang)
    reps = LANES // HEAD_DIM
    cos_t = jnp.tile(jnp.concatenate([cos, cos], axis=1), (1, reps))
    sin_t = jnp.tile(jnp.concatenate([-sin, sin], axis=1), (1, reps))
    return cos_t, sin_t


def _moe_plan(top_e, rank, counts, t):
    a = t * TOP_K
    padded = (counts + MOE_ROWS - 1) // MOE_ROWS * MOE_ROWS
    pend = jnp.cumsum(padded)
    pstart = pend - padded
    experts = jnp.arange(N_EXPERTS, dtype=I32)
    base = jnp.sum(jnp.where(top_e[:, :, None] == experts, pstart, 0), axis=2)
    dest = (base + rank).astype(I32).reshape(a)
    n_blk = -(-a // MOE_ROWS) + N_EXPERTS
    blk_first = jnp.arange(n_blk, dtype=I32) * MOE_ROWS
    blk_e = jnp.minimum(jnp.sum(pend[None, :] <= blk_first[:, None], axis=1),
                        N_EXPERTS - 1).astype(I32)
    n_used = (pend[-1] // MOE_ROWS).astype(I32).reshape(1)
    return (dest, blk_e, n_used, n_blk * MOE_ROWS, pend.astype(I32), padded.astype(I32),
            pstart.astype(I32))


def _combine_plan(top_e, rank, tile_base, counts, pstart, tm):
    nxt = jnp.concatenate([tile_base[1:], counts[None, :]], axis=0)
    cnt = nxt - tile_base
    first = pstart[None, :] + tile_base
    lead = first & (F32_SUBLANES - 1)
    run_len = jnp.where(cnt > 0, (lead + cnt + F32_SUBLANES - 1) // F32_SUBLANES * F32_SUBLANES,
                        0)
    run_seg = jnp.cumsum(run_len, axis=1) - run_len
    shift = jnp.repeat(run_seg + lead - tile_base, tm, axis=0)
    experts = jnp.arange(N_EXPERTS, dtype=I32)
    spos = jnp.sum(jnp.where(top_e[:, :, None] == experts, shift[:, None, :], 0), axis=2) + rank
    flat = lambda v: v.astype(I32).reshape(-1)
    return flat(first - lead), flat(run_len), flat(run_seg), spos.astype(I32)


def _layer(x, w_in, k_pe, k_w1, k_w2, v_pe, v_w1, v_w2, sinks, w_br_nsa, w_br_swa, w_out,
           ln1_g, ln1_b, w_router, b_router, w_e_in, b_e_in, w_e_out, b_e_out, ln2_g, ln2_b,
           alpha):
    b, s, d = x.shape
    t = b * s
    qb = Q_BLOCK
    nq_n, nkv = NSA_HEADS * HEAD_DIM, NSA_KV * HEAD_DIM
    nq_s, nkv_s = SWA_HEADS * HEAD_DIM, SWA_KV * HEAD_DIM
    widths = (nq_n, nkv, nkv, nkv, nkv, nkv, nkv, NSA_HEADS * 3, nq_s, nkv_s, nkv_s, 2 * d)
    offs = [0]
    for w in widths:
        offs.append(offs[-1] + w)
    col = lambda j: w_in[:, offs[j]:offs[j + 1]]
    (c_qn, c_kc, c_vc, c_ks, c_vs, c_kw, c_vw, c_gn, c_qs, c_k_s, c_v_s, c_gm) = map(col, range(12))
    w_rope = jnp.concatenate([c_qn, c_qs, c_ks, c_kw, c_k_s], axis=1).astype(BF16)
    w_plain = jnp.concatenate([c_kc, c_vc, c_vs, c_vw, c_v_s], axis=1).astype(BF16)
    gn_pad = LANES - NSA_HEADS * 3
    w_gate = jnp.concatenate([c_gm, c_gn, jnp.zeros((d, gn_pad), F32)], axis=1).astype(BF16)
    cos_t, sin_t = _rope_tables(s)

    x2 = x.reshape(t, d)
    qn_rot, qs_rot, kk_rot, qn_raw, plain, gates = _project(
        x2, w_rope, w_plain, w_gate, cos_t, sin_t, s, min(256, s))

    nc = (s - CMP_BLOCK) // CMP_STRIDE + 1
    ncp = s // CMP_STRIDE
    half = CMP_STRIDE * HEAD_DIM

    def halves(cols):
        v = cols.reshape(b, s, NSA_KV, HEAD_DIM).transpose(0, 2, 1, 3)
        return v.reshape(b * NSA_KV, ncp, half)

    t2 = jnp.stack([halves(plain[:, 0:nkv]), halves(plain[:, nkv:2 * nkv])])
    t_lo = t2
    t_hi = jnp.concatenate([t2[:, :, 1:], jnp.zeros_like(t2[:, :, :1])], axis=2)
    pe2 = jnp.stack([k_pe.reshape(2, half), v_pe.reshape(2, half)])
    w1 = jnp.stack([k_w1, v_w1]).astype(BF16)
    w2 = jnp.stack([k_w2, v_w2]).astype(BF16)
    kvc = _compress(t_lo, t_hi, pe2, w1, w2).reshape(2, b, NSA_KV, ncp, HEAD_DIM)

    nsel = s // SEL_BLOCK
    nselp = -(-nsel // LANES) * LANES
    cstart = jnp.arange(ncp) * CMP_STRIDE
    sstart = jnp.arange(nselp) * SEL_BLOCK
    overlap = ((cstart[:, None] < sstart[None, :] + SEL_BLOCK)
               & (cstart[:, None] + CMP_BLOCK > sstart[None, :])
               & (jnp.arange(ncp)[:, None] < nc) & (jnp.arange(nselp)[None, :] < nsel))
    o_cmp, notsel = _cmp_attention(qn_raw.reshape(b, s, nq_n), kvc[0], kvc[1],
                                   overlap.T.astype(BF16), min(CMP_STEP_QUERIES, s))

    def group_major(cols):
        return cols.reshape(b, s, NSA_KV, HEAD_DIM).transpose(0, 2, 1, 3)

    k_sel = group_major(kk_rot[:, 0:nkv])
    v_sel = group_major(plain[:, 2 * nkv:3 * nkv])
    onehot = (jnp.arange(s)[:, None] // SEL_BLOCK == jnp.arange(nselp)[None, :])
    k_tail = jnp.concatenate([jnp.zeros((s, LANES - HEAD_DIM), F32),
                              jnp.where(onehot, SEL_PENALTY, 0.0)], axis=1).astype(BF16)
    k_aug = jnp.concatenate(
        [k_sel, jnp.broadcast_to(k_tail, (b, NSA_KV) + k_tail.shape)], axis=3)
    v_tail = jnp.zeros((LANES - HEAD_DIM, 1), BF16).at[0].set(1.0)
    vt_aug = jnp.concatenate(
        [v_sel.transpose(0, 1, 3, 2),
         jnp.broadcast_to(v_tail, (b, NSA_KV, LANES - HEAD_DIM, s))], axis=2)
    o_sel = _sel_attention(qn_rot.reshape(b, s, nq_n), notsel, k_aug, vt_aug,
                           min(SEL_STEP_QUERIES, s), min(SEL_KEY_TILE, s))

    o_win = _band_attention(qn_rot.reshape(b, s, nq_n), kk_rot[:, nkv:2 * nkv].reshape(b, s, nkv),
                            plain[:, 3 * nkv:4 * nkv].reshape(b, s, nkv), NSA_WINDOW, None, qb)
    o_swa = _band_attention(qs_rot.reshape(b, s, nq_s),
                            kk_rot[:, 2 * nkv:2 * nkv + nkv_s].reshape(b, s, nkv_s),
                            plain[:, 4 * nkv:4 * nkv + nkv_s].reshape(b, s, nkv_s),
                            SWA_WINDOW, sinks, qb)

    gi = jnp.arange(LANES)
    ci = jnp.arange(3 * nq_n)
    expand = ((gi[:, None] // 3 == (ci[None, :] % nq_n) // HEAD_DIM)
              & (gi[:, None] % 3 == ci[None, :] // nq_n)
              & (gi[:, None] < NSA_HEADS * 3)).astype(BF16)
    wr_pad = jnp.pad(w_router, ((0, 0), (0, LANES - N_EXPERTS)))
    wr_hi = wr_pad.astype(BF16)
    wr_lo = (wr_pad - wr_hi.astype(F32)).astype(BF16)
    b_r = jnp.concatenate([b_router, jnp.full((LANES - N_EXPERTS,), -jnp.inf, F32)]).reshape(1, LANES)
    tm = min(256, t)
    h, top_e, top_g, rank, counts, tile_base = _merge(
        o_cmp.reshape(t, nq_n), o_sel.reshape(t, nq_n), o_win.reshape(t, nq_n),
        o_swa.reshape(t, nq_s), gates, x2, expand, w_br_nsa.astype(BF16), w_br_swa.astype(BF16),
        w_out.astype(BF16), ln1_g.reshape(1, d), ln1_b.reshape(1, d), wr_hi, wr_lo, b_r,
        alpha, tm)

    top_e, rank = top_e[:, :TOP_K], rank[:, :TOP_K]
    counts = counts[0, :N_EXPERTS].astype(I32)
    dest, blk_e, n_used, n_rows, pend, padded, pstart = _moe_plan(top_e, rank, counts, t)
    xs = _dispatch(pend, padded, dest, h, n_rows, tm)
    ys = _moe_blocks(blk_e, n_used, xs, w_e_in, b_e_in.reshape(N_EXPERTS, 1, -1),
                     w_e_out, b_e_out.reshape(N_EXPERTS, 1, -1))
    run_start, run_len, run_seg, spos = _combine_plan(
        top_e, rank, tile_base[:, 0, :N_EXPERTS].astype(I32), counts, pstart, tm)
    out = _final(run_start, run_len, run_seg, h, top_g, spos, ys, ln2_g.reshape(1, d),
                 ln2_b.reshape(1, d), alpha, tm)
    return out.reshape(b, s, d)


def kernel(x, w_in, nsa_k_pe, nsa_k_w1, nsa_k_w2, nsa_v_pe, nsa_v_w1, nsa_v_w2, swa_sinks, w_br_nsa, w_br_swa, w_out, ln1_g, ln1_b, w_router, b_router, w_expert_in, b_expert_in, w_expert_out, b_expert_out, ln2_g, ln2_b):
    depth = w_in.shape[0]
    alpha = (2.0 * depth) ** 0.25
    for l in range(depth):
        x = _layer(x, w_in[l], nsa_k_pe[l], nsa_k_w1[l], nsa_k_w2[l], nsa_v_pe[l], nsa_v_w1[l],
                   nsa_v_w2[l], swa_sinks[l], w_br_nsa[l], w_br_swa[l], w_out[l], ln1_g[l],
                   ln1_b[l], w_router[l], b_router[l], w_expert_in[l], b_expert_in[l],
                   w_expert_out[l], b_expert_out[l], ln2_g[l], ln2_b[l], alpha)
    return x
```

```python
import functools

import jax
import jax.numpy as jnp
from jax import lax
from jax.experimental import pallas as pl
from jax.experimental.pallas import tpu as pltpu

BF16 = jnp.bfloat16
F32 = jnp.float32
I32 = jnp.int32

HEAD_DIM = 64
NSA_HEADS = 8
NSA_KV = 2
CMP_BLOCK = 32
CMP_STRIDE = 16
SEL_BLOCK = 64
SEL_TOPN = 16
NSA_WINDOW = 512
SWA_HEADS = 8
SWA_KV = 2
SWA_WINDOW = 128
Q_BLOCK = 128
ROPE_THETA = 10000.0
N_EXPERTS = 32
TOP_K = 4
SWIGLU_LIMIT = 7.0
SWIGLU_ALPHA = 1.702
LN_EPS = 1e-5

LANES = 128
BF16_SUBLANES = 16
F32_SUBLANES = 8
MXU_DEPTH = 256
MASKED = -1e30
M_INIT = -1e29
SEL_PENALTY = -(2.0 ** 100)
VMEM_LIMIT = 52 * 1024 * 1024
MOE_ROWS = 256
BAND_STEP_QUERIES = 512
CMP_STEP_QUERIES = 512
SEL_STEP_QUERIES = 128
SEL_CHUNK = 512
SEL_KEY_TILE = 512

R_NSA = NSA_HEADS // NSA_KV
R_SWA = SWA_HEADS // SWA_KV
NT_DIMS = (((1,), (1,)), ((), ()))


def _params(*sem):
    return pltpu.CompilerParams(dimension_semantics=sem, vmem_limit_bytes=VMEM_LIMIT)


def _full(shape):
    n = len(shape)
    return pl.BlockSpec(shape, lambda *_: (0,) * n)


def _proj_kernel(x_ref, wr_ref, wp_ref, wg_ref, cos_ref, sin_ref,
                 qn_rot_ref, qs_rot_ref, kk_rot_ref, qn_raw_ref, plain_ref, gates_ref):
    xb = x_ref[...].astype(BF16)
    acc = jnp.dot(xb, wr_ref[...], preferred_element_type=F32)
    cos = cos_ref[...]
    sin = sin_ref[...]
    lane = lax.broadcasted_iota(I32, cos.shape, 1)
    first_half = (lane & (HEAD_DIM - 1)) < HEAD_DIM // 2

    def rope(t):
        partner = jnp.where(first_half, pltpu.roll(t, LANES - HEAD_DIM // 2, 1),
                            pltpu.roll(t, HEAD_DIM // 2, 1))
        return (t * cos + partner * sin).astype(BF16)

    nq = qn_rot_ref.shape[1] // LANES
    ns = qs_rot_ref.shape[1] // LANES
    nk = kk_rot_ref.shape[1] // LANES
    for c in range(nq):
        qn_rot_ref[:, c * LANES:(c + 1) * LANES] = rope(acc[:, c * LANES:(c + 1) * LANES])
    for c in range(ns):
        o = (nq + c) * LANES
        qs_rot_ref[:, c * LANES:(c + 1) * LANES] = rope(acc[:, o:o + LANES])
    for c in range(nk):
        o = (nq + ns + c) * LANES
        kk_rot_ref[:, c * LANES:(c + 1) * LANES] = rope(acc[:, o:o + LANES])
    qn_raw_ref[...] = acc[:, :nq * LANES].astype(BF16)
    plain_ref[...] = jnp.dot(xb, wp_ref[...], preferred_element_type=F32).astype(BF16)
    gates_ref[...] = jnp.dot(xb, wg_ref[...], preferred_element_type=F32)


def _project(x2, w_rope, w_plain, w_gate, cos_t, sin_t, seq, tm):
    t, d = x2.shape
    nr, npl, ng = w_rope.shape[1], w_plain.shape[1], w_gate.shape[1]
    nqn, nqs = NSA_HEADS * HEAD_DIM, SWA_HEADS * HEAD_DIM
    nkk = nr - nqn - nqs
    spb = seq // tm
    row = lambda i: (i, 0)
    return pl.pallas_call(
        _proj_kernel,
        grid=(t // tm,),
        in_specs=[pl.BlockSpec((tm, d), row), _full(w_rope.shape), _full(w_plain.shape),
                  _full(w_gate.shape),
                  pl.BlockSpec((tm, LANES), lambda i: (i % spb, 0)),
                  pl.BlockSpec((tm, LANES), lambda i: (i % spb, 0))],
        out_specs=[pl.BlockSpec((tm, nqn), row), pl.BlockSpec((tm, nqs), row),
                   pl.BlockSpec((tm, nkk), row), pl.BlockSpec((tm, nqn), row),
                   pl.BlockSpec((tm, npl), row), pl.BlockSpec((tm, ng), row)],
        out_shape=[jax.ShapeDtypeStruct((t, nqn), BF16), jax.ShapeDtypeStruct((t, nqs), BF16),
                   jax.ShapeDtypeStruct((t, nkk), BF16), jax.ShapeDtypeStruct((t, nqn), BF16),
                   jax.ShapeDtypeStruct((t, npl), BF16), jax.ShapeDtypeStruct((t, ng), F32)],
        compiler_params=_params("parallel"),
        name="proj",
    )(x2, w_rope, w_plain, w_gate, cos_t, sin_t)


def _compress_kernel(a_ref, b_ref, pe_ref, w1_ref, w2_ref, out_ref):
    half = a_ref.shape[1]
    a = (a_ref[...].astype(F32) + pe_ref[0:1, :]).astype(BF16)
    b = (b_ref[...].astype(F32) + pe_ref[1:2, :]).astype(BF16)
    hid = jnp.dot(a, w1_ref[0:half, :], preferred_element_type=F32)
    hid = hid + jnp.dot(b, w1_ref[half:2 * half, :], preferred_element_type=F32)
    act = jax.nn.gelu(hid).astype(BF16)
    out_ref[...] = jnp.dot(act, w2_ref[...], preferred_element_type=F32).astype(BF16)


def _compress(t_lo, t_hi, pe2, w1, w2):
    two, bg, ncp, half = t_lo.shape
    hid = w1.shape[2]
    blk = lambda shape: pl.BlockSpec((None, None) + shape, lambda j, i: (j, i, 0, 0))
    wsp = lambda shape: pl.BlockSpec((None,) + shape, lambda j, i: (j, 0, 0))
    return pl.pallas_call(
        _compress_kernel,
        grid=(two, bg),
        in_specs=[blk((ncp, half)), blk((ncp, half)), wsp((2, half)), wsp((2 * half, hid)),
                  wsp((hid, HEAD_DIM))],
        out_specs=blk((ncp, HEAD_DIM)),
        out_shape=jax.ShapeDtypeStruct((two, bg, ncp, HEAD_DIM), BF16),
        compiler_params=_params("parallel", "parallel"),
        name="compress",
    )(t_lo, t_hi, pe2, w1, w2)


def _stack_heads(q, g, r):
    return jnp.concatenate(
        [q[:, (g * r + j) * HEAD_DIM:(g * r + j + 1) * HEAD_DIM] for j in range(r)], axis=0)


def _unstack_heads(parts, r, qb):
    return jnp.concatenate([o[j * qb:(j + 1) * qb] for o in parts for j in range(r)], axis=1)


def _unstack_heads_t(parts, r, qb):
    blocks = []
    for o in parts:
        for j in range(0, r, 2):
            pair = jnp.concatenate([o[:, j * qb:(j + 1) * qb], o[:, (j + 1) * qb:(j + 2) * qb]],
                                   axis=0)
            blocks.append(pair.T)
    return jnp.concatenate(blocks, axis=1)


def _topk_mask_cols(vals, k):
    n = vals.shape[0]
    row = lax.broadcasted_iota(I32, vals.shape, 0).astype(F32)
    taken = jnp.zeros(vals.shape, F32)
    work = vals
    for _ in range(k):
        mx = jnp.max(work, axis=0, keepdims=True)
        first = jnp.min(jnp.where(work == mx, row, float(n)), axis=0, keepdims=True)
        pick = row == first
        taken = jnp.where(pick, 1.0, taken)
        work = jnp.where(pick, -jnp.inf, work)
    return taken > 0.5


def _cmp_kernel(q_ref, kc_ref, vct_ref, ovt_ref, o_ref, notsel_ref):
    qb = q_ref.shape[0]
    ncp = kc_ref.shape[1]
    nselp = ovt_ref.shape[0]
    i = pl.program_id(1)
    rows = R_NSA * qb
    pos = i * qb + (lax.broadcasted_iota(I32, (1, rows), 1) & (qb - 1))
    cend = lax.broadcasted_iota(I32, (ncp, 1), 0) * CMP_STRIDE + (CMP_BLOCK - 1)
    bias = jnp.where(cend <= pos, 0.0, MASKED)
    live = (pos >= CMP_BLOCK - 1).astype(F32)
    q = q_ref[...]
    outs, imps = [], []
    for g in range(NSA_KV):
        qg = _stack_heads(q, g, R_NSA) * (HEAD_DIM ** -0.5)
        st = lax.dot_general(kc_ref[g], qg, NT_DIMS, preferred_element_type=F32)
        st = st + bias
        e = jnp.exp(st - jnp.max(st, axis=0, keepdims=True))
        pt = e * (live / jnp.sum(e, axis=0, keepdims=True))
        outs.append(jnp.dot(vct_ref[g], pt.astype(BF16), preferred_element_type=F32))
        psum = pt[:, 0:qb]
        for j in range(1, R_NSA):
            psum = psum + pt[:, j * qb:(j + 1) * qb]
        p_hi = psum.astype(BF16)
        p_lo = (psum - p_hi.astype(F32)).astype(BF16)
        imps.append(jnp.dot(ovt_ref[...], p_hi, preferred_element_type=F32)
                    + jnp.dot(ovt_ref[...], p_lo, preferred_element_type=F32))
    imp = jnp.concatenate(imps, axis=1)
    lane = lax.broadcasted_iota(I32, (1, NSA_KV * qb), 1)
    cur = (i * qb + (lane & (qb - 1))) >> 6
    jb = lax.broadcasted_iota(I32, (nselp, 1), 0)
    forced = (jb == 0) | (jb == cur) | (jb == cur - 1)
    imp = jnp.where(jb > cur, -1.0, jnp.where(forced, 1e6, imp))
    notsel = jnp.where(_topk_mask_cols(imp, SEL_TOPN), 0.0, 1.0)
    for g in range(NSA_KV):
        notsel_ref[g] = notsel[:, g * qb:(g + 1) * qb].T.astype(BF16)
    o_ref[...] = _unstack_heads_t(outs, R_NSA, qb)


def _cmp_attention(q_raw, kc, vc, overlap_t, qb):
    vc = vc.transpose(0, 1, 3, 2)
    b, s, hq = q_raw.shape
    _, g, ncp, dh = kc.shape
    nselp = overlap_t.shape[0]
    return pl.pallas_call(
        _cmp_kernel,
        grid=(b, s // qb),
        in_specs=[pl.BlockSpec((None, qb, hq), lambda bi, i: (bi, i, 0)),
                  pl.BlockSpec((None, g, ncp, dh), lambda bi, i: (bi, 0, 0, 0)),
                  pl.BlockSpec((None, g, dh, ncp), lambda bi, i: (bi, 0, 0, 0)),
                  _full(overlap_t.shape)],
        out_specs=[pl.BlockSpec((None, qb, hq), lambda bi, i: (bi, i, 0)),
                   pl.BlockSpec((None, g, qb, nselp), lambda bi, i: (bi, 0, i, 0))],
        out_shape=[jax.ShapeDtypeStruct((b, s, hq), F32),
                   jax.ShapeDtypeStruct((b, g, s, nselp), BF16)],
        compiler_params=_params("parallel", "parallel"),
        name="cmp_attn",
    )(q_raw, kc, vc, overlap_t)


def _sel_kernel(q_ref, notsel_ref, k_ref, vt_ref, o_ref, *score_bufs, tk):
    s_even, s_odd = score_bufs[:NSA_KV], score_bufs[NSA_KV:]
    qb = q_ref.shape[0]
    i = pl.program_id(1)
    rows = R_NSA * qb
    qpos = i * qb + (lax.broadcasted_iota(I32, (1, rows), 1) & (qb - 1))
    n_clear = (i * qb) // tk
    q = q_ref[...]
    q_augs = []
    for g in range(NSA_KV):
        qg = _stack_heads(q, g, R_NSA) * (HEAD_DIM ** -0.5)
        q_augs.append(jnp.concatenate(
            [qg, jnp.zeros((rows, LANES - HEAD_DIM), BF16),
             jnp.concatenate([notsel_ref[g]] * R_NSA, axis=0)], axis=1))

    def scores(kt, g):
        start = pl.multiple_of(kt * tk, tk)
        return lax.dot_general(k_ref[g, pl.ds(start, tk), :], q_augs[g], NT_DIMS,
                               preferred_element_type=F32)

    def consume(kt, g, s_ref, m, acc, causal):
        start = pl.multiple_of(kt * tk, tk)
        vt_t = vt_ref[g, :, pl.ds(start, tk)]
        ms, accs = [], []
        for c in range(rows // SEL_CHUNK):
            cols = slice(c * SEL_CHUNK, (c + 1) * SEL_CHUNK)
            st = s_ref[:, cols]
            if causal:
                kpos = start + lax.broadcasted_iota(I32, (tk, 1), 0)
                st = jnp.where(kpos <= qpos[:, cols], st, MASKED)
            m_new = jnp.maximum(m[:, cols], jnp.max(st, axis=0, keepdims=True))
            pt = jnp.exp(st - m_new).astype(BF16)
            accs.append(jnp.exp(m[:, cols] - m_new) * acc[:, cols]
                        + jnp.dot(vt_t, pt, preferred_element_type=F32))
            ms.append(m_new)
        return jnp.concatenate(ms, axis=1), jnp.concatenate(accs, axis=1)

    def advance(kt, carry, cur, nxt):
        new = []
        for g in range(NSA_KV):
            nxt[g][...] = scores(kt + 1, g)
            new.append(consume(kt, g, cur[g], *carry[g], False))
        return tuple(new)

    def pair(j, carry):
        carry = advance(2 * j, carry, s_even, s_odd)
        return advance(2 * j + 1, carry, s_odd, s_even)

    def finish(carry, cur):
        outs = []
        for g in range(NSA_KV):
            _, acc = consume(n_clear, g, cur[g], *carry[g], True)
            outs.append(acc[:HEAD_DIM] / acc[HEAD_DIM:HEAD_DIM + 1])
        o_ref[...] = _unstack_heads_t(outs, R_NSA, qb)

    for g in range(NSA_KV):
        s_even[g][...] = scores(0, g)
    init = tuple((jnp.full((1, rows), M_INIT, F32), jnp.zeros((LANES, rows), F32))
                 for _ in range(NSA_KV))
    carry = lax.fori_loop(0, n_clear // 2, pair, init)

    @pl.when((n_clear & 1) == 0)
    def _():
        finish(carry, s_even)

    @pl.when((n_clear & 1) == 1)
    def _():
        finish(advance(n_clear - 1, carry, s_even, s_odd), s_odd)


def _sel_attention(q_rot, notsel, k_aug, vt_aug, qb, tk):
    b, s, hq = q_rot.shape
    _, g, _, kw = k_aug.shape
    nselp = notsel.shape[3]
    return pl.pallas_call(
        functools.partial(_sel_kernel, tk=tk),
        grid=(b, s // qb),
        in_specs=[pl.BlockSpec((None, qb, hq), lambda bi, i: (bi, i, 0)),
                  pl.BlockSpec((None, g, qb, nselp), lambda bi, i: (bi, 0, i, 0)),
                  pl.BlockSpec((None, g, s, kw), lambda bi, i: (bi, 0, 0, 0)),
                  pl.BlockSpec((None, g, LANES, s), lambda bi, i: (bi, 0, 0, 0))],
        out_specs=pl.BlockSpec((None, qb, hq), lambda bi, i: (bi, i, 0)),
        out_shape=jax.ShapeDtypeStruct((b, s, hq), F32),
        scratch_shapes=[pltpu.VMEM((tk, R_NSA * qb), F32)] * (2 * g),
        compiler_params=_params("parallel", "parallel"),
        name="sel_attn",
    )(q_rot, notsel, k_aug, vt_aug)


def _band_kernel(*refs, window, wlen, r, kv, qb, has_sinks):
    if has_sinks:
        sink_ref, q_ref, k_ref, vt_ref, o_ref = refs
    else:
        q_ref, k_ref, vt_ref, o_ref = refs
    nsub = q_ref.shape[0] // qb
    rows = r * qb
    nv = kv * HEAD_DIM
    lane = lax.broadcasted_iota(I32, (1, rows), 1)

    def mask_bias(i):
        start = jnp.maximum((i + 1) * qb - wlen, 0)
        rel = i * qb + (lane & (qb - 1)) - start - lax.broadcasted_iota(I32, (wlen, 1), 0)
        return jnp.where((rel >= 0) & (rel < window), 0.0, MASKED)

    def body(shared_bias):
        first = pl.program_id(1) * nsub
        if shared_bias:
            bias = mask_bias(first)
        for sb in range(nsub):
            i = first + sb
            if not shared_bias:
                bias = mask_bias(i)
            start = pl.multiple_of(jnp.maximum((i + 1) * qb - wlen, 0), qb)
            q = q_ref[sb * qb:(sb + 1) * qb, :]
            kw = k_ref[pl.ds(start, wlen), :]
            vtw = vt_ref[:, pl.ds(start, wlen)]
            outs = []
            for g in range(kv):
                qg = _stack_heads(q, g, r) * (HEAD_DIM ** -0.5)
                st = lax.dot_general(kw[:, g * HEAD_DIM:(g + 1) * HEAD_DIM], qg, NT_DIMS,
                                     preferred_element_type=F32) + bias
                m = jnp.max(st, axis=0, keepdims=True)
                if has_sinks:
                    sk = jnp.full((1, rows), sink_ref[g * r], F32)
                    for j in range(1, r):
                        sk = jnp.where(lane >= j * qb, sink_ref[g * r + j], sk)
                    m = jnp.maximum(m, sk)
                e = jnp.exp(st - m).astype(BF16)
                v_ones = jnp.concatenate([vtw[g * HEAD_DIM:(g + 1) * HEAD_DIM, :], vtw[nv:, :]],
                                         axis=0)
                ot = jnp.dot(v_ones, e, preferred_element_type=F32)
                den = ot[HEAD_DIM:HEAD_DIM + 1]
                if has_sinks:
                    den = den + jnp.exp(sk - m)
                outs.append(ot[:HEAD_DIM] / den)
            o_ref[sb * qb:(sb + 1) * qb, :] = _unstack_heads_t(outs, r, qb)

    if nsub * qb >= wlen - qb:
        @pl.when(pl.program_id(1) == 0)
        def _():
            body(False)

        @pl.when(pl.program_id(1) > 0)
        def _():
            body(True)
    else:
        body(False)


def _band_attention(q_rot, k, v, window, sinks, qb):
    b, s, hq = q_rot.shape
    gk = k.shape[2]
    v = jnp.concatenate([v.transpose(0, 2, 1), jnp.ones((b, BF16_SUBLANES, s), v.dtype)], axis=1)
    kv = gk // HEAD_DIM
    r = hq // gk
    back = -(-window // qb)
    wlen = (back + 1) * qb
    assert wlen <= s
    has_sinks = sinks is not None
    tq = min(BAND_STEP_QUERIES, s)
    in_specs = [pl.BlockSpec((None, tq, hq), lambda bi, i: (bi, i, 0)),
                pl.BlockSpec((None, s, gk), lambda bi, i: (bi, 0, 0)),
                pl.BlockSpec((None, gk + BF16_SUBLANES, s), lambda bi, i: (bi, 0, 0))]
    args = [q_rot, k, v]
    if has_sinks:
        in_specs = [pl.BlockSpec(memory_space=pltpu.SMEM)] + in_specs
        args = [sinks.astype(F32)] + args
    return pl.pallas_call(
        functools.partial(_band_kernel, window=window, wlen=wlen, r=r, kv=kv, qb=qb,
                          has_sinks=has_sinks),
        grid=(b, s // tq),
        in_specs=in_specs,
        out_specs=pl.BlockSpec((None, tq, hq), lambda bi, i: (bi, i, 0)),
        out_shape=jax.ShapeDtypeStruct((b, s, hq), F32),
        compiler_params=_params("parallel", "parallel"),
        name="band_attn_sink" if has_sinks else "band_attn",
    )(*args)


def _layer_norm(v, g, b):
    mu = jnp.mean(v, axis=1, keepdims=True)
    c = v - mu
    var = jnp.mean(c * c, axis=1, keepdims=True)
    return c * lax.rsqrt(var + LN_EPS) * g + b


def _split_dot(a, w):
    hi = a.astype(BF16)
    lo = (a - hi.astype(F32)).astype(BF16)
    return (jnp.dot(hi, w, preferred_element_type=F32)
            + jnp.dot(lo, w, preferred_element_type=F32))


def _merge_kernel(ocmp_ref, osel_ref, owin_ref, oswa_ref, gates_ref, x_ref, exp_ref,
                  wbn_ref, wbs_ref, wo_ref, lng_ref, lnb_ref, wrh_ref, wrl_ref, br_ref,
                  h_ref, te_ref, tg_ref, tr_ref, counts_ref, base_ref, cnt_ref, *, alpha):
    d = x_ref.shape[1]
    hq = ocmp_ref.shape[1]
    gates = gates_ref[...]
    gn = jax.nn.sigmoid(gates[:, 2 * d:])
    gexp = _split_dot(gn, exp_ref[...])
    o_nsa = (gexp[:, 0:hq] * ocmp_ref[...] + gexp[:, hq:2 * hq] * osel_ref[...]
             + gexp[:, 2 * hq:3 * hq] * owin_ref[...])
    y_nsa = jnp.dot(o_nsa.astype(BF16), wbn_ref[...], preferred_element_type=F32)
    y_swa = jnp.dot(oswa_ref[...].astype(BF16), wbs_ref[...], preferred_element_type=F32)
    gm = jax.nn.sigmoid(gates[:, :2 * d])
    mixed = gm[:, :d] * y_nsa + gm[:, d:] * y_swa
    z = jnp.dot(mixed.astype(BF16), wo_ref[...], preferred_element_type=F32)
    h = _layer_norm(alpha * x_ref[...] + z, lng_ref[...], lnb_ref[...])
    h_ref[...] = h
    h_hi = h.astype(BF16)
    h_lo = (h - h_hi.astype(F32)).astype(BF16)
    logits = (jnp.dot(h_hi, wrh_ref[...], preferred_element_type=F32)
              + jnp.dot(h_lo, wrh_ref[...], preferred_element_type=F32)
              + jnp.dot(h_hi, wrl_ref[...], preferred_element_type=F32)) + br_ref[...]
    col = lax.broadcasted_iota(I32, logits.shape, 1).astype(F32)
    work = logits
    vals, ids = [], []
    for _ in range(TOP_K):
        mx = jnp.max(work, axis=1, keepdims=True)
        first = jnp.min(jnp.where(work == mx, col, float(LANES)), axis=1, keepdims=True)
        vals.append(mx)
        ids.append(first)
        work = jnp.where(col == first, -jnp.inf, work)
    es = [jnp.exp(v - vals[0]) for v in vals]
    den = es[0]
    for e in es[1:]:
        den = den + e
    @pl.when(pl.program_id(0) == 0)
    def _():
        cnt_ref[...] = jnp.zeros(cnt_ref.shape, F32)

    tm = logits.shape[0]
    hits = jnp.zeros(logits.shape, F32)
    for k in range(TOP_K):
        hits = jnp.where(col == ids[k], 1.0, hits)
    earlier = (lax.broadcasted_iota(I32, (tm, tm), 1)
               < lax.broadcasted_iota(I32, (tm, tm), 0))
    before = jnp.dot(jnp.where(earlier, 1.0, 0.0).astype(BF16), hits.astype(BF16),
                     preferred_element_type=F32) + cnt_ref[...]
    base_ref[...] = cnt_ref[...]
    cnt_ref[...] = cnt_ref[...] + jnp.sum(hits, axis=0, keepdims=True)
    counts_ref[...] = cnt_ref[...]
    te = jnp.zeros(logits.shape, F32)
    tg = jnp.zeros(logits.shape, F32)
    tr = jnp.zeros(logits.shape, F32)
    for k in range(TOP_K):
        rank = jnp.sum(jnp.where(col == ids[k], before, 0.0), axis=1, keepdims=True)
        te = jnp.where(col == float(k), ids[k], te)
        tg = jnp.where(col == float(k), es[k] / den, tg)
        tr = jnp.where(col == float(k), rank, tr)
    te_ref[...] = te.astype(I32)
    tg_ref[...] = tg
    tr_ref[...] = tr.astype(I32)


def _merge(o_cmp, o_sel, o_win, o_swa, gates, x2, expand, w_bn, w_bs, w_o, ln_g, ln_b,
           wr_hi, wr_lo, b_r, alpha, tm):
    t, d = x2.shape
    hq = o_cmp.shape[1]
    row = lambda i: (i, 0)
    tok = lambda n: pl.BlockSpec((tm, n), row)
    return pl.pallas_call(
        functools.partial(_merge_kernel, alpha=alpha),
        grid=(t // tm,),
        in_specs=[tok(hq), tok(hq), tok(hq), tok(hq), tok(gates.shape[1]), tok(d),
                  _full(expand.shape), _full(w_bn.shape), _full(w_bs.shape), _full(w_o.shape),
                  _full(ln_g.shape), _full(ln_b.shape), _full(wr_hi.shape), _full(wr_lo.shape),
                  _full(b_r.shape)],
        out_specs=[tok(d), tok(LANES), tok(LANES), tok(LANES), _full((1, LANES)),
                   pl.BlockSpec((None, 1, LANES), lambda i: (i, 0, 0))],
        out_shape=[jax.ShapeDtypeStruct((t, d), F32), jax.ShapeDtypeStruct((t, LANES), I32),
                   jax.ShapeDtypeStruct((t, LANES), F32), jax.ShapeDtypeStruct((t, LANES), I32),
                   jax.ShapeDtypeStruct((1, LANES), F32),
                   jax.ShapeDtypeStruct((t // tm, 1, LANES), F32)],
        scratch_shapes=[pltpu.VMEM((1, LANES), F32)],
        compiler_params=_params("arbitrary"),
        name="merge_ln_router",
    )(o_cmp, o_sel, o_win, o_swa, gates, x2, expand, w_bn, w_bs, w_o, ln_g, ln_b,
      wr_hi, wr_lo, b_r)


def _dispatch_kernel(pend_ref, padded_ref, dest_ref, h_ref, xs_ref, zbuf, sem, zsem):
    tm = h_ref.shape[0]
    n_rows = xs_ref.shape[0]

    @pl.when(pl.program_id(0) == 0)
    def _():
        zbuf[...] = jnp.zeros(zbuf.shape, F32)
        used_rows = pend_ref[N_EXPERTS - 1]

        def zero_copy(start):
            start = pl.multiple_of(start, MOE_ROWS)
            return pltpu.make_async_copy(zbuf, xs_ref.at[pl.ds(start, MOE_ROWS), :], zsem)

        blocks = [(padded_ref[e] > 0, pend_ref[e] - MOE_ROWS) for e in range(N_EXPERTS)]
        blocks += [(used_rows + j * MOE_ROWS < n_rows, used_rows + j * MOE_ROWS)
                   for j in range(N_EXPERTS)]
        for cond, start in blocks:
            @pl.when(cond)
            def _(start=start):
                zero_copy(start).start()
        for cond, start in blocks:
            @pl.when(cond)
            def _(start=start):
                zero_copy(start).wait()

    def issue(r, carry):
        for k in range(TOP_K):
            dst = dest_ref[r * TOP_K + k]
            pltpu.make_async_copy(h_ref.at[pl.ds(r, 1), :], xs_ref.at[pl.ds(dst, 1), :],
                                  sem).start(priority=k % 2)
        return carry

    lax.fori_loop(0, tm, issue, 0, unroll=8)
    for _ in range(TOP_K):
        pltpu.make_async_copy(h_ref, xs_ref.at[pl.ds(0, tm), :], sem).wait()


def _dispatch(pend, padded, dest, h, n_rows, tm):
    t, d = h.shape
    return pl.pallas_call(
        _dispatch_kernel,
        grid=(t // tm,),
        in_specs=[pl.BlockSpec(memory_space=pltpu.SMEM), pl.BlockSpec(memory_space=pltpu.SMEM),
                  pl.BlockSpec((tm * TOP_K,), lambda i: (i,), memory_space=pltpu.SMEM),
                  pl.BlockSpec((tm, d), lambda i: (i, 0))],
        out_specs=pl.BlockSpec(memory_space=pl.ANY),
        out_shape=jax.ShapeDtypeStruct((n_rows, d), F32),
        scratch_shapes=[pltpu.VMEM((MOE_ROWS, d), F32), pltpu.SemaphoreType.DMA(()),
                        pltpu.SemaphoreType.DMA(())],
        compiler_params=_params("arbitrary"),
        name="moe_dispatch",
    )(pend, padded, dest, h)


def _moe_kernel(blk_e_ref, n_used_ref, x_ref, wi_ref, bi_ref, wo_ref, bo_ref, y_ref,
                wi_bf, wo_bf):
    f = wo_ref.shape[0]
    b = pl.program_id(0)

    @pl.when((b == 0) | (blk_e_ref[b] != blk_e_ref[jnp.maximum(b - 1, 0)]))
    def _():
        wi_bf[...] = wi_ref[...].astype(BF16)
        wo_bf[...] = wo_ref[...].astype(BF16)

    @pl.when(b < n_used_ref[0])
    def _():
        hdn = jnp.dot(x_ref[...].astype(BF16), wi_bf[...],
                      preferred_element_type=F32) + bi_ref[...]
        hg = jnp.minimum(hdn[:, :f], SWIGLU_LIMIT)
        hu = jnp.clip(hdn[:, f:], -SWIGLU_LIMIT, SWIGLU_LIMIT)
        act = hg * jax.nn.sigmoid(SWIGLU_ALPHA * hg) * (hu + 1.0)
        y_ref[...] = jnp.dot(act.astype(BF16), wo_bf[...],
                             preferred_element_type=F32) + bo_ref[...]

    @pl.when(b >= n_used_ref[0])
    def _():
        y_ref[...] = jnp.zeros(y_ref.shape, F32)


def _moe_blocks(blk_e, n_used, xs, w_in, b_in, w_out, b_out):
    n_rows, d = xs.shape
    e, _, f2 = w_in.shape
    f = w_out.shape[1]
    n_blk = n_rows // MOE_ROWS
    grid_spec = pltpu.PrefetchScalarGridSpec(
        num_scalar_prefetch=2,
        grid=(n_blk,),
        in_specs=[pl.BlockSpec((MOE_ROWS, d), lambda b, be, nu: (jnp.minimum(b, nu[0] - 1), 0)),
                  pl.BlockSpec((None, d, f2), lambda b, be, nu: (be[b], 0, 0)),
                  pl.BlockSpec((None, 1, f2), lambda b, be, nu: (be[b], 0, 0)),
                  pl.BlockSpec((None, f, d), lambda b, be, nu: (be[b], 0, 0)),
                  pl.BlockSpec((None, 1, d), lambda b, be, nu: (be[b], 0, 0))],
        out_specs=pl.BlockSpec((MOE_ROWS, d), lambda b, be, nu: (b, 0)),
        scratch_shapes=[pltpu.VMEM((d, f2), BF16), pltpu.VMEM((f, d), BF16)],
    )
    return pl.pallas_call(
        _moe_kernel,
        grid_spec=grid_spec,
        out_shape=jax.ShapeDtypeStruct((n_rows, d), F32),
        compiler_params=_params("arbitrary"),
        name="moe_experts",
    )(blk_e, n_used, xs, w_in, b_in, w_out, b_out)


def _run_sizes(tm):
    sizes, p = [], F32_SUBLANES
    while p <= tm:
        sizes.append(p)
        p *= 2
    return tuple(reversed(sizes))


def _stage_rows(tm):
    rows = tm * TOP_K + N_EXPERTS * 2 * (F32_SUBLANES - 1)
    return -(-rows // MXU_DEPTH) * MXU_DEPTH


def _final_kernel(start_ref, len_ref, seg_ref, tot_ref, h_ref, tg_ref, sp_ref, ys_ref, g_ref,
                  b_ref, o_ref, stage0, stage1, sem, *, alpha):
    i = pl.program_id(0)
    tm = h_ref.shape[0]
    stages = (stage0, stage1)
    n_stage = stage0.shape[0]

    def fetch(tile, into):
        for e in range(N_EXPERTS):
            ln = len_ref[tile * N_EXPERTS + e]
            first = start_ref[tile * N_EXPERTS + e]
            seg = seg_ref[tile * N_EXPERTS + e]
            for p in _run_sizes(tm):
                done = ln & (-2 * p)
                src = ys_ref.at[pl.ds(pl.multiple_of(first + done, F32_SUBLANES), p), :]
                dst = stages[into].at[pl.ds(pl.multiple_of(seg + done, F32_SUBLANES), p), :]
                pl.when((ln & p) != 0)(pltpu.make_async_copy(src, dst, sem.at[into]).start)

    @pl.when(i == 0)
    def _():
        stage0[...] = jnp.zeros(stage0.shape, F32)
        stage1[...] = jnp.zeros(stage1.shape, F32)
        fetch(0, 0)

    for slot in range(2):
        @pl.when((i & 1) == slot)
        def _(slot=slot):
            fetch(i + 1, 1 - slot)
            rows = pl.multiple_of(tot_ref[i], F32_SUBLANES)
            pltpu.make_async_copy(ys_ref.at[pl.ds(0, rows), :],
                                  stages[slot].at[pl.ds(0, rows), :], sem.at[slot]).wait()
            col = lax.broadcasted_iota(I32, (1, n_stage), 1)
            tg = tg_ref[...]
            sp = sp_ref[...]
            q = jnp.zeros((tm, n_stage), F32)
            for k in range(TOP_K):
                q = jnp.where(col == sp[:, k:k + 1], tg[:, k:k + 1], q)
            f = _split_dot(q, stages[slot][...].astype(BF16))
            o_ref[...] = _layer_norm(alpha * h_ref[...] + f, g_ref[...], b_ref[...])


def _final(run_start, run_len, run_seg, run_tot, h, top_g, spos, ys, ln_g, ln_b, alpha, tm):
    t, d = h.shape
    grid_spec = pltpu.PrefetchScalarGridSpec(
        num_scalar_prefetch=4,
        grid=(t // tm,),
        in_specs=[pl.BlockSpec((tm, d), lambda i, *_: (i, 0)),
                  pl.BlockSpec((tm, LANES), lambda i, *_: (i, 0)),
                  pl.BlockSpec((tm, TOP_K), lambda i, *_: (i, 0)),
                  pl.BlockSpec(memory_space=pl.ANY),
                  pl.BlockSpec(ln_g.shape, lambda i, *_: (0, 0)),
                  pl.BlockSpec(ln_b.shape, lambda i, *_: (0, 0))],
        out_specs=pl.BlockSpec((tm, d), lambda i, *_: (i, 0)),
        scratch_shapes=[pltpu.VMEM((_stage_rows(tm), d), F32),
                        pltpu.VMEM((_stage_rows(tm), d), F32), pltpu.SemaphoreType.DMA((2,))],
    )
    return pl.pallas_call(
        functools.partial(_final_kernel, alpha=alpha),
        grid_spec=grid_spec,
        out_shape=jax.ShapeDtypeStruct((t, d), F32),
        compiler_params=_params("arbitrary"),
        name="combine_ln",
    )(run_start, run_len, run_seg, run_tot, h, top_g, spos, ys, ln_g, ln_b)


def _rope_tables(s):
    half = HEAD_DIM // 2
    inv = ROPE_THETA ** (-jnp.arange(half, dtype=F32) / half)
    ang = jnp.arange(s, dtype=F32)[:, None] * inv[None, :]
    cos, sin = jnp.cos(ang), jnp.sin(ang)
    reps = LANES // HEAD_DIM
    cos_t = jnp.tile(jnp.concatenate([cos, cos], axis=1), (1, reps))
    sin_t = jnp.tile(jnp.concatenate([-sin, sin], axis=1), (1, reps))
    return cos_t, sin_t


def _moe_plan(top_e, rank, counts, t):
    a = t * TOP_K
    padded = (counts + MOE_ROWS - 1) // MOE_ROWS * MOE_ROWS
    pend = jnp.cumsum(padded)
    pstart = pend - padded
    experts = jnp.arange(N_EXPERTS, dtype=I32)
    base = jnp.sum(jnp.where(top_e[:, :, None] == experts, pstart, 0), axis=2)
    dest = (base + rank).astype(I32).reshape(a)
    n_blk = -(-a // MOE_ROWS) + N_EXPERTS
    blk_first = jnp.arange(n_blk, dtype=I32) * MOE_ROWS
    blk_e = jnp.minimum(jnp.sum(pend[None, :] <= blk_first[:, None], axis=1),
                        N_EXPERTS - 1).astype(I32)
    n_used = (pend[-1] // MOE_ROWS).astype(I32).reshape(1)
    return (dest, blk_e, n_used, n_blk * MOE_ROWS, pend.astype(I32), padded.astype(I32),
            pstart.astype(I32))


def _combine_plan(top_e, rank, tile_base, counts, pstart, tm):
    nxt = jnp.concatenate([tile_base[1:], counts[None, :]], axis=0)
    cnt = nxt - tile_base
    first = pstart[None, :] + tile_base
    lead = first & (F32_SUBLANES - 1)
    run_len = jnp.where(cnt > 0, (lead + cnt + F32_SUBLANES - 1) // F32_SUBLANES * F32_SUBLANES,
                        0)
    run_seg = jnp.cumsum(run_len, axis=1) - run_len
    shift = jnp.repeat(run_seg + lead - tile_base, tm, axis=0)
    experts = jnp.arange(N_EXPERTS, dtype=I32)
    spos = jnp.sum(jnp.where(top_e[:, :, None] == experts, shift[:, None, :], 0), axis=2) + rank
    flat = lambda v: jnp.pad(v.astype(I32), ((0, 1), (0, 0))).reshape(-1)
    run_tot = jnp.sum(run_len, axis=1).astype(I32)
    return flat(first - lead), flat(run_len), flat(run_seg), run_tot, spos.astype(I32)


def _layer(x, w_in, k_pe, k_w1, k_w2, v_pe, v_w1, v_w2, sinks, w_br_nsa, w_br_swa, w_out,
           ln1_g, ln1_b, w_router, b_router, w_e_in, b_e_in, w_e_out, b_e_out, ln2_g, ln2_b,
           alpha):
    b, s, d = x.shape
    t = b * s
    qb = Q_BLOCK
    nq_n, nkv = NSA_HEADS * HEAD_DIM, NSA_KV * HEAD_DIM
    nq_s, nkv_s = SWA_HEADS * HEAD_DIM, SWA_KV * HEAD_DIM
    widths = (nq_n, nkv, nkv, nkv, nkv, nkv, nkv, NSA_HEADS * 3, nq_s, nkv_s, nkv_s, 2 * d)
    offs = [0]
    for w in widths:
        offs.append(offs[-1] + w)
    col = lambda j: w_in[:, offs[j]:offs[j + 1]]
    (c_qn, c_kc, c_vc, c_ks, c_vs, c_kw, c_vw, c_gn, c_qs, c_k_s, c_v_s, c_gm) = map(col, range(12))
    w_rope = jnp.concatenate([c_qn, c_qs, c_ks, c_kw, c_k_s], axis=1).astype(BF16)
    w_plain = jnp.concatenate([c_kc, c_vc, c_vs, c_vw, c_v_s], axis=1).astype(BF16)
    gn_pad = LANES - NSA_HEADS * 3
    w_gate = jnp.concatenate([c_gm, c_gn, jnp.zeros((d, gn_pad), F32)], axis=1).astype(BF16)
    cos_t, sin_t = _rope_tables(s)

    x2 = x.reshape(t, d)
    qn_rot, qs_rot, kk_rot, qn_raw, plain, gates = _project(
        x2, w_rope, w_plain, w_gate, cos_t, sin_t, s, min(256, s))

    nc = (s - CMP_BLOCK) // CMP_STRIDE + 1
    ncp = s // CMP_STRIDE
    half = CMP_STRIDE * HEAD_DIM

    def halves(cols):
        v = cols.reshape(b, s, NSA_KV, HEAD_DIM).transpose(0, 2, 1, 3)
        return v.reshape(b * NSA_KV, ncp, half)

    t2 = jnp.stack([halves(plain[:, 0:nkv]), halves(plain[:, nkv:2 * nkv])])
    t_lo = t2
    t_hi = jnp.concatenate([t2[:, :, 1:], jnp.zeros_like(t2[:, :, :1])], axis=2)
    pe2 = jnp.stack([k_pe.reshape(2, half), v_pe.reshape(2, half)])
    w1 = jnp.stack([k_w1, v_w1]).astype(BF16)
    w2 = jnp.stack([k_w2, v_w2]).astype(BF16)
    kvc = _compress(t_lo, t_hi, pe2, w1, w2).reshape(2, b, NSA_KV, ncp, HEAD_DIM)

    nsel = s // SEL_BLOCK
    nselp = -(-nsel // LANES) * LANES
    cstart = jnp.arange(ncp) * CMP_STRIDE
    sstart = jnp.arange(nselp) * SEL_BLOCK
    overlap = ((cstart[:, None] < sstart[None, :] + SEL_BLOCK)
               & (cstart[:, None] + CMP_BLOCK > sstart[None, :])
               & (jnp.arange(ncp)[:, None] < nc) & (jnp.arange(nselp)[None, :] < nsel))
    o_cmp, notsel = _cmp_attention(qn_raw.reshape(b, s, nq_n), kvc[0], kvc[1],
                                   overlap.T.astype(BF16), min(CMP_STEP_QUERIES, s))

    def group_major(cols):
        return cols.reshape(b, s, NSA_KV, HEAD_DIM).transpose(0, 2, 1, 3)

    k_sel = group_major(kk_rot[:, 0:nkv])
    v_sel = group_major(plain[:, 2 * nkv:3 * nkv])
    onehot = (jnp.arange(s)[:, None] // SEL_BLOCK == jnp.arange(nselp)[None, :])
    k_tail = jnp.concatenate([jnp.zeros((s, LANES - HEAD_DIM), F32),
                              jnp.where(onehot, SEL_PENALTY, 0.0)], axis=1).astype(BF16)
    k_aug = jnp.concatenate(
        [k_sel, jnp.broadcast_to(k_tail, (b, NSA_KV) + k_tail.shape)], axis=3)
    v_tail = jnp.zeros((LANES - HEAD_DIM, 1), BF16).at[0].set(1.0)
    vt_aug = jnp.concatenate(
        [v_sel.transpose(0, 1, 3, 2),
         jnp.broadcast_to(v_tail, (b, NSA_KV, LANES - HEAD_DIM, s))], axis=2)
    o_sel = _sel_attention(qn_rot.reshape(b, s, nq_n), notsel, k_aug, vt_aug,
                           min(SEL_STEP_QUERIES, s), min(SEL_KEY_TILE, s))

    o_win = _band_attention(qn_rot.reshape(b, s, nq_n), kk_rot[:, nkv:2 * nkv].reshape(b, s, nkv),
                            plain[:, 3 * nkv:4 * nkv].reshape(b, s, nkv), NSA_WINDOW, None, qb)
    o_swa = _band_attention(qs_rot.reshape(b, s, nq_s),
                            kk_rot[:, 2 * nkv:2 * nkv + nkv_s].reshape(b, s, nkv_s),
                            plain[:, 4 * nkv:4 * nkv + nkv_s].reshape(b, s, nkv_s),
                            SWA_WINDOW, sinks, qb)

    gi = jnp.arange(LANES)
    ci = jnp.arange(3 * nq_n)
    expand = ((gi[:, None] // 3 == (ci[None, :] % nq_n) // HEAD_DIM)
              & (gi[:, None] % 3 == ci[None, :] // nq_n)
              & (gi[:, None] < NSA_HEADS * 3)).astype(BF16)
    wr_pad = jnp.pad(w_router, ((0, 0), (0, LANES - N_EXPERTS)))
    wr_hi = wr_pad.astype(BF16)
    wr_lo = (wr_pad - wr_hi.astype(F32)).astype(BF16)
    b_r = jnp.concatenate([b_router, jnp.full((LANES - N_EXPERTS,), -jnp.inf, F32)]).reshape(1, LANES)
    tm = min(256, t)
    h, top_e, top_g, rank, counts, tile_base = _merge(
        o_cmp.reshape(t, nq_n), o_sel.reshape(t, nq_n), o_win.reshape(t, nq_n),
        o_swa.reshape(t, nq_s), gates, x2, expand, w_br_nsa.astype(BF16), w_br_swa.astype(BF16),
        w_out.astype(BF16), ln1_g.reshape(1, d), ln1_b.reshape(1, d), wr_hi, wr_lo, b_r,
        alpha, tm)

    top_e, rank = top_e[:, :TOP_K], rank[:, :TOP_K]
    counts = counts[0, :N_EXPERTS].astype(I32)
    dest, blk_e, n_used, n_rows, pend, padded, pstart = _moe_plan(top_e, rank, counts, t)
    xs = _dispatch(pend, padded, dest, h, n_rows, tm)
    ys = _moe_blocks(blk_e, n_used, xs, w_e_in, b_e_in.reshape(N_EXPERTS, 1, -1),
                     w_e_out, b_e_out.reshape(N_EXPERTS, 1, -1))
    run_start, run_len, run_seg, run_tot, spos = _combine_plan(
        top_e, rank, tile_base[:, 0, :N_EXPERTS].astype(I32), counts, pstart, tm)
    out = _final(run_start, run_len, run_seg, run_tot, h, top_g, spos, ys, ln2_g.reshape(1, d),
                 ln2_b.reshape(1, d), alpha, tm)
    return out.reshape(b, s, d)


def kernel(x, w_in, nsa_k_pe, nsa_k_w1, nsa_k_w2, nsa_v_pe, nsa_v_w1, nsa_v_w2, swa_sinks, w_br_nsa, w_br_swa, w_out, ln1_g, ln1_b, w_router, b_router, w_expert_in, b_expert_in, w_expert_out, b_expert_out, ln2_g, ln2_b):
    depth = w_in.shape[0]
    alpha = (2.0 * depth) ** 0.25
    for l in range(depth):
        x = _layer(x, w_in[l], nsa_k_pe[l], nsa_k_w1[l], nsa_k_w2[l], nsa_v_pe[l], nsa_v_w1[l],
                   nsa_v_w2[l], swa_sinks[l], w_br_nsa[l], w_br_swa[l], w_out[l], ln1_g[l],
                   ln1_b[l], w_router[l], b_router[l], w_expert_in[l], b_expert_in[l],
                   w_expert_out[l], b_expert_out[l], ln2_g[l], ln2_b[l], alpha)
    return x
```

```python
import functools

import jax
import jax.numpy as jnp
from jax import lax
from jax.experimental import pallas as pl
from jax.experimental.pallas import tpu as pltpu

BF16 = jnp.bfloat16
F32 = jnp.float32
I32 = jnp.int32

HEAD_DIM = 64
NSA_HEADS = 8
NSA_KV = 2
CMP_BLOCK = 32
CMP_STRIDE = 16
SEL_BLOCK = 64
SEL_TOPN = 16
NSA_WINDOW = 512
SWA_HEADS = 8
SWA_KV = 2
SWA_WINDOW = 128
Q_BLOCK = 128
ROPE_THETA = 10000.0
N_EXPERTS = 32
TOP_K = 4
SWIGLU_LIMIT = 7.0
SWIGLU_ALPHA = 1.702
LN_EPS = 1e-5

LANES = 128
BF16_SUBLANES = 16
F32_SUBLANES = 8
MXU_DEPTH = 256
MASKED = -1e30
M_INIT = -1e29
SEL_PENALTY = -(2.0 ** 100)
VMEM_LIMIT = 52 * 1024 * 1024
MOE_ROWS = 256
DISPATCH_TOKENS = 512
BAND_STEP_QUERIES = 512
CMP_STEP_QUERIES = 512
SEL_STEP_QUERIES = 128
SEL_CHUNK = 512
SEL_KEY_TILE = 512

R_NSA = NSA_HEADS // NSA_KV
R_SWA = SWA_HEADS // SWA_KV
NT_DIMS = (((1,), (1,)), ((), ()))


def _params(*sem):
    return pltpu.CompilerParams(dimension_semantics=sem, vmem_limit_bytes=VMEM_LIMIT)


def _full(shape):
    n = len(shape)
    return pl.BlockSpec(shape, lambda *_: (0,) * n)


def _proj_kernel(x_ref, wr_ref, wp_ref, wg_ref, cos_ref, sin_ref,
                 qn_rot_ref, qs_rot_ref, kk_rot_ref, qn_raw_ref, plain_ref, gates_ref):
    xb = x_ref[...].astype(BF16)
    acc = jnp.dot(xb, wr_ref[...], preferred_element_type=F32)
    cos = cos_ref[...]
    sin = sin_ref[...]
    lane = lax.broadcasted_iota(I32, cos.shape, 1)
    first_half = (lane & (HEAD_DIM - 1)) < HEAD_DIM // 2

    def rope(t):
        partner = jnp.where(first_half, pltpu.roll(t, LANES - HEAD_DIM // 2, 1),
                            pltpu.roll(t, HEAD_DIM // 2, 1))
        return (t * cos + partner * sin).astype(BF16)

    nq = qn_rot_ref.shape[1] // LANES
    ns = qs_rot_ref.shape[1] // LANES
    nk = kk_rot_ref.shape[1] // LANES
    for c in range(nq):
        qn_rot_ref[:, c * LANES:(c + 1) * LANES] = rope(acc[:, c * LANES:(c + 1) * LANES])
    for c in range(ns):
        o = (nq + c) * LANES
        qs_rot_ref[:, c * LANES:(c + 1) * LANES] = rope(acc[:, o:o + LANES])
    for c in range(nk):
        o = (nq + ns + c) * LANES
        kk_rot_ref[:, c * LANES:(c + 1) * LANES] = rope(acc[:, o:o + LANES])
    qn_raw_ref[...] = acc[:, :nq * LANES].astype(BF16)
    plain_ref[...] = jnp.dot(xb, wp_ref[...], preferred_element_type=F32).astype(BF16)
    gates_ref[...] = jnp.dot(xb, wg_ref[...], preferred_element_type=F32)


def _project(x2, w_rope, w_plain, w_gate, cos_t, sin_t, seq, tm):
    t, d = x2.shape
    nr, npl, ng = w_rope.shape[1], w_plain.shape[1], w_gate.shape[1]
    nqn, nqs = NSA_HEADS * HEAD_DIM, SWA_HEADS * HEAD_DIM
    nkk = nr - nqn - nqs
    spb = seq // tm
    row = lambda i: (i, 0)
    return pl.pallas_call(
        _proj_kernel,
        grid=(t // tm,),
        in_specs=[pl.BlockSpec((tm, d), row), _full(w_rope.shape), _full(w_plain.shape),
                  _full(w_gate.shape),
                  pl.BlockSpec((tm, LANES), lambda i: (i % spb, 0)),
                  pl.BlockSpec((tm, LANES), lambda i: (i % spb, 0))],
        out_specs=[pl.BlockSpec((tm, nqn), row), pl.BlockSpec((tm, nqs), row),
                   pl.BlockSpec((tm, nkk), row), pl.BlockSpec((tm, nqn), row),
                   pl.BlockSpec((tm, npl), row), pl.BlockSpec((tm, ng), row)],
        out_shape=[jax.ShapeDtypeStruct((t, nqn), BF16), jax.ShapeDtypeStruct((t, nqs), BF16),
                   jax.ShapeDtypeStruct((t, nkk), BF16), jax.ShapeDtypeStruct((t, nqn), BF16),
                   jax.ShapeDtypeStruct((t, npl), BF16), jax.ShapeDtypeStruct((t, ng), F32)],
        compiler_params=_params("parallel"),
        name="proj",
    )(x2, w_rope, w_plain, w_gate, cos_t, sin_t)


def _compress_kernel(a_ref, b_ref, pe_ref, w1_ref, w2_ref, out_ref):
    half = a_ref.shape[1]
    a = (a_ref[...].astype(F32) + pe_ref[0:1, :]).astype(BF16)
    b = (b_ref[...].astype(F32) + pe_ref[1:2, :]).astype(BF16)
    hid = jnp.dot(a, w1_ref[0:half, :], preferred_element_type=F32)
    hid = hid + jnp.dot(b, w1_ref[half:2 * half, :], preferred_element_type=F32)
    act = jax.nn.gelu(hid).astype(BF16)
    out_ref[...] = jnp.dot(act, w2_ref[...], preferred_element_type=F32).astype(BF16)


def _compress(t_lo, t_hi, pe2, w1, w2):
    two, bg, ncp, half = t_lo.shape
    hid = w1.shape[2]
    blk = lambda shape: pl.BlockSpec((None, None) + shape, lambda j, i: (j, i, 0, 0))
    wsp = lambda shape: pl.BlockSpec((None,) + shape, lambda j, i: (j, 0, 0))
    return pl.pallas_call(
        _compress_kernel,
        grid=(two, bg),
        in_specs=[blk((ncp, half)), blk((ncp, half)), wsp((2, half)), wsp((2 * half, hid)),
                  wsp((hid, HEAD_DIM))],
        out_specs=blk((ncp, HEAD_DIM)),
        out_shape=jax.ShapeDtypeStruct((two, bg, ncp, HEAD_DIM), BF16),
        compiler_params=_params("parallel", "parallel"),
        name="compress",
    )(t_lo, t_hi, pe2, w1, w2)


def _stack_heads(q, g, r):
    return jnp.concatenate(
        [q[:, (g * r + j) * HEAD_DIM:(g * r + j + 1) * HEAD_DIM] for j in range(r)], axis=0)


def _unstack_heads(parts, r, qb):
    return jnp.concatenate([o[j * qb:(j + 1) * qb] for o in parts for j in range(r)], axis=1)


def _unstack_heads_t(parts, r, qb):
    blocks = []
    for o in parts:
        for j in range(0, r, 2):
            pair = jnp.concatenate([o[:, j * qb:(j + 1) * qb], o[:, (j + 1) * qb:(j + 2) * qb]],
                                   axis=0)
            blocks.append(pair.T)
    return jnp.concatenate(blocks, axis=1)


def _topk_mask_cols(vals, k):
    n = vals.shape[0]
    row = lax.broadcasted_iota(I32, vals.shape, 0).astype(F32)
    taken = jnp.zeros(vals.shape, F32)
    work = vals
    for _ in range(k):
        mx = jnp.max(work, axis=0, keepdims=True)
        first = jnp.min(jnp.where(work == mx, row, float(n)), axis=0, keepdims=True)
        pick = row == first
        taken = jnp.where(pick, 1.0, taken)
        work = jnp.where(pick, -jnp.inf, work)
    return taken > 0.5


def _cmp_kernel(q_ref, kc_ref, vct_ref, ovt_ref, o_ref, notsel_ref):
    qb = q_ref.shape[0]
    ncp = kc_ref.shape[1]
    nselp = ovt_ref.shape[0]
    i = pl.program_id(1)
    rows = R_NSA * qb
    pos = i * qb + (lax.broadcasted_iota(I32, (1, rows), 1) & (qb - 1))
    cend = lax.broadcasted_iota(I32, (ncp, 1), 0) * CMP_STRIDE + (CMP_BLOCK - 1)
    bias = jnp.where(cend <= pos, 0.0, MASKED)
    live = (pos >= CMP_BLOCK - 1).astype(F32)
    q = q_ref[...]
    outs, imps = [], []
    for g in range(NSA_KV):
        qg = _stack_heads(q, g, R_NSA) * (HEAD_DIM ** -0.5)
        st = lax.dot_general(kc_ref[g], qg, NT_DIMS, preferred_element_type=F32)
        st = st + bias
        e = jnp.exp(st - jnp.max(st, axis=0, keepdims=True))
        pt = e * (live / jnp.sum(e, axis=0, keepdims=True))
        outs.append(jnp.dot(vct_ref[g], pt.astype(BF16), preferred_element_type=F32))
        psum = pt[:, 0:qb]
        for j in range(1, R_NSA):
            psum = psum + pt[:, j * qb:(j + 1) * qb]
        p_hi = psum.astype(BF16)
        p_lo = (psum - p_hi.astype(F32)).astype(BF16)
        imps.append(jnp.dot(ovt_ref[...], p_hi, preferred_element_type=F32)
                    + jnp.dot(ovt_ref[...], p_lo, preferred_element_type=F32))
    imp = jnp.concatenate(imps, axis=1)
    lane = lax.broadcasted_iota(I32, (1, NSA_KV * qb), 1)
    cur = (i * qb + (lane & (qb - 1))) >> 6
    jb = lax.broadcasted_iota(I32, (nselp, 1), 0)
    forced = (jb == 0) | (jb == cur) | (jb == cur - 1)
    imp = jnp.where(jb > cur, -1.0, jnp.where(forced, 1e6, imp))
    notsel = jnp.where(_topk_mask_cols(imp, SEL_TOPN), 0.0, 1.0)
    for g in range(NSA_KV):
        notsel_ref[g] = notsel[:, g * qb:(g + 1) * qb].T.astype(BF16)
    o_ref[...] = _unstack_heads_t(outs, R_NSA, qb)


def _cmp_attention(q_raw, kc, vc, overlap_t, qb):
    vc = vc.transpose(0, 1, 3, 2)
    b, s, hq = q_raw.shape
    _, g, ncp, dh = kc.shape
    nselp = overlap_t.shape[0]
    return pl.pallas_call(
        _cmp_kernel,
        grid=(b, s // qb),
        in_specs=[pl.BlockSpec((None, qb, hq), lambda bi, i: (bi, i, 0)),
                  pl.BlockSpec((None, g, ncp, dh), lambda bi, i: (bi, 0, 0, 0)),
                  pl.BlockSpec((None, g, dh, ncp), lambda bi, i: (bi, 0, 0, 0)),
                  _full(overlap_t.shape)],
        out_specs=[pl.BlockSpec((None, qb, hq), lambda bi, i: (bi, i, 0)),
                   pl.BlockSpec((None, g, qb, nselp), lambda bi, i: (bi, 0, i, 0))],
        out_shape=[jax.ShapeDtypeStruct((b, s, hq), F32),
                   jax.ShapeDtypeStruct((b, g, s, nselp), BF16)],
        compiler_params=_params("parallel", "parallel"),
        name="cmp_attn",
    )(q_raw, kc, vc, overlap_t)


def _sel_kernel(q_ref, notsel_ref, k_ref, vt_ref, o_ref, *score_bufs, tk):
    s_even, s_odd = score_bufs[:NSA_KV], score_bufs[NSA_KV:]
    qb = q_ref.shape[0]
    i = pl.program_id(1)
    rows = R_NSA * qb
    qpos = i * qb + (lax.broadcasted_iota(I32, (1, rows), 1) & (qb - 1))
    n_clear = (i * qb) // tk
    q = q_ref[...]
    q_augs = []
    for g in range(NSA_KV):
        qg = _stack_heads(q, g, R_NSA) * (HEAD_DIM ** -0.5)
        q_augs.append(jnp.concatenate(
            [qg, jnp.zeros((rows, LANES - HEAD_DIM), BF16),
             jnp.concatenate([notsel_ref[g]] * R_NSA, axis=0)], axis=1))

    def scores(kt, g):
        start = pl.multiple_of(kt * tk, tk)
        return lax.dot_general(k_ref[g, pl.ds(start, tk), :], q_augs[g], NT_DIMS,
                               preferred_element_type=F32)

    def consume(kt, g, s_ref, m, acc, causal):
        start = pl.multiple_of(kt * tk, tk)
        vt_t = vt_ref[g, :, pl.ds(start, tk)]
        ms, accs = [], []
        for c in range(rows // SEL_CHUNK):
            cols = slice(c * SEL_CHUNK, (c + 1) * SEL_CHUNK)
            st = s_ref[:, cols]
            if causal:
                kpos = start + lax.broadcasted_iota(I32, (tk, 1), 0)
                st = jnp.where(kpos <= qpos[:, cols], st, MASKED)
            m_new = jnp.maximum(m[:, cols], jnp.max(st, axis=0, keepdims=True))
            pt = jnp.exp(st - m_new).astype(BF16)
            accs.append(jnp.exp(m[:, cols] - m_new) * acc[:, cols]
                        + jnp.dot(vt_t, pt, preferred_element_type=F32))
            ms.append(m_new)
        return jnp.concatenate(ms, axis=1), jnp.concatenate(accs, axis=1)

    def advance(kt, carry, cur, nxt):
        new = []
        for g in range(NSA_KV):
            nxt[g][...] = scores(kt + 1, g)
            new.append(consume(kt, g, cur[g], *carry[g], False))
        return tuple(new)

    def pair(j, carry):
        carry = advance(2 * j, carry, s_even, s_odd)
        return advance(2 * j + 1, carry, s_odd, s_even)

    def finish(carry, cur):
        outs = []
        for g in range(NSA_KV):
            _, acc = consume(n_clear, g, cur[g], *carry[g], True)
            outs.append(acc[:HEAD_DIM] / acc[HEAD_DIM:HEAD_DIM + 1])
        o_ref[...] = _unstack_heads_t(outs, R_NSA, qb)

    for g in range(NSA_KV):
        s_even[g][...] = scores(0, g)
    init = tuple((jnp.full((1, rows), M_INIT, F32), jnp.zeros((LANES, rows), F32))
                 for _ in range(NSA_KV))
    carry = lax.fori_loop(0, n_clear // 2, pair, init)

    @pl.when((n_clear & 1) == 0)
    def _():
        finish(carry, s_even)

    @pl.when((n_clear & 1) == 1)
    def _():
        finish(advance(n_clear - 1, carry, s_even, s_odd), s_odd)


def _sel_attention(q_rot, notsel, k_aug, vt_aug, qb, tk):
    b, s, hq = q_rot.shape
    _, g, _, kw = k_aug.shape
    nselp = notsel.shape[3]
    return pl.pallas_call(
        functools.partial(_sel_kernel, tk=tk),
        grid=(b, s // qb),
        in_specs=[pl.BlockSpec((None, qb, hq), lambda bi, i: (bi, i, 0)),
                  pl.BlockSpec((None, g, qb, nselp), lambda bi, i: (bi, 0, i, 0)),
                  pl.BlockSpec((None, g, s, kw), lambda bi, i: (bi, 0, 0, 0)),
                  pl.BlockSpec((None, g, LANES, s), lambda bi, i: (bi, 0, 0, 0))],
        out_specs=pl.BlockSpec((None, qb, hq), lambda bi, i: (bi, i, 0)),
        out_shape=jax.ShapeDtypeStruct((b, s, hq), F32),
        scratch_shapes=[pltpu.VMEM((tk, R_NSA * qb), F32)] * (2 * g),
        compiler_params=_params("parallel", "parallel"),
        name="sel_attn",
    )(q_rot, notsel, k_aug, vt_aug)


def _band_kernel(*refs, window, wlen, r, kv, qb, has_sinks):
    if has_sinks:
        sink_ref, q_ref, k_ref, vt_ref, o_ref = refs
    else:
        q_ref, k_ref, vt_ref, o_ref = refs
    nsub = q_ref.shape[0] // qb
    rows = r * qb
    nv = kv * HEAD_DIM
    lane = lax.broadcasted_iota(I32, (1, rows), 1)

    def mask_bias(i):
        start = jnp.maximum((i + 1) * qb - wlen, 0)
        rel = i * qb + (lane & (qb - 1)) - start - lax.broadcasted_iota(I32, (wlen, 1), 0)
        return jnp.where((rel >= 0) & (rel < window), 0.0, MASKED)

    def body(shared_bias):
        first = pl.program_id(1) * nsub
        if shared_bias:
            bias = mask_bias(first)
        for sb in range(nsub):
            i = first + sb
            if not shared_bias:
                bias = mask_bias(i)
            start = pl.multiple_of(jnp.maximum((i + 1) * qb - wlen, 0), qb)
            q = q_ref[sb * qb:(sb + 1) * qb, :]
            kw = k_ref[pl.ds(start, wlen), :]
            vtw = vt_ref[:, pl.ds(start, wlen)]
            outs = []
            for g in range(kv):
                qg = _stack_heads(q, g, r) * (HEAD_DIM ** -0.5)
                st = lax.dot_general(kw[:, g * HEAD_DIM:(g + 1) * HEAD_DIM], qg, NT_DIMS,
                                     preferred_element_type=F32) + bias
                m = jnp.max(st, axis=0, keepdims=True)
                if has_sinks:
                    sk = jnp.full((1, rows), sink_ref[g * r], F32)
                    for j in range(1, r):
                        sk = jnp.where(lane >= j * qb, sink_ref[g * r + j], sk)
                    m = jnp.maximum(m, sk)
                e = jnp.exp(st - m).astype(BF16)
                v_ones = jnp.concatenate([vtw[g * HEAD_DIM:(g + 1) * HEAD_DIM, :], vtw[nv:, :]],
                                         axis=0)
                ot = jnp.dot(v_ones, e, preferred_element_type=F32)
                den = ot[HEAD_DIM:HEAD_DIM + 1]
                if has_sinks:
                    den = den + jnp.exp(sk - m)
                outs.append(ot[:HEAD_DIM] / den)
            o_ref[sb * qb:(sb + 1) * qb, :] = _unstack_heads_t(outs, r, qb)

    if nsub * qb >= wlen - qb:
        @pl.when(pl.program_id(1) == 0)
        def _():
            body(False)

        @pl.when(pl.program_id(1) > 0)
        def _():
            body(True)
    else:
        body(False)


def _band_attention(q_rot, k, v, window, sinks, qb):
    b, s, hq = q_rot.shape
    gk = k.shape[2]
    v = jnp.concatenate([v.transpose(0, 2, 1), jnp.ones((b, BF16_SUBLANES, s), v.dtype)], axis=1)
    kv = gk // HEAD_DIM
    r = hq // gk
    back = -(-window // qb)
    wlen = (back + 1) * qb
    assert wlen <= s
    has_sinks = sinks is not None
    tq = min(BAND_STEP_QUERIES, s)
    in_specs = [pl.BlockSpec((None, tq, hq), lambda bi, i: (bi, i, 0)),
                pl.BlockSpec((None, s, gk), lambda bi, i: (bi, 0, 0)),
                pl.BlockSpec((None, gk + BF16_SUBLANES, s), lambda bi, i: (bi, 0, 0))]
    args = [q_rot, k, v]
    if has_sinks:
        in_specs = [pl.BlockSpec(memory_space=pltpu.SMEM)] + in_specs
        args = [sinks.astype(F32)] + args
    return pl.pallas_call(
        functools.partial(_band_kernel, window=window, wlen=wlen, r=r, kv=kv, qb=qb,
                          has_sinks=has_sinks),
        grid=(b, s // tq),
        in_specs=in_specs,
        out_specs=pl.BlockSpec((None, tq, hq), lambda bi, i: (bi, i, 0)),
        out_shape=jax.ShapeDtypeStruct((b, s, hq), F32),
        compiler_params=_params("parallel", "parallel"),
        name="band_attn_sink" if has_sinks else "band_attn",
    )(*args)


def _layer_norm(v, g, b):
    mu = jnp.mean(v, axis=1, keepdims=True)
    c = v - mu
    var = jnp.mean(c * c, axis=1, keepdims=True)
    return c * lax.rsqrt(var + LN_EPS) * g + b


def _split_dot(a, w):
    hi = a.astype(BF16)
    lo = (a - hi.astype(F32)).astype(BF16)
    return (jnp.dot(hi, w, preferred_element_type=F32)
            + jnp.dot(lo, w, preferred_element_type=F32))


def _merge_kernel(ocmp_ref, osel_ref, owin_ref, oswa_ref, gates_ref, x_ref, exp_ref,
                  wbn_ref, wbs_ref, wo_ref, lng_ref, lnb_ref, wrh_ref, wrl_ref, br_ref,
                  h_ref, te_ref, tg_ref, tr_ref, counts_ref, base_ref, cnt_ref, *, alpha):
    d = x_ref.shape[1]
    hq = ocmp_ref.shape[1]
    gates = gates_ref[...]
    gn = jax.nn.sigmoid(gates[:, 2 * d:])
    gexp = _split_dot(gn, exp_ref[...])
    o_nsa = (gexp[:, 0:hq] * ocmp_ref[...] + gexp[:, hq:2 * hq] * osel_ref[...]
             + gexp[:, 2 * hq:3 * hq] * owin_ref[...])
    y_nsa = jnp.dot(o_nsa.astype(BF16), wbn_ref[...], preferred_element_type=F32)
    y_swa = jnp.dot(oswa_ref[...].astype(BF16), wbs_ref[...], preferred_element_type=F32)
    gm = jax.nn.sigmoid(gates[:, :2 * d])
    mixed = gm[:, :d] * y_nsa + gm[:, d:] * y_swa
    z = jnp.dot(mixed.astype(BF16), wo_ref[...], preferred_element_type=F32)
    h = _layer_norm(alpha * x_ref[...] + z, lng_ref[...], lnb_ref[...])
    h_ref[...] = h
    h_hi = h.astype(BF16)
    h_lo = (h - h_hi.astype(F32)).astype(BF16)
    logits = (jnp.dot(h_hi, wrh_ref[...], preferred_element_type=F32)
              + jnp.dot(h_lo, wrh_ref[...], preferred_element_type=F32)
              + jnp.dot(h_hi, wrl_ref[...], preferred_element_type=F32)) + br_ref[...]
    col = lax.broadcasted_iota(I32, logits.shape, 1).astype(F32)
    work = logits
    vals, ids = [], []
    for _ in range(TOP_K):
        mx = jnp.max(work, axis=1, keepdims=True)
        first = jnp.min(jnp.where(work == mx, col, float(LANES)), axis=1, keepdims=True)
        vals.append(mx)
        ids.append(first)
        work = jnp.where(col == first, -jnp.inf, work)
    es = [jnp.exp(v - vals[0]) for v in vals]
    den = es[0]
    for e in es[1:]:
        den = den + e
    @pl.when(pl.program_id(0) == 0)
    def _():
        cnt_ref[...] = jnp.zeros(cnt_ref.shape, F32)

    tm = logits.shape[0]
    hits = jnp.zeros(logits.shape, F32)
    for k in range(TOP_K):
        hits = jnp.where(col == ids[k], 1.0, hits)
    earlier = (lax.broadcasted_iota(I32, (tm, tm), 1)
               < lax.broadcasted_iota(I32, (tm, tm), 0))
    before = jnp.dot(jnp.where(earlier, 1.0, 0.0).astype(BF16), hits.astype(BF16),
                     preferred_element_type=F32) + cnt_ref[...]
    base_ref[...] = cnt_ref[...]
    cnt_ref[...] = cnt_ref[...] + jnp.sum(hits, axis=0, keepdims=True)
    counts_ref[...] = cnt_ref[...]
    te = jnp.zeros(logits.shape, F32)
    tg = jnp.zeros(logits.shape, F32)
    tr = jnp.zeros(logits.shape, F32)
    for k in range(TOP_K):
        rank = jnp.sum(jnp.where(col == ids[k], before, 0.0), axis=1, keepdims=True)
        te = jnp.where(col == float(k), ids[k], te)
        tg = jnp.where(col == float(k), es[k] / den, tg)
        tr = jnp.where(col == float(k), rank, tr)
    te_ref[...] = te.astype(I32)
    tg_ref[...] = tg
    tr_ref[...] = tr.astype(I32)


def _merge(o_cmp, o_sel, o_win, o_swa, gates, x2, expand, w_bn, w_bs, w_o, ln_g, ln_b,
           wr_hi, wr_lo, b_r, alpha, tm):
    t, d = x2.shape
    hq = o_cmp.shape[1]
    row = lambda i: (i, 0)
    tok = lambda n: pl.BlockSpec((tm, n), row)
    return pl.pallas_call(
        functools.partial(_merge_kernel, alpha=alpha),
        grid=(t // tm,),
        in_specs=[tok(hq), tok(hq), tok(hq), tok(hq), tok(gates.shape[1]), tok(d),
                  _full(expand.shape), _full(w_bn.shape), _full(w_bs.shape), _full(w_o.shape),
                  _full(ln_g.shape), _full(ln_b.shape), _full(wr_hi.shape), _full(wr_lo.shape),
                  _full(b_r.shape)],
        out_specs=[tok(d), tok(LANES), tok(LANES), tok(LANES), _full((1, LANES)),
                   pl.BlockSpec((None, 1, LANES), lambda i: (i, 0, 0))],
        out_shape=[jax.ShapeDtypeStruct((t, d), F32), jax.ShapeDtypeStruct((t, LANES), I32),
                   jax.ShapeDtypeStruct((t, LANES), F32), jax.ShapeDtypeStruct((t, LANES), I32),
                   jax.ShapeDtypeStruct((1, LANES), F32),
                   jax.ShapeDtypeStruct((t // tm, 1, LANES), F32)],
        scratch_shapes=[pltpu.VMEM((1, LANES), F32)],
        compiler_params=_params("arbitrary"),
        name="merge_ln_router",
    )(o_cmp, o_sel, o_win, o_swa, gates, x2, expand, w_bn, w_bs, w_o, ln_g, ln_b,
      wr_hi, wr_lo, b_r)


def _dispatch_kernel(pend_ref, padded_ref, cseg_ref, first_ref, len_ref, tot_ref,
                     h_ref, cpos_ref, xs_ref, comp0, comp1, zbuf, sem, zsem):
    j = pl.program_id(0)
    nt = pl.num_programs(0)
    tm = h_ref.shape[0]
    n_rows = xs_ref.shape[0]
    comps = (comp0, comp1)
    n_comp = comp0.shape[0]

    @pl.when(j == 0)
    def _():
        zbuf[...] = jnp.zeros(zbuf.shape, F32)
        used_rows = pend_ref[N_EXPERTS - 1]

        def zero_copy(start):
            start = pl.multiple_of(start, MOE_ROWS)
            return pltpu.make_async_copy(zbuf, xs_ref.at[pl.ds(start, MOE_ROWS), :], zsem)

        blocks = [(padded_ref[e] > 0, pend_ref[e] - MOE_ROWS) for e in range(N_EXPERTS)]
        blocks += [(used_rows + b * MOE_ROWS < n_rows, used_rows + b * MOE_ROWS)
                   for b in range(n_rows // MOE_ROWS - (nt * tm * TOP_K) // MOE_ROWS)]
        for cond, start in blocks:
            @pl.when(cond)
            def _(start=start):
                zero_copy(start).start()
        for cond, start in blocks:
            @pl.when(cond)
            def _(start=start):
                zero_copy(start).wait()

    def wait_writes(tile, slot):
        rows = pl.multiple_of(tot_ref[tile], F32_SUBLANES)
        pltpu.make_async_copy(comps[slot].at[pl.ds(0, rows), :], xs_ref.at[pl.ds(0, rows), :],
                              sem.at[slot]).wait()

    for slot in range(2):
        @pl.when((j >= 2) & ((j & 1) == slot))
        def _(slot=slot):
            wait_writes(j - 2, slot)

        @pl.when((j & 1) == slot)
        def _(slot=slot):
            row = lax.broadcasted_iota(I32, (n_comp, 1), 0)
            cpos = cpos_ref[...]
            sel = jnp.zeros((n_comp, tm), F32)
            for k in range(TOP_K):
                sel = jnp.where(row == cpos[k:k + 1, :], 1.0, sel)
            comps[slot][...] = jnp.dot(sel.astype(BF16), h_ref[...].astype(BF16),
                                       preferred_element_type=F32)
            for e in range(N_EXPERTS):
                ln = len_ref[j * N_EXPERTS + e]
                first = first_ref[j * N_EXPERTS + e]
                seg = cseg_ref[j * N_EXPERTS + e]
                for p in _run_sizes(tm):
                    done = ln & (-2 * p)
                    src = comps[slot].at[pl.ds(pl.multiple_of(seg + done, F32_SUBLANES), p), :]
                    dst = xs_ref.at[pl.ds(pl.multiple_of(first + done, F32_SUBLANES), p), :]
                    pl.when((ln & p) != 0)(pltpu.make_async_copy(src, dst, sem.at[slot]).start)

        @pl.when((j == nt - 1) & ((j & 1) == slot))
        def _(slot=slot):
            @pl.when(j >= 1)
            def _():
                wait_writes(j - 1, 1 - slot)
            wait_writes(j, slot)


def _dispatch(pend, padded, cseg, seg_first, seg_len, seg_tot, cpos_t, h, n_rows, tm):
    t, d = h.shape
    n_comp = -(-(tm * TOP_K + N_EXPERTS * (F32_SUBLANES - 1)) // MXU_DEPTH) * MXU_DEPTH
    grid_spec = pltpu.PrefetchScalarGridSpec(
        num_scalar_prefetch=6,
        grid=(t // tm,),
        in_specs=[pl.BlockSpec((tm, d), lambda i, *_: (i, 0)),
                  pl.BlockSpec((TOP_K, tm), lambda i, *_: (0, i))],
        out_specs=pl.BlockSpec(memory_space=pl.ANY),
        scratch_shapes=[pltpu.VMEM((n_comp, d), F32), pltpu.VMEM((n_comp, d), F32),
                        pltpu.VMEM((MOE_ROWS, d), F32), pltpu.SemaphoreType.DMA((2,)),
                        pltpu.SemaphoreType.DMA(())],
    )
    return pl.pallas_call(
        _dispatch_kernel,
        grid_spec=grid_spec,
        out_shape=jax.ShapeDtypeStruct((n_rows, d), F32),
        compiler_params=_params("arbitrary"),
        name="moe_dispatch",
    )(pend, padded, cseg, seg_first, seg_len, seg_tot, h, cpos_t)


def _moe_kernel(blk_e_ref, n_used_ref, x_ref, wi_ref, bi_ref, wo_ref, bo_ref, y_ref,
                wi_bf, wo_bf):
    f = wo_ref.shape[0]
    b = pl.program_id(0)

    @pl.when((b == 0) | (blk_e_ref[b] != blk_e_ref[jnp.maximum(b - 1, 0)]))
    def _():
        wi_bf[...] = wi_ref[...].astype(BF16)
        wo_bf[...] = wo_ref[...].astype(BF16)

    @pl.when(b < n_used_ref[0])
    def _():
        hdn = jnp.dot(x_ref[...].astype(BF16), wi_bf[...],
                      preferred_element_type=F32) + bi_ref[...]
        hg = jnp.minimum(hdn[:, :f], SWIGLU_LIMIT)
        hu = jnp.clip(hdn[:, f:], -SWIGLU_LIMIT, SWIGLU_LIMIT)
        act = hg * jax.nn.sigmoid(SWIGLU_ALPHA * hg) * (hu + 1.0)
        y_ref[...] = jnp.dot(act.astype(BF16), wo_bf[...],
                             preferred_element_type=F32) + bo_ref[...]

    @pl.when(b >= n_used_ref[0])
    def _():
        y_ref[...] = jnp.zeros(y_ref.shape, F32)


def _moe_blocks(blk_e, n_used, xs, w_in, b_in, w_out, b_out):
    n_rows, d = xs.shape
    e, _, f2 = w_in.shape
    f = w_out.shape[1]
    n_blk = n_rows // MOE_ROWS
    grid_spec = pltpu.PrefetchScalarGridSpec(
        num_scalar_prefetch=2,
        grid=(n_blk,),
        in_specs=[pl.BlockSpec((MOE_ROWS, d), lambda b, be, nu: (jnp.minimum(b, nu[0] - 1), 0)),
                  pl.BlockSpec((None, d, f2), lambda b, be, nu: (be[b], 0, 0)),
                  pl.BlockSpec((None, 1, f2), lambda b, be, nu: (be[b], 0, 0)),
                  pl.BlockSpec((None, f, d), lambda b, be, nu: (be[b], 0, 0)),
                  pl.BlockSpec((None, 1, d), lambda b, be, nu: (be[b], 0, 0))],
        out_specs=pl.BlockSpec((MOE_ROWS, d), lambda b, be, nu: (b, 0)),
        scratch_shapes=[pltpu.VMEM((d, f2), BF16), pltpu.VMEM((f, d), BF16)],
    )
    return pl.pallas_call(
        _moe_kernel,
        grid_spec=grid_spec,
        out_shape=jax.ShapeDtypeStruct((n_rows, d), F32),
        compiler_params=_params("arbitrary"),
        name="moe_experts",
    )(blk_e, n_used, xs, w_in, b_in, w_out, b_out)


def _run_sizes(tm):
    sizes, p = [], F32_SUBLANES
    while p <= tm:
        sizes.append(p)
        p *= 2
    return tuple(reversed(sizes))


def _stage_rows(tm):
    rows = tm * TOP_K + N_EXPERTS * 2 * (F32_SUBLANES - 1)
    return -(-rows // MXU_DEPTH) * MXU_DEPTH


def _final_kernel(start_ref, len_ref, seg_ref, tot_ref, h_ref, tg_ref, sp_ref, ys_ref, g_ref,
                  b_ref, o_ref, stage0, stage1, sem, *, alpha):
    i = pl.program_id(0)
    tm = h_ref.shape[0]
    stages = (stage0, stage1)
    n_stage = stage0.shape[0]

    def fetch(tile, into):
        for e in range(N_EXPERTS):
            ln = len_ref[tile * N_EXPERTS + e]
            first = start_ref[tile * N_EXPERTS + e]
            seg = seg_ref[tile * N_EXPERTS + e]
            for p in _run_sizes(tm):
                done = ln & (-2 * p)
                src = ys_ref.at[pl.ds(pl.multiple_of(first + done, F32_SUBLANES), p), :]
                dst = stages[into].at[pl.ds(pl.multiple_of(seg + done, F32_SUBLANES), p), :]
                pl.when((ln & p) != 0)(pltpu.make_async_copy(src, dst, sem.at[into]).start)

    @pl.when(i == 0)
    def _():
        stage0[...] = jnp.zeros(stage0.shape, F32)
        stage1[...] = jnp.zeros(stage1.shape, F32)
        fetch(0, 0)

    for slot in range(2):
        @pl.when((i & 1) == slot)
        def _(slot=slot):
            fetch(i + 1, 1 - slot)
            rows = pl.multiple_of(tot_ref[i], F32_SUBLANES)
            pltpu.make_async_copy(ys_ref.at[pl.ds(0, rows), :],
                                  stages[slot].at[pl.ds(0, rows), :], sem.at[slot]).wait()
            col = lax.broadcasted_iota(I32, (1, n_stage), 1)
            tg = tg_ref[...]
            sp = sp_ref[...]
            q = jnp.zeros((tm, n_stage), F32)
            for k in range(TOP_K):
                q = jnp.where(col == sp[:, k:k + 1], tg[:, k:k + 1], q)
            f = _split_dot(q, stages[slot][...].astype(BF16))
            o_ref[...] = _layer_norm(alpha * h_ref[...] + f, g_ref[...], b_ref[...])


def _final(run_start, run_len, run_seg, run_tot, h, top_g, spos, ys, ln_g, ln_b, alpha, tm):
    t, d = h.shape
    grid_spec = pltpu.PrefetchScalarGridSpec(
        num_scalar_prefetch=4,
        grid=(t // tm,),
        in_specs=[pl.BlockSpec((tm, d), lambda i, *_: (i, 0)),
                  pl.BlockSpec((tm, LANES), lambda i, *_: (i, 0)),
                  pl.BlockSpec((tm, TOP_K), lambda i, *_: (i, 0)),
                  pl.BlockSpec(memory_space=pl.ANY),
                  pl.BlockSpec(ln_g.shape, lambda i, *_: (0, 0)),
                  pl.BlockSpec(ln_b.shape, lambda i, *_: (0, 0))],
        out_specs=pl.BlockSpec((tm, d), lambda i, *_: (i, 0)),
        scratch_shapes=[pltpu.VMEM((_stage_rows(tm), d), F32),
                        pltpu.VMEM((_stage_rows(tm), d), F32), pltpu.SemaphoreType.DMA((2,))],
    )
    return pl.pallas_call(
        functools.partial(_final_kernel, alpha=alpha),
        grid_spec=grid_spec,
        out_shape=jax.ShapeDtypeStruct((t, d), F32),
        compiler_params=_params("arbitrary"),
        name="combine_ln",
    )(run_start, run_len, run_seg, run_tot, h, top_g, spos, ys, ln_g, ln_b)


def _rope_tables(s):
    half = HEAD_DIM // 2
    inv = ROPE_THETA ** (-jnp.arange(half, dtype=F32) / half)
    ang = jnp.arange(s, dtype=F32)[:, None] * inv[None, :]
    cos, sin = jnp.cos(ang), jnp.sin(ang)
    reps = LANES // HEAD_DIM
    cos_t = jnp.tile(jnp.concatenate([cos, cos], axis=1), (1, reps))
    sin_t = jnp.tile(jnp.concatenate([-sin, sin], axis=1), (1, reps))
    return cos_t, sin_t


def _ceil_to(v, m):
    return (v + m - 1) // m * m


def _moe_plan(top_e, rank, tile_base, counts, t, tm, dt):
    a = t * TOP_K
    per, nd = dt // tm, t // dt
    d_base = tile_base[::per]
    d_cnt = jnp.concatenate([d_base[1:], counts[None, :]], axis=0) - d_base
    seg_len = _ceil_to(d_cnt, F32_SUBLANES)
    padded = _ceil_to(jnp.sum(seg_len, axis=0), MOE_ROWS)
    pend = jnp.cumsum(padded)
    seg_first = (pend - padded)[None, :] + jnp.cumsum(seg_len, axis=0) - seg_len
    cseg = jnp.cumsum(seg_len, axis=1) - seg_len
    onehot = top_e[:, :, None] == jnp.arange(N_EXPERTS, dtype=I32)

    def pick(table):
        return jnp.sum(jnp.where(onehot, jnp.repeat(table, dt, axis=0)[:, None, :], 0), axis=2)

    local = rank - pick(d_base)
    dest = pick(seg_first) + local
    cpos = pick(cseg) + local
    run_first = jnp.repeat(seg_first - d_base, per, axis=0) + tile_base
    n_blk = -(-(a + nd * N_EXPERTS * (F32_SUBLANES - 1)) // MOE_ROWS) + N_EXPERTS
    blk_first = jnp.arange(n_blk, dtype=I32) * MOE_ROWS
    blk_e = jnp.minimum(jnp.sum(pend[None, :] <= blk_first[:, None], axis=1),
                        N_EXPERTS - 1).astype(I32)
    n_used = (pend[-1] // MOE_ROWS).astype(I32).reshape(1)
    flat = lambda v: v.astype(I32).reshape(-1)
    segments = (flat(cseg), flat(seg_first), flat(seg_len), jnp.sum(seg_len, axis=1).astype(I32))
    return (segments, dest.astype(I32), cpos.astype(I32), run_first, blk_e, n_used,
            n_blk * MOE_ROWS, pend.astype(I32), padded.astype(I32))


def _combine_plan(top_e, dest, run_first, tile_base, counts, tm):
    cnt = jnp.concatenate([tile_base[1:], counts[None, :]], axis=0) - tile_base
    lead = run_first & (F32_SUBLANES - 1)
    run_len = jnp.where(cnt > 0, _ceil_to(lead + cnt, F32_SUBLANES), 0)
    run_seg = jnp.cumsum(run_len, axis=1) - run_len
    shift = jnp.repeat(run_seg + lead - run_first, tm, axis=0)
    onehot = top_e[:, :, None] == jnp.arange(N_EXPERTS, dtype=I32)
    spos = jnp.sum(jnp.where(onehot, shift[:, None, :], 0), axis=2) + dest
    flat = lambda v: jnp.pad(v.astype(I32), ((0, 1), (0, 0))).reshape(-1)
    run_tot = jnp.sum(run_len, axis=1).astype(I32)
    return flat(run_first - lead), flat(run_len), flat(run_seg), run_tot, spos.astype(I32)


def _layer(x, w_in, k_pe, k_w1, k_w2, v_pe, v_w1, v_w2, sinks, w_br_nsa, w_br_swa, w_out,
           ln1_g, ln1_b, w_router, b_router, w_e_in, b_e_in, w_e_out, b_e_out, ln2_g, ln2_b,
           alpha):
    b, s, d = x.shape
    t = b * s
    qb = Q_BLOCK
    nq_n, nkv = NSA_HEADS * HEAD_DIM, NSA_KV * HEAD_DIM
    nq_s, nkv_s = SWA_HEADS * HEAD_DIM, SWA_KV * HEAD_DIM
    widths = (nq_n, nkv, nkv, nkv, nkv, nkv, nkv, NSA_HEADS * 3, nq_s, nkv_s, nkv_s, 2 * d)
    offs = [0]
    for w in widths:
        offs.append(offs[-1] + w)
    col = lambda j: w_in[:, offs[j]:offs[j + 1]]
    (c_qn, c_kc, c_vc, c_ks, c_vs, c_kw, c_vw, c_gn, c_qs, c_k_s, c_v_s, c_gm) = map(col, range(12))
    w_rope = jnp.concatenate([c_qn, c_qs, c_ks, c_kw, c_k_s], axis=1).astype(BF16)
    w_plain = jnp.concatenate([c_kc, c_vc, c_vs, c_vw, c_v_s], axis=1).astype(BF16)
    gn_pad = LANES - NSA_HEADS * 3
    w_gate = jnp.concatenate([c_gm, c_gn, jnp.zeros((d, gn_pad), F32)], axis=1).astype(BF16)
    cos_t, sin_t = _rope_tables(s)

    x2 = x.reshape(t, d)
    qn_rot, qs_rot, kk_rot, qn_raw, plain, gates = _project(
        x2, w_rope, w_plain, w_gate, cos_t, sin_t, s, min(256, s))

    nc = (s - CMP_BLOCK) // CMP_STRIDE + 1
    ncp = s // CMP_STRIDE
    half = CMP_STRIDE * HEAD_DIM

    def halves(cols):
        v = cols.reshape(b, s, NSA_KV, HEAD_DIM).transpose(0, 2, 1, 3)
        return v.reshape(b * NSA_KV, ncp, half)

    t2 = jnp.stack([halves(plain[:, 0:nkv]), halves(plain[:, nkv:2 * nkv])])
    t_lo = t2
    t_hi = jnp.concatenate([t2[:, :, 1:], jnp.zeros_like(t2[:, :, :1])], axis=2)
    pe2 = jnp.stack([k_pe.reshape(2, half), v_pe.reshape(2, half)])
    w1 = jnp.stack([k_w1, v_w1]).astype(BF16)
    w2 = jnp.stack([k_w2, v_w2]).astype(BF16)
    kvc = _compress(t_lo, t_hi, pe2, w1, w2).reshape(2, b, NSA_KV, ncp, HEAD_DIM)

    nsel = s // SEL_BLOCK
    nselp = -(-nsel // LANES) * LANES
    cstart = jnp.arange(ncp) * CMP_STRIDE
    sstart = jnp.arange(nselp) * SEL_BLOCK
    overlap = ((cstart[:, None] < sstart[None, :] + SEL_BLOCK)
               & (cstart[:, None] + CMP_BLOCK > sstart[None, :])
               & (jnp.arange(ncp)[:, None] < nc) & (jnp.arange(nselp)[None, :] < nsel))
    o_cmp, notsel = _cmp_attention(qn_raw.reshape(b, s, nq_n), kvc[0], kvc[1],
                                   overlap.T.astype(BF16), min(CMP_STEP_QUERIES, s))

    def group_major(cols):
        return cols.reshape(b, s, NSA_KV, HEAD_DIM).transpose(0, 2, 1, 3)

    k_sel = group_major(kk_rot[:, 0:nkv])
    v_sel = group_major(plain[:, 2 * nkv:3 * nkv])
    onehot = (jnp.arange(s)[:, None] // SEL_BLOCK == jnp.arange(nselp)[None, :])
    k_tail = jnp.concatenate([jnp.zeros((s, LANES - HEAD_DIM), F32),
                              jnp.where(onehot, SEL_PENALTY, 0.0)], axis=1).astype(BF16)
    k_aug = jnp.concatenate(
        [k_sel, jnp.broadcast_to(k_tail, (b, NSA_KV) + k_tail.shape)], axis=3)
    v_tail = jnp.zeros((LANES - HEAD_DIM, 1), BF16).at[0].set(1.0)
    vt_aug = jnp.concatenate(
        [v_sel.transpose(0, 1, 3, 2),
         jnp.broadcast_to(v_tail, (b, NSA_KV, LANES - HEAD_DIM, s))], axis=2)
    o_sel = _sel_attention(qn_rot.reshape(b, s, nq_n), notsel, k_aug, vt_aug,
                           min(SEL_STEP_QUERIES, s), min(SEL_KEY_TILE, s))

    o_win = _band_attention(qn_rot.reshape(b, s, nq_n), kk_rot[:, nkv:2 * nkv].reshape(b, s, nkv),
                            plain[:, 3 * nkv:4 * nkv].reshape(b, s, nkv), NSA_WINDOW, None, qb)
    o_swa = _band_attention(qs_rot.reshape(b, s, nq_s),
                            kk_rot[:, 2 * nkv:2 * nkv + nkv_s].reshape(b, s, nkv_s),
                            plain[:, 4 * nkv:4 * nkv + nkv_s].reshape(b, s, nkv_s),
                            SWA_WINDOW, sinks, qb)

    gi = jnp.arange(LANES)
    ci = jnp.arange(3 * nq_n)
    expand = ((gi[:, None] // 3 == (ci[None, :] % nq_n) // HEAD_DIM)
              & (gi[:, None] % 3 == ci[None, :] // nq_n)
              & (gi[:, None] < NSA_HEADS * 3)).astype(BF16)
    wr_pad = jnp.pad(w_router, ((0, 0), (0, LANES - N_EXPERTS)))
    wr_hi = wr_pad.astype(BF16)
    wr_lo = (wr_pad - wr_hi.astype(F32)).astype(BF16)
    b_r = jnp.concatenate([b_router, jnp.full((LANES - N_EXPERTS,), -jnp.inf, F32)]).reshape(1, LANES)
    tm = min(256, t)
    h, top_e, top_g, rank, counts, tile_base = _merge(
        o_cmp.reshape(t, nq_n), o_sel.reshape(t, nq_n), o_win.reshape(t, nq_n),
        o_swa.reshape(t, nq_s), gates, x2, expand, w_br_nsa.astype(BF16), w_br_swa.astype(BF16),
        w_out.astype(BF16), ln1_g.reshape(1, d), ln1_b.reshape(1, d), wr_hi, wr_lo, b_r,
        alpha, tm)

    top_e, rank = top_e[:, :TOP_K], rank[:, :TOP_K]
    counts = counts[0, :N_EXPERTS].astype(I32)
    tile_base = tile_base[:, 0, :N_EXPERTS].astype(I32)
    dt = min(DISPATCH_TOKENS, t)
    segments, dest, cpos, run_first, blk_e, n_used, n_rows, pend, padded = _moe_plan(
        top_e, rank, tile_base, counts, t, tm, dt)
    xs = _dispatch(pend, padded, *segments, cpos.T, h, n_rows, dt)
    ys = _moe_blocks(blk_e, n_used, xs, w_e_in, b_e_in.reshape(N_EXPERTS, 1, -1),
                     w_e_out, b_e_out.reshape(N_EXPERTS, 1, -1))
    run_start, run_len, run_seg, run_tot, spos = _combine_plan(
        top_e, dest, run_first, tile_base, counts, tm)
    out = _final(run_start, run_len, run_seg, run_tot, h, top_g, spos, ys, ln2_g.reshape(1, d),
                 ln2_b.reshape(1, d), alpha, tm)
    return out.reshape(b, s, d)


def kernel(x, w_in, nsa_k_pe, nsa_k_w1, nsa_k_w2, nsa_v_pe, nsa_v_w1, nsa_v_w2, swa_sinks, w_br_nsa, w_br_swa, w_out, ln1_g, ln1_b, w_router, b_router, w_expert_in, b_expert_in, w_expert_out, b_expert_out, ln2_g, ln2_b):
    depth = w_in.shape[0]
    alpha = (2.0 * depth) ** 0.25
    for l in range(depth):
        x = _layer(x, w_in[l], nsa_k_pe[l], nsa_k_w1[l], nsa_k_w2[l], nsa_v_pe[l], nsa_v_w1[l],
                   nsa_v_w2[l], swa_sinks[l], w_br_nsa[l], w_br_swa[l], w_out[l], ln1_g[l],
                   ln1_b[l], w_router[l], b_router[l], w_expert_in[l], b_expert_in[l],
                   w_expert_out[l], b_expert_out[l], ln2_g[l], ln2_b[l], alpha)
    return x
```

```python
import functools

import jax
import jax.numpy as jnp
import numpy as np
from jax import lax
from jax.experimental import pallas as pl
from jax.experimental.pallas import tpu as pltpu

BF16 = jnp.bfloat16
F32 = jnp.float32
I32 = jnp.int32

HEAD_DIM = 64
NSA_HEADS = 8
NSA_KV = 2
CMP_BLOCK = 32
CMP_STRIDE = 16
SEL_BLOCK = 64
SEL_TOPN = 16
NSA_WINDOW = 512
SWA_HEADS = 8
SWA_KV = 2
SWA_WINDOW = 128
Q_BLOCK = 128
ROPE_THETA = 10000.0
N_EXPERTS = 32
TOP_K = 4
SWIGLU_LIMIT = 7.0
SWIGLU_ALPHA = 1.702
LN_EPS = 1e-5

LANES = 128
BF16_SUBLANES = 16
F32_SUBLANES = 8
MXU_DEPTH = 256
MASKED = -1e30
M_INIT = -1e29
SEL_PENALTY = -(2.0 ** 100)
VMEM_LIMIT = 52 * 1024 * 1024
MOE_ROWS = 256
DISPATCH_TOKENS = 512
BAND_STEP_QUERIES = 512
CMP_STEP_QUERIES = 512
SEL_STEP_QUERIES = 128
SEL_CHUNK = 512
SEL_KEY_TILE = 512

R_NSA = NSA_HEADS // NSA_KV
R_SWA = SWA_HEADS // SWA_KV
NT_DIMS = (((1,), (1,)), ((), ()))


def _params(*sem):
    return pltpu.CompilerParams(dimension_semantics=sem, vmem_limit_bytes=VMEM_LIMIT)


def _full(shape):
    n = len(shape)
    return pl.BlockSpec(shape, lambda *_: (0,) * n)


def _proj_kernel(x_ref, wr_ref, wp_ref, wg_ref, cos_ref, sin_ref,
                 qn_rot_ref, qs_rot_ref, kk_rot_ref, qn_raw_ref, plain_ref, gates_ref):
    xb = x_ref[...].astype(BF16)
    acc = jnp.dot(xb, wr_ref[...], preferred_element_type=F32)
    cos = cos_ref[...]
    sin = sin_ref[...]
    lane = lax.broadcasted_iota(I32, cos.shape, 1)
    first_half = (lane & (HEAD_DIM - 1)) < HEAD_DIM // 2

    def rope(t):
        partner = jnp.where(first_half, pltpu.roll(t, LANES - HEAD_DIM // 2, 1),
                            pltpu.roll(t, HEAD_DIM // 2, 1))
        return (t * cos + partner * sin).astype(BF16)

    nq = qn_rot_ref.shape[1] // LANES
    ns = qs_rot_ref.shape[1] // LANES
    nk = kk_rot_ref.shape[1] // LANES
    for c in range(nq):
        qn_rot_ref[:, c * LANES:(c + 1) * LANES] = rope(acc[:, c * LANES:(c + 1) * LANES])
    for c in range(ns):
        o = (nq + c) * LANES
        qs_rot_ref[:, c * LANES:(c + 1) * LANES] = rope(acc[:, o:o + LANES])
    for c in range(nk):
        o = (nq + ns + c) * LANES
        kk_rot_ref[:, c * LANES:(c + 1) * LANES] = rope(acc[:, o:o + LANES])
    qn_raw_ref[...] = acc[:, :nq * LANES].astype(BF16)
    plain_ref[...] = jnp.dot(xb, wp_ref[...], preferred_element_type=F32).astype(BF16)
    gates_ref[...] = jnp.dot(xb, wg_ref[...], preferred_element_type=F32)


def _project(x2, w_rope, w_plain, w_gate, cos_t, sin_t, seq, tm):
    t, d = x2.shape
    nr, npl, ng = w_rope.shape[1], w_plain.shape[1], w_gate.shape[1]
    nqn, nqs = NSA_HEADS * HEAD_DIM, SWA_HEADS * HEAD_DIM
    nkk = nr - nqn - nqs
    spb = seq // tm
    row = lambda i: (i, 0)
    return pl.pallas_call(
        _proj_kernel,
        grid=(t // tm,),
        in_specs=[pl.BlockSpec((tm, d), row), _full(w_rope.shape), _full(w_plain.shape),
                  _full(w_gate.shape),
                  pl.BlockSpec((tm, LANES), lambda i: (i % spb, 0)),
                  pl.BlockSpec((tm, LANES), lambda i: (i % spb, 0))],
        out_specs=[pl.BlockSpec((tm, nqn), row), pl.BlockSpec((tm, nqs), row),
                   pl.BlockSpec((tm, nkk), row), pl.BlockSpec((tm, nqn), row),
                   pl.BlockSpec((tm, npl), row), pl.BlockSpec((tm, ng), row)],
        out_shape=[jax.ShapeDtypeStruct((t, nqn), BF16), jax.ShapeDtypeStruct((t, nqs), BF16),
                   jax.ShapeDtypeStruct((t, nkk), BF16), jax.ShapeDtypeStruct((t, nqn), BF16),
                   jax.ShapeDtypeStruct((t, npl), BF16), jax.ShapeDtypeStruct((t, ng), F32)],
        compiler_params=_params("parallel"),
        name="proj",
    )(x2, w_rope, w_plain, w_gate, cos_t, sin_t)


def _compress_kernel(a_ref, b_ref, pe_ref, w1_ref, w2_ref, out_ref):
    half = a_ref.shape[1]
    a = (a_ref[...].astype(F32) + pe_ref[0:1, :]).astype(BF16)
    b = (b_ref[...].astype(F32) + pe_ref[1:2, :]).astype(BF16)
    hid = jnp.dot(a, w1_ref[0:half, :], preferred_element_type=F32)
    hid = hid + jnp.dot(b, w1_ref[half:2 * half, :], preferred_element_type=F32)
    act = jax.nn.gelu(hid).astype(BF16)
    out_ref[...] = jnp.dot(act, w2_ref[...], preferred_element_type=F32).astype(BF16)


def _compress(t_lo, t_hi, pe2, w1, w2):
    two, bg, ncp, half = t_lo.shape
    hid = w1.shape[2]
    blk = lambda shape: pl.BlockSpec((None, None) + shape, lambda j, i: (j, i, 0, 0))
    wsp = lambda shape: pl.BlockSpec((None,) + shape, lambda j, i: (j, 0, 0))
    return pl.pallas_call(
        _compress_kernel,
        grid=(two, bg),
        in_specs=[blk((ncp, half)), blk((ncp, half)), wsp((2, half)), wsp((2 * half, hid)),
                  wsp((hid, HEAD_DIM))],
        out_specs=blk((ncp, HEAD_DIM)),
        out_shape=jax.ShapeDtypeStruct((two, bg, ncp, HEAD_DIM), BF16),
        compiler_params=_params("parallel", "parallel"),
        name="compress",
    )(t_lo, t_hi, pe2, w1, w2)


def _stack_heads(q, g, r):
    return jnp.concatenate(
        [q[:, (g * r + j) * HEAD_DIM:(g * r + j + 1) * HEAD_DIM] for j in range(r)], axis=0)


def _unstack_heads(parts, r, qb):
    return jnp.concatenate([o[j * qb:(j + 1) * qb] for o in parts for j in range(r)], axis=1)


def _unstack_heads_t(parts, r, qb):
    blocks = []
    for o in parts:
        for j in range(0, r, 2):
            pair = jnp.concatenate([o[:, j * qb:(j + 1) * qb], o[:, (j + 1) * qb:(j + 2) * qb]],
                                   axis=0)
            blocks.append(pair.T)
    return jnp.concatenate(blocks, axis=1)


def _topk_mask_cols(vals, k):
    n = vals.shape[0]
    row = lax.broadcasted_iota(I32, vals.shape, 0).astype(F32)
    taken = jnp.zeros(vals.shape, F32)
    work = vals
    for _ in range(k):
        mx = jnp.max(work, axis=0, keepdims=True)
        first = jnp.min(jnp.where(work == mx, row, float(n)), axis=0, keepdims=True)
        pick = row == first
        taken = jnp.where(pick, 1.0, taken)
        work = jnp.where(pick, -jnp.inf, work)
    return taken > 0.5


def _cmp_kernel(q_ref, kc_ref, vct_ref, ovt_ref, o_ref, notsel_ref):
    qb = q_ref.shape[0]
    ncp = kc_ref.shape[1]
    nselp = ovt_ref.shape[0]
    i = pl.program_id(1)
    rows = R_NSA * qb
    pos = i * qb + (lax.broadcasted_iota(I32, (1, rows), 1) & (qb - 1))
    cend = lax.broadcasted_iota(I32, (ncp, 1), 0) * CMP_STRIDE + (CMP_BLOCK - 1)
    bias = jnp.where(cend <= pos, 0.0, MASKED)
    live = (pos >= CMP_BLOCK - 1).astype(F32)
    q = q_ref[...]
    outs, imps = [], []
    for g in range(NSA_KV):
        qg = _stack_heads(q, g, R_NSA) * (HEAD_DIM ** -0.5)
        st = lax.dot_general(kc_ref[g], qg, NT_DIMS, preferred_element_type=F32)
        st = st + bias
        e = jnp.exp(st - jnp.max(st, axis=0, keepdims=True))
        pt = e * (live / jnp.sum(e, axis=0, keepdims=True))
        outs.append(jnp.dot(vct_ref[g], pt.astype(BF16), preferred_element_type=F32))
        psum = pt[:, 0:qb]
        for j in range(1, R_NSA):
            psum = psum + pt[:, j * qb:(j + 1) * qb]
        p_hi = psum.astype(BF16)
        p_lo = (psum - p_hi.astype(F32)).astype(BF16)
        imps.append(jnp.dot(ovt_ref[...], p_hi, preferred_element_type=F32)
                    + jnp.dot(ovt_ref[...], p_lo, preferred_element_type=F32))
    imp = jnp.concatenate(imps, axis=1)
    lane = lax.broadcasted_iota(I32, (1, NSA_KV * qb), 1)
    cur = (i * qb + (lane & (qb - 1))) >> 6
    jb = lax.broadcasted_iota(I32, (nselp, 1), 0)
    forced = (jb == 0) | (jb == cur) | (jb == cur - 1)
    imp = jnp.where(jb > cur, -1.0, jnp.where(forced, 1e6, imp))
    notsel = jnp.where(_topk_mask_cols(imp, SEL_TOPN), 0.0, 1.0)
    for g in range(NSA_KV):
        notsel_ref[g] = notsel[:, g * qb:(g + 1) * qb].T.astype(BF16)
    o_ref[...] = _unstack_heads_t(outs, R_NSA, qb)


def _cmp_attention(q_raw, kc, vc, overlap_t, qb):
    vc = vc.transpose(0, 1, 3, 2)
    b, s, hq = q_raw.shape
    _, g, ncp, dh = kc.shape
    nselp = overlap_t.shape[0]
    return pl.pallas_call(
        _cmp_kernel,
        grid=(b, s // qb),
        in_specs=[pl.BlockSpec((None, qb, hq), lambda bi, i: (bi, i, 0)),
                  pl.BlockSpec((None, g, ncp, dh), lambda bi, i: (bi, 0, 0, 0)),
                  pl.BlockSpec((None, g, dh, ncp), lambda bi, i: (bi, 0, 0, 0)),
                  _full(overlap_t.shape)],
        out_specs=[pl.BlockSpec((None, qb, hq), lambda bi, i: (bi, i, 0)),
                   pl.BlockSpec((None, g, qb, nselp), lambda bi, i: (bi, 0, i, 0))],
        out_shape=[jax.ShapeDtypeStruct((b, s, hq), F32),
                   jax.ShapeDtypeStruct((b, g, s, nselp), BF16)],
        compiler_params=_params("parallel", "parallel"),
        name="cmp_attn",
    )(q_raw, kc, vc, overlap_t)


def _sel_kernel(q_ref, notsel_ref, k_ref, vt_ref, o_ref, *score_bufs, tk):
    s_even, s_odd = score_bufs[:NSA_KV], score_bufs[NSA_KV:]
    qb = q_ref.shape[0]
    i = pl.program_id(1)
    rows = R_NSA * qb
    qpos = i * qb + (lax.broadcasted_iota(I32, (1, rows), 1) & (qb - 1))
    n_clear = (i * qb) // tk
    q = q_ref[...]
    q_augs = []
    for g in range(NSA_KV):
        qg = _stack_heads(q, g, R_NSA) * (HEAD_DIM ** -0.5)
        q_augs.append(jnp.concatenate(
            [qg, jnp.zeros((rows, LANES - HEAD_DIM), BF16),
             jnp.concatenate([notsel_ref[g]] * R_NSA, axis=0)], axis=1))

    def scores(kt, g):
        start = pl.multiple_of(kt * tk, tk)
        return lax.dot_general(k_ref[g, pl.ds(start, tk), :], q_augs[g], NT_DIMS,
                               preferred_element_type=F32)

    def consume(kt, g, s_ref, m, acc, causal):
        start = pl.multiple_of(kt * tk, tk)
        vt_t = jnp.concatenate([vt_ref[g * HEAD_DIM:(g + 1) * HEAD_DIM, pl.ds(start, tk)],
                                jnp.ones((BF16_SUBLANES, tk), vt_ref.dtype)], axis=0)
        ms, accs = [], []
        for c in range(rows // SEL_CHUNK):
            cols = slice(c * SEL_CHUNK, (c + 1) * SEL_CHUNK)
            st = s_ref[:, cols]
            if causal:
                kpos = start + lax.broadcasted_iota(I32, (tk, 1), 0)
                st = jnp.where(kpos <= qpos[:, cols], st, MASKED)
            m_new = jnp.maximum(m[:, cols], jnp.max(st, axis=0, keepdims=True))
            pt = jnp.exp(st - m_new).astype(BF16)
            accs.append(jnp.exp(m[:, cols] - m_new) * acc[:, cols]
                        + jnp.dot(vt_t, pt, preferred_element_type=F32))
            ms.append(m_new)
        return jnp.concatenate(ms, axis=1), jnp.concatenate(accs, axis=1)

    def advance(kt, carry, cur, nxt):
        new = []
        for g in range(NSA_KV):
            nxt[g][...] = scores(kt + 1, g)
            new.append(consume(kt, g, cur[g], *carry[g], False))
        return tuple(new)

    def pair(j, carry):
        carry = advance(2 * j, carry, s_even, s_odd)
        return advance(2 * j + 1, carry, s_odd, s_even)

    def finish(carry, cur):
        outs = []
        for g in range(NSA_KV):
            _, acc = consume(n_clear, g, cur[g], *carry[g], True)
            outs.append(acc[:HEAD_DIM] / acc[HEAD_DIM:HEAD_DIM + 1])
        o_ref[...] = _unstack_heads_t(outs, R_NSA, qb)

    for g in range(NSA_KV):
        s_even[g][...] = scores(0, g)
    init = tuple((jnp.full((1, rows), M_INIT, F32),
                  jnp.zeros((HEAD_DIM + BF16_SUBLANES, rows), F32))
                 for _ in range(NSA_KV))
    carry = lax.fori_loop(0, n_clear // 2, pair, init)

    @pl.when((n_clear & 1) == 0)
    def _():
        finish(carry, s_even)

    @pl.when((n_clear & 1) == 1)
    def _():
        finish(advance(n_clear - 1, carry, s_even, s_odd), s_odd)


def _sel_attention(q_rot, notsel, k_aug, vt_aug, qb, tk):
    b, s, hq = q_rot.shape
    _, g, _, kw = k_aug.shape
    nselp = notsel.shape[3]
    return pl.pallas_call(
        functools.partial(_sel_kernel, tk=tk),
        grid=(b, s // qb),
        in_specs=[pl.BlockSpec((None, qb, hq), lambda bi, i: (bi, i, 0)),
                  pl.BlockSpec((None, g, qb, nselp), lambda bi, i: (bi, 0, i, 0)),
                  pl.BlockSpec((None, g, s, kw), lambda bi, i: (bi, 0, 0, 0)),
                  pl.BlockSpec((None, g * HEAD_DIM, s), lambda bi, i: (bi, 0, 0))],
        out_specs=pl.BlockSpec((None, qb, hq), lambda bi, i: (bi, i, 0)),
        out_shape=jax.ShapeDtypeStruct((b, s, hq), F32),
        scratch_shapes=[pltpu.VMEM((tk, R_NSA * qb), F32)] * (2 * g),
        compiler_params=_params("parallel", "parallel"),
        name="sel_attn",
    )(q_rot, notsel, k_aug, vt_aug)


def _band_kernel(*refs, window, wlen, r, kv, qb, has_sinks):
    if has_sinks:
        sink_ref, q_ref, k_ref, vt_ref, o_ref = refs
    else:
        q_ref, k_ref, vt_ref, o_ref = refs
    nsub = q_ref.shape[0] // qb
    rows = r * qb
    lane = lax.broadcasted_iota(I32, (1, rows), 1)

    def mask_bias(i):
        start = jnp.maximum((i + 1) * qb - wlen, 0)
        rel = i * qb + (lane & (qb - 1)) - start - lax.broadcasted_iota(I32, (wlen, 1), 0)
        return jnp.where((rel >= 0) & (rel < window), 0.0, MASKED)

    def body(shared_bias):
        first = pl.program_id(1) * nsub
        if shared_bias:
            bias = mask_bias(first)
        for sb in range(nsub):
            i = first + sb
            if not shared_bias:
                bias = mask_bias(i)
            start = pl.multiple_of(jnp.maximum((i + 1) * qb - wlen, 0), qb)
            q = q_ref[sb * qb:(sb + 1) * qb, :]
            kw = k_ref[pl.ds(start, wlen), :]
            vtw = vt_ref[:, pl.ds(start, wlen)]
            outs = []
            for g in range(kv):
                qg = _stack_heads(q, g, r) * (HEAD_DIM ** -0.5)
                st = lax.dot_general(kw[:, g * HEAD_DIM:(g + 1) * HEAD_DIM], qg, NT_DIMS,
                                     preferred_element_type=F32) + bias
                m = jnp.max(st, axis=0, keepdims=True)
                if has_sinks:
                    sk = jnp.full((1, rows), sink_ref[g * r], F32)
                    for j in range(1, r):
                        sk = jnp.where(lane >= j * qb, sink_ref[g * r + j], sk)
                    m = jnp.maximum(m, sk)
                e = jnp.exp(st - m).astype(BF16)
                v_ones = jnp.concatenate([vtw[g * HEAD_DIM:(g + 1) * HEAD_DIM, :],
                                          jnp.ones((BF16_SUBLANES, wlen), vtw.dtype)], axis=0)
                ot = jnp.dot(v_ones, e, preferred_element_type=F32)
                den = ot[HEAD_DIM:HEAD_DIM + 1]
                if has_sinks:
                    den = den + jnp.exp(sk - m)
                outs.append(ot[:HEAD_DIM] / den)
            o_ref[sb * qb:(sb + 1) * qb, :] = _unstack_heads_t(outs, r, qb)

    if nsub * qb >= wlen - qb:
        @pl.when(pl.program_id(1) == 0)
        def _():
            body(False)

        @pl.when(pl.program_id(1) > 0)
        def _():
            body(True)
    else:
        body(False)


def _band_attention(q_rot, k_all, k_blk, vt_all, v_blk, gk, window, sinks, qb):
    b, s, hq = q_rot.shape
    assert k_all.shape[2] % gk == 0 and vt_all.shape[1] % gk == 0
    kv = gk // HEAD_DIM
    r = hq // gk
    back = -(-window // qb)
    wlen = (back + 1) * qb
    assert wlen <= s
    has_sinks = sinks is not None
    tq = min(BAND_STEP_QUERIES, s)
    in_specs = [pl.BlockSpec((None, tq, hq), lambda bi, i: (bi, i, 0)),
                pl.BlockSpec((None, s, gk), lambda bi, i: (bi, 0, k_blk)),
                pl.BlockSpec((None, gk, s), lambda bi, i: (bi, v_blk, 0))]
    args = [q_rot, k_all, vt_all]
    if has_sinks:
        in_specs = [pl.BlockSpec(memory_space=pltpu.SMEM)] + in_specs
        args = [sinks.astype(F32)] + args
    return pl.pallas_call(
        functools.partial(_band_kernel, window=window, wlen=wlen, r=r, kv=kv, qb=qb,
                          has_sinks=has_sinks),
        grid=(b, s // tq),
        in_specs=in_specs,
        out_specs=pl.BlockSpec((None, tq, hq), lambda bi, i: (bi, i, 0)),
        out_shape=jax.ShapeDtypeStruct((b, s, hq), F32),
        compiler_params=_params("parallel", "parallel"),
        name="band_attn_sink" if has_sinks else "band_attn",
    )(*args)


def _layer_norm(v, g, b):
    mu = jnp.mean(v, axis=1, keepdims=True)
    c = v - mu
    var = jnp.mean(c * c, axis=1, keepdims=True)
    return c * lax.rsqrt(var + LN_EPS) * g + b


def _split_dot(a, w):
    hi = a.astype(BF16)
    lo = (a - hi.astype(F32)).astype(BF16)
    return (jnp.dot(hi, w, preferred_element_type=F32)
            + jnp.dot(lo, w, preferred_element_type=F32))


def _merge_kernel(ocmp_ref, osel_ref, owin_ref, oswa_ref, gates_ref, x_ref, exp_ref,
                  wbn_ref, wbs_ref, wo_ref, lng_ref, lnb_ref, wrh_ref, wrl_ref, br_ref,
                  h_ref, te_ref, tg_ref, tr_ref, counts_ref, base_ref, cnt_ref, *, alpha):
    d = x_ref.shape[1]
    hq = ocmp_ref.shape[1]
    gates = gates_ref[...]
    gn = jax.nn.sigmoid(gates[:, 2 * d:])
    gexp = _split_dot(gn, exp_ref[...])
    o_nsa = (gexp[:, 0:hq] * ocmp_ref[...] + gexp[:, hq:2 * hq] * osel_ref[...]
             + gexp[:, 2 * hq:3 * hq] * owin_ref[...])
    y_nsa = jnp.dot(o_nsa.astype(BF16), wbn_ref[...], preferred_element_type=F32)
    y_swa = jnp.dot(oswa_ref[...].astype(BF16), wbs_ref[...], preferred_element_type=F32)
    gm = jax.nn.sigmoid(gates[:, :2 * d])
    mixed = gm[:, :d] * y_nsa + gm[:, d:] * y_swa
    z = jnp.dot(mixed.astype(BF16), wo_ref[...], preferred_element_type=F32)
    h = _layer_norm(alpha * x_ref[...] + z, lng_ref[...], lnb_ref[...])
    h_ref[...] = h
    h_hi = h.astype(BF16)
    h_lo = (h - h_hi.astype(F32)).astype(BF16)
    logits = (jnp.dot(h_hi, wrh_ref[...], preferred_element_type=F32)
              + jnp.dot(h_lo, wrh_ref[...], preferred_element_type=F32)
              + jnp.dot(h_hi, wrl_ref[...], preferred_element_type=F32)) + br_ref[...]
    col = lax.broadcasted_iota(I32, logits.shape, 1).astype(F32)
    work = logits
    vals, ids = [], []
    for _ in range(TOP_K):
        mx = jnp.max(work, axis=1, keepdims=True)
        first = jnp.min(jnp.where(work == mx, col, float(LANES)), axis=1, keepdims=True)
        vals.append(mx)
        ids.append(first)
        work = jnp.where(col == first, -jnp.inf, work)
    es = [jnp.exp(v - vals[0]) for v in vals]
    den = es[0]
    for e in es[1:]:
        den = den + e
    @pl.when(pl.program_id(0) == 0)
    def _():
        cnt_ref[...] = jnp.zeros(cnt_ref.shape, F32)

    tm = logits.shape[0]
    hits = jnp.zeros(logits.shape, F32)
    for k in range(TOP_K):
        hits = jnp.where(col == ids[k], 1.0, hits)
    earlier = (lax.broadcasted_iota(I32, (tm, tm), 1)
               < lax.broadcasted_iota(I32, (tm, tm), 0))
    before = jnp.dot(jnp.where(earlier, 1.0, 0.0).astype(BF16), hits.astype(BF16),
                     preferred_element_type=F32) + cnt_ref[...]
    base_ref[...] = cnt_ref[...]
    cnt_ref[...] = cnt_ref[...] + jnp.sum(hits, axis=0, keepdims=True)
    counts_ref[...] = cnt_ref[...]
    te = jnp.zeros(logits.shape, F32)
    tg = jnp.zeros(logits.shape, F32)
    tr = jnp.zeros(logits.shape, F32)
    for k in range(TOP_K):
        rank = jnp.sum(jnp.where(col == ids[k], before, 0.0), axis=1, keepdims=True)
        te = jnp.where(col == float(k), ids[k], te)
        tg = jnp.where(col == float(k), es[k] / den, tg)
        tr = jnp.where(col == float(k), rank, tr)
    te_ref[...] = te.astype(I32)
    tg_ref[...] = tg
    tr_ref[...] = tr.astype(I32)


def _merge(o_cmp, o_sel, o_win, o_swa, gates, x2, expand, w_bn, w_bs, w_o, ln_g, ln_b,
           wr_hi, wr_lo, b_r, alpha, tm):
    t, d = x2.shape
    hq = o_cmp.shape[1]
    row = lambda i: (i, 0)
    tok = lambda n: pl.BlockSpec((tm, n), row)
    return pl.pallas_call(
        functools.partial(_merge_kernel, alpha=alpha),
        grid=(t // tm,),
        in_specs=[tok(hq), tok(hq), tok(hq), tok(hq), tok(gates.shape[1]), tok(d),
                  _full(expand.shape), _full(w_bn.shape), _full(w_bs.shape), _full(w_o.shape),
                  _full(ln_g.shape), _full(ln_b.shape), _full(wr_hi.shape), _full(wr_lo.shape),
                  _full(b_r.shape)],
        out_specs=[tok(d), tok(LANES), tok(LANES), tok(LANES), _full((1, LANES)),
                   pl.BlockSpec((None, 1, LANES), lambda i: (i, 0, 0))],
        out_shape=[jax.ShapeDtypeStruct((t, d), F32), jax.ShapeDtypeStruct((t, LANES), I32),
                   jax.ShapeDtypeStruct((t, LANES), F32), jax.ShapeDtypeStruct((t, LANES), I32),
                   jax.ShapeDtypeStruct((1, LANES), F32),
                   jax.ShapeDtypeStruct((t // tm, 1, LANES), F32)],
        scratch_shapes=[pltpu.VMEM((1, LANES), F32)],
        compiler_params=_params("arbitrary"),
        name="merge_ln_router",
    )(o_cmp, o_sel, o_win, o_swa, gates, x2, expand, w_bn, w_bs, w_o, ln_g, ln_b,
      wr_hi, wr_lo, b_r)


def _dispatch_kernel(pend_ref, padded_ref, cseg_ref, first_ref, len_ref, tot_ref,
                     h_ref, cpos_ref, xs_ref, comp0, comp1, zbuf, sem, zsem):
    j = pl.program_id(0)
    nt = pl.num_programs(0)
    tm = h_ref.shape[0]
    n_rows = xs_ref.shape[0]
    comps = (comp0, comp1)
    n_comp = comp0.shape[0]

    @pl.when(j == 0)
    def _():
        zbuf[...] = jnp.zeros(zbuf.shape, F32)
        used_rows = pend_ref[N_EXPERTS - 1]

        def zero_copy(start):
            start = pl.multiple_of(start, MOE_ROWS)
            return pltpu.make_async_copy(zbuf, xs_ref.at[pl.ds(start, MOE_ROWS), :], zsem)

        blocks = [(padded_ref[e] > 0, pend_ref[e] - MOE_ROWS) for e in range(N_EXPERTS)]
        blocks += [(used_rows + b * MOE_ROWS < n_rows, used_rows + b * MOE_ROWS)
                   for b in range(n_rows // MOE_ROWS - (nt * tm * TOP_K) // MOE_ROWS)]
        for cond, start in blocks:
            @pl.when(cond)
            def _(start=start):
                zero_copy(start).start()
        for cond, start in blocks:
            @pl.when(cond)
            def _(start=start):
                zero_copy(start).wait()

    def wait_writes(tile, slot):
        rows = pl.multiple_of(tot_ref[tile], F32_SUBLANES)
        pltpu.make_async_copy(comps[slot].at[pl.ds(0, rows), :], xs_ref.at[pl.ds(0, rows), :],
                              sem.at[slot]).wait()

    for slot in range(2):
        @pl.when((j >= 2) & ((j & 1) == slot))
        def _(slot=slot):
            wait_writes(j - 2, slot)

        @pl.when((j & 1) == slot)
        def _(slot=slot):
            row = lax.broadcasted_iota(I32, (n_comp, 1), 0)
            cpos = cpos_ref[...]
            sel = jnp.zeros((n_comp, tm), F32)
            for k in range(TOP_K):
                sel = jnp.where(row == cpos[k:k + 1, :], 1.0, sel)
            comps[slot][...] = jnp.dot(sel.astype(BF16), h_ref[...].astype(BF16),
                                       preferred_element_type=F32)
            for e in range(N_EXPERTS):
                ln = len_ref[j * N_EXPERTS + e]
                first = first_ref[j * N_EXPERTS + e]
                seg = cseg_ref[j * N_EXPERTS + e]
                for p in _run_sizes(tm):
                    done = ln & (-2 * p)
                    src = comps[slot].at[pl.ds(pl.multiple_of(seg + done, F32_SUBLANES), p), :]
                    dst = xs_ref.at[pl.ds(pl.multiple_of(first + done, F32_SUBLANES), p), :]
                    pl.when((ln & p) != 0)(pltpu.make_async_copy(src, dst, sem.at[slot]).start)

        @pl.when((j == nt - 1) & ((j & 1) == slot))
        def _(slot=slot):
            @pl.when(j >= 1)
            def _():
                wait_writes(j - 1, 1 - slot)
            wait_writes(j, slot)


def _dispatch(pend, padded, cseg, seg_first, seg_len, seg_tot, cpos_t, h, n_rows, tm):
    t, d = h.shape
    n_comp = -(-(tm * TOP_K + N_EXPERTS * (F32_SUBLANES - 1)) // MXU_DEPTH) * MXU_DEPTH
    grid_spec = pltpu.PrefetchScalarGridSpec(
        num_scalar_prefetch=6,
        grid=(t // tm,),
        in_specs=[pl.BlockSpec((tm, d), lambda i, *_: (i, 0)),
                  pl.BlockSpec((TOP_K, tm), lambda i, *_: (0, i))],
        out_specs=pl.BlockSpec(memory_space=pl.ANY),
        scratch_shapes=[pltpu.VMEM((n_comp, d), F32), pltpu.VMEM((n_comp, d), F32),
                        pltpu.VMEM((MOE_ROWS, d), F32), pltpu.SemaphoreType.DMA((2,)),
                        pltpu.SemaphoreType.DMA(())],
    )
    return pl.pallas_call(
        _dispatch_kernel,
        grid_spec=grid_spec,
        out_shape=jax.ShapeDtypeStruct((n_rows, d), F32),
        compiler_params=_params("arbitrary"),
        name="moe_dispatch",
    )(pend, padded, cseg, seg_first, seg_len, seg_tot, h, cpos_t)


def _moe_kernel(blk_e_ref, n_used_ref, x_ref, wi_ref, bi_ref, wo_ref, bo_ref, y_ref,
                wi_bf, wo_bf):
    f = wo_ref.shape[0]
    b = pl.program_id(0)

    @pl.when((b == 0) | (blk_e_ref[b] != blk_e_ref[jnp.maximum(b - 1, 0)]))
    def _():
        wi_bf[...] = wi_ref[...].astype(BF16)
        wo_bf[...] = wo_ref[...].astype(BF16)

    @pl.when(b < n_used_ref[0])
    def _():
        hdn = jnp.dot(x_ref[...].astype(BF16), wi_bf[...],
                      preferred_element_type=F32) + bi_ref[...]
        hg = jnp.minimum(hdn[:, :f], SWIGLU_LIMIT)
        hu = jnp.clip(hdn[:, f:], -SWIGLU_LIMIT, SWIGLU_LIMIT)
        act = hg * jax.nn.sigmoid(SWIGLU_ALPHA * hg) * (hu + 1.0)
        y_ref[...] = jnp.dot(act.astype(BF16), wo_bf[...],
                             preferred_element_type=F32) + bo_ref[...]

    @pl.when(b >= n_used_ref[0])
    def _():
        y_ref[...] = jnp.zeros(y_ref.shape, F32)


def _moe_blocks(blk_e, n_used, xs, w_in, b_in, w_out, b_out):
    n_rows, d = xs.shape
    e, _, f2 = w_in.shape
    f = w_out.shape[1]
    n_blk = n_rows // MOE_ROWS
    grid_spec = pltpu.PrefetchScalarGridSpec(
        num_scalar_prefetch=2,
        grid=(n_blk,),
        in_specs=[pl.BlockSpec((MOE_ROWS, d), lambda b, be, nu: (jnp.minimum(b, nu[0] - 1), 0)),
                  pl.BlockSpec((None, d, f2), lambda b, be, nu: (be[b], 0, 0)),
                  pl.BlockSpec((None, 1, f2), lambda b, be, nu: (be[b], 0, 0)),
                  pl.BlockSpec((None, f, d), lambda b, be, nu: (be[b], 0, 0)),
                  pl.BlockSpec((None, 1, d), lambda b, be, nu: (be[b], 0, 0))],
        out_specs=pl.BlockSpec((MOE_ROWS, d), lambda b, be, nu: (b, 0)),
        scratch_shapes=[pltpu.VMEM((d, f2), BF16), pltpu.VMEM((f, d), BF16)],
    )
    return pl.pallas_call(
        _moe_kernel,
        grid_spec=grid_spec,
        out_shape=jax.ShapeDtypeStruct((n_rows, d), F32),
        compiler_params=_params("arbitrary"),
        name="moe_experts",
    )(blk_e, n_used, xs, w_in, b_in, w_out, b_out)


def _run_sizes(tm):
    sizes, p = [], F32_SUBLANES
    while p <= tm:
        sizes.append(p)
        p *= 2
    return tuple(reversed(sizes))


def _stage_rows(tm):
    rows = tm * TOP_K + N_EXPERTS * 2 * (F32_SUBLANES - 1)
    return -(-rows // MXU_DEPTH) * MXU_DEPTH


def _final_kernel(start_ref, len_ref, seg_ref, tot_ref, h_ref, tg_ref, sp_ref, ys_ref, g_ref,
                  b_ref, o_ref, stage0, stage1, sem, *, alpha):
    i = pl.program_id(0)
    tm = h_ref.shape[0]
    stages = (stage0, stage1)
    n_stage = stage0.shape[0]

    def fetch(tile, into):
        for e in range(N_EXPERTS):
            ln = len_ref[tile * N_EXPERTS + e]
            first = start_ref[tile * N_EXPERTS + e]
            seg = seg_ref[tile * N_EXPERTS + e]
            for p in _run_sizes(tm):
                done = ln & (-2 * p)
                src = ys_ref.at[pl.ds(pl.multiple_of(first + done, F32_SUBLANES), p), :]
                dst = stages[into].at[pl.ds(pl.multiple_of(seg + done, F32_SUBLANES), p), :]
                pl.when((ln & p) != 0)(pltpu.make_async_copy(src, dst, sem.at[into]).start)

    @pl.when(i == 0)
    def _():
        stage0[...] = jnp.zeros(stage0.shape, F32)
        stage1[...] = jnp.zeros(stage1.shape, F32)
        fetch(0, 0)

    for slot in range(2):
        @pl.when((i & 1) == slot)
        def _(slot=slot):
            fetch(i + 1, 1 - slot)
            rows = pl.multiple_of(tot_ref[i], F32_SUBLANES)
            pltpu.make_async_copy(ys_ref.at[pl.ds(0, rows), :],
                                  stages[slot].at[pl.ds(0, rows), :], sem.at[slot]).wait()
            col = lax.broadcasted_iota(I32, (1, n_stage), 1)
            tg = tg_ref[...]
            sp = sp_ref[...]
            q = jnp.zeros((tm, n_stage), F32)
            for k in range(TOP_K):
                q = jnp.where(col == sp[:, k:k + 1], tg[:, k:k + 1], q)
            f = _split_dot(q, stages[slot][...].astype(BF16))
            o_ref[...] = _layer_norm(alpha * h_ref[...] + f, g_ref[...], b_ref[...])


def _final(run_start, run_len, run_seg, run_tot, h, top_g, spos, ys, ln_g, ln_b, alpha, tm):
    t, d = h.shape
    grid_spec = pltpu.PrefetchScalarGridSpec(
        num_scalar_prefetch=4,
        grid=(t // tm,),
        in_specs=[pl.BlockSpec((tm, d), lambda i, *_: (i, 0)),
                  pl.BlockSpec((tm, LANES), lambda i, *_: (i, 0)),
                  pl.BlockSpec((tm, TOP_K), lambda i, *_: (i, 0)),
                  pl.BlockSpec(memory_space=pl.ANY),
                  pl.BlockSpec(ln_g.shape, lambda i, *_: (0, 0)),
                  pl.BlockSpec(ln_b.shape, lambda i, *_: (0, 0))],
        out_specs=pl.BlockSpec((tm, d), lambda i, *_: (i, 0)),
        scratch_shapes=[pltpu.VMEM((_stage_rows(tm), d), F32),
                        pltpu.VMEM((_stage_rows(tm), d), F32), pltpu.SemaphoreType.DMA((2,))],
    )
    return pl.pallas_call(
        functools.partial(_final_kernel, alpha=alpha),
        grid_spec=grid_spec,
        out_shape=jax.ShapeDtypeStruct((t, d), F32),
        compiler_params=_params("arbitrary"),
        name="combine_ln",
    )(run_start, run_len, run_seg, run_tot, h, top_g, spos, ys, ln_g, ln_b)


def _rope_tables(s):
    half = HEAD_DIM // 2
    inv = ROPE_THETA ** (-np.arange(half, dtype=np.float64) / half)
    ang = np.arange(s, dtype=np.float64)[:, None] * inv[None, :]
    cos, sin = np.cos(ang).astype(np.float32), np.sin(ang).astype(np.float32)
    reps = LANES // HEAD_DIM
    cos_t = np.tile(np.concatenate([cos, cos], axis=1), (1, reps))
    sin_t = np.tile(np.concatenate([-sin, sin], axis=1), (1, reps))
    return jnp.asarray(cos_t), jnp.asarray(sin_t)


def _ceil_to(v, m):
    return (v + m - 1) // m * m


def _moe_plan(top_e, rank, tile_base, counts, t, tm, dt):
    a = t * TOP_K
    per, nd = dt // tm, t // dt
    d_base = tile_base[::per]
    d_cnt = jnp.concatenate([d_base[1:], counts[None, :]], axis=0) - d_base
    seg_len = _ceil_to(d_cnt, F32_SUBLANES)
    padded = _ceil_to(jnp.sum(seg_len, axis=0), MOE_ROWS)
    pend = jnp.cumsum(padded)
    seg_first = (pend - padded)[None, :] + jnp.cumsum(seg_len, axis=0) - seg_len
    cseg = jnp.cumsum(seg_len, axis=1) - seg_len
    onehot = (top_e[:, :, None] == jnp.arange(N_EXPERTS, dtype=I32)).reshape(
        nd, dt, TOP_K, N_EXPERTS)

    def pick(table):
        return jnp.sum(jnp.where(onehot, table[:, None, None, :], 0), axis=3).reshape(t, TOP_K)

    local = rank - pick(d_base)
    dest = pick(seg_first) + local
    cpos = pick(cseg) + local
    run_first = jnp.repeat(seg_first - d_base, per, axis=0) + tile_base
    n_blk = -(-(a + nd * N_EXPERTS * (F32_SUBLANES - 1)) // MOE_ROWS) + N_EXPERTS
    blk_first = jnp.arange(n_blk, dtype=I32) * MOE_ROWS
    blk_e = jnp.minimum(jnp.sum(pend[None, :] <= blk_first[:, None], axis=1),
                        N_EXPERTS - 1).astype(I32)
    n_used = (pend[-1] // MOE_ROWS).astype(I32).reshape(1)
    flat = lambda v: v.astype(I32).reshape(-1)
    segments = (flat(cseg), flat(seg_first), flat(seg_len), jnp.sum(seg_len, axis=1).astype(I32))
    return (segments, dest.astype(I32), cpos.astype(I32), run_first, blk_e, n_used,
            n_blk * MOE_ROWS, pend.astype(I32), padded.astype(I32))


def _combine_plan(top_e, dest, run_first, tile_base, counts, tm):
    cnt = jnp.concatenate([tile_base[1:], counts[None, :]], axis=0) - tile_base
    lead = run_first & (F32_SUBLANES - 1)
    run_len = jnp.where(cnt > 0, _ceil_to(lead + cnt, F32_SUBLANES), 0)
    run_seg = jnp.cumsum(run_len, axis=1) - run_len
    shift = (run_seg + lead - run_first)[:, None, None, :]
    onehot = (top_e[:, :, None] == jnp.arange(N_EXPERTS, dtype=I32)).reshape(
        -1, tm, TOP_K, N_EXPERTS)
    spos = jnp.sum(jnp.where(onehot, shift, 0), axis=3).reshape(dest.shape) + dest
    flat = lambda v: jnp.pad(v.astype(I32), ((0, 1), (0, 0))).reshape(-1)
    run_tot = jnp.sum(run_len, axis=1).astype(I32)
    return flat(run_first - lead), flat(run_len), flat(run_seg), run_tot, spos.astype(I32)


def _layer(x, w_in, k_pe, k_w1, k_w2, v_pe, v_w1, v_w2, sinks, w_br_nsa, w_br_swa, w_out,
           ln1_g, ln1_b, w_router, b_router, w_e_in, b_e_in, w_e_out, b_e_out, ln2_g, ln2_b,
           alpha):
    b, s, d = x.shape
    t = b * s
    qb = Q_BLOCK
    nq_n, nkv = NSA_HEADS * HEAD_DIM, NSA_KV * HEAD_DIM
    nq_s, nkv_s = SWA_HEADS * HEAD_DIM, SWA_KV * HEAD_DIM
    widths = (nq_n, nkv, nkv, nkv, nkv, nkv, nkv, NSA_HEADS * 3, nq_s, nkv_s, nkv_s, 2 * d)
    offs = [0]
    for w in widths:
        offs.append(offs[-1] + w)
    col = lambda j: w_in[:, offs[j]:offs[j + 1]]
    (c_qn, c_kc, c_vc, c_ks, c_vs, c_kw, c_vw, c_gn, c_qs, c_k_s, c_v_s, c_gm) = map(col, range(12))
    w_rope = jnp.concatenate([c_qn, c_qs, c_ks, c_kw, c_k_s], axis=1).astype(BF16)
    w_plain = jnp.concatenate([c_kc, c_vc, c_vs, c_vw, c_v_s], axis=1).astype(BF16)
    gn_pad = LANES - NSA_HEADS * 3
    w_gate = jnp.concatenate([c_gm, c_gn, jnp.zeros((d, gn_pad), F32)], axis=1).astype(BF16)
    cos_t, sin_t = _rope_tables(s)

    x2 = x.reshape(t, d)
    qn_rot, qs_rot, kk_rot, qn_raw, plain, gates = _project(
        x2, w_rope, w_plain, w_gate, cos_t, sin_t, s, min(256, s))

    nc = (s - CMP_BLOCK) // CMP_STRIDE + 1
    ncp = s // CMP_STRIDE
    half = CMP_STRIDE * HEAD_DIM

    def halves(cols):
        v = cols.reshape(b, s, NSA_KV, HEAD_DIM).transpose(0, 2, 1, 3)
        return v.reshape(b * NSA_KV, ncp, half)

    t2 = jnp.stack([halves(plain[:, 0:nkv]), halves(plain[:, nkv:2 * nkv])])
    t_lo = t2
    t_hi = jnp.concatenate([t2[:, :, 1:], jnp.zeros_like(t2[:, :, :1])], axis=2)
    pe2 = jnp.stack([k_pe.reshape(2, half), v_pe.reshape(2, half)])
    w1 = jnp.stack([k_w1, v_w1]).astype(BF16)
    w2 = jnp.stack([k_w2, v_w2]).astype(BF16)
    kvc = _compress(t_lo, t_hi, pe2, w1, w2).reshape(2, b, NSA_KV, ncp, HEAD_DIM)

    nsel = s // SEL_BLOCK
    nselp = -(-nsel // LANES) * LANES
    cstart = np.arange(ncp) * CMP_STRIDE
    sstart = np.arange(nselp) * SEL_BLOCK
    overlap = ((cstart[:, None] < sstart[None, :] + SEL_BLOCK)
               & (cstart[:, None] + CMP_BLOCK > sstart[None, :])
               & (np.arange(ncp)[:, None] < nc) & (np.arange(nselp)[None, :] < nsel))
    o_cmp, notsel = _cmp_attention(qn_raw.reshape(b, s, nq_n), kvc[0], kvc[1],
                                   jnp.asarray(overlap.T.astype(BF16)),
                                   min(CMP_STEP_QUERIES, s))

    def group_major(cols):
        return cols.reshape(b, s, NSA_KV, HEAD_DIM).transpose(0, 2, 1, 3)

    k_sel = group_major(kk_rot[:, 0:nkv])
    onehot = (np.arange(s)[:, None] // SEL_BLOCK == np.arange(nselp)[None, :])
    k_tail = np.concatenate([np.zeros((s, LANES - HEAD_DIM), np.float32),
                             np.where(onehot, SEL_PENALTY, 0.0).astype(np.float32)], axis=1)
    k_aug = jnp.concatenate(
        [k_sel, jnp.broadcast_to(jnp.asarray(k_tail.astype(BF16)), (b, NSA_KV) + k_tail.shape)],
        axis=3)
    vt_all = plain.reshape(b, s, -1)[:, :, 2 * nkv:].transpose(0, 2, 1)
    o_sel = _sel_attention(qn_rot.reshape(b, s, nq_n), notsel, k_aug, vt_all,
                           min(SEL_STEP_QUERIES, s), min(SEL_KEY_TILE, s))

    kk3 = kk_rot.reshape(b, s, -1)
    o_win = _band_attention(qn_rot.reshape(b, s, nq_n), kk3, 1, vt_all, 1, nkv,
                            NSA_WINDOW, None, qb)
    o_swa = _band_attention(qs_rot.reshape(b, s, nq_s), kk3, 2, vt_all, 2, nkv_s,
                            SWA_WINDOW, sinks, qb)

    gi = np.arange(LANES)
    ci = np.arange(3 * nq_n)
    expand = jnp.asarray(((gi[:, None] // 3 == (ci[None, :] % nq_n) // HEAD_DIM)
                          & (gi[:, None] % 3 == ci[None, :] // nq_n)
                          & (gi[:, None] < NSA_HEADS * 3)).astype(BF16))
    wr_pad = jnp.pad(w_router, ((0, 0), (0, LANES - N_EXPERTS)))
    wr_hi = wr_pad.astype(BF16)
    wr_lo = (wr_pad - wr_hi.astype(F32)).astype(BF16)
    b_r = jnp.concatenate([b_router, jnp.full((LANES - N_EXPERTS,), -jnp.inf, F32)]).reshape(1, LANES)
    tm = min(256, t)
    h, top_e, top_g, rank, counts, tile_base = _merge(
        o_cmp.reshape(t, nq_n), o_sel.reshape(t, nq_n), o_win.reshape(t, nq_n),
        o_swa.reshape(t, nq_s), gates, x2, expand, w_br_nsa.astype(BF16), w_br_swa.astype(BF16),
        w_out.astype(BF16), ln1_g.reshape(1, d), ln1_b.reshape(1, d), wr_hi, wr_lo, b_r,
        alpha, tm)

    top_e, rank = top_e[:, :TOP_K], rank[:, :TOP_K]
    counts = counts[0, :N_EXPERTS].astype(I32)
    tile_base = tile_base[:, 0, :N_EXPERTS].astype(I32)
    dt = min(DISPATCH_TOKENS, t)
    segments, dest, cpos, run_first, blk_e, n_used, n_rows, pend, padded = _moe_plan(
        top_e, rank, tile_base, counts, t, tm, dt)
    xs = _dispatch(pend, padded, *segments, cpos.T, h, n_rows, dt)
    ys = _moe_blocks(blk_e, n_used, xs, w_e_in, b_e_in.reshape(N_EXPERTS, 1, -1),
                     w_e_out, b_e_out.reshape(N_EXPERTS, 1, -1))
    run_start, run_len, run_seg, run_tot, spos = _combine_plan(
        top_e, dest, run_first, tile_base, counts, tm)
    out = _final(run_start, run_len, run_seg, run_tot, h, top_g, spos, ys, ln2_g.reshape(1, d),
                 ln2_b.reshape(1, d), alpha, tm)
    return out.reshape(b, s, d)


def kernel(x, w_in, nsa_k_pe, nsa_k_w1, nsa_k_w2, nsa_v_pe, nsa_v_w1, nsa_v_w2, swa_sinks, w_br_nsa, w_br_swa, w_out, ln1_g, ln1_b, w_router, b_router, w_expert_in, b_expert_in, w_expert_out, b_expert_out, ln2_g, ln2_b):
    depth = w_in.shape[0]
    alpha = (2.0 * depth) ** 0.25
    for l in range(depth):
        x = _layer(x, w_in[l], nsa_k_pe[l], nsa_k_w1[l], nsa_k_w2[l], nsa_v_pe[l], nsa_v_w1[l],
                   nsa_v_w2[l], swa_sinks[l], w_br_nsa[l], w_br_swa[l], w_out[l], ln1_g[l],
                   ln1_b[l], w_router[l], b_router[l], w_expert_in[l], b_expert_in[l],
                   w_expert_out[l], b_expert_out[l], ln2_g[l], ln2_b[l], alpha)
    return x
```

```python
import functools

import jax
import jax.numpy as jnp
import numpy as np
from jax import lax
from jax.experimental import pallas as pl
from jax.experimental.pallas import tpu as pltpu

BF16 = jnp.bfloat16
F32 = jnp.float32
I32 = jnp.int32

HEAD_DIM = 64
NSA_HEADS = 8
NSA_KV = 2
CMP_BLOCK = 32
CMP_STRIDE = 16
SEL_BLOCK = 64
SEL_TOPN = 16
NSA_WINDOW = 512
SWA_HEADS = 8
SWA_KV = 2
SWA_WINDOW = 128
Q_BLOCK = 128
ROPE_THETA = 10000.0
N_EXPERTS = 32
TOP_K = 4
SWIGLU_LIMIT = 7.0
SWIGLU_ALPHA = 1.702
LN_EPS = 1e-5

LANES = 128
BF16_SUBLANES = 16
F32_SUBLANES = 8
MXU_DEPTH = 256
MASKED = -1e30
M_INIT = -1e29
SEL_PENALTY = -(2.0 ** 100)
VMEM_LIMIT = 52 * 1024 * 1024
MOE_ROWS = 256
DISPATCH_TOKENS = 512
BAND_STEP_QUERIES = 512
CMP_STEP_QUERIES = 512
SEL_STEP_QUERIES = 128
SEL_CHUNK = 512
SEL_KEY_TILE = 512

R_NSA = NSA_HEADS // NSA_KV
R_SWA = SWA_HEADS // SWA_KV
NT_DIMS = (((1,), (1,)), ((), ()))


def _params(*sem):
    return pltpu.CompilerParams(dimension_semantics=sem, vmem_limit_bytes=VMEM_LIMIT)


def _full(shape):
    n = len(shape)
    return pl.BlockSpec(shape, lambda *_: (0,) * n)


def _proj_kernel(x_ref, wr_ref, wp_ref, wg_ref, cos_ref, sin_ref,
                 qn_rot_ref, qs_rot_ref, kk_rot_ref, qn_raw_ref, plain_ref, gates_ref):
    xb = x_ref[...].astype(BF16)
    acc = jnp.dot(xb, wr_ref[...], preferred_element_type=F32)
    cos = cos_ref[...]
    sin = sin_ref[...]
    lane = lax.broadcasted_iota(I32, cos.shape, 1)
    first_half = (lane & (HEAD_DIM - 1)) < HEAD_DIM // 2

    def rope(t):
        partner = jnp.where(first_half, pltpu.roll(t, LANES - HEAD_DIM // 2, 1),
                            pltpu.roll(t, HEAD_DIM // 2, 1))
        return (t * cos + partner * sin).astype(BF16)

    nq = qn_rot_ref.shape[1] // LANES
    ns = qs_rot_ref.shape[1] // LANES
    nk = kk_rot_ref.shape[1] // LANES
    for c in range(nq):
        qn_rot_ref[:, c * LANES:(c + 1) * LANES] = rope(acc[:, c * LANES:(c + 1) * LANES])
    for c in range(ns):
        o = (nq + c) * LANES
        qs_rot_ref[:, c * LANES:(c + 1) * LANES] = rope(acc[:, o:o + LANES])
    for c in range(nk):
        o = (nq + ns + c) * LANES
        kk_rot_ref[:, c * LANES:(c + 1) * LANES] = rope(acc[:, o:o + LANES])
    qn_raw_ref[...] = acc[:, :nq * LANES].astype(BF16)
    plain_ref[...] = jnp.dot(xb, wp_ref[...], preferred_element_type=F32).astype(BF16)
    gates_ref[...] = jnp.dot(xb, wg_ref[...], preferred_element_type=F32)


def _project(x2, w_rope, w_plain, w_gate, cos_t, sin_t, seq, tm):
    t, d = x2.shape
    nr, npl, ng = w_rope.shape[1], w_plain.shape[1], w_gate.shape[1]
    nqn, nqs = NSA_HEADS * HEAD_DIM, SWA_HEADS * HEAD_DIM
    nkk = nr - nqn - nqs
    spb = seq // tm
    row = lambda i: (i, 0)
    return pl.pallas_call(
        _proj_kernel,
        grid=(t // tm,),
        in_specs=[pl.BlockSpec((tm, d), row), _full(w_rope.shape), _full(w_plain.shape),
                  _full(w_gate.shape),
                  pl.BlockSpec((tm, LANES), lambda i: (i % spb, 0)),
                  pl.BlockSpec((tm, LANES), lambda i: (i % spb, 0))],
        out_specs=[pl.BlockSpec((tm, nqn), row), pl.BlockSpec((tm, nqs), row),
                   pl.BlockSpec((tm, nkk), row), pl.BlockSpec((tm, nqn), row),
                   pl.BlockSpec((tm, npl), row), pl.BlockSpec((tm, ng), row)],
        out_shape=[jax.ShapeDtypeStruct((t, nqn), BF16), jax.ShapeDtypeStruct((t, nqs), BF16),
                   jax.ShapeDtypeStruct((t, nkk), BF16), jax.ShapeDtypeStruct((t, nqn), BF16),
                   jax.ShapeDtypeStruct((t, npl), BF16), jax.ShapeDtypeStruct((t, ng), F32)],
        compiler_params=_params("parallel"),
        name="proj",
    )(x2, w_rope, w_plain, w_gate, cos_t, sin_t)


def _compress_kernel(a_ref, b_ref, pe_ref, w1_ref, w2_ref, out_ref):
    half = a_ref.shape[1]
    a = (a_ref[...].astype(F32) + pe_ref[0:1, :]).astype(BF16)
    b = (b_ref[...].astype(F32) + pe_ref[1:2, :]).astype(BF16)
    hid = jnp.dot(a, w1_ref[0:half, :], preferred_element_type=F32)
    hid = hid + jnp.dot(b, w1_ref[half:2 * half, :], preferred_element_type=F32)
    act = jax.nn.gelu(hid).astype(BF16)
    out_ref[...] = jnp.dot(act, w2_ref[...], preferred_element_type=F32).astype(BF16)


def _compress(t_lo, t_hi, pe2, w1, w2):
    two, bg, ncp, half = t_lo.shape
    hid = w1.shape[2]
    blk = lambda shape: pl.BlockSpec((None, None) + shape, lambda j, i: (j, i, 0, 0))
    wsp = lambda shape: pl.BlockSpec((None,) + shape, lambda j, i: (j, 0, 0))
    return pl.pallas_call(
        _compress_kernel,
        grid=(two, bg),
        in_specs=[blk((ncp, half)), blk((ncp, half)), wsp((2, half)), wsp((2 * half, hid)),
                  wsp((hid, HEAD_DIM))],
        out_specs=blk((ncp, HEAD_DIM)),
        out_shape=jax.ShapeDtypeStruct((two, bg, ncp, HEAD_DIM), BF16),
        compiler_params=_params("parallel", "parallel"),
        name="compress",
    )(t_lo, t_hi, pe2, w1, w2)


def _stack_heads(q, g, r):
    return jnp.concatenate(
        [q[:, (g * r + j) * HEAD_DIM:(g * r + j + 1) * HEAD_DIM] for j in range(r)], axis=0)


def _unstack_heads(parts, r, qb):
    return jnp.concatenate([o[j * qb:(j + 1) * qb] for o in parts for j in range(r)], axis=1)


def _unstack_heads_t(parts, r, qb):
    blocks = []
    for o in parts:
        for j in range(0, r, 2):
            pair = jnp.concatenate([o[:, j * qb:(j + 1) * qb], o[:, (j + 1) * qb:(j + 2) * qb]],
                                   axis=0)
            blocks.append(pair.T)
    return jnp.concatenate(blocks, axis=1)


def _topk_mask_cols(vals, k):
    n = vals.shape[0]
    row = lax.broadcasted_iota(I32, vals.shape, 0).astype(F32)
    taken = jnp.zeros(vals.shape, F32)
    work = vals
    for _ in range(k):
        mx = jnp.max(work, axis=0, keepdims=True)
        first = jnp.min(jnp.where(work == mx, row, float(n)), axis=0, keepdims=True)
        pick = row == first
        taken = jnp.where(pick, 1.0, taken)
        work = jnp.where(pick, -jnp.inf, work)
    return taken > 0.5


def _cmp_kernel(q_ref, kc_ref, vct_ref, ovt_ref, o_ref, notsel_ref):
    qb = q_ref.shape[0]
    ncp = kc_ref.shape[1]
    nselp = ovt_ref.shape[0]
    i = pl.program_id(1)
    rows = R_NSA * qb
    pos = i * qb + (lax.broadcasted_iota(I32, (1, rows), 1) & (qb - 1))
    cend = lax.broadcasted_iota(I32, (ncp, 1), 0) * CMP_STRIDE + (CMP_BLOCK - 1)
    bias = jnp.where(cend <= pos, 0.0, MASKED)
    live = (pos >= CMP_BLOCK - 1).astype(F32)
    q = q_ref[...]
    outs, imps = [], []
    for g in range(NSA_KV):
        qg = _stack_heads(q, g, R_NSA) * (HEAD_DIM ** -0.5)
        st = lax.dot_general(kc_ref[g], qg, NT_DIMS, preferred_element_type=F32)
        st = st + bias
        e = jnp.exp(st - jnp.max(st, axis=0, keepdims=True))
        pt = e * (live / jnp.sum(e, axis=0, keepdims=True))
        outs.append(jnp.dot(vct_ref[g], pt.astype(BF16), preferred_element_type=F32))
        psum = pt[:, 0:qb]
        for j in range(1, R_NSA):
            psum = psum + pt[:, j * qb:(j + 1) * qb]
        p_hi = psum.astype(BF16)
        p_lo = (psum - p_hi.astype(F32)).astype(BF16)
        imps.append(jnp.dot(ovt_ref[...], p_hi, preferred_element_type=F32)
                    + jnp.dot(ovt_ref[...], p_lo, preferred_element_type=F32))
    imp = jnp.concatenate(imps, axis=1)
    lane = lax.broadcasted_iota(I32, (1, NSA_KV * qb), 1)
    cur = (i * qb + (lane & (qb - 1))) >> 6
    jb = lax.broadcasted_iota(I32, (nselp, 1), 0)
    forced = (jb == 0) | (jb == cur) | (jb == cur - 1)
    imp = jnp.where(jb > cur, -1.0, jnp.where(forced, 1e6, imp))
    notsel = jnp.where(_topk_mask_cols(imp, SEL_TOPN), 0.0, 1.0)
    for g in range(NSA_KV):
        notsel_ref[g] = notsel[:, g * qb:(g + 1) * qb].T.astype(BF16)
    o_ref[...] = _unstack_heads_t(outs, R_NSA, qb)


def _cmp_attention(q_raw, kc, vc, overlap_t, qb):
    vc = vc.transpose(0, 1, 3, 2)
    b, s, hq = q_raw.shape
    _, g, ncp, dh = kc.shape
    nselp = overlap_t.shape[0]
    return pl.pallas_call(
        _cmp_kernel,
        grid=(b, s // qb),
        in_specs=[pl.BlockSpec((None, qb, hq), lambda bi, i: (bi, i, 0)),
                  pl.BlockSpec((None, g, ncp, dh), lambda bi, i: (bi, 0, 0, 0)),
                  pl.BlockSpec((None, g, dh, ncp), lambda bi, i: (bi, 0, 0, 0)),
                  _full(overlap_t.shape)],
        out_specs=[pl.BlockSpec((None, qb, hq), lambda bi, i: (bi, i, 0)),
                   pl.BlockSpec((None, g, qb, nselp), lambda bi, i: (bi, 0, i, 0))],
        out_shape=[jax.ShapeDtypeStruct((b, s, hq), F32),
                   jax.ShapeDtypeStruct((b, g, s, nselp), BF16)],
        compiler_params=_params("parallel", "parallel"),
        name="cmp_attn",
    )(q_raw, kc, vc, overlap_t)


def _sel_kernel(q_ref, notsel_ref, k_ref, vt_ref, o_ref, *score_bufs, tk):
    s_even, s_odd = score_bufs[:NSA_KV], score_bufs[NSA_KV:]
    qb = q_ref.shape[0]
    i = pl.program_id(1)
    rows = R_NSA * qb
    qpos = i * qb + (lax.broadcasted_iota(I32, (1, rows), 1) & (qb - 1))
    n_clear = (i * qb) // tk
    q = q_ref[...]
    q_augs = []
    for g in range(NSA_KV):
        qg = _stack_heads(q, g, R_NSA) * (HEAD_DIM ** -0.5)
        q_augs.append(jnp.concatenate(
            [qg, jnp.zeros((rows, LANES - HEAD_DIM), BF16),
             jnp.concatenate([notsel_ref[g]] * R_NSA, axis=0)], axis=1))

    def scores(kt, g):
        start = pl.multiple_of(kt * tk, tk)
        return lax.dot_general(k_ref[g, pl.ds(start, tk), :], q_augs[g], NT_DIMS,
                               preferred_element_type=F32)

    def consume(kt, g, s_ref, m, acc, causal):
        start = pl.multiple_of(kt * tk, tk)
        vt_t = jnp.concatenate([vt_ref[g * HEAD_DIM:(g + 1) * HEAD_DIM, pl.ds(start, tk)],
                                jnp.ones((BF16_SUBLANES, tk), vt_ref.dtype)], axis=0)
        ms, accs = [], []
        for c in range(rows // SEL_CHUNK):
            cols = slice(c * SEL_CHUNK, (c + 1) * SEL_CHUNK)
            st = s_ref[:, cols]
            if causal:
                kpos = start + lax.broadcasted_iota(I32, (tk, 1), 0)
                st = jnp.where(kpos <= qpos[:, cols], st, MASKED)
            m_new = jnp.maximum(m[:, cols], jnp.max(st, axis=0, keepdims=True))
            pt = jnp.exp(st - m_new).astype(BF16)
            accs.append(jnp.exp(m[:, cols] - m_new) * acc[:, cols]
                        + jnp.dot(vt_t, pt, preferred_element_type=F32))
            ms.append(m_new)
        return jnp.concatenate(ms, axis=1), jnp.concatenate(accs, axis=1)

    def advance(kt, carry, cur, nxt):
        new = []
        for g in range(NSA_KV):
            nxt[g][...] = scores(kt + 1, g)
            new.append(consume(kt, g, cur[g], *carry[g], False))
        return tuple(new)

    def pair(j, carry):
        carry = advance(2 * j, carry, s_even, s_odd)
        return advance(2 * j + 1, carry, s_odd, s_even)

    def finish(carry, cur):
        outs = []
        for g in range(NSA_KV):
            _, acc = consume(n_clear, g, cur[g], *carry[g], True)
            outs.append(acc[:HEAD_DIM] / acc[HEAD_DIM:HEAD_DIM + 1])
        o_ref[...] = _unstack_heads_t(outs, R_NSA, qb)

    for g in range(NSA_KV):
        s_even[g][...] = scores(0, g)
    init = tuple((jnp.full((1, rows), M_INIT, F32),
                  jnp.zeros((HEAD_DIM + BF16_SUBLANES, rows), F32))
                 for _ in range(NSA_KV))
    carry = lax.fori_loop(0, n_clear // 2, pair, init)

    @pl.when((n_clear & 1) == 0)
    def _():
        finish(carry, s_even)

    @pl.when((n_clear & 1) == 1)
    def _():
        finish(advance(n_clear - 1, carry, s_even, s_odd), s_odd)


def _sel_attention(q_rot, notsel, k_aug, vt_aug, qb, tk):
    b, s, hq = q_rot.shape
    _, g, _, kw = k_aug.shape
    nselp = notsel.shape[3]
    return pl.pallas_call(
        functools.partial(_sel_kernel, tk=tk),
        grid=(b, s // qb),
        in_specs=[pl.BlockSpec((None, qb, hq), lambda bi, i: (bi, i, 0)),
                  pl.BlockSpec((None, g, qb, nselp), lambda bi, i: (bi, 0, i, 0)),
                  pl.BlockSpec((None, g, s, kw), lambda bi, i: (bi, 0, 0, 0)),
                  pl.BlockSpec((None, g * HEAD_DIM, s), lambda bi, i: (bi, 0, 0))],
        out_specs=pl.BlockSpec((None, qb, hq), lambda bi, i: (bi, i, 0)),
        out_shape=jax.ShapeDtypeStruct((b, s, hq), F32),
        scratch_shapes=[pltpu.VMEM((tk, R_NSA * qb), F32)] * (2 * g),
        compiler_params=_params("parallel", "parallel"),
        name="sel_attn",
    )(q_rot, notsel, k_aug, vt_aug)


def _band_kernel(*refs, window, wlen, r, kv, qb, has_sinks):
    if has_sinks:
        sink_ref, q_ref, k_ref, vt_ref, o_ref = refs
    else:
        q_ref, k_ref, vt_ref, o_ref = refs
    nsub = q_ref.shape[0] // qb
    rows = r * qb
    lane = lax.broadcasted_iota(I32, (1, rows), 1)

    def mask_bias(i):
        start = jnp.maximum((i + 1) * qb - wlen, 0)
        rel = i * qb + (lane & (qb - 1)) - start - lax.broadcasted_iota(I32, (wlen, 1), 0)
        return jnp.where((rel >= 0) & (rel < window), 0.0, MASKED)

    def body(shared_bias):
        first = pl.program_id(1) * nsub
        if shared_bias:
            bias = mask_bias(first)
        for sb in range(nsub):
            i = first + sb
            if not shared_bias:
                bias = mask_bias(i)
            start = pl.multiple_of(jnp.maximum((i + 1) * qb - wlen, 0), qb)
            q = q_ref[sb * qb:(sb + 1) * qb, :]
            kw = k_ref[pl.ds(start, wlen), :]
            vtw = vt_ref[:, pl.ds(start, wlen)]
            outs = []
            for g in range(kv):
                qg = _stack_heads(q, g, r) * (HEAD_DIM ** -0.5)
                st = lax.dot_general(kw[:, g * HEAD_DIM:(g + 1) * HEAD_DIM], qg, NT_DIMS,
                                     preferred_element_type=F32) + bias
                m = jnp.max(st, axis=0, keepdims=True)
                if has_sinks:
                    sk = jnp.full((1, rows), sink_ref[g * r], F32)
                    for j in range(1, r):
                        sk = jnp.where(lane >= j * qb, sink_ref[g * r + j], sk)
                    m = jnp.maximum(m, sk)
                e = jnp.exp(st - m).astype(BF16)
                v_ones = jnp.concatenate([vtw[g * HEAD_DIM:(g + 1) * HEAD_DIM, :],
                                          jnp.ones((BF16_SUBLANES, wlen), vtw.dtype)], axis=0)
                ot = jnp.dot(v_ones, e, preferred_element_type=F32)
                den = ot[HEAD_DIM:HEAD_DIM + 1]
                if has_sinks:
                    den = den + jnp.exp(sk - m)
                outs.append(ot[:HEAD_DIM] / den)
            o_ref[sb * qb:(sb + 1) * qb, :] = _unstack_heads_t(outs, r, qb)

    if nsub * qb >= wlen - qb:
        @pl.when(pl.program_id(1) == 0)
        def _():
            body(False)

        @pl.when(pl.program_id(1) > 0)
        def _():
            body(True)
    else:
        body(False)


def _band_attention(q_rot, k_all, k_blk, vt_all, v_blk, gk, window, sinks, qb):
    b, s, hq = q_rot.shape
    assert k_all.shape[2] % gk == 0 and vt_all.shape[1] % gk == 0
    kv = gk // HEAD_DIM
    r = hq // gk
    back = -(-window // qb)
    wlen = (back + 1) * qb
    assert wlen <= s
    has_sinks = sinks is not None
    tq = min(BAND_STEP_QUERIES, s)
    in_specs = [pl.BlockSpec((None, tq, hq), lambda bi, i: (bi, i, 0)),
                pl.BlockSpec((None, s, gk), lambda bi, i: (bi, 0, k_blk)),
                pl.BlockSpec((None, gk, s), lambda bi, i: (bi, v_blk, 0))]
    args = [q_rot, k_all, vt_all]
    if has_sinks:
        in_specs = [pl.BlockSpec(memory_space=pltpu.SMEM)] + in_specs
        args = [sinks.astype(F32)] + args
    return pl.pallas_call(
        functools.partial(_band_kernel, window=window, wlen=wlen, r=r, kv=kv, qb=qb,
                          has_sinks=has_sinks),
        grid=(b, s // tq),
        in_specs=in_specs,
        out_specs=pl.BlockSpec((None, tq, hq), lambda bi, i: (bi, i, 0)),
        out_shape=jax.ShapeDtypeStruct((b, s, hq), F32),
        compiler_params=_params("parallel", "parallel"),
        name="band_attn_sink" if has_sinks else "band_attn",
    )(*args)


def _layer_norm(v, g, b):
    mu = jnp.mean(v, axis=1, keepdims=True)
    c = v - mu
    var = jnp.mean(c * c, axis=1, keepdims=True)
    return c * lax.rsqrt(var + LN_EPS) * g + b


def _split_dot(a, w):
    hi = a.astype(BF16)
    lo = (a - hi.astype(F32)).astype(BF16)
    return (jnp.dot(hi, w, preferred_element_type=F32)
            + jnp.dot(lo, w, preferred_element_type=F32))


def _merge_kernel(ocmp_ref, osel_ref, owin_ref, oswa_ref, gates_ref, x_ref, exp_ref,
                  wbn_ref, wbs_ref, wo_ref, lng_ref, lnb_ref, wrh_ref, wrl_ref, br_ref,
                  h_ref, te_ref, tg_ref, tr_ref, counts_ref, base_ref, cnt_ref, *, alpha):
    d = x_ref.shape[1]
    hq = ocmp_ref.shape[1]
    gates = gates_ref[...]
    gn = jax.nn.sigmoid(gates[:, 2 * d:])
    gexp = _split_dot(gn, exp_ref[...])
    o_nsa = (gexp[:, 0:hq] * ocmp_ref[...] + gexp[:, hq:2 * hq] * osel_ref[...]
             + gexp[:, 2 * hq:3 * hq] * owin_ref[...])
    y_nsa = jnp.dot(o_nsa.astype(BF16), wbn_ref[...], preferred_element_type=F32)
    y_swa = jnp.dot(oswa_ref[...].astype(BF16), wbs_ref[...], preferred_element_type=F32)
    gm = jax.nn.sigmoid(gates[:, :2 * d])
    mixed = gm[:, :d] * y_nsa + gm[:, d:] * y_swa
    z = jnp.dot(mixed.astype(BF16), wo_ref[...], preferred_element_type=F32)
    h = _layer_norm(alpha * x_ref[...] + z, lng_ref[...], lnb_ref[...])
    h_ref[...] = h
    h_hi = h.astype(BF16)
    h_lo = (h - h_hi.astype(F32)).astype(BF16)
    logits = (jnp.dot(h_hi, wrh_ref[...], preferred_element_type=F32)
              + jnp.dot(h_lo, wrh_ref[...], preferred_element_type=F32)
              + jnp.dot(h_hi, wrl_ref[...], preferred_element_type=F32)) + br_ref[...]
    col = lax.broadcasted_iota(I32, logits.shape, 1).astype(F32)
    work = logits
    vals, ids = [], []
    for _ in range(TOP_K):
        mx = jnp.max(work, axis=1, keepdims=True)
        first = jnp.min(jnp.where(work == mx, col, float(LANES)), axis=1, keepdims=True)
        vals.append(mx)
        ids.append(first)
        work = jnp.where(col == first, -jnp.inf, work)
    es = [jnp.exp(v - vals[0]) for v in vals]
    den = es[0]
    for e in es[1:]:
        den = den + e
    @pl.when(pl.program_id(0) == 0)
    def _():
        cnt_ref[...] = jnp.zeros(cnt_ref.shape, F32)

    tm = logits.shape[0]
    hits = jnp.zeros(logits.shape, F32)
    for k in range(TOP_K):
        hits = jnp.where(col == ids[k], 1.0, hits)
    earlier = (lax.broadcasted_iota(I32, (tm, tm), 1)
               < lax.broadcasted_iota(I32, (tm, tm), 0))
    before = jnp.dot(jnp.where(earlier, 1.0, 0.0).astype(BF16), hits.astype(BF16),
                     preferred_element_type=F32) + cnt_ref[...]
    base_ref[...] = cnt_ref[...]
    cnt_ref[...] = cnt_ref[...] + jnp.sum(hits, axis=0, keepdims=True)
    counts_ref[...] = cnt_ref[...]
    te = jnp.zeros(logits.shape, F32)
    tg = jnp.zeros(logits.shape, F32)
    tr = jnp.zeros(logits.shape, F32)
    for k in range(TOP_K):
        rank = jnp.sum(jnp.where(col == ids[k], before, 0.0), axis=1, keepdims=True)
        te = jnp.where(col == float(k), ids[k], te)
        tg = jnp.where(col == float(k), es[k] / den, tg)
        tr = jnp.where(col == float(k), rank, tr)
    te_ref[...] = te.astype(I32)
    tg_ref[...] = tg
    tr_ref[...] = tr.astype(I32)


def _merge(o_cmp, o_sel, o_win, o_swa, gates, x2, expand, w_bn, w_bs, w_o, ln_g, ln_b,
           wr_hi, wr_lo, b_r, alpha, tm):
    t, d = x2.shape
    hq = o_cmp.shape[1]
    row = lambda i: (i, 0)
    tok = lambda n: pl.BlockSpec((tm, n), row)
    return pl.pallas_call(
        functools.partial(_merge_kernel, alpha=alpha),
        grid=(t // tm,),
        in_specs=[tok(hq), tok(hq), tok(hq), tok(hq), tok(gates.shape[1]), tok(d),
                  _full(expand.shape), _full(w_bn.shape), _full(w_bs.shape), _full(w_o.shape),
                  _full(ln_g.shape), _full(ln_b.shape), _full(wr_hi.shape), _full(wr_lo.shape),
                  _full(b_r.shape)],
        out_specs=[tok(d), tok(LANES), tok(LANES), tok(LANES), _full((1, LANES)),
                   pl.BlockSpec((None, 1, LANES), lambda i: (i, 0, 0))],
        out_shape=[jax.ShapeDtypeStruct((t, d), F32), jax.ShapeDtypeStruct((t, LANES), I32),
                   jax.ShapeDtypeStruct((t, LANES), F32), jax.ShapeDtypeStruct((t, LANES), I32),
                   jax.ShapeDtypeStruct((1, LANES), F32),
                   jax.ShapeDtypeStruct((t // tm, 1, LANES), F32)],
        scratch_shapes=[pltpu.VMEM((1, LANES), F32)],
        compiler_params=_params("arbitrary"),
        name="merge_ln_router",
    )(o_cmp, o_sel, o_win, o_swa, gates, x2, expand, w_bn, w_bs, w_o, ln_g, ln_b,
      wr_hi, wr_lo, b_r)


def _dispatch_kernel(pend_ref, padded_ref, cseg_ref, first_ref, len_ref, tot_ref,
                     h_ref, cpos_ref, xs_ref, comp0, comp1, zbuf, sem, zsem):
    j = pl.program_id(0)
    nt = pl.num_programs(0)
    tm = h_ref.shape[0]
    n_rows = xs_ref.shape[0]
    comps = (comp0, comp1)
    n_comp = comp0.shape[0]

    @pl.when(j == 0)
    def _():
        zbuf[...] = jnp.zeros(zbuf.shape, F32)
        used_rows = pend_ref[N_EXPERTS - 1]

        def zero_copy(start):
            start = pl.multiple_of(start, MOE_ROWS)
            return pltpu.make_async_copy(zbuf, xs_ref.at[pl.ds(start, MOE_ROWS), :], zsem)

        blocks = [(padded_ref[e] > 0, pend_ref[e] - MOE_ROWS) for e in range(N_EXPERTS)]
        blocks += [(used_rows + b * MOE_ROWS < n_rows, used_rows + b * MOE_ROWS)
                   for b in range(n_rows // MOE_ROWS - (nt * tm * TOP_K) // MOE_ROWS)]
        for cond, start in blocks:
            @pl.when(cond)
            def _(start=start):
                zero_copy(start).start()
        for cond, start in blocks:
            @pl.when(cond)
            def _(start=start):
                zero_copy(start).wait()

    def wait_writes(tile, slot):
        rows = pl.multiple_of(tot_ref[tile], F32_SUBLANES)
        pltpu.make_async_copy(comps[slot].at[pl.ds(0, rows), :], xs_ref.at[pl.ds(0, rows), :],
                              sem.at[slot]).wait()

    for slot in range(2):
        @pl.when((j >= 2) & ((j & 1) == slot))
        def _(slot=slot):
            wait_writes(j - 2, slot)

        @pl.when((j & 1) == slot)
        def _(slot=slot):
            row = lax.broadcasted_iota(I32, (n_comp, 1), 0)
            cpos = cpos_ref[...]
            sel = jnp.zeros((n_comp, tm), F32)
            for k in range(TOP_K):
                sel = jnp.where(row == cpos[k:k + 1, :], 1.0, sel)
            comps[slot][...] = jnp.dot(sel.astype(BF16), h_ref[...].astype(BF16),
                                       preferred_element_type=F32)
            for e in range(N_EXPERTS):
                ln = len_ref[j * N_EXPERTS + e]
                first = first_ref[j * N_EXPERTS + e]
                seg = cseg_ref[j * N_EXPERTS + e]
                for p in _run_sizes(tm):
                    done = ln & (-2 * p)
                    src = comps[slot].at[pl.ds(pl.multiple_of(seg + done, F32_SUBLANES), p), :]
                    dst = xs_ref.at[pl.ds(pl.multiple_of(first + done, F32_SUBLANES), p), :]
                    pl.when((ln & p) != 0)(pltpu.make_async_copy(src, dst, sem.at[slot]).start)

        @pl.when((j == nt - 1) & ((j & 1) == slot))
        def _(slot=slot):
            @pl.when(j >= 1)
            def _():
                wait_writes(j - 1, 1 - slot)
            wait_writes(j, slot)


def _dispatch(pend, padded, cseg, seg_first, seg_len, seg_tot, cpos_t, h, n_rows, tm):
    t, d = h.shape
    n_comp = -(-(tm * TOP_K + N_EXPERTS * (F32_SUBLANES - 1)) // MXU_DEPTH) * MXU_DEPTH
    grid_spec = pltpu.PrefetchScalarGridSpec(
        num_scalar_prefetch=6,
        grid=(t // tm,),
        in_specs=[pl.BlockSpec((tm, d), lambda i, *_: (i, 0)),
                  pl.BlockSpec((TOP_K, tm), lambda i, *_: (0, i))],
        out_specs=pl.BlockSpec(memory_space=pl.ANY),
        scratch_shapes=[pltpu.VMEM((n_comp, d), F32), pltpu.VMEM((n_comp, d), F32),
                        pltpu.VMEM((MOE_ROWS, d), F32), pltpu.SemaphoreType.DMA((2,)),
                        pltpu.SemaphoreType.DMA(())],
    )
    return pl.pallas_call(
        _dispatch_kernel,
        grid_spec=grid_spec,
        out_shape=jax.ShapeDtypeStruct((n_rows, d), F32),
        compiler_params=_params("arbitrary"),
        name="moe_dispatch",
    )(pend, padded, cseg, seg_first, seg_len, seg_tot, h, cpos_t)


def _moe_kernel(blk_e_ref, n_used_ref, first_ref, slot_ref, next_ref, x_ref, wi_hbm, bi_ref,
                wo_hbm, bo_ref, y_ref, wi_buf, wo_buf, wi_bf, wo_bf, wi_sem, wo_sem):
    f = wo_hbm.shape[1]
    b = pl.program_id(0)
    used = b < n_used_ref[0]
    e = blk_e_ref[b]
    slot = slot_ref[b]

    def fetch(expert, into):
        return (pltpu.make_async_copy(wi_hbm.at[expert], wi_buf.at[into], wi_sem.at[into]),
                pltpu.make_async_copy(wo_hbm.at[expert], wo_buf.at[into], wo_sem.at[into]))

    @pl.when(b == 0)
    def _():
        for cp in fetch(e, 0):
            cp.start()

    for s in range(2):
        @pl.when(used & (first_ref[b] == 1) & (slot == s))
        def _(s=s):
            @pl.when(next_ref[b] != e)
            def _():
                for cp in fetch(next_ref[b], 1 - s):
                    cp.start()
            for cp in fetch(e, s):
                cp.wait()
            wi_bf[...] = wi_buf[s].astype(BF16)
            wo_bf[...] = wo_buf[s].astype(BF16)

    @pl.when(used)
    def _():
        hdn = jnp.dot(x_ref[...].astype(BF16), wi_bf[...],
                      preferred_element_type=F32) + bi_ref[...]
        hg = jnp.minimum(hdn[:, :f], SWIGLU_LIMIT)
        hu = jnp.clip(hdn[:, f:], -SWIGLU_LIMIT, SWIGLU_LIMIT)
        act = hg * jax.nn.sigmoid(SWIGLU_ALPHA * hg) * (hu + 1.0)
        y_ref[...] = jnp.dot(act.astype(BF16), wo_bf[...],
                             preferred_element_type=F32) + bo_ref[...]

    @pl.when(jnp.logical_not(used))
    def _():
        y_ref[...] = jnp.zeros(y_ref.shape, F32)


def _moe_blocks(blk_e, n_used, padded, xs, w_in, b_in, w_out, b_out):
    n_rows, d = xs.shape
    e, _, f2 = w_in.shape
    f = w_out.shape[1]
    n_blk = n_rows // MOE_ROWS
    first = jnp.concatenate([jnp.ones((1,), I32), (blk_e[1:] != blk_e[:-1]).astype(I32)])
    ids = jnp.arange(e, dtype=I32)
    has_rows = padded > 0
    is_e = blk_e[:, None] == ids[None, :]
    slot = jnp.sum((ids[None, :] < blk_e[:, None]) & has_rows[None, :], axis=1) & 1
    later = jnp.where((ids[None, :] > ids[:, None]) & has_rows[None, :], ids[None, :], e)
    next_used = jnp.min(later, axis=1)
    next_used = jnp.where(next_used == e, ids, next_used).astype(I32)
    next_blk = jnp.sum(jnp.where(is_e, next_used[None, :], 0), axis=1).astype(I32)
    grid_spec = pltpu.PrefetchScalarGridSpec(
        num_scalar_prefetch=5,
        grid=(n_blk,),
        in_specs=[pl.BlockSpec((MOE_ROWS, d), lambda b, be, nu, *_: (jnp.minimum(b, nu[0] - 1), 0)),
                  pl.BlockSpec(memory_space=pl.ANY),
                  pl.BlockSpec((None, 1, f2), lambda b, be, *_: (be[b], 0, 0)),
                  pl.BlockSpec(memory_space=pl.ANY),
                  pl.BlockSpec((None, 1, d), lambda b, be, *_: (be[b], 0, 0))],
        out_specs=pl.BlockSpec((MOE_ROWS, d), lambda b, *_: (b, 0)),
        scratch_shapes=[pltpu.VMEM((2, d, f2), F32), pltpu.VMEM((2, f, d), F32),
                        pltpu.VMEM((d, f2), BF16), pltpu.VMEM((f, d), BF16),
                        pltpu.SemaphoreType.DMA((2,)), pltpu.SemaphoreType.DMA((2,))],
    )
    return pl.pallas_call(
        _moe_kernel,
        grid_spec=grid_spec,
        out_shape=jax.ShapeDtypeStruct((n_rows, d), F32),
        compiler_params=_params("arbitrary"),
        name="moe_experts",
    )(blk_e, n_used, first, slot.astype(I32), next_blk, xs, w_in, b_in, w_out, b_out)


def _run_sizes(tm):
    sizes, p = [], F32_SUBLANES
    while p <= tm:
        sizes.append(p)
        p *= 2
    return tuple(reversed(sizes))


def _stage_rows(tm):
    rows = tm * TOP_K + N_EXPERTS * 2 * (F32_SUBLANES - 1)
    return -(-rows // MXU_DEPTH) * MXU_DEPTH


def _final_kernel(start_ref, len_ref, seg_ref, tot_ref, h_ref, tg_ref, sp_ref, ys_ref, g_ref,
                  b_ref, o_ref, stage0, stage1, sem, *, alpha):
    i = pl.program_id(0)
    tm = h_ref.shape[0]
    stages = (stage0, stage1)
    n_stage = stage0.shape[0]

    def fetch(tile, into):
        for e in range(N_EXPERTS):
            ln = len_ref[tile * N_EXPERTS + e]
            first = start_ref[tile * N_EXPERTS + e]
            seg = seg_ref[tile * N_EXPERTS + e]
            for p in _run_sizes(tm):
                done = ln & (-2 * p)
                src = ys_ref.at[pl.ds(pl.multiple_of(first + done, F32_SUBLANES), p), :]
                dst = stages[into].at[pl.ds(pl.multiple_of(seg + done, F32_SUBLANES), p), :]
                pl.when((ln & p) != 0)(pltpu.make_async_copy(src, dst, sem.at[into]).start)

    @pl.when(i == 0)
    def _():
        stage0[...] = jnp.zeros(stage0.shape, F32)
        stage1[...] = jnp.zeros(stage1.shape, F32)
        fetch(0, 0)

    for slot in range(2):
        @pl.when((i & 1) == slot)
        def _(slot=slot):
            fetch(i + 1, 1 - slot)
            rows = pl.multiple_of(tot_ref[i], F32_SUBLANES)
            pltpu.make_async_copy(ys_ref.at[pl.ds(0, rows), :],
                                  stages[slot].at[pl.ds(0, rows), :], sem.at[slot]).wait()
            col = lax.broadcasted_iota(I32, (1, n_stage), 1)
            tg = tg_ref[...]
            sp = sp_ref[...]
            q = jnp.zeros((tm, n_stage), F32)
            for k in range(TOP_K):
                q = jnp.where(col == sp[:, k:k + 1], tg[:, k:k + 1], q)
            f = _split_dot(q, stages[slot][...].astype(BF16))
            o_ref[...] = _layer_norm(alpha * h_ref[...] + f, g_ref[...], b_ref[...])


def _final(run_start, run_len, run_seg, run_tot, h, top_g, spos, ys, ln_g, ln_b, alpha, tm):
    t, d = h.shape
    grid_spec = pltpu.PrefetchScalarGridSpec(
        num_scalar_prefetch=4,
        grid=(t // tm,),
        in_specs=[pl.BlockSpec((tm, d), lambda i, *_: (i, 0)),
                  pl.BlockSpec((tm, LANES), lambda i, *_: (i, 0)),
                  pl.BlockSpec((tm, TOP_K), lambda i, *_: (i, 0)),
                  pl.BlockSpec(memory_space=pl.ANY),
                  pl.BlockSpec(ln_g.shape, lambda i, *_: (0, 0)),
                  pl.BlockSpec(ln_b.shape, lambda i, *_: (0, 0))],
        out_specs=pl.BlockSpec((tm, d), lambda i, *_: (i, 0)),
        scratch_shapes=[pltpu.VMEM((_stage_rows(tm), d), F32),
                        pltpu.VMEM((_stage_rows(tm), d), F32), pltpu.SemaphoreType.DMA((2,))],
    )
    return pl.pallas_call(
        functools.partial(_final_kernel, alpha=alpha),
        grid_spec=grid_spec,
        out_shape=jax.ShapeDtypeStruct((t, d), F32),
        compiler_params=_params("arbitrary"),
        name="combine_ln",
    )(run_start, run_len, run_seg, run_tot, h, top_g, spos, ys, ln_g, ln_b)


def _rope_tables(s):
    half = HEAD_DIM // 2
    inv = ROPE_THETA ** (-np.arange(half, dtype=np.float64) / half)
    ang = np.arange(s, dtype=np.float64)[:, None] * inv[None, :]
    cos, sin = np.cos(ang).astype(np.float32), np.sin(ang).astype(np.float32)
    reps = LANES // HEAD_DIM
    cos_t = np.tile(np.concatenate([cos, cos], axis=1), (1, reps))
    sin_t = np.tile(np.concatenate([-sin, sin], axis=1), (1, reps))
    return jnp.asarray(cos_t), jnp.asarray(sin_t)


def _ceil_to(v, m):
    return (v + m - 1) // m * m


def _moe_plan(top_e, rank, tile_base, counts, t, tm, dt):
    a = t * TOP_K
    per, nd = dt // tm, t // dt
    d_base = tile_base[::per]
    d_cnt = jnp.concatenate([d_base[1:], counts[None, :]], axis=0) - d_base
    seg_len = _ceil_to(d_cnt, F32_SUBLANES)
    padded = _ceil_to(jnp.sum(seg_len, axis=0), MOE_ROWS)
    pend = jnp.cumsum(padded)
    seg_first = (pend - padded)[None, :] + jnp.cumsum(seg_len, axis=0) - seg_len
    cseg = jnp.cumsum(seg_len, axis=1) - seg_len
    onehot = (top_e[:, :, None] == jnp.arange(N_EXPERTS, dtype=I32)).reshape(
        nd, dt, TOP_K, N_EXPERTS)

    def pick(table):
        return jnp.sum(jnp.where(onehot, table[:, None, None, :], 0), axis=3).reshape(t, TOP_K)

    local = rank - pick(d_base)
    dest = pick(seg_first) + local
    cpos = pick(cseg) + local
    run_first = jnp.repeat(seg_first - d_base, per, axis=0) + tile_base
    n_blk = -(-(a + nd * N_EXPERTS * (F32_SUBLANES - 1)) // MOE_ROWS) + N_EXPERTS
    blk_first = jnp.arange(n_blk, dtype=I32) * MOE_ROWS
    blk_e = jnp.minimum(jnp.sum(pend[None, :] <= blk_first[:, None], axis=1),
                        N_EXPERTS - 1).astype(I32)
    n_used = (pend[-1] // MOE_ROWS).astype(I32).reshape(1)
    flat = lambda v: v.astype(I32).reshape(-1)
    segments = (flat(cseg), flat(seg_first), flat(seg_len), jnp.sum(seg_len, axis=1).astype(I32))
    return (segments, dest.astype(I32), cpos.astype(I32), run_first, blk_e, n_used,
            n_blk * MOE_ROWS, pend.astype(I32), padded.astype(I32))


def _combine_plan(top_e, dest, run_first, tile_base, counts, tm):
    cnt = jnp.concatenate([tile_base[1:], counts[None, :]], axis=0) - tile_base
    lead = run_first & (F32_SUBLANES - 1)
    run_len = jnp.where(cnt > 0, _ceil_to(lead + cnt, F32_SUBLANES), 0)
    run_seg = jnp.cumsum(run_len, axis=1) - run_len
    shift = (run_seg + lead - run_first)[:, None, None, :]
    onehot = (top_e[:, :, None] == jnp.arange(N_EXPERTS, dtype=I32)).reshape(
        -1, tm, TOP_K, N_EXPERTS)
    spos = jnp.sum(jnp.where(onehot, shift, 0), axis=3).reshape(dest.shape) + dest
    flat = lambda v: jnp.pad(v.astype(I32), ((0, 1), (0, 0))).reshape(-1)
    run_tot = jnp.sum(run_len, axis=1).astype(I32)
    return flat(run_first - lead), flat(run_len), flat(run_seg), run_tot, spos.astype(I32)


def _layer(x, w_in, k_pe, k_w1, k_w2, v_pe, v_w1, v_w2, sinks, w_br_nsa, w_br_swa, w_out,
           ln1_g, ln1_b, w_router, b_router, w_e_in, b_e_in, w_e_out, b_e_out, ln2_g, ln2_b,
           alpha):
    b, s, d = x.shape
    t = b * s
    qb = Q_BLOCK
    nq_n, nkv = NSA_HEADS * HEAD_DIM, NSA_KV * HEAD_DIM
    nq_s, nkv_s = SWA_HEADS * HEAD_DIM, SWA_KV * HEAD_DIM
    widths = (nq_n, nkv, nkv, nkv, nkv, nkv, nkv, NSA_HEADS * 3, nq_s, nkv_s, nkv_s, 2 * d)
    offs = [0]
    for w in widths:
        offs.append(offs[-1] + w)
    col = lambda j: w_in[:, offs[j]:offs[j + 1]]
    (c_qn, c_kc, c_vc, c_ks, c_vs, c_kw, c_vw, c_gn, c_qs, c_k_s, c_v_s, c_gm) = map(col, range(12))
    w_rope = jnp.concatenate([c_qn, c_qs, c_ks, c_kw, c_k_s], axis=1).astype(BF16)
    w_plain = jnp.concatenate([c_kc, c_vc, c_vs, c_vw, c_v_s], axis=1).astype(BF16)
    gn_pad = LANES - NSA_HEADS * 3
    w_gate = jnp.concatenate([c_gm, c_gn, jnp.zeros((d, gn_pad), F32)], axis=1).astype(BF16)
    cos_t, sin_t = _rope_tables(s)

    x2 = x.reshape(t, d)
    qn_rot, qs_rot, kk_rot, qn_raw, plain, gates = _project(
        x2, w_rope, w_plain, w_gate, cos_t, sin_t, s, min(256, s))

    nc = (s - CMP_BLOCK) // CMP_STRIDE + 1
    ncp = s // CMP_STRIDE
    half = CMP_STRIDE * HEAD_DIM

    def halves(cols):
        v = cols.reshape(b, s, NSA_KV, HEAD_DIM).transpose(0, 2, 1, 3)
        return v.reshape(b * NSA_KV, ncp, half)

    t2 = jnp.stack([halves(plain[:, 0:nkv]), halves(plain[:, nkv:2 * nkv])])
    t_lo = t2
    t_hi = jnp.concatenate([t2[:, :, 1:], jnp.zeros_like(t2[:, :, :1])], axis=2)
    pe2 = jnp.stack([k_pe.reshape(2, half), v_pe.reshape(2, half)])
    w1 = jnp.stack([k_w1, v_w1]).astype(BF16)
    w2 = jnp.stack([k_w2, v_w2]).astype(BF16)
    kvc = _compress(t_lo, t_hi, pe2, w1, w2).reshape(2, b, NSA_KV, ncp, HEAD_DIM)

    nsel = s // SEL_BLOCK
    nselp = -(-nsel // LANES) * LANES
    cstart = np.arange(ncp) * CMP_STRIDE
    sstart = np.arange(nselp) * SEL_BLOCK
    overlap = ((cstart[:, None] < sstart[None, :] + SEL_BLOCK)
               & (cstart[:, None] + CMP_BLOCK > sstart[None, :])
               & (np.arange(ncp)[:, None] < nc) & (np.arange(nselp)[None, :] < nsel))
    o_cmp, notsel = _cmp_attention(qn_raw.reshape(b, s, nq_n), kvc[0], kvc[1],
                                   jnp.asarray(overlap.T.astype(BF16)),
                                   min(CMP_STEP_QUERIES, s))

    def group_major(cols):
        return cols.reshape(b, s, NSA_KV, HEAD_DIM).transpose(0, 2, 1, 3)

    k_sel = group_major(kk_rot[:, 0:nkv])
    onehot = (np.arange(s)[:, None] // SEL_BLOCK == np.arange(nselp)[None, :])
    k_tail = np.concatenate([np.zeros((s, LANES - HEAD_DIM), np.float32),
                             np.where(onehot, SEL_PENALTY, 0.0).astype(np.float32)], axis=1)
    k_aug = jnp.concatenate(
        [k_sel, jnp.broadcast_to(jnp.asarray(k_tail.astype(BF16)), (b, NSA_KV) + k_tail.shape)],
        axis=3)
    vt_all = plain.reshape(b, s, -1)[:, :, 2 * nkv:].transpose(0, 2, 1)
    o_sel = _sel_attention(qn_rot.reshape(b, s, nq_n), notsel, k_aug, vt_all,
                           min(SEL_STEP_QUERIES, s), min(SEL_KEY_TILE, s))

    kk3 = kk_rot.reshape(b, s, -1)
    o_win = _band_attention(qn_rot.reshape(b, s, nq_n), kk3, 1, vt_all, 1, nkv,
                            NSA_WINDOW, None, qb)
    o_swa = _band_attention(qs_rot.reshape(b, s, nq_s), kk3, 2, vt_all, 2, nkv_s,
                            SWA_WINDOW, sinks, qb)

    gi = np.arange(LANES)
    ci = np.arange(3 * nq_n)
    expand = jnp.asarray(((gi[:, None] // 3 == (ci[None, :] % nq_n) // HEAD_DIM)
                          & (gi[:, None] % 3 == ci[None, :] // nq_n)
                          & (gi[:, None] < NSA_HEADS * 3)).astype(BF16))
    wr_pad = jnp.pad(w_router, ((0, 0), (0, LANES - N_EXPERTS)))
    wr_hi = wr_pad.astype(BF16)
    wr_lo = (wr_pad - wr_hi.astype(F32)).astype(BF16)
    b_r = jnp.concatenate([b_router, jnp.full((LANES - N_EXPERTS,), -jnp.inf, F32)]).reshape(1, LANES)
    tm = min(256, t)
    h, top_e, top_g, rank, counts, tile_base = _merge(
        o_cmp.reshape(t, nq_n), o_sel.reshape(t, nq_n), o_win.reshape(t, nq_n),
        o_swa.reshape(t, nq_s), gates, x2, expand, w_br_nsa.astype(BF16), w_br_swa.astype(BF16),
        w_out.astype(BF16), ln1_g.reshape(1, d), ln1_b.reshape(1, d), wr_hi, wr_lo, b_r,
        alpha, tm)

    top_e, rank = top_e[:, :TOP_K], rank[:, :TOP_K]
    counts = counts[0, :N_EXPERTS].astype(I32)
    tile_base = tile_base[:, 0, :N_EXPERTS].astype(I32)
    dt = min(DISPATCH_TOKENS, t)
    segments, dest, cpos, run_first, blk_e, n_used, n_rows, pend, padded = _moe_plan(
        top_e, rank, tile_base, counts, t, tm, dt)
    xs = _dispatch(pend, padded, *segments, cpos.T, h, n_rows, dt)
    ys = _moe_blocks(blk_e, n_used, padded, xs, w_e_in, b_e_in.reshape(N_EXPERTS, 1, -1),
                     w_e_out, b_e_out.reshape(N_EXPERTS, 1, -1))
    run_start, run_len, run_seg, run_tot, spos = _combine_plan(
        top_e, dest, run_first, tile_base, counts, tm)
    out = _final(run_start, run_len, run_seg, run_tot, h, top_g, spos, ys, ln2_g.reshape(1, d),
                 ln2_b.reshape(1, d), alpha, tm)
    return out.reshape(b, s, d)


def kernel(x, w_in, nsa_k_pe, nsa_k_w1, nsa_k_w2, nsa_v_pe, nsa_v_w1, nsa_v_w2, swa_sinks, w_br_nsa, w_br_swa, w_out, ln1_g, ln1_b, w_router, b_router, w_expert_in, b_expert_in, w_expert_out, b_expert_out, ln2_g, ln2_b):
    depth = w_in.shape[0]
    alpha = (2.0 * depth) ** 0.25
    for l in range(depth):
        x = _layer(x, w_in[l], nsa_k_pe[l], nsa_k_w1[l], nsa_k_w2[l], nsa_v_pe[l], nsa_v_w1[l],
                   nsa_v_w2[l], swa_sinks[l], w_br_nsa[l], w_br_swa[l], w_out[l], ln1_g[l],
                   ln1_b[l], w_router[l], b_router[l], w_expert_in[l], b_expert_in[l],
                   w_expert_out[l], b_expert_out[l], ln2_g[l], ln2_b[l], alpha)
    return x
```

```python
import functools

import jax
import jax.numpy as jnp
import numpy as np
from jax import lax
from jax.experimental import pallas as pl
from jax.experimental.pallas import tpu as pltpu

BF16 = jnp.bfloat16
F32 = jnp.float32
I32 = jnp.int32

HEAD_DIM = 64
NSA_HEADS = 8
NSA_KV = 2
CMP_BLOCK = 32
CMP_STRIDE = 16
SEL_BLOCK = 64
SEL_TOPN = 16
NSA_WINDOW = 512
SWA_HEADS = 8
SWA_KV = 2
SWA_WINDOW = 128
Q_BLOCK = 128
ROPE_THETA = 10000.0
N_EXPERTS = 32
TOP_K = 4
SWIGLU_LIMIT = 7.0
SWIGLU_ALPHA = 1.702
LN_EPS = 1e-5

LANES = 128
BF16_SUBLANES = 16
F32_SUBLANES = 8
MXU_DEPTH = 256
MASKED = -1e30
M_INIT = -1e29
SEL_PENALTY = -(2.0 ** 100)
VMEM_LIMIT = 52 * 1024 * 1024
MOE_ROWS = 256
DISPATCH_TOKENS = 512
MERGE_STEP_TOKENS = 512
BAND_STEP_QUERIES = 512
CMP_STEP_QUERIES = 512
SEL_STEP_QUERIES = 128
SEL_KEY_TILE = 512

R_NSA = NSA_HEADS // NSA_KV
R_SWA = SWA_HEADS // SWA_KV
NT_DIMS = (((1,), (1,)), ((), ()))


def _params(*sem):
    return pltpu.CompilerParams(dimension_semantics=sem, vmem_limit_bytes=VMEM_LIMIT)


def _full(shape):
    n = len(shape)
    return pl.BlockSpec(shape, lambda *_: (0,) * n)


def _proj_kernel(x_ref, wr_ref, wp_ref, wg_ref, cos_ref, sin_ref,
                 qn_rot_ref, qs_rot_ref, kk_rot_ref, qn_raw_ref, plain_ref, gates_ref):
    xb = x_ref[...].astype(BF16)
    acc = jnp.dot(xb, wr_ref[...], preferred_element_type=F32)
    cos = cos_ref[...]
    sin = sin_ref[...]
    lane = lax.broadcasted_iota(I32, cos.shape, 1)
    first_half = (lane & (HEAD_DIM - 1)) < HEAD_DIM // 2

    def rope(t):
        partner = jnp.where(first_half, pltpu.roll(t, LANES - HEAD_DIM // 2, 1),
                            pltpu.roll(t, HEAD_DIM // 2, 1))
        return (t * cos + partner * sin).astype(BF16)

    nq = qn_rot_ref.shape[1] // LANES
    ns = qs_rot_ref.shape[1] // LANES
    nk = kk_rot_ref.shape[1] // LANES
    for c in range(nq):
        qn_rot_ref[:, c * LANES:(c + 1) * LANES] = rope(acc[:, c * LANES:(c + 1) * LANES])
    for c in range(ns):
        o = (nq + c) * LANES
        qs_rot_ref[:, c * LANES:(c + 1) * LANES] = rope(acc[:, o:o + LANES])
    for c in range(nk):
        o = (nq + ns + c) * LANES
        kk_rot_ref[:, c * LANES:(c + 1) * LANES] = rope(acc[:, o:o + LANES])
    qn_raw_ref[...] = acc[:, :nq * LANES].astype(BF16)
    plain_ref[...] = jnp.dot(xb, wp_ref[...], preferred_element_type=F32).astype(BF16)
    gates_ref[...] = jnp.dot(xb, wg_ref[...], preferred_element_type=F32)


def _project(x2, w_rope, w_plain, w_gate, cos_t, sin_t, seq, tm):
    t, d = x2.shape
    nr, npl, ng = w_rope.shape[1], w_plain.shape[1], w_gate.shape[1]
    nqn, nqs = NSA_HEADS * HEAD_DIM, SWA_HEADS * HEAD_DIM
    nkk = nr - nqn - nqs
    spb = seq // tm
    row = lambda i: (i, 0)
    return pl.pallas_call(
        _proj_kernel,
        grid=(t // tm,),
        in_specs=[pl.BlockSpec((tm, d), row), _full(w_rope.shape), _full(w_plain.shape),
                  _full(w_gate.shape),
                  pl.BlockSpec((tm, LANES), lambda i: (i % spb, 0)),
                  pl.BlockSpec((tm, LANES), lambda i: (i % spb, 0))],
        out_specs=[pl.BlockSpec((tm, nqn), row), pl.BlockSpec((tm, nqs), row),
                   pl.BlockSpec((tm, nkk), row), pl.BlockSpec((tm, nqn), row),
                   pl.BlockSpec((tm, npl), row), pl.BlockSpec((tm, ng), row)],
        out_shape=[jax.ShapeDtypeStruct((t, nqn), BF16), jax.ShapeDtypeStruct((t, nqs), BF16),
                   jax.ShapeDtypeStruct((t, nkk), BF16), jax.ShapeDtypeStruct((t, nqn), BF16),
                   jax.ShapeDtypeStruct((t, npl), BF16), jax.ShapeDtypeStruct((t, ng), F32)],
        compiler_params=_params("parallel"),
        name="proj",
    )(x2, w_rope, w_plain, w_gate, cos_t, sin_t)


def _compress_kernel(a_ref, b_ref, pe_ref, w1_ref, w2_ref, out_ref):
    half = a_ref.shape[1]
    a = (a_ref[...].astype(F32) + pe_ref[0:1, :]).astype(BF16)
    b = (b_ref[...].astype(F32) + pe_ref[1:2, :]).astype(BF16)
    hid = jnp.dot(a, w1_ref[0:half, :], preferred_element_type=F32)
    hid = hid + jnp.dot(b, w1_ref[half:2 * half, :], preferred_element_type=F32)
    act = jax.nn.gelu(hid).astype(BF16)
    out_ref[...] = jnp.dot(act, w2_ref[...], preferred_element_type=F32).astype(BF16)


def _compress(t_lo, t_hi, pe2, w1, w2):
    two, bg, ncp, half = t_lo.shape
    hid = w1.shape[2]
    blk = lambda shape: pl.BlockSpec((None, None) + shape, lambda j, i: (j, i, 0, 0))
    wsp = lambda shape: pl.BlockSpec((None,) + shape, lambda j, i: (j, 0, 0))
    return pl.pallas_call(
        _compress_kernel,
        grid=(two, bg),
        in_specs=[blk((ncp, half)), blk((ncp, half)), wsp((2, half)), wsp((2 * half, hid)),
                  wsp((hid, HEAD_DIM))],
        out_specs=blk((ncp, HEAD_DIM)),
        out_shape=jax.ShapeDtypeStruct((two, bg, ncp, HEAD_DIM), BF16),
        compiler_params=_params("parallel", "parallel"),
        name="compress",
    )(t_lo, t_hi, pe2, w1, w2)


def _stack_heads(q, g, r):
    return jnp.concatenate(
        [q[:, (g * r + j) * HEAD_DIM:(g * r + j + 1) * HEAD_DIM] for j in range(r)], axis=0)


def _unstack_heads_t(parts, r, qb):
    blocks = []
    for o in parts:
        for j in range(0, r, 2):
            pair = jnp.concatenate([o[:, j * qb:(j + 1) * qb], o[:, (j + 1) * qb:(j + 2) * qb]],
                                   axis=0)
            blocks.append(pair.T)
    return jnp.concatenate(blocks, axis=1)


def _topk_mask_cols(vals, k):
    n = vals.shape[0]
    row = lax.broadcasted_iota(I32, vals.shape, 0).astype(F32)
    taken = jnp.zeros(vals.shape, F32)
    work = vals
    for _ in range(k):
        mx = jnp.max(work, axis=0, keepdims=True)
        first = jnp.min(jnp.where(work == mx, row, float(n)), axis=0, keepdims=True)
        pick = row == first
        taken = jnp.where(pick, 1.0, taken)
        work = jnp.where(pick, -jnp.inf, work)
    return taken > 0.5


def _cmp_kernel(q_ref, kc_ref, vct_ref, ovt_ref, o_ref, notsel_ref):
    qb = q_ref.shape[0]
    ncp = kc_ref.shape[1]
    nselp = ovt_ref.shape[0]
    i = pl.program_id(1)
    rows = R_NSA * qb
    pos = i * qb + (lax.broadcasted_iota(I32, (1, rows), 1) & (qb - 1))
    cend = lax.broadcasted_iota(I32, (ncp, 1), 0) * CMP_STRIDE + (CMP_BLOCK - 1)
    bias = jnp.where(cend <= pos, 0.0, MASKED)
    live = (pos >= CMP_BLOCK - 1).astype(F32)
    q = q_ref[...]
    outs, imps = [], []
    for g in range(NSA_KV):
        qg = _stack_heads(q, g, R_NSA) * (HEAD_DIM ** -0.5)
        st = lax.dot_general(kc_ref[g], qg, NT_DIMS, preferred_element_type=F32)
        st = st + bias
        e = jnp.exp(st - jnp.max(st, axis=0, keepdims=True))
        pt = e * (live / jnp.sum(e, axis=0, keepdims=True))
        outs.append(jnp.dot(vct_ref[g], pt.astype(BF16), preferred_element_type=F32))
        psum = pt[:, 0:qb]
        for j in range(1, R_NSA):
            psum = psum + pt[:, j * qb:(j + 1) * qb]
        p_hi = psum.astype(BF16)
        p_lo = (psum - p_hi.astype(F32)).astype(BF16)
        imps.append(jnp.dot(ovt_ref[...], p_hi, preferred_element_type=F32)
                    + jnp.dot(ovt_ref[...], p_lo, preferred_element_type=F32))
    imp = jnp.concatenate(imps, axis=1)
    lane = lax.broadcasted_iota(I32, (1, NSA_KV * qb), 1)
    cur = (i * qb + (lane & (qb - 1))) >> 6
    jb = lax.broadcasted_iota(I32, (nselp, 1), 0)
    forced = (jb == 0) | (jb == cur) | (jb == cur - 1)
    imp = jnp.where(jb > cur, -1.0, jnp.where(forced, 1e6, imp))
    notsel = jnp.where(_topk_mask_cols(imp, SEL_TOPN), 0.0, 1.0)
    for g in range(NSA_KV):
        notsel_ref[g] = notsel[:, g * qb:(g + 1) * qb].T.astype(BF16)
    o_ref[...] = _unstack_heads_t(outs, R_NSA, qb)


def _cmp_attention(q_raw, kc, vc, overlap_t, qb):
    vc = vc.transpose(0, 1, 3, 2)
    b, s, hq = q_raw.shape
    _, g, ncp, dh = kc.shape
    nselp = overlap_t.shape[0]
    return pl.pallas_call(
        _cmp_kernel,
        grid=(b, s // qb),
        in_specs=[pl.BlockSpec((None, qb, hq), lambda bi, i: (bi, i, 0)),
                  pl.BlockSpec((None, g, ncp, dh), lambda bi, i: (bi, 0, 0, 0)),
                  pl.BlockSpec((None, g, dh, ncp), lambda bi, i: (bi, 0, 0, 0)),
                  _full(overlap_t.shape)],
        out_specs=[pl.BlockSpec((None, qb, hq), lambda bi, i: (bi, i, 0)),
                   pl.BlockSpec((None, g, qb, nselp), lambda bi, i: (bi, 0, i, 0))],
        out_shape=[jax.ShapeDtypeStruct((b, s, hq), F32),
                   jax.ShapeDtypeStruct((b, g, s, nselp), BF16)],
        compiler_params=_params("parallel", "parallel"),
        name="cmp_attn",
    )(q_raw, kc, vc, overlap_t)


def _sel_kernel(q_ref, notsel_ref, k_ref, vt_ref, o_ref, *score_bufs, tk):
    s_even, s_odd = score_bufs[:NSA_KV], score_bufs[NSA_KV:]
    qb = q_ref.shape[0]
    i = pl.program_id(1)
    rows = R_NSA * qb
    qpos = i * qb + (lax.broadcasted_iota(I32, (1, rows), 1) & (qb - 1))
    n_clear = (i * qb) // tk
    q = q_ref[...]
    q_augs = []
    for g in range(NSA_KV):
        qg = _stack_heads(q, g, R_NSA) * (HEAD_DIM ** -0.5)
        q_augs.append(jnp.concatenate(
            [qg, jnp.zeros((rows, LANES - HEAD_DIM), BF16),
             jnp.concatenate([notsel_ref[g]] * R_NSA, axis=0)], axis=1))

    def scores(kt, g):
        start = pl.multiple_of(kt * tk, tk)
        return lax.dot_general(k_ref[g, pl.ds(start, tk), :], q_augs[g], NT_DIMS,
                               preferred_element_type=F32)

    def consume(kt, g, s_ref, m, acc, causal):
        start = pl.multiple_of(kt * tk, tk)
        vt_t = jnp.concatenate([vt_ref[g * HEAD_DIM:(g + 1) * HEAD_DIM, pl.ds(start, tk)],
                                jnp.ones((BF16_SUBLANES, tk), vt_ref.dtype)], axis=0)
        st = s_ref[...]
        if causal:
            kpos = start + lax.broadcasted_iota(I32, (tk, 1), 0)
            st = jnp.where(kpos <= qpos, st, MASKED)
        m_new = jnp.maximum(m, jnp.max(st, axis=0, keepdims=True))
        pt = jnp.exp(st - m_new).astype(BF16)
        acc = jnp.exp(m - m_new) * acc + jnp.dot(vt_t, pt, preferred_element_type=F32)
        return m_new, acc

    def advance(kt, carry, cur, nxt):
        new = []
        for g in range(NSA_KV):
            nxt[g][...] = scores(kt + 1, g)
            new.append(consume(kt, g, cur[g], *carry[g], False))
        return tuple(new)

    def pair(j, carry):
        carry = advance(2 * j, carry, s_even, s_odd)
        return advance(2 * j + 1, carry, s_odd, s_even)

    def finish(carry, cur):
        outs = []
        for g in range(NSA_KV):
            _, acc = consume(n_clear, g, cur[g], *carry[g], True)
            outs.append(acc[:HEAD_DIM] / acc[HEAD_DIM:HEAD_DIM + 1])
        o_ref[...] = _unstack_heads_t(outs, R_NSA, qb)

    for g in range(NSA_KV):
        s_even[g][...] = scores(0, g)
    init = tuple((jnp.full((1, rows), M_INIT, F32),
                  jnp.zeros((HEAD_DIM + BF16_SUBLANES, rows), F32))
                 for _ in range(NSA_KV))
    carry = lax.fori_loop(0, n_clear // 2, pair, init)

    @pl.when((n_clear & 1) == 0)
    def _():
        finish(carry, s_even)

    @pl.when((n_clear & 1) == 1)
    def _():
        finish(advance(n_clear - 1, carry, s_even, s_odd), s_odd)


def _sel_attention(q_rot, notsel, k_aug, vt_aug, qb, tk):
    b, s, hq = q_rot.shape
    _, g, _, kw = k_aug.shape
    nselp = notsel.shape[3]
    return pl.pallas_call(
        functools.partial(_sel_kernel, tk=tk),
        grid=(b, s // qb),
        in_specs=[pl.BlockSpec((None, qb, hq), lambda bi, i: (bi, i, 0)),
                  pl.BlockSpec((None, g, qb, nselp), lambda bi, i: (bi, 0, i, 0)),
                  pl.BlockSpec((None, g, s, kw), lambda bi, i: (bi, 0, 0, 0)),
                  pl.BlockSpec((None, g * HEAD_DIM, s), lambda bi, i: (bi, 0, 0))],
        out_specs=pl.BlockSpec((None, qb, hq), lambda bi, i: (bi, i, 0)),
        out_shape=jax.ShapeDtypeStruct((b, s, hq), F32),
        scratch_shapes=[pltpu.VMEM((tk, R_NSA * qb), F32)] * (2 * g),
        compiler_params=_params("parallel", "parallel"),
        name="sel_attn",
    )(q_rot, notsel, k_aug, vt_aug)


def _band_kernel(*refs, window, wlen, r, kv, qb, has_sinks):
    if has_sinks:
        sink_ref, q_ref, k_ref, vt_ref, o_ref = refs
    else:
        q_ref, k_ref, vt_ref, o_ref = refs
    nsub = q_ref.shape[0] // qb
    rows = r * qb
    lane = lax.broadcasted_iota(I32, (1, rows), 1)

    def mask_bias(i):
        start = jnp.maximum((i + 1) * qb - wlen, 0)
        rel = i * qb + (lane & (qb - 1)) - start - lax.broadcasted_iota(I32, (wlen, 1), 0)
        return jnp.where((rel >= 0) & (rel < window), 0.0, MASKED)

    def body(shared_bias):
        first = pl.program_id(1) * nsub
        if shared_bias:
            bias = mask_bias(first)
        for sb in range(nsub):
            i = first + sb
            if not shared_bias:
                bias = mask_bias(i)
            start = pl.multiple_of(jnp.maximum((i + 1) * qb - wlen, 0), qb)
            q = q_ref[sb * qb:(sb + 1) * qb, :]
            kw = k_ref[pl.ds(start, wlen), :]
            vtw = vt_ref[:, pl.ds(start, wlen)]
            outs = []
            for g in range(kv):
                qg = _stack_heads(q, g, r) * (HEAD_DIM ** -0.5)
                st = lax.dot_general(kw[:, g * HEAD_DIM:(g + 1) * HEAD_DIM], qg, NT_DIMS,
                                     preferred_element_type=F32) + bias
                m = jnp.max(st, axis=0, keepdims=True)
                if has_sinks:
                    sk = jnp.full((1, rows), sink_ref[g * r], F32)
                    for j in range(1, r):
                        sk = jnp.where(lane >= j * qb, sink_ref[g * r + j], sk)
                    m = jnp.maximum(m, sk)
                e = jnp.exp(st - m).astype(BF16)
                v_ones = jnp.concatenate([vtw[g * HEAD_DIM:(g + 1) * HEAD_DIM, :],
                                          jnp.ones((BF16_SUBLANES, wlen), vtw.dtype)], axis=0)
                ot = jnp.dot(v_ones, e, preferred_element_type=F32)
                den = ot[HEAD_DIM:HEAD_DIM + 1]
                if has_sinks:
                    den = den + jnp.exp(sk - m)
                outs.append(ot[:HEAD_DIM] / den)
            o_ref[sb * qb:(sb + 1) * qb, :] = _unstack_heads_t(outs, r, qb)

    if nsub * qb >= wlen - qb:
        @pl.when(pl.program_id(1) == 0)
        def _():
            body(False)

        @pl.when(pl.program_id(1) > 0)
        def _():
            body(True)
    else:
        body(False)


def _band_attention(q_rot, k_all, k_blk, vt_all, v_blk, gk, window, sinks, qb):
    b, s, hq = q_rot.shape
    assert k_all.shape[2] % gk == 0 and vt_all.shape[1] % gk == 0
    kv = gk // HEAD_DIM
    r = hq // gk
    back = -(-window // qb)
    wlen = (back + 1) * qb
    assert wlen <= s
    has_sinks = sinks is not None
    tq = min(BAND_STEP_QUERIES, s)
    in_specs = [pl.BlockSpec((None, tq, hq), lambda bi, i: (bi, i, 0)),
                pl.BlockSpec((None, s, gk), lambda bi, i: (bi, 0, k_blk)),
                pl.BlockSpec((None, gk, s), lambda bi, i: (bi, v_blk, 0))]
    args = [q_rot, k_all, vt_all]
    if has_sinks:
        in_specs = [pl.BlockSpec(memory_space=pltpu.SMEM)] + in_specs
        args = [sinks.astype(F32)] + args
    return pl.pallas_call(
        functools.partial(_band_kernel, window=window, wlen=wlen, r=r, kv=kv, qb=qb,
                          has_sinks=has_sinks),
        grid=(b, s // tq),
        in_specs=in_specs,
        out_specs=pl.BlockSpec((None, tq, hq), lambda bi, i: (bi, i, 0)),
        out_shape=jax.ShapeDtypeStruct((b, s, hq), F32),
        compiler_params=_params("parallel", "parallel"),
        name="band_attn_sink" if has_sinks else "band_attn",
    )(*args)


def _layer_norm(v, g, b):
    mu = jnp.mean(v, axis=1, keepdims=True)
    c = v - mu
    var = jnp.mean(c * c, axis=1, keepdims=True)
    return c * lax.rsqrt(var + LN_EPS) * g + b


def _split_dot(a, w):
    hi = a.astype(BF16)
    lo = (a - hi.astype(F32)).astype(BF16)
    return (jnp.dot(hi, w, preferred_element_type=F32)
            + jnp.dot(lo, w, preferred_element_type=F32))


def _merge_kernel(ocmp_ref, osel_ref, owin_ref, oswa_ref, gates_ref, x_ref, exp_ref,
                  wbn_ref, wbs_ref, wo_ref, lng_ref, lnb_ref, wrh_ref, wrl_ref, br_ref,
                  h_ref, te_ref, tg_ref, tr_ref, counts_ref, base_ref, cnt_ref, *, alpha, sub):
    d = x_ref.shape[1]
    hq = ocmp_ref.shape[1]
    n_sub = x_ref.shape[0] // sub
    col = lax.broadcasted_iota(I32, (sub, LANES), 1).astype(F32)

    def route(rows):
        gates = gates_ref[rows, :]
        gn = jax.nn.sigmoid(gates[:, 2 * d:])
        gexp = _split_dot(gn, exp_ref[...])
        o_nsa = (gexp[:, 0:hq] * ocmp_ref[rows, :] + gexp[:, hq:2 * hq] * osel_ref[rows, :]
                 + gexp[:, 2 * hq:3 * hq] * owin_ref[rows, :])
        y_nsa = jnp.dot(o_nsa.astype(BF16), wbn_ref[...], preferred_element_type=F32)
        y_swa = jnp.dot(oswa_ref[rows, :].astype(BF16), wbs_ref[...],
                        preferred_element_type=F32)
        gm = jax.nn.sigmoid(gates[:, :2 * d])
        mixed = gm[:, :d] * y_nsa + gm[:, d:] * y_swa
        z = jnp.dot(mixed.astype(BF16), wo_ref[...], preferred_element_type=F32)
        h = _layer_norm(alpha * x_ref[rows, :] + z, lng_ref[...], lnb_ref[...])
        h_ref[rows, :] = h
        h_hi = h.astype(BF16)
        h_lo = (h - h_hi.astype(F32)).astype(BF16)
        logits = (jnp.dot(h_hi, wrh_ref[...], preferred_element_type=F32)
                  + jnp.dot(h_lo, wrh_ref[...], preferred_element_type=F32)
                  + jnp.dot(h_hi, wrl_ref[...], preferred_element_type=F32)) + br_ref[...]
        work = logits
        vals, ids = [], []
        for _ in range(TOP_K):
            mx = jnp.max(work, axis=1, keepdims=True)
            first = jnp.min(jnp.where(work == mx, col, float(LANES)), axis=1, keepdims=True)
            vals.append(mx)
            ids.append(first)
            work = jnp.where(col == first, -jnp.inf, work)
        es = [jnp.exp(v - vals[0]) for v in vals]
        den = es[0]
        for e in es[1:]:
            den = den + e
        hits = jnp.zeros(logits.shape, F32)
        te = jnp.zeros(logits.shape, F32)
        tg = jnp.zeros(logits.shape, F32)
        for k in range(TOP_K):
            hits = jnp.where(col == ids[k], 1.0, hits)
            te = jnp.where(col == float(k), ids[k], te)
            tg = jnp.where(col == float(k), es[k] / den, tg)
        te_ref[rows, :] = te.astype(I32)
        tg_ref[rows, :] = tg
        earlier = (lax.broadcasted_iota(I32, (sub, sub), 1)
                   < lax.broadcasted_iota(I32, (sub, sub), 0))
        prefix = jnp.dot(jnp.where(earlier, 1.0, 0.0).astype(BF16), hits.astype(BF16),
                         preferred_element_type=F32)
        return ids, prefix, jnp.sum(hits, axis=0, keepdims=True)

    routed = [route(slice(j * sub, (j + 1) * sub)) for j in range(n_sub)]

    @pl.when(pl.program_id(0) == 0)
    def _():
        cnt_ref[...] = jnp.zeros(cnt_ref.shape, F32)

    for j, (ids, prefix, total) in enumerate(routed):
        base = cnt_ref[...]
        base_ref[j] = base
        before = prefix + base
        tr = jnp.zeros((sub, LANES), F32)
        for k in range(TOP_K):
            rank = jnp.sum(jnp.where(col == ids[k], before, 0.0), axis=1, keepdims=True)
            tr = jnp.where(col == float(k), rank, tr)
        tr_ref[j * sub:(j + 1) * sub, :] = tr.astype(I32)
        cnt_ref[...] = base + total
    counts_ref[...] = cnt_ref[...]


def _merge(o_cmp, o_sel, o_win, o_swa, gates, x2, expand, w_bn, w_bs, w_o, ln_g, ln_b,
           wr_hi, wr_lo, b_r, alpha, sub):
    t, d = x2.shape
    hq = o_cmp.shape[1]
    tm = min(MERGE_STEP_TOKENS, t)
    n_sub = tm // sub
    row = lambda i: (i, 0)
    tok = lambda n: pl.BlockSpec((tm, n), row)
    return pl.pallas_call(
        functools.partial(_merge_kernel, alpha=alpha, sub=sub),
        grid=(t // tm,),
        in_specs=[tok(hq), tok(hq), tok(hq), tok(hq), tok(gates.shape[1]), tok(d),
                  _full(expand.shape), _full(w_bn.shape), _full(w_bs.shape), _full(w_o.shape),
                  _full(ln_g.shape), _full(ln_b.shape), _full(wr_hi.shape), _full(wr_lo.shape),
                  _full(b_r.shape)],
        out_specs=[tok(d), tok(LANES), tok(LANES), tok(LANES), _full((1, LANES)),
                   pl.BlockSpec((n_sub, 1, LANES), lambda i: (i, 0, 0))],
        out_shape=[jax.ShapeDtypeStruct((t, d), F32), jax.ShapeDtypeStruct((t, LANES), I32),
                   jax.ShapeDtypeStruct((t, LANES), F32), jax.ShapeDtypeStruct((t, LANES), I32),
                   jax.ShapeDtypeStruct((1, LANES), F32),
                   jax.ShapeDtypeStruct((t // sub, 1, LANES), F32)],
        scratch_shapes=[pltpu.VMEM((1, LANES), F32)],
        compiler_params=_params("arbitrary"),
        name="merge_ln_router",
    )(o_cmp, o_sel, o_win, o_swa, gates, x2, expand, w_bn, w_bs, w_o, ln_g, ln_b,
      wr_hi, wr_lo, b_r)


def _dispatch_kernel(pend_ref, padded_ref, cseg_ref, first_ref, len_ref, tot_ref,
                     h_ref, cpos_ref, xs_ref, comp0, comp1, zbuf, sem, zsem):
    j = pl.program_id(0)
    nt = pl.num_programs(0)
    tm = h_ref.shape[0]
    n_rows = xs_ref.shape[0]
    comps = (comp0, comp1)
    n_comp = comp0.shape[0]

    @pl.when(j == 0)
    def _():
        zbuf[...] = jnp.zeros(zbuf.shape, F32)
        used_rows = pend_ref[N_EXPERTS - 1]

        def zero_copy(start):
            start = pl.multiple_of(start, MOE_ROWS)
            return pltpu.make_async_copy(zbuf, xs_ref.at[pl.ds(start, MOE_ROWS), :], zsem)

        blocks = [(padded_ref[e] > 0, pend_ref[e] - MOE_ROWS) for e in range(N_EXPERTS)]
        blocks += [(used_rows + b * MOE_ROWS < n_rows, used_rows + b * MOE_ROWS)
                   for b in range(n_rows // MOE_ROWS - (nt * tm * TOP_K) // MOE_ROWS)]
        for cond, start in blocks:
            @pl.when(cond)
            def _(start=start):
                zero_copy(start).start()
        for cond, start in blocks:
            @pl.when(cond)
            def _(start=start):
                zero_copy(start).wait()

    def wait_writes(tile, slot):
        rows = pl.multiple_of(tot_ref[tile], F32_SUBLANES)
        pltpu.make_async_copy(comps[slot].at[pl.ds(0, rows), :], xs_ref.at[pl.ds(0, rows), :],
                              sem.at[slot]).wait()

    for slot in range(2):
        @pl.when((j >= 2) & ((j & 1) == slot))
        def _(slot=slot):
            wait_writes(j - 2, slot)

        @pl.when((j & 1) == slot)
        def _(slot=slot):
            row = lax.broadcasted_iota(I32, (n_comp, 1), 0)
            cpos = cpos_ref[...]
            sel = jnp.zeros((n_comp, tm), F32)
            for k in range(TOP_K):
                sel = jnp.where(row == cpos[k:k + 1, :], 1.0, sel)
            comps[slot][...] = jnp.dot(sel.astype(BF16), h_ref[...].astype(BF16),
                                       preferred_element_type=F32)
            for e in range(N_EXPERTS):
                ln = len_ref[j * N_EXPERTS + e]
                first = first_ref[j * N_EXPERTS + e]
                seg = cseg_ref[j * N_EXPERTS + e]
                for p in _run_sizes(tm):
                    done = ln & (-2 * p)
                    src = comps[slot].at[pl.ds(pl.multiple_of(seg + done, F32_SUBLANES), p), :]
                    dst = xs_ref.at[pl.ds(pl.multiple_of(first + done, F32_SUBLANES), p), :]
                    pl.when((ln & p) != 0)(pltpu.make_async_copy(src, dst, sem.at[slot]).start)

        @pl.when((j == nt - 1) & ((j & 1) == slot))
        def _(slot=slot):
            @pl.when(j >= 1)
            def _():
                wait_writes(j - 1, 1 - slot)
            wait_writes(j, slot)


def _dispatch(pend, padded, cseg, seg_first, seg_len, seg_tot, cpos_t, h, n_rows, tm):
    t, d = h.shape
    n_comp = -(-(tm * TOP_K + N_EXPERTS * (F32_SUBLANES - 1)) // MXU_DEPTH) * MXU_DEPTH
    grid_spec = pltpu.PrefetchScalarGridSpec(
        num_scalar_prefetch=6,
        grid=(t // tm,),
        in_specs=[pl.BlockSpec((tm, d), lambda i, *_: (i, 0)),
                  pl.BlockSpec((TOP_K, tm), lambda i, *_: (0, i))],
        out_specs=pl.BlockSpec(memory_space=pl.ANY),
        scratch_shapes=[pltpu.VMEM((n_comp, d), F32), pltpu.VMEM((n_comp, d), F32),
                        pltpu.VMEM((MOE_ROWS, d), F32), pltpu.SemaphoreType.DMA((2,)),
                        pltpu.SemaphoreType.DMA(())],
    )
    return pl.pallas_call(
        _dispatch_kernel,
        grid_spec=grid_spec,
        out_shape=jax.ShapeDtypeStruct((n_rows, d), F32),
        compiler_params=_params("arbitrary"),
        name="moe_dispatch",
    )(pend, padded, cseg, seg_first, seg_len, seg_tot, h, cpos_t)


def _moe_kernel(blk_e_ref, n_used_ref, first_ref, slot_ref, next_ref, x_ref, wi_hbm, bi_ref,
                wo_hbm, bo_ref, y_ref, wi_buf, wo_buf, wi_bf, wo_bf, wi_sem, wo_sem):
    f = wo_hbm.shape[1]
    b = pl.program_id(0)
    used = b < n_used_ref[0]
    e = blk_e_ref[b]
    slot = slot_ref[b]

    def fetch(expert, into):
        return (pltpu.make_async_copy(wi_hbm.at[expert], wi_buf.at[into], wi_sem.at[into]),
                pltpu.make_async_copy(wo_hbm.at[expert], wo_buf.at[into], wo_sem.at[into]))

    @pl.when(b == 0)
    def _():
        for cp in fetch(e, 0):
            cp.start()

    for s in range(2):
        @pl.when(used & (first_ref[b] == 1) & (slot == s))
        def _(s=s):
            @pl.when(next_ref[b] != e)
            def _():
                for cp in fetch(next_ref[b], 1 - s):
                    cp.start()
            for cp in fetch(e, s):
                cp.wait()
            wi_bf[...] = wi_buf[s].astype(BF16)
            wo_bf[...] = wo_buf[s].astype(BF16)

    @pl.when(used)
    def _():
        hdn = jnp.dot(x_ref[...].astype(BF16), wi_bf[...],
                      preferred_element_type=F32) + bi_ref[...]
        hg = jnp.minimum(hdn[:, :f], SWIGLU_LIMIT)
        hu = jnp.clip(hdn[:, f:], -SWIGLU_LIMIT, SWIGLU_LIMIT)
        act = hg * jax.nn.sigmoid(SWIGLU_ALPHA * hg) * (hu + 1.0)
        y_ref[...] = jnp.dot(act.astype(BF16), wo_bf[...],
                             preferred_element_type=F32) + bo_ref[...]

    @pl.when(jnp.logical_not(used))
    def _():
        y_ref[...] = jnp.zeros(y_ref.shape, F32)


def _moe_blocks(blk_e, n_used, padded, xs, w_in, b_in, w_out, b_out):
    n_rows, d = xs.shape
    e, _, f2 = w_in.shape
    f = w_out.shape[1]
    n_blk = n_rows // MOE_ROWS
    first = jnp.concatenate([jnp.ones((1,), I32), (blk_e[1:] != blk_e[:-1]).astype(I32)])
    ids = jnp.arange(e, dtype=I32)
    has_rows = padded > 0
    is_e = blk_e[:, None] == ids[None, :]
    slot = jnp.sum((ids[None, :] < blk_e[:, None]) & has_rows[None, :], axis=1) & 1
    later = jnp.where((ids[None, :] > ids[:, None]) & has_rows[None, :], ids[None, :], e)
    next_used = jnp.min(later, axis=1)
    next_used = jnp.where(next_used == e, ids, next_used).astype(I32)
    next_blk = jnp.sum(jnp.where(is_e, next_used[None, :], 0), axis=1).astype(I32)
    grid_spec = pltpu.PrefetchScalarGridSpec(
        num_scalar_prefetch=5,
        grid=(n_blk,),
        in_specs=[pl.BlockSpec((MOE_ROWS, d), lambda b, be, nu, *_: (jnp.minimum(b, nu[0] - 1), 0)),
                  pl.BlockSpec(memory_space=pl.ANY),
                  pl.BlockSpec((None, 1, f2), lambda b, be, *_: (be[b], 0, 0)),
                  pl.BlockSpec(memory_space=pl.ANY),
                  pl.BlockSpec((None, 1, d), lambda b, be, *_: (be[b], 0, 0))],
        out_specs=pl.BlockSpec((MOE_ROWS, d), lambda b, *_: (b, 0)),
        scratch_shapes=[pltpu.VMEM((2, d, f2), F32), pltpu.VMEM((2, f, d), F32),
                        pltpu.VMEM((d, f2), BF16), pltpu.VMEM((f, d), BF16),
                        pltpu.SemaphoreType.DMA((2,)), pltpu.SemaphoreType.DMA((2,))],
    )
    return pl.pallas_call(
        _moe_kernel,
        grid_spec=grid_spec,
        out_shape=jax.ShapeDtypeStruct((n_rows, d), F32),
        compiler_params=_params("arbitrary"),
        name="moe_experts",
    )(blk_e, n_used, first, slot.astype(I32), next_blk, xs, w_in, b_in, w_out, b_out)


def _run_sizes(tm):
    sizes, p = [], F32_SUBLANES
    while p <= tm:
        sizes.append(p)
        p *= 2
    return tuple(reversed(sizes))


def _stage_rows(tm):
    rows = tm * TOP_K + N_EXPERTS * 2 * (F32_SUBLANES - 1)
    return -(-rows // MXU_DEPTH) * MXU_DEPTH


def _final_kernel(start_ref, len_ref, seg_ref, tot_ref, h_ref, tg_ref, sp_ref, ys_ref, g_ref,
                  b_ref, o_ref, stage0, stage1, sem, *, alpha):
    i = pl.program_id(0)
    tm = h_ref.shape[0]
    stages = (stage0, stage1)
    n_stage = stage0.shape[0]

    def fetch(tile, into):
        for e in range(N_EXPERTS):
            ln = len_ref[tile * N_EXPERTS + e]
            first = start_ref[tile * N_EXPERTS + e]
            seg = seg_ref[tile * N_EXPERTS + e]
            for p in _run_sizes(tm):
                done = ln & (-2 * p)
                src = ys_ref.at[pl.ds(pl.multiple_of(first + done, F32_SUBLANES), p), :]
                dst = stages[into].at[pl.ds(pl.multiple_of(seg + done, F32_SUBLANES), p), :]
                pl.when((ln & p) != 0)(pltpu.make_async_copy(src, dst, sem.at[into]).start)

    @pl.when(i == 0)
    def _():
        stage0[...] = jnp.zeros(stage0.shape, F32)
        stage1[...] = jnp.zeros(stage1.shape, F32)
        fetch(0, 0)

    for slot in range(2):
        @pl.when((i & 1) == slot)
        def _(slot=slot):
            fetch(i + 1, 1 - slot)
            rows = pl.multiple_of(tot_ref[i], F32_SUBLANES)
            pltpu.make_async_copy(ys_ref.at[pl.ds(0, rows), :],
                                  stages[slot].at[pl.ds(0, rows), :], sem.at[slot]).wait()
            col = lax.broadcasted_iota(I32, (1, n_stage), 1)
            tg = tg_ref[...]
            sp = sp_ref[...]
            q = jnp.zeros((tm, n_stage), F32)
            for k in range(TOP_K):
                q = jnp.where(col == sp[:, k:k + 1], tg[:, k:k + 1], q)
            f = _split_dot(q, stages[slot][...].astype(BF16))
            o_ref[...] = _layer_norm(alpha * h_ref[...] + f, g_ref[...], b_ref[...])


def _final(run_start, run_len, run_seg, run_tot, h, top_g, spos, ys, ln_g, ln_b, alpha, tm):
    t, d = h.shape
    grid_spec = pltpu.PrefetchScalarGridSpec(
        num_scalar_prefetch=4,
        grid=(t // tm,),
        in_specs=[pl.BlockSpec((tm, d), lambda i, *_: (i, 0)),
                  pl.BlockSpec((tm, LANES), lambda i, *_: (i, 0)),
                  pl.BlockSpec((tm, TOP_K), lambda i, *_: (i, 0)),
                  pl.BlockSpec(memory_space=pl.ANY),
                  pl.BlockSpec(ln_g.shape, lambda i, *_: (0, 0)),
                  pl.BlockSpec(ln_b.shape, lambda i, *_: (0, 0))],
        out_specs=pl.BlockSpec((tm, d), lambda i, *_: (i, 0)),
        scratch_shapes=[pltpu.VMEM((_stage_rows(tm), d), F32),
                        pltpu.VMEM((_stage_rows(tm), d), F32), pltpu.SemaphoreType.DMA((2,))],
    )
    return pl.pallas_call(
        functools.partial(_final_kernel, alpha=alpha),
        grid_spec=grid_spec,
        out_shape=jax.ShapeDtypeStruct((t, d), F32),
        compiler_params=_params("arbitrary"),
        name="combine_ln",
    )(run_start, run_len, run_seg, run_tot, h, top_g, spos, ys, ln_g, ln_b)


def _rope_tables(s):
    half = HEAD_DIM // 2
    inv = ROPE_THETA ** (-np.arange(half, dtype=np.float64) / half)
    ang = np.arange(s, dtype=np.float64)[:, None] * inv[None, :]
    cos, sin = np.cos(ang).astype(np.float32), np.sin(ang).astype(np.float32)
    reps = LANES // HEAD_DIM
    cos_t = np.tile(np.concatenate([cos, cos], axis=1), (1, reps))
    sin_t = np.tile(np.concatenate([-sin, sin], axis=1), (1, reps))
    return jnp.asarray(cos_t), jnp.asarray(sin_t)


def _ceil_to(v, m):
    return (v + m - 1) // m * m


def _moe_plan(top_e, rank, tile_base, counts, t, tm, dt):
    a = t * TOP_K
    per, nd = dt // tm, t // dt
    d_base = tile_base[::per]
    d_cnt = jnp.concatenate([d_base[1:], counts[None, :]], axis=0) - d_base
    seg_len = _ceil_to(d_cnt, F32_SUBLANES)
    padded = _ceil_to(jnp.sum(seg_len, axis=0), MOE_ROWS)
    pend = jnp.cumsum(padded)
    seg_first = (pend - padded)[None, :] + jnp.cumsum(seg_len, axis=0) - seg_len
    cseg = jnp.cumsum(seg_len, axis=1) - seg_len
    onehot = (top_e[:, :, None] == jnp.arange(N_EXPERTS, dtype=I32)).reshape(
        nd, dt, TOP_K, N_EXPERTS)

    def pick(table):
        return jnp.sum(jnp.where(onehot, table[:, None, None, :], 0), axis=3).reshape(t, TOP_K)

    local = rank - pick(d_base)
    dest = pick(seg_first) + local
    cpos = pick(cseg) + local
    run_first = jnp.repeat(seg_first - d_base, per, axis=0) + tile_base
    n_blk = -(-(a + nd * N_EXPERTS * (F32_SUBLANES - 1)) // MOE_ROWS) + N_EXPERTS
    blk_first = jnp.arange(n_blk, dtype=I32) * MOE_ROWS
    blk_e = jnp.minimum(jnp.sum(pend[None, :] <= blk_first[:, None], axis=1),
                        N_EXPERTS - 1).astype(I32)
    n_used = (pend[-1] // MOE_ROWS).astype(I32).reshape(1)
    flat = lambda v: v.astype(I32).reshape(-1)
    segments = (flat(cseg), flat(seg_first), flat(seg_len), jnp.sum(seg_len, axis=1).astype(I32))
    return (segments, dest.astype(I32), cpos.astype(I32), run_first, blk_e, n_used,
            n_blk * MOE_ROWS, pend.astype(I32), padded.astype(I32))


def _combine_plan(top_e, dest, run_first, tile_base, counts, tm):
    cnt = jnp.concatenate([tile_base[1:], counts[None, :]], axis=0) - tile_base
    lead = run_first & (F32_SUBLANES - 1)
    run_len = jnp.where(cnt > 0, _ceil_to(lead + cnt, F32_SUBLANES), 0)
    run_seg = jnp.cumsum(run_len, axis=1) - run_len
    shift = (run_seg + lead - run_first)[:, None, None, :]
    onehot = (top_e[:, :, None] == jnp.arange(N_EXPERTS, dtype=I32)).reshape(
        -1, tm, TOP_K, N_EXPERTS)
    spos = jnp.sum(jnp.where(onehot, shift, 0), axis=3).reshape(dest.shape) + dest
    flat = lambda v: jnp.pad(v.astype(I32), ((0, 1), (0, 0))).reshape(-1)
    run_tot = jnp.sum(run_len, axis=1).astype(I32)
    return flat(run_first - lead), flat(run_len), flat(run_seg), run_tot, spos.astype(I32)


def _layer(x, w_in, k_pe, k_w1, k_w2, v_pe, v_w1, v_w2, sinks, w_br_nsa, w_br_swa, w_out,
           ln1_g, ln1_b, w_router, b_router, w_e_in, b_e_in, w_e_out, b_e_out, ln2_g, ln2_b,
           alpha):
    b, s, d = x.shape
    t = b * s
    qb = Q_BLOCK
    nq_n, nkv = NSA_HEADS * HEAD_DIM, NSA_KV * HEAD_DIM
    nq_s, nkv_s = SWA_HEADS * HEAD_DIM, SWA_KV * HEAD_DIM
    widths = (nq_n, nkv, nkv, nkv, nkv, nkv, nkv, NSA_HEADS * 3, nq_s, nkv_s, nkv_s, 2 * d)
    offs = [0]
    for w in widths:
        offs.append(offs[-1] + w)
    col = lambda j: w_in[:, offs[j]:offs[j + 1]]
    (c_qn, c_kc, c_vc, c_ks, c_vs, c_kw, c_vw, c_gn, c_qs, c_k_s, c_v_s, c_gm) = map(col, range(12))
    w_rope = jnp.concatenate([c_qn, c_qs, c_ks, c_kw, c_k_s], axis=1).astype(BF16)
    w_plain = jnp.concatenate([c_kc, c_vc, c_vs, c_vw, c_v_s], axis=1).astype(BF16)
    gn_pad = LANES - NSA_HEADS * 3
    w_gate = jnp.concatenate([c_gm, c_gn, jnp.zeros((d, gn_pad), F32)], axis=1).astype(BF16)
    cos_t, sin_t = _rope_tables(s)

    x2 = x.reshape(t, d)
    qn_rot, qs_rot, kk_rot, qn_raw, plain, gates = _project(
        x2, w_rope, w_plain, w_gate, cos_t, sin_t, s, min(256, s))

    nc = (s - CMP_BLOCK) // CMP_STRIDE + 1
    ncp = s // CMP_STRIDE
    half = CMP_STRIDE * HEAD_DIM

    def halves(cols):
        v = cols.reshape(b, s, NSA_KV, HEAD_DIM).transpose(0, 2, 1, 3)
        return v.reshape(b * NSA_KV, ncp, half)

    t2 = jnp.stack([halves(plain[:, 0:nkv]), halves(plain[:, nkv:2 * nkv])])
    t_lo = t2
    t_hi = jnp.concatenate([t2[:, :, 1:], jnp.zeros_like(t2[:, :, :1])], axis=2)
    pe2 = jnp.stack([k_pe.reshape(2, half), v_pe.reshape(2, half)])
    w1 = jnp.stack([k_w1, v_w1]).astype(BF16)
    w2 = jnp.stack([k_w2, v_w2]).astype(BF16)
    kvc = _compress(t_lo, t_hi, pe2, w1, w2).reshape(2, b, NSA_KV, ncp, HEAD_DIM)

    nsel = s // SEL_BLOCK
    nselp = -(-nsel // LANES) * LANES
    cstart = np.arange(ncp) * CMP_STRIDE
    sstart = np.arange(nselp) * SEL_BLOCK
    overlap = ((cstart[:, None] < sstart[None, :] + SEL_BLOCK)
               & (cstart[:, None] + CMP_BLOCK > sstart[None, :])
               & (np.arange(ncp)[:, None] < nc) & (np.arange(nselp)[None, :] < nsel))
    o_cmp, notsel = _cmp_attention(qn_raw.reshape(b, s, nq_n), kvc[0], kvc[1],
                                   jnp.asarray(overlap.T.astype(BF16)),
                                   min(CMP_STEP_QUERIES, s))

    def group_major(cols):
        return cols.reshape(b, s, NSA_KV, HEAD_DIM).transpose(0, 2, 1, 3)

    k_sel = group_major(kk_rot[:, 0:nkv])
    onehot = (np.arange(s)[:, None] // SEL_BLOCK == np.arange(nselp)[None, :])
    k_tail = np.concatenate([np.zeros((s, LANES - HEAD_DIM), np.float32),
                             np.where(onehot, SEL_PENALTY, 0.0).astype(np.float32)], axis=1)
    k_aug = jnp.concatenate(
        [k_sel, jnp.broadcast_to(jnp.asarray(k_tail.astype(BF16)), (b, NSA_KV) + k_tail.shape)],
        axis=3)
    vt_all = plain.reshape(b, s, -1)[:, :, 2 * nkv:].transpose(0, 2, 1)
    o_sel = _sel_attention(qn_rot.reshape(b, s, nq_n), notsel, k_aug, vt_all,
                           min(SEL_STEP_QUERIES, s), min(SEL_KEY_TILE, s))

    kk3 = kk_rot.reshape(b, s, -1)
    o_win = _band_attention(qn_rot.reshape(b, s, nq_n), kk3, 1, vt_all, 1, nkv,
                            NSA_WINDOW, None, qb)
    o_swa = _band_attention(qs_rot.reshape(b, s, nq_s), kk3, 2, vt_all, 2, nkv_s,
                            SWA_WINDOW, sinks, qb)

    gi = np.arange(LANES)
    ci = np.arange(3 * nq_n)
    expand = jnp.asarray(((gi[:, None] // 3 == (ci[None, :] % nq_n) // HEAD_DIM)
                          & (gi[:, None] % 3 == ci[None, :] // nq_n)
                          & (gi[:, None] < NSA_HEADS * 3)).astype(BF16))
    wr_pad = jnp.pad(w_router, ((0, 0), (0, LANES - N_EXPERTS)))
    wr_hi = wr_pad.astype(BF16)
    wr_lo = (wr_pad - wr_hi.astype(F32)).astype(BF16)
    b_r = jnp.concatenate([b_router, jnp.full((LANES - N_EXPERTS,), -jnp.inf, F32)]).reshape(1, LANES)
    tm = min(256, t)
    h, top_e, top_g, rank, counts, tile_base = _merge(
        o_cmp.reshape(t, nq_n), o_sel.reshape(t, nq_n), o_win.reshape(t, nq_n),
        o_swa.reshape(t, nq_s), gates, x2, expand, w_br_nsa.astype(BF16), w_br_swa.astype(BF16),
        w_out.astype(BF16), ln1_g.reshape(1, d), ln1_b.reshape(1, d), wr_hi, wr_lo, b_r,
        alpha, tm)

    top_e, rank = top_e[:, :TOP_K], rank[:, :TOP_K]
    counts = counts[0, :N_EXPERTS].astype(I32)
    tile_base = tile_base[:, 0, :N_EXPERTS].astype(I32)
    dt = min(DISPATCH_TOKENS, t)
    segments, dest, cpos, run_first, blk_e, n_used, n_rows, pend, padded = _moe_plan(
        top_e, rank, tile_base, counts, t, tm, dt)
    xs = _dispatch(pend, padded, *segments, cpos.T, h, n_rows, dt)
    ys = _moe_blocks(blk_e, n_used, padded, xs, w_e_in, b_e_in.reshape(N_EXPERTS, 1, -1),
                     w_e_out, b_e_out.reshape(N_EXPERTS, 1, -1))
    run_start, run_len, run_seg, run_tot, spos = _combine_plan(
        top_e, dest, run_first, tile_base, counts, tm)
    out = _final(run_start, run_len, run_seg, run_tot, h, top_g, spos, ys, ln2_g.reshape(1, d),
                 ln2_b.reshape(1, d), alpha, tm)
    return out.reshape(b, s, d)


def kernel(x, w_in, nsa_k_pe, nsa_k_w1, nsa_k_w2, nsa_v_pe, nsa_v_w1, nsa_v_w2, swa_sinks, w_br_nsa, w_br_swa, w_out, ln1_g, ln1_b, w_router, b_router, w_expert_in, b_expert_in, w_expert_out, b_expert_out, ln2_g, ln2_b):
    depth = w_in.shape[0]
    alpha = (2.0 * depth) ** 0.25
    for l in range(depth):
        x = _layer(x, w_in[l], nsa_k_pe[l], nsa_k_w1[l], nsa_k_w2[l], nsa_v_pe[l], nsa_v_w1[l],
                   nsa_v_w2[l], swa_sinks[l], w_br_nsa[l], w_br_swa[l], w_out[l], ln1_g[l],
                   ln1_b[l], w_router[l], b_router[l], w_expert_in[l], b_expert_in[l],
                   w_expert_out[l], b_expert_out[l], ln2_g[l], ln2_b[l], alpha)
    return x
```

```python
import functools

import jax
import jax.numpy as jnp
import numpy as np
from jax import lax
from jax.experimental import pallas as pl
from jax.experimental.pallas import tpu as pltpu

BF16 = jnp.bfloat16
F32 = jnp.float32
I32 = jnp.int32

HEAD_DIM = 64
NSA_HEADS = 8
NSA_KV = 2
CMP_BLOCK = 32
CMP_STRIDE = 16
SEL_BLOCK = 64
SEL_TOPN = 16
NSA_WINDOW = 512
SWA_HEADS = 8
SWA_KV = 2
SWA_WINDOW = 128
Q_BLOCK = 128
ROPE_THETA = 10000.0
N_EXPERTS = 32
TOP_K = 4
SWIGLU_LIMIT = 7.0
SWIGLU_ALPHA = 1.702
LN_EPS = 1e-5

LANES = 128
BF16_SUBLANES = 16
F32_SUBLANES = 8
MXU_DEPTH = 256
MASKED = -1e30
M_INIT = -1e29
SEL_PENALTY = -(2.0 ** 100)
VMEM_LIMIT = 52 * 1024 * 1024
MOE_ROWS = 256
DISPATCH_TOKENS = 512
MERGE_STEP_TOKENS = 512
BAND_STEP_QUERIES = 512
CMP_STEP_QUERIES = 512
SEL_STEP_QUERIES = 128
SEL_KEY_TILE = 512

R_NSA = NSA_HEADS // NSA_KV
R_SWA = SWA_HEADS // SWA_KV
NT_DIMS = (((1,), (1,)), ((), ()))


def _params(*sem):
    return pltpu.CompilerParams(dimension_semantics=sem, vmem_limit_bytes=VMEM_LIMIT)


def _full(shape):
    n = len(shape)
    return pl.BlockSpec(shape, lambda *_: (0,) * n)


def _proj_kernel(x_ref, wr_ref, wp_ref, wg_ref, cos_ref, sin_ref,
                 qn_rot_ref, qs_rot_ref, kk_rot_ref, qn_raw_ref, plain_ref, gates_ref):
    xb = x_ref[...].astype(BF16)
    acc = jnp.dot(xb, wr_ref[...], preferred_element_type=F32)
    cos = cos_ref[...]
    sin = sin_ref[...]
    lane = lax.broadcasted_iota(I32, cos.shape, 1)
    first_half = (lane & (HEAD_DIM - 1)) < HEAD_DIM // 2

    def rope(t):
        partner = jnp.where(first_half, pltpu.roll(t, LANES - HEAD_DIM // 2, 1),
                            pltpu.roll(t, HEAD_DIM // 2, 1))
        return (t * cos + partner * sin).astype(BF16)

    nq = qn_rot_ref.shape[1] // LANES
    ns = qs_rot_ref.shape[1] // LANES
    nk = kk_rot_ref.shape[1] // LANES
    for c in range(nq):
        qn_rot_ref[:, c * LANES:(c + 1) * LANES] = rope(acc[:, c * LANES:(c + 1) * LANES])
    for c in range(ns):
        o = (nq + c) * LANES
        qs_rot_ref[:, c * LANES:(c + 1) * LANES] = rope(acc[:, o:o + LANES])
    for c in range(nk):
        o = (nq + ns + c) * LANES
        kk_rot_ref[:, c * LANES:(c + 1) * LANES] = rope(acc[:, o:o + LANES])
    qn_raw_ref[...] = acc[:, :nq * LANES].astype(BF16)
    plain_ref[...] = jnp.dot(xb, wp_ref[...], preferred_element_type=F32).astype(BF16)
    gates_ref[...] = jnp.dot(xb, wg_ref[...], preferred_element_type=F32)


def _project(x, w_rope, w_plain, w_gate, cos_t, sin_t, tm):
    b, seq, d = x.shape
    t = b * seq
    nr, npl, ng = w_rope.shape[1], w_plain.shape[1], w_gate.shape[1]
    nqn, nqs = NSA_HEADS * HEAD_DIM, SWA_HEADS * HEAD_DIM
    nkk = nr - nqn - nqs
    spb = seq // tm
    row = lambda i: (i, 0)
    seq_blk = lambda n: pl.BlockSpec((None, tm, n), lambda i: (i // spb, i % spb, 0))
    seq_out = lambda n: jax.ShapeDtypeStruct((b, seq, n), BF16)
    return pl.pallas_call(
        _proj_kernel,
        grid=(t // tm,),
        in_specs=[seq_blk(d), _full(w_rope.shape), _full(w_plain.shape), _full(w_gate.shape),
                  pl.BlockSpec((tm, LANES), lambda i: (i % spb, 0)),
                  pl.BlockSpec((tm, LANES), lambda i: (i % spb, 0))],
        out_specs=[seq_blk(nqn), seq_blk(nqs), seq_blk(nkk), seq_blk(nqn), seq_blk(npl),
                   pl.BlockSpec((tm, ng), row)],
        out_shape=[seq_out(nqn), seq_out(nqs), seq_out(nkk), seq_out(nqn), seq_out(npl),
                   jax.ShapeDtypeStruct((t, ng), F32)],
        compiler_params=_params("parallel"),
        name="proj",
    )(x, w_rope, w_plain, w_gate, cos_t, sin_t)


def _compress_kernel(a_ref, b_ref, pe_ref, w1_ref, w2_ref, out_ref):
    half = a_ref.shape[1]
    a = (a_ref[...].astype(F32) + pe_ref[0:1, :]).astype(BF16)
    b = (b_ref[...].astype(F32) + pe_ref[1:2, :]).astype(BF16)
    hid = jnp.dot(a, w1_ref[0:half, :], preferred_element_type=F32)
    hid = hid + jnp.dot(b, w1_ref[half:2 * half, :], preferred_element_type=F32)
    act = jax.nn.gelu(hid).astype(BF16)
    out_ref[...] = jnp.dot(act, w2_ref[...], preferred_element_type=F32).astype(BF16)


def _compress(t_lo, t_hi, pe2, w1, w2):
    two, bg, ncp, half = t_lo.shape
    hid = w1.shape[2]
    blk = lambda shape: pl.BlockSpec((None, None) + shape, lambda j, i: (j, i, 0, 0))
    wsp = lambda shape: pl.BlockSpec((None,) + shape, lambda j, i: (j, 0, 0))
    return pl.pallas_call(
        _compress_kernel,
        grid=(two, bg),
        in_specs=[blk((ncp, half)), blk((ncp, half)), wsp((2, half)), wsp((2 * half, hid)),
                  wsp((hid, HEAD_DIM))],
        out_specs=blk((ncp, HEAD_DIM)),
        out_shape=jax.ShapeDtypeStruct((two, bg, ncp, HEAD_DIM), BF16),
        compiler_params=_params("parallel", "parallel"),
        name="compress",
    )(t_lo, t_hi, pe2, w1, w2)


def _stack_heads(q, g, r):
    return jnp.concatenate(
        [q[:, (g * r + j) * HEAD_DIM:(g * r + j + 1) * HEAD_DIM] for j in range(r)], axis=0)


def _unstack_heads_t(parts, r, qb):
    blocks = []
    for o in parts:
        for j in range(0, r, 2):
            pair = jnp.concatenate([o[:, j * qb:(j + 1) * qb], o[:, (j + 1) * qb:(j + 2) * qb]],
                                   axis=0)
            blocks.append(pair.T)
    return jnp.concatenate(blocks, axis=1)


def _topk_mask_cols(vals, k):
    n = vals.shape[0]
    row = lax.broadcasted_iota(I32, vals.shape, 0).astype(F32)
    taken = jnp.zeros(vals.shape, F32)
    work = vals
    for _ in range(k):
        mx = jnp.max(work, axis=0, keepdims=True)
        first = jnp.min(jnp.where(work == mx, row, float(n)), axis=0, keepdims=True)
        pick = row == first
        taken = jnp.where(pick, 1.0, taken)
        work = jnp.where(pick, -jnp.inf, work)
    return taken > 0.5


def _cmp_kernel(q_ref, kc_ref, vct_ref, ovt_ref, o_ref, notsel_ref):
    qb = q_ref.shape[0]
    ncp = kc_ref.shape[1]
    nselp = ovt_ref.shape[0]
    i = pl.program_id(1)
    rows = R_NSA * qb
    pos = i * qb + (lax.broadcasted_iota(I32, (1, rows), 1) & (qb - 1))
    cend = lax.broadcasted_iota(I32, (ncp, 1), 0) * CMP_STRIDE + (CMP_BLOCK - 1)
    bias = jnp.where(cend <= pos, 0.0, MASKED)
    live = (pos >= CMP_BLOCK - 1).astype(F32)
    q = q_ref[...]
    outs, imps = [], []
    for g in range(NSA_KV):
        qg = _stack_heads(q, g, R_NSA) * (HEAD_DIM ** -0.5)
        st = lax.dot_general(kc_ref[g], qg, NT_DIMS, preferred_element_type=F32)
        st = st + bias
        e = jnp.exp(st - jnp.max(st, axis=0, keepdims=True))
        pt = e * (live / jnp.sum(e, axis=0, keepdims=True))
        outs.append(jnp.dot(vct_ref[g], pt.astype(BF16), preferred_element_type=F32))
        psum = pt[:, 0:qb]
        for j in range(1, R_NSA):
            psum = psum + pt[:, j * qb:(j + 1) * qb]
        p_hi = psum.astype(BF16)
        p_lo = (psum - p_hi.astype(F32)).astype(BF16)
        imps.append(jnp.dot(ovt_ref[...], p_hi, preferred_element_type=F32)
                    + jnp.dot(ovt_ref[...], p_lo, preferred_element_type=F32))
    imp = jnp.concatenate(imps, axis=1)
    lane = lax.broadcasted_iota(I32, (1, NSA_KV * qb), 1)
    cur = (i * qb + (lane & (qb - 1))) >> 6
    jb = lax.broadcasted_iota(I32, (nselp, 1), 0)
    forced = (jb == 0) | (jb == cur) | (jb == cur - 1)
    imp = jnp.where(jb > cur, -1.0, jnp.where(forced, 1e6, imp))
    notsel = jnp.where(_topk_mask_cols(imp, SEL_TOPN), 0.0, 1.0)
    for g in range(NSA_KV):
        notsel_ref[g] = notsel[:, g * qb:(g + 1) * qb].T.astype(BF16)
    o_ref[...] = _unstack_heads_t(outs, R_NSA, qb)


def _cmp_attention(q_raw, kc, vc, overlap_t, qb):
    vc = vc.transpose(0, 1, 3, 2)
    b, s, hq = q_raw.shape
    _, g, ncp, dh = kc.shape
    nselp = overlap_t.shape[0]
    return pl.pallas_call(
        _cmp_kernel,
        grid=(b, s // qb),
        in_specs=[pl.BlockSpec((None, qb, hq), lambda bi, i: (bi, i, 0)),
                  pl.BlockSpec((None, g, ncp, dh), lambda bi, i: (bi, 0, 0, 0)),
                  pl.BlockSpec((None, g, dh, ncp), lambda bi, i: (bi, 0, 0, 0)),
                  _full(overlap_t.shape)],
        out_specs=[pl.BlockSpec((None, qb, hq), lambda bi, i: (bi, i, 0)),
                   pl.BlockSpec((None, g, qb, nselp), lambda bi, i: (bi, 0, i, 0))],
        out_shape=[jax.ShapeDtypeStruct((b, s, hq), F32),
                   jax.ShapeDtypeStruct((b, g, s, nselp), BF16)],
        compiler_params=_params("parallel", "parallel"),
        name="cmp_attn",
    )(q_raw, kc, vc, overlap_t)


def _sel_kernel(q_ref, notsel_ref, k_ref, vt_ref, o_ref, *score_bufs, tk):
    s_even, s_odd = score_bufs[:NSA_KV], score_bufs[NSA_KV:]
    qb = q_ref.shape[0]
    i = pl.program_id(1)
    rows = R_NSA * qb
    qpos = i * qb + (lax.broadcasted_iota(I32, (1, rows), 1) & (qb - 1))
    n_clear = (i * qb) // tk
    q = q_ref[...]
    q_augs = []
    for g in range(NSA_KV):
        qg = _stack_heads(q, g, R_NSA) * (HEAD_DIM ** -0.5)
        q_augs.append(jnp.concatenate(
            [qg, jnp.zeros((rows, LANES - HEAD_DIM), BF16),
             jnp.concatenate([notsel_ref[g]] * R_NSA, axis=0)], axis=1))

    def scores(kt, g):
        start = pl.multiple_of(kt * tk, tk)
        return lax.dot_general(k_ref[g, pl.ds(start, tk), :], q_augs[g], NT_DIMS,
                               preferred_element_type=F32)

    def consume(kt, g, s_ref, m, acc, causal):
        start = pl.multiple_of(kt * tk, tk)
        vt_t = jnp.concatenate([vt_ref[g * HEAD_DIM:(g + 1) * HEAD_DIM, pl.ds(start, tk)],
                                jnp.ones((BF16_SUBLANES, tk), vt_ref.dtype)], axis=0)
        st = s_ref[...]
        if causal:
            kpos = start + lax.broadcasted_iota(I32, (tk, 1), 0)
            st = jnp.where(kpos <= qpos, st, MASKED)
        m_new = jnp.maximum(m, jnp.max(st, axis=0, keepdims=True))
        pt = jnp.exp(st - m_new).astype(BF16)
        acc = jnp.exp(m - m_new) * acc + jnp.dot(vt_t, pt, preferred_element_type=F32)
        return m_new, acc

    def advance(kt, carry, cur, nxt):
        new = []
        for g in range(NSA_KV):
            nxt[g][...] = scores(kt + 1, g)
            new.append(consume(kt, g, cur[g], *carry[g], False))
        return tuple(new)

    def pair(j, carry):
        carry = advance(2 * j, carry, s_even, s_odd)
        return advance(2 * j + 1, carry, s_odd, s_even)

    def finish(carry, cur):
        outs = []
        for g in range(NSA_KV):
            _, acc = consume(n_clear, g, cur[g], *carry[g], True)
            outs.append(acc[:HEAD_DIM] / acc[HEAD_DIM:HEAD_DIM + 1])
        o_ref[...] = _unstack_heads_t(outs, R_NSA, qb)

    for g in range(NSA_KV):
        s_even[g][...] = scores(0, g)
    init = tuple((jnp.full((1, rows), M_INIT, F32),
                  jnp.zeros((HEAD_DIM + BF16_SUBLANES, rows), F32))
                 for _ in range(NSA_KV))
    carry = lax.fori_loop(0, n_clear // 2, pair, init)

    @pl.when((n_clear & 1) == 0)
    def _():
        finish(carry, s_even)

    @pl.when((n_clear & 1) == 1)
    def _():
        finish(advance(n_clear - 1, carry, s_even, s_odd), s_odd)


def _sel_attention(q_rot, notsel, k_aug, vt_aug, qb, tk):
    b, s, hq = q_rot.shape
    _, g, _, kw = k_aug.shape
    nselp = notsel.shape[3]
    return pl.pallas_call(
        functools.partial(_sel_kernel, tk=tk),
        grid=(b, s // qb),
        in_specs=[pl.BlockSpec((None, qb, hq), lambda bi, i: (bi, i, 0)),
                  pl.BlockSpec((None, g, qb, nselp), lambda bi, i: (bi, 0, i, 0)),
                  pl.BlockSpec((None, g, s, kw), lambda bi, i: (bi, 0, 0, 0)),
                  pl.BlockSpec((None, g * HEAD_DIM, s), lambda bi, i: (bi, 0, 0))],
        out_specs=pl.BlockSpec((None, qb, hq), lambda bi, i: (bi, i, 0)),
        out_shape=jax.ShapeDtypeStruct((b, s, hq), F32),
        scratch_shapes=[pltpu.VMEM((tk, R_NSA * qb), F32)] * (2 * g),
        compiler_params=_params("parallel", "parallel"),
        name="sel_attn",
    )(q_rot, notsel, k_aug, vt_aug)


def _band_kernel(*refs, window, wlen, r, kv, qb, has_sinks):
    if has_sinks:
        sink_ref, q_ref, k_ref, vt_ref, o_ref = refs
    else:
        q_ref, k_ref, vt_ref, o_ref = refs
    nsub = q_ref.shape[0] // qb
    rows = r * qb
    lane = lax.broadcasted_iota(I32, (1, rows), 1)

    def mask_bias(i):
        start = jnp.maximum((i + 1) * qb - wlen, 0)
        rel = i * qb + (lane & (qb - 1)) - start - lax.broadcasted_iota(I32, (wlen, 1), 0)
        return jnp.where((rel >= 0) & (rel < window), 0.0, MASKED)

    def body(shared_bias):
        first = pl.program_id(1) * nsub
        if shared_bias:
            bias = mask_bias(first)
        for sb in range(nsub):
            i = first + sb
            if not shared_bias:
                bias = mask_bias(i)
            start = pl.multiple_of(jnp.maximum((i + 1) * qb - wlen, 0), qb)
            q = q_ref[sb * qb:(sb + 1) * qb, :]
            kw = k_ref[pl.ds(start, wlen), :]
            vtw = vt_ref[:, pl.ds(start, wlen)]
            outs = []
            for g in range(kv):
                qg = _stack_heads(q, g, r) * (HEAD_DIM ** -0.5)
                st = lax.dot_general(kw[:, g * HEAD_DIM:(g + 1) * HEAD_DIM], qg, NT_DIMS,
                                     preferred_element_type=F32) + bias
                m = jnp.max(st, axis=0, keepdims=True)
                if has_sinks:
                    sk = jnp.full((1, rows), sink_ref[g * r], F32)
                    for j in range(1, r):
                        sk = jnp.where(lane >= j * qb, sink_ref[g * r + j], sk)
                    m = jnp.maximum(m, sk)
                e = jnp.exp(st - m).astype(BF16)
                v_ones = jnp.concatenate([vtw[g * HEAD_DIM:(g + 1) * HEAD_DIM, :],
                                          jnp.ones((BF16_SUBLANES, wlen), vtw.dtype)], axis=0)
                ot = jnp.dot(v_ones, e, preferred_element_type=F32)
                den = ot[HEAD_DIM:HEAD_DIM + 1]
                if has_sinks:
                    den = den + jnp.exp(sk - m)
                outs.append(ot[:HEAD_DIM] / den)
            o_ref[sb * qb:(sb + 1) * qb, :] = _unstack_heads_t(outs, r, qb)

    if nsub * qb >= wlen - qb:
        @pl.when(pl.program_id(1) == 0)
        def _():
            body(False)

        @pl.when(pl.program_id(1) > 0)
        def _():
            body(True)
    else:
        body(False)


def _band_attention(q_rot, k_all, k_blk, vt_all, v_blk, gk, window, sinks, qb):
    b, s, hq = q_rot.shape
    assert k_all.shape[2] % gk == 0 and vt_all.shape[1] % gk == 0
    kv = gk // HEAD_DIM
    r = hq // gk
    back = -(-window // qb)
    wlen = (back + 1) * qb
    assert wlen <= s
    has_sinks = sinks is not None
    tq = min(BAND_STEP_QUERIES, s)
    in_specs = [pl.BlockSpec((None, tq, hq), lambda bi, i: (bi, i, 0)),
                pl.BlockSpec((None, s, gk), lambda bi, i: (bi, 0, k_blk)),
                pl.BlockSpec((None, gk, s), lambda bi, i: (bi, v_blk, 0))]
    args = [q_rot, k_all, vt_all]
    if has_sinks:
        in_specs = [pl.BlockSpec(memory_space=pltpu.SMEM)] + in_specs
        args = [sinks.astype(F32)] + args
    return pl.pallas_call(
        functools.partial(_band_kernel, window=window, wlen=wlen, r=r, kv=kv, qb=qb,
                          has_sinks=has_sinks),
        grid=(b, s // tq),
        in_specs=in_specs,
        out_specs=pl.BlockSpec((None, tq, hq), lambda bi, i: (bi, i, 0)),
        out_shape=jax.ShapeDtypeStruct((b, s, hq), F32),
        compiler_params=_params("parallel", "parallel"),
        name="band_attn_sink" if has_sinks else "band_attn",
    )(*args)


def _layer_norm(v, g, b):
    mu = jnp.mean(v, axis=1, keepdims=True)
    c = v - mu
    var = jnp.mean(c * c, axis=1, keepdims=True)
    return c * lax.rsqrt(var + LN_EPS) * g + b


def _split_dot(a, w):
    hi = a.astype(BF16)
    lo = (a - hi.astype(F32)).astype(BF16)
    return (jnp.dot(hi, w, preferred_element_type=F32)
            + jnp.dot(lo, w, preferred_element_type=F32))


def _merge_kernel(ocmp_ref, osel_ref, owin_ref, oswa_ref, gates_ref, x_ref, exp_ref,
                  wbn_ref, wbs_ref, wo_ref, lng_ref, lnb_ref, wrh_ref, wrl_ref, br_ref,
                  h_ref, te_ref, tg_ref, tr_ref, counts_ref, base_ref, cnt_ref, *, alpha, sub):
    d = x_ref.shape[1]
    hq = ocmp_ref.shape[1]
    n_sub = x_ref.shape[0] // sub
    col = lax.broadcasted_iota(I32, (sub, LANES), 1).astype(F32)

    def route(rows):
        gates = gates_ref[rows, :]
        gn = jax.nn.sigmoid(gates[:, 2 * d:])
        gexp = _split_dot(gn, exp_ref[...])
        o_nsa = (gexp[:, 0:hq] * ocmp_ref[rows, :] + gexp[:, hq:2 * hq] * osel_ref[rows, :]
                 + gexp[:, 2 * hq:3 * hq] * owin_ref[rows, :])
        y_nsa = jnp.dot(o_nsa.astype(BF16), wbn_ref[...], preferred_element_type=F32)
        y_swa = jnp.dot(oswa_ref[rows, :].astype(BF16), wbs_ref[...],
                        preferred_element_type=F32)
        gm = jax.nn.sigmoid(gates[:, :2 * d])
        mixed = gm[:, :d] * y_nsa + gm[:, d:] * y_swa
        z = jnp.dot(mixed.astype(BF16), wo_ref[...], preferred_element_type=F32)
        h = _layer_norm(alpha * x_ref[rows, :] + z, lng_ref[...], lnb_ref[...])
        h_ref[rows, :] = h
        h_hi = h.astype(BF16)
        h_lo = (h - h_hi.astype(F32)).astype(BF16)
        logits = (jnp.dot(h_hi, wrh_ref[...], preferred_element_type=F32)
                  + jnp.dot(h_lo, wrh_ref[...], preferred_element_type=F32)
                  + jnp.dot(h_hi, wrl_ref[...], preferred_element_type=F32)) + br_ref[...]
        work = logits
        vals, ids = [], []
        for _ in range(TOP_K):
            mx = jnp.max(work, axis=1, keepdims=True)
            first = jnp.min(jnp.where(work == mx, col, float(LANES)), axis=1, keepdims=True)
            vals.append(mx)
            ids.append(first)
            work = jnp.where(col == first, -jnp.inf, work)
        es = [jnp.exp(v - vals[0]) for v in vals]
        den = es[0]
        for e in es[1:]:
            den = den + e
        hits = jnp.zeros(logits.shape, F32)
        te = jnp.zeros(logits.shape, F32)
        tg = jnp.zeros(logits.shape, F32)
        for k in range(TOP_K):
            hits = jnp.where(col == ids[k], 1.0, hits)
            te = jnp.where(col == float(k), ids[k], te)
            tg = jnp.where(col == float(k), es[k] / den, tg)
        te_ref[rows, :] = te.astype(I32)
        tg_ref[rows, :] = tg
        earlier = (lax.broadcasted_iota(I32, (sub, sub), 1)
                   < lax.broadcasted_iota(I32, (sub, sub), 0))
        prefix = jnp.dot(jnp.where(earlier, 1.0, 0.0).astype(BF16), hits.astype(BF16),
                         preferred_element_type=F32)
        return ids, prefix, jnp.sum(hits, axis=0, keepdims=True)

    routed = [route(slice(j * sub, (j + 1) * sub)) for j in range(n_sub)]

    @pl.when(pl.program_id(0) == 0)
    def _():
        cnt_ref[...] = jnp.zeros(cnt_ref.shape, F32)

    for j, (ids, prefix, total) in enumerate(routed):
        base = cnt_ref[...]
        base_ref[j] = base
        before = prefix + base
        tr = jnp.zeros((sub, LANES), F32)
        for k in range(TOP_K):
            rank = jnp.sum(jnp.where(col == ids[k], before, 0.0), axis=1, keepdims=True)
            tr = jnp.where(col == float(k), rank, tr)
        tr_ref[j * sub:(j + 1) * sub, :] = tr.astype(I32)
        cnt_ref[...] = base + total
    counts_ref[...] = cnt_ref[...]


def _merge(o_cmp, o_sel, o_win, o_swa, gates, x, expand, w_bn, w_bs, w_o, ln_g, ln_b,
           wr_hi, wr_lo, b_r, alpha, sub):
    b, s, d = x.shape
    t = b * s
    hq = o_cmp.shape[2]
    tm = min(MERGE_STEP_TOKENS, s)
    n_sub = tm // sub
    spb = s // tm
    row = lambda i: (i, 0)
    tok = lambda n: pl.BlockSpec((tm, n), row)
    seq = lambda n: pl.BlockSpec((None, tm, n), lambda i: (i // spb, i % spb, 0))
    return pl.pallas_call(
        functools.partial(_merge_kernel, alpha=alpha, sub=sub),
        grid=(t // tm,),
        in_specs=[seq(hq), seq(hq), seq(hq), seq(hq), tok(gates.shape[1]), seq(d),
                  _full(expand.shape), _full(w_bn.shape), _full(w_bs.shape), _full(w_o.shape),
                  _full(ln_g.shape), _full(ln_b.shape), _full(wr_hi.shape), _full(wr_lo.shape),
                  _full(b_r.shape)],
        out_specs=[tok(d), tok(LANES), tok(LANES), tok(LANES), _full((1, LANES)),
                   pl.BlockSpec((n_sub, 1, LANES), lambda i: (i, 0, 0))],
        out_shape=[jax.ShapeDtypeStruct((t, d), F32), jax.ShapeDtypeStruct((t, LANES), I32),
                   jax.ShapeDtypeStruct((t, LANES), F32), jax.ShapeDtypeStruct((t, LANES), I32),
                   jax.ShapeDtypeStruct((1, LANES), F32),
                   jax.ShapeDtypeStruct((t // sub, 1, LANES), F32)],
        scratch_shapes=[pltpu.VMEM((1, LANES), F32)],
        compiler_params=_params("arbitrary"),
        name="merge_ln_router",
    )(o_cmp, o_sel, o_win, o_swa, gates, x, expand, w_bn, w_bs, w_o, ln_g, ln_b,
      wr_hi, wr_lo, b_r)


def _dispatch_kernel(pend_ref, padded_ref, cseg_ref, first_ref, len_ref, tot_ref,
                     h_ref, cpos_ref, xs_ref, comp0, comp1, zbuf, sem, zsem):
    j = pl.program_id(0)
    nt = pl.num_programs(0)
    tm = h_ref.shape[0]
    n_rows = xs_ref.shape[0]
    comps = (comp0, comp1)
    n_comp = comp0.shape[0]

    @pl.when(j == 0)
    def _():
        zbuf[...] = jnp.zeros(zbuf.shape, F32)
        used_rows = pend_ref[N_EXPERTS - 1]

        def zero_copy(start):
            start = pl.multiple_of(start, MOE_ROWS)
            return pltpu.make_async_copy(zbuf, xs_ref.at[pl.ds(start, MOE_ROWS), :], zsem)

        blocks = [(padded_ref[e] > 0, pend_ref[e] - MOE_ROWS) for e in range(N_EXPERTS)]
        blocks += [(used_rows + b * MOE_ROWS < n_rows, used_rows + b * MOE_ROWS)
                   for b in range(n_rows // MOE_ROWS - (nt * tm * TOP_K) // MOE_ROWS)]
        for cond, start in blocks:
            @pl.when(cond)
            def _(start=start):
                zero_copy(start).start()
        for cond, start in blocks:
            @pl.when(cond)
            def _(start=start):
                zero_copy(start).wait()

    def wait_writes(tile, slot):
        rows = pl.multiple_of(tot_ref[tile], F32_SUBLANES)
        pltpu.make_async_copy(comps[slot].at[pl.ds(0, rows), :], xs_ref.at[pl.ds(0, rows), :],
                              sem.at[slot]).wait()

    for slot in range(2):
        @pl.when((j >= 2) & ((j & 1) == slot))
        def _(slot=slot):
            wait_writes(j - 2, slot)

        @pl.when((j & 1) == slot)
        def _(slot=slot):
            row = lax.broadcasted_iota(I32, (n_comp, 1), 0)
            cpos = cpos_ref[...]
            sel = jnp.zeros((n_comp, tm), F32)
            for k in range(TOP_K):
                sel = jnp.where(row == cpos[k:k + 1, :], 1.0, sel)
            comps[slot][...] = jnp.dot(sel.astype(BF16), h_ref[...].astype(BF16),
                                       preferred_element_type=F32)
            for e in range(N_EXPERTS):
                ln = len_ref[j * N_EXPERTS + e]
                first = first_ref[j * N_EXPERTS + e]
                seg = cseg_ref[j * N_EXPERTS + e]
                for p in _run_sizes(tm):
                    done = ln & (-2 * p)
                    src = comps[slot].at[pl.ds(pl.multiple_of(seg + done, F32_SUBLANES), p), :]
                    dst = xs_ref.at[pl.ds(pl.multiple_of(first + done, F32_SUBLANES), p), :]
                    pl.when((ln & p) != 0)(pltpu.make_async_copy(src, dst, sem.at[slot]).start)

        @pl.when((j == nt - 1) & ((j & 1) == slot))
        def _(slot=slot):
            @pl.when(j >= 1)
            def _():
                wait_writes(j - 1, 1 - slot)
            wait_writes(j, slot)


def _dispatch(pend, padded, cseg, seg_first, seg_len, seg_tot, cpos_t, h, n_rows, tm):
    t, d = h.shape
    n_comp = -(-(tm * TOP_K + N_EXPERTS * (F32_SUBLANES - 1)) // MXU_DEPTH) * MXU_DEPTH
    grid_spec = pltpu.PrefetchScalarGridSpec(
        num_scalar_prefetch=6,
        grid=(t // tm,),
        in_specs=[pl.BlockSpec((tm, d), lambda i, *_: (i, 0)),
                  pl.BlockSpec((TOP_K, tm), lambda i, *_: (0, i))],
        out_specs=pl.BlockSpec(memory_space=pl.ANY),
        scratch_shapes=[pltpu.VMEM((n_comp, d), F32), pltpu.VMEM((n_comp, d), F32),
                        pltpu.VMEM((MOE_ROWS, d), F32), pltpu.SemaphoreType.DMA((2,)),
                        pltpu.SemaphoreType.DMA(())],
    )
    return pl.pallas_call(
        _dispatch_kernel,
        grid_spec=grid_spec,
        out_shape=jax.ShapeDtypeStruct((n_rows, d), F32),
        compiler_params=_params("arbitrary"),
        name="moe_dispatch",
    )(pend, padded, cseg, seg_first, seg_len, seg_tot, h, cpos_t)


def _moe_kernel(blk_e_ref, n_used_ref, first_ref, slot_ref, next_ref, x_ref, wi_hbm, bi_ref,
                wo_hbm, bo_ref, y_ref, wi_buf, wo_buf, wi_bf, wo_bf, wi_sem, wo_sem):
    f = wo_hbm.shape[1]
    b = pl.program_id(0)
    used = b < n_used_ref[0]
    e = blk_e_ref[b]
    slot = slot_ref[b]

    def fetch(expert, into):
        return (pltpu.make_async_copy(wi_hbm.at[expert], wi_buf.at[into], wi_sem.at[into]),
                pltpu.make_async_copy(wo_hbm.at[expert], wo_buf.at[into], wo_sem.at[into]))

    @pl.when(b == 0)
    def _():
        for cp in fetch(e, 0):
            cp.start()

    for s in range(2):
        @pl.when(used & (first_ref[b] == 1) & (slot == s))
        def _(s=s):
            @pl.when(next_ref[b] != e)
            def _():
                for cp in fetch(next_ref[b], 1 - s):
                    cp.start()
            for cp in fetch(e, s):
                cp.wait()
            wi_bf[...] = wi_buf[s].astype(BF16)
            wo_bf[...] = wo_buf[s].astype(BF16)

    @pl.when(used)
    def _():
        hdn = jnp.dot(x_ref[...].astype(BF16), wi_bf[...],
                      preferred_element_type=F32) + bi_ref[...]
        hg = jnp.minimum(hdn[:, :f], SWIGLU_LIMIT)
        hu = jnp.clip(hdn[:, f:], -SWIGLU_LIMIT, SWIGLU_LIMIT)
        act = hg * jax.nn.sigmoid(SWIGLU_ALPHA * hg) * (hu + 1.0)
        y_ref[...] = jnp.dot(act.astype(BF16), wo_bf[...],
                             preferred_element_type=F32) + bo_ref[...]

    @pl.when(jnp.logical_not(used))
    def _():
        y_ref[...] = jnp.zeros(y_ref.shape, F32)


def _moe_blocks(blk_e, n_used, padded, xs, w_in, b_in, w_out, b_out):
    n_rows, d = xs.shape
    e, _, f2 = w_in.shape
    f = w_out.shape[1]
    n_blk = n_rows // MOE_ROWS
    first = jnp.concatenate([jnp.ones((1,), I32), (blk_e[1:] != blk_e[:-1]).astype(I32)])
    ids = jnp.arange(e, dtype=I32)
    has_rows = padded > 0
    is_e = blk_e[:, None] == ids[None, :]
    slot = jnp.sum((ids[None, :] < blk_e[:, None]) & has_rows[None, :], axis=1) & 1
    later = jnp.where((ids[None, :] > ids[:, None]) & has_rows[None, :], ids[None, :], e)
    next_used = jnp.min(later, axis=1)
    next_used = jnp.where(next_used == e, ids, next_used).astype(I32)
    next_blk = jnp.sum(jnp.where(is_e, next_used[None, :], 0), axis=1).astype(I32)
    grid_spec = pltpu.PrefetchScalarGridSpec(
        num_scalar_prefetch=5,
        grid=(n_blk,),
        in_specs=[pl.BlockSpec((MOE_ROWS, d), lambda b, be, nu, *_: (jnp.minimum(b, nu[0] - 1), 0)),
                  pl.BlockSpec(memory_space=pl.ANY),
                  pl.BlockSpec((None, 1, f2), lambda b, be, *_: (be[b], 0, 0)),
                  pl.BlockSpec(memory_space=pl.ANY),
                  pl.BlockSpec((None, 1, d), lambda b, be, *_: (be[b], 0, 0))],
        out_specs=pl.BlockSpec((MOE_ROWS, d), lambda b, *_: (b, 0)),
        scratch_shapes=[pltpu.VMEM((2, d, f2), F32), pltpu.VMEM((2, f, d), F32),
                        pltpu.VMEM((d, f2), BF16), pltpu.VMEM((f, d), BF16),
                        pltpu.SemaphoreType.DMA((2,)), pltpu.SemaphoreType.DMA((2,))],
    )
    return pl.pallas_call(
        _moe_kernel,
        grid_spec=grid_spec,
        out_shape=jax.ShapeDtypeStruct((n_rows, d), F32),
        compiler_params=_params("arbitrary"),
        name="moe_experts",
    )(blk_e, n_used, first, slot.astype(I32), next_blk, xs, w_in, b_in, w_out, b_out)


def _run_sizes(tm):
    sizes, p = [], F32_SUBLANES
    while p <= tm:
        sizes.append(p)
        p *= 2
    return tuple(reversed(sizes))


def _stage_rows(tm):
    rows = tm * TOP_K + N_EXPERTS * 2 * (F32_SUBLANES - 1)
    return -(-rows // MXU_DEPTH) * MXU_DEPTH


def _final_kernel(start_ref, len_ref, seg_ref, tot_ref, h_ref, tg_ref, sp_ref, ys_ref, g_ref,
                  b_ref, o_ref, stage0, stage1, sem, *, alpha):
    i = pl.program_id(0)
    tm = h_ref.shape[0]
    stages = (stage0, stage1)
    n_stage = stage0.shape[0]

    def fetch(tile, into):
        for e in range(N_EXPERTS):
            ln = len_ref[tile * N_EXPERTS + e]
            first = start_ref[tile * N_EXPERTS + e]
            seg = seg_ref[tile * N_EXPERTS + e]
            for p in _run_sizes(tm):
                done = ln & (-2 * p)
                src = ys_ref.at[pl.ds(pl.multiple_of(first + done, F32_SUBLANES), p), :]
                dst = stages[into].at[pl.ds(pl.multiple_of(seg + done, F32_SUBLANES), p), :]
                pl.when((ln & p) != 0)(pltpu.make_async_copy(src, dst, sem.at[into]).start)

    @pl.when(i == 0)
    def _():
        stage0[...] = jnp.zeros(stage0.shape, F32)
        stage1[...] = jnp.zeros(stage1.shape, F32)
        fetch(0, 0)

    for slot in range(2):
        @pl.when((i & 1) == slot)
        def _(slot=slot):
            fetch(i + 1, 1 - slot)
            rows = pl.multiple_of(tot_ref[i], F32_SUBLANES)
            pltpu.make_async_copy(ys_ref.at[pl.ds(0, rows), :],
                                  stages[slot].at[pl.ds(0, rows), :], sem.at[slot]).wait()
            col = lax.broadcasted_iota(I32, (1, n_stage), 1)
            tg = tg_ref[...]
            sp = sp_ref[...]
            q = jnp.zeros((tm, n_stage), F32)
            for k in range(TOP_K):
                q = jnp.where(col == sp[:, k:k + 1], tg[:, k:k + 1], q)
            f = _split_dot(q, stages[slot][...].astype(BF16))
            o_ref[...] = _layer_norm(alpha * h_ref[...] + f, g_ref[...], b_ref[...])


def _final(run_start, run_len, run_seg, run_tot, h, top_g, spos, ys, ln_g, ln_b, alpha, tm,
           seq):
    t, d = h.shape
    spb = seq // tm
    grid_spec = pltpu.PrefetchScalarGridSpec(
        num_scalar_prefetch=4,
        grid=(t // tm,),
        in_specs=[pl.BlockSpec((tm, d), lambda i, *_: (i, 0)),
                  pl.BlockSpec((tm, LANES), lambda i, *_: (i, 0)),
                  pl.BlockSpec((tm, TOP_K), lambda i, *_: (i, 0)),
                  pl.BlockSpec(memory_space=pl.ANY),
                  pl.BlockSpec(ln_g.shape, lambda i, *_: (0, 0)),
                  pl.BlockSpec(ln_b.shape, lambda i, *_: (0, 0))],
        out_specs=pl.BlockSpec((None, tm, d), lambda i, *_: (i // spb, i % spb, 0)),
        scratch_shapes=[pltpu.VMEM((_stage_rows(tm), d), F32),
                        pltpu.VMEM((_stage_rows(tm), d), F32), pltpu.SemaphoreType.DMA((2,))],
    )
    return pl.pallas_call(
        functools.partial(_final_kernel, alpha=alpha),
        grid_spec=grid_spec,
        out_shape=jax.ShapeDtypeStruct((t // seq, seq, d), F32),
        compiler_params=_params("arbitrary"),
        name="combine_ln",
    )(run_start, run_len, run_seg, run_tot, h, top_g, spos, ys, ln_g, ln_b)


def _rope_tables(s):
    half = HEAD_DIM // 2
    inv = ROPE_THETA ** (-np.arange(half, dtype=np.float64) / half)
    ang = np.arange(s, dtype=np.float64)[:, None] * inv[None, :]
    cos, sin = np.cos(ang).astype(np.float32), np.sin(ang).astype(np.float32)
    reps = LANES // HEAD_DIM
    cos_t = np.tile(np.concatenate([cos, cos], axis=1), (1, reps))
    sin_t = np.tile(np.concatenate([-sin, sin], axis=1), (1, reps))
    return jnp.asarray(cos_t), jnp.asarray(sin_t)


def _ceil_to(v, m):
    return (v + m - 1) // m * m


def _moe_plan(top_e, rank, tile_base, counts, t, tm, dt):
    a = t * TOP_K
    per, nd = dt // tm, t // dt
    d_base = tile_base[::per]
    d_cnt = jnp.concatenate([d_base[1:], counts[None, :]], axis=0) - d_base
    seg_len = _ceil_to(d_cnt, F32_SUBLANES)
    padded = _ceil_to(jnp.sum(seg_len, axis=0), MOE_ROWS)
    pend = jnp.cumsum(padded)
    seg_first = (pend - padded)[None, :] + jnp.cumsum(seg_len, axis=0) - seg_len
    cseg = jnp.cumsum(seg_len, axis=1) - seg_len
    onehot = (top_e[:, :, None] == jnp.arange(N_EXPERTS, dtype=I32)).reshape(
        nd, dt, TOP_K, N_EXPERTS)

    def pick(table):
        return jnp.sum(jnp.where(onehot, table[:, None, None, :], 0), axis=3).reshape(t, TOP_K)

    local = rank - pick(d_base)
    dest = pick(seg_first) + local
    cpos = pick(cseg) + local
    run_first = jnp.repeat(seg_first - d_base, per, axis=0) + tile_base
    n_blk = -(-(a + nd * N_EXPERTS * (F32_SUBLANES - 1)) // MOE_ROWS) + N_EXPERTS
    blk_first = jnp.arange(n_blk, dtype=I32) * MOE_ROWS
    blk_e = jnp.minimum(jnp.sum(pend[None, :] <= blk_first[:, None], axis=1),
                        N_EXPERTS - 1).astype(I32)
    n_used = (pend[-1] // MOE_ROWS).astype(I32).reshape(1)
    flat = lambda v: v.astype(I32).reshape(-1)
    segments = (flat(cseg), flat(seg_first), flat(seg_len), jnp.sum(seg_len, axis=1).astype(I32))
    return (segments, dest.astype(I32), cpos.astype(I32), run_first, blk_e, n_used,
            n_blk * MOE_ROWS, pend.astype(I32), padded.astype(I32))


def _combine_plan(top_e, dest, run_first, tile_base, counts, tm):
    cnt = jnp.concatenate([tile_base[1:], counts[None, :]], axis=0) - tile_base
    lead = run_first & (F32_SUBLANES - 1)
    run_len = jnp.where(cnt > 0, _ceil_to(lead + cnt, F32_SUBLANES), 0)
    run_seg = jnp.cumsum(run_len, axis=1) - run_len
    shift = (run_seg + lead - run_first)[:, None, None, :]
    onehot = (top_e[:, :, None] == jnp.arange(N_EXPERTS, dtype=I32)).reshape(
        -1, tm, TOP_K, N_EXPERTS)
    spos = jnp.sum(jnp.where(onehot, shift, 0), axis=3).reshape(dest.shape) + dest
    flat = lambda v: jnp.pad(v.astype(I32), ((0, 1), (0, 0))).reshape(-1)
    run_tot = jnp.sum(run_len, axis=1).astype(I32)
    return flat(run_first - lead), flat(run_len), flat(run_seg), run_tot, spos.astype(I32)


def _layer(x, w_in, k_pe, k_w1, k_w2, v_pe, v_w1, v_w2, sinks, w_br_nsa, w_br_swa, w_out,
           ln1_g, ln1_b, w_router, b_router, w_e_in, b_e_in, w_e_out, b_e_out, ln2_g, ln2_b,
           alpha):
    b, s, d = x.shape
    t = b * s
    qb = Q_BLOCK
    nq_n, nkv = NSA_HEADS * HEAD_DIM, NSA_KV * HEAD_DIM
    nq_s, nkv_s = SWA_HEADS * HEAD_DIM, SWA_KV * HEAD_DIM
    widths = (nq_n, nkv, nkv, nkv, nkv, nkv, nkv, NSA_HEADS * 3, nq_s, nkv_s, nkv_s, 2 * d)
    offs = [0]
    for w in widths:
        offs.append(offs[-1] + w)
    col = lambda j: w_in[:, offs[j]:offs[j + 1]]
    (c_qn, c_kc, c_vc, c_ks, c_vs, c_kw, c_vw, c_gn, c_qs, c_k_s, c_v_s, c_gm) = map(col, range(12))
    w_rope = jnp.concatenate([c_qn, c_qs, c_ks, c_kw, c_k_s], axis=1).astype(BF16)
    w_plain = jnp.concatenate([c_kc, c_vc, c_vs, c_vw, c_v_s], axis=1).astype(BF16)
    gn_pad = LANES - NSA_HEADS * 3
    w_gate = jnp.concatenate([c_gm, c_gn, jnp.zeros((d, gn_pad), F32)], axis=1).astype(BF16)
    cos_t, sin_t = _rope_tables(s)

    qn_rot, qs_rot, kk_rot, qn_raw, plain, gates = _project(
        x, w_rope, w_plain, w_gate, cos_t, sin_t, min(256, s))

    nc = (s - CMP_BLOCK) // CMP_STRIDE + 1
    ncp = s // CMP_STRIDE
    half = CMP_STRIDE * HEAD_DIM

    def halves(cols):
        v = cols.reshape(b, s, NSA_KV, HEAD_DIM).transpose(0, 2, 1, 3)
        return v.reshape(b * NSA_KV, ncp, half)

    t2 = jnp.stack([halves(plain[:, :, 0:nkv]), halves(plain[:, :, nkv:2 * nkv])])
    t_lo = t2
    t_hi = jnp.concatenate([t2[:, :, 1:], jnp.zeros_like(t2[:, :, :1])], axis=2)
    pe2 = jnp.stack([k_pe.reshape(2, half), v_pe.reshape(2, half)])
    w1 = jnp.stack([k_w1, v_w1]).astype(BF16)
    w2 = jnp.stack([k_w2, v_w2]).astype(BF16)
    kvc = _compress(t_lo, t_hi, pe2, w1, w2).reshape(2, b, NSA_KV, ncp, HEAD_DIM)

    nsel = s // SEL_BLOCK
    nselp = -(-nsel // LANES) * LANES
    cstart = np.arange(ncp) * CMP_STRIDE
    sstart = np.arange(nselp) * SEL_BLOCK
    overlap = ((cstart[:, None] < sstart[None, :] + SEL_BLOCK)
               & (cstart[:, None] + CMP_BLOCK > sstart[None, :])
               & (np.arange(ncp)[:, None] < nc) & (np.arange(nselp)[None, :] < nsel))
    o_cmp, notsel = _cmp_attention(qn_raw, kvc[0], kvc[1],
                                   jnp.asarray(overlap.T.astype(BF16)),
                                   min(CMP_STEP_QUERIES, s))

    def group_major(cols):
        return cols.reshape(b, s, NSA_KV, HEAD_DIM).transpose(0, 2, 1, 3)

    k_sel = group_major(kk_rot[:, :, 0:nkv])
    onehot = (np.arange(s)[:, None] // SEL_BLOCK == np.arange(nselp)[None, :])
    k_tail = np.concatenate([np.zeros((s, LANES - HEAD_DIM), np.float32),
                             np.where(onehot, SEL_PENALTY, 0.0).astype(np.float32)], axis=1)
    k_aug = jnp.concatenate(
        [k_sel, jnp.broadcast_to(jnp.asarray(k_tail.astype(BF16)), (b, NSA_KV) + k_tail.shape)],
        axis=3)
    vt_all = plain[:, :, 2 * nkv:].transpose(0, 2, 1)
    o_sel = _sel_attention(qn_rot, notsel, k_aug, vt_all,
                           min(SEL_STEP_QUERIES, s), min(SEL_KEY_TILE, s))

    o_win = _band_attention(qn_rot, kk_rot, 1, vt_all, 1, nkv, NSA_WINDOW, None, qb)
    o_swa = _band_attention(qs_rot, kk_rot, 2, vt_all, 2, nkv_s, SWA_WINDOW, sinks, qb)

    gi = np.arange(LANES)
    ci = np.arange(3 * nq_n)
    expand = jnp.asarray(((gi[:, None] // 3 == (ci[None, :] % nq_n) // HEAD_DIM)
                          & (gi[:, None] % 3 == ci[None, :] // nq_n)
                          & (gi[:, None] < NSA_HEADS * 3)).astype(BF16))
    wr_pad = jnp.pad(w_router, ((0, 0), (0, LANES - N_EXPERTS)))
    wr_hi = wr_pad.astype(BF16)
    wr_lo = (wr_pad - wr_hi.astype(F32)).astype(BF16)
    b_r = jnp.concatenate([b_router, jnp.full((LANES - N_EXPERTS,), -jnp.inf, F32)]).reshape(1, LANES)
    tm = min(256, t)
    h, top_e, top_g, rank, counts, tile_base = _merge(
        o_cmp, o_sel, o_win, o_swa, gates, x, expand, w_br_nsa.astype(BF16),
        w_br_swa.astype(BF16), w_out.astype(BF16), ln1_g.reshape(1, d), ln1_b.reshape(1, d),
        wr_hi, wr_lo, b_r, alpha, tm)

    top_e, rank = top_e[:, :TOP_K], rank[:, :TOP_K]
    counts = counts[0, :N_EXPERTS].astype(I32)
    tile_base = tile_base[:, 0, :N_EXPERTS].astype(I32)
    dt = min(DISPATCH_TOKENS, t)
    segments, dest, cpos, run_first, blk_e, n_used, n_rows, pend, padded = _moe_plan(
        top_e, rank, tile_base, counts, t, tm, dt)
    xs = _dispatch(pend, padded, *segments, cpos.T, h, n_rows, dt)
    ys = _moe_blocks(blk_e, n_used, padded, xs, w_e_in, b_e_in.reshape(N_EXPERTS, 1, -1),
                     w_e_out, b_e_out.reshape(N_EXPERTS, 1, -1))
    run_start, run_len, run_seg, run_tot, spos = _combine_plan(
        top_e, dest, run_first, tile_base, counts, tm)
    return _final(run_start, run_len, run_seg, run_tot, h, top_g, spos, ys,
                  ln2_g.reshape(1, d), ln2_b.reshape(1, d), alpha, tm, s)


def kernel(x, w_in, nsa_k_pe, nsa_k_w1, nsa_k_w2, nsa_v_pe, nsa_v_w1, nsa_v_w2, swa_sinks, w_br_nsa, w_br_swa, w_out, ln1_g, ln1_b, w_router, b_router, w_expert_in, b_expert_in, w_expert_out, b_expert_out, ln2_g, ln2_b):
    depth = w_in.shape[0]
    alpha = (2.0 * depth) ** 0.25
    for l in range(depth):
        x = _layer(x, w_in[l], nsa_k_pe[l], nsa_k_w1[l], nsa_k_w2[l], nsa_v_pe[l], nsa_v_w1[l],
                   nsa_v_w2[l], swa_sinks[l], w_br_nsa[l], w_br_swa[l], w_out[l], ln1_g[l],
                   ln1_b[l], w_router[l], b_router[l], w_expert_in[l], b_expert_in[l],
                   w_expert_out[l], b_expert_out[l], ln2_g[l], ln2_b[l], alpha)
    return x
```

```python
import functools

import jax
import jax.numpy as jnp
import numpy as np
from jax import lax
from jax.experimental import pallas as pl
from jax.experimental.pallas import tpu as pltpu

BF16 = jnp.bfloat16
F32 = jnp.float32
I32 = jnp.int32

HEAD_DIM = 64
NSA_HEADS = 8
NSA_KV = 2
CMP_BLOCK = 32
CMP_STRIDE = 16
SEL_BLOCK = 64
SEL_TOPN = 16
NSA_WINDOW = 512
SWA_HEADS = 8
SWA_KV = 2
SWA_WINDOW = 128
Q_BLOCK = 128
ROPE_THETA = 10000.0
N_EXPERTS = 32
TOP_K = 4
SWIGLU_LIMIT = 7.0
SWIGLU_ALPHA = 1.702
LN_EPS = 1e-5

LANES = 128
BF16_SUBLANES = 16
F32_SUBLANES = 8
MXU_DEPTH = 256
MASKED = -1e30
M_INIT = -1e29
SEL_PENALTY = -(2.0 ** 100)
VMEM_LIMIT = 52 * 1024 * 1024
MOE_ROWS = 256
DISPATCH_TOKENS = 512
MERGE_STEP_TOKENS = 512
BAND_STEP_QUERIES = 512
CMP_STEP_QUERIES = 512
SEL_STEP_QUERIES = 128
SEL_KEY_TILE = 512

R_NSA = NSA_HEADS // NSA_KV
R_SWA = SWA_HEADS // SWA_KV
NT_DIMS = (((1,), (1,)), ((), ()))


def _params(*sem):
    return pltpu.CompilerParams(dimension_semantics=sem, vmem_limit_bytes=VMEM_LIMIT)


def _full(shape):
    n = len(shape)
    return pl.BlockSpec(shape, lambda *_: (0,) * n)


def _proj_kernel(x_ref, wr_ref, wp_ref, wg_ref, cos_ref, sin_ref,
                 qn_rot_ref, qs_rot_ref, kk_rot_ref, qn_raw_ref, plain_ref, gates_ref):
    xb = x_ref[...].astype(BF16)
    acc = jnp.dot(xb, wr_ref[...], preferred_element_type=F32)
    cos = cos_ref[...]
    sin = sin_ref[...]
    lane = lax.broadcasted_iota(I32, cos.shape, 1)
    first_half = (lane & (HEAD_DIM - 1)) < HEAD_DIM // 2

    def rope(t):
        partner = jnp.where(first_half, pltpu.roll(t, LANES - HEAD_DIM // 2, 1),
                            pltpu.roll(t, HEAD_DIM // 2, 1))
        return (t * cos + partner * sin).astype(BF16)

    nq = qn_rot_ref.shape[1] // LANES
    ns = qs_rot_ref.shape[1] // LANES
    nk = kk_rot_ref.shape[1] // LANES
    for c in range(nq):
        qn_rot_ref[:, c * LANES:(c + 1) * LANES] = rope(acc[:, c * LANES:(c + 1) * LANES])
    for c in range(ns):
        o = (nq + c) * LANES
        qs_rot_ref[:, c * LANES:(c + 1) * LANES] = rope(acc[:, o:o + LANES])
    for c in range(nk):
        o = (nq + ns + c) * LANES
        kk_rot_ref[:, c * LANES:(c + 1) * LANES] = rope(acc[:, o:o + LANES])
    qn_raw_ref[...] = acc[:, :nq * LANES].astype(BF16)
    plain_ref[...] = jnp.dot(xb, wp_ref[...], preferred_element_type=F32).astype(BF16)
    gates_ref[...] = jnp.dot(xb, wg_ref[...], preferred_element_type=F32).astype(gates_ref.dtype)


def _project(x, w_rope, w_plain, w_gate, cos_t, sin_t, tm):
    b, seq, d = x.shape
    t = b * seq
    nr, npl, ng = w_rope.shape[1], w_plain.shape[1], w_gate.shape[1]
    nqn, nqs = NSA_HEADS * HEAD_DIM, SWA_HEADS * HEAD_DIM
    nkk = nr - nqn - nqs
    spb = seq // tm
    row = lambda i: (i, 0)
    seq_blk = lambda n: pl.BlockSpec((None, tm, n), lambda i: (i // spb, i % spb, 0))
    seq_out = lambda n: jax.ShapeDtypeStruct((b, seq, n), BF16)
    return pl.pallas_call(
        _proj_kernel,
        grid=(t // tm,),
        in_specs=[seq_blk(d), _full(w_rope.shape), _full(w_plain.shape), _full(w_gate.shape),
                  pl.BlockSpec((tm, LANES), lambda i: (i % spb, 0)),
                  pl.BlockSpec((tm, LANES), lambda i: (i % spb, 0))],
        out_specs=[seq_blk(nqn), seq_blk(nqs), seq_blk(nkk), seq_blk(nqn), seq_blk(npl),
                   pl.BlockSpec((tm, ng), row)],
        out_shape=[seq_out(nqn), seq_out(nqs), seq_out(nkk), seq_out(nqn), seq_out(npl),
                   jax.ShapeDtypeStruct((t, ng), BF16)],
        compiler_params=_params("parallel"),
        name="proj",
    )(x, w_rope, w_plain, w_gate, cos_t, sin_t)


def _compress_kernel(a_ref, b_ref, pe_ref, w1_ref, w2_ref, out_ref):
    half = a_ref.shape[1]
    a = (a_ref[...].astype(F32) + pe_ref[0:1, :]).astype(BF16)
    b = (b_ref[...].astype(F32) + pe_ref[1:2, :]).astype(BF16)
    hid = jnp.dot(a, w1_ref[0:half, :], preferred_element_type=F32)
    hid = hid + jnp.dot(b, w1_ref[half:2 * half, :], preferred_element_type=F32)
    act = jax.nn.gelu(hid).astype(BF16)
    out_ref[...] = jnp.dot(act, w2_ref[...], preferred_element_type=F32).astype(BF16)


def _compress(t_lo, t_hi, pe2, w1, w2):
    two, bg, ncp, half = t_lo.shape
    hid = w1.shape[2]
    blk = lambda shape: pl.BlockSpec((None, None) + shape, lambda j, i: (j, i, 0, 0))
    wsp = lambda shape: pl.BlockSpec((None,) + shape, lambda j, i: (j, 0, 0))
    return pl.pallas_call(
        _compress_kernel,
        grid=(two, bg),
        in_specs=[blk((ncp, half)), blk((ncp, half)), wsp((2, half)), wsp((2 * half, hid)),
                  wsp((hid, HEAD_DIM))],
        out_specs=blk((ncp, HEAD_DIM)),
        out_shape=jax.ShapeDtypeStruct((two, bg, ncp, HEAD_DIM), BF16),
        compiler_params=_params("parallel", "parallel"),
        name="compress",
    )(t_lo, t_hi, pe2, w1, w2)


def _stack_heads(q, g, r):
    return jnp.concatenate(
        [q[:, (g * r + j) * HEAD_DIM:(g * r + j + 1) * HEAD_DIM] for j in range(r)], axis=0)


def _unstack_heads_t(parts, r, qb):
    blocks = []
    for o in parts:
        for j in range(0, r, 2):
            pair = jnp.concatenate([o[:, j * qb:(j + 1) * qb], o[:, (j + 1) * qb:(j + 2) * qb]],
                                   axis=0)
            blocks.append(pair.T)
    return jnp.concatenate(blocks, axis=1)


def _topk_mask_cols(vals, k):
    n = vals.shape[0]
    row = lax.broadcasted_iota(I32, vals.shape, 0).astype(F32)
    taken = jnp.zeros(vals.shape, F32)
    work = vals
    for _ in range(k):
        mx = jnp.max(work, axis=0, keepdims=True)
        first = jnp.min(jnp.where(work == mx, row, float(n)), axis=0, keepdims=True)
        pick = row == first
        taken = jnp.where(pick, 1.0, taken)
        work = jnp.where(pick, -jnp.inf, work)
    return taken > 0.5


def _cmp_kernel(q_ref, kc_ref, vct_ref, ovt_ref, o_ref, notsel_ref):
    qb = q_ref.shape[0]
    ncp = kc_ref.shape[1]
    nselp = ovt_ref.shape[0]
    i = pl.program_id(1)
    rows = R_NSA * qb
    pos = i * qb + (lax.broadcasted_iota(I32, (1, rows), 1) & (qb - 1))
    cend = lax.broadcasted_iota(I32, (ncp, 1), 0) * CMP_STRIDE + (CMP_BLOCK - 1)
    bias = jnp.where(cend <= pos, 0.0, MASKED)
    live = (pos >= CMP_BLOCK - 1).astype(F32)
    q = q_ref[...]
    outs, imps = [], []
    for g in range(NSA_KV):
        qg = _stack_heads(q, g, R_NSA) * (HEAD_DIM ** -0.5)
        st = lax.dot_general(kc_ref[g], qg, NT_DIMS, preferred_element_type=F32)
        st = st + bias
        e = jnp.exp(st - jnp.max(st, axis=0, keepdims=True))
        pt = e * (live / jnp.sum(e, axis=0, keepdims=True))
        outs.append(jnp.dot(vct_ref[g], pt.astype(BF16), preferred_element_type=F32))
        psum = pt[:, 0:qb]
        for j in range(1, R_NSA):
            psum = psum + pt[:, j * qb:(j + 1) * qb]
        p_hi = psum.astype(BF16)
        p_lo = (psum - p_hi.astype(F32)).astype(BF16)
        imps.append(jnp.dot(ovt_ref[...], p_hi, preferred_element_type=F32)
                    + jnp.dot(ovt_ref[...], p_lo, preferred_element_type=F32))
    imp = jnp.concatenate(imps, axis=1)
    lane = lax.broadcasted_iota(I32, (1, NSA_KV * qb), 1)
    cur = (i * qb + (lane & (qb - 1))) >> 6
    jb = lax.broadcasted_iota(I32, (nselp, 1), 0)
    forced = (jb == 0) | (jb == cur) | (jb == cur - 1)
    imp = jnp.where(jb > cur, -1.0, jnp.where(forced, 1e6, imp))
    notsel = jnp.where(_topk_mask_cols(imp, SEL_TOPN), 0.0, 1.0)
    for g in range(NSA_KV):
        notsel_ref[g] = notsel[:, g * qb:(g + 1) * qb].T.astype(BF16)
    o_ref[...] = _unstack_heads_t(outs, R_NSA, qb).astype(o_ref.dtype)


def _cmp_attention(q_raw, kc, vc, overlap_t, qb):
    vc = vc.transpose(0, 1, 3, 2)
    b, s, hq = q_raw.shape
    _, g, ncp, dh = kc.shape
    nselp = overlap_t.shape[0]
    return pl.pallas_call(
        _cmp_kernel,
        grid=(b, s // qb),
        in_specs=[pl.BlockSpec((None, qb, hq), lambda bi, i: (bi, i, 0)),
                  pl.BlockSpec((None, g, ncp, dh), lambda bi, i: (bi, 0, 0, 0)),
                  pl.BlockSpec((None, g, dh, ncp), lambda bi, i: (bi, 0, 0, 0)),
                  _full(overlap_t.shape)],
        out_specs=[pl.BlockSpec((None, qb, hq), lambda bi, i: (bi, i, 0)),
                   pl.BlockSpec((None, g, qb, nselp), lambda bi, i: (bi, 0, i, 0))],
        out_shape=[jax.ShapeDtypeStruct((b, s, hq), BF16),
                   jax.ShapeDtypeStruct((b, g, s, nselp), BF16)],
        compiler_params=_params("parallel", "parallel"),
        name="cmp_attn",
    )(q_raw, kc, vc, overlap_t)


def _sel_kernel(q_ref, notsel_ref, k_ref, vt_ref, o_ref, *score_bufs, tk):
    s_even, s_odd = score_bufs[:NSA_KV], score_bufs[NSA_KV:]
    qb = q_ref.shape[0]
    i = pl.program_id(1)
    rows = R_NSA * qb
    qpos = i * qb + (lax.broadcasted_iota(I32, (1, rows), 1) & (qb - 1))
    n_clear = (i * qb) // tk
    q = q_ref[...]
    q_augs = []
    for g in range(NSA_KV):
        qg = _stack_heads(q, g, R_NSA) * (HEAD_DIM ** -0.5)
        q_augs.append(jnp.concatenate(
            [qg, jnp.zeros((rows, LANES - HEAD_DIM), BF16),
             jnp.concatenate([notsel_ref[g]] * R_NSA, axis=0)], axis=1))

    def scores(kt, g):
        start = pl.multiple_of(kt * tk, tk)
        return lax.dot_general(k_ref[g, pl.ds(start, tk), :], q_augs[g], NT_DIMS,
                               preferred_element_type=F32)

    def consume(kt, g, s_ref, m, acc, causal):
        start = pl.multiple_of(kt * tk, tk)
        vt_t = jnp.concatenate([vt_ref[g * HEAD_DIM:(g + 1) * HEAD_DIM, pl.ds(start, tk)],
                                jnp.ones((BF16_SUBLANES, tk), vt_ref.dtype)], axis=0)
        st = s_ref[...]
        if causal:
            kpos = start + lax.broadcasted_iota(I32, (tk, 1), 0)
            st = jnp.where(kpos <= qpos, st, MASKED)
        m_new = jnp.maximum(m, jnp.max(st, axis=0, keepdims=True))
        pt = jnp.exp(st - m_new).astype(BF16)
        acc = jnp.exp(m - m_new) * acc + jnp.dot(vt_t, pt, preferred_element_type=F32)
        return m_new, acc

    def advance(kt, carry, cur, nxt):
        new = []
        for g in range(NSA_KV):
            nxt[g][...] = scores(kt + 1, g)
            new.append(consume(kt, g, cur[g], *carry[g], False))
        return tuple(new)

    def pair(j, carry):
        carry = advance(2 * j, carry, s_even, s_odd)
        return advance(2 * j + 1, carry, s_odd, s_even)

    def finish(carry, cur):
        outs = []
        for g in range(NSA_KV):
            _, acc = consume(n_clear, g, cur[g], *carry[g], True)
            outs.append(acc[:HEAD_DIM] / acc[HEAD_DIM:HEAD_DIM + 1])
        o_ref[...] = _unstack_heads_t(outs, R_NSA, qb).astype(o_ref.dtype)

    for g in range(NSA_KV):
        s_even[g][...] = scores(0, g)
    init = tuple((jnp.full((1, rows), M_INIT, F32),
                  jnp.zeros((HEAD_DIM + BF16_SUBLANES, rows), F32))
                 for _ in range(NSA_KV))
    carry = lax.fori_loop(0, n_clear // 2, pair, init)

    @pl.when((n_clear & 1) == 0)
    def _():
        finish(carry, s_even)

    @pl.when((n_clear & 1) == 1)
    def _():
        finish(advance(n_clear - 1, carry, s_even, s_odd), s_odd)


def _sel_attention(q_rot, notsel, k_aug, vt_aug, qb, tk):
    b, s, hq = q_rot.shape
    _, g, _, kw = k_aug.shape
    nselp = notsel.shape[3]
    return pl.pallas_call(
        functools.partial(_sel_kernel, tk=tk),
        grid=(b, s // qb),
        in_specs=[pl.BlockSpec((None, qb, hq), lambda bi, i: (bi, i, 0)),
                  pl.BlockSpec((None, g, qb, nselp), lambda bi, i: (bi, 0, i, 0)),
                  pl.BlockSpec((None, g, s, kw), lambda bi, i: (bi, 0, 0, 0)),
                  pl.BlockSpec((None, g * HEAD_DIM, s), lambda bi, i: (bi, 0, 0))],
        out_specs=pl.BlockSpec((None, qb, hq), lambda bi, i: (bi, i, 0)),
        out_shape=jax.ShapeDtypeStruct((b, s, hq), BF16),
        scratch_shapes=[pltpu.VMEM((tk, R_NSA * qb), F32)] * (2 * g),
        compiler_params=_params("parallel", "parallel"),
        name="sel_attn",
    )(q_rot, notsel, k_aug, vt_aug)


def _band_kernel(*refs, window, wlen, r, kv, qb, has_sinks):
    if has_sinks:
        sink_ref, q_ref, k_ref, vt_ref, o_ref = refs
    else:
        q_ref, k_ref, vt_ref, o_ref = refs
    nsub = q_ref.shape[0] // qb
    rows = r * qb
    lane = lax.broadcasted_iota(I32, (1, rows), 1)

    def mask_bias(i):
        start = jnp.maximum((i + 1) * qb - wlen, 0)
        rel = i * qb + (lane & (qb - 1)) - start - lax.broadcasted_iota(I32, (wlen, 1), 0)
        return jnp.where((rel >= 0) & (rel < window), 0.0, MASKED)

    def body(shared_bias):
        first = pl.program_id(1) * nsub
        if shared_bias:
            bias = mask_bias(first)
        for sb in range(nsub):
            i = first + sb
            if not shared_bias:
                bias = mask_bias(i)
            start = pl.multiple_of(jnp.maximum((i + 1) * qb - wlen, 0), qb)
            q = q_ref[sb * qb:(sb + 1) * qb, :]
            kw = k_ref[pl.ds(start, wlen), :]
            vtw = vt_ref[:, pl.ds(start, wlen)]
            outs = []
            for g in range(kv):
                qg = _stack_heads(q, g, r) * (HEAD_DIM ** -0.5)
                st = lax.dot_general(kw[:, g * HEAD_DIM:(g + 1) * HEAD_DIM], qg, NT_DIMS,
                                     preferred_element_type=F32) + bias
                m = jnp.max(st, axis=0, keepdims=True)
                if has_sinks:
                    sk = jnp.full((1, rows), sink_ref[g * r], F32)
                    for j in range(1, r):
                        sk = jnp.where(lane >= j * qb, sink_ref[g * r + j], sk)
                    m = jnp.maximum(m, sk)
                e = jnp.exp(st - m).astype(BF16)
                v_ones = jnp.concatenate([vtw[g * HEAD_DIM:(g + 1) * HEAD_DIM, :],
                                          jnp.ones((BF16_SUBLANES, wlen), vtw.dtype)], axis=0)
                ot = jnp.dot(v_ones, e, preferred_element_type=F32)
                den = ot[HEAD_DIM:HEAD_DIM + 1]
                if has_sinks:
                    den = den + jnp.exp(sk - m)
                outs.append(ot[:HEAD_DIM] / den)
            o_ref[sb * qb:(sb + 1) * qb, :] = _unstack_heads_t(outs, r, qb).astype(o_ref.dtype)

    if nsub * qb >= wlen - qb:
        @pl.when(pl.program_id(1) == 0)
        def _():
            body(False)

        @pl.when(pl.program_id(1) > 0)
        def _():
            body(True)
    else:
        body(False)


def _band_attention(q_rot, k_all, k_blk, vt_all, v_blk, gk, window, sinks, qb):
    b, s, hq = q_rot.shape
    assert k_all.shape[2] % gk == 0 and vt_all.shape[1] % gk == 0
    kv = gk // HEAD_DIM
    r = hq // gk
    back = -(-window // qb)
    wlen = (back + 1) * qb
    assert wlen <= s
    has_sinks = sinks is not None
    tq = min(BAND_STEP_QUERIES, s)
    in_specs = [pl.BlockSpec((None, tq, hq), lambda bi, i: (bi, i, 0)),
                pl.BlockSpec((None, s, gk), lambda bi, i: (bi, 0, k_blk)),
                pl.BlockSpec((None, gk, s), lambda bi, i: (bi, v_blk, 0))]
    args = [q_rot, k_all, vt_all]
    if has_sinks:
        in_specs = [pl.BlockSpec(memory_space=pltpu.SMEM)] + in_specs
        args = [sinks.astype(F32)] + args
    return pl.pallas_call(
        functools.partial(_band_kernel, window=window, wlen=wlen, r=r, kv=kv, qb=qb,
                          has_sinks=has_sinks),
        grid=(b, s // tq),
        in_specs=in_specs,
        out_specs=pl.BlockSpec((None, tq, hq), lambda bi, i: (bi, i, 0)),
        out_shape=jax.ShapeDtypeStruct((b, s, hq), BF16),
        compiler_params=_params("parallel", "parallel"),
        name="band_attn_sink" if has_sinks else "band_attn",
    )(*args)


def _layer_norm(v, g, b):
    mu = jnp.mean(v, axis=1, keepdims=True)
    c = v - mu
    var = jnp.mean(c * c, axis=1, keepdims=True)
    return c * lax.rsqrt(var + LN_EPS) * g + b


def _split_dot(a, w):
    hi = a.astype(BF16)
    lo = (a - hi.astype(F32)).astype(BF16)
    return (jnp.dot(hi, w, preferred_element_type=F32)
            + jnp.dot(lo, w, preferred_element_type=F32))


def _merge_kernel(ocmp_ref, osel_ref, owin_ref, oswa_ref, gates_ref, x_ref, exp_ref,
                  wbn_ref, wbs_ref, wo_ref, lng_ref, lnb_ref, wrh_ref, wrl_ref, br_ref,
                  h_ref, te_ref, tg_ref, tr_ref, counts_ref, base_ref, cnt_ref, *, alpha, sub):
    d = x_ref.shape[1]
    hq = ocmp_ref.shape[1]
    n_sub = x_ref.shape[0] // sub
    col = lax.broadcasted_iota(I32, (sub, LANES), 1).astype(F32)

    def route(rows):
        gates = gates_ref[rows, :].astype(F32)
        gn = jax.nn.sigmoid(gates[:, 2 * d:])
        gexp = _split_dot(gn, exp_ref[...])
        o_nsa = (gexp[:, 0:hq] * ocmp_ref[rows, :] + gexp[:, hq:2 * hq] * osel_ref[rows, :]
                 + gexp[:, 2 * hq:3 * hq] * owin_ref[rows, :])
        y_nsa = jnp.dot(o_nsa.astype(BF16), wbn_ref[...], preferred_element_type=F32)
        y_swa = jnp.dot(oswa_ref[rows, :].astype(BF16), wbs_ref[...],
                        preferred_element_type=F32)
        gm = jax.nn.sigmoid(gates[:, :2 * d])
        mixed = gm[:, :d] * y_nsa + gm[:, d:] * y_swa
        z = jnp.dot(mixed.astype(BF16), wo_ref[...], preferred_element_type=F32)
        h = _layer_norm(alpha * x_ref[rows, :] + z, lng_ref[...], lnb_ref[...])
        h_ref[rows, :] = h
        h_hi = h.astype(BF16)
        h_lo = (h - h_hi.astype(F32)).astype(BF16)
        logits = (jnp.dot(h_hi, wrh_ref[...], preferred_element_type=F32)
                  + jnp.dot(h_lo, wrh_ref[...], preferred_element_type=F32)
                  + jnp.dot(h_hi, wrl_ref[...], preferred_element_type=F32)) + br_ref[...]
        work = logits
        vals, ids = [], []
        for _ in range(TOP_K):
            mx = jnp.max(work, axis=1, keepdims=True)
            first = jnp.min(jnp.where(work == mx, col, float(LANES)), axis=1, keepdims=True)
            vals.append(mx)
            ids.append(first)
            work = jnp.where(col == first, -jnp.inf, work)
        es = [jnp.exp(v - vals[0]) for v in vals]
        den = es[0]
        for e in es[1:]:
            den = den + e
        hits = jnp.zeros(logits.shape, F32)
        te = jnp.zeros(logits.shape, F32)
        tg = jnp.zeros(logits.shape, F32)
        for k in range(TOP_K):
            hits = jnp.where(col == ids[k], 1.0, hits)
            te = jnp.where(col == float(k), ids[k], te)
            tg = jnp.where(col == float(k), es[k] / den, tg)
        te_ref[rows, :] = te.astype(I32)
        tg_ref[rows, :] = tg
        earlier = (lax.broadcasted_iota(I32, (sub, sub), 1)
                   < lax.broadcasted_iota(I32, (sub, sub), 0))
        prefix = jnp.dot(jnp.where(earlier, 1.0, 0.0).astype(BF16), hits.astype(BF16),
                         preferred_element_type=F32)
        return ids, prefix, jnp.sum(hits, axis=0, keepdims=True)

    routed = [route(slice(j * sub, (j + 1) * sub)) for j in range(n_sub)]

    @pl.when(pl.program_id(0) == 0)
    def _():
        cnt_ref[...] = jnp.zeros(cnt_ref.shape, F32)

    for j, (ids, prefix, total) in enumerate(routed):
        base = cnt_ref[...]
        base_ref[j] = base
        before = prefix + base
        tr = jnp.zeros((sub, LANES), F32)
        for k in range(TOP_K):
            rank = jnp.sum(jnp.where(col == ids[k], before, 0.0), axis=1, keepdims=True)
            tr = jnp.where(col == float(k), rank, tr)
        tr_ref[j * sub:(j + 1) * sub, :] = tr.astype(I32)
        cnt_ref[...] = base + total
    counts_ref[...] = cnt_ref[...]


def _merge(o_cmp, o_sel, o_win, o_swa, gates, x, expand, w_bn, w_bs, w_o, ln_g, ln_b,
           wr_hi, wr_lo, b_r, alpha, sub):
    b, s, d = x.shape
    t = b * s
    hq = o_cmp.shape[2]
    tm = min(MERGE_STEP_TOKENS, s)
    n_sub = tm // sub
    spb = s // tm
    row = lambda i: (i, 0)
    tok = lambda n: pl.BlockSpec((tm, n), row)
    seq = lambda n: pl.BlockSpec((None, tm, n), lambda i: (i // spb, i % spb, 0))
    return pl.pallas_call(
        functools.partial(_merge_kernel, alpha=alpha, sub=sub),
        grid=(t // tm,),
        in_specs=[seq(hq), seq(hq), seq(hq), seq(hq), tok(gates.shape[1]), seq(d),
                  _full(expand.shape), _full(w_bn.shape), _full(w_bs.shape), _full(w_o.shape),
                  _full(ln_g.shape), _full(ln_b.shape), _full(wr_hi.shape), _full(wr_lo.shape),
                  _full(b_r.shape)],
        out_specs=[tok(d), tok(LANES), tok(LANES), tok(LANES), _full((1, LANES)),
                   pl.BlockSpec((n_sub, 1, LANES), lambda i: (i, 0, 0))],
        out_shape=[jax.ShapeDtypeStruct((t, d), F32), jax.ShapeDtypeStruct((t, LANES), I32),
                   jax.ShapeDtypeStruct((t, LANES), F32), jax.ShapeDtypeStruct((t, LANES), I32),
                   jax.ShapeDtypeStruct((1, LANES), F32),
                   jax.ShapeDtypeStruct((t // sub, 1, LANES), F32)],
        scratch_shapes=[pltpu.VMEM((1, LANES), F32)],
        compiler_params=_params("arbitrary"),
        name="merge_ln_router",
    )(o_cmp, o_sel, o_win, o_swa, gates, x, expand, w_bn, w_bs, w_o, ln_g, ln_b,
      wr_hi, wr_lo, b_r)


def _dispatch_kernel(pend_ref, padded_ref, cseg_ref, first_ref, len_ref, tot_ref,
                     h_ref, cpos_ref, xs_ref, comp0, comp1, zbuf, sem, zsem):
    j = pl.program_id(0)
    nt = pl.num_programs(0)
    tm = h_ref.shape[0]
    n_rows = xs_ref.shape[0]
    comps = (comp0, comp1)
    n_comp = comp0.shape[0]

    @pl.when(j == 0)
    def _():
        zbuf[...] = jnp.zeros(zbuf.shape, F32)
        used_rows = pend_ref[N_EXPERTS - 1]

        def zero_copy(start):
            start = pl.multiple_of(start, MOE_ROWS)
            return pltpu.make_async_copy(zbuf, xs_ref.at[pl.ds(start, MOE_ROWS), :], zsem)

        blocks = [(padded_ref[e] > 0, pend_ref[e] - MOE_ROWS) for e in range(N_EXPERTS)]
        blocks += [(used_rows + b * MOE_ROWS < n_rows, used_rows + b * MOE_ROWS)
                   for b in range(n_rows // MOE_ROWS - (nt * tm * TOP_K) // MOE_ROWS)]
        for cond, start in blocks:
            @pl.when(cond)
            def _(start=start):
                zero_copy(start).start()
        for cond, start in blocks:
            @pl.when(cond)
            def _(start=start):
                zero_copy(start).wait()

    def wait_writes(tile, slot):
        rows = pl.multiple_of(tot_ref[tile], F32_SUBLANES)
        pltpu.make_async_copy(comps[slot].at[pl.ds(0, rows), :], xs_ref.at[pl.ds(0, rows), :],
                              sem.at[slot]).wait()

    for slot in range(2):
        @pl.when((j >= 2) & ((j & 1) == slot))
        def _(slot=slot):
            wait_writes(j - 2, slot)

        @pl.when((j & 1) == slot)
        def _(slot=slot):
            row = lax.broadcasted_iota(I32, (n_comp, 1), 0)
            cpos = cpos_ref[...]
            sel = jnp.zeros((n_comp, tm), F32)
            for k in range(TOP_K):
                sel = jnp.where(row == cpos[k:k + 1, :], 1.0, sel)
            comps[slot][...] = jnp.dot(sel.astype(BF16), h_ref[...].astype(BF16),
                                       preferred_element_type=F32)
            for e in range(N_EXPERTS):
                ln = len_ref[j * N_EXPERTS + e]
                first = first_ref[j * N_EXPERTS + e]
                seg = cseg_ref[j * N_EXPERTS + e]
                for p in _run_sizes(tm):
                    done = ln & (-2 * p)
                    src = comps[slot].at[pl.ds(pl.multiple_of(seg + done, F32_SUBLANES), p), :]
                    dst = xs_ref.at[pl.ds(pl.multiple_of(first + done, F32_SUBLANES), p), :]
                    pl.when((ln & p) != 0)(pltpu.make_async_copy(src, dst, sem.at[slot]).start)

        @pl.when((j == nt - 1) & ((j & 1) == slot))
        def _(slot=slot):
            @pl.when(j >= 1)
            def _():
                wait_writes(j - 1, 1 - slot)
            wait_writes(j, slot)


def _dispatch(pend, padded, cseg, seg_first, seg_len, seg_tot, cpos_t, h, n_rows, tm):
    t, d = h.shape
    n_comp = -(-(tm * TOP_K + N_EXPERTS * (F32_SUBLANES - 1)) // MXU_DEPTH) * MXU_DEPTH
    grid_spec = pltpu.PrefetchScalarGridSpec(
        num_scalar_prefetch=6,
        grid=(t // tm,),
        in_specs=[pl.BlockSpec((tm, d), lambda i, *_: (i, 0)),
                  pl.BlockSpec((TOP_K, tm), lambda i, *_: (0, i))],
        out_specs=pl.BlockSpec(memory_space=pl.ANY),
        scratch_shapes=[pltpu.VMEM((n_comp, d), F32), pltpu.VMEM((n_comp, d), F32),
                        pltpu.VMEM((MOE_ROWS, d), F32), pltpu.SemaphoreType.DMA((2,)),
                        pltpu.SemaphoreType.DMA(())],
    )
    return pl.pallas_call(
        _dispatch_kernel,
        grid_spec=grid_spec,
        out_shape=jax.ShapeDtypeStruct((n_rows, d), F32),
        compiler_params=_params("arbitrary"),
        name="moe_dispatch",
    )(pend, padded, cseg, seg_first, seg_len, seg_tot, h, cpos_t)


def _moe_kernel(blk_e_ref, n_used_ref, first_ref, slot_ref, next_ref, x_ref, wi_hbm, bi_ref,
                wo_hbm, bo_ref, y_ref, wi_buf, wo_buf, wi_bf, wo_bf, wi_sem, wo_sem):
    f = wo_hbm.shape[1]
    b = pl.program_id(0)
    used = b < n_used_ref[0]
    e = blk_e_ref[b]
    slot = slot_ref[b]

    def fetch(expert, into):
        return (pltpu.make_async_copy(wi_hbm.at[expert], wi_buf.at[into], wi_sem.at[into]),
                pltpu.make_async_copy(wo_hbm.at[expert], wo_buf.at[into], wo_sem.at[into]))

    @pl.when(b == 0)
    def _():
        for cp in fetch(e, 0):
            cp.start()

    for s in range(2):
        @pl.when(used & (first_ref[b] == 1) & (slot == s))
        def _(s=s):
            @pl.when(next_ref[b] != e)
            def _():
                for cp in fetch(next_ref[b], 1 - s):
                    cp.start()
            for cp in fetch(e, s):
                cp.wait()
            wi_bf[...] = wi_buf[s].astype(BF16)
            wo_bf[...] = wo_buf[s].astype(BF16)

    @pl.when(used)
    def _():
        hdn = jnp.dot(x_ref[...].astype(BF16), wi_bf[...],
                      preferred_element_type=F32) + bi_ref[...]
        hg = jnp.minimum(hdn[:, :f], SWIGLU_LIMIT)
        hu = jnp.clip(hdn[:, f:], -SWIGLU_LIMIT, SWIGLU_LIMIT)
        act = hg * jax.nn.sigmoid(SWIGLU_ALPHA * hg) * (hu + 1.0)
        y_ref[...] = jnp.dot(act.astype(BF16), wo_bf[...],
                             preferred_element_type=F32) + bo_ref[...]

    @pl.when(jnp.logical_not(used))
    def _():
        y_ref[...] = jnp.zeros(y_ref.shape, F32)


def _moe_blocks(blk_e, n_used, padded, xs, w_in, b_in, w_out, b_out):
    n_rows, d = xs.shape
    e, _, f2 = w_in.shape
    f = w_out.shape[1]
    n_blk = n_rows // MOE_ROWS
    first = jnp.concatenate([jnp.ones((1,), I32), (blk_e[1:] != blk_e[:-1]).astype(I32)])
    ids = jnp.arange(e, dtype=I32)
    has_rows = padded > 0
    is_e = blk_e[:, None] == ids[None, :]
    slot = jnp.sum((ids[None, :] < blk_e[:, None]) & has_rows[None, :], axis=1) & 1
    later = jnp.where((ids[None, :] > ids[:, None]) & has_rows[None, :], ids[None, :], e)
    next_used = jnp.min(later, axis=1)
    next_used = jnp.where(next_used == e, ids, next_used).astype(I32)
    next_blk = jnp.sum(jnp.where(is_e, next_used[None, :], 0), axis=1).astype(I32)
    grid_spec = pltpu.PrefetchScalarGridSpec(
        num_scalar_prefetch=5,
        grid=(n_blk,),
        in_specs=[pl.BlockSpec((MOE_ROWS, d), lambda b, be, nu, *_: (jnp.minimum(b, nu[0] - 1), 0)),
                  pl.BlockSpec(memory_space=pl.ANY),
                  pl.BlockSpec((None, 1, f2), lambda b, be, *_: (be[b], 0, 0)),
                  pl.BlockSpec(memory_space=pl.ANY),
                  pl.BlockSpec((None, 1, d), lambda b, be, *_: (be[b], 0, 0))],
        out_specs=pl.BlockSpec((MOE_ROWS, d), lambda b, *_: (b, 0)),
        scratch_shapes=[pltpu.VMEM((2, d, f2), F32), pltpu.VMEM((2, f, d), F32),
                        pltpu.VMEM((d, f2), BF16), pltpu.VMEM((f, d), BF16),
                        pltpu.SemaphoreType.DMA((2,)), pltpu.SemaphoreType.DMA((2,))],
    )
    return pl.pallas_call(
        _moe_kernel,
        grid_spec=grid_spec,
        out_shape=jax.ShapeDtypeStruct((n_rows, d), F32),
        compiler_params=_params("arbitrary"),
        name="moe_experts",
    )(blk_e, n_used, first, slot.astype(I32), next_blk, xs, w_in, b_in, w_out, b_out)


def _run_sizes(tm):
    sizes, p = [], F32_SUBLANES
    while p <= tm:
        sizes.append(p)
        p *= 2
    return tuple(reversed(sizes))


def _stage_rows(tm):
    rows = tm * TOP_K + N_EXPERTS * 2 * (F32_SUBLANES - 1)
    return -(-rows // MXU_DEPTH) * MXU_DEPTH


def _final_kernel(start_ref, len_ref, seg_ref, tot_ref, h_ref, tg_ref, sp_ref, ys_ref, g_ref,
                  b_ref, o_ref, stage0, stage1, sem, *, alpha):
    i = pl.program_id(0)
    tm = h_ref.shape[0]
    stages = (stage0, stage1)
    n_stage = stage0.shape[0]

    def fetch(tile, into):
        for e in range(N_EXPERTS):
            ln = len_ref[tile * N_EXPERTS + e]
            first = start_ref[tile * N_EXPERTS + e]
            seg = seg_ref[tile * N_EXPERTS + e]
            for p in _run_sizes(tm):
                done = ln & (-2 * p)
                src = ys_ref.at[pl.ds(pl.multiple_of(first + done, F32_SUBLANES), p), :]
                dst = stages[into].at[pl.ds(pl.multiple_of(seg + done, F32_SUBLANES), p), :]
                pl.when((ln & p) != 0)(pltpu.make_async_copy(src, dst, sem.at[into]).start)

    @pl.when(i == 0)
    def _():
        stage0[...] = jnp.zeros(stage0.shape, F32)
        stage1[...] = jnp.zeros(stage1.shape, F32)
        fetch(0, 0)

    for slot in range(2):
        @pl.when((i & 1) == slot)
        def _(slot=slot):
            fetch(i + 1, 1 - slot)
            rows = pl.multiple_of(tot_ref[i], F32_SUBLANES)
            pltpu.make_async_copy(ys_ref.at[pl.ds(0, rows), :],
                                  stages[slot].at[pl.ds(0, rows), :], sem.at[slot]).wait()
            col = lax.broadcasted_iota(I32, (1, n_stage), 1)
            tg = tg_ref[...]
            sp = sp_ref[...]
            q = jnp.zeros((tm, n_stage), F32)
            for k in range(TOP_K):
                q = jnp.where(col == sp[:, k:k + 1], tg[:, k:k + 1], q)
            f = _split_dot(q, stages[slot][...].astype(BF16))
            o_ref[...] = _layer_norm(alpha * h_ref[...] + f, g_ref[...], b_ref[...])


def _final(run_start, run_len, run_seg, run_tot, h, top_g, spos, ys, ln_g, ln_b, alpha, tm,
           seq):
    t, d = h.shape
    spb = seq // tm
    grid_spec = pltpu.PrefetchScalarGridSpec(
        num_scalar_prefetch=4,
        grid=(t // tm,),
        in_specs=[pl.BlockSpec((tm, d), lambda i, *_: (i, 0)),
                  pl.BlockSpec((tm, LANES), lambda i, *_: (i, 0)),
                  pl.BlockSpec((tm, TOP_K), lambda i, *_: (i, 0)),
                  pl.BlockSpec(memory_space=pl.ANY),
                  pl.BlockSpec(ln_g.shape, lambda i, *_: (0, 0)),
                  pl.BlockSpec(ln_b.shape, lambda i, *_: (0, 0))],
        out_specs=pl.BlockSpec((None, tm, d), lambda i, *_: (i // spb, i % spb, 0)),
        scratch_shapes=[pltpu.VMEM((_stage_rows(tm), d), F32),
                        pltpu.VMEM((_stage_rows(tm), d), F32), pltpu.SemaphoreType.DMA((2,))],
    )
    return pl.pallas_call(
        functools.partial(_final_kernel, alpha=alpha),
        grid_spec=grid_spec,
        out_shape=jax.ShapeDtypeStruct((t // seq, seq, d), F32),
        compiler_params=_params("arbitrary"),
        name="combine_ln",
    )(run_start, run_len, run_seg, run_tot, h, top_g, spos, ys, ln_g, ln_b)


def _rope_tables(s):
    half = HEAD_DIM // 2
    inv = ROPE_THETA ** (-np.arange(half, dtype=np.float64) / half)
    ang = np.arange(s, dtype=np.float64)[:, None] * inv[None, :]
    cos, sin = np.cos(ang).astype(np.float32), np.sin(ang).astype(np.float32)
    reps = LANES // HEAD_DIM
    cos_t = np.tile(np.concatenate([cos, cos], axis=1), (1, reps))
    sin_t = np.tile(np.concatenate([-sin, sin], axis=1), (1, reps))
    return jnp.asarray(cos_t), jnp.asarray(sin_t)


def _ceil_to(v, m):
    return (v + m - 1) // m * m


def _moe_plan(top_e, rank, tile_base, counts, t, tm, dt):
    a = t * TOP_K
    per, nd = dt // tm, t // dt
    d_base = tile_base[::per]
    d_cnt = jnp.concatenate([d_base[1:], counts[None, :]], axis=0) - d_base
    seg_len = _ceil_to(d_cnt, F32_SUBLANES)
    padded = _ceil_to(jnp.sum(seg_len, axis=0), MOE_ROWS)
    pend = jnp.cumsum(padded)
    seg_first = (pend - padded)[None, :] + jnp.cumsum(seg_len, axis=0) - seg_len
    cseg = jnp.cumsum(seg_len, axis=1) - seg_len
    onehot = (top_e[:, :, None] == jnp.arange(N_EXPERTS, dtype=I32)).reshape(
        nd, dt, TOP_K, N_EXPERTS)

    def pick(table):
        return jnp.sum(jnp.where(onehot, table[:, None, None, :], 0), axis=3).reshape(t, TOP_K)

    local = rank - pick(d_base)
    dest = pick(seg_first) + local
    cpos = pick(cseg) + local
    run_first = jnp.repeat(seg_first - d_base, per, axis=0) + tile_base
    n_blk = -(-(a + nd * N_EXPERTS * (F32_SUBLANES - 1)) // MOE_ROWS) + N_EXPERTS
    blk_first = jnp.arange(n_blk, dtype=I32) * MOE_ROWS
    blk_e = jnp.minimum(jnp.sum(pend[None, :] <= blk_first[:, None], axis=1),
                        N_EXPERTS - 1).astype(I32)
    n_used = (pend[-1] // MOE_ROWS).astype(I32).reshape(1)
    flat = lambda v: v.astype(I32).reshape(-1)
    segments = (flat(cseg), flat(seg_first), flat(seg_len), jnp.sum(seg_len, axis=1).astype(I32))
    return (segments, dest.astype(I32), cpos.astype(I32), run_first, blk_e, n_used,
            n_blk * MOE_ROWS, pend.astype(I32), padded.astype(I32))


def _combine_plan(top_e, dest, run_first, tile_base, counts, tm):
    cnt = jnp.concatenate([tile_base[1:], counts[None, :]], axis=0) - tile_base
    lead = run_first & (F32_SUBLANES - 1)
    run_len = jnp.where(cnt > 0, _ceil_to(lead + cnt, F32_SUBLANES), 0)
    run_seg = jnp.cumsum(run_len, axis=1) - run_len
    shift = (run_seg + lead - run_first)[:, None, None, :]
    onehot = (top_e[:, :, None] == jnp.arange(N_EXPERTS, dtype=I32)).reshape(
        -1, tm, TOP_K, N_EXPERTS)
    spos = jnp.sum(jnp.where(onehot, shift, 0), axis=3).reshape(dest.shape) + dest
    flat = lambda v: jnp.pad(v.astype(I32), ((0, 1), (0, 0))).reshape(-1)
    run_tot = jnp.sum(run_len, axis=1).astype(I32)
    return flat(run_first - lead), flat(run_len), flat(run_seg), run_tot, spos.astype(I32)


def _layer(x, w_in, k_pe, k_w1, k_w2, v_pe, v_w1, v_w2, sinks, w_br_nsa, w_br_swa, w_out,
           ln1_g, ln1_b, w_router, b_router, w_e_in, b_e_in, w_e_out, b_e_out, ln2_g, ln2_b,
           alpha):
    b, s, d = x.shape
    t = b * s
    qb = Q_BLOCK
    nq_n, nkv = NSA_HEADS * HEAD_DIM, NSA_KV * HEAD_DIM
    nq_s, nkv_s = SWA_HEADS * HEAD_DIM, SWA_KV * HEAD_DIM
    widths = (nq_n, nkv, nkv, nkv, nkv, nkv, nkv, NSA_HEADS * 3, nq_s, nkv_s, nkv_s, 2 * d)
    offs = [0]
    for w in widths:
        offs.append(offs[-1] + w)
    col = lambda j: w_in[:, offs[j]:offs[j + 1]]
    (c_qn, c_kc, c_vc, c_ks, c_vs, c_kw, c_vw, c_gn, c_qs, c_k_s, c_v_s, c_gm) = map(col, range(12))
    w_rope = jnp.concatenate([c_qn, c_qs, c_ks, c_kw, c_k_s], axis=1).astype(BF16)
    w_plain = jnp.concatenate([c_kc, c_vc, c_vs, c_vw, c_v_s], axis=1).astype(BF16)
    gn_pad = LANES - NSA_HEADS * 3
    w_gate = jnp.concatenate([c_gm, c_gn, jnp.zeros((d, gn_pad), F32)], axis=1).astype(BF16)
    cos_t, sin_t = _rope_tables(s)

    qn_rot, qs_rot, kk_rot, qn_raw, plain, gates = _project(
        x, w_rope, w_plain, w_gate, cos_t, sin_t, min(256, s))

    nc = (s - CMP_BLOCK) // CMP_STRIDE + 1
    ncp = s // CMP_STRIDE
    half = CMP_STRIDE * HEAD_DIM

    def halves(cols):
        v = cols.reshape(b, s, NSA_KV, HEAD_DIM).transpose(0, 2, 1, 3)
        return v.reshape(b * NSA_KV, ncp, half)

    t2 = jnp.stack([halves(plain[:, :, 0:nkv]), halves(plain[:, :, nkv:2 * nkv])])
    t_lo = t2
    t_hi = jnp.concatenate([t2[:, :, 1:], jnp.zeros_like(t2[:, :, :1])], axis=2)
    pe2 = jnp.stack([k_pe.reshape(2, half), v_pe.reshape(2, half)])
    w1 = jnp.stack([k_w1, v_w1]).astype(BF16)
    w2 = jnp.stack([k_w2, v_w2]).astype(BF16)
    kvc = _compress(t_lo, t_hi, pe2, w1, w2).reshape(2, b, NSA_KV, ncp, HEAD_DIM)

    nsel = s // SEL_BLOCK
    nselp = -(-nsel // LANES) * LANES
    cstart = np.arange(ncp) * CMP_STRIDE
    sstart = np.arange(nselp) * SEL_BLOCK
    overlap = ((cstart[:, None] < sstart[None, :] + SEL_BLOCK)
               & (cstart[:, None] + CMP_BLOCK > sstart[None, :])
               & (np.arange(ncp)[:, None] < nc) & (np.arange(nselp)[None, :] < nsel))
    o_cmp, notsel = _cmp_attention(qn_raw, kvc[0], kvc[1],
                                   jnp.asarray(overlap.T.astype(BF16)),
                                   min(CMP_STEP_QUERIES, s))

    def group_major(cols):
        return cols.reshape(b, s, NSA_KV, HEAD_DIM).transpose(0, 2, 1, 3)

    k_sel = group_major(kk_rot[:, :, 0:nkv])
    onehot = (np.arange(s)[:, None] // SEL_BLOCK == np.arange(nselp)[None, :])
    k_tail = np.concatenate([np.zeros((s, LANES - HEAD_DIM), np.float32),
                             np.where(onehot, SEL_PENALTY, 0.0).astype(np.float32)], axis=1)
    k_aug = jnp.concatenate(
        [k_sel, jnp.broadcast_to(jnp.asarray(k_tail.astype(BF16)), (b, NSA_KV) + k_tail.shape)],
        axis=3)
    vt_all = plain[:, :, 2 * nkv:].transpose(0, 2, 1)
    o_sel = _sel_attention(qn_rot, notsel, k_aug, vt_all,
                           min(SEL_STEP_QUERIES, s), min(SEL_KEY_TILE, s))

    o_win = _band_attention(qn_rot, kk_rot, 1, vt_all, 1, nkv, NSA_WINDOW, None, qb)
    o_swa = _band_attention(qs_rot, kk_rot, 2, vt_all, 2, nkv_s, SWA_WINDOW, sinks, qb)

    gi = np.arange(LANES)
    ci = np.arange(3 * nq_n)
    expand = jnp.asarray(((gi[:, None] // 3 == (ci[None, :] % nq_n) // HEAD_DIM)
                          & (gi[:, None] % 3 == ci[None, :] // nq_n)
                          & (gi[:, None] < NSA_HEADS * 3)).astype(BF16))
    wr_pad = jnp.pad(w_router, ((0, 0), (0, LANES - N_EXPERTS)))
    wr_hi = wr_pad.astype(BF16)
    wr_lo = (wr_pad - wr_hi.astype(F32)).astype(BF16)
    b_r = jnp.concatenate([b_router, jnp.full((LANES - N_EXPERTS,), -jnp.inf, F32)]).reshape(1, LANES)
    tm = min(256, t)
    h, top_e, top_g, rank, counts, tile_base = _merge(
        o_cmp, o_sel, o_win, o_swa, gates, x, expand, w_br_nsa.astype(BF16),
        w_br_swa.astype(BF16), w_out.astype(BF16), ln1_g.reshape(1, d), ln1_b.reshape(1, d),
        wr_hi, wr_lo, b_r, alpha, tm)

    top_e, rank = top_e[:, :TOP_K], rank[:, :TOP_K]
    counts = counts[0, :N_EXPERTS].astype(I32)
    tile_base = tile_base[:, 0, :N_EXPERTS].astype(I32)
    dt = min(DISPATCH_TOKENS, t)
    segments, dest, cpos, run_first, blk_e, n_used, n_rows, pend, padded = _moe_plan(
        top_e, rank, tile_base, counts, t, tm, dt)
    xs = _dispatch(pend, padded, *segments, cpos.T, h, n_rows, dt)
    ys = _moe_blocks(blk_e, n_used, padded, xs, w_e_in, b_e_in.reshape(N_EXPERTS, 1, -1),
                     w_e_out, b_e_out.reshape(N_EXPERTS, 1, -1))
    run_start, run_len, run_seg, run_tot, spos = _combine_plan(
        top_e, dest, run_first, tile_base, counts, tm)
    return _final(run_start, run_len, run_seg, run_tot, h, top_g, spos, ys,
                  ln2_g.reshape(1, d), ln2_b.reshape(1, d), alpha, tm, s)


def kernel(x, w_in, nsa_k_pe, nsa_k_w1, nsa_k_w2, nsa_v_pe, nsa_v_w1, nsa_v_w2, swa_sinks, w_br_nsa, w_br_swa, w_out, ln1_g, ln1_b, w_router, b_router, w_expert_in, b_expert_in, w_expert_out, b_expert_out, ln2_g, ln2_b):
    depth = w_in.shape[0]
    alpha = (2.0 * depth) ** 0.25
    for l in range(depth):
        x = _layer(x, w_in[l], nsa_k_pe[l], nsa_k_w1[l], nsa_k_w2[l], nsa_v_pe[l], nsa_v_w1[l],
                   nsa_v_w2[l], swa_sinks[l], w_br_nsa[l], w_br_swa[l], w_out[l], ln1_g[l],
                   ln1_b[l], w_router[l], b_router[l], w_expert_in[l], b_expert_in[l],
                   w_expert_out[l], b_expert_out[l], ln2_g[l], ln2_b[l], alpha)
    return x
```

```python
import functools

import jax
import jax.numpy as jnp
import numpy as np
from jax import lax
from jax.experimental import pallas as pl
from jax.experimental.pallas import tpu as pltpu

BF16 = jnp.bfloat16
F32 = jnp.float32
I32 = jnp.int32

HEAD_DIM = 64
NSA_HEADS = 8
NSA_KV = 2
CMP_BLOCK = 32
CMP_STRIDE = 16
SEL_BLOCK = 64
SEL_TOPN = 16
NSA_WINDOW = 512
SWA_HEADS = 8
SWA_KV = 2
SWA_WINDOW = 128
Q_BLOCK = 128
ROPE_THETA = 10000.0
N_EXPERTS = 32
TOP_K = 4
SWIGLU_LIMIT = 7.0
SWIGLU_ALPHA = 1.702
LN_EPS = 1e-5

LANES = 128
BF16_SUBLANES = 16
F32_SUBLANES = 8
MXU_DEPTH = 256
MASKED = -1e30
M_INIT = -1e29
SEL_PENALTY = -(2.0 ** 100)
VMEM_LIMIT = 52 * 1024 * 1024
MOE_ROWS = 512
DISPATCH_TOKENS = 512
MERGE_STEP_TOKENS = 512
BAND_STEP_QUERIES = 512
CMP_STEP_QUERIES = 512
SEL_STEP_QUERIES = 128
SEL_KEY_TILE = 512

R_NSA = NSA_HEADS // NSA_KV
R_SWA = SWA_HEADS // SWA_KV
NT_DIMS = (((1,), (1,)), ((), ()))


def _params(*sem):
    return pltpu.CompilerParams(dimension_semantics=sem, vmem_limit_bytes=VMEM_LIMIT)


def _full(shape):
    n = len(shape)
    return pl.BlockSpec(shape, lambda *_: (0,) * n)


def _proj_kernel(x_ref, wr_ref, wp_ref, wg_ref, cos_ref, sin_ref,
                 qn_rot_ref, qs_rot_ref, kk_rot_ref, qn_raw_ref, plain_ref, gates_ref):
    xb = x_ref[...].astype(BF16)
    acc = jnp.dot(xb, wr_ref[...], preferred_element_type=F32)
    cos = cos_ref[...]
    sin = sin_ref[...]
    lane = lax.broadcasted_iota(I32, cos.shape, 1)
    first_half = (lane & (HEAD_DIM - 1)) < HEAD_DIM // 2

    def rope(t):
        partner = jnp.where(first_half, pltpu.roll(t, LANES - HEAD_DIM // 2, 1),
                            pltpu.roll(t, HEAD_DIM // 2, 1))
        return (t * cos + partner * sin).astype(BF16)

    nq = qn_rot_ref.shape[1] // LANES
    ns = qs_rot_ref.shape[1] // LANES
    nk = kk_rot_ref.shape[1] // LANES
    for c in range(nq):
        qn_rot_ref[:, c * LANES:(c + 1) * LANES] = rope(acc[:, c * LANES:(c + 1) * LANES])
    for c in range(ns):
        o = (nq + c) * LANES
        qs_rot_ref[:, c * LANES:(c + 1) * LANES] = rope(acc[:, o:o + LANES])
    for c in range(nk):
        o = (nq + ns + c) * LANES
        kk_rot_ref[:, c * LANES:(c + 1) * LANES] = rope(acc[:, o:o + LANES])
    qn_raw_ref[...] = acc[:, :nq * LANES].astype(BF16)
    plain_ref[...] = jnp.dot(xb, wp_ref[...], preferred_element_type=F32).astype(BF16)
    gates_ref[...] = jnp.dot(xb, wg_ref[...], preferred_element_type=F32)


def _project(x, w_rope, w_plain, w_gate, cos_t, sin_t, tm):
    b, seq, d = x.shape
    t = b * seq
    nr, npl, ng = w_rope.shape[1], w_plain.shape[1], w_gate.shape[1]
    nqn, nqs = NSA_HEADS * HEAD_DIM, SWA_HEADS * HEAD_DIM
    nkk = nr - nqn - nqs
    spb = seq // tm
    row = lambda i: (i, 0)
    seq_blk = lambda n: pl.BlockSpec((None, tm, n), lambda i: (i // spb, i % spb, 0))
    seq_out = lambda n: jax.ShapeDtypeStruct((b, seq, n), BF16)
    return pl.pallas_call(
        _proj_kernel,
        grid=(t // tm,),
        in_specs=[seq_blk(d), _full(w_rope.shape), _full(w_plain.shape), _full(w_gate.shape),
                  pl.BlockSpec((tm, LANES), lambda i: (i % spb, 0)),
                  pl.BlockSpec((tm, LANES), lambda i: (i % spb, 0))],
        out_specs=[seq_blk(nqn), seq_blk(nqs), seq_blk(nkk), seq_blk(nqn), seq_blk(npl),
                   pl.BlockSpec((tm, ng), row)],
        out_shape=[seq_out(nqn), seq_out(nqs), seq_out(nkk), seq_out(nqn), seq_out(npl),
                   jax.ShapeDtypeStruct((t, ng), F32)],
        compiler_params=_params("parallel"),
        name="proj",
    )(x, w_rope, w_plain, w_gate, cos_t, sin_t)


def _compress_kernel(a_ref, b_ref, pe_ref, w1_ref, w2_ref, out_ref):
    half = a_ref.shape[1]
    a = (a_ref[...].astype(F32) + pe_ref[0:1, :]).astype(BF16)
    b = (b_ref[...].astype(F32) + pe_ref[1:2, :]).astype(BF16)
    hid = jnp.dot(a, w1_ref[0:half, :], preferred_element_type=F32)
    hid = hid + jnp.dot(b, w1_ref[half:2 * half, :], preferred_element_type=F32)
    act = jax.nn.gelu(hid).astype(BF16)
    out_ref[...] = jnp.dot(act, w2_ref[...], preferred_element_type=F32).astype(BF16)


def _compress(t_lo, t_hi, pe2, w1, w2):
    two, bg, ncp, half = t_lo.shape
    hid = w1.shape[2]
    blk = lambda shape: pl.BlockSpec((None, None) + shape, lambda j, i: (j, i, 0, 0))
    wsp = lambda shape: pl.BlockSpec((None,) + shape, lambda j, i: (j, 0, 0))
    return pl.pallas_call(
        _compress_kernel,
        grid=(two, bg),
        in_specs=[blk((ncp, half)), blk((ncp, half)), wsp((2, half)), wsp((2 * half, hid)),
                  wsp((hid, HEAD_DIM))],
        out_specs=blk((ncp, HEAD_DIM)),
        out_shape=jax.ShapeDtypeStruct((two, bg, ncp, HEAD_DIM), BF16),
        compiler_params=_params("parallel", "parallel"),
        name="compress",
    )(t_lo, t_hi, pe2, w1, w2)


def _stack_heads(q, g, r):
    return jnp.concatenate(
        [q[:, (g * r + j) * HEAD_DIM:(g * r + j + 1) * HEAD_DIM] for j in range(r)], axis=0)


def _unstack_heads_t(parts, r, qb):
    blocks = []
    for o in parts:
        for j in range(0, r, 2):
            pair = jnp.concatenate([o[:, j * qb:(j + 1) * qb], o[:, (j + 1) * qb:(j + 2) * qb]],
                                   axis=0)
            blocks.append(pair.T)
    return jnp.concatenate(blocks, axis=1)


def _topk_mask_cols(vals, k):
    n = vals.shape[0]
    row = lax.broadcasted_iota(I32, vals.shape, 0).astype(F32)
    taken = jnp.zeros(vals.shape, F32)
    work = vals
    for _ in range(k):
        mx = jnp.max(work, axis=0, keepdims=True)
        first = jnp.min(jnp.where(work == mx, row, float(n)), axis=0, keepdims=True)
        pick = row == first
        taken = jnp.where(pick, 1.0, taken)
        work = jnp.where(pick, -jnp.inf, work)
    return taken > 0.5


def _cmp_kernel(q_ref, kc_ref, vct_ref, ovt_ref, o_ref, notsel_ref):
    qb = q_ref.shape[0]
    ncp = kc_ref.shape[1]
    nselp = ovt_ref.shape[0]
    i = pl.program_id(1)
    rows = R_NSA * qb
    pos = i * qb + (lax.broadcasted_iota(I32, (1, rows), 1) & (qb - 1))
    cend = lax.broadcasted_iota(I32, (ncp, 1), 0) * CMP_STRIDE + (CMP_BLOCK - 1)
    bias = jnp.where(cend <= pos, 0.0, MASKED)
    live = (pos >= CMP_BLOCK - 1).astype(F32)
    q = q_ref[...]
    outs, imps = [], []
    for g in range(NSA_KV):
        qg = _stack_heads(q, g, R_NSA) * (HEAD_DIM ** -0.5)
        st = lax.dot_general(kc_ref[g], qg, NT_DIMS, preferred_element_type=F32)
        st = st + bias
        e = jnp.exp(st - jnp.max(st, axis=0, keepdims=True))
        pt = e * (live / jnp.sum(e, axis=0, keepdims=True))
        outs.append(jnp.dot(vct_ref[g], pt.astype(BF16), preferred_element_type=F32))
        psum = pt[:, 0:qb]
        for j in range(1, R_NSA):
            psum = psum + pt[:, j * qb:(j + 1) * qb]
        p_hi = psum.astype(BF16)
        p_lo = (psum - p_hi.astype(F32)).astype(BF16)
        imps.append(jnp.dot(ovt_ref[...], p_hi, preferred_element_type=F32)
                    + jnp.dot(ovt_ref[...], p_lo, preferred_element_type=F32))
    imp = jnp.concatenate(imps, axis=1)
    lane = lax.broadcasted_iota(I32, (1, NSA_KV * qb), 1)
    cur = (i * qb + (lane & (qb - 1))) >> 6
    jb = lax.broadcasted_iota(I32, (nselp, 1), 0)
    forced = (jb == 0) | (jb == cur) | (jb == cur - 1)
    imp = jnp.where(jb > cur, -1.0, jnp.where(forced, 1e6, imp))
    notsel = jnp.where(_topk_mask_cols(imp, SEL_TOPN), 0.0, 1.0)
    for g in range(NSA_KV):
        notsel_ref[g] = notsel[:, g * qb:(g + 1) * qb].T.astype(BF16)
    o_ref[...] = _unstack_heads_t(outs, R_NSA, qb)


def _cmp_attention(q_raw, kc, vc, overlap_t, qb):
    vc = vc.transpose(0, 1, 3, 2)
    b, s, hq = q_raw.shape
    _, g, ncp, dh = kc.shape
    nselp = overlap_t.shape[0]
    return pl.pallas_call(
        _cmp_kernel,
        grid=(b, s // qb),
        in_specs=[pl.BlockSpec((None, qb, hq), lambda bi, i: (bi, i, 0)),
                  pl.BlockSpec((None, g, ncp, dh), lambda bi, i: (bi, 0, 0, 0)),
                  pl.BlockSpec((None, g, dh, ncp), lambda bi, i: (bi, 0, 0, 0)),
                  _full(overlap_t.shape)],
        out_specs=[pl.BlockSpec((None, qb, hq), lambda bi, i: (bi, i, 0)),
                   pl.BlockSpec((None, g, qb, nselp), lambda bi, i: (bi, 0, i, 0))],
        out_shape=[jax.ShapeDtypeStruct((b, s, hq), F32),
                   jax.ShapeDtypeStruct((b, g, s, nselp), BF16)],
        compiler_params=_params("parallel", "parallel"),
        name="cmp_attn",
    )(q_raw, kc, vc, overlap_t)


def _sel_kernel(q_ref, notsel_ref, k_ref, vt_ref, o_ref, *score_bufs, tk):
    s_even, s_odd = score_bufs[:NSA_KV], score_bufs[NSA_KV:]
    qb = q_ref.shape[0]
    i = pl.program_id(1)
    rows = R_NSA * qb
    qpos = i * qb + (lax.broadcasted_iota(I32, (1, rows), 1) & (qb - 1))
    n_clear = (i * qb) // tk
    q = q_ref[...]
    q_augs = []
    for g in range(NSA_KV):
        qg = _stack_heads(q, g, R_NSA) * (HEAD_DIM ** -0.5)
        q_augs.append(jnp.concatenate(
            [qg, jnp.zeros((rows, LANES - HEAD_DIM), BF16),
             jnp.concatenate([notsel_ref[g]] * R_NSA, axis=0)], axis=1))

    def scores(kt, g):
        start = pl.multiple_of(kt * tk, tk)
        return lax.dot_general(k_ref[g, pl.ds(start, tk), :], q_augs[g], NT_DIMS,
                               preferred_element_type=F32)

    def consume(kt, g, s_ref, m, acc, causal):
        start = pl.multiple_of(kt * tk, tk)
        vt_t = jnp.concatenate([vt_ref[g * HEAD_DIM:(g + 1) * HEAD_DIM, pl.ds(start, tk)],
                                jnp.ones((BF16_SUBLANES, tk), vt_ref.dtype)], axis=0)
        st = s_ref[...]
        if causal:
            kpos = start + lax.broadcasted_iota(I32, (tk, 1), 0)
            st = jnp.where(kpos <= qpos, st, MASKED)
        m_new = jnp.maximum(m, jnp.max(st, axis=0, keepdims=True))
        pt = jnp.exp(st - m_new).astype(BF16)
        acc = jnp.exp(m - m_new) * acc + jnp.dot(vt_t, pt, preferred_element_type=F32)
        return m_new, acc

    def advance(kt, carry, cur, nxt):
        new = []
        for g in range(NSA_KV):
            nxt[g][...] = scores(kt + 1, g)
            new.append(consume(kt, g, cur[g], *carry[g], False))
        return tuple(new)

    def pair(j, carry):
        carry = advance(2 * j, carry, s_even, s_odd)
        return advance(2 * j + 1, carry, s_odd, s_even)

    def finish(carry, cur):
        outs = []
        for g in range(NSA_KV):
            _, acc = consume(n_clear, g, cur[g], *carry[g], True)
            outs.append(acc[:HEAD_DIM] / acc[HEAD_DIM:HEAD_DIM + 1])
        o_ref[...] = _unstack_heads_t(outs, R_NSA, qb)

    for g in range(NSA_KV):
        s_even[g][...] = scores(0, g)
    init = tuple((jnp.full((1, rows), M_INIT, F32),
                  jnp.zeros((HEAD_DIM + BF16_SUBLANES, rows), F32))
                 for _ in range(NSA_KV))
    carry = lax.fori_loop(0, n_clear // 2, pair, init)

    @pl.when((n_clear & 1) == 0)
    def _():
        finish(carry, s_even)

    @pl.when((n_clear & 1) == 1)
    def _():
        finish(advance(n_clear - 1, carry, s_even, s_odd), s_odd)


def _sel_attention(q_rot, notsel, k_aug, vt_aug, qb, tk):
    b, s, hq = q_rot.shape
    _, g, _, kw = k_aug.shape
    nselp = notsel.shape[3]
    return pl.pallas_call(
        functools.partial(_sel_kernel, tk=tk),
        grid=(b, s // qb),
        in_specs=[pl.BlockSpec((None, qb, hq), lambda bi, i: (bi, i, 0)),
                  pl.BlockSpec((None, g, qb, nselp), lambda bi, i: (bi, 0, i, 0)),
                  pl.BlockSpec((None, g, s, kw), lambda bi, i: (bi, 0, 0, 0)),
                  pl.BlockSpec((None, g * HEAD_DIM, s), lambda bi, i: (bi, 0, 0))],
        out_specs=pl.BlockSpec((None, qb, hq), lambda bi, i: (bi, i, 0)),
        out_shape=jax.ShapeDtypeStruct((b, s, hq), F32),
        scratch_shapes=[pltpu.VMEM((tk, R_NSA * qb), F32)] * (2 * g),
        compiler_params=_params("parallel", "parallel"),
        name="sel_attn",
    )(q_rot, notsel, k_aug, vt_aug)


def _band_kernel(*refs, window, wlen, r, kv, qb, has_sinks):
    if has_sinks:
        sink_ref, q_ref, k_ref, vt_ref, o_ref = refs
    else:
        q_ref, k_ref, vt_ref, o_ref = refs
    nsub = q_ref.shape[0] // qb
    rows = r * qb
    lane = lax.broadcasted_iota(I32, (1, rows), 1)

    def mask_bias(i):
        start = jnp.maximum((i + 1) * qb - wlen, 0)
        rel = i * qb + (lane & (qb - 1)) - start - lax.broadcasted_iota(I32, (wlen, 1), 0)
        return jnp.where((rel >= 0) & (rel < window), 0.0, MASKED)

    def body(shared_bias):
        first = pl.program_id(1) * nsub
        if shared_bias:
            bias = mask_bias(first)
        for sb in range(nsub):
            i = first + sb
            if not shared_bias:
                bias = mask_bias(i)
            start = pl.multiple_of(jnp.maximum((i + 1) * qb - wlen, 0), qb)
            q = q_ref[sb * qb:(sb + 1) * qb, :]
            kw = k_ref[pl.ds(start, wlen), :]
            vtw = vt_ref[:, pl.ds(start, wlen)]
            outs = []
            for g in range(kv):
                qg = _stack_heads(q, g, r) * (HEAD_DIM ** -0.5)
                st = lax.dot_general(kw[:, g * HEAD_DIM:(g + 1) * HEAD_DIM], qg, NT_DIMS,
                                     preferred_element_type=F32) + bias
                m = jnp.max(st, axis=0, keepdims=True)
                if has_sinks:
                    sk = jnp.full((1, rows), sink_ref[g * r], F32)
                    for j in range(1, r):
                        sk = jnp.where(lane >= j * qb, sink_ref[g * r + j], sk)
                    m = jnp.maximum(m, sk)
                e = jnp.exp(st - m).astype(BF16)
                v_ones = jnp.concatenate([vtw[g * HEAD_DIM:(g + 1) * HEAD_DIM, :],
                                          jnp.ones((BF16_SUBLANES, wlen), vtw.dtype)], axis=0)
                ot = jnp.dot(v_ones, e, preferred_element_type=F32)
                den = ot[HEAD_DIM:HEAD_DIM + 1]
                if has_sinks:
                    den = den + jnp.exp(sk - m)
                outs.append(ot[:HEAD_DIM] / den)
            o_ref[sb * qb:(sb + 1) * qb, :] = _unstack_heads_t(outs, r, qb)

    if nsub * qb >= wlen - qb:
        @pl.when(pl.program_id(1) == 0)
        def _():
            body(False)

        @pl.when(pl.program_id(1) > 0)
        def _():
            body(True)
    else:
        body(False)


def _band_attention(q_rot, k_all, k_blk, vt_all, v_blk, gk, window, sinks, qb):
    b, s, hq = q_rot.shape
    assert k_all.shape[2] % gk == 0 and vt_all.shape[1] % gk == 0
    kv = gk // HEAD_DIM
    r = hq // gk
    back = -(-window // qb)
    wlen = (back + 1) * qb
    assert wlen <= s
    has_sinks = sinks is not None
    tq = min(BAND_STEP_QUERIES, s)
    in_specs = [pl.BlockSpec((None, tq, hq), lambda bi, i: (bi, i, 0)),
                pl.BlockSpec((None, s, gk), lambda bi, i: (bi, 0, k_blk)),
                pl.BlockSpec((None, gk, s), lambda bi, i: (bi, v_blk, 0))]
    args = [q_rot, k_all, vt_all]
    if has_sinks:
        in_specs = [pl.BlockSpec(memory_space=pltpu.SMEM)] + in_specs
        args = [sinks.astype(F32)] + args
    return pl.pallas_call(
        functools.partial(_band_kernel, window=window, wlen=wlen, r=r, kv=kv, qb=qb,
                          has_sinks=has_sinks),
        grid=(b, s // tq),
        in_specs=in_specs,
        out_specs=pl.BlockSpec((None, tq, hq), lambda bi, i: (bi, i, 0)),
        out_shape=jax.ShapeDtypeStruct((b, s, hq), F32),
        compiler_params=_params("parallel", "parallel"),
        name="band_attn_sink" if has_sinks else "band_attn",
    )(*args)


def _layer_norm(v, g, b):
    mu = jnp.mean(v, axis=1, keepdims=True)
    c = v - mu
    var = jnp.mean(c * c, axis=1, keepdims=True)
    return c * lax.rsqrt(var + LN_EPS) * g + b


def _split_dot(a, w):
    hi = a.astype(BF16)
    lo = (a - hi.astype(F32)).astype(BF16)
    return (jnp.dot(hi, w, preferred_element_type=F32)
            + jnp.dot(lo, w, preferred_element_type=F32))


def _merge_kernel(ocmp_ref, osel_ref, owin_ref, oswa_ref, gates_ref, x_ref, exp_ref,
                  wbn_ref, wbs_ref, wo_ref, lng_ref, lnb_ref, wrh_ref, wrl_ref, br_ref,
                  h_ref, te_ref, tg_ref, tr_ref, counts_ref, base_ref, cnt_ref, *, alpha, sub):
    d = x_ref.shape[1]
    hq = ocmp_ref.shape[1]
    n_sub = x_ref.shape[0] // sub
    col = lax.broadcasted_iota(I32, (sub, LANES), 1).astype(F32)

    def route(rows):
        gates = gates_ref[rows, :]
        gn = jax.nn.sigmoid(gates[:, 2 * d:])
        gexp = _split_dot(gn, exp_ref[...])
        o_nsa = (gexp[:, 0:hq] * ocmp_ref[rows, :] + gexp[:, hq:2 * hq] * osel_ref[rows, :]
                 + gexp[:, 2 * hq:3 * hq] * owin_ref[rows, :])
        y_nsa = jnp.dot(o_nsa.astype(BF16), wbn_ref[...], preferred_element_type=F32)
        y_swa = jnp.dot(oswa_ref[rows, :].astype(BF16), wbs_ref[...],
                        preferred_element_type=F32)
        gm = jax.nn.sigmoid(gates[:, :2 * d])
        mixed = gm[:, :d] * y_nsa + gm[:, d:] * y_swa
        z = jnp.dot(mixed.astype(BF16), wo_ref[...], preferred_element_type=F32)
        h = _layer_norm(alpha * x_ref[rows, :] + z, lng_ref[...], lnb_ref[...])
        h_ref[rows, :] = h
        h_hi = h.astype(BF16)
        h_lo = (h - h_hi.astype(F32)).astype(BF16)
        logits = (jnp.dot(h_hi, wrh_ref[...], preferred_element_type=F32)
                  + jnp.dot(h_lo, wrh_ref[...], preferred_element_type=F32)
                  + jnp.dot(h_hi, wrl_ref[...], preferred_element_type=F32)) + br_ref[...]
        work = logits
        vals, ids = [], []
        for _ in range(TOP_K):
            mx = jnp.max(work, axis=1, keepdims=True)
            first = jnp.min(jnp.where(work == mx, col, float(LANES)), axis=1, keepdims=True)
            vals.append(mx)
            ids.append(first)
            work = jnp.where(col == first, -jnp.inf, work)
        es = [jnp.exp(v - vals[0]) for v in vals]
        den = es[0]
        for e in es[1:]:
            den = den + e
        hits = jnp.zeros(logits.shape, F32)
        te = jnp.zeros(logits.shape, F32)
        tg = jnp.zeros(logits.shape, F32)
        for k in range(TOP_K):
            hits = jnp.where(col == ids[k], 1.0, hits)
            te = jnp.where(col == float(k), ids[k], te)
            tg = jnp.where(col == float(k), es[k] / den, tg)
        te_ref[rows, :] = te.astype(I32)
        tg_ref[rows, :] = tg
        earlier = (lax.broadcasted_iota(I32, (sub, sub), 1)
                   < lax.broadcasted_iota(I32, (sub, sub), 0))
        prefix = jnp.dot(jnp.where(earlier, 1.0, 0.0).astype(BF16), hits.astype(BF16),
                         preferred_element_type=F32)
        return ids, prefix, jnp.sum(hits, axis=0, keepdims=True)

    routed = [route(slice(j * sub, (j + 1) * sub)) for j in range(n_sub)]

    @pl.when(pl.program_id(0) == 0)
    def _():
        cnt_ref[...] = jnp.zeros(cnt_ref.shape, F32)

    for j, (ids, prefix, total) in enumerate(routed):
        base = cnt_ref[...]
        base_ref[j] = base
        before = prefix + base
        tr = jnp.zeros((sub, LANES), F32)
        for k in range(TOP_K):
            rank = jnp.sum(jnp.where(col == ids[k], before, 0.0), axis=1, keepdims=True)
            tr = jnp.where(col == float(k), rank, tr)
        tr_ref[j * sub:(j + 1) * sub, :] = tr.astype(I32)
        cnt_ref[...] = base + total
    counts_ref[...] = cnt_ref[...]


def _merge(o_cmp, o_sel, o_win, o_swa, gates, x, expand, w_bn, w_bs, w_o, ln_g, ln_b,
           wr_hi, wr_lo, b_r, alpha, sub):
    b, s, d = x.shape
    t = b * s
    hq = o_cmp.shape[2]
    tm = min(MERGE_STEP_TOKENS, s)
    n_sub = tm // sub
    spb = s // tm
    row = lambda i: (i, 0)
    tok = lambda n: pl.BlockSpec((tm, n), row)
    seq = lambda n: pl.BlockSpec((None, tm, n), lambda i: (i // spb, i % spb, 0))
    return pl.pallas_call(
        functools.partial(_merge_kernel, alpha=alpha, sub=sub),
        grid=(t // tm,),
        in_specs=[seq(hq), seq(hq), seq(hq), seq(hq), tok(gates.shape[1]), seq(d),
                  _full(expand.shape), _full(w_bn.shape), _full(w_bs.shape), _full(w_o.shape),
                  _full(ln_g.shape), _full(ln_b.shape), _full(wr_hi.shape), _full(wr_lo.shape),
                  _full(b_r.shape)],
        out_specs=[tok(d), tok(LANES), tok(LANES), tok(LANES), _full((1, LANES)),
                   pl.BlockSpec((n_sub, 1, LANES), lambda i: (i, 0, 0))],
        out_shape=[jax.ShapeDtypeStruct((t, d), F32), jax.ShapeDtypeStruct((t, LANES), I32),
                   jax.ShapeDtypeStruct((t, LANES), F32), jax.ShapeDtypeStruct((t, LANES), I32),
                   jax.ShapeDtypeStruct((1, LANES), F32),
                   jax.ShapeDtypeStruct((t // sub, 1, LANES), F32)],
        scratch_shapes=[pltpu.VMEM((1, LANES), F32)],
        compiler_params=_params("arbitrary"),
        name="merge_ln_router",
    )(o_cmp, o_sel, o_win, o_swa, gates, x, expand, w_bn, w_bs, w_o, ln_g, ln_b,
      wr_hi, wr_lo, b_r)


def _dispatch_kernel(pend_ref, padded_ref, cseg_ref, first_ref, len_ref, tot_ref,
                     h_ref, cpos_ref, xs_ref, comp0, comp1, zbuf, sem, zsem):
    j = pl.program_id(0)
    nt = pl.num_programs(0)
    tm = h_ref.shape[0]
    n_rows = xs_ref.shape[0]
    comps = (comp0, comp1)
    n_comp = comp0.shape[0]

    @pl.when(j == 0)
    def _():
        zbuf[...] = jnp.zeros(zbuf.shape, F32)
        used_rows = pend_ref[N_EXPERTS - 1]

        def zero_copy(start):
            start = pl.multiple_of(start, MOE_ROWS)
            return pltpu.make_async_copy(zbuf, xs_ref.at[pl.ds(start, MOE_ROWS), :], zsem)

        blocks = [(padded_ref[e] > 0, pend_ref[e] - MOE_ROWS) for e in range(N_EXPERTS)]
        blocks += [(used_rows + b * MOE_ROWS < n_rows, used_rows + b * MOE_ROWS)
                   for b in range(n_rows // MOE_ROWS - (nt * tm * TOP_K) // MOE_ROWS)]
        for cond, start in blocks:
            @pl.when(cond)
            def _(start=start):
                zero_copy(start).start()
        for cond, start in blocks:
            @pl.when(cond)
            def _(start=start):
                zero_copy(start).wait()

    def wait_writes(tile, slot):
        rows = pl.multiple_of(tot_ref[tile], F32_SUBLANES)
        pltpu.make_async_copy(comps[slot].at[pl.ds(0, rows), :], xs_ref.at[pl.ds(0, rows), :],
                              sem.at[slot]).wait()

    for slot in range(2):
        @pl.when((j >= 2) & ((j & 1) == slot))
        def _(slot=slot):
            wait_writes(j - 2, slot)

        @pl.when((j & 1) == slot)
        def _(slot=slot):
            row = lax.broadcasted_iota(I32, (n_comp, 1), 0)
            cpos = cpos_ref[...]
            sel = jnp.zeros((n_comp, tm), F32)
            for k in range(TOP_K):
                sel = jnp.where(row == cpos[k:k + 1, :], 1.0, sel)
            comps[slot][...] = jnp.dot(sel.astype(BF16), h_ref[...].astype(BF16),
                                       preferred_element_type=F32)
            for e in range(N_EXPERTS):
                ln = len_ref[j * N_EXPERTS + e]
                first = first_ref[j * N_EXPERTS + e]
                seg = cseg_ref[j * N_EXPERTS + e]
                for p in _run_sizes(tm):
                    done = ln & (-2 * p)
                    src = comps[slot].at[pl.ds(pl.multiple_of(seg + done, F32_SUBLANES), p), :]
                    dst = xs_ref.at[pl.ds(pl.multiple_of(first + done, F32_SUBLANES), p), :]
                    pl.when((ln & p) != 0)(pltpu.make_async_copy(src, dst, sem.at[slot]).start)

        @pl.when((j == nt - 1) & ((j & 1) == slot))
        def _(slot=slot):
            @pl.when(j >= 1)
            def _():
                wait_writes(j - 1, 1 - slot)
            wait_writes(j, slot)


def _dispatch(pend, padded, cseg, seg_first, seg_len, seg_tot, cpos_t, h, n_rows, tm):
    t, d = h.shape
    n_comp = -(-(tm * TOP_K + N_EXPERTS * (F32_SUBLANES - 1)) // MXU_DEPTH) * MXU_DEPTH
    grid_spec = pltpu.PrefetchScalarGridSpec(
        num_scalar_prefetch=6,
        grid=(t // tm,),
        in_specs=[pl.BlockSpec((tm, d), lambda i, *_: (i, 0)),
                  pl.BlockSpec((TOP_K, tm), lambda i, *_: (0, i))],
        out_specs=pl.BlockSpec(memory_space=pl.ANY),
        scratch_shapes=[pltpu.VMEM((n_comp, d), F32), pltpu.VMEM((n_comp, d), F32),
                        pltpu.VMEM((MOE_ROWS, d), F32), pltpu.SemaphoreType.DMA((2,)),
                        pltpu.SemaphoreType.DMA(())],
    )
    return pl.pallas_call(
        _dispatch_kernel,
        grid_spec=grid_spec,
        out_shape=jax.ShapeDtypeStruct((n_rows, d), F32),
        compiler_params=_params("arbitrary"),
        name="moe_dispatch",
    )(pend, padded, cseg, seg_first, seg_len, seg_tot, h, cpos_t)


def _moe_kernel(blk_e_ref, n_used_ref, first_ref, slot_ref, next_ref, x_ref, wi_hbm, bi_ref,
                wo_hbm, bo_ref, y_ref, wi_buf, wo_buf, wi_bf, wo_bf, wi_sem, wo_sem):
    f = wo_hbm.shape[1]
    b = pl.program_id(0)
    used = b < n_used_ref[0]
    e = blk_e_ref[b]
    slot = slot_ref[b]

    def fetch(expert, into):
        return (pltpu.make_async_copy(wi_hbm.at[expert], wi_buf.at[into], wi_sem.at[into]),
                pltpu.make_async_copy(wo_hbm.at[expert], wo_buf.at[into], wo_sem.at[into]))

    @pl.when(b == 0)
    def _():
        for cp in fetch(e, 0):
            cp.start()

    for s in range(2):
        @pl.when(used & (first_ref[b] == 1) & (slot == s))
        def _(s=s):
            @pl.when(next_ref[b] != e)
            def _():
                for cp in fetch(next_ref[b], 1 - s):
                    cp.start()
            for cp in fetch(e, s):
                cp.wait()
            wi_bf[...] = wi_buf[s].astype(BF16)
            wo_bf[...] = wo_buf[s].astype(BF16)

    @pl.when(used)
    def _():
        hdn = jnp.dot(x_ref[...].astype(BF16), wi_bf[...],
                      preferred_element_type=F32) + bi_ref[...]
        hg = jnp.minimum(hdn[:, :f], SWIGLU_LIMIT)
        hu = jnp.clip(hdn[:, f:], -SWIGLU_LIMIT, SWIGLU_LIMIT)
        act = hg * jax.nn.sigmoid(SWIGLU_ALPHA * hg) * (hu + 1.0)
        y_ref[...] = jnp.dot(act.astype(BF16), wo_bf[...],
                             preferred_element_type=F32) + bo_ref[...]

    @pl.when(jnp.logical_not(used))
    def _():
        y_ref[...] = jnp.zeros(y_ref.shape, F32)


def _moe_blocks(blk_e, n_used, padded, xs, w_in, b_in, w_out, b_out):
    n_rows, d = xs.shape
    e, _, f2 = w_in.shape
    f = w_out.shape[1]
    n_blk = n_rows // MOE_ROWS
    first = jnp.concatenate([jnp.ones((1,), I32), (blk_e[1:] != blk_e[:-1]).astype(I32)])
    ids = jnp.arange(e, dtype=I32)
    has_rows = padded > 0
    is_e = blk_e[:, None] == ids[None, :]
    slot = jnp.sum((ids[None, :] < blk_e[:, None]) & has_rows[None, :], axis=1) & 1
    later = jnp.where((ids[None, :] > ids[:, None]) & has_rows[None, :], ids[None, :], e)
    next_used = jnp.min(later, axis=1)
    next_used = jnp.where(next_used == e, ids, next_used).astype(I32)
    next_blk = jnp.sum(jnp.where(is_e, next_used[None, :], 0), axis=1).astype(I32)
    grid_spec = pltpu.PrefetchScalarGridSpec(
        num_scalar_prefetch=5,
        grid=(n_blk,),
        in_specs=[pl.BlockSpec((MOE_ROWS, d), lambda b, be, nu, *_: (jnp.minimum(b, nu[0] - 1), 0)),
                  pl.BlockSpec(memory_space=pl.ANY),
                  pl.BlockSpec((None, 1, f2), lambda b, be, *_: (be[b], 0, 0)),
                  pl.BlockSpec(memory_space=pl.ANY),
                  pl.BlockSpec((None, 1, d), lambda b, be, *_: (be[b], 0, 0))],
        out_specs=pl.BlockSpec((MOE_ROWS, d), lambda b, *_: (b, 0)),
        scratch_shapes=[pltpu.VMEM((2, d, f2), F32), pltpu.VMEM((2, f, d), F32),
                        pltpu.VMEM((d, f2), BF16), pltpu.VMEM((f, d), BF16),
                        pltpu.SemaphoreType.DMA((2,)), pltpu.SemaphoreType.DMA((2,))],
    )
    return pl.pallas_call(
        _moe_kernel,
        grid_spec=grid_spec,
        out_shape=jax.ShapeDtypeStruct((n_rows, d), F32),
        compiler_params=_params("arbitrary"),
        name="moe_experts",
    )(blk_e, n_used, first, slot.astype(I32), next_blk, xs, w_in, b_in, w_out, b_out)


def _run_sizes(tm):
    sizes, p = [], F32_SUBLANES
    while p <= tm:
        sizes.append(p)
        p *= 2
    return tuple(reversed(sizes))


def _stage_rows(tm):
    rows = tm * TOP_K + N_EXPERTS * 2 * (F32_SUBLANES - 1)
    return -(-rows // MXU_DEPTH) * MXU_DEPTH


def _final_kernel(start_ref, len_ref, seg_ref, tot_ref, h_ref, tg_ref, sp_ref, ys_ref, g_ref,
                  b_ref, o_ref, stage0, stage1, sem, *, alpha):
    i = pl.program_id(0)
    tm = h_ref.shape[0]
    stages = (stage0, stage1)
    n_stage = stage0.shape[0]

    def fetch(tile, into):
        for e in range(N_EXPERTS):
            ln = len_ref[tile * N_EXPERTS + e]
            first = start_ref[tile * N_EXPERTS + e]
            seg = seg_ref[tile * N_EXPERTS + e]
            for p in _run_sizes(tm):
                done = ln & (-2 * p)
                src = ys_ref.at[pl.ds(pl.multiple_of(first + done, F32_SUBLANES), p), :]
                dst = stages[into].at[pl.ds(pl.multiple_of(seg + done, F32_SUBLANES), p), :]
                pl.when((ln & p) != 0)(pltpu.make_async_copy(src, dst, sem.at[into]).start)

    @pl.when(i == 0)
    def _():
        stage0[...] = jnp.zeros(stage0.shape, F32)
        stage1[...] = jnp.zeros(stage1.shape, F32)
        fetch(0, 0)

    for slot in range(2):
        @pl.when((i & 1) == slot)
        def _(slot=slot):
            fetch(i + 1, 1 - slot)
            rows = pl.multiple_of(tot_ref[i], F32_SUBLANES)
            pltpu.make_async_copy(ys_ref.at[pl.ds(0, rows), :],
                                  stages[slot].at[pl.ds(0, rows), :], sem.at[slot]).wait()
            col = lax.broadcasted_iota(I32, (1, n_stage), 1)
            tg = tg_ref[...]
            sp = sp_ref[...]
            q = jnp.zeros((tm, n_stage), F32)
            for k in range(TOP_K):
                q = jnp.where(col == sp[:, k:k + 1], tg[:, k:k + 1], q)
            f = _split_dot(q, stages[slot][...].astype(BF16))
            o_ref[...] = _layer_norm(alpha * h_ref[...] + f, g_ref[...], b_ref[...])


def _final(run_start, run_len, run_seg, run_tot, h, top_g, spos, ys, ln_g, ln_b, alpha, tm,
           seq):
    t, d = h.shape
    spb = seq // tm
    grid_spec = pltpu.PrefetchScalarGridSpec(
        num_scalar_prefetch=4,
        grid=(t // tm,),
        in_specs=[pl.BlockSpec((tm, d), lambda i, *_: (i, 0)),
                  pl.BlockSpec((tm, LANES), lambda i, *_: (i, 0)),
                  pl.BlockSpec((tm, TOP_K), lambda i, *_: (i, 0)),
                  pl.BlockSpec(memory_space=pl.ANY),
                  pl.BlockSpec(ln_g.shape, lambda i, *_: (0, 0)),
                  pl.BlockSpec(ln_b.shape, lambda i, *_: (0, 0))],
        out_specs=pl.BlockSpec((None, tm, d), lambda i, *_: (i // spb, i % spb, 0)),
        scratch_shapes=[pltpu.VMEM((_stage_rows(tm), d), F32),
                        pltpu.VMEM((_stage_rows(tm), d), F32), pltpu.SemaphoreType.DMA((2,))],
    )
    return pl.pallas_call(
        functools.partial(_final_kernel, alpha=alpha),
        grid_spec=grid_spec,
        out_shape=jax.ShapeDtypeStruct((t // seq, seq, d), F32),
        compiler_params=_params("arbitrary"),
        name="combine_ln",
    )(run_start, run_len, run_seg, run_tot, h, top_g, spos, ys, ln_g, ln_b)


def _rope_tables(s):
    half = HEAD_DIM // 2
    inv = ROPE_THETA ** (-np.arange(half, dtype=np.float64) / half)
    ang = np.arange(s, dtype=np.float64)[:, None] * inv[None, :]
    cos, sin = np.cos(ang).astype(np.float32), np.sin(ang).astype(np.float32)
    reps = LANES // HEAD_DIM
    cos_t = np.tile(np.concatenate([cos, cos], axis=1), (1, reps))
    sin_t = np.tile(np.concatenate([-sin, sin], axis=1), (1, reps))
    return jnp.asarray(cos_t), jnp.asarray(sin_t)


def _ceil_to(v, m):
    return (v + m - 1) // m * m


def _moe_plan(top_e, rank, tile_base, counts, t, tm, dt):
    a = t * TOP_K
    per, nd = dt // tm, t // dt
    d_base = tile_base[::per]
    d_cnt = jnp.concatenate([d_base[1:], counts[None, :]], axis=0) - d_base
    seg_len = _ceil_to(d_cnt, F32_SUBLANES)
    padded = _ceil_to(jnp.sum(seg_len, axis=0), MOE_ROWS)
    pend = jnp.cumsum(padded)
    seg_first = (pend - padded)[None, :] + jnp.cumsum(seg_len, axis=0) - seg_len
    cseg = jnp.cumsum(seg_len, axis=1) - seg_len
    onehot = (top_e[:, :, None] == jnp.arange(N_EXPERTS, dtype=I32)).reshape(
        nd, dt, TOP_K, N_EXPERTS)

    def pick(table):
        return jnp.sum(jnp.where(onehot, table[:, None, None, :], 0), axis=3).reshape(t, TOP_K)

    local = rank - pick(d_base)
    dest = pick(seg_first) + local
    cpos = pick(cseg) + local
    run_first = jnp.repeat(seg_first - d_base, per, axis=0) + tile_base
    n_blk = -(-(a + nd * N_EXPERTS * (F32_SUBLANES - 1)) // MOE_ROWS) + N_EXPERTS
    blk_first = jnp.arange(n_blk, dtype=I32) * MOE_ROWS
    blk_e = jnp.minimum(jnp.sum(pend[None, :] <= blk_first[:, None], axis=1),
                        N_EXPERTS - 1).astype(I32)
    n_used = (pend[-1] // MOE_ROWS).astype(I32).reshape(1)
    flat = lambda v: v.astype(I32).reshape(-1)
    segments = (flat(cseg), flat(seg_first), flat(seg_len), jnp.sum(seg_len, axis=1).astype(I32))
    return (segments, dest.astype(I32), cpos.astype(I32), run_first, blk_e, n_used,
            n_blk * MOE_ROWS, pend.astype(I32), padded.astype(I32))


def _combine_plan(top_e, dest, run_first, tile_base, counts, tm):
    cnt = jnp.concatenate([tile_base[1:], counts[None, :]], axis=0) - tile_base
    lead = run_first & (F32_SUBLANES - 1)
    run_len = jnp.where(cnt > 0, _ceil_to(lead + cnt, F32_SUBLANES), 0)
    run_seg = jnp.cumsum(run_len, axis=1) - run_len
    shift = (run_seg + lead - run_first)[:, None, None, :]
    onehot = (top_e[:, :, None] == jnp.arange(N_EXPERTS, dtype=I32)).reshape(
        -1, tm, TOP_K, N_EXPERTS)
    spos = jnp.sum(jnp.where(onehot, shift, 0), axis=3).reshape(dest.shape) + dest
    flat = lambda v: jnp.pad(v.astype(I32), ((0, 1), (0, 0))).reshape(-1)
    run_tot = jnp.sum(run_len, axis=1).astype(I32)
    return flat(run_first - lead), flat(run_len), flat(run_seg), run_tot, spos.astype(I32)


def _layer(x, w_in, k_pe, k_w1, k_w2, v_pe, v_w1, v_w2, sinks, w_br_nsa, w_br_swa, w_out,
           ln1_g, ln1_b, w_router, b_router, w_e_in, b_e_in, w_e_out, b_e_out, ln2_g, ln2_b,
           alpha):
    b, s, d = x.shape
    t = b * s
    qb = Q_BLOCK
    nq_n, nkv = NSA_HEADS * HEAD_DIM, NSA_KV * HEAD_DIM
    nq_s, nkv_s = SWA_HEADS * HEAD_DIM, SWA_KV * HEAD_DIM
    widths = (nq_n, nkv, nkv, nkv, nkv, nkv, nkv, NSA_HEADS * 3, nq_s, nkv_s, nkv_s, 2 * d)
    offs = [0]
    for w in widths:
        offs.append(offs[-1] + w)
    col = lambda j: w_in[:, offs[j]:offs[j + 1]]
    (c_qn, c_kc, c_vc, c_ks, c_vs, c_kw, c_vw, c_gn, c_qs, c_k_s, c_v_s, c_gm) = map(col, range(12))
    w_rope = jnp.concatenate([c_qn, c_qs, c_ks, c_kw, c_k_s], axis=1).astype(BF16)
    w_plain = jnp.concatenate([c_kc, c_vc, c_vs, c_vw, c_v_s], axis=1).astype(BF16)
    gn_pad = LANES - NSA_HEADS * 3
    w_gate = jnp.concatenate([c_gm, c_gn, jnp.zeros((d, gn_pad), F32)], axis=1).astype(BF16)
    cos_t, sin_t = _rope_tables(s)

    qn_rot, qs_rot, kk_rot, qn_raw, plain, gates = _project(
        x, w_rope, w_plain, w_gate, cos_t, sin_t, min(256, s))

    nc = (s - CMP_BLOCK) // CMP_STRIDE + 1
    ncp = s // CMP_STRIDE
    half = CMP_STRIDE * HEAD_DIM

    def halves(cols):
        v = cols.reshape(b, s, NSA_KV, HEAD_DIM).transpose(0, 2, 1, 3)
        return v.reshape(b * NSA_KV, ncp, half)

    t2 = jnp.stack([halves(plain[:, :, 0:nkv]), halves(plain[:, :, nkv:2 * nkv])])
    t_lo = t2
    t_hi = jnp.concatenate([t2[:, :, 1:], jnp.zeros_like(t2[:, :, :1])], axis=2)
    pe2 = jnp.stack([k_pe.reshape(2, half), v_pe.reshape(2, half)])
    w1 = jnp.stack([k_w1, v_w1]).astype(BF16)
    w2 = jnp.stack([k_w2, v_w2]).astype(BF16)
    kvc = _compress(t_lo, t_hi, pe2, w1, w2).reshape(2, b, NSA_KV, ncp, HEAD_DIM)

    nsel = s // SEL_BLOCK
    nselp = -(-nsel // LANES) * LANES
    cstart = np.arange(ncp) * CMP_STRIDE
    sstart = np.arange(nselp) * SEL_BLOCK
    overlap = ((cstart[:, None] < sstart[None, :] + SEL_BLOCK)
               & (cstart[:, None] + CMP_BLOCK > sstart[None, :])
               & (np.arange(ncp)[:, None] < nc) & (np.arange(nselp)[None, :] < nsel))
    o_cmp, notsel = _cmp_attention(qn_raw, kvc[0], kvc[1],
                                   jnp.asarray(overlap.T.astype(BF16)),
                                   min(CMP_STEP_QUERIES, s))

    def group_major(cols):
        return cols.reshape(b, s, NSA_KV, HEAD_DIM).transpose(0, 2, 1, 3)

    k_sel = group_major(kk_rot[:, :, 0:nkv])
    onehot = (np.arange(s)[:, None] // SEL_BLOCK == np.arange(nselp)[None, :])
    k_tail = np.concatenate([np.zeros((s, LANES - HEAD_DIM), np.float32),
                             np.where(onehot, SEL_PENALTY, 0.0).astype(np.float32)], axis=1)
    k_aug = jnp.concatenate(
        [k_sel, jnp.broadcast_to(jnp.asarray(k_tail.astype(BF16)), (b, NSA_KV) + k_tail.shape)],
        axis=3)
    vt_all = plain[:, :, 2 * nkv:].transpose(0, 2, 1)
    o_sel = _sel_attention(qn_rot, notsel, k_aug, vt_all,
                           min(SEL_STEP_QUERIES, s), min(SEL_KEY_TILE, s))

    o_win = _band_attention(qn_rot, kk_rot, 1, vt_all, 1, nkv, NSA_WINDOW, None, qb)
    o_swa = _band_attention(qs_rot, kk_rot, 2, vt_all, 2, nkv_s, SWA_WINDOW, sinks, qb)

    gi = np.arange(LANES)
    ci = np.arange(3 * nq_n)
    expand = jnp.asarray(((gi[:, None] // 3 == (ci[None, :] % nq_n) // HEAD_DIM)
                          & (gi[:, None] % 3 == ci[None, :] // nq_n)
                          & (gi[:, None] < NSA_HEADS * 3)).astype(BF16))
    wr_pad = jnp.pad(w_router, ((0, 0), (0, LANES - N_EXPERTS)))
    wr_hi = wr_pad.astype(BF16)
    wr_lo = (wr_pad - wr_hi.astype(F32)).astype(BF16)
    b_r = jnp.concatenate([b_router, jnp.full((LANES - N_EXPERTS,), -jnp.inf, F32)]).reshape(1, LANES)
    tm = min(256, t)
    h, top_e, top_g, rank, counts, tile_base = _merge(
        o_cmp, o_sel, o_win, o_swa, gates, x, expand, w_br_nsa.astype(BF16),
        w_br_swa.astype(BF16), w_out.astype(BF16), ln1_g.reshape(1, d), ln1_b.reshape(1, d),
        wr_hi, wr_lo, b_r, alpha, tm)

    top_e, rank = top_e[:, :TOP_K], rank[:, :TOP_K]
    counts = counts[0, :N_EXPERTS].astype(I32)
    tile_base = tile_base[:, 0, :N_EXPERTS].astype(I32)
    dt = min(DISPATCH_TOKENS, t)
    segments, dest, cpos, run_first, blk_e, n_used, n_rows, pend, padded = _moe_plan(
        top_e, rank, tile_base, counts, t, tm, dt)
    xs = _dispatch(pend, padded, *segments, cpos.T, h, n_rows, dt)
    ys = _moe_blocks(blk_e, n_used, padded, xs, w_e_in, b_e_in.reshape(N_EXPERTS, 1, -1),
                     w_e_out, b_e_out.reshape(N_EXPERTS, 1, -1))
    run_start, run_len, run_seg, run_tot, spos = _combine_plan(
        top_e, dest, run_first, tile_base, counts, tm)
    return _final(run_start, run_len, run_seg, run_tot, h, top_g, spos, ys,
                  ln2_g.reshape(1, d), ln2_b.reshape(1, d), alpha, tm, s)


def kernel(x, w_in, nsa_k_pe, nsa_k_w1, nsa_k_w2, nsa_v_pe, nsa_v_w1, nsa_v_w2, swa_sinks, w_br_nsa, w_br_swa, w_out, ln1_g, ln1_b, w_router, b_router, w_expert_in, b_expert_in, w_expert_out, b_expert_out, ln2_g, ln2_b):
    depth = w_in.shape[0]
    alpha = (2.0 * depth) ** 0.25
    for l in range(depth):
        x = _layer(x, w_in[l], nsa_k_pe[l], nsa_k_w1[l], nsa_k_w2[l], nsa_v_pe[l], nsa_v_w1[l],
                   nsa_v_w2[l], swa_sinks[l], w_br_nsa[l], w_br_swa[l], w_out[l], ln1_g[l],
                   ln1_b[l], w_router[l], b_router[l], w_expert_in[l], b_expert_in[l],
                   w_expert_out[l], b_expert_out[l], ln2_g[l], ln2_b[l], alpha)
    return x
```

```python
import functools

import jax
import jax.numpy as jnp
import numpy as np
from jax import lax
from jax.experimental import pallas as pl
from jax.experimental.pallas import tpu as pltpu

BF16 = jnp.bfloat16
F32 = jnp.float32
I32 = jnp.int32

HEAD_DIM = 64
NSA_HEADS = 8
NSA_KV = 2
CMP_BLOCK = 32
CMP_STRIDE = 16
SEL_BLOCK = 64
SEL_TOPN = 16
NSA_WINDOW = 512
SWA_HEADS = 8
SWA_KV = 2
SWA_WINDOW = 128
Q_BLOCK = 128
ROPE_THETA = 10000.0
N_EXPERTS = 32
TOP_K = 4
SWIGLU_LIMIT = 7.0
SWIGLU_ALPHA = 1.702
LN_EPS = 1e-5

LANES = 128
BF16_SUBLANES = 16
F32_SUBLANES = 8
MXU_DEPTH = 256
LOG2_E = 1.4426950408889634
QK_SCALE = HEAD_DIM ** -0.5 * LOG2_E
MASKED = -1e30
M_INIT = -1e29
SEL_PENALTY = -(2.0 ** 100)
VMEM_LIMIT = 52 * 1024 * 1024
MOE_ROWS = 512
DISPATCH_TOKENS = 512
MERGE_STEP_TOKENS = 512
BAND_STEP_QUERIES = 512
CMP_STEP_QUERIES = 512
SEL_STEP_QUERIES = 128
SEL_KEY_TILE = 512

R_NSA = NSA_HEADS // NSA_KV
R_SWA = SWA_HEADS // SWA_KV
NT_DIMS = (((1,), (1,)), ((), ()))


def _params(*sem):
    return pltpu.CompilerParams(dimension_semantics=sem, vmem_limit_bytes=VMEM_LIMIT)


def _full(shape):
    n = len(shape)
    return pl.BlockSpec(shape, lambda *_: (0,) * n)


def _proj_kernel(x_ref, wr_ref, wp_ref, wg_ref, cos_ref, sin_ref,
                 qn_rot_ref, qs_rot_ref, kk_rot_ref, qn_raw_ref, plain_ref, gates_ref):
    xb = x_ref[...].astype(BF16)
    acc = jnp.dot(xb, wr_ref[...], preferred_element_type=F32)
    cos = cos_ref[...]
    sin = sin_ref[...]
    lane = lax.broadcasted_iota(I32, cos.shape, 1)
    first_half = (lane & (HEAD_DIM - 1)) < HEAD_DIM // 2

    def rope(t):
        partner = jnp.where(first_half, pltpu.roll(t, LANES - HEAD_DIM // 2, 1),
                            pltpu.roll(t, HEAD_DIM // 2, 1))
        return t * cos + partner * sin

    nq = qn_rot_ref.shape[1] // LANES
    ns = qs_rot_ref.shape[1] // LANES
    nk = kk_rot_ref.shape[1] // LANES
    for c in range(nq):
        qn_rot_ref[:, c * LANES:(c + 1) * LANES] = (
            rope(acc[:, c * LANES:(c + 1) * LANES]) * QK_SCALE).astype(BF16)
    for c in range(ns):
        o = (nq + c) * LANES
        qs_rot_ref[:, c * LANES:(c + 1) * LANES] = (
            rope(acc[:, o:o + LANES]) * QK_SCALE).astype(BF16)
    for c in range(nk):
        o = (nq + ns + c) * LANES
        kk_rot_ref[:, c * LANES:(c + 1) * LANES] = rope(acc[:, o:o + LANES]).astype(BF16)
    qn_raw_ref[...] = (acc[:, :nq * LANES] * QK_SCALE).astype(BF16)
    plain_ref[...] = jnp.dot(xb, wp_ref[...], preferred_element_type=F32).astype(BF16)
    gates_ref[...] = jnp.dot(xb, wg_ref[...], preferred_element_type=F32)


def _project(x, w_rope, w_plain, w_gate, cos_t, sin_t, tm):
    b, seq, d = x.shape
    t = b * seq
    nr, npl, ng = w_rope.shape[1], w_plain.shape[1], w_gate.shape[1]
    nqn, nqs = NSA_HEADS * HEAD_DIM, SWA_HEADS * HEAD_DIM
    nkk = nr - nqn - nqs
    spb = seq // tm
    row = lambda i: (i, 0)
    seq_blk = lambda n: pl.BlockSpec((None, tm, n), lambda i: (i // spb, i % spb, 0))
    seq_out = lambda n: jax.ShapeDtypeStruct((b, seq, n), BF16)
    return pl.pallas_call(
        _proj_kernel,
        grid=(t // tm,),
        in_specs=[seq_blk(d), _full(w_rope.shape), _full(w_plain.shape), _full(w_gate.shape),
                  pl.BlockSpec((tm, LANES), lambda i: (i % spb, 0)),
                  pl.BlockSpec((tm, LANES), lambda i: (i % spb, 0))],
        out_specs=[seq_blk(nqn), seq_blk(nqs), seq_blk(nkk), seq_blk(nqn), seq_blk(npl),
                   pl.BlockSpec((tm, ng), row)],
        out_shape=[seq_out(nqn), seq_out(nqs), seq_out(nkk), seq_out(nqn), seq_out(npl),
                   jax.ShapeDtypeStruct((t, ng), F32)],
        compiler_params=_params("parallel"),
        name="proj",
    )(x, w_rope, w_plain, w_gate, cos_t, sin_t)


def _compress_kernel(a_ref, b_ref, pe_ref, w1_ref, w2_ref, out_ref):
    half = a_ref.shape[1]
    a = (a_ref[...].astype(F32) + pe_ref[0:1, :]).astype(BF16)
    b = (b_ref[...].astype(F32) + pe_ref[1:2, :]).astype(BF16)
    hid = jnp.dot(a, w1_ref[0:half, :], preferred_element_type=F32)
    hid = hid + jnp.dot(b, w1_ref[half:2 * half, :], preferred_element_type=F32)
    act = jax.nn.gelu(hid).astype(BF16)
    out_ref[...] = jnp.dot(act, w2_ref[...], preferred_element_type=F32).astype(BF16)


def _compress(t_lo, t_hi, pe2, w1, w2):
    two, bg, ncp, half = t_lo.shape
    hid = w1.shape[2]
    blk = lambda shape: pl.BlockSpec((None, None) + shape, lambda j, i: (j, i, 0, 0))
    wsp = lambda shape: pl.BlockSpec((None,) + shape, lambda j, i: (j, 0, 0))
    return pl.pallas_call(
        _compress_kernel,
        grid=(two, bg),
        in_specs=[blk((ncp, half)), blk((ncp, half)), wsp((2, half)), wsp((2 * half, hid)),
                  wsp((hid, HEAD_DIM))],
        out_specs=blk((ncp, HEAD_DIM)),
        out_shape=jax.ShapeDtypeStruct((two, bg, ncp, HEAD_DIM), BF16),
        compiler_params=_params("parallel", "parallel"),
        name="compress",
    )(t_lo, t_hi, pe2, w1, w2)


def _stack_heads(q, g, r):
    return jnp.concatenate(
        [q[:, (g * r + j) * HEAD_DIM:(g * r + j + 1) * HEAD_DIM] for j in range(r)], axis=0)


def _unstack_heads_t(parts, r, qb):
    blocks = []
    for o in parts:
        for j in range(0, r, 2):
            pair = jnp.concatenate([o[:, j * qb:(j + 1) * qb], o[:, (j + 1) * qb:(j + 2) * qb]],
                                   axis=0)
            blocks.append(pair.T)
    return jnp.concatenate(blocks, axis=1)


def _topk_mask_cols(vals, k):
    n = vals.shape[0]
    row = lax.broadcasted_iota(I32, vals.shape, 0).astype(F32)
    taken = jnp.zeros(vals.shape, F32)
    work = vals
    for _ in range(k):
        mx = jnp.max(work, axis=0, keepdims=True)
        first = jnp.min(jnp.where(work == mx, row, float(n)), axis=0, keepdims=True)
        pick = row == first
        taken = jnp.where(pick, 1.0, taken)
        work = jnp.where(pick, -jnp.inf, work)
    return taken > 0.5


def _cmp_kernel(q_ref, kc_ref, vct_ref, ovt_ref, o_ref, notsel_ref):
    qb = q_ref.shape[0]
    ncp = kc_ref.shape[1]
    nselp = ovt_ref.shape[0]
    i = pl.program_id(1)
    rows = R_NSA * qb
    pos = i * qb + (lax.broadcasted_iota(I32, (1, rows), 1) & (qb - 1))
    cend = lax.broadcasted_iota(I32, (ncp, 1), 0) * CMP_STRIDE + (CMP_BLOCK - 1)
    bias = jnp.where(cend <= pos, 0.0, MASKED)
    live = (pos >= CMP_BLOCK - 1).astype(F32)
    q = q_ref[...]
    outs, imps = [], []
    for g in range(NSA_KV):
        qg = _stack_heads(q, g, R_NSA)
        st = lax.dot_general(kc_ref[g], qg, NT_DIMS, preferred_element_type=F32)
        st = st + bias
        e = jnp.exp2(st - jnp.max(st, axis=0, keepdims=True))
        pt = e * (live / jnp.sum(e, axis=0, keepdims=True))
        outs.append(jnp.dot(vct_ref[g], pt.astype(BF16), preferred_element_type=F32))
        psum = pt[:, 0:qb]
        for j in range(1, R_NSA):
            psum = psum + pt[:, j * qb:(j + 1) * qb]
        p_hi = psum.astype(BF16)
        p_lo = (psum - p_hi.astype(F32)).astype(BF16)
        imps.append(jnp.dot(ovt_ref[...], p_hi, preferred_element_type=F32)
                    + jnp.dot(ovt_ref[...], p_lo, preferred_element_type=F32))
    imp = jnp.concatenate(imps, axis=1)
    lane = lax.broadcasted_iota(I32, (1, NSA_KV * qb), 1)
    cur = (i * qb + (lane & (qb - 1))) >> 6
    jb = lax.broadcasted_iota(I32, (nselp, 1), 0)
    forced = (jb == 0) | (jb == cur) | (jb == cur - 1)
    imp = jnp.where(jb > cur, -1.0, jnp.where(forced, 1e6, imp))
    notsel = jnp.where(_topk_mask_cols(imp, SEL_TOPN), 0.0, 1.0)
    for g in range(NSA_KV):
        notsel_ref[g] = notsel[:, g * qb:(g + 1) * qb].T.astype(BF16)
    o_ref[...] = _unstack_heads_t(outs, R_NSA, qb)


def _cmp_attention(q_raw, kc, vc, overlap_t, qb):
    vc = vc.transpose(0, 1, 3, 2)
    b, s, hq = q_raw.shape
    _, g, ncp, dh = kc.shape
    nselp = overlap_t.shape[0]
    return pl.pallas_call(
        _cmp_kernel,
        grid=(b, s // qb),
        in_specs=[pl.BlockSpec((None, qb, hq), lambda bi, i: (bi, i, 0)),
                  pl.BlockSpec((None, g, ncp, dh), lambda bi, i: (bi, 0, 0, 0)),
                  pl.BlockSpec((None, g, dh, ncp), lambda bi, i: (bi, 0, 0, 0)),
                  _full(overlap_t.shape)],
        out_specs=[pl.BlockSpec((None, qb, hq), lambda bi, i: (bi, i, 0)),
                   pl.BlockSpec((None, g, qb, nselp), lambda bi, i: (bi, 0, i, 0))],
        out_shape=[jax.ShapeDtypeStruct((b, s, hq), F32),
                   jax.ShapeDtypeStruct((b, g, s, nselp), BF16)],
        compiler_params=_params("parallel", "parallel"),
        name="cmp_attn",
    )(q_raw, kc, vc, overlap_t)


def _sel_kernel(q_ref, notsel_ref, k_ref, vt_ref, o_ref, *score_bufs, tk):
    s_even, s_odd = score_bufs[:NSA_KV], score_bufs[NSA_KV:]
    qb = q_ref.shape[0]
    i = pl.program_id(1)
    rows = R_NSA * qb
    qpos = i * qb + (lax.broadcasted_iota(I32, (1, rows), 1) & (qb - 1))
    n_clear = (i * qb) // tk
    q = q_ref[...]
    q_augs = []
    for g in range(NSA_KV):
        qg = _stack_heads(q, g, R_NSA)
        q_augs.append(jnp.concatenate(
            [qg, jnp.zeros((rows, LANES - HEAD_DIM), BF16),
             jnp.concatenate([notsel_ref[g]] * R_NSA, axis=0)], axis=1))

    def scores(kt, g):
        start = pl.multiple_of(kt * tk, tk)
        return lax.dot_general(k_ref[g, pl.ds(start, tk), :], q_augs[g], NT_DIMS,
                               preferred_element_type=F32)

    def consume(kt, g, s_ref, m, acc, causal):
        start = pl.multiple_of(kt * tk, tk)
        vt_t = jnp.concatenate([vt_ref[g * HEAD_DIM:(g + 1) * HEAD_DIM, pl.ds(start, tk)],
                                jnp.ones((BF16_SUBLANES, tk), vt_ref.dtype)], axis=0)
        st = s_ref[...]
        if causal:
            kpos = start + lax.broadcasted_iota(I32, (tk, 1), 0)
            st = jnp.where(kpos <= qpos, st, MASKED)
        m_new = jnp.maximum(m, jnp.max(st, axis=0, keepdims=True))
        pt = jnp.exp2(st - m_new).astype(BF16)
        acc = jnp.exp2(m - m_new) * acc + jnp.dot(vt_t, pt, preferred_element_type=F32)
        return m_new, acc

    def advance(kt, carry, cur, nxt):
        new = []
        for g in range(NSA_KV):
            nxt[g][...] = scores(kt + 1, g)
            new.append(consume(kt, g, cur[g], *carry[g], False))
        return tuple(new)

    def pair(j, carry):
        carry = advance(2 * j, carry, s_even, s_odd)
        return advance(2 * j + 1, carry, s_odd, s_even)

    def finish(carry, cur):
        outs = []
        for g in range(NSA_KV):
            _, acc = consume(n_clear, g, cur[g], *carry[g], True)
            outs.append(acc[:HEAD_DIM] / acc[HEAD_DIM:HEAD_DIM + 1])
        o_ref[...] = _unstack_heads_t(outs, R_NSA, qb)

    for g in range(NSA_KV):
        s_even[g][...] = scores(0, g)
    init = tuple((jnp.full((1, rows), M_INIT, F32),
                  jnp.zeros((HEAD_DIM + BF16_SUBLANES, rows), F32))
                 for _ in range(NSA_KV))
    carry = lax.fori_loop(0, n_clear // 2, pair, init)

    @pl.when((n_clear & 1) == 0)
    def _():
        finish(carry, s_even)

    @pl.when((n_clear & 1) == 1)
    def _():
        finish(advance(n_clear - 1, carry, s_even, s_odd), s_odd)


def _sel_attention(q_rot, notsel, k_aug, vt_aug, qb, tk):
    b, s, hq = q_rot.shape
    _, g, _, kw = k_aug.shape
    nselp = notsel.shape[3]
    return pl.pallas_call(
        functools.partial(_sel_kernel, tk=tk),
        grid=(b, s // qb),
        in_specs=[pl.BlockSpec((None, qb, hq), lambda bi, i: (bi, i, 0)),
                  pl.BlockSpec((None, g, qb, nselp), lambda bi, i: (bi, 0, i, 0)),
                  pl.BlockSpec((None, g, s, kw), lambda bi, i: (bi, 0, 0, 0)),
                  pl.BlockSpec((None, g * HEAD_DIM, s), lambda bi, i: (bi, 0, 0))],
        out_specs=pl.BlockSpec((None, qb, hq), lambda bi, i: (bi, i, 0)),
        out_shape=jax.ShapeDtypeStruct((b, s, hq), F32),
        scratch_shapes=[pltpu.VMEM((tk, R_NSA * qb), F32)] * (2 * g),
        compiler_params=_params("parallel", "parallel"),
        name="sel_attn",
    )(q_rot, notsel, k_aug, vt_aug)


def _band_kernel(*refs, window, wlen, r, kv, qb, has_sinks):
    if has_sinks:
        sink_ref, q_ref, k_ref, vt_ref, o_ref = refs
    else:
        q_ref, k_ref, vt_ref, o_ref = refs
    nsub = q_ref.shape[0] // qb
    rows = r * qb
    lane = lax.broadcasted_iota(I32, (1, rows), 1)

    def mask_bias(i):
        start = jnp.maximum((i + 1) * qb - wlen, 0)
        rel = i * qb + (lane & (qb - 1)) - start - lax.broadcasted_iota(I32, (wlen, 1), 0)
        return jnp.where((rel >= 0) & (rel < window), 0.0, MASKED)

    def body(shared_bias):
        first = pl.program_id(1) * nsub
        if shared_bias:
            bias = mask_bias(first)
        for sb in range(nsub):
            i = first + sb
            if not shared_bias:
                bias = mask_bias(i)
            start = pl.multiple_of(jnp.maximum((i + 1) * qb - wlen, 0), qb)
            q = q_ref[sb * qb:(sb + 1) * qb, :]
            kw = k_ref[pl.ds(start, wlen), :]
            vtw = vt_ref[:, pl.ds(start, wlen)]
            outs = []
            for g in range(kv):
                qg = _stack_heads(q, g, r)
                st = lax.dot_general(kw[:, g * HEAD_DIM:(g + 1) * HEAD_DIM], qg, NT_DIMS,
                                     preferred_element_type=F32) + bias
                m = jnp.max(st, axis=0, keepdims=True)
                if has_sinks:
                    sk = jnp.full((1, rows), sink_ref[g * r], F32)
                    for j in range(1, r):
                        sk = jnp.where(lane >= j * qb, sink_ref[g * r + j], sk)
                    sk = sk * LOG2_E
                    m = jnp.maximum(m, sk)
                e = jnp.exp2(st - m).astype(BF16)
                v_ones = jnp.concatenate([vtw[g * HEAD_DIM:(g + 1) * HEAD_DIM, :],
                                          jnp.ones((BF16_SUBLANES, wlen), vtw.dtype)], axis=0)
                ot = jnp.dot(v_ones, e, preferred_element_type=F32)
                den = ot[HEAD_DIM:HEAD_DIM + 1]
                if has_sinks:
                    den = den + jnp.exp2(sk - m)
                outs.append(ot[:HEAD_DIM] / den)
            o_ref[sb * qb:(sb + 1) * qb, :] = _unstack_heads_t(outs, r, qb)

    if nsub * qb >= wlen - qb:
        @pl.when(pl.program_id(1) == 0)
        def _():
            body(False)

        @pl.when(pl.program_id(1) > 0)
        def _():
            body(True)
    else:
        body(False)


def _band_attention(q_rot, k_all, k_blk, vt_all, v_blk, gk, window, sinks, qb):
    b, s, hq = q_rot.shape
    assert k_all.shape[2] % gk == 0 and vt_all.shape[1] % gk == 0
    kv = gk // HEAD_DIM
    r = hq // gk
    back = -(-window // qb)
    wlen = (back + 1) * qb
    assert wlen <= s
    has_sinks = sinks is not None
    tq = min(BAND_STEP_QUERIES, s)
    in_specs = [pl.BlockSpec((None, tq, hq), lambda bi, i: (bi, i, 0)),
                pl.BlockSpec((None, s, gk), lambda bi, i: (bi, 0, k_blk)),
                pl.BlockSpec((None, gk, s), lambda bi, i: (bi, v_blk, 0))]
    args = [q_rot, k_all, vt_all]
    if has_sinks:
        in_specs = [pl.BlockSpec(memory_space=pltpu.SMEM)] + in_specs
        args = [sinks.astype(F32)] + args
    return pl.pallas_call(
        functools.partial(_band_kernel, window=window, wlen=wlen, r=r, kv=kv, qb=qb,
                          has_sinks=has_sinks),
        grid=(b, s // tq),
        in_specs=in_specs,
        out_specs=pl.BlockSpec((None, tq, hq), lambda bi, i: (bi, i, 0)),
        out_shape=jax.ShapeDtypeStruct((b, s, hq), F32),
        compiler_params=_params("parallel", "parallel"),
        name="band_attn_sink" if has_sinks else "band_attn",
    )(*args)


def _layer_norm(v, g, b):
    mu = jnp.mean(v, axis=1, keepdims=True)
    c = v - mu
    var = jnp.mean(c * c, axis=1, keepdims=True)
    return c * lax.rsqrt(var + LN_EPS) * g + b


def _split_dot(a, w):
    hi = a.astype(BF16)
    lo = (a - hi.astype(F32)).astype(BF16)
    return (jnp.dot(hi, w, preferred_element_type=F32)
            + jnp.dot(lo, w, preferred_element_type=F32))


def _merge_kernel(ocmp_ref, osel_ref, owin_ref, oswa_ref, gates_ref, x_ref, exp_ref,
                  wbn_ref, wbs_ref, wo_ref, lng_ref, lnb_ref, wrh_ref, wrl_ref, br_ref,
                  h_ref, te_ref, tg_ref, tr_ref, counts_ref, base_ref, cnt_ref, *, alpha, sub):
    d = x_ref.shape[1]
    hq = ocmp_ref.shape[1]
    n_sub = x_ref.shape[0] // sub
    col = lax.broadcasted_iota(I32, (sub, LANES), 1).astype(F32)

    def route(rows):
        gates = gates_ref[rows, :]
        gn = jax.nn.sigmoid(gates[:, 2 * d:])
        gexp = _split_dot(gn, exp_ref[...])
        o_nsa = (gexp[:, 0:hq] * ocmp_ref[rows, :] + gexp[:, hq:2 * hq] * osel_ref[rows, :]
                 + gexp[:, 2 * hq:3 * hq] * owin_ref[rows, :])
        y_nsa = jnp.dot(o_nsa.astype(BF16), wbn_ref[...], preferred_element_type=F32)
        y_swa = jnp.dot(oswa_ref[rows, :].astype(BF16), wbs_ref[...],
                        preferred_element_type=F32)
        gm = jax.nn.sigmoid(gates[:, :2 * d])
        mixed = gm[:, :d] * y_nsa + gm[:, d:] * y_swa
        z = jnp.dot(mixed.astype(BF16), wo_ref[...], preferred_element_type=F32)
        h = _layer_norm(alpha * x_ref[rows, :] + z, lng_ref[...], lnb_ref[...])
        h_ref[rows, :] = h
        h_hi = h.astype(BF16)
        h_lo = (h - h_hi.astype(F32)).astype(BF16)
        logits = (jnp.dot(h_hi, wrh_ref[...], preferred_element_type=F32)
                  + jnp.dot(h_lo, wrh_ref[...], preferred_element_type=F32)
                  + jnp.dot(h_hi, wrl_ref[...], preferred_element_type=F32)) + br_ref[...]
        work = logits
        vals, ids = [], []
        for _ in range(TOP_K):
            mx = jnp.max(work, axis=1, keepdims=True)
            first = jnp.min(jnp.where(work == mx, col, float(LANES)), axis=1, keepdims=True)
            vals.append(mx)
            ids.append(first)
            work = jnp.where(col == first, -jnp.inf, work)
        es = [jnp.exp(v - vals[0]) for v in vals]
        den = es[0]
        for e in es[1:]:
            den = den + e
        hits = jnp.zeros(logits.shape, F32)
        te = jnp.zeros(logits.shape, F32)
        tg = jnp.zeros(logits.shape, F32)
        for k in range(TOP_K):
            hits = jnp.where(col == ids[k], 1.0, hits)
            te = jnp.where(col == float(k), ids[k], te)
            tg = jnp.where(col == float(k), es[k] / den, tg)
        te_ref[rows, :] = te.astype(I32)
        tg_ref[rows, :] = tg
        earlier = (lax.broadcasted_iota(I32, (sub, sub), 1)
                   < lax.broadcasted_iota(I32, (sub, sub), 0))
        prefix = jnp.dot(jnp.where(earlier, 1.0, 0.0).astype(BF16), hits.astype(BF16),
                         preferred_element_type=F32)
        return ids, prefix, jnp.sum(hits, axis=0, keepdims=True)

    routed = [route(slice(j * sub, (j + 1) * sub)) for j in range(n_sub)]

    @pl.when(pl.program_id(0) == 0)
    def _():
        cnt_ref[...] = jnp.zeros(cnt_ref.shape, F32)

    for j, (ids, prefix, total) in enumerate(routed):
        base = cnt_ref[...]
        base_ref[j] = base
        before = prefix + base
        tr = jnp.zeros((sub, LANES), F32)
        for k in range(TOP_K):
            rank = jnp.sum(jnp.where(col == ids[k], before, 0.0), axis=1, keepdims=True)
            tr = jnp.where(col == float(k), rank, tr)
        tr_ref[j * sub:(j + 1) * sub, :] = tr.astype(I32)
        cnt_ref[...] = base + total
    counts_ref[...] = cnt_ref[...]


def _merge(o_cmp, o_sel, o_win, o_swa, gates, x, expand, w_bn, w_bs, w_o, ln_g, ln_b,
           wr_hi, wr_lo, b_r, alpha, sub):
    b, s, d = x.shape
    t = b * s
    hq = o_cmp.shape[2]
    tm = min(MERGE_STEP_TOKENS, s)
    n_sub = tm // sub
    spb = s // tm
    row = lambda i: (i, 0)
    tok = lambda n: pl.BlockSpec((tm, n), row)
    seq = lambda n: pl.BlockSpec((None, tm, n), lambda i: (i // spb, i % spb, 0))
    return pl.pallas_call(
        functools.partial(_merge_kernel, alpha=alpha, sub=sub),
        grid=(t // tm,),
        in_specs=[seq(hq), seq(hq), seq(hq), seq(hq), tok(gates.shape[1]), seq(d),
                  _full(expand.shape), _full(w_bn.shape), _full(w_bs.shape), _full(w_o.shape),
                  _full(ln_g.shape), _full(ln_b.shape), _full(wr_hi.shape), _full(wr_lo.shape),
                  _full(b_r.shape)],
        out_specs=[tok(d), tok(LANES), tok(LANES), tok(LANES), _full((1, LANES)),
                   pl.BlockSpec((n_sub, 1, LANES), lambda i: (i, 0, 0))],
        out_shape=[jax.ShapeDtypeStruct((t, d), F32), jax.ShapeDtypeStruct((t, LANES), I32),
                   jax.ShapeDtypeStruct((t, LANES), F32), jax.ShapeDtypeStruct((t, LANES), I32),
                   jax.ShapeDtypeStruct((1, LANES), F32),
                   jax.ShapeDtypeStruct((t // sub, 1, LANES), F32)],
        scratch_shapes=[pltpu.VMEM((1, LANES), F32)],
        compiler_params=_params("arbitrary"),
        name="merge_ln_router",
    )(o_cmp, o_sel, o_win, o_swa, gates, x, expand, w_bn, w_bs, w_o, ln_g, ln_b,
      wr_hi, wr_lo, b_r)


def _dispatch_kernel(pend_ref, padded_ref, cseg_ref, first_ref, len_ref, tot_ref,
                     h_ref, cpos_ref, xs_ref, comp0, comp1, zbuf, sem, zsem):
    j = pl.program_id(0)
    nt = pl.num_programs(0)
    tm = h_ref.shape[0]
    n_rows = xs_ref.shape[0]
    comps = (comp0, comp1)
    n_comp = comp0.shape[0]

    @pl.when(j == 0)
    def _():
        zbuf[...] = jnp.zeros(zbuf.shape, F32)
        used_rows = pend_ref[N_EXPERTS - 1]

        def zero_copy(start):
            start = pl.multiple_of(start, MOE_ROWS)
            return pltpu.make_async_copy(zbuf, xs_ref.at[pl.ds(start, MOE_ROWS), :], zsem)

        blocks = [(padded_ref[e] > 0, pend_ref[e] - MOE_ROWS) for e in range(N_EXPERTS)]
        blocks += [(used_rows + b * MOE_ROWS < n_rows, used_rows + b * MOE_ROWS)
                   for b in range(n_rows // MOE_ROWS - (nt * tm * TOP_K) // MOE_ROWS)]
        for cond, start in blocks:
            @pl.when(cond)
            def _(start=start):
                zero_copy(start).start()
        for cond, start in blocks:
            @pl.when(cond)
            def _(start=start):
                zero_copy(start).wait()

    def wait_writes(tile, slot):
        rows = pl.multiple_of(tot_ref[tile], F32_SUBLANES)
        pltpu.make_async_copy(comps[slot].at[pl.ds(0, rows), :], xs_ref.at[pl.ds(0, rows), :],
                              sem.at[slot]).wait()

    for slot in range(2):
        @pl.when((j >= 2) & ((j & 1) == slot))
        def _(slot=slot):
            wait_writes(j - 2, slot)

        @pl.when((j & 1) == slot)
        def _(slot=slot):
            row = lax.broadcasted_iota(I32, (n_comp, 1), 0)
            cpos = cpos_ref[...]
            sel = jnp.zeros((n_comp, tm), F32)
            for k in range(TOP_K):
                sel = jnp.where(row == cpos[k:k + 1, :], 1.0, sel)
            comps[slot][...] = jnp.dot(sel.astype(BF16), h_ref[...].astype(BF16),
                                       preferred_element_type=F32)
            for e in range(N_EXPERTS):
                ln = len_ref[j * N_EXPERTS + e]
                first = first_ref[j * N_EXPERTS + e]
                seg = cseg_ref[j * N_EXPERTS + e]
                for p in _run_sizes(tm):
                    done = ln & (-2 * p)
                    src = comps[slot].at[pl.ds(pl.multiple_of(seg + done, F32_SUBLANES), p), :]
                    dst = xs_ref.at[pl.ds(pl.multiple_of(first + done, F32_SUBLANES), p), :]
                    pl.when((ln & p) != 0)(pltpu.make_async_copy(src, dst, sem.at[slot]).start)

        @pl.when((j == nt - 1) & ((j & 1) == slot))
        def _(slot=slot):
            @pl.when(j >= 1)
            def _():
                wait_writes(j - 1, 1 - slot)
            wait_writes(j, slot)


def _dispatch(pend, padded, cseg, seg_first, seg_len, seg_tot, cpos_t, h, n_rows, tm):
    t, d = h.shape
    n_comp = -(-(tm * TOP_K + N_EXPERTS * (F32_SUBLANES - 1)) // MXU_DEPTH) * MXU_DEPTH
    grid_spec = pltpu.PrefetchScalarGridSpec(
        num_scalar_prefetch=6,
        grid=(t // tm,),
        in_specs=[pl.BlockSpec((tm, d), lambda i, *_: (i, 0)),
                  pl.BlockSpec((TOP_K, tm), lambda i, *_: (0, i))],
        out_specs=pl.BlockSpec(memory_space=pl.ANY),
        scratch_shapes=[pltpu.VMEM((n_comp, d), F32), pltpu.VMEM((n_comp, d), F32),
                        pltpu.VMEM((MOE_ROWS, d), F32), pltpu.SemaphoreType.DMA((2,)),
                        pltpu.SemaphoreType.DMA(())],
    )
    return pl.pallas_call(
        _dispatch_kernel,
        grid_spec=grid_spec,
        out_shape=jax.ShapeDtypeStruct((n_rows, d), F32),
        compiler_params=_params("arbitrary"),
        name="moe_dispatch",
    )(pend, padded, cseg, seg_first, seg_len, seg_tot, h, cpos_t)


def _moe_kernel(blk_e_ref, n_used_ref, first_ref, slot_ref, next_ref, x_ref, wi_hbm, bi_ref,
                wo_hbm, bo_ref, y_ref, wi_buf, wo_buf, wi_bf, wo_bf, wi_sem, wo_sem):
    f = wo_hbm.shape[1]
    b = pl.program_id(0)
    used = b < n_used_ref[0]
    e = blk_e_ref[b]
    slot = slot_ref[b]

    def fetch(expert, into):
        return (pltpu.make_async_copy(wi_hbm.at[expert], wi_buf.at[into], wi_sem.at[into]),
                pltpu.make_async_copy(wo_hbm.at[expert], wo_buf.at[into], wo_sem.at[into]))

    @pl.when(b == 0)
    def _():
        for cp in fetch(e, 0):
            cp.start()

    for s in range(2):
        @pl.when(used & (first_ref[b] == 1) & (slot == s))
        def _(s=s):
            @pl.when(next_ref[b] != e)
            def _():
                for cp in fetch(next_ref[b], 1 - s):
                    cp.start()
            for cp in fetch(e, s):
                cp.wait()
            wi_bf[...] = wi_buf[s].astype(BF16)
            wo_bf[...] = wo_buf[s].astype(BF16)

    @pl.when(used)
    def _():
        hdn = jnp.dot(x_ref[...].astype(BF16), wi_bf[...],
                      preferred_element_type=F32) + bi_ref[...]
        hg = jnp.minimum(hdn[:, :f], SWIGLU_LIMIT)
        hu = jnp.clip(hdn[:, f:], -SWIGLU_LIMIT, SWIGLU_LIMIT)
        act = hg * jax.nn.sigmoid(SWIGLU_ALPHA * hg) * (hu + 1.0)
        y_ref[...] = jnp.dot(act.astype(BF16), wo_bf[...],
                             preferred_element_type=F32) + bo_ref[...]

    @pl.when(jnp.logical_not(used))
    def _():
        y_ref[...] = jnp.zeros(y_ref.shape, F32)


def _moe_blocks(blk_e, n_used, padded, xs, w_in, b_in, w_out, b_out):
    n_rows, d = xs.shape
    e, _, f2 = w_in.shape
    f = w_out.shape[1]
    n_blk = n_rows // MOE_ROWS
    first = jnp.concatenate([jnp.ones((1,), I32), (blk_e[1:] != blk_e[:-1]).astype(I32)])
    ids = jnp.arange(e, dtype=I32)
    has_rows = padded > 0
    is_e = blk_e[:, None] == ids[None, :]
    slot = jnp.sum((ids[None, :] < blk_e[:, None]) & has_rows[None, :], axis=1) & 1
    later = jnp.where((ids[None, :] > ids[:, None]) & has_rows[None, :], ids[None, :], e)
    next_used = jnp.min(later, axis=1)
    next_used = jnp.where(next_used == e, ids, next_used).astype(I32)
    next_blk = jnp.sum(jnp.where(is_e, next_used[None, :], 0), axis=1).astype(I32)
    grid_spec = pltpu.PrefetchScalarGridSpec(
        num_scalar_prefetch=5,
        grid=(n_blk,),
        in_specs=[pl.BlockSpec((MOE_ROWS, d), lambda b, be, nu, *_: (jnp.minimum(b, nu[0] - 1), 0)),
                  pl.BlockSpec(memory_space=pl.ANY),
                  pl.BlockSpec((None, 1, f2), lambda b, be, *_: (be[b], 0, 0)),
                  pl.BlockSpec(memory_space=pl.ANY),
                  pl.BlockSpec((None, 1, d), lambda b, be, *_: (be[b], 0, 0))],
        out_specs=pl.BlockSpec((MOE_ROWS, d), lambda b, *_: (b, 0)),
        scratch_shapes=[pltpu.VMEM((2, d, f2), F32), pltpu.VMEM((2, f, d), F32),
                        pltpu.VMEM((d, f2), BF16), pltpu.VMEM((f, d), BF16),
                        pltpu.SemaphoreType.DMA((2,)), pltpu.SemaphoreType.DMA((2,))],
    )
    return pl.pallas_call(
        _moe_kernel,
        grid_spec=grid_spec,
        out_shape=jax.ShapeDtypeStruct((n_rows, d), F32),
        compiler_params=_params("arbitrary"),
        name="moe_experts",
    )(blk_e, n_used, first, slot.astype(I32), next_blk, xs, w_in, b_in, w_out, b_out)


def _run_sizes(tm):
    sizes, p = [], F32_SUBLANES
    while p <= tm:
        sizes.append(p)
        p *= 2
    return tuple(reversed(sizes))


def _stage_rows(tm):
    rows = tm * TOP_K + N_EXPERTS * 2 * (F32_SUBLANES - 1)
    return -(-rows // MXU_DEPTH) * MXU_DEPTH


def _final_kernel(start_ref, len_ref, seg_ref, tot_ref, h_ref, tg_ref, sp_ref, ys_ref, g_ref,
                  b_ref, o_ref, stage0, stage1, sem, *, alpha):
    i = pl.program_id(0)
    tm = h_ref.shape[0]
    stages = (stage0, stage1)
    n_stage = stage0.shape[0]

    def fetch(tile, into):
        for e in range(N_EXPERTS):
            ln = len_ref[tile * N_EXPERTS + e]
            first = start_ref[tile * N_EXPERTS + e]
            seg = seg_ref[tile * N_EXPERTS + e]
            for p in _run_sizes(tm):
                done = ln & (-2 * p)
                src = ys_ref.at[pl.ds(pl.multiple_of(first + done, F32_SUBLANES), p), :]
                dst = stages[into].at[pl.ds(pl.multiple_of(seg + done, F32_SUBLANES), p), :]
                pl.when((ln & p) != 0)(pltpu.make_async_copy(src, dst, sem.at[into]).start)

    @pl.when(i == 0)
    def _():
        stage0[...] = jnp.zeros(stage0.shape, F32)
        stage1[...] = jnp.zeros(stage1.shape, F32)
        fetch(0, 0)

    for slot in range(2):
        @pl.when((i & 1) == slot)
        def _(slot=slot):
            fetch(i + 1, 1 - slot)
            rows = pl.multiple_of(tot_ref[i], F32_SUBLANES)
            pltpu.make_async_copy(ys_ref.at[pl.ds(0, rows), :],
                                  stages[slot].at[pl.ds(0, rows), :], sem.at[slot]).wait()
            col = lax.broadcasted_iota(I32, (1, n_stage), 1)
            tg = tg_ref[...]
            sp = sp_ref[...]
            q = jnp.zeros((tm, n_stage), F32)
            for k in range(TOP_K):
                q = jnp.where(col == sp[:, k:k + 1], tg[:, k:k + 1], q)
            f = _split_dot(q, stages[slot][...].astype(BF16))
            o_ref[...] = _layer_norm(alpha * h_ref[...] + f, g_ref[...], b_ref[...])


def _final(run_start, run_len, run_seg, run_tot, h, top_g, spos, ys, ln_g, ln_b, alpha, tm,
           seq):
    t, d = h.shape
    spb = seq // tm
    grid_spec = pltpu.PrefetchScalarGridSpec(
        num_scalar_prefetch=4,
        grid=(t // tm,),
        in_specs=[pl.BlockSpec((tm, d), lambda i, *_: (i, 0)),
                  pl.BlockSpec((tm, LANES), lambda i, *_: (i, 0)),
                  pl.BlockSpec((tm, TOP_K), lambda i, *_: (i, 0)),
                  pl.BlockSpec(memory_space=pl.ANY),
                  pl.BlockSpec(ln_g.shape, lambda i, *_: (0, 0)),
                  pl.BlockSpec(ln_b.shape, lambda i, *_: (0, 0))],
        out_specs=pl.BlockSpec((None, tm, d), lambda i, *_: (i // spb, i % spb, 0)),
        scratch_shapes=[pltpu.VMEM((_stage_rows(tm), d), F32),
                        pltpu.VMEM((_stage_rows(tm), d), F32), pltpu.SemaphoreType.DMA((2,))],
    )
    return pl.pallas_call(
        functools.partial(_final_kernel, alpha=alpha),
        grid_spec=grid_spec,
        out_shape=jax.ShapeDtypeStruct((t // seq, seq, d), F32),
        compiler_params=_params("arbitrary"),
        name="combine_ln",
    )(run_start, run_len, run_seg, run_tot, h, top_g, spos, ys, ln_g, ln_b)


def _rope_tables(s):
    half = HEAD_DIM // 2
    inv = ROPE_THETA ** (-np.arange(half, dtype=np.float64) / half)
    ang = np.arange(s, dtype=np.float64)[:, None] * inv[None, :]
    cos, sin = np.cos(ang).astype(np.float32), np.sin(ang).astype(np.float32)
    reps = LANES // HEAD_DIM
    cos_t = np.tile(np.concatenate([cos, cos], axis=1), (1, reps))
    sin_t = np.tile(np.concatenate([-sin, sin], axis=1), (1, reps))
    return jnp.asarray(cos_t), jnp.asarray(sin_t)


def _ceil_to(v, m):
    return (v + m - 1) // m * m


def _moe_plan(top_e, rank, tile_base, counts, t, tm, dt):
    a = t * TOP_K
    per, nd = dt // tm, t // dt
    d_base = tile_base[::per]
    d_cnt = jnp.concatenate([d_base[1:], counts[None, :]], axis=0) - d_base
    seg_len = _ceil_to(d_cnt, F32_SUBLANES)
    padded = _ceil_to(jnp.sum(seg_len, axis=0), MOE_ROWS)
    pend = jnp.cumsum(padded)
    seg_first = (pend - padded)[None, :] + jnp.cumsum(seg_len, axis=0) - seg_len
    cseg = jnp.cumsum(seg_len, axis=1) - seg_len
    onehot = (top_e[:, :, None] == jnp.arange(N_EXPERTS, dtype=I32)).reshape(
        nd, dt, TOP_K, N_EXPERTS)

    def pick(table):
        return jnp.sum(jnp.where(onehot, table[:, None, None, :], 0), axis=3).reshape(t, TOP_K)

    local = rank - pick(d_base)
    dest = pick(seg_first) + local
    cpos = pick(cseg) + local
    run_first = jnp.repeat(seg_first - d_base, per, axis=0) + tile_base
    n_blk = -(-(a + nd * N_EXPERTS * (F32_SUBLANES - 1)) // MOE_ROWS) + N_EXPERTS
    blk_first = jnp.arange(n_blk, dtype=I32) * MOE_ROWS
    blk_e = jnp.minimum(jnp.sum(pend[None, :] <= blk_first[:, None], axis=1),
                        N_EXPERTS - 1).astype(I32)
    n_used = (pend[-1] // MOE_ROWS).astype(I32).reshape(1)
    flat = lambda v: v.astype(I32).reshape(-1)
    segments = (flat(cseg), flat(seg_first), flat(seg_len), jnp.sum(seg_len, axis=1).astype(I32))
    return (segments, dest.astype(I32), cpos.astype(I32), run_first, blk_e, n_used,
            n_blk * MOE_ROWS, pend.astype(I32), padded.astype(I32))


def _combine_plan(top_e, dest, run_first, tile_base, counts, tm):
    cnt = jnp.concatenate([tile_base[1:], counts[None, :]], axis=0) - tile_base
    lead = run_first & (F32_SUBLANES - 1)
    run_len = jnp.where(cnt > 0, _ceil_to(lead + cnt, F32_SUBLANES), 0)
    run_seg = jnp.cumsum(run_len, axis=1) - run_len
    shift = (run_seg + lead - run_first)[:, None, None, :]
    onehot = (top_e[:, :, None] == jnp.arange(N_EXPERTS, dtype=I32)).reshape(
        -1, tm, TOP_K, N_EXPERTS)
    spos = jnp.sum(jnp.where(onehot, shift, 0), axis=3).reshape(dest.shape) + dest
    flat = lambda v: jnp.pad(v.astype(I32), ((0, 1), (0, 0))).reshape(-1)
    run_tot = jnp.sum(run_len, axis=1).astype(I32)
    return flat(run_first - lead), flat(run_len), flat(run_seg), run_tot, spos.astype(I32)


def _layer(x, w_in, k_pe, k_w1, k_w2, v_pe, v_w1, v_w2, sinks, w_br_nsa, w_br_swa, w_out,
           ln1_g, ln1_b, w_router, b_router, w_e_in, b_e_in, w_e_out, b_e_out, ln2_g, ln2_b,
           alpha):
    b, s, d = x.shape
    t = b * s
    qb = Q_BLOCK
    nq_n, nkv = NSA_HEADS * HEAD_DIM, NSA_KV * HEAD_DIM
    nq_s, nkv_s = SWA_HEADS * HEAD_DIM, SWA_KV * HEAD_DIM
    widths = (nq_n, nkv, nkv, nkv, nkv, nkv, nkv, NSA_HEADS * 3, nq_s, nkv_s, nkv_s, 2 * d)
    offs = [0]
    for w in widths:
        offs.append(offs[-1] + w)
    col = lambda j: w_in[:, offs[j]:offs[j + 1]]
    (c_qn, c_kc, c_vc, c_ks, c_vs, c_kw, c_vw, c_gn, c_qs, c_k_s, c_v_s, c_gm) = map(col, range(12))
    w_rope = jnp.concatenate([c_qn, c_qs, c_ks, c_kw, c_k_s], axis=1).astype(BF16)
    w_plain = jnp.concatenate([c_kc, c_vc, c_vs, c_vw, c_v_s], axis=1).astype(BF16)
    gn_pad = LANES - NSA_HEADS * 3
    w_gate = jnp.concatenate([c_gm, c_gn, jnp.zeros((d, gn_pad), F32)], axis=1).astype(BF16)
    cos_t, sin_t = _rope_tables(s)

    qn_rot, qs_rot, kk_rot, qn_raw, plain, gates = _project(
        x, w_rope, w_plain, w_gate, cos_t, sin_t, min(256, s))

    nc = (s - CMP_BLOCK) // CMP_STRIDE + 1
    ncp = s // CMP_STRIDE
    half = CMP_STRIDE * HEAD_DIM

    def halves(cols):
        v = cols.reshape(b, s, NSA_KV, HEAD_DIM).transpose(0, 2, 1, 3)
        return v.reshape(b * NSA_KV, ncp, half)

    t2 = jnp.stack([halves(plain[:, :, 0:nkv]), halves(plain[:, :, nkv:2 * nkv])])
    t_lo = t2
    t_hi = jnp.concatenate([t2[:, :, 1:], jnp.zeros_like(t2[:, :, :1])], axis=2)
    pe2 = jnp.stack([k_pe.reshape(2, half), v_pe.reshape(2, half)])
    w1 = jnp.stack([k_w1, v_w1]).astype(BF16)
    w2 = jnp.stack([k_w2, v_w2]).astype(BF16)
    kvc = _compress(t_lo, t_hi, pe2, w1, w2).reshape(2, b, NSA_KV, ncp, HEAD_DIM)

    nsel = s // SEL_BLOCK
    nselp = -(-nsel // LANES) * LANES
    cstart = np.arange(ncp) * CMP_STRIDE
    sstart = np.arange(nselp) * SEL_BLOCK
    overlap = ((cstart[:, None] < sstart[None, :] + SEL_BLOCK)
               & (cstart[:, None] + CMP_BLOCK > sstart[None, :])
               & (np.arange(ncp)[:, None] < nc) & (np.arange(nselp)[None, :] < nsel))
    o_cmp, notsel = _cmp_attention(qn_raw, kvc[0], kvc[1],
                                   jnp.asarray(overlap.T.astype(BF16)),
                                   min(CMP_STEP_QUERIES, s))

    def group_major(cols):
        return cols.reshape(b, s, NSA_KV, HEAD_DIM).transpose(0, 2, 1, 3)

    k_sel = group_major(kk_rot[:, :, 0:nkv])
    onehot = (np.arange(s)[:, None] // SEL_BLOCK == np.arange(nselp)[None, :])
    k_tail = np.concatenate([np.zeros((s, LANES - HEAD_DIM), np.float32),
                             np.where(onehot, SEL_PENALTY, 0.0).astype(np.float32)], axis=1)
    k_aug = jnp.concatenate(
        [k_sel, jnp.broadcast_to(jnp.asarray(k_tail.astype(BF16)), (b, NSA_KV) + k_tail.shape)],
        axis=3)
    vt_all = plain[:, :, 2 * nkv:].transpose(0, 2, 1)
    o_sel = _sel_attention(qn_rot, notsel, k_aug, vt_all,
                           min(SEL_STEP_QUERIES, s), min(SEL_KEY_TILE, s))

    o_win = _band_attention(qn_rot, kk_rot, 1, vt_all, 1, nkv, NSA_WINDOW, None, qb)
    o_swa = _band_attention(qs_rot, kk_rot, 2, vt_all, 2, nkv_s, SWA_WINDOW, sinks, qb)

    gi = np.arange(LANES)
    ci = np.arange(3 * nq_n)
    expand = jnp.asarray(((gi[:, None] // 3 == (ci[None, :] % nq_n) // HEAD_DIM)
                          & (gi[:, None] % 3 == ci[None, :] // nq_n)
                          & (gi[:, None] < NSA_HEADS * 3)).astype(BF16))
    wr_pad = jnp.pad(w_router, ((0, 0), (0, LANES - N_EXPERTS)))
    wr_hi = wr_pad.astype(BF16)
    wr_lo = (wr_pad - wr_hi.astype(F32)).astype(BF16)
    b_r = jnp.concatenate([b_router, jnp.full((LANES - N_EXPERTS,), -jnp.inf, F32)]).reshape(1, LANES)
    tm = min(256, t)
    h, top_e, top_g, rank, counts, tile_base = _merge(
        o_cmp, o_sel, o_win, o_swa, gates, x, expand, w_br_nsa.astype(BF16),
        w_br_swa.astype(BF16), w_out.astype(BF16), ln1_g.reshape(1, d), ln1_b.reshape(1, d),
        wr_hi, wr_lo, b_r, alpha, tm)

    top_e, rank = top_e[:, :TOP_K], rank[:, :TOP_K]
    counts = counts[0, :N_EXPERTS].astype(I32)
    tile_base = tile_base[:, 0, :N_EXPERTS].astype(I32)
    dt = min(DISPATCH_TOKENS, t)
    segments, dest, cpos, run_first, blk_e, n_used, n_rows, pend, padded = _moe_plan(
        top_e, rank, tile_base, counts, t, tm, dt)
    xs = _dispatch(pend, padded, *segments, cpos.T, h, n_rows, dt)
    ys = _moe_blocks(blk_e, n_used, padded, xs, w_e_in, b_e_in.reshape(N_EXPERTS, 1, -1),
                     w_e_out, b_e_out.reshape(N_EXPERTS, 1, -1))
    run_start, run_len, run_seg, run_tot, spos = _combine_plan(
        top_e, dest, run_first, tile_base, counts, tm)
    return _final(run_start, run_len, run_seg, run_tot, h, top_g, spos, ys,
                  ln2_g.reshape(1, d), ln2_b.reshape(1, d), alpha, tm, s)


def kernel(x, w_in, nsa_k_pe, nsa_k_w1, nsa_k_w2, nsa_v_pe, nsa_v_w1, nsa_v_w2, swa_sinks, w_br_nsa, w_br_swa, w_out, ln1_g, ln1_b, w_router, b_router, w_expert_in, b_expert_in, w_expert_out, b_expert_out, ln2_g, ln2_b):
    depth = w_in.shape[0]
    alpha = (2.0 * depth) ** 0.25
    for l in range(depth):
        x = _layer(x, w_in[l], nsa_k_pe[l], nsa_k_w1[l], nsa_k_w2[l], nsa_v_pe[l], nsa_v_w1[l],
                   nsa_v_w2[l], swa_sinks[l], w_br_nsa[l], w_br_swa[l], w_out[l], ln1_g[l],
                   ln1_b[l], w_router[l], b_router[l], w_expert_in[l], b_expert_in[l],
                   w_expert_out[l], b_expert_out[l], ln2_g[l], ln2_b[l], alpha)
    return x
```

```python
import functools

import jax
import jax.numpy as jnp
import numpy as np
from jax import lax
from jax.experimental import pallas as pl
from jax.experimental.pallas import tpu as pltpu

BF16 = jnp.bfloat16
F32 = jnp.float32
I32 = jnp.int32

HEAD_DIM = 64
NSA_HEADS = 8
NSA_KV = 2
CMP_BLOCK = 32
CMP_STRIDE = 16
SEL_BLOCK = 64
SEL_TOPN = 16
NSA_WINDOW = 512
SWA_HEADS = 8
SWA_KV = 2
SWA_WINDOW = 128
Q_BLOCK = 128
ROPE_THETA = 10000.0
N_EXPERTS = 32
TOP_K = 4
SWIGLU_LIMIT = 7.0
SWIGLU_ALPHA = 1.702
LN_EPS = 1e-5

LANES = 128
BF16_SUBLANES = 16
F32_SUBLANES = 8
MXU_DEPTH = 256
MASKED = -1e30
M_INIT = -1e29
SEL_PENALTY = -(2.0 ** 100)
VMEM_LIMIT = 52 * 1024 * 1024
MOE_ROWS = 512
DISPATCH_TOKENS = 512
MERGE_STEP_TOKENS = 512
BAND_STEP_QUERIES = 512
CMP_STEP_QUERIES = 512
SEL_STEP_QUERIES = 128
SEL_KEY_TILE = 512

R_NSA = NSA_HEADS // NSA_KV
R_SWA = SWA_HEADS // SWA_KV
NT_DIMS = (((1,), (1,)), ((), ()))


def _params(*sem):
    return pltpu.CompilerParams(dimension_semantics=sem, vmem_limit_bytes=VMEM_LIMIT)


def _full(shape):
    n = len(shape)
    return pl.BlockSpec(shape, lambda *_: (0,) * n)


def _proj_kernel(x_ref, wr_ref, wp_ref, wg_ref, cos_ref, sin_ref,
                 qn_rot_ref, qs_rot_ref, kk_rot_ref, qn_raw_ref, plain_ref, gates_ref):
    xb = x_ref[...].astype(BF16)
    acc = jnp.dot(xb, wr_ref[...], preferred_element_type=F32)
    cos = cos_ref[...]
    sin = sin_ref[...]
    lane = lax.broadcasted_iota(I32, cos.shape, 1)
    first_half = (lane & (HEAD_DIM - 1)) < HEAD_DIM // 2

    def rope(t):
        partner = jnp.where(first_half, pltpu.roll(t, LANES - HEAD_DIM // 2, 1),
                            pltpu.roll(t, HEAD_DIM // 2, 1))
        return (t * cos + partner * sin).astype(BF16)

    nq = qn_rot_ref.shape[1] // LANES
    ns = qs_rot_ref.shape[1] // LANES
    nk = kk_rot_ref.shape[1] // LANES
    for c in range(nq):
        qn_rot_ref[:, c * LANES:(c + 1) * LANES] = rope(acc[:, c * LANES:(c + 1) * LANES])
    for c in range(ns):
        o = (nq + c) * LANES
        qs_rot_ref[:, c * LANES:(c + 1) * LANES] = rope(acc[:, o:o + LANES])
    for c in range(nk):
        o = (nq + ns + c) * LANES
        kk_rot_ref[:, c * LANES:(c + 1) * LANES] = rope(acc[:, o:o + LANES])
    qn_raw_ref[...] = acc[:, :nq * LANES].astype(BF16)
    plain_ref[...] = jnp.dot(xb, wp_ref[...], preferred_element_type=F32).astype(BF16)
    gates_ref[...] = jnp.dot(xb, wg_ref[...], preferred_element_type=F32)


def _project(x, w_rope, w_plain, w_gate, cos_t, sin_t, tm):
    b, seq, d = x.shape
    t = b * seq
    nr, npl, ng = w_rope.shape[1], w_plain.shape[1], w_gate.shape[1]
    nqn, nqs = NSA_HEADS * HEAD_DIM, SWA_HEADS * HEAD_DIM
    nkk = nr - nqn - nqs
    spb = seq // tm
    row = lambda i: (i, 0)
    seq_blk = lambda n: pl.BlockSpec((None, tm, n), lambda i: (i // spb, i % spb, 0))
    seq_out = lambda n: jax.ShapeDtypeStruct((b, seq, n), BF16)
    return pl.pallas_call(
        _proj_kernel,
        grid=(t // tm,),
        in_specs=[seq_blk(d), _full(w_rope.shape), _full(w_plain.shape), _full(w_gate.shape),
                  pl.BlockSpec((tm, LANES), lambda i: (i % spb, 0)),
                  pl.BlockSpec((tm, LANES), lambda i: (i % spb, 0))],
        out_specs=[seq_blk(nqn), seq_blk(nqs), seq_blk(nkk), seq_blk(nqn), seq_blk(npl),
                   pl.BlockSpec((tm, ng), row)],
        out_shape=[seq_out(nqn), seq_out(nqs), seq_out(nkk), seq_out(nqn), seq_out(npl),
                   jax.ShapeDtypeStruct((t, ng), F32)],
        compiler_params=_params("parallel"),
        name="proj",
    )(x, w_rope, w_plain, w_gate, cos_t, sin_t)


def _compress_kernel(a_ref, b_ref, pe_ref, w1_ref, w2_ref, out_ref):
    half = a_ref.shape[1]
    a = (a_ref[...].astype(F32) + pe_ref[0:1, :]).astype(BF16)
    b = (b_ref[...].astype(F32) + pe_ref[1:2, :]).astype(BF16)
    hid = jnp.dot(a, w1_ref[0:half, :], preferred_element_type=F32)
    hid = hid + jnp.dot(b, w1_ref[half:2 * half, :], preferred_element_type=F32)
    act = jax.nn.gelu(hid).astype(BF16)
    out_ref[...] = jnp.dot(act, w2_ref[...], preferred_element_type=F32).astype(BF16)


def _compress(t_lo, t_hi, pe2, w1, w2):
    two, bg, ncp, half = t_lo.shape
    hid = w1.shape[2]
    blk = lambda shape: pl.BlockSpec((None, None) + shape, lambda j, i: (j, i, 0, 0))
    wsp = lambda shape: pl.BlockSpec((None,) + shape, lambda j, i: (j, 0, 0))
    return pl.pallas_call(
        _compress_kernel,
        grid=(two, bg),
        in_specs=[blk((ncp, half)), blk((ncp, half)), wsp((2, half)), wsp((2 * half, hid)),
                  wsp((hid, HEAD_DIM))],
        out_specs=blk((ncp, HEAD_DIM)),
        out_shape=jax.ShapeDtypeStruct((two, bg, ncp, HEAD_DIM), BF16),
        compiler_params=_params("parallel", "parallel"),
        name="compress",
    )(t_lo, t_hi, pe2, w1, w2)


def _stack_heads(q, g, r):
    return jnp.concatenate(
        [q[:, (g * r + j) * HEAD_DIM:(g * r + j + 1) * HEAD_DIM] for j in range(r)], axis=0)


def _unstack_heads_t(parts, r, qb):
    blocks = []
    for o in parts:
        for j in range(0, r, 2):
            pair = jnp.concatenate([o[:, j * qb:(j + 1) * qb], o[:, (j + 1) * qb:(j + 2) * qb]],
                                   axis=0)
            blocks.append(pair.T)
    return jnp.concatenate(blocks, axis=1)


def _topk_mask_cols(vals, k):
    n = vals.shape[0]
    row = lax.broadcasted_iota(I32, vals.shape, 0).astype(F32)
    taken = jnp.zeros(vals.shape, F32)
    work = vals
    for _ in range(k):
        mx = jnp.max(work, axis=0, keepdims=True)
        first = jnp.min(jnp.where(work == mx, row, float(n)), axis=0, keepdims=True)
        pick = row == first
        taken = jnp.where(pick, 1.0, taken)
        work = jnp.where(pick, -jnp.inf, work)
    return taken > 0.5


def _cmp_kernel(q_ref, kc_ref, vct_ref, ovt_ref, o_ref, notsel_ref):
    qb = q_ref.shape[0]
    ncp = kc_ref.shape[1]
    nselp = ovt_ref.shape[0]
    i = pl.program_id(1)
    rows = R_NSA * qb
    pos = i * qb + (lax.broadcasted_iota(I32, (1, rows), 1) & (qb - 1))
    cend = lax.broadcasted_iota(I32, (ncp, 1), 0) * CMP_STRIDE + (CMP_BLOCK - 1)
    bias = jnp.where(cend <= pos, 0.0, MASKED)
    live = (pos >= CMP_BLOCK - 1).astype(F32)
    q = q_ref[...]
    outs, imps = [], []
    for g in range(NSA_KV):
        qg = _stack_heads(q, g, R_NSA) * (HEAD_DIM ** -0.5)
        st = lax.dot_general(kc_ref[g], qg, NT_DIMS, preferred_element_type=F32)
        st = st + bias
        e = jnp.exp(st - jnp.max(st, axis=0, keepdims=True))
        pt = e * (live / jnp.sum(e, axis=0, keepdims=True))
        outs.append(jnp.dot(vct_ref[g], pt.astype(BF16), preferred_element_type=F32))
        psum = pt[:, 0:qb]
        for j in range(1, R_NSA):
            psum = psum + pt[:, j * qb:(j + 1) * qb]
        p_hi = psum.astype(BF16)
        p_lo = (psum - p_hi.astype(F32)).astype(BF16)
        imps.append(jnp.dot(ovt_ref[...], p_hi, preferred_element_type=F32)
                    + jnp.dot(ovt_ref[...], p_lo, preferred_element_type=F32))
    imp = jnp.concatenate(imps, axis=1)
    lane = lax.broadcasted_iota(I32, (1, NSA_KV * qb), 1)
    cur = (i * qb + (lane & (qb - 1))) >> 6
    jb = lax.broadcasted_iota(I32, (nselp, 1), 0)
    forced = (jb == 0) | (jb == cur) | (jb == cur - 1)
    imp = jnp.where(jb > cur, -1.0, jnp.where(forced, 1e6, imp))
    notsel = jnp.where(_topk_mask_cols(imp, SEL_TOPN), 0.0, 1.0)
    for g in range(NSA_KV):
        notsel_ref[g] = notsel[:, g * qb:(g + 1) * qb].T.astype(BF16)
    o_ref[...] = _unstack_heads_t(outs, R_NSA, qb)


def _cmp_attention(q_raw, kc, vc, overlap_t, qb):
    vc = vc.transpose(0, 1, 3, 2)
    b, s, hq = q_raw.shape
    _, g, ncp, dh = kc.shape
    nselp = overlap_t.shape[0]
    return pl.pallas_call(
        _cmp_kernel,
        grid=(b, s // qb),
        in_specs=[pl.BlockSpec((None, qb, hq), lambda bi, i: (bi, i, 0)),
                  pl.BlockSpec((None, g, ncp, dh), lambda bi, i: (bi, 0, 0, 0)),
                  pl.BlockSpec((None, g, dh, ncp), lambda bi, i: (bi, 0, 0, 0)),
                  _full(overlap_t.shape)],
        out_specs=[pl.BlockSpec((None, qb, hq), lambda bi, i: (bi, i, 0)),
                   pl.BlockSpec((None, g, qb, nselp), lambda bi, i: (bi, 0, i, 0))],
        out_shape=[jax.ShapeDtypeStruct((b, s, hq), F32),
                   jax.ShapeDtypeStruct((b, g, s, nselp), BF16)],
        compiler_params=_params("parallel", "parallel"),
        name="cmp_attn",
    )(q_raw, kc, vc, overlap_t)


def _sel_kernel(q_ref, notsel_ref, k_ref, vt_ref, o_ref, *score_bufs, tk):
    s_even, s_odd = score_bufs[:NSA_KV], score_bufs[NSA_KV:]
    qb = q_ref.shape[0]
    i = pl.program_id(1)
    rows = R_NSA * qb
    qpos = i * qb + (lax.broadcasted_iota(I32, (1, rows), 1) & (qb - 1))
    n_clear = (i * qb) // tk
    q = q_ref[...]
    q_augs = []
    for g in range(NSA_KV):
        qg = _stack_heads(q, g, R_NSA) * (HEAD_DIM ** -0.5)
        q_augs.append(jnp.concatenate(
            [qg, jnp.zeros((rows, LANES - HEAD_DIM), BF16),
             jnp.concatenate([notsel_ref[g]] * R_NSA, axis=0)], axis=1))

    def scores(kt, g):
        start = pl.multiple_of(kt * tk, tk)
        return lax.dot_general(k_ref[g, pl.ds(start, tk), :], q_augs[g], NT_DIMS,
                               preferred_element_type=F32)

    def consume(kt, g, s_ref, m, acc, causal):
        start = pl.multiple_of(kt * tk, tk)
        vt_t = jnp.concatenate([vt_ref[g * HEAD_DIM:(g + 1) * HEAD_DIM, pl.ds(start, tk)],
                                jnp.ones((BF16_SUBLANES, tk), vt_ref.dtype)], axis=0)
        st = s_ref[...]
        if causal:
            kpos = start + lax.broadcasted_iota(I32, (tk, 1), 0)
            st = jnp.where(kpos <= qpos, st, MASKED)
        m_new = jnp.maximum(m, jnp.max(st, axis=0, keepdims=True))
        pt = jnp.exp(st - m_new).astype(BF16)
        acc = jnp.exp(m - m_new) * acc + jnp.dot(vt_t, pt, preferred_element_type=F32)
        return m_new, acc

    def advance(kt, carry, cur, nxt):
        new = []
        for g in range(NSA_KV):
            nxt[g][...] = scores(kt + 1, g)
            new.append(consume(kt, g, cur[g], *carry[g], False))
        return tuple(new)

    def pair(j, carry):
        carry = advance(2 * j, carry, s_even, s_odd)
        return advance(2 * j + 1, carry, s_odd, s_even)

    def finish(carry, cur):
        outs = []
        for g in range(NSA_KV):
            _, acc = consume(n_clear, g, cur[g], *carry[g], True)
            outs.append(acc[:HEAD_DIM] / acc[HEAD_DIM:HEAD_DIM + 1])
        o_ref[...] = _unstack_heads_t(outs, R_NSA, qb)

    for g in range(NSA_KV):
        s_even[g][...] = scores(0, g)
    init = tuple((jnp.full((1, rows), M_INIT, F32),
                  jnp.zeros((HEAD_DIM + BF16_SUBLANES, rows), F32))
                 for _ in range(NSA_KV))
    carry = lax.fori_loop(0, n_clear // 2, pair, init)

    @pl.when((n_clear & 1) == 0)
    def _():
        finish(carry, s_even)

    @pl.when((n_clear & 1) == 1)
    def _():
        finish(advance(n_clear - 1, carry, s_even, s_odd), s_odd)


def _sel_attention(q_rot, notsel, k_aug, vt_aug, qb, tk):
    b, s, hq = q_rot.shape
    _, g, _, kw = k_aug.shape
    nselp = notsel.shape[3]
    return pl.pallas_call(
        functools.partial(_sel_kernel, tk=tk),
        grid=(b, s // qb),
        in_specs=[pl.BlockSpec((None, qb, hq), lambda bi, i: (bi, i, 0)),
                  pl.BlockSpec((None, g, qb, nselp), lambda bi, i: (bi, 0, i, 0)),
                  pl.BlockSpec((None, g, s, kw), lambda bi, i: (bi, 0, 0, 0)),
                  pl.BlockSpec((None, g * HEAD_DIM, s), lambda bi, i: (bi, 0, 0))],
        out_specs=pl.BlockSpec((None, qb, hq), lambda bi, i: (bi, i, 0)),
        out_shape=jax.ShapeDtypeStruct((b, s, hq), F32),
        scratch_shapes=[pltpu.VMEM((tk, R_NSA * qb), F32)] * (2 * g),
        compiler_params=_params("parallel", "parallel"),
        name="sel_attn",
    )(q_rot, notsel, k_aug, vt_aug)


def _band_kernel(*refs, window, wlen, r, kv, qb, has_sinks):
    if has_sinks:
        sink_ref, q_ref, k_ref, vt_ref, o_ref = refs
    else:
        q_ref, k_ref, vt_ref, o_ref = refs
    nsub = q_ref.shape[0] // qb
    rows = r * qb
    lane = lax.broadcasted_iota(I32, (1, rows), 1)

    def mask_bias(i):
        start = jnp.maximum((i + 1) * qb - wlen, 0)
        rel = i * qb + (lane & (qb - 1)) - start - lax.broadcasted_iota(I32, (wlen, 1), 0)
        return jnp.where((rel >= 0) & (rel < window), 0.0, MASKED)

    def body(shared_bias):
        first = pl.program_id(1) * nsub
        if shared_bias:
            bias = mask_bias(first)
        for sb in range(nsub):
            i = first + sb
            if not shared_bias:
                bias = mask_bias(i)
            start = pl.multiple_of(jnp.maximum((i + 1) * qb - wlen, 0), qb)
            q = q_ref[sb * qb:(sb + 1) * qb, :]
            kw = k_ref[pl.ds(start, wlen), :]
            vtw = vt_ref[:, pl.ds(start, wlen)]
            outs = []
            for g in range(kv):
                qg = _stack_heads(q, g, r) * (HEAD_DIM ** -0.5)
                st = lax.dot_general(kw[:, g * HEAD_DIM:(g + 1) * HEAD_DIM], qg, NT_DIMS,
                                     preferred_element_type=F32) + bias
                m = jnp.max(st, axis=0, keepdims=True)
                if has_sinks:
                    sk = jnp.full((1, rows), sink_ref[g * r], F32)
                    for j in range(1, r):
                        sk = jnp.where(lane >= j * qb, sink_ref[g * r + j], sk)
                    m = jnp.maximum(m, sk)
                e = jnp.exp(st - m).astype(BF16)
                v_ones = jnp.concatenate([vtw[g * HEAD_DIM:(g + 1) * HEAD_DIM, :],
                                          jnp.ones((BF16_SUBLANES, wlen), vtw.dtype)], axis=0)
                ot = jnp.dot(v_ones, e, preferred_element_type=F32)
                den = ot[HEAD_DIM:HEAD_DIM + 1]
                if has_sinks:
                    den = den + jnp.exp(sk - m)
                outs.append(ot[:HEAD_DIM] / den)
            o_ref[sb * qb:(sb + 1) * qb, :] = _unstack_heads_t(outs, r, qb)

    if nsub * qb >= wlen - qb:
        @pl.when(pl.program_id(1) == 0)
        def _():
            body(False)

        @pl.when(pl.program_id(1) > 0)
        def _():
            body(True)
    else:
        body(False)


def _band_attention(q_rot, k_all, k_blk, vt_all, v_blk, gk, window, sinks, qb):
    b, s, hq = q_rot.shape
    assert k_all.shape[2] % gk == 0 and vt_all.shape[1] % gk == 0
    kv = gk // HEAD_DIM
    r = hq // gk
    back = -(-window // qb)
    wlen = (back + 1) * qb
    assert wlen <= s
    has_sinks = sinks is not None
    tq = min(BAND_STEP_QUERIES, s)
    in_specs = [pl.BlockSpec((None, tq, hq), lambda bi, i: (bi, i, 0)),
                pl.BlockSpec((None, s, gk), lambda bi, i: (bi, 0, k_blk)),
                pl.BlockSpec((None, gk, s), lambda bi, i: (bi, v_blk, 0))]
    args = [q_rot, k_all, vt_all]
    if has_sinks:
        in_specs = [pl.BlockSpec(memory_space=pltpu.SMEM)] + in_specs
        args = [sinks.astype(F32)] + args
    return pl.pallas_call(
        functools.partial(_band_kernel, window=window, wlen=wlen, r=r, kv=kv, qb=qb,
                          has_sinks=has_sinks),
        grid=(b, s // tq),
        in_specs=in_specs,
        out_specs=pl.BlockSpec((None, tq, hq), lambda bi, i: (bi, i, 0)),
        out_shape=jax.ShapeDtypeStruct((b, s, hq), F32),
        compiler_params=_params("parallel", "parallel"),
        name="band_attn_sink" if has_sinks else "band_attn",
    )(*args)


def _layer_norm(v, g, b):
    mu = jnp.mean(v, axis=1, keepdims=True)
    c = v - mu
    var = jnp.mean(c * c, axis=1, keepdims=True)
    return c * lax.rsqrt(var + LN_EPS) * g + b


def _split_dot(a, w):
    hi = a.astype(BF16)
    lo = (a - hi.astype(F32)).astype(BF16)
    return (jnp.dot(hi, w, preferred_element_type=F32)
            + jnp.dot(lo, w, preferred_element_type=F32))


def _merge_kernel(ocmp_ref, osel_ref, owin_ref, oswa_ref, gates_ref, x_ref, exp_ref,
                  wbn_ref, wbs_ref, wo_ref, lng_ref, lnb_ref, wrh_ref, wrl_ref, br_ref,
                  h_ref, te_ref, tg_ref, tr_ref, counts_ref, base_ref, cnt_ref, *, alpha, sub):
    d = x_ref.shape[1]
    hq = ocmp_ref.shape[1]
    n_sub = x_ref.shape[0] // sub
    col = lax.broadcasted_iota(I32, (sub, LANES), 1).astype(F32)

    def route(rows):
        gates = gates_ref[rows, :]
        gn = jax.nn.sigmoid(gates[:, 2 * d:])
        gexp = _split_dot(gn, exp_ref[...])
        o_nsa = (gexp[:, 0:hq] * ocmp_ref[rows, :] + gexp[:, hq:2 * hq] * osel_ref[rows, :]
                 + gexp[:, 2 * hq:3 * hq] * owin_ref[rows, :])
        y_nsa = jnp.dot(o_nsa.astype(BF16), wbn_ref[...], preferred_element_type=F32)
        y_swa = jnp.dot(oswa_ref[rows, :].astype(BF16), wbs_ref[...],
                        preferred_element_type=F32)
        gm = jax.nn.sigmoid(gates[:, :2 * d])
        mixed = gm[:, :d] * y_nsa + gm[:, d:] * y_swa
        z = jnp.dot(mixed.astype(BF16), wo_ref[...], preferred_element_type=F32)
        h = _layer_norm(alpha * x_ref[rows, :] + z, lng_ref[...], lnb_ref[...])
        h_ref[rows, :] = h
        h_hi = h.astype(BF16)
        h_lo = (h - h_hi.astype(F32)).astype(BF16)
        logits = (jnp.dot(h_hi, wrh_ref[...], preferred_element_type=F32)
                  + jnp.dot(h_lo, wrh_ref[...], preferred_element_type=F32)
                  + jnp.dot(h_hi, wrl_ref[...], preferred_element_type=F32)) + br_ref[...]
        work = logits
        vals, ids = [], []
        for _ in range(TOP_K):
            mx = jnp.max(work, axis=1, keepdims=True)
            first = jnp.min(jnp.where(work == mx, col, float(LANES)), axis=1, keepdims=True)
            vals.append(mx)
            ids.append(first)
            work = jnp.where(col == first, -jnp.inf, work)
        es = [jnp.exp(v - vals[0]) for v in vals]
        den = es[0]
        for e in es[1:]:
            den = den + e
        hits = jnp.zeros(logits.shape, F32)
        te = jnp.zeros(logits.shape, F32)
        tg = jnp.zeros(logits.shape, F32)
        for k in range(TOP_K):
            hits = jnp.where(col == ids[k], 1.0, hits)
            te = jnp.where(col == float(k), ids[k], te)
            tg = jnp.where(col == float(k), es[k] / den, tg)
        te_ref[rows, :] = te.astype(I32)
        tg_ref[rows, :] = tg
        earlier = (lax.broadcasted_iota(I32, (sub, sub), 1)
                   < lax.broadcasted_iota(I32, (sub, sub), 0))
        prefix = jnp.dot(jnp.where(earlier, 1.0, 0.0).astype(BF16), hits.astype(BF16),
                         preferred_element_type=F32)
        return ids, prefix, jnp.sum(hits, axis=0, keepdims=True)

    routed = [route(slice(j * sub, (j + 1) * sub)) for j in range(n_sub)]

    @pl.when(pl.program_id(0) == 0)
    def _():
        cnt_ref[...] = jnp.zeros(cnt_ref.shape, F32)

    for j, (ids, prefix, total) in enumerate(routed):
        base = cnt_ref[...]
        base_ref[j] = base
        before = prefix + base
        tr = jnp.zeros((sub, LANES), F32)
        for k in range(TOP_K):
            rank = jnp.sum(jnp.where(col == ids[k], before, 0.0), axis=1, keepdims=True)
            tr = jnp.where(col == float(k), rank, tr)
        tr_ref[j * sub:(j + 1) * sub, :] = tr.astype(I32)
        cnt_ref[...] = base + total
    counts_ref[...] = cnt_ref[...]


def _merge(o_cmp, o_sel, o_win, o_swa, gates, x, expand, w_bn, w_bs, w_o, ln_g, ln_b,
           wr_hi, wr_lo, b_r, alpha, sub):
    b, s, d = x.shape
    t = b * s
    hq = o_cmp.shape[2]
    tm = min(MERGE_STEP_TOKENS, s)
    n_sub = tm // sub
    spb = s // tm
    row = lambda i: (i, 0)
    tok = lambda n: pl.BlockSpec((tm, n), row)
    seq = lambda n: pl.BlockSpec((None, tm, n), lambda i: (i // spb, i % spb, 0))
    return pl.pallas_call(
        functools.partial(_merge_kernel, alpha=alpha, sub=sub),
        grid=(t // tm,),
        in_specs=[seq(hq), seq(hq), seq(hq), seq(hq), tok(gates.shape[1]), seq(d),
                  _full(expand.shape), _full(w_bn.shape), _full(w_bs.shape), _full(w_o.shape),
                  _full(ln_g.shape), _full(ln_b.shape), _full(wr_hi.shape), _full(wr_lo.shape),
                  _full(b_r.shape)],
        out_specs=[tok(d), tok(LANES), tok(LANES), tok(LANES), _full((1, LANES)),
                   pl.BlockSpec((n_sub, 1, LANES), lambda i: (i, 0, 0))],
        out_shape=[jax.ShapeDtypeStruct((t, d), F32), jax.ShapeDtypeStruct((t, LANES), I32),
                   jax.ShapeDtypeStruct((t, LANES), F32), jax.ShapeDtypeStruct((t, LANES), I32),
                   jax.ShapeDtypeStruct((1, LANES), F32),
                   jax.ShapeDtypeStruct((t // sub, 1, LANES), F32)],
        scratch_shapes=[pltpu.VMEM((1, LANES), F32)],
        compiler_params=_params("arbitrary"),
        name="merge_ln_router",
    )(o_cmp, o_sel, o_win, o_swa, gates, x, expand, w_bn, w_bs, w_o, ln_g, ln_b,
      wr_hi, wr_lo, b_r)


def _dispatch_kernel(pend_ref, padded_ref, cseg_ref, first_ref, len_ref, tot_ref,
                     h_ref, cpos_ref, xs_ref, comp0, comp1, zbuf, sem, zsem):
    j = pl.program_id(0)
    nt = pl.num_programs(0)
    tm = h_ref.shape[0]
    n_rows = xs_ref.shape[0]
    comps = (comp0, comp1)
    n_comp = comp0.shape[0]

    @pl.when(j == 0)
    def _():
        zbuf[...] = jnp.zeros(zbuf.shape, F32)
        used_rows = pend_ref[N_EXPERTS - 1]

        def zero_copy(start):
            start = pl.multiple_of(start, MOE_ROWS)
            return pltpu.make_async_copy(zbuf, xs_ref.at[pl.ds(start, MOE_ROWS), :], zsem)

        blocks = [(padded_ref[e] > 0, pend_ref[e] - MOE_ROWS) for e in range(N_EXPERTS)]
        blocks += [(used_rows + b * MOE_ROWS < n_rows, used_rows + b * MOE_ROWS)
                   for b in range(n_rows // MOE_ROWS - (nt * tm * TOP_K) // MOE_ROWS)]
        for cond, start in blocks:
            @pl.when(cond)
            def _(start=start):
                zero_copy(start).start()
        for cond, start in blocks:
            @pl.when(cond)
            def _(start=start):
                zero_copy(start).wait()

    def wait_writes(tile, slot):
        rows = pl.multiple_of(tot_ref[tile], F32_SUBLANES)
        pltpu.make_async_copy(comps[slot].at[pl.ds(0, rows), :], xs_ref.at[pl.ds(0, rows), :],
                              sem.at[slot]).wait()

    for slot in range(2):
        @pl.when((j >= 2) & ((j & 1) == slot))
        def _(slot=slot):
            wait_writes(j - 2, slot)

        @pl.when((j & 1) == slot)
        def _(slot=slot):
            row = lax.broadcasted_iota(I32, (n_comp, 1), 0)
            cpos = cpos_ref[...]
            sel = jnp.zeros((n_comp, tm), F32)
            for k in range(TOP_K):
                sel = jnp.where(row == cpos[k:k + 1, :], 1.0, sel)
            comps[slot][...] = jnp.dot(sel.astype(BF16), h_ref[...].astype(BF16),
                                       preferred_element_type=F32)
            for e in range(N_EXPERTS):
                ln = len_ref[j * N_EXPERTS + e]
                first = first_ref[j * N_EXPERTS + e]
                seg = cseg_ref[j * N_EXPERTS + e]
                for p in _run_sizes(tm):
                    done = ln & (-2 * p)
                    src = comps[slot].at[pl.ds(pl.multiple_of(seg + done, F32_SUBLANES), p), :]
                    dst = xs_ref.at[pl.ds(pl.multiple_of(first + done, F32_SUBLANES), p), :]
                    pl.when((ln & p) != 0)(pltpu.make_async_copy(src, dst, sem.at[slot]).start)

        @pl.when((j == nt - 1) & ((j & 1) == slot))
        def _(slot=slot):
            @pl.when(j >= 1)
            def _():
                wait_writes(j - 1, 1 - slot)
            wait_writes(j, slot)


def _dispatch(pend, padded, cseg, seg_first, seg_len, seg_tot, cpos_t, h, n_rows, tm):
    t, d = h.shape
    n_comp = -(-(tm * TOP_K + N_EXPERTS * (F32_SUBLANES - 1)) // MXU_DEPTH) * MXU_DEPTH
    grid_spec = pltpu.PrefetchScalarGridSpec(
        num_scalar_prefetch=6,
        grid=(t // tm,),
        in_specs=[pl.BlockSpec((tm, d), lambda i, *_: (i, 0)),
                  pl.BlockSpec((TOP_K, tm), lambda i, *_: (0, i))],
        out_specs=pl.BlockSpec(memory_space=pl.ANY),
        scratch_shapes=[pltpu.VMEM((n_comp, d), F32), pltpu.VMEM((n_comp, d), F32),
                        pltpu.VMEM((MOE_ROWS, d), F32), pltpu.SemaphoreType.DMA((2,)),
                        pltpu.SemaphoreType.DMA(())],
    )
    return pl.pallas_call(
        _dispatch_kernel,
        grid_spec=grid_spec,
        out_shape=jax.ShapeDtypeStruct((n_rows, d), F32),
        compiler_params=_params("arbitrary"),
        name="moe_dispatch",
    )(pend, padded, cseg, seg_first, seg_len, seg_tot, h, cpos_t)


def _moe_kernel(blk_e_ref, n_used_ref, first_ref, slot_ref, next_ref, x_ref, wi_hbm, bi_ref,
                wo_hbm, bo_ref, y_ref, wi_buf, wo_buf, wi_bf, wo_bf, wi_sem, wo_sem):
    f = wo_hbm.shape[1]
    b = pl.program_id(0)
    used = b < n_used_ref[0]
    e = blk_e_ref[b]
    slot = slot_ref[b]

    def fetch(expert, into):
        return (pltpu.make_async_copy(wi_hbm.at[expert], wi_buf.at[into], wi_sem.at[into]),
                pltpu.make_async_copy(wo_hbm.at[expert], wo_buf.at[into], wo_sem.at[into]))

    @pl.when(b == 0)
    def _():
        for cp in fetch(e, 0):
            cp.start()

    for s in range(2):
        @pl.when(used & (first_ref[b] == 1) & (slot == s))
        def _(s=s):
            @pl.when(next_ref[b] != e)
            def _():
                for cp in fetch(next_ref[b], 1 - s):
                    cp.start()
            for cp in fetch(e, s):
                cp.wait()
            wi_bf[...] = wi_buf[s].astype(BF16)
            wo_bf[...] = wo_buf[s].astype(BF16)

    @pl.when(used)
    def _():
        hdn = jnp.dot(x_ref[...].astype(BF16), wi_bf[...],
                      preferred_element_type=F32) + bi_ref[...]
        hg = jnp.minimum(hdn[:, :f], SWIGLU_LIMIT)
        hu = jnp.clip(hdn[:, f:], -SWIGLU_LIMIT, SWIGLU_LIMIT)
        act = hg * jax.nn.sigmoid(SWIGLU_ALPHA * hg) * (hu + 1.0)
        y_ref[...] = jnp.dot(act.astype(BF16), wo_bf[...],
                             preferred_element_type=F32) + bo_ref[...]

    @pl.when(jnp.logical_not(used))
    def _():
        y_ref[...] = jnp.zeros(y_ref.shape, F32)


def _moe_blocks(blk_e, n_used, padded, xs, w_in, b_in, w_out, b_out):
    n_rows, d = xs.shape
    e, _, f2 = w_in.shape
    f = w_out.shape[1]
    n_blk = n_rows // MOE_ROWS
    first = jnp.concatenate([jnp.ones((1,), I32), (blk_e[1:] != blk_e[:-1]).astype(I32)])
    ids = jnp.arange(e, dtype=I32)
    has_rows = padded > 0
    is_e = blk_e[:, None] == ids[None, :]
    slot = jnp.sum((ids[None, :] < blk_e[:, None]) & has_rows[None, :], axis=1) & 1
    later = jnp.where((ids[None, :] > ids[:, None]) & has_rows[None, :], ids[None, :], e)
    next_used = jnp.min(later, axis=1)
    next_used = jnp.where(next_used == e, ids, next_used).astype(I32)
    next_blk = jnp.sum(jnp.where(is_e, next_used[None, :], 0), axis=1).astype(I32)
    grid_spec = pltpu.PrefetchScalarGridSpec(
        num_scalar_prefetch=5,
        grid=(n_blk,),
        in_specs=[pl.BlockSpec((MOE_ROWS, d), lambda b, be, nu, *_: (jnp.minimum(b, nu[0] - 1), 0)),
                  pl.BlockSpec(memory_space=pl.ANY),
                  pl.BlockSpec((None, 1, f2), lambda b, be, *_: (be[b], 0, 0)),
                  pl.BlockSpec(memory_space=pl.ANY),
                  pl.BlockSpec((None, 1, d), lambda b, be, *_: (be[b], 0, 0))],
        out_specs=pl.BlockSpec((MOE_ROWS, d), lambda b, *_: (b, 0)),
        scratch_shapes=[pltpu.VMEM((2, d, f2), F32), pltpu.VMEM((2, f, d), F32),
                        pltpu.VMEM((d, f2), BF16), pltpu.VMEM((f, d), BF16),
                        pltpu.SemaphoreType.DMA((2,)), pltpu.SemaphoreType.DMA((2,))],
    )
    return pl.pallas_call(
        _moe_kernel,
        grid_spec=grid_spec,
        out_shape=jax.ShapeDtypeStruct((n_rows, d), F32),
        compiler_params=_params("arbitrary"),
        name="moe_experts",
    )(blk_e, n_used, first, slot.astype(I32), next_blk, xs, w_in, b_in, w_out, b_out)


def _run_sizes(tm):
    sizes, p = [], F32_SUBLANES
    while p <= tm:
        sizes.append(p)
        p *= 2
    return tuple(reversed(sizes))


def _stage_rows(tm):
    rows = tm * TOP_K + N_EXPERTS * 2 * (F32_SUBLANES - 1)
    return -(-rows // MXU_DEPTH) * MXU_DEPTH


def _final_kernel(start_ref, len_ref, seg_ref, tot_ref, h_ref, tg_ref, sp_ref, ys_ref, g_ref,
                  b_ref, o_ref, stage0, stage1, sem, *, alpha):
    i = pl.program_id(0)
    tm = h_ref.shape[0]
    stages = (stage0, stage1)
    n_stage = stage0.shape[0]

    def fetch(tile, into):
        for e in range(N_EXPERTS):
            ln = len_ref[tile * N_EXPERTS + e]
            first = start_ref[tile * N_EXPERTS + e]
            seg = seg_ref[tile * N_EXPERTS + e]
            for p in _run_sizes(tm):
                done = ln & (-2 * p)
                src = ys_ref.at[pl.ds(pl.multiple_of(first + done, F32_SUBLANES), p), :]
                dst = stages[into].at[pl.ds(pl.multiple_of(seg + done, F32_SUBLANES), p), :]
                pl.when((ln & p) != 0)(pltpu.make_async_copy(src, dst, sem.at[into]).start)

    @pl.when(i == 0)
    def _():
        stage0[...] = jnp.zeros(stage0.shape, F32)
        stage1[...] = jnp.zeros(stage1.shape, F32)
        fetch(0, 0)

    for slot in range(2):
        @pl.when((i & 1) == slot)
        def _(slot=slot):
            fetch(i + 1, 1 - slot)
            rows = pl.multiple_of(tot_ref[i], F32_SUBLANES)
            pltpu.make_async_copy(ys_ref.at[pl.ds(0, rows), :],
                                  stages[slot].at[pl.ds(0, rows), :], sem.at[slot]).wait()
            col = lax.broadcasted_iota(I32, (1, n_stage), 1)
            tg = tg_ref[...]
            sp = sp_ref[...]
            q = jnp.zeros((tm, n_stage), F32)
            for k in range(TOP_K):
                q = jnp.where(col == sp[:, k:k + 1], tg[:, k:k + 1], q)
            f = jnp.dot(q.astype(BF16), stages[slot][...].astype(BF16),
                        preferred_element_type=F32)
            o_ref[...] = _layer_norm(alpha * h_ref[...] + f, g_ref[...], b_ref[...])


def _final(run_start, run_len, run_seg, run_tot, h, top_g, spos, ys, ln_g, ln_b, alpha, tm,
           seq):
    t, d = h.shape
    spb = seq // tm
    grid_spec = pltpu.PrefetchScalarGridSpec(
        num_scalar_prefetch=4,
        grid=(t // tm,),
        in_specs=[pl.BlockSpec((tm, d), lambda i, *_: (i, 0)),
                  pl.BlockSpec((tm, LANES), lambda i, *_: (i, 0)),
                  pl.BlockSpec((tm, TOP_K), lambda i, *_: (i, 0)),
                  pl.BlockSpec(memory_space=pl.ANY),
                  pl.BlockSpec(ln_g.shape, lambda i, *_: (0, 0)),
                  pl.BlockSpec(ln_b.shape, lambda i, *_: (0, 0))],
        out_specs=pl.BlockSpec((None, tm, d), lambda i, *_: (i // spb, i % spb, 0)),
        scratch_shapes=[pltpu.VMEM((_stage_rows(tm), d), F32),
                        pltpu.VMEM((_stage_rows(tm), d), F32), pltpu.SemaphoreType.DMA((2,))],
    )
    return pl.pallas_call(
        functools.partial(_final_kernel, alpha=alpha),
        grid_spec=grid_spec,
        out_shape=jax.ShapeDtypeStruct((t // seq, seq, d), F32),
        compiler_params=_params("arbitrary"),
        name="combine_ln",
    )(run_start, run_len, run_seg, run_tot, h, top_g, spos, ys, ln_g, ln_b)


def _rope_tables(s):
    half = HEAD_DIM // 2
    inv = ROPE_THETA ** (-np.arange(half, dtype=np.float64) / half)
    ang = np.arange(s, dtype=np.float64)[:, None] * inv[None, :]
    cos, sin = np.cos(ang).astype(np.float32), np.sin(ang).astype(np.float32)
    reps = LANES // HEAD_DIM
    cos_t = np.tile(np.concatenate([cos, cos], axis=1), (1, reps))
    sin_t = np.tile(np.concatenate([-sin, sin], axis=1), (1, reps))
    return jnp.asarray(cos_t), jnp.asarray(sin_t)


def _ceil_to(v, m):
    return (v + m - 1) // m * m


def _moe_plan(top_e, rank, tile_base, counts, t, tm, dt):
    a = t * TOP_K
    per, nd = dt // tm, t // dt
    d_base = tile_base[::per]
    d_cnt = jnp.concatenate([d_base[1:], counts[None, :]], axis=0) - d_base
    seg_len = _ceil_to(d_cnt, F32_SUBLANES)
    padded = _ceil_to(jnp.sum(seg_len, axis=0), MOE_ROWS)
    pend = jnp.cumsum(padded)
    seg_first = (pend - padded)[None, :] + jnp.cumsum(seg_len, axis=0) - seg_len
    cseg = jnp.cumsum(seg_len, axis=1) - seg_len
    onehot = (top_e[:, :, None] == jnp.arange(N_EXPERTS, dtype=I32)).reshape(
        nd, dt, TOP_K, N_EXPERTS)

    def pick(table):
        return jnp.sum(jnp.where(onehot, table[:, None, None, :], 0), axis=3).reshape(t, TOP_K)

    local = rank - pick(d_base)
    dest = pick(seg_first) + local
    cpos = pick(cseg) + local
    run_first = jnp.repeat(seg_first - d_base, per, axis=0) + tile_base
    n_blk = -(-(a + nd * N_EXPERTS * (F32_SUBLANES - 1)) // MOE_ROWS) + N_EXPERTS
    blk_first = jnp.arange(n_blk, dtype=I32) * MOE_ROWS
    blk_e = jnp.minimum(jnp.sum(pend[None, :] <= blk_first[:, None], axis=1),
                        N_EXPERTS - 1).astype(I32)
    n_used = (pend[-1] // MOE_ROWS).astype(I32).reshape(1)
    flat = lambda v: v.astype(I32).reshape(-1)
    segments = (flat(cseg), flat(seg_first), flat(seg_len), jnp.sum(seg_len, axis=1).astype(I32))
    return (segments, dest.astype(I32), cpos.astype(I32), run_first, blk_e, n_used,
            n_blk * MOE_ROWS, pend.astype(I32), padded.astype(I32))


def _combine_plan(top_e, dest, run_first, tile_base, counts, tm):
    cnt = jnp.concatenate([tile_base[1:], counts[None, :]], axis=0) - tile_base
    lead = run_first & (F32_SUBLANES - 1)
    run_len = jnp.where(cnt > 0, _ceil_to(lead + cnt, F32_SUBLANES), 0)
    run_seg = jnp.cumsum(run_len, axis=1) - run_len
    shift = (run_seg + lead - run_first)[:, None, None, :]
    onehot = (top_e[:, :, None] == jnp.arange(N_EXPERTS, dtype=I32)).reshape(
        -1, tm, TOP_K, N_EXPERTS)
    spos = jnp.sum(jnp.where(onehot, shift, 0), axis=3).reshape(dest.shape) + dest
    flat = lambda v: jnp.pad(v.astype(I32), ((0, 1), (0, 0))).reshape(-1)
    run_tot = jnp.sum(run_len, axis=1).astype(I32)
    return flat(run_first - lead), flat(run_len), flat(run_seg), run_tot, spos.astype(I32)


def _layer(x, w_in, k_pe, k_w1, k_w2, v_pe, v_w1, v_w2, sinks, w_br_nsa, w_br_swa, w_out,
           ln1_g, ln1_b, w_router, b_router, w_e_in, b_e_in, w_e_out, b_e_out, ln2_g, ln2_b,
           alpha):
    b, s, d = x.shape
    t = b * s
    qb = Q_BLOCK
    nq_n, nkv = NSA_HEADS * HEAD_DIM, NSA_KV * HEAD_DIM
    nq_s, nkv_s = SWA_HEADS * HEAD_DIM, SWA_KV * HEAD_DIM
    widths = (nq_n, nkv, nkv, nkv, nkv, nkv, nkv, NSA_HEADS * 3, nq_s, nkv_s, nkv_s, 2 * d)
    offs = [0]
    for w in widths:
        offs.append(offs[-1] + w)
    col = lambda j: w_in[:, offs[j]:offs[j + 1]]
    (c_qn, c_kc, c_vc, c_ks, c_vs, c_kw, c_vw, c_gn, c_qs, c_k_s, c_v_s, c_gm) = map(col, range(12))
    w_rope = jnp.concatenate([c_qn, c_qs, c_ks, c_kw, c_k_s], axis=1).astype(BF16)
    w_plain = jnp.concatenate([c_kc, c_vc, c_vs, c_vw, c_v_s], axis=1).astype(BF16)
    gn_pad = LANES - NSA_HEADS * 3
    w_gate = jnp.concatenate([c_gm, c_gn, jnp.zeros((d, gn_pad), F32)], axis=1).astype(BF16)
    cos_t, sin_t = _rope_tables(s)

    qn_rot, qs_rot, kk_rot, qn_raw, plain, gates = _project(
        x, w_rope, w_plain, w_gate, cos_t, sin_t, min(256, s))

    nc = (s - CMP_BLOCK) // CMP_STRIDE + 1
    ncp = s // CMP_STRIDE
    half = CMP_STRIDE * HEAD_DIM

    def halves(cols):
        v = cols.reshape(b, s, NSA_KV, HEAD_DIM).transpose(0, 2, 1, 3)
        return v.reshape(b * NSA_KV, ncp, half)

    t2 = jnp.stack([halves(plain[:, :, 0:nkv]), halves(plain[:, :, nkv:2 * nkv])])
    t_lo = t2
    t_hi = jnp.concatenate([t2[:, :, 1:], jnp.zeros_like(t2[:, :, :1])], axis=2)
    pe2 = jnp.stack([k_pe.reshape(2, half), v_pe.reshape(2, half)])
    w1 = jnp.stack([k_w1, v_w1]).astype(BF16)
    w2 = jnp.stack([k_w2, v_w2]).astype(BF16)
    kvc = _compress(t_lo, t_hi, pe2, w1, w2).reshape(2, b, NSA_KV, ncp, HEAD_DIM)

    nsel = s // SEL_BLOCK
    nselp = -(-nsel // LANES) * LANES
    cstart = np.arange(ncp) * CMP_STRIDE
    sstart = np.arange(nselp) * SEL_BLOCK
    overlap = ((cstart[:, None] < sstart[None, :] + SEL_BLOCK)
               & (cstart[:, None] + CMP_BLOCK > sstart[None, :])
               & (np.arange(ncp)[:, None] < nc) & (np.arange(nselp)[None, :] < nsel))
    o_cmp, notsel = _cmp_attention(qn_raw, kvc[0], kvc[1],
                                   jnp.asarray(overlap.T.astype(BF16)),
                                   min(CMP_STEP_QUERIES, s))

    def group_major(cols):
        return cols.reshape(b, s, NSA_KV, HEAD_DIM).transpose(0, 2, 1, 3)

    k_sel = group_major(kk_rot[:, :, 0:nkv])
    onehot = (np.arange(s)[:, None] // SEL_BLOCK == np.arange(nselp)[None, :])
    k_tail = np.concatenate([np.zeros((s, LANES - HEAD_DIM), np.float32),
                             np.where(onehot, SEL_PENALTY, 0.0).astype(np.float32)], axis=1)
    k_aug = jnp.concatenate(
        [k_sel, jnp.broadcast_to(jnp.asarray(k_tail.astype(BF16)), (b, NSA_KV) + k_tail.shape)],
        axis=3)
    vt_all = plain[:, :, 2 * nkv:].transpose(0, 2, 1)
    o_sel = _sel_attention(qn_rot, notsel, k_aug, vt_all,
                           min(SEL_STEP_QUERIES, s), min(SEL_KEY_TILE, s))

    o_win = _band_attention(qn_rot, kk_rot, 1, vt_all, 1, nkv, NSA_WINDOW, None, qb)
    o_swa = _band_attention(qs_rot, kk_rot, 2, vt_all, 2, nkv_s, SWA_WINDOW, sinks, qb)

    gi = np.arange(LANES)
    ci = np.arange(3 * nq_n)
    expand = jnp.asarray(((gi[:, None] // 3 == (ci[None, :] % nq_n) // HEAD_DIM)
                          & (gi[:, None] % 3 == ci[None, :] // nq_n)
                          & (gi[:, None] < NSA_HEADS * 3)).astype(BF16))
    wr_pad = jnp.pad(w_router, ((0, 0), (0, LANES - N_EXPERTS)))
    wr_hi = wr_pad.astype(BF16)
    wr_lo = (wr_pad - wr_hi.astype(F32)).astype(BF16)
    b_r = jnp.concatenate([b_router, jnp.full((LANES - N_EXPERTS,), -jnp.inf, F32)]).reshape(1, LANES)
    tm = min(256, t)
    h, top_e, top_g, rank, counts, tile_base = _merge(
        o_cmp, o_sel, o_win, o_swa, gates, x, expand, w_br_nsa.astype(BF16),
        w_br_swa.astype(BF16), w_out.astype(BF16), ln1_g.reshape(1, d), ln1_b.reshape(1, d),
        wr_hi, wr_lo, b_r, alpha, tm)

    top_e, rank = top_e[:, :TOP_K], rank[:, :TOP_K]
    counts = counts[0, :N_EXPERTS].astype(I32)
    tile_base = tile_base[:, 0, :N_EXPERTS].astype(I32)
    dt = min(DISPATCH_TOKENS, t)
    segments, dest, cpos, run_first, blk_e, n_used, n_rows, pend, padded = _moe_plan(
        top_e, rank, tile_base, counts, t, tm, dt)
    xs = _dispatch(pend, padded, *segments, cpos.T, h, n_rows, dt)
    ys = _moe_blocks(blk_e, n_used, padded, xs, w_e_in, b_e_in.reshape(N_EXPERTS, 1, -1),
                     w_e_out, b_e_out.reshape(N_EXPERTS, 1, -1))
    run_start, run_len, run_seg, run_tot, spos = _combine_plan(
        top_e, dest, run_first, tile_base, counts, tm)
    return _final(run_start, run_len, run_seg, run_tot, h, top_g, spos, ys,
                  ln2_g.reshape(1, d), ln2_b.reshape(1, d), alpha, tm, s)


def kernel(x, w_in, nsa_k_pe, nsa_k_w1, nsa_k_w2, nsa_v_pe, nsa_v_w1, nsa_v_w2, swa_sinks, w_br_nsa, w_br_swa, w_out, ln1_g, ln1_b, w_router, b_router, w_expert_in, b_expert_in, w_expert_out, b_expert_out, ln2_g, ln2_b):
    depth = w_in.shape[0]
    alpha = (2.0 * depth) ** 0.25
    for l in range(depth):
        x = _layer(x, w_in[l], nsa_k_pe[l], nsa_k_w1[l], nsa_k_w2[l], nsa_v_pe[l], nsa_v_w1[l],
                   nsa_v_w2[l], swa_sinks[l], w_br_nsa[l], w_br_swa[l], w_out[l], ln1_g[l],
                   ln1_b[l], w_router[l], b_router[l], w_expert_in[l], b_expert_in[l],
                   w_expert_out[l], b_expert_out[l], ln2_g[l], ln2_b[l], alpha)
    return x
```

```python
import functools

import jax
import jax.numpy as jnp
import numpy as np
from jax import lax
from jax.experimental import pallas as pl
from jax.experimental.pallas import tpu as pltpu

BF16 = jnp.bfloat16
F32 = jnp.float32
I32 = jnp.int32

HEAD_DIM = 64
NSA_HEADS = 8
NSA_KV = 2
CMP_BLOCK = 32
CMP_STRIDE = 16
SEL_BLOCK = 64
SEL_TOPN = 16
NSA_WINDOW = 512
SWA_HEADS = 8
SWA_KV = 2
SWA_WINDOW = 128
Q_BLOCK = 128
ROPE_THETA = 10000.0
N_EXPERTS = 32
TOP_K = 4
SWIGLU_LIMIT = 7.0
SWIGLU_ALPHA = 1.702
LN_EPS = 1e-5

LANES = 128
BF16_SUBLANES = 16
F32_SUBLANES = 8
MXU_DEPTH = 256
MASKED = -1e30
M_INIT = -1e29
SEL_PENALTY = -(2.0 ** 100)
VMEM_LIMIT = 52 * 1024 * 1024
MOE_ROWS = 512
DISPATCH_TOKENS = 512
MERGE_STEP_TOKENS = 512
BAND_STEP_QUERIES = 512
CMP_STEP_QUERIES = 512
SEL_STEP_QUERIES = 128
SEL_KEY_TILE = 512

R_NSA = NSA_HEADS // NSA_KV
R_SWA = SWA_HEADS // SWA_KV
NT_DIMS = (((1,), (1,)), ((), ()))


def _params(*sem):
    return pltpu.CompilerParams(dimension_semantics=sem, vmem_limit_bytes=VMEM_LIMIT)


def _full(shape):
    n = len(shape)
    return pl.BlockSpec(shape, lambda *_: (0,) * n)


def _proj_kernel(x_ref, wr_ref, wp_ref, wg_ref, cos_ref, sin_ref,
                 qn_rot_ref, qs_rot_ref, kk_rot_ref, qn_raw_ref, plain_ref, gates_ref):
    xb = x_ref[...].astype(BF16)
    acc = jnp.dot(xb, wr_ref[...], preferred_element_type=F32)
    cos = cos_ref[...]
    sin = sin_ref[...]
    lane = lax.broadcasted_iota(I32, cos.shape, 1)
    first_half = (lane & (HEAD_DIM - 1)) < HEAD_DIM // 2

    def rope(t):
        partner = jnp.where(first_half, pltpu.roll(t, LANES - HEAD_DIM // 2, 1),
                            pltpu.roll(t, HEAD_DIM // 2, 1))
        return (t * cos + partner * sin).astype(BF16)

    nq = qn_rot_ref.shape[1] // LANES
    ns = qs_rot_ref.shape[1] // LANES
    nk = kk_rot_ref.shape[1] // LANES
    for c in range(nq):
        qn_rot_ref[:, c * LANES:(c + 1) * LANES] = rope(acc[:, c * LANES:(c + 1) * LANES])
    for c in range(ns):
        o = (nq + c) * LANES
        qs_rot_ref[:, c * LANES:(c + 1) * LANES] = rope(acc[:, o:o + LANES])
    for c in range(nk):
        o = (nq + ns + c) * LANES
        kk_rot_ref[:, c * LANES:(c + 1) * LANES] = rope(acc[:, o:o + LANES])
    qn_raw_ref[...] = acc[:, :nq * LANES].astype(BF16)
    plain_ref[...] = jnp.dot(xb, wp_ref[...], preferred_element_type=F32).astype(BF16)
    gates_ref[...] = jnp.dot(xb, wg_ref[...], preferred_element_type=F32)


def _project(x, w_rope, w_plain, w_gate, cos_t, sin_t, tm):
    b, seq, d = x.shape
    t = b * seq
    nr, npl, ng = w_rope.shape[1], w_plain.shape[1], w_gate.shape[1]
    nqn, nqs = NSA_HEADS * HEAD_DIM, SWA_HEADS * HEAD_DIM
    nkk = nr - nqn - nqs
    spb = seq // tm
    row = lambda i: (i, 0)
    seq_blk = lambda n: pl.BlockSpec((None, tm, n), lambda i: (i // spb, i % spb, 0))
    seq_out = lambda n: jax.ShapeDtypeStruct((b, seq, n), BF16)
    return pl.pallas_call(
        _proj_kernel,
        grid=(t // tm,),
        in_specs=[seq_blk(d), _full(w_rope.shape), _full(w_plain.shape), _full(w_gate.shape),
                  pl.BlockSpec((tm, LANES), lambda i: (i % spb, 0)),
                  pl.BlockSpec((tm, LANES), lambda i: (i % spb, 0))],
        out_specs=[seq_blk(nqn), seq_blk(nqs), seq_blk(nkk), seq_blk(nqn), seq_blk(npl),
                   pl.BlockSpec((tm, ng), row)],
        out_shape=[seq_out(nqn), seq_out(nqs), seq_out(nkk), seq_out(nqn), seq_out(npl),
                   jax.ShapeDtypeStruct((t, ng), F32)],
        compiler_params=_params("parallel"),
        name="proj",
    )(x, w_rope, w_plain, w_gate, cos_t, sin_t)


def _compress_kernel(a_ref, b_ref, pe_ref, w1_ref, w2_ref, out_ref):
    half = a_ref.shape[1]
    a = (a_ref[...].astype(F32) + pe_ref[0:1, :]).astype(BF16)
    b = (b_ref[...].astype(F32) + pe_ref[1:2, :]).astype(BF16)
    hid = jnp.dot(a, w1_ref[0:half, :], preferred_element_type=F32)
    hid = hid + jnp.dot(b, w1_ref[half:2 * half, :], preferred_element_type=F32)
    act = jax.nn.gelu(hid).astype(BF16)
    out_ref[...] = jnp.dot(act, w2_ref[...], preferred_element_type=F32).astype(BF16)


def _compress(t_lo, t_hi, pe2, w1, w2):
    two, bg, ncp, half = t_lo.shape
    hid = w1.shape[2]
    blk = lambda shape: pl.BlockSpec((None, None) + shape, lambda j, i: (j, i, 0, 0))
    wsp = lambda shape: pl.BlockSpec((None,) + shape, lambda j, i: (j, 0, 0))
    return pl.pallas_call(
        _compress_kernel,
        grid=(two, bg),
        in_specs=[blk((ncp, half)), blk((ncp, half)), wsp((2, half)), wsp((2 * half, hid)),
                  wsp((hid, HEAD_DIM))],
        out_specs=blk((ncp, HEAD_DIM)),
        out_shape=jax.ShapeDtypeStruct((two, bg, ncp, HEAD_DIM), BF16),
        compiler_params=_params("parallel", "parallel"),
        name="compress",
    )(t_lo, t_hi, pe2, w1, w2)


def _stack_heads(q, g, r):
    return jnp.concatenate(
        [q[:, (g * r + j) * HEAD_DIM:(g * r + j + 1) * HEAD_DIM] for j in range(r)], axis=0)


def _unstack_heads_t(parts, r, qb):
    blocks = []
    for o in parts:
        for j in range(0, r, 2):
            pair = jnp.concatenate([o[:, j * qb:(j + 1) * qb], o[:, (j + 1) * qb:(j + 2) * qb]],
                                   axis=0)
            blocks.append(pair.T)
    return jnp.concatenate(blocks, axis=1)


def _topk_mask_cols(vals, k):
    n = vals.shape[0]
    row = lax.broadcasted_iota(I32, vals.shape, 0).astype(F32)
    taken = jnp.zeros(vals.shape, F32)
    work = vals
    for _ in range(k):
        mx = jnp.max(work, axis=0, keepdims=True)
        first = jnp.min(jnp.where(work == mx, row, float(n)), axis=0, keepdims=True)
        pick = row == first
        taken = jnp.where(pick, 1.0, taken)
        work = jnp.where(pick, -jnp.inf, work)
    return taken > 0.5


def _cmp_kernel(q_ref, kc_ref, vct_ref, ovt_ref, o_ref, notsel_ref):
    qb = q_ref.shape[0]
    ncp = kc_ref.shape[1]
    nselp = ovt_ref.shape[0]
    i = pl.program_id(1)
    rows = R_NSA * qb
    pos = i * qb + (lax.broadcasted_iota(I32, (1, rows), 1) & (qb - 1))
    cend = lax.broadcasted_iota(I32, (ncp, 1), 0) * CMP_STRIDE + (CMP_BLOCK - 1)
    bias = jnp.where(cend <= pos, 0.0, MASKED)
    live = (pos >= CMP_BLOCK - 1).astype(F32)
    q = q_ref[...]
    outs, imps = [], []
    for g in range(NSA_KV):
        qg = _stack_heads(q, g, R_NSA) * (HEAD_DIM ** -0.5)
        st = lax.dot_general(kc_ref[g], qg, NT_DIMS, preferred_element_type=F32)
        st = st + bias
        e = jnp.exp(st - jnp.max(st, axis=0, keepdims=True))
        pt = e * (live / jnp.sum(e, axis=0, keepdims=True))
        outs.append(jnp.dot(vct_ref[g], pt.astype(BF16), preferred_element_type=F32))
        psum = pt[:, 0:qb]
        for j in range(1, R_NSA):
            psum = psum + pt[:, j * qb:(j + 1) * qb]
        p_hi = psum.astype(BF16)
        p_lo = (psum - p_hi.astype(F32)).astype(BF16)
        imps.append(jnp.dot(ovt_ref[...], p_hi, preferred_element_type=F32)
                    + jnp.dot(ovt_ref[...], p_lo, preferred_element_type=F32))
    imp = jnp.concatenate(imps, axis=1)
    lane = lax.broadcasted_iota(I32, (1, NSA_KV * qb), 1)
    cur = (i * qb + (lane & (qb - 1))) >> 6
    jb = lax.broadcasted_iota(I32, (nselp, 1), 0)
    forced = (jb == 0) | (jb == cur) | (jb == cur - 1)
    imp = jnp.where(jb > cur, -1.0, jnp.where(forced, 1e6, imp))
    notsel = jnp.where(_topk_mask_cols(imp, SEL_TOPN), 0.0, 1.0)
    for g in range(NSA_KV):
        notsel_ref[g] = notsel[:, g * qb:(g + 1) * qb].T.astype(BF16)
    o_ref[...] = _unstack_heads_t(outs, R_NSA, qb)


def _cmp_attention(q_raw, kc, vc, overlap_t, qb):
    vc = vc.transpose(0, 1, 3, 2)
    b, s, hq = q_raw.shape
    _, g, ncp, dh = kc.shape
    nselp = overlap_t.shape[0]
    return pl.pallas_call(
        _cmp_kernel,
        grid=(b, s // qb),
        in_specs=[pl.BlockSpec((None, qb, hq), lambda bi, i: (bi, i, 0)),
                  pl.BlockSpec((None, g, ncp, dh), lambda bi, i: (bi, 0, 0, 0)),
                  pl.BlockSpec((None, g, dh, ncp), lambda bi, i: (bi, 0, 0, 0)),
                  _full(overlap_t.shape)],
        out_specs=[pl.BlockSpec((None, qb, hq), lambda bi, i: (bi, i, 0)),
                   pl.BlockSpec((None, g, qb, nselp), lambda bi, i: (bi, 0, i, 0))],
        out_shape=[jax.ShapeDtypeStruct((b, s, hq), F32),
                   jax.ShapeDtypeStruct((b, g, s, nselp), BF16)],
        compiler_params=_params("parallel", "parallel"),
        name="cmp_attn",
    )(q_raw, kc, vc, overlap_t)


def _sel_kernel(q_ref, notsel_ref, k_ref, vt_ref, o_ref, *score_bufs, tk):
    s_even, s_odd = score_bufs[:NSA_KV], score_bufs[NSA_KV:]
    qb = q_ref.shape[0]
    i = pl.program_id(1)
    rows = R_NSA * qb
    qpos = i * qb + (lax.broadcasted_iota(I32, (1, rows), 1) & (qb - 1))
    n_clear = (i * qb) // tk
    q = q_ref[...]
    q_augs = []
    for g in range(NSA_KV):
        qg = _stack_heads(q, g, R_NSA) * (HEAD_DIM ** -0.5)
        q_augs.append(jnp.concatenate(
            [qg, jnp.zeros((rows, LANES - HEAD_DIM), BF16),
             jnp.concatenate([notsel_ref[g]] * R_NSA, axis=0)], axis=1))

    def scores(kt, g):
        start = pl.multiple_of(kt * tk, tk)
        return lax.dot_general(k_ref[g, pl.ds(start, tk), :], q_augs[g], NT_DIMS,
                               preferred_element_type=F32)

    def consume(kt, g, s_ref, m, acc, diag_keys=None):
        nk = tk if diag_keys is None else diag_keys
        start = pl.multiple_of(kt * tk, tk)
        vt_t = jnp.concatenate([vt_ref[g * HEAD_DIM:(g + 1) * HEAD_DIM, pl.ds(start, nk)],
                                jnp.ones((BF16_SUBLANES, nk), vt_ref.dtype)], axis=0)
        st = s_ref[0:nk, :]
        if diag_keys is not None:
            kpos = start + lax.broadcasted_iota(I32, (nk, 1), 0)
            st = jnp.where(kpos <= qpos, st, MASKED)
        m_new = jnp.maximum(m, jnp.max(st, axis=0, keepdims=True))
        pt = jnp.exp(st - m_new).astype(BF16)
        acc = jnp.exp(m - m_new) * acc + jnp.dot(vt_t, pt, preferred_element_type=F32)
        return m_new, acc

    def advance(kt, carry, cur, nxt):
        new = []
        for g in range(NSA_KV):
            nxt[g][...] = scores(kt + 1, g)
            new.append(consume(kt, g, cur[g], *carry[g]))
        return tuple(new)

    def pair(j, carry):
        carry = advance(2 * j, carry, s_even, s_odd)
        return advance(2 * j + 1, carry, s_odd, s_even)

    def finish(carry, cur):
        where = (i * qb - n_clear * tk) // qb
        for v in range(tk // qb):
            @pl.when(where == v)
            def _(v=v):
                outs = []
                for g in range(NSA_KV):
                    _, acc = consume(n_clear, g, cur[g], *carry[g], diag_keys=(v + 1) * qb)
                    outs.append(acc[:HEAD_DIM] / acc[HEAD_DIM:HEAD_DIM + 1])
                o_ref[...] = _unstack_heads_t(outs, R_NSA, qb)

    for g in range(NSA_KV):
        s_even[g][...] = scores(0, g)
    init = tuple((jnp.full((1, rows), M_INIT, F32),
                  jnp.zeros((HEAD_DIM + BF16_SUBLANES, rows), F32))
                 for _ in range(NSA_KV))
    carry = lax.fori_loop(0, n_clear // 2, pair, init)

    @pl.when((n_clear & 1) == 0)
    def _():
        finish(carry, s_even)

    @pl.when((n_clear & 1) == 1)
    def _():
        finish(advance(n_clear - 1, carry, s_even, s_odd), s_odd)


def _sel_attention(q_rot, notsel, k_aug, vt_aug, qb, tk):
    b, s, hq = q_rot.shape
    _, g, _, kw = k_aug.shape
    nselp = notsel.shape[3]
    return pl.pallas_call(
        functools.partial(_sel_kernel, tk=tk),
        grid=(b, s // qb),
        in_specs=[pl.BlockSpec((None, qb, hq), lambda bi, i: (bi, i, 0)),
                  pl.BlockSpec((None, g, qb, nselp), lambda bi, i: (bi, 0, i, 0)),
                  pl.BlockSpec((None, g, s, kw), lambda bi, i: (bi, 0, 0, 0)),
                  pl.BlockSpec((None, g * HEAD_DIM, s), lambda bi, i: (bi, 0, 0))],
        out_specs=pl.BlockSpec((None, qb, hq), lambda bi, i: (bi, i, 0)),
        out_shape=jax.ShapeDtypeStruct((b, s, hq), F32),
        scratch_shapes=[pltpu.VMEM((tk, R_NSA * qb), F32)] * (2 * g),
        compiler_params=_params("parallel", "parallel"),
        name="sel_attn",
    )(q_rot, notsel, k_aug, vt_aug)


def _band_kernel(*refs, window, wlen, r, kv, qb, has_sinks):
    if has_sinks:
        sink_ref, q_ref, k_ref, vt_ref, o_ref = refs
    else:
        q_ref, k_ref, vt_ref, o_ref = refs
    nsub = q_ref.shape[0] // qb
    rows = r * qb
    lane = lax.broadcasted_iota(I32, (1, rows), 1)

    def mask_bias(i):
        start = jnp.maximum((i + 1) * qb - wlen, 0)
        rel = i * qb + (lane & (qb - 1)) - start - lax.broadcasted_iota(I32, (wlen, 1), 0)
        return jnp.where((rel >= 0) & (rel < window), 0.0, MASKED)

    def body(shared_bias):
        first = pl.program_id(1) * nsub
        if shared_bias:
            bias = mask_bias(first)
        for sb in range(nsub):
            i = first + sb
            if not shared_bias:
                bias = mask_bias(i)
            start = pl.multiple_of(jnp.maximum((i + 1) * qb - wlen, 0), qb)
            q = q_ref[sb * qb:(sb + 1) * qb, :]
            kw = k_ref[pl.ds(start, wlen), :]
            vtw = vt_ref[:, pl.ds(start, wlen)]
            outs = []
            for g in range(kv):
                qg = _stack_heads(q, g, r) * (HEAD_DIM ** -0.5)
                st = lax.dot_general(kw[:, g * HEAD_DIM:(g + 1) * HEAD_DIM], qg, NT_DIMS,
                                     preferred_element_type=F32) + bias
                m = jnp.max(st, axis=0, keepdims=True)
                if has_sinks:
                    sk = jnp.full((1, rows), sink_ref[g * r], F32)
                    for j in range(1, r):
                        sk = jnp.where(lane >= j * qb, sink_ref[g * r + j], sk)
                    m = jnp.maximum(m, sk)
                e = jnp.exp(st - m).astype(BF16)
                v_ones = jnp.concatenate([vtw[g * HEAD_DIM:(g + 1) * HEAD_DIM, :],
                                          jnp.ones((BF16_SUBLANES, wlen), vtw.dtype)], axis=0)
                ot = jnp.dot(v_ones, e, preferred_element_type=F32)
                den = ot[HEAD_DIM:HEAD_DIM + 1]
                if has_sinks:
                    den = den + jnp.exp(sk - m)
                outs.append(ot[:HEAD_DIM] / den)
            o_ref[sb * qb:(sb + 1) * qb, :] = _unstack_heads_t(outs, r, qb)

    if nsub * qb >= wlen - qb:
        @pl.when(pl.program_id(1) == 0)
        def _():
            body(False)

        @pl.when(pl.program_id(1) > 0)
        def _():
            body(True)
    else:
        body(False)


def _band_attention(q_rot, k_all, k_blk, vt_all, v_blk, gk, window, sinks, qb):
    b, s, hq = q_rot.shape
    assert k_all.shape[2] % gk == 0 and vt_all.shape[1] % gk == 0
    kv = gk // HEAD_DIM
    r = hq // gk
    back = -(-window // qb)
    wlen = (back + 1) * qb
    assert wlen <= s
    has_sinks = sinks is not None
    tq = min(BAND_STEP_QUERIES, s)
    in_specs = [pl.BlockSpec((None, tq, hq), lambda bi, i: (bi, i, 0)),
                pl.BlockSpec((None, s, gk), lambda bi, i: (bi, 0, k_blk)),
                pl.BlockSpec((None, gk, s), lambda bi, i: (bi, v_blk, 0))]
    args = [q_rot, k_all, vt_all]
    if has_sinks:
        in_specs = [pl.BlockSpec(memory_space=pltpu.SMEM)] + in_specs
        args = [sinks.astype(F32)] + args
    return pl.pallas_call(
        functools.partial(_band_kernel, window=window, wlen=wlen, r=r, kv=kv, qb=qb,
                          has_sinks=has_sinks),
        grid=(b, s // tq),
        in_specs=in_specs,
        out_specs=pl.BlockSpec((None, tq, hq), lambda bi, i: (bi, i, 0)),
        out_shape=jax.ShapeDtypeStruct((b, s, hq), F32),
        compiler_params=_params("parallel", "parallel"),
        name="band_attn_sink" if has_sinks else "band_attn",
    )(*args)


def _layer_norm(v, g, b):
    mu = jnp.mean(v, axis=1, keepdims=True)
    c = v - mu
    var = jnp.mean(c * c, axis=1, keepdims=True)
    return c * lax.rsqrt(var + LN_EPS) * g + b


def _split_dot(a, w):
    hi = a.astype(BF16)
    lo = (a - hi.astype(F32)).astype(BF16)
    return (jnp.dot(hi, w, preferred_element_type=F32)
            + jnp.dot(lo, w, preferred_element_type=F32))


def _merge_kernel(ocmp_ref, osel_ref, owin_ref, oswa_ref, gates_ref, x_ref, exp_ref,
                  wbn_ref, wbs_ref, wo_ref, lng_ref, lnb_ref, wrh_ref, wrl_ref, br_ref,
                  h_ref, te_ref, tg_ref, tr_ref, counts_ref, base_ref, cnt_ref, *, alpha, sub):
    d = x_ref.shape[1]
    hq = ocmp_ref.shape[1]
    n_sub = x_ref.shape[0] // sub
    col = lax.broadcasted_iota(I32, (sub, LANES), 1).astype(F32)

    def route(rows):
        gates = gates_ref[rows, :]
        gn = jax.nn.sigmoid(gates[:, 2 * d:])
        gexp = _split_dot(gn, exp_ref[...])
        o_nsa = (gexp[:, 0:hq] * ocmp_ref[rows, :] + gexp[:, hq:2 * hq] * osel_ref[rows, :]
                 + gexp[:, 2 * hq:3 * hq] * owin_ref[rows, :])
        y_nsa = jnp.dot(o_nsa.astype(BF16), wbn_ref[...], preferred_element_type=F32)
        y_swa = jnp.dot(oswa_ref[rows, :].astype(BF16), wbs_ref[...],
                        preferred_element_type=F32)
        gm = jax.nn.sigmoid(gates[:, :2 * d])
        mixed = gm[:, :d] * y_nsa + gm[:, d:] * y_swa
        z = jnp.dot(mixed.astype(BF16), wo_ref[...], preferred_element_type=F32)
        h = _layer_norm(alpha * x_ref[rows, :] + z, lng_ref[...], lnb_ref[...])
        h_ref[rows, :] = h
        h_hi = h.astype(BF16)
        h_lo = (h - h_hi.astype(F32)).astype(BF16)
        logits = (jnp.dot(h_hi, wrh_ref[...], preferred_element_type=F32)
                  + jnp.dot(h_lo, wrh_ref[...], preferred_element_type=F32)
                  + jnp.dot(h_hi, wrl_ref[...], preferred_element_type=F32)) + br_ref[...]
        work = logits
        vals, ids = [], []
        for _ in range(TOP_K):
            mx = jnp.max(work, axis=1, keepdims=True)
            first = jnp.min(jnp.where(work == mx, col, float(LANES)), axis=1, keepdims=True)
            vals.append(mx)
            ids.append(first)
            work = jnp.where(col == first, -jnp.inf, work)
        es = [jnp.exp(v - vals[0]) for v in vals]
        den = es[0]
        for e in es[1:]:
            den = den + e
        hits = jnp.zeros(logits.shape, F32)
        te = jnp.zeros(logits.shape, F32)
        tg = jnp.zeros(logits.shape, F32)
        for k in range(TOP_K):
            hits = jnp.where(col == ids[k], 1.0, hits)
            te = jnp.where(col == float(k), ids[k], te)
            tg = jnp.where(col == float(k), es[k] / den, tg)
        te_ref[rows, :] = te.astype(I32)
        tg_ref[rows, :] = tg
        earlier = (lax.broadcasted_iota(I32, (sub, sub), 1)
                   < lax.broadcasted_iota(I32, (sub, sub), 0))
        prefix = jnp.dot(jnp.where(earlier, 1.0, 0.0).astype(BF16), hits.astype(BF16),
                         preferred_element_type=F32)
        return ids, prefix, jnp.sum(hits, axis=0, keepdims=True)

    routed = [route(slice(j * sub, (j + 1) * sub)) for j in range(n_sub)]

    @pl.when(pl.program_id(0) == 0)
    def _():
        cnt_ref[...] = jnp.zeros(cnt_ref.shape, F32)

    for j, (ids, prefix, total) in enumerate(routed):
        base = cnt_ref[...]
        base_ref[j] = base
        before = prefix + base
        tr = jnp.zeros((sub, LANES), F32)
        for k in range(TOP_K):
            rank = jnp.sum(jnp.where(col == ids[k], before, 0.0), axis=1, keepdims=True)
            tr = jnp.where(col == float(k), rank, tr)
        tr_ref[j * sub:(j + 1) * sub, :] = tr.astype(I32)
        cnt_ref[...] = base + total
    counts_ref[...] = cnt_ref[...]


def _merge(o_cmp, o_sel, o_win, o_swa, gates, x, expand, w_bn, w_bs, w_o, ln_g, ln_b,
           wr_hi, wr_lo, b_r, alpha, sub):
    b, s, d = x.shape
    t = b * s
    hq = o_cmp.shape[2]
    tm = min(MERGE_STEP_TOKENS, s)
    n_sub = tm // sub
    spb = s // tm
    row = lambda i: (i, 0)
    tok = lambda n: pl.BlockSpec((tm, n), row)
    seq = lambda n: pl.BlockSpec((None, tm, n), lambda i: (i // spb, i % spb, 0))
    return pl.pallas_call(
        functools.partial(_merge_kernel, alpha=alpha, sub=sub),
        grid=(t // tm,),
        in_specs=[seq(hq), seq(hq), seq(hq), seq(hq), tok(gates.shape[1]), seq(d),
                  _full(expand.shape), _full(w_bn.shape), _full(w_bs.shape), _full(w_o.shape),
                  _full(ln_g.shape), _full(ln_b.shape), _full(wr_hi.shape), _full(wr_lo.shape),
                  _full(b_r.shape)],
        out_specs=[tok(d), tok(LANES), tok(LANES), tok(LANES), _full((1, LANES)),
                   pl.BlockSpec((n_sub, 1, LANES), lambda i: (i, 0, 0))],
        out_shape=[jax.ShapeDtypeStruct((t, d), F32), jax.ShapeDtypeStruct((t, LANES), I32),
                   jax.ShapeDtypeStruct((t, LANES), F32), jax.ShapeDtypeStruct((t, LANES), I32),
                   jax.ShapeDtypeStruct((1, LANES), F32),
                   jax.ShapeDtypeStruct((t // sub, 1, LANES), F32)],
        scratch_shapes=[pltpu.VMEM((1, LANES), F32)],
        compiler_params=_params("arbitrary"),
        name="merge_ln_router",
    )(o_cmp, o_sel, o_win, o_swa, gates, x, expand, w_bn, w_bs, w_o, ln_g, ln_b,
      wr_hi, wr_lo, b_r)


def _dispatch_kernel(pend_ref, padded_ref, cseg_ref, first_ref, len_ref, tot_ref,
                     h_ref, cpos_ref, xs_ref, comp0, comp1, zbuf, sem, zsem):
    j = pl.program_id(0)
    nt = pl.num_programs(0)
    tm = h_ref.shape[0]
    n_rows = xs_ref.shape[0]
    comps = (comp0, comp1)
    n_comp = comp0.shape[0]

    @pl.when(j == 0)
    def _():
        zbuf[...] = jnp.zeros(zbuf.shape, F32)
        used_rows = pend_ref[N_EXPERTS - 1]

        def zero_copy(start):
            start = pl.multiple_of(start, MOE_ROWS)
            return pltpu.make_async_copy(zbuf, xs_ref.at[pl.ds(start, MOE_ROWS), :], zsem)

        blocks = [(padded_ref[e] > 0, pend_ref[e] - MOE_ROWS) for e in range(N_EXPERTS)]
        blocks += [(used_rows + b * MOE_ROWS < n_rows, used_rows + b * MOE_ROWS)
                   for b in range(n_rows // MOE_ROWS - (nt * tm * TOP_K) // MOE_ROWS)]
        for cond, start in blocks:
            @pl.when(cond)
            def _(start=start):
                zero_copy(start).start()
        for cond, start in blocks:
            @pl.when(cond)
            def _(start=start):
                zero_copy(start).wait()

    def wait_writes(tile, slot):
        rows = pl.multiple_of(tot_ref[tile], F32_SUBLANES)
        pltpu.make_async_copy(comps[slot].at[pl.ds(0, rows), :], xs_ref.at[pl.ds(0, rows), :],
                              sem.at[slot]).wait()

    for slot in range(2):
        @pl.when((j >= 2) & ((j & 1) == slot))
        def _(slot=slot):
            wait_writes(j - 2, slot)

        @pl.when((j & 1) == slot)
        def _(slot=slot):
            row = lax.broadcasted_iota(I32, (n_comp, 1), 0)
            cpos = cpos_ref[...]
            sel = jnp.zeros((n_comp, tm), F32)
            for k in range(TOP_K):
                sel = jnp.where(row == cpos[k:k + 1, :], 1.0, sel)
            comps[slot][...] = jnp.dot(sel.astype(BF16), h_ref[...].astype(BF16),
                                       preferred_element_type=F32)
            for e in range(N_EXPERTS):
                ln = len_ref[j * N_EXPERTS + e]
                first = first_ref[j * N_EXPERTS + e]
                seg = cseg_ref[j * N_EXPERTS + e]
                for p in _run_sizes(tm):
                    done = ln & (-2 * p)
                    src = comps[slot].at[pl.ds(pl.multiple_of(seg + done, F32_SUBLANES), p), :]
                    dst = xs_ref.at[pl.ds(pl.multiple_of(first + done, F32_SUBLANES), p), :]
                    pl.when((ln & p) != 0)(pltpu.make_async_copy(src, dst, sem.at[slot]).start)

        @pl.when((j == nt - 1) & ((j & 1) == slot))
        def _(slot=slot):
            @pl.when(j >= 1)
            def _():
                wait_writes(j - 1, 1 - slot)
            wait_writes(j, slot)


def _dispatch(pend, padded, cseg, seg_first, seg_len, seg_tot, cpos_t, h, n_rows, tm):
    t, d = h.shape
    n_comp = -(-(tm * TOP_K + N_EXPERTS * (F32_SUBLANES - 1)) // MXU_DEPTH) * MXU_DEPTH
    grid_spec = pltpu.PrefetchScalarGridSpec(
        num_scalar_prefetch=6,
        grid=(t // tm,),
        in_specs=[pl.BlockSpec((tm, d), lambda i, *_: (i, 0)),
                  pl.BlockSpec((TOP_K, tm), lambda i, *_: (0, i))],
        out_specs=pl.BlockSpec(memory_space=pl.ANY),
        scratch_shapes=[pltpu.VMEM((n_comp, d), F32), pltpu.VMEM((n_comp, d), F32),
                        pltpu.VMEM((MOE_ROWS, d), F32), pltpu.SemaphoreType.DMA((2,)),
                        pltpu.SemaphoreType.DMA(())],
    )
    return pl.pallas_call(
        _dispatch_kernel,
        grid_spec=grid_spec,
        out_shape=jax.ShapeDtypeStruct((n_rows, d), F32),
        compiler_params=_params("arbitrary"),
        name="moe_dispatch",
    )(pend, padded, cseg, seg_first, seg_len, seg_tot, h, cpos_t)


def _moe_kernel(blk_e_ref, n_used_ref, first_ref, slot_ref, next_ref, x_ref, wi_hbm, bi_ref,
                wo_hbm, bo_ref, y_ref, wi_buf, wo_buf, wi_bf, wo_bf, wi_sem, wo_sem):
    f = wo_hbm.shape[1]
    b = pl.program_id(0)
    used = b < n_used_ref[0]
    e = blk_e_ref[b]
    slot = slot_ref[b]

    def fetch(expert, into):
        return (pltpu.make_async_copy(wi_hbm.at[expert], wi_buf.at[into], wi_sem.at[into]),
                pltpu.make_async_copy(wo_hbm.at[expert], wo_buf.at[into], wo_sem.at[into]))

    @pl.when(b == 0)
    def _():
        for cp in fetch(e, 0):
            cp.start()

    for s in range(2):
        @pl.when(used & (first_ref[b] == 1) & (slot == s))
        def _(s=s):
            @pl.when(next_ref[b] != e)
            def _():
                for cp in fetch(next_ref[b], 1 - s):
                    cp.start()
            for cp in fetch(e, s):
                cp.wait()
            wi_bf[...] = wi_buf[s].astype(BF16)
            wo_bf[...] = wo_buf[s].astype(BF16)

    @pl.when(used)
    def _():
        hdn = jnp.dot(x_ref[...].astype(BF16), wi_bf[...],
                      preferred_element_type=F32) + bi_ref[...]
        hg = jnp.minimum(hdn[:, :f], SWIGLU_LIMIT)
        hu = jnp.clip(hdn[:, f:], -SWIGLU_LIMIT, SWIGLU_LIMIT)
        act = hg * jax.nn.sigmoid(SWIGLU_ALPHA * hg) * (hu + 1.0)
        y_ref[...] = jnp.dot(act.astype(BF16), wo_bf[...],
                             preferred_element_type=F32) + bo_ref[...]

    @pl.when(jnp.logical_not(used))
    def _():
        y_ref[...] = jnp.zeros(y_ref.shape, F32)


def _moe_blocks(blk_e, n_used, padded, xs, w_in, b_in, w_out, b_out):
    n_rows, d = xs.shape
    e, _, f2 = w_in.shape
    f = w_out.shape[1]
    n_blk = n_rows // MOE_ROWS
    first = jnp.concatenate([jnp.ones((1,), I32), (blk_e[1:] != blk_e[:-1]).astype(I32)])
    ids = jnp.arange(e, dtype=I32)
    has_rows = padded > 0
    is_e = blk_e[:, None] == ids[None, :]
    slot = jnp.sum((ids[None, :] < blk_e[:, None]) & has_rows[None, :], axis=1) & 1
    later = jnp.where((ids[None, :] > ids[:, None]) & has_rows[None, :], ids[None, :], e)
    next_used = jnp.min(later, axis=1)
    next_used = jnp.where(next_used == e, ids, next_used).astype(I32)
    next_blk = jnp.sum(jnp.where(is_e, next_used[None, :], 0), axis=1).astype(I32)
    grid_spec = pltpu.PrefetchScalarGridSpec(
        num_scalar_prefetch=5,
        grid=(n_blk,),
        in_specs=[pl.BlockSpec((MOE_ROWS, d), lambda b, be, nu, *_: (jnp.minimum(b, nu[0] - 1), 0)),
                  pl.BlockSpec(memory_space=pl.ANY),
                  pl.BlockSpec((None, 1, f2), lambda b, be, *_: (be[b], 0, 0)),
                  pl.BlockSpec(memory_space=pl.ANY),
                  pl.BlockSpec((None, 1, d), lambda b, be, *_: (be[b], 0, 0))],
        out_specs=pl.BlockSpec((MOE_ROWS, d), lambda b, *_: (b, 0)),
        scratch_shapes=[pltpu.VMEM((2, d, f2), F32), pltpu.VMEM((2, f, d), F32),
                        pltpu.VMEM((d, f2), BF16), pltpu.VMEM((f, d), BF16),
                        pltpu.SemaphoreType.DMA((2,)), pltpu.SemaphoreType.DMA((2,))],
    )
    return pl.pallas_call(
        _moe_kernel,
        grid_spec=grid_spec,
        out_shape=jax.ShapeDtypeStruct((n_rows, d), F32),
        compiler_params=_params("arbitrary"),
        name="moe_experts",
    )(blk_e, n_used, first, slot.astype(I32), next_blk, xs, w_in, b_in, w_out, b_out)


def _run_sizes(tm):
    sizes, p = [], F32_SUBLANES
    while p <= tm:
        sizes.append(p)
        p *= 2
    return tuple(reversed(sizes))


def _stage_rows(tm):
    rows = tm * TOP_K + N_EXPERTS * 2 * (F32_SUBLANES - 1)
    return -(-rows // MXU_DEPTH) * MXU_DEPTH


def _final_kernel(start_ref, len_ref, seg_ref, tot_ref, h_ref, tg_ref, sp_ref, ys_ref, g_ref,
                  b_ref, o_ref, stage0, stage1, sem, *, alpha):
    i = pl.program_id(0)
    tm = h_ref.shape[0]
    stages = (stage0, stage1)
    n_stage = stage0.shape[0]

    def fetch(tile, into):
        for e in range(N_EXPERTS):
            ln = len_ref[tile * N_EXPERTS + e]
            first = start_ref[tile * N_EXPERTS + e]
            seg = seg_ref[tile * N_EXPERTS + e]
            for p in _run_sizes(tm):
                done = ln & (-2 * p)
                src = ys_ref.at[pl.ds(pl.multiple_of(first + done, F32_SUBLANES), p), :]
                dst = stages[into].at[pl.ds(pl.multiple_of(seg + done, F32_SUBLANES), p), :]
                pl.when((ln & p) != 0)(pltpu.make_async_copy(src, dst, sem.at[into]).start)

    @pl.when(i == 0)
    def _():
        stage0[...] = jnp.zeros(stage0.shape, F32)
        stage1[...] = jnp.zeros(stage1.shape, F32)
        fetch(0, 0)

    for slot in range(2):
        @pl.when((i & 1) == slot)
        def _(slot=slot):
            fetch(i + 1, 1 - slot)
            rows = pl.multiple_of(tot_ref[i], F32_SUBLANES)
            pltpu.make_async_copy(ys_ref.at[pl.ds(0, rows), :],
                                  stages[slot].at[pl.ds(0, rows), :], sem.at[slot]).wait()
            col = lax.broadcasted_iota(I32, (1, n_stage), 1)
            tg = tg_ref[...]
            sp = sp_ref[...]
            q = jnp.zeros((tm, n_stage), F32)
            for k in range(TOP_K):
                q = jnp.where(col == sp[:, k:k + 1], tg[:, k:k + 1], q)
            f = jnp.dot(q.astype(BF16), stages[slot][...].astype(BF16),
                        preferred_element_type=F32)
            o_ref[...] = _layer_norm(alpha * h_ref[...] + f, g_ref[...], b_ref[...])


def _final(run_start, run_len, run_seg, run_tot, h, top_g, spos, ys, ln_g, ln_b, alpha, tm,
           seq):
    t, d = h.shape
    spb = seq // tm
    grid_spec = pltpu.PrefetchScalarGridSpec(
        num_scalar_prefetch=4,
        grid=(t // tm,),
        in_specs=[pl.BlockSpec((tm, d), lambda i, *_: (i, 0)),
                  pl.BlockSpec((tm, LANES), lambda i, *_: (i, 0)),
                  pl.BlockSpec((tm, TOP_K), lambda i, *_: (i, 0)),
                  pl.BlockSpec(memory_space=pl.ANY),
                  pl.BlockSpec(ln_g.shape, lambda i, *_: (0, 0)),
                  pl.BlockSpec(ln_b.shape, lambda i, *_: (0, 0))],
        out_specs=pl.BlockSpec((None, tm, d), lambda i, *_: (i // spb, i % spb, 0)),
        scratch_shapes=[pltpu.VMEM((_stage_rows(tm), d), F32),
                        pltpu.VMEM((_stage_rows(tm), d), F32), pltpu.SemaphoreType.DMA((2,))],
    )
    return pl.pallas_call(
        functools.partial(_final_kernel, alpha=alpha),
        grid_spec=grid_spec,
        out_shape=jax.ShapeDtypeStruct((t // seq, seq, d), F32),
        compiler_params=_params("arbitrary"),
        name="combine_ln",
    )(run_start, run_len, run_seg, run_tot, h, top_g, spos, ys, ln_g, ln_b)


def _rope_tables(s):
    half = HEAD_DIM // 2
    inv = ROPE_THETA ** (-np.arange(half, dtype=np.float64) / half)
    ang = np.arange(s, dtype=np.float64)[:, None] * inv[None, :]
    cos, sin = np.cos(ang).astype(np.float32), np.sin(ang).astype(np.float32)
    reps = LANES // HEAD_DIM
    cos_t = np.tile(np.concatenate([cos, cos], axis=1), (1, reps))
    sin_t = np.tile(np.concatenate([-sin, sin], axis=1), (1, reps))
    return jnp.asarray(cos_t), jnp.asarray(sin_t)


def _ceil_to(v, m):
    return (v + m - 1) // m * m


def _moe_plan(top_e, rank, tile_base, counts, t, tm, dt):
    a = t * TOP_K
    per, nd = dt // tm, t // dt
    d_base = tile_base[::per]
    d_cnt = jnp.concatenate([d_base[1:], counts[None, :]], axis=0) - d_base
    seg_len = _ceil_to(d_cnt, F32_SUBLANES)
    padded = _ceil_to(jnp.sum(seg_len, axis=0), MOE_ROWS)
    pend = jnp.cumsum(padded)
    seg_first = (pend - padded)[None, :] + jnp.cumsum(seg_len, axis=0) - seg_len
    cseg = jnp.cumsum(seg_len, axis=1) - seg_len
    onehot = (top_e[:, :, None] == jnp.arange(N_EXPERTS, dtype=I32)).reshape(
        nd, dt, TOP_K, N_EXPERTS)

    def pick(table):
        return jnp.sum(jnp.where(onehot, table[:, None, None, :], 0), axis=3).reshape(t, TOP_K)

    local = rank - pick(d_base)
    dest = pick(seg_first) + local
    cpos = pick(cseg) + local
    run_first = jnp.repeat(seg_first - d_base, per, axis=0) + tile_base
    n_blk = -(-(a + nd * N_EXPERTS * (F32_SUBLANES - 1)) // MOE_ROWS) + N_EXPERTS
    blk_first = jnp.arange(n_blk, dtype=I32) * MOE_ROWS
    blk_e = jnp.minimum(jnp.sum(pend[None, :] <= blk_first[:, None], axis=1),
                        N_EXPERTS - 1).astype(I32)
    n_used = (pend[-1] // MOE_ROWS).astype(I32).reshape(1)
    flat = lambda v: v.astype(I32).reshape(-1)
    segments = (flat(cseg), flat(seg_first), flat(seg_len), jnp.sum(seg_len, axis=1).astype(I32))
    return (segments, dest.astype(I32), cpos.astype(I32), run_first, blk_e, n_used,
            n_blk * MOE_ROWS, pend.astype(I32), padded.astype(I32))


def _combine_plan(top_e, dest, run_first, tile_base, counts, tm):
    cnt = jnp.concatenate([tile_base[1:], counts[None, :]], axis=0) - tile_base
    lead = run_first & (F32_SUBLANES - 1)
    run_len = jnp.where(cnt > 0, _ceil_to(lead + cnt, F32_SUBLANES), 0)
    run_seg = jnp.cumsum(run_len, axis=1) - run_len
    shift = (run_seg + lead - run_first)[:, None, None, :]
    onehot = (top_e[:, :, None] == jnp.arange(N_EXPERTS, dtype=I32)).reshape(
        -1, tm, TOP_K, N_EXPERTS)
    spos = jnp.sum(jnp.where(onehot, shift, 0), axis=3).reshape(dest.shape) + dest
    flat = lambda v: jnp.pad(v.astype(I32), ((0, 1), (0, 0))).reshape(-1)
    run_tot = jnp.sum(run_len, axis=1).astype(I32)
    return flat(run_first - lead), flat(run_len), flat(run_seg), run_tot, spos.astype(I32)


def _layer(x, w_in, k_pe, k_w1, k_w2, v_pe, v_w1, v_w2, sinks, w_br_nsa, w_br_swa, w_out,
           ln1_g, ln1_b, w_router, b_router, w_e_in, b_e_in, w_e_out, b_e_out, ln2_g, ln2_b,
           alpha):
    b, s, d = x.shape
    t = b * s
    qb = Q_BLOCK
    nq_n, nkv = NSA_HEADS * HEAD_DIM, NSA_KV * HEAD_DIM
    nq_s, nkv_s = SWA_HEADS * HEAD_DIM, SWA_KV * HEAD_DIM
    widths = (nq_n, nkv, nkv, nkv, nkv, nkv, nkv, NSA_HEADS * 3, nq_s, nkv_s, nkv_s, 2 * d)
    offs = [0]
    for w in widths:
        offs.append(offs[-1] + w)
    col = lambda j: w_in[:, offs[j]:offs[j + 1]]
    (c_qn, c_kc, c_vc, c_ks, c_vs, c_kw, c_vw, c_gn, c_qs, c_k_s, c_v_s, c_gm) = map(col, range(12))
    w_rope = jnp.concatenate([c_qn, c_qs, c_ks, c_kw, c_k_s], axis=1).astype(BF16)
    w_plain = jnp.concatenate([c_kc, c_vc, c_vs, c_vw, c_v_s], axis=1).astype(BF16)
    gn_pad = LANES - NSA_HEADS * 3
    w_gate = jnp.concatenate([c_gm, c_gn, jnp.zeros((d, gn_pad), F32)], axis=1).astype(BF16)
    cos_t, sin_t = _rope_tables(s)

    qn_rot, qs_rot, kk_rot, qn_raw, plain, gates = _project(
        x, w_rope, w_plain, w_gate, cos_t, sin_t, min(256, s))

    nc = (s - CMP_BLOCK) // CMP_STRIDE + 1
    ncp = s // CMP_STRIDE
    half = CMP_STRIDE * HEAD_DIM

    def halves(cols):
        v = cols.reshape(b, s, NSA_KV, HEAD_DIM).transpose(0, 2, 1, 3)
        return v.reshape(b * NSA_KV, ncp, half)

    t2 = jnp.stack([halves(plain[:, :, 0:nkv]), halves(plain[:, :, nkv:2 * nkv])])
    t_lo = t2
    t_hi = jnp.concatenate([t2[:, :, 1:], jnp.zeros_like(t2[:, :, :1])], axis=2)
    pe2 = jnp.stack([k_pe.reshape(2, half), v_pe.reshape(2, half)])
    w1 = jnp.stack([k_w1, v_w1]).astype(BF16)
    w2 = jnp.stack([k_w2, v_w2]).astype(BF16)
    kvc = _compress(t_lo, t_hi, pe2, w1, w2).reshape(2, b, NSA_KV, ncp, HEAD_DIM)

    nsel = s // SEL_BLOCK
    nselp = -(-nsel // LANES) * LANES
    cstart = np.arange(ncp) * CMP_STRIDE
    sstart = np.arange(nselp) * SEL_BLOCK
    overlap = ((cstart[:, None] < sstart[None, :] + SEL_BLOCK)
               & (cstart[:, None] + CMP_BLOCK > sstart[None, :])
               & (np.arange(ncp)[:, None] < nc) & (np.arange(nselp)[None, :] < nsel))
    o_cmp, notsel = _cmp_attention(qn_raw, kvc[0], kvc[1],
                                   jnp.asarray(overlap.T.astype(BF16)),
                                   min(CMP_STEP_QUERIES, s))

    def group_major(cols):
        return cols.reshape(b, s, NSA_KV, HEAD_DIM).transpose(0, 2, 1, 3)

    k_sel = group_major(kk_rot[:, :, 0:nkv])
    onehot = (np.arange(s)[:, None] // SEL_BLOCK == np.arange(nselp)[None, :])
    k_tail = np.concatenate([np.zeros((s, LANES - HEAD_DIM), np.float32),
                             np.where(onehot, SEL_PENALTY, 0.0).astype(np.float32)], axis=1)
    k_aug = jnp.concatenate(
        [k_sel, jnp.broadcast_to(jnp.asarray(k_tail.astype(BF16)), (b, NSA_KV) + k_tail.shape)],
        axis=3)
    vt_all = plain[:, :, 2 * nkv:].transpose(0, 2, 1)
    o_sel = _sel_attention(qn_rot, notsel, k_aug, vt_all,
                           min(SEL_STEP_QUERIES, s), min(SEL_KEY_TILE, s))

    o_win = _band_attention(qn_rot, kk_rot, 1, vt_all, 1, nkv, NSA_WINDOW, None, qb)
    o_swa = _band_attention(qs_rot, kk_rot, 2, vt_all, 2, nkv_s, SWA_WINDOW, sinks, qb)

    gi = np.arange(LANES)
    ci = np.arange(3 * nq_n)
    expand = jnp.asarray(((gi[:, None] // 3 == (ci[None, :] % nq_n) // HEAD_DIM)
                          & (gi[:, None] % 3 == ci[None, :] // nq_n)
                          & (gi[:, None] < NSA_HEADS * 3)).astype(BF16))
    wr_pad = jnp.pad(w_router, ((0, 0), (0, LANES - N_EXPERTS)))
    wr_hi = wr_pad.astype(BF16)
    wr_lo = (wr_pad - wr_hi.astype(F32)).astype(BF16)
    b_r = jnp.concatenate([b_router, jnp.full((LANES - N_EXPERTS,), -jnp.inf, F32)]).reshape(1, LANES)
    tm = min(256, t)
    h, top_e, top_g, rank, counts, tile_base = _merge(
        o_cmp, o_sel, o_win, o_swa, gates, x, expand, w_br_nsa.astype(BF16),
        w_br_swa.astype(BF16), w_out.astype(BF16), ln1_g.reshape(1, d), ln1_b.reshape(1, d),
        wr_hi, wr_lo, b_r, alpha, tm)

    top_e, rank = top_e[:, :TOP_K], rank[:, :TOP_K]
    counts = counts[0, :N_EXPERTS].astype(I32)
    tile_base = tile_base[:, 0, :N_EXPERTS].astype(I32)
    dt = min(DISPATCH_TOKENS, t)
    segments, dest, cpos, run_first, blk_e, n_used, n_rows, pend, padded = _moe_plan(
        top_e, rank, tile_base, counts, t, tm, dt)
    xs = _dispatch(pend, padded, *segments, cpos.T, h, n_rows, dt)
    ys = _moe_blocks(blk_e, n_used, padded, xs, w_e_in, b_e_in.reshape(N_EXPERTS, 1, -1),
                     w_e_out, b_e_out.reshape(N_EXPERTS, 1, -1))
    run_start, run_len, run_seg, run_tot, spos = _combine_plan(
        top_e, dest, run_first, tile_base, counts, tm)
    return _final(run_start, run_len, run_seg, run_tot, h, top_g, spos, ys,
                  ln2_g.reshape(1, d), ln2_b.reshape(1, d), alpha, tm, s)


def kernel(x, w_in, nsa_k_pe, nsa_k_w1, nsa_k_w2, nsa_v_pe, nsa_v_w1, nsa_v_w2, swa_sinks, w_br_nsa, w_br_swa, w_out, ln1_g, ln1_b, w_router, b_router, w_expert_in, b_expert_in, w_expert_out, b_expert_out, ln2_g, ln2_b):
    depth = w_in.shape[0]
    alpha = (2.0 * depth) ** 0.25
    for l in range(depth):
        x = _layer(x, w_in[l], nsa_k_pe[l], nsa_k_w1[l], nsa_k_w2[l], nsa_v_pe[l], nsa_v_w1[l],
                   nsa_v_w2[l], swa_sinks[l], w_br_nsa[l], w_br_swa[l], w_out[l], ln1_g[l],
                   ln1_b[l], w_router[l], b_router[l], w_expert_in[l], b_expert_in[l],
                   w_expert_out[l], b_expert_out[l], ln2_g[l], ln2_b[l], alpha)
    return x
```

```python
import functools

import jax
import jax.numpy as jnp
import numpy as np
from jax import lax
from jax.experimental import pallas as pl
from jax.experimental.pallas import tpu as pltpu

BF16 = jnp.bfloat16
F32 = jnp.float32
I32 = jnp.int32

HEAD_DIM = 64
NSA_HEADS = 8
NSA_KV = 2
CMP_BLOCK = 32
CMP_STRIDE = 16
SEL_BLOCK = 64
SEL_TOPN = 16
NSA_WINDOW = 512
SWA_HEADS = 8
SWA_KV = 2
SWA_WINDOW = 128
Q_BLOCK = 128
ROPE_THETA = 10000.0
N_EXPERTS = 32
TOP_K = 4
SWIGLU_LIMIT = 7.0
SWIGLU_ALPHA = 1.702
LN_EPS = 1e-5

LANES = 128
BF16_SUBLANES = 16
F32_SUBLANES = 8
MXU_DEPTH = 256
MASKED = -1e30
M_INIT = -1e29
SEL_PENALTY = -(2.0 ** 100)
VMEM_LIMIT = 52 * 1024 * 1024
MOE_ROWS = 512
DISPATCH_TOKENS = 512
MERGE_STEP_TOKENS = 512
BAND_STEP_QUERIES = 1024
CMP_STEP_QUERIES = 512
SEL_STEP_QUERIES = 128
SEL_KEY_TILE = 512

R_NSA = NSA_HEADS // NSA_KV
R_SWA = SWA_HEADS // SWA_KV
NT_DIMS = (((1,), (1,)), ((), ()))


def _params(*sem):
    return pltpu.CompilerParams(dimension_semantics=sem, vmem_limit_bytes=VMEM_LIMIT)


def _full(shape):
    n = len(shape)
    return pl.BlockSpec(shape, lambda *_: (0,) * n)


def _proj_kernel(x_ref, wr_ref, wp_ref, wg_ref, cos_ref, sin_ref,
                 qn_rot_ref, qs_rot_ref, kk_rot_ref, qn_raw_ref, plain_ref, gates_ref):
    xb = x_ref[...].astype(BF16)
    acc = jnp.dot(xb, wr_ref[...], preferred_element_type=F32)
    cos = cos_ref[...]
    sin = sin_ref[...]
    lane = lax.broadcasted_iota(I32, cos.shape, 1)
    first_half = (lane & (HEAD_DIM - 1)) < HEAD_DIM // 2

    def rope(t):
        partner = jnp.where(first_half, pltpu.roll(t, LANES - HEAD_DIM // 2, 1),
                            pltpu.roll(t, HEAD_DIM // 2, 1))
        return (t * cos + partner * sin).astype(BF16)

    nq = qn_rot_ref.shape[1] // LANES
    ns = qs_rot_ref.shape[1] // LANES
    nk = kk_rot_ref.shape[1] // LANES
    for c in range(nq):
        qn_rot_ref[:, c * LANES:(c + 1) * LANES] = rope(acc[:, c * LANES:(c + 1) * LANES])
    for c in range(ns):
        o = (nq + c) * LANES
        qs_rot_ref[:, c * LANES:(c + 1) * LANES] = rope(acc[:, o:o + LANES])
    for c in range(nk):
        o = (nq + ns + c) * LANES
        kk_rot_ref[:, c * LANES:(c + 1) * LANES] = rope(acc[:, o:o + LANES])
    qn_raw_ref[...] = acc[:, :nq * LANES].astype(BF16)
    plain_ref[...] = jnp.dot(xb, wp_ref[...], preferred_element_type=F32).astype(BF16)
    gates_ref[...] = jnp.dot(xb, wg_ref[...], preferred_element_type=F32)


def _project(x, w_rope, w_plain, w_gate, cos_t, sin_t, tm):
    b, seq, d = x.shape
    t = b * seq
    nr, npl, ng = w_rope.shape[1], w_plain.shape[1], w_gate.shape[1]
    nqn, nqs = NSA_HEADS * HEAD_DIM, SWA_HEADS * HEAD_DIM
    nkk = nr - nqn - nqs
    spb = seq // tm
    row = lambda i: (i, 0)
    seq_blk = lambda n: pl.BlockSpec((None, tm, n), lambda i: (i // spb, i % spb, 0))
    seq_out = lambda n: jax.ShapeDtypeStruct((b, seq, n), BF16)
    return pl.pallas_call(
        _proj_kernel,
        grid=(t // tm,),
        in_specs=[seq_blk(d), _full(w_rope.shape), _full(w_plain.shape), _full(w_gate.shape),
                  pl.BlockSpec((tm, LANES), lambda i: (i % spb, 0)),
                  pl.BlockSpec((tm, LANES), lambda i: (i % spb, 0))],
        out_specs=[seq_blk(nqn), seq_blk(nqs), seq_blk(nkk), seq_blk(nqn), seq_blk(npl),
                   pl.BlockSpec((tm, ng), row)],
        out_shape=[seq_out(nqn), seq_out(nqs), seq_out(nkk), seq_out(nqn), seq_out(npl),
                   jax.ShapeDtypeStruct((t, ng), F32)],
        compiler_params=_params("parallel"),
        name="proj",
    )(x, w_rope, w_plain, w_gate, cos_t, sin_t)


def _compress_kernel(a_ref, b_ref, pe_ref, w1_ref, w2_ref, out_ref):
    half = a_ref.shape[1]
    a = (a_ref[...].astype(F32) + pe_ref[0:1, :]).astype(BF16)
    b = (b_ref[...].astype(F32) + pe_ref[1:2, :]).astype(BF16)
    hid = jnp.dot(a, w1_ref[0:half, :], preferred_element_type=F32)
    hid = hid + jnp.dot(b, w1_ref[half:2 * half, :], preferred_element_type=F32)
    act = jax.nn.gelu(hid).astype(BF16)
    out_ref[...] = jnp.dot(act, w2_ref[...], preferred_element_type=F32).astype(BF16)


def _compress(t_lo, t_hi, pe2, w1, w2):
    two, bg, ncp, half = t_lo.shape
    hid = w1.shape[2]
    blk = lambda shape: pl.BlockSpec((None, None) + shape, lambda j, i: (j, i, 0, 0))
    wsp = lambda shape: pl.BlockSpec((None,) + shape, lambda j, i: (j, 0, 0))
    return pl.pallas_call(
        _compress_kernel,
        grid=(two, bg),
        in_specs=[blk((ncp, half)), blk((ncp, half)), wsp((2, half)), wsp((2 * half, hid)),
                  wsp((hid, HEAD_DIM))],
        out_specs=blk((ncp, HEAD_DIM)),
        out_shape=jax.ShapeDtypeStruct((two, bg, ncp, HEAD_DIM), BF16),
        compiler_params=_params("parallel", "parallel"),
        name="compress",
    )(t_lo, t_hi, pe2, w1, w2)


def _stack_heads(q, g, r):
    return jnp.concatenate(
        [q[:, (g * r + j) * HEAD_DIM:(g * r + j + 1) * HEAD_DIM] for j in range(r)], axis=0)


def _unstack_heads_t(parts, r, qb):
    blocks = []
    for o in parts:
        for j in range(0, r, 2):
            pair = jnp.concatenate([o[:, j * qb:(j + 1) * qb], o[:, (j + 1) * qb:(j + 2) * qb]],
                                   axis=0)
            blocks.append(pair.T)
    return jnp.concatenate(blocks, axis=1)


def _topk_mask_cols(vals, k):
    n = vals.shape[0]
    row = lax.broadcasted_iota(I32, vals.shape, 0).astype(F32)
    taken = jnp.zeros(vals.shape, F32)
    work = vals
    for _ in range(k):
        mx = jnp.max(work, axis=0, keepdims=True)
        first = jnp.min(jnp.where(work == mx, row, float(n)), axis=0, keepdims=True)
        pick = row == first
        taken = jnp.where(pick, 1.0, taken)
        work = jnp.where(pick, -jnp.inf, work)
    return taken > 0.5


def _cmp_kernel(q_ref, kc_ref, vct_ref, ovt_ref, o_ref, notsel_ref):
    qb = q_ref.shape[0]
    ncp = kc_ref.shape[1]
    nselp = ovt_ref.shape[0]
    i = pl.program_id(1)
    rows = R_NSA * qb
    pos = i * qb + (lax.broadcasted_iota(I32, (1, rows), 1) & (qb - 1))
    cend = lax.broadcasted_iota(I32, (ncp, 1), 0) * CMP_STRIDE + (CMP_BLOCK - 1)
    bias = jnp.where(cend <= pos, 0.0, MASKED)
    live = (pos >= CMP_BLOCK - 1).astype(F32)
    q = q_ref[...]
    outs, imps = [], []
    for g in range(NSA_KV):
        qg = _stack_heads(q, g, R_NSA) * (HEAD_DIM ** -0.5)
        st = lax.dot_general(kc_ref[g], qg, NT_DIMS, preferred_element_type=F32)
        st = st + bias
        e = jnp.exp(st - jnp.max(st, axis=0, keepdims=True))
        pt = e * (live / jnp.sum(e, axis=0, keepdims=True))
        outs.append(jnp.dot(vct_ref[g], pt.astype(BF16), preferred_element_type=F32))
        psum = pt[:, 0:qb]
        for j in range(1, R_NSA):
            psum = psum + pt[:, j * qb:(j + 1) * qb]
        p_hi = psum.astype(BF16)
        p_lo = (psum - p_hi.astype(F32)).astype(BF16)
        imps.append(jnp.dot(ovt_ref[...], p_hi, preferred_element_type=F32)
                    + jnp.dot(ovt_ref[...], p_lo, preferred_element_type=F32))
    imp = jnp.concatenate(imps, axis=1)
    lane = lax.broadcasted_iota(I32, (1, NSA_KV * qb), 1)
    cur = (i * qb + (lane & (qb - 1))) >> 6
    jb = lax.broadcasted_iota(I32, (nselp, 1), 0)
    forced = (jb == 0) | (jb == cur) | (jb == cur - 1)
    imp = jnp.where(jb > cur, -1.0, jnp.where(forced, 1e6, imp))
    notsel = jnp.where(_topk_mask_cols(imp, SEL_TOPN), 0.0, 1.0)
    for g in range(NSA_KV):
        notsel_ref[g] = notsel[:, g * qb:(g + 1) * qb].T.astype(BF16)
    o_ref[...] = _unstack_heads_t(outs, R_NSA, qb)


def _cmp_attention(q_raw, kc, vc, overlap_t, qb):
    vc = vc.transpose(0, 1, 3, 2)
    b, s, hq = q_raw.shape
    _, g, ncp, dh = kc.shape
    nselp = overlap_t.shape[0]
    return pl.pallas_call(
        _cmp_kernel,
        grid=(b, s // qb),
        in_specs=[pl.BlockSpec((None, qb, hq), lambda bi, i: (bi, i, 0)),
                  pl.BlockSpec((None, g, ncp, dh), lambda bi, i: (bi, 0, 0, 0)),
                  pl.BlockSpec((None, g, dh, ncp), lambda bi, i: (bi, 0, 0, 0)),
                  _full(overlap_t.shape)],
        out_specs=[pl.BlockSpec((None, qb, hq), lambda bi, i: (bi, i, 0)),
                   pl.BlockSpec((None, g, qb, nselp), lambda bi, i: (bi, 0, i, 0))],
        out_shape=[jax.ShapeDtypeStruct((b, s, hq), F32),
                   jax.ShapeDtypeStruct((b, g, s, nselp), BF16)],
        compiler_params=_params("parallel", "parallel"),
        name="cmp_attn",
    )(q_raw, kc, vc, overlap_t)


def _sel_kernel(q_ref, notsel_ref, k_ref, vt_ref, o_ref, *score_bufs, tk):
    s_even, s_odd = score_bufs[:NSA_KV], score_bufs[NSA_KV:]
    qb = q_ref.shape[0]
    i = pl.program_id(1)
    rows = R_NSA * qb
    qpos = i * qb + (lax.broadcasted_iota(I32, (1, rows), 1) & (qb - 1))
    n_clear = (i * qb) // tk
    q = q_ref[...]
    q_augs = []
    for g in range(NSA_KV):
        qg = _stack_heads(q, g, R_NSA) * (HEAD_DIM ** -0.5)
        q_augs.append(jnp.concatenate(
            [qg, jnp.zeros((rows, LANES - HEAD_DIM), BF16),
             jnp.concatenate([notsel_ref[g]] * R_NSA, axis=0)], axis=1))

    def scores(kt, g):
        start = pl.multiple_of(kt * tk, tk)
        return lax.dot_general(k_ref[g, pl.ds(start, tk), :], q_augs[g], NT_DIMS,
                               preferred_element_type=F32)

    def consume(kt, g, s_ref, m, acc, diag_keys=None):
        nk = tk if diag_keys is None else diag_keys
        start = pl.multiple_of(kt * tk, tk)
        vt_t = jnp.concatenate([vt_ref[g * HEAD_DIM:(g + 1) * HEAD_DIM, pl.ds(start, nk)],
                                jnp.ones((BF16_SUBLANES, nk), vt_ref.dtype)], axis=0)
        st = s_ref[0:nk, :]
        if diag_keys is not None:
            kpos = start + lax.broadcasted_iota(I32, (nk, 1), 0)
            st = jnp.where(kpos <= qpos, st, MASKED)
        m_new = jnp.maximum(m, jnp.max(st, axis=0, keepdims=True))
        pt = jnp.exp(st - m_new).astype(BF16)
        acc = jnp.exp(m - m_new) * acc + jnp.dot(vt_t, pt, preferred_element_type=F32)
        return m_new, acc

    def advance(kt, carry, cur, nxt):
        new = []
        for g in range(NSA_KV):
            nxt[g][...] = scores(kt + 1, g)
            new.append(consume(kt, g, cur[g], *carry[g]))
        return tuple(new)

    def pair(j, carry):
        carry = advance(2 * j, carry, s_even, s_odd)
        return advance(2 * j + 1, carry, s_odd, s_even)

    def finish(carry, cur):
        where = (i * qb - n_clear * tk) // qb
        for v in range(tk // qb):
            @pl.when(where == v)
            def _(v=v):
                outs = []
                for g in range(NSA_KV):
                    _, acc = consume(n_clear, g, cur[g], *carry[g], diag_keys=(v + 1) * qb)
                    outs.append(acc[:HEAD_DIM] / acc[HEAD_DIM:HEAD_DIM + 1])
                o_ref[...] = _unstack_heads_t(outs, R_NSA, qb)

    for g in range(NSA_KV):
        s_even[g][...] = scores(0, g)
    init = tuple((jnp.full((1, rows), M_INIT, F32),
                  jnp.zeros((HEAD_DIM + BF16_SUBLANES, rows), F32))
                 for _ in range(NSA_KV))
    carry = lax.fori_loop(0, n_clear // 2, pair, init)

    @pl.when((n_clear & 1) == 0)
    def _():
        finish(carry, s_even)

    @pl.when((n_clear & 1) == 1)
    def _():
        finish(advance(n_clear - 1, carry, s_even, s_odd), s_odd)


def _sel_attention(q_rot, notsel, k_aug, vt_aug, qb, tk):
    b, s, hq = q_rot.shape
    _, g, _, kw = k_aug.shape
    nselp = notsel.shape[3]
    return pl.pallas_call(
        functools.partial(_sel_kernel, tk=tk),
        grid=(b, s // qb),
        in_specs=[pl.BlockSpec((None, qb, hq), lambda bi, i: (bi, i, 0)),
                  pl.BlockSpec((None, g, qb, nselp), lambda bi, i: (bi, 0, i, 0)),
                  pl.BlockSpec((None, g, s, kw), lambda bi, i: (bi, 0, 0, 0)),
                  pl.BlockSpec((None, g * HEAD_DIM, s), lambda bi, i: (bi, 0, 0))],
        out_specs=pl.BlockSpec((None, qb, hq), lambda bi, i: (bi, i, 0)),
        out_shape=jax.ShapeDtypeStruct((b, s, hq), F32),
        scratch_shapes=[pltpu.VMEM((tk, R_NSA * qb), F32)] * (2 * g),
        compiler_params=_params("parallel", "parallel"),
        name="sel_attn",
    )(q_rot, notsel, k_aug, vt_aug)


def _band_kernel(*refs, window, wlen, r, kv, qb, has_sinks):
    if has_sinks:
        sink_ref, q_ref, k_ref, vt_ref, o_ref = refs
    else:
        q_ref, k_ref, vt_ref, o_ref = refs
    nsub = q_ref.shape[0] // qb
    rows = r * qb
    lane = lax.broadcasted_iota(I32, (1, rows), 1)

    def mask_bias(i):
        start = jnp.maximum((i + 1) * qb - wlen, 0)
        rel = i * qb + (lane & (qb - 1)) - start - lax.broadcasted_iota(I32, (wlen, 1), 0)
        return jnp.where((rel >= 0) & (rel < window), 0.0, MASKED)

    def body(shared_bias):
        first = pl.program_id(1) * nsub
        if shared_bias:
            bias = mask_bias(first)
        for sb in range(nsub):
            i = first + sb
            if not shared_bias:
                bias = mask_bias(i)
            start = pl.multiple_of(jnp.maximum((i + 1) * qb - wlen, 0), qb)
            q = q_ref[sb * qb:(sb + 1) * qb, :]
            kw = k_ref[pl.ds(start, wlen), :]
            vtw = vt_ref[:, pl.ds(start, wlen)]
            outs = []
            for g in range(kv):
                qg = _stack_heads(q, g, r) * (HEAD_DIM ** -0.5)
                st = lax.dot_general(kw[:, g * HEAD_DIM:(g + 1) * HEAD_DIM], qg, NT_DIMS,
                                     preferred_element_type=F32) + bias
                m = jnp.max(st, axis=0, keepdims=True)
                if has_sinks:
                    sk = jnp.full((1, rows), sink_ref[g * r], F32)
                    for j in range(1, r):
                        sk = jnp.where(lane >= j * qb, sink_ref[g * r + j], sk)
                    m = jnp.maximum(m, sk)
                e = jnp.exp(st - m).astype(BF16)
                v_ones = jnp.concatenate([vtw[g * HEAD_DIM:(g + 1) * HEAD_DIM, :],
                                          jnp.ones((BF16_SUBLANES, wlen), vtw.dtype)], axis=0)
                ot = jnp.dot(v_ones, e, preferred_element_type=F32)
                den = ot[HEAD_DIM:HEAD_DIM + 1]
                if has_sinks:
                    den = den + jnp.exp(sk - m)
                outs.append(ot[:HEAD_DIM] / den)
            o_ref[sb * qb:(sb + 1) * qb, :] = _unstack_heads_t(outs, r, qb)

    if nsub * qb >= wlen - qb:
        @pl.when(pl.program_id(1) == 0)
        def _():
            body(False)

        @pl.when(pl.program_id(1) > 0)
        def _():
            body(True)
    else:
        body(False)


def _band_attention(q_rot, k_all, k_blk, vt_all, v_blk, gk, window, sinks, qb):
    b, s, hq = q_rot.shape
    assert k_all.shape[2] % gk == 0 and vt_all.shape[1] % gk == 0
    kv = gk // HEAD_DIM
    r = hq // gk
    back = -(-window // qb)
    wlen = (back + 1) * qb
    assert wlen <= s
    has_sinks = sinks is not None
    tq = min(BAND_STEP_QUERIES, s)
    in_specs = [pl.BlockSpec((None, tq, hq), lambda bi, i: (bi, i, 0)),
                pl.BlockSpec((None, s, gk), lambda bi, i: (bi, 0, k_blk)),
                pl.BlockSpec((None, gk, s), lambda bi, i: (bi, v_blk, 0))]
    args = [q_rot, k_all, vt_all]
    if has_sinks:
        in_specs = [pl.BlockSpec(memory_space=pltpu.SMEM)] + in_specs
        args = [sinks.astype(F32)] + args
    return pl.pallas_call(
        functools.partial(_band_kernel, window=window, wlen=wlen, r=r, kv=kv, qb=qb,
                          has_sinks=has_sinks),
        grid=(b, s // tq),
        in_specs=in_specs,
        out_specs=pl.BlockSpec((None, tq, hq), lambda bi, i: (bi, i, 0)),
        out_shape=jax.ShapeDtypeStruct((b, s, hq), F32),
        compiler_params=_params("parallel", "parallel"),
        name="band_attn_sink" if has_sinks else "band_attn",
    )(*args)


def _layer_norm(v, g, b):
    mu = jnp.mean(v, axis=1, keepdims=True)
    c = v - mu
    var = jnp.mean(c * c, axis=1, keepdims=True)
    return c * lax.rsqrt(var + LN_EPS) * g + b


def _split_dot(a, w):
    hi = a.astype(BF16)
    lo = (a - hi.astype(F32)).astype(BF16)
    return (jnp.dot(hi, w, preferred_element_type=F32)
            + jnp.dot(lo, w, preferred_element_type=F32))


def _merge_kernel(ocmp_ref, osel_ref, owin_ref, oswa_ref, gates_ref, x_ref, exp_ref,
                  wbn_ref, wbs_ref, wo_ref, lng_ref, lnb_ref, wrh_ref, wrl_ref, br_ref,
                  h_ref, te_ref, tg_ref, tr_ref, counts_ref, base_ref, cnt_ref, *, alpha, sub):
    d = x_ref.shape[1]
    hq = ocmp_ref.shape[1]
    n_sub = x_ref.shape[0] // sub
    col = lax.broadcasted_iota(I32, (sub, LANES), 1).astype(F32)

    def route(rows):
        gates = gates_ref[rows, :]
        gn = jax.nn.sigmoid(gates[:, 2 * d:])
        gexp = _split_dot(gn, exp_ref[...])
        o_nsa = (gexp[:, 0:hq] * ocmp_ref[rows, :] + gexp[:, hq:2 * hq] * osel_ref[rows, :]
                 + gexp[:, 2 * hq:3 * hq] * owin_ref[rows, :])
        y_nsa = jnp.dot(o_nsa.astype(BF16), wbn_ref[...], preferred_element_type=F32)
        y_swa = jnp.dot(oswa_ref[rows, :].astype(BF16), wbs_ref[...],
                        preferred_element_type=F32)
        gm = jax.nn.sigmoid(gates[:, :2 * d])
        mixed = gm[:, :d] * y_nsa + gm[:, d:] * y_swa
        z = jnp.dot(mixed.astype(BF16), wo_ref[...], preferred_element_type=F32)
        h = _layer_norm(alpha * x_ref[rows, :] + z, lng_ref[...], lnb_ref[...])
        h_ref[rows, :] = h
        h_hi = h.astype(BF16)
        h_lo = (h - h_hi.astype(F32)).astype(BF16)
        logits = (jnp.dot(h_hi, wrh_ref[...], preferred_element_type=F32)
                  + jnp.dot(h_lo, wrh_ref[...], preferred_element_type=F32)
                  + jnp.dot(h_hi, wrl_ref[...], preferred_element_type=F32)) + br_ref[...]
        work = logits
        vals, ids = [], []
        for _ in range(TOP_K):
            mx = jnp.max(work, axis=1, keepdims=True)
            first = jnp.min(jnp.where(work == mx, col, float(LANES)), axis=1, keepdims=True)
            vals.append(mx)
            ids.append(first)
            work = jnp.where(col == first, -jnp.inf, work)
        es = [jnp.exp(v - vals[0]) for v in vals]
        den = es[0]
        for e in es[1:]:
            den = den + e
        hits = jnp.zeros(logits.shape, F32)
        te = jnp.zeros(logits.shape, F32)
        tg = jnp.zeros(logits.shape, F32)
        for k in range(TOP_K):
            hits = jnp.where(col == ids[k], 1.0, hits)
            te = jnp.where(col == float(k), ids[k], te)
            tg = jnp.where(col == float(k), es[k] / den, tg)
        te_ref[rows, :] = te.astype(I32)
        tg_ref[rows, :] = tg
        earlier = (lax.broadcasted_iota(I32, (sub, sub), 1)
                   < lax.broadcasted_iota(I32, (sub, sub), 0))
        prefix = jnp.dot(jnp.where(earlier, 1.0, 0.0).astype(BF16), hits.astype(BF16),
                         preferred_element_type=F32)
        return ids, prefix, jnp.sum(hits, axis=0, keepdims=True)

    routed = [route(slice(j * sub, (j + 1) * sub)) for j in range(n_sub)]

    @pl.when(pl.program_id(0) == 0)
    def _():
        cnt_ref[...] = jnp.zeros(cnt_ref.shape, F32)

    for j, (ids, prefix, total) in enumerate(routed):
        base = cnt_ref[...]
        base_ref[j] = base
        before = prefix + base
        tr = jnp.zeros((sub, LANES), F32)
        for k in range(TOP_K):
            rank = jnp.sum(jnp.where(col == ids[k], before, 0.0), axis=1, keepdims=True)
            tr = jnp.where(col == float(k), rank, tr)
        tr_ref[j * sub:(j + 1) * sub, :] = tr.astype(I32)
        cnt_ref[...] = base + total
    counts_ref[...] = cnt_ref[...]


def _merge(o_cmp, o_sel, o_win, o_swa, gates, x, expand, w_bn, w_bs, w_o, ln_g, ln_b,
           wr_hi, wr_lo, b_r, alpha, sub):
    b, s, d = x.shape
    t = b * s
    hq = o_cmp.shape[2]
    tm = min(MERGE_STEP_TOKENS, s)
    n_sub = tm // sub
    spb = s // tm
    row = lambda i: (i, 0)
    tok = lambda n: pl.BlockSpec((tm, n), row)
    seq = lambda n: pl.BlockSpec((None, tm, n), lambda i: (i // spb, i % spb, 0))
    return pl.pallas_call(
        functools.partial(_merge_kernel, alpha=alpha, sub=sub),
        grid=(t // tm,),
        in_specs=[seq(hq), seq(hq), seq(hq), seq(hq), tok(gates.shape[1]), seq(d),
                  _full(expand.shape), _full(w_bn.shape), _full(w_bs.shape), _full(w_o.shape),
                  _full(ln_g.shape), _full(ln_b.shape), _full(wr_hi.shape), _full(wr_lo.shape),
                  _full(b_r.shape)],
        out_specs=[tok(d), tok(LANES), tok(LANES), tok(LANES), _full((1, LANES)),
                   pl.BlockSpec((n_sub, 1, LANES), lambda i: (i, 0, 0))],
        out_shape=[jax.ShapeDtypeStruct((t, d), F32), jax.ShapeDtypeStruct((t, LANES), I32),
                   jax.ShapeDtypeStruct((t, LANES), F32), jax.ShapeDtypeStruct((t, LANES), I32),
                   jax.ShapeDtypeStruct((1, LANES), F32),
                   jax.ShapeDtypeStruct((t // sub, 1, LANES), F32)],
        scratch_shapes=[pltpu.VMEM((1, LANES), F32)],
        compiler_params=_params("arbitrary"),
        name="merge_ln_router",
    )(o_cmp, o_sel, o_win, o_swa, gates, x, expand, w_bn, w_bs, w_o, ln_g, ln_b,
      wr_hi, wr_lo, b_r)


def _dispatch_kernel(pend_ref, padded_ref, cseg_ref, first_ref, len_ref, tot_ref,
                     h_ref, cpos_ref, xs_ref, comp0, comp1, zbuf, sem, zsem):
    j = pl.program_id(0)
    nt = pl.num_programs(0)
    tm = h_ref.shape[0]
    n_rows = xs_ref.shape[0]
    comps = (comp0, comp1)
    n_comp = comp0.shape[0]

    @pl.when(j == 0)
    def _():
        zbuf[...] = jnp.zeros(zbuf.shape, F32)
        used_rows = pend_ref[N_EXPERTS - 1]

        def zero_copy(start):
            start = pl.multiple_of(start, MOE_ROWS)
            return pltpu.make_async_copy(zbuf, xs_ref.at[pl.ds(start, MOE_ROWS), :], zsem)

        blocks = [(padded_ref[e] > 0, pend_ref[e] - MOE_ROWS) for e in range(N_EXPERTS)]
        blocks += [(used_rows + b * MOE_ROWS < n_rows, used_rows + b * MOE_ROWS)
                   for b in range(n_rows // MOE_ROWS - (nt * tm * TOP_K) // MOE_ROWS)]
        for cond, start in blocks:
            @pl.when(cond)
            def _(start=start):
                zero_copy(start).start()
        for cond, start in blocks:
            @pl.when(cond)
            def _(start=start):
                zero_copy(start).wait()

    def wait_writes(tile, slot):
        rows = pl.multiple_of(tot_ref[tile], F32_SUBLANES)
        pltpu.make_async_copy(comps[slot].at[pl.ds(0, rows), :], xs_ref.at[pl.ds(0, rows), :],
                              sem.at[slot]).wait()

    for slot in range(2):
        @pl.when((j >= 2) & ((j & 1) == slot))
        def _(slot=slot):
            wait_writes(j - 2, slot)

        @pl.when((j & 1) == slot)
        def _(slot=slot):
            row = lax.broadcasted_iota(I32, (n_comp, 1), 0)
            cpos = cpos_ref[...]
            sel = jnp.zeros((n_comp, tm), F32)
            for k in range(TOP_K):
                sel = jnp.where(row == cpos[k:k + 1, :], 1.0, sel)
            comps[slot][...] = jnp.dot(sel.astype(BF16), h_ref[...].astype(BF16),
                                       preferred_element_type=F32)
            for e in range(N_EXPERTS):
                ln = len_ref[j * N_EXPERTS + e]
                first = first_ref[j * N_EXPERTS + e]
                seg = cseg_ref[j * N_EXPERTS + e]
                for p in _run_sizes(tm):
                    done = ln & (-2 * p)
                    src = comps[slot].at[pl.ds(pl.multiple_of(seg + done, F32_SUBLANES), p), :]
                    dst = xs_ref.at[pl.ds(pl.multiple_of(first + done, F32_SUBLANES), p), :]
                    pl.when((ln & p) != 0)(pltpu.make_async_copy(src, dst, sem.at[slot]).start)

        @pl.when((j == nt - 1) & ((j & 1) == slot))
        def _(slot=slot):
            @pl.when(j >= 1)
            def _():
                wait_writes(j - 1, 1 - slot)
            wait_writes(j, slot)


def _dispatch(pend, padded, cseg, seg_first, seg_len, seg_tot, cpos_t, h, n_rows, tm):
    t, d = h.shape
    n_comp = -(-(tm * TOP_K + N_EXPERTS * (F32_SUBLANES - 1)) // MXU_DEPTH) * MXU_DEPTH
    grid_spec = pltpu.PrefetchScalarGridSpec(
        num_scalar_prefetch=6,
        grid=(t // tm,),
        in_specs=[pl.BlockSpec((tm, d), lambda i, *_: (i, 0)),
                  pl.BlockSpec((TOP_K, tm), lambda i, *_: (0, i))],
        out_specs=pl.BlockSpec(memory_space=pl.ANY),
        scratch_shapes=[pltpu.VMEM((n_comp, d), F32), pltpu.VMEM((n_comp, d), F32),
                        pltpu.VMEM((MOE_ROWS, d), F32), pltpu.SemaphoreType.DMA((2,)),
                        pltpu.SemaphoreType.DMA(())],
    )
    return pl.pallas_call(
        _dispatch_kernel,
        grid_spec=grid_spec,
        out_shape=jax.ShapeDtypeStruct((n_rows, d), F32),
        compiler_params=_params("arbitrary"),
        name="moe_dispatch",
    )(pend, padded, cseg, seg_first, seg_len, seg_tot, h, cpos_t)


def _moe_kernel(blk_e_ref, n_used_ref, first_ref, slot_ref, next_ref, x_ref, wi_hbm, bi_ref,
                wo_hbm, bo_ref, y_ref, wi_buf, wo_buf, wi_bf, wo_bf, wi_sem, wo_sem):
    f = wo_hbm.shape[1]
    b = pl.program_id(0)
    used = b < n_used_ref[0]
    e = blk_e_ref[b]
    slot = slot_ref[b]

    def fetch(expert, into):
        return (pltpu.make_async_copy(wi_hbm.at[expert], wi_buf.at[into], wi_sem.at[into]),
                pltpu.make_async_copy(wo_hbm.at[expert], wo_buf.at[into], wo_sem.at[into]))

    @pl.when(b == 0)
    def _():
        for cp in fetch(e, 0):
            cp.start()

    for s in range(2):
        @pl.when(used & (first_ref[b] == 1) & (slot == s))
        def _(s=s):
            @pl.when(next_ref[b] != e)
            def _():
                for cp in fetch(next_ref[b], 1 - s):
                    cp.start()
            for cp in fetch(e, s):
                cp.wait()
            wi_bf[...] = wi_buf[s].astype(BF16)
            wo_bf[...] = wo_buf[s].astype(BF16)

    @pl.when(used)
    def _():
        hdn = jnp.dot(x_ref[...].astype(BF16), wi_bf[...],
                      preferred_element_type=F32) + bi_ref[...]
        hg = jnp.minimum(hdn[:, :f], SWIGLU_LIMIT)
        hu = jnp.clip(hdn[:, f:], -SWIGLU_LIMIT, SWIGLU_LIMIT)
        act = hg * jax.nn.sigmoid(SWIGLU_ALPHA * hg) * (hu + 1.0)
        y_ref[...] = jnp.dot(act.astype(BF16), wo_bf[...],
                             preferred_element_type=F32) + bo_ref[...]

    @pl.when(jnp.logical_not(used))
    def _():
        y_ref[...] = jnp.zeros(y_ref.shape, F32)


def _moe_blocks(blk_e, n_used, padded, xs, w_in, b_in, w_out, b_out):
    n_rows, d = xs.shape
    e, _, f2 = w_in.shape
    f = w_out.shape[1]
    n_blk = n_rows // MOE_ROWS
    first = jnp.concatenate([jnp.ones((1,), I32), (blk_e[1:] != blk_e[:-1]).astype(I32)])
    ids = jnp.arange(e, dtype=I32)
    has_rows = padded > 0
    is_e = blk_e[:, None] == ids[None, :]
    slot = jnp.sum((ids[None, :] < blk_e[:, None]) & has_rows[None, :], axis=1) & 1
    later = jnp.where((ids[None, :] > ids[:, None]) & has_rows[None, :], ids[None, :], e)
    next_used = jnp.min(later, axis=1)
    next_used = jnp.where(next_used == e, ids, next_used).astype(I32)
    next_blk = jnp.sum(jnp.where(is_e, next_used[None, :], 0), axis=1).astype(I32)
    grid_spec = pltpu.PrefetchScalarGridSpec(
        num_scalar_prefetch=5,
        grid=(n_blk,),
        in_specs=[pl.BlockSpec((MOE_ROWS, d), lambda b, be, nu, *_: (jnp.minimum(b, nu[0] - 1), 0)),
                  pl.BlockSpec(memory_space=pl.ANY),
                  pl.BlockSpec((None, 1, f2), lambda b, be, *_: (be[b], 0, 0)),
                  pl.BlockSpec(memory_space=pl.ANY),
                  pl.BlockSpec((None, 1, d), lambda b, be, *_: (be[b], 0, 0))],
        out_specs=pl.BlockSpec((MOE_ROWS, d), lambda b, *_: (b, 0)),
        scratch_shapes=[pltpu.VMEM((2, d, f2), F32), pltpu.VMEM((2, f, d), F32),
                        pltpu.VMEM((d, f2), BF16), pltpu.VMEM((f, d), BF16),
                        pltpu.SemaphoreType.DMA((2,)), pltpu.SemaphoreType.DMA((2,))],
    )
    return pl.pallas_call(
        _moe_kernel,
        grid_spec=grid_spec,
        out_shape=jax.ShapeDtypeStruct((n_rows, d), F32),
        compiler_params=_params("arbitrary"),
        name="moe_experts",
    )(blk_e, n_used, first, slot.astype(I32), next_blk, xs, w_in, b_in, w_out, b_out)


def _run_sizes(tm):
    sizes, p = [], F32_SUBLANES
    while p <= tm:
        sizes.append(p)
        p *= 2
    return tuple(reversed(sizes))


def _stage_rows(tm):
    rows = tm * TOP_K + N_EXPERTS * 2 * (F32_SUBLANES - 1)
    return -(-rows // MXU_DEPTH) * MXU_DEPTH


def _final_kernel(start_ref, len_ref, seg_ref, tot_ref, h_ref, tg_ref, sp_ref, ys_ref, g_ref,
                  b_ref, o_ref, stage0, stage1, sem, *, alpha):
    i = pl.program_id(0)
    tm = h_ref.shape[0]
    stages = (stage0, stage1)
    n_stage = stage0.shape[0]

    def fetch(tile, into):
        for e in range(N_EXPERTS):
            ln = len_ref[tile * N_EXPERTS + e]
            first = start_ref[tile * N_EXPERTS + e]
            seg = seg_ref[tile * N_EXPERTS + e]
            for p in _run_sizes(tm):
                done = ln & (-2 * p)
                src = ys_ref.at[pl.ds(pl.multiple_of(first + done, F32_SUBLANES), p), :]
                dst = stages[into].at[pl.ds(pl.multiple_of(seg + done, F32_SUBLANES), p), :]
                pl.when((ln & p) != 0)(pltpu.make_async_copy(src, dst, sem.at[into]).start)

    @pl.when(i == 0)
    def _():
        stage0[...] = jnp.zeros(stage0.shape, F32)
        stage1[...] = jnp.zeros(stage1.shape, F32)
        fetch(0, 0)

    for slot in range(2):
        @pl.when((i & 1) == slot)
        def _(slot=slot):
            fetch(i + 1, 1 - slot)
            rows = pl.multiple_of(tot_ref[i], F32_SUBLANES)
            pltpu.make_async_copy(ys_ref.at[pl.ds(0, rows), :],
                                  stages[slot].at[pl.ds(0, rows), :], sem.at[slot]).wait()
            col = lax.broadcasted_iota(I32, (1, n_stage), 1)
            tg = tg_ref[...]
            sp = sp_ref[...]
            q = jnp.zeros((tm, n_stage), F32)
            for k in range(TOP_K):
                q = jnp.where(col == sp[:, k:k + 1], tg[:, k:k + 1], q)
            f = jnp.dot(q.astype(BF16), stages[slot][...].astype(BF16),
                        preferred_element_type=F32)
            o_ref[...] = _layer_norm(alpha * h_ref[...] + f, g_ref[...], b_ref[...])


def _final(run_start, run_len, run_seg, run_tot, h, top_g, spos, ys, ln_g, ln_b, alpha, tm,
           seq):
    t, d = h.shape
    spb = seq // tm
    grid_spec = pltpu.PrefetchScalarGridSpec(
        num_scalar_prefetch=4,
        grid=(t // tm,),
        in_specs=[pl.BlockSpec((tm, d), lambda i, *_: (i, 0)),
                  pl.BlockSpec((tm, LANES), lambda i, *_: (i, 0)),
                  pl.BlockSpec((tm, TOP_K), lambda i, *_: (i, 0)),
                  pl.BlockSpec(memory_space=pl.ANY),
                  pl.BlockSpec(ln_g.shape, lambda i, *_: (0, 0)),
                  pl.BlockSpec(ln_b.shape, lambda i, *_: (0, 0))],
        out_specs=pl.BlockSpec((None, tm, d), lambda i, *_: (i // spb, i % spb, 0)),
        scratch_shapes=[pltpu.VMEM((_stage_rows(tm), d), F32),
                        pltpu.VMEM((_stage_rows(tm), d), F32), pltpu.SemaphoreType.DMA((2,))],
    )
    return pl.pallas_call(
        functools.partial(_final_kernel, alpha=alpha),
        grid_spec=grid_spec,
        out_shape=jax.ShapeDtypeStruct((t // seq, seq, d), F32),
        compiler_params=_params("arbitrary"),
        name="combine_ln",
    )(run_start, run_len, run_seg, run_tot, h, top_g, spos, ys, ln_g, ln_b)


def _rope_tables(s):
    half = HEAD_DIM // 2
    inv = ROPE_THETA ** (-np.arange(half, dtype=np.float64) / half)
    ang = np.arange(s, dtype=np.float64)[:, None] * inv[None, :]
    cos, sin = np.cos(ang).astype(np.float32), np.sin(ang).astype(np.float32)
    reps = LANES // HEAD_DIM
    cos_t = np.tile(np.concatenate([cos, cos], axis=1), (1, reps))
    sin_t = np.tile(np.concatenate([-sin, sin], axis=1), (1, reps))
    return jnp.asarray(cos_t), jnp.asarray(sin_t)


def _ceil_to(v, m):
    return (v + m - 1) // m * m


def _moe_plan(top_e, rank, tile_base, counts, t, tm, dt):
    a = t * TOP_K
    per, nd = dt // tm, t // dt
    d_base = tile_base[::per]
    d_cnt = jnp.concatenate([d_base[1:], counts[None, :]], axis=0) - d_base
    seg_len = _ceil_to(d_cnt, F32_SUBLANES)
    padded = _ceil_to(jnp.sum(seg_len, axis=0), MOE_ROWS)
    pend = jnp.cumsum(padded)
    seg_first = (pend - padded)[None, :] + jnp.cumsum(seg_len, axis=0) - seg_len
    cseg = jnp.cumsum(seg_len, axis=1) - seg_len
    onehot = (top_e[:, :, None] == jnp.arange(N_EXPERTS, dtype=I32)).reshape(
        nd, dt, TOP_K, N_EXPERTS)

    def pick(table):
        return jnp.sum(jnp.where(onehot, table[:, None, None, :], 0), axis=3).reshape(t, TOP_K)

    local = rank - pick(d_base)
    dest = pick(seg_first) + local
    cpos = pick(cseg) + local
    run_first = jnp.repeat(seg_first - d_base, per, axis=0) + tile_base
    n_blk = -(-(a + nd * N_EXPERTS * (F32_SUBLANES - 1)) // MOE_ROWS) + N_EXPERTS
    blk_first = jnp.arange(n_blk, dtype=I32) * MOE_ROWS
    blk_e = jnp.minimum(jnp.sum(pend[None, :] <= blk_first[:, None], axis=1),
                        N_EXPERTS - 1).astype(I32)
    n_used = (pend[-1] // MOE_ROWS).astype(I32).reshape(1)
    flat = lambda v: v.astype(I32).reshape(-1)
    segments = (flat(cseg), flat(seg_first), flat(seg_len), jnp.sum(seg_len, axis=1).astype(I32))
    return (segments, dest.astype(I32), cpos.astype(I32), run_first, blk_e, n_used,
            n_blk * MOE_ROWS, pend.astype(I32), padded.astype(I32))


def _combine_plan(top_e, dest, run_first, tile_base, counts, tm):
    cnt = jnp.concatenate([tile_base[1:], counts[None, :]], axis=0) - tile_base
    lead = run_first & (F32_SUBLANES - 1)
    run_len = jnp.where(cnt > 0, _ceil_to(lead + cnt, F32_SUBLANES), 0)
    run_seg = jnp.cumsum(run_len, axis=1) - run_len
    shift = (run_seg + lead - run_first)[:, None, None, :]
    onehot = (top_e[:, :, None] == jnp.arange(N_EXPERTS, dtype=I32)).reshape(
        -1, tm, TOP_K, N_EXPERTS)
    spos = jnp.sum(jnp.where(onehot, shift, 0), axis=3).reshape(dest.shape) + dest
    flat = lambda v: jnp.pad(v.astype(I32), ((0, 1), (0, 0))).reshape(-1)
    run_tot = jnp.sum(run_len, axis=1).astype(I32)
    return flat(run_first - lead), flat(run_len), flat(run_seg), run_tot, spos.astype(I32)


def _layer(x, w_in, k_pe, k_w1, k_w2, v_pe, v_w1, v_w2, sinks, w_br_nsa, w_br_swa, w_out,
           ln1_g, ln1_b, w_router, b_router, w_e_in, b_e_in, w_e_out, b_e_out, ln2_g, ln2_b,
           alpha):
    b, s, d = x.shape
    t = b * s
    qb = Q_BLOCK
    nq_n, nkv = NSA_HEADS * HEAD_DIM, NSA_KV * HEAD_DIM
    nq_s, nkv_s = SWA_HEADS * HEAD_DIM, SWA_KV * HEAD_DIM
    widths = (nq_n, nkv, nkv, nkv, nkv, nkv, nkv, NSA_HEADS * 3, nq_s, nkv_s, nkv_s, 2 * d)
    offs = [0]
    for w in widths:
        offs.append(offs[-1] + w)
    col = lambda j: w_in[:, offs[j]:offs[j + 1]]
    (c_qn, c_kc, c_vc, c_ks, c_vs, c_kw, c_vw, c_gn, c_qs, c_k_s, c_v_s, c_gm) = map(col, range(12))
    w_rope = jnp.concatenate([c_qn, c_qs, c_ks, c_kw, c_k_s], axis=1).astype(BF16)
    w_plain = jnp.concatenate([c_kc, c_vc, c_vs, c_vw, c_v_s], axis=1).astype(BF16)
    gn_pad = LANES - NSA_HEADS * 3
    w_gate = jnp.concatenate([c_gm, c_gn, jnp.zeros((d, gn_pad), F32)], axis=1).astype(BF16)
    cos_t, sin_t = _rope_tables(s)

    qn_rot, qs_rot, kk_rot, qn_raw, plain, gates = _project(
        x, w_rope, w_plain, w_gate, cos_t, sin_t, min(256, s))

    nc = (s - CMP_BLOCK) // CMP_STRIDE + 1
    ncp = s // CMP_STRIDE
    half = CMP_STRIDE * HEAD_DIM

    def halves(cols):
        v = cols.reshape(b, s, NSA_KV, HEAD_DIM).transpose(0, 2, 1, 3)
        return v.reshape(b * NSA_KV, ncp, half)

    t2 = jnp.stack([halves(plain[:, :, 0:nkv]), halves(plain[:, :, nkv:2 * nkv])])
    t_lo = t2
    t_hi = jnp.concatenate([t2[:, :, 1:], jnp.zeros_like(t2[:, :, :1])], axis=2)
    pe2 = jnp.stack([k_pe.reshape(2, half), v_pe.reshape(2, half)])
    w1 = jnp.stack([k_w1, v_w1]).astype(BF16)
    w2 = jnp.stack([k_w2, v_w2]).astype(BF16)
    kvc = _compress(t_lo, t_hi, pe2, w1, w2).reshape(2, b, NSA_KV, ncp, HEAD_DIM)

    nsel = s // SEL_BLOCK
    nselp = -(-nsel // LANES) * LANES
    cstart = np.arange(ncp) * CMP_STRIDE
    sstart = np.arange(nselp) * SEL_BLOCK
    overlap = ((cstart[:, None] < sstart[None, :] + SEL_BLOCK)
               & (cstart[:, None] + CMP_BLOCK > sstart[None, :])
               & (np.arange(ncp)[:, None] < nc) & (np.arange(nselp)[None, :] < nsel))
    o_cmp, notsel = _cmp_attention(qn_raw, kvc[0], kvc[1],
                                   jnp.asarray(overlap.T.astype(BF16)),
                                   min(CMP_STEP_QUERIES, s))

    def group_major(cols):
        return cols.reshape(b, s, NSA_KV, HEAD_DIM).transpose(0, 2, 1, 3)

    k_sel = group_major(kk_rot[:, :, 0:nkv])
    onehot = (np.arange(s)[:, None] // SEL_BLOCK == np.arange(nselp)[None, :])
    k_tail = np.concatenate([np.zeros((s, LANES - HEAD_DIM), np.float32),
                             np.where(onehot, SEL_PENALTY, 0.0).astype(np.float32)], axis=1)
    k_aug = jnp.concatenate(
        [k_sel, jnp.broadcast_to(jnp.asarray(k_tail.astype(BF16)), (b, NSA_KV) + k_tail.shape)],
        axis=3)
    vt_all = plain[:, :, 2 * nkv:].transpose(0, 2, 1)
    o_sel = _sel_attention(qn_rot, notsel, k_aug, vt_all,
                           min(SEL_STEP_QUERIES, s), min(SEL_KEY_TILE, s))

    o_win = _band_attention(qn_rot, kk_rot, 1, vt_all, 1, nkv, NSA_WINDOW, None, qb)
    o_swa = _band_attention(qs_rot, kk_rot, 2, vt_all, 2, nkv_s, SWA_WINDOW, sinks, qb)

    gi = np.arange(LANES)
    ci = np.arange(3 * nq_n)
    expand = jnp.asarray(((gi[:, None] // 3 == (ci[None, :] % nq_n) // HEAD_DIM)
                          & (gi[:, None] % 3 == ci[None, :] // nq_n)
                          & (gi[:, None] < NSA_HEADS * 3)).astype(BF16))
    wr_pad = jnp.pad(w_router, ((0, 0), (0, LANES - N_EXPERTS)))
    wr_hi = wr_pad.astype(BF16)
    wr_lo = (wr_pad - wr_hi.astype(F32)).astype(BF16)
    b_r = jnp.concatenate([b_router, jnp.full((LANES - N_EXPERTS,), -jnp.inf, F32)]).reshape(1, LANES)
    tm = min(256, t)
    h, top_e, top_g, rank, counts, tile_base = _merge(
        o_cmp, o_sel, o_win, o_swa, gates, x, expand, w_br_nsa.astype(BF16),
        w_br_swa.astype(BF16), w_out.astype(BF16), ln1_g.reshape(1, d), ln1_b.reshape(1, d),
        wr_hi, wr_lo, b_r, alpha, tm)

    top_e, rank = top_e[:, :TOP_K], rank[:, :TOP_K]
    counts = counts[0, :N_EXPERTS].astype(I32)
    tile_base = tile_base[:, 0, :N_EXPERTS].astype(I32)
    dt = min(DISPATCH_TOKENS, t)
    segments, dest, cpos, run_first, blk_e, n_used, n_rows, pend, padded = _moe_plan(
        top_e, rank, tile_base, counts, t, tm, dt)
    xs = _dispatch(pend, padded, *segments, cpos.T, h, n_rows, dt)
    ys = _moe_blocks(blk_e, n_used, padded, xs, w_e_in, b_e_in.reshape(N_EXPERTS, 1, -1),
                     w_e_out, b_e_out.reshape(N_EXPERTS, 1, -1))
    run_start, run_len, run_seg, run_tot, spos = _combine_plan(
        top_e, dest, run_first, tile_base, counts, tm)
    return _final(run_start, run_len, run_seg, run_tot, h, top_g, spos, ys,
                  ln2_g.reshape(1, d), ln2_b.reshape(1, d), alpha, tm, s)


def kernel(x, w_in, nsa_k_pe, nsa_k_w1, nsa_k_w2, nsa_v_pe, nsa_v_w1, nsa_v_w2, swa_sinks, w_br_nsa, w_br_swa, w_out, ln1_g, ln1_b, w_router, b_router, w_expert_in, b_expert_in, w_expert_out, b_expert_out, ln2_g, ln2_b):
    depth = w_in.shape[0]
    alpha = (2.0 * depth) ** 0.25
    for l in range(depth):
        x = _layer(x, w_in[l], nsa_k_pe[l], nsa_k_w1[l], nsa_k_w2[l], nsa_v_pe[l], nsa_v_w1[l],
                   nsa_v_w2[l], swa_sinks[l], w_br_nsa[l], w_br_swa[l], w_out[l], ln1_g[l],
                   ln1_b[l], w_router[l], b_router[l], w_expert_in[l], b_expert_in[l],
                   w_expert_out[l], b_expert_out[l], ln2_g[l], ln2_b[l], alpha)
    return x
```
